```python
import jax, jax.numpy as jnp
from jax import lax
import numpy as np

D_MODEL = 1024
BATCH = 8
SEQ = 2048
DEPTH = 1

CHUNK = 64
HEAD_DIM = 64
H_SB = 8
H_CH = 8
W_SB = H_SB * HEAD_DIM
W_CH = H_CH * HEAD_DIM
MIX_WIDTH = W_SB + W_CH
LOOKBACK = 8
BAND = (LOOKBACK + 1) * CHUNK
REL_CLIP = 128
Q_BLOCK = 128
D_FF = 2816
PLE_DIM = 256
EPS = 1e-6
NEG_INF = -1e30

kernel_name = "hybrid_stickbreak_chunkattn_macaron_block"


def rms_norm(x, g):
    xf = x.astype(jnp.float32)
    y = xf * lax.rsqrt(jnp.mean(xf * xf, axis=-1, keepdims=True) + EPS)
    return (y * g.astype(jnp.float32)).astype(x.dtype)


def swiglu(x, w_gate, w_up, w_down):
    return (jax.nn.silu(x @ w_gate) * (x @ w_up)) @ w_down


def split_heads(t, n_heads):
    b, s, _ = t.shape
    return t.reshape(b, s, n_heads, HEAD_DIM).transpose(0, 2, 1, 3)


def merge_heads(t):
    b, h, s, d = t.shape
    return t.transpose(0, 2, 1, 3).reshape(b, s, h * d)


def stick_breaking_attention(q, k, v):
    b, h, s, d = q.shape
    nq = s // Q_BLOCK
    scale = d ** -0.5
    q_blocks = q.reshape(b, h, nq, Q_BLOCK, d).transpose(2, 0, 1, 3, 4)
    starts = jnp.arange(nq, dtype=jnp.int32) * Q_BLOCK
    key_pos = jnp.arange(s, dtype=jnp.int32)

    def one_block(args):
        q_blk, start = args
        z = jnp.einsum('bhqd,bhkd->bhqk', q_blk, k,
                       preferred_element_type=jnp.float32) * scale
        q_pos = start + jnp.arange(Q_BLOCK, dtype=jnp.int32)
        before = key_pos[None, :] < q_pos[:, None]
        log_fail = jnp.where(before, jax.nn.log_sigmoid(-z), 0.0)
        later = lax.cumsum(log_fail, axis=3, reverse=True) - log_fail
        log_a = jax.nn.log_sigmoid(z) + later
        a = jnp.where(before, jnp.exp(jnp.where(before, log_a, 0.0)), 0.0)
        return jnp.einsum('bhqk,bhkd->bhqd', a.astype(v.dtype), v)

    out = lax.map(one_block, (q_blocks, starts))
    return out.transpose(1, 2, 0, 3, 4).reshape(b, h, s, d)


def rel_bias_index():
    i = np.arange(CHUNK)[:, None]
    j = np.arange(BAND)[None, :]
    dist = i + LOOKBACK * CHUNK - j
    return jnp.asarray(np.clip(dist, -REL_CLIP, REL_CLIP) + REL_CLIP, dtype=jnp.int32)


def chunk_band_attention(q, k, v, rel_bias):
    b, h, s, d = q.shape
    nc = s // CHUNK
    scale = d ** -0.5
    qc = q.reshape(b, h, nc, CHUNK, d)

    def band(t):
        tc = t.reshape(b, h, nc, CHUNK, d)
        tp = jnp.pad(tc, ((0, 0), (0, 0), (LOOKBACK, 0), (0, 0), (0, 0)))
        return jnp.concatenate([tp[:, :, w:w + nc] for w in range(LOOKBACK + 1)], axis=3)

    kb, vb = band(k), band(v)
    bias = rel_bias.astype(jnp.float32)[:, rel_bias_index()]
    z = jnp.einsum('bhnqd,bhnkd->bhnqk', qc, kb,
                   preferred_element_type=jnp.float32) * scale + bias[None, :, None]
    slot_chunk = jnp.arange(BAND, dtype=jnp.int32) // CHUNK
    chunk_id = jnp.arange(nc, dtype=jnp.int32)
    valid = (chunk_id[:, None] + slot_chunk[None, :] - LOOKBACK) >= 0
    z = jnp.where(valid[None, None, :, None, :], z, NEG_INF)
    prob = jax.nn.softmax(z, axis=-1)
    o = jnp.einsum('bhnqk,bhnkd->bhnqd', prob.astype(vb.dtype), vb)
    return o.reshape(b, h, s, d)


def _fwd_setup_inputs(seed: int = 0) -> dict:
    key = jax.random.key(seed)
    ks = jax.random.split(key, 24)
    f32 = jnp.float32

    def w(k, shape, fan_in):
        return jax.random.normal(k, shape, f32) * (fan_in ** -0.5)

    def gain(k, n):
        return 1.0 + 0.05 * jax.random.normal(k, (DEPTH, n), f32)

    return {
        "x": jax.random.normal(ks[0], (BATCH, SEQ, D_MODEL), f32),
        "p": jax.random.normal(ks[1], (DEPTH, BATCH, SEQ, PLE_DIM), f32),
        "g_ffn1_pre": gain(ks[2], D_MODEL),
        "g_ffn1_post": gain(ks[3], D_MODEL),
        "w_ffn1_gate": w(ks[4], (DEPTH, D_MODEL, D_FF), D_MODEL),
        "w_ffn1_up": w(ks[5], (DEPTH, D_MODEL, D_FF), D_MODEL),
        "w_ffn1_down": w(ks[6], (DEPTH, D_FF, D_MODEL), D_FF),
        "g_mix_pre": gain(ks[7], D_MODEL),
        "g_mix_post": gain(ks[8], D_MODEL),
        "w_in": w(ks[9], (DEPTH, D_MODEL, 3 * MIX_WIDTH), D_MODEL),
        "g_out_sb": gain(ks[10], W_SB),
        "g_out_ch": gain(ks[11], W_CH),
        "rel_bias": 0.02 * jax.random.normal(ks[12], (DEPTH, H_CH, 2 * REL_CLIP + 1), f32),
        "w_out": w(ks[13], (DEPTH, MIX_WIDTH, D_MODEL), MIX_WIDTH),
        "g_ffn2_pre": gain(ks[14], D_MODEL),
        "g_ffn2_post": gain(ks[15], D_MODEL),
        "w_ffn2_gate": w(ks[16], (DEPTH, D_MODEL, D_FF), D_MODEL),
        "w_ffn2_up": w(ks[17], (DEPTH, D_MODEL, D_FF), D_MODEL),
        "w_ffn2_down": w(ks[18], (DEPTH, D_FF, D_MODEL), D_FF),
        "w_ple_proj": w(ks[19], (DEPTH, PLE_DIM, D_MODEL), PLE_DIM),
        "w_ple_gate": w(ks[20], (DEPTH, D_MODEL, D_MODEL), D_MODEL),
        "g_ple_post": gain(ks[21], D_MODEL),
    }


def _fwd_reference(x, p, g_ffn1_pre, g_ffn1_post, w_ffn1_gate, w_ffn1_up, w_ffn1_down,
              g_mix_pre, g_mix_post, w_in, g_out_sb, g_out_ch, rel_bias, w_out,
              g_ffn2_pre, g_ffn2_post, w_ffn2_gate, w_ffn2_up, w_ffn2_down,
              w_ple_proj, w_ple_gate, g_ple_post):
    h = x
    for i in range(DEPTH):
        f = swiglu(rms_norm(h, g_ffn1_pre[i]), w_ffn1_gate[i], w_ffn1_up[i], w_ffn1_down[i])
        h = h + 0.5 * rms_norm(f, g_ffn1_post[i])

        u = rms_norm(h, g_mix_pre[i])
        qkv = u @ w_in[i]
        q_a, k_a, v_a, q_b, k_b, v_b = jnp.split(
            qkv, np.cumsum([W_SB, W_SB, W_SB, W_CH, W_CH])[:5].tolist(), axis=-1)
        o_a = stick_breaking_attention(split_heads(q_a, H_SB), split_heads(k_a, H_SB),
                                       split_heads(v_a, H_SB))
        o_b = chunk_band_attention(split_heads(q_b, H_CH), split_heads(k_b, H_CH),
                                   split_heads(v_b, H_CH), rel_bias[i])
        mixed = jnp.concatenate([rms_norm(merge_heads(o_a), g_out_sb[i]),
                                 rms_norm(merge_heads(o_b), g_out_ch[i])], axis=-1)
        h = h + rms_norm(mixed @ w_out[i], g_mix_post[i])

        f = swiglu(rms_norm(h, g_ffn2_pre[i]), w_ffn2_gate[i], w_ffn2_up[i], w_ffn2_down[i])
        h = h + 0.5 * rms_norm(f, g_ffn2_post[i])

        e = (p[i] @ w_ple_proj[i]) * jax.nn.sigmoid(h @ w_ple_gate[i])
        h = h + rms_norm(e, g_ple_post[i])
    return h


import jax as _jax
import jax.numpy as _jnp

TWIN_FORMAT = 'train_step'
FWD_PARAMS = ['x', 'p', 'g_ffn1_pre', 'g_ffn1_post', 'w_ffn1_gate', 'w_ffn1_up', 'w_ffn1_down', 'g_mix_pre', 'g_mix_post', 'w_in', 'g_out_sb', 'g_out_ch', 'rel_bias', 'w_out', 'g_ffn2_pre', 'g_ffn2_post', 'w_ffn2_gate', 'w_ffn2_up', 'w_ffn2_down', 'w_ple_proj', 'w_ple_gate', 'g_ple_post']
TWIN_WEIGHTS = ['g_ffn1_pre', 'g_ffn1_post', 'w_ffn1_gate', 'w_ffn1_up', 'w_ffn1_down', 'g_mix_pre', 'g_mix_post', 'w_in', 'g_out_sb', 'g_out_ch', 'rel_bias', 'w_out', 'g_ffn2_pre', 'g_ffn2_post', 'w_ffn2_gate', 'w_ffn2_up', 'w_ffn2_down', 'w_ple_proj', 'w_ple_gate', 'g_ple_post']
TWIN_DIFF_INPUT = 'x'
TWIN_INPUTS = ['x', 'p', 'g_ffn1_pre', 'g_ffn1_post', 'w_ffn1_gate', 'w_ffn1_up', 'w_ffn1_down', 'g_mix_pre', 'g_mix_post', 'w_in', 'g_out_sb', 'g_out_ch', 'rel_bias', 'w_out', 'g_ffn2_pre', 'g_ffn2_post', 'w_ffn2_gate', 'w_ffn2_up', 'w_ffn2_down', 'w_ple_proj', 'w_ple_gate', 'g_ple_post', 'loss_target', 'm_g_ffn1_pre', 'm_g_ffn1_post', 'm_w_ffn1_gate', 'm_w_ffn1_up', 'm_w_ffn1_down', 'm_g_mix_pre', 'm_g_mix_post', 'm_w_in', 'm_g_out_sb', 'm_g_out_ch', 'm_rel_bias', 'm_w_out', 'm_g_ffn2_pre', 'm_g_ffn2_post', 'm_w_ffn2_gate', 'm_w_ffn2_up', 'm_w_ffn2_down', 'm_w_ple_proj', 'm_w_ple_gate', 'm_g_ple_post', 'v_g_ffn1_pre', 'v_g_ffn1_post', 'v_w_ffn1_gate', 'v_w_ffn1_up', 'v_w_ffn1_down', 'v_g_mix_pre', 'v_g_mix_post', 'v_w_in', 'v_g_out_sb', 'v_g_out_ch', 'v_rel_bias', 'v_w_out', 'v_g_ffn2_pre', 'v_g_ffn2_post', 'v_w_ffn2_gate', 'v_w_ffn2_up', 'v_w_ffn2_down', 'v_w_ple_proj', 'v_w_ple_gate', 'v_g_ple_post']
TWIN_OUTPUTS = ['loss', 'grad_x', 'grad_g_ffn1_pre', 'grad_g_ffn1_post', 'grad_w_ffn1_gate', 'grad_w_ffn1_up', 'grad_w_ffn1_down', 'grad_g_mix_pre', 'grad_g_mix_post', 'grad_w_in', 'grad_g_out_sb', 'grad_g_out_ch', 'grad_rel_bias', 'grad_w_out', 'grad_g_ffn2_pre', 'grad_g_ffn2_post', 'grad_w_ffn2_gate', 'grad_w_ffn2_up', 'grad_w_ffn2_down', 'grad_w_ple_proj', 'grad_w_ple_gate', 'grad_g_ple_post', 'delta_g_ffn1_pre', 'delta_g_ffn1_post', 'delta_w_ffn1_gate', 'delta_w_ffn1_up', 'delta_w_ffn1_down', 'delta_g_mix_pre', 'delta_g_mix_post', 'delta_w_in', 'delta_g_out_sb', 'delta_g_out_ch', 'delta_rel_bias', 'delta_w_out', 'delta_g_ffn2_pre', 'delta_g_ffn2_post', 'delta_w_ffn2_gate', 'delta_w_ffn2_up', 'delta_w_ffn2_down', 'delta_w_ple_proj', 'delta_w_ple_gate', 'delta_g_ple_post', 'new_m_g_ffn1_pre', 'new_m_g_ffn1_post', 'new_m_w_ffn1_gate', 'new_m_w_ffn1_up', 'new_m_w_ffn1_down', 'new_m_g_mix_pre', 'new_m_g_mix_post', 'new_m_w_in', 'new_m_g_out_sb', 'new_m_g_out_ch', 'new_m_rel_bias', 'new_m_w_out', 'new_m_g_ffn2_pre', 'new_m_g_ffn2_post', 'new_m_w_ffn2_gate', 'new_m_w_ffn2_up', 'new_m_w_ffn2_down', 'new_m_w_ple_proj', 'new_m_w_ple_gate', 'new_m_g_ple_post', 'new_v_g_ffn1_pre', 'new_v_g_ffn1_post', 'new_v_w_ffn1_gate', 'new_v_w_ffn1_up', 'new_v_w_ffn1_down', 'new_v_g_mix_pre', 'new_v_g_mix_post', 'new_v_w_in', 'new_v_g_out_sb', 'new_v_g_out_ch', 'new_v_rel_bias', 'new_v_w_out', 'new_v_g_ffn2_pre', 'new_v_g_ffn2_post', 'new_v_w_ffn2_gate', 'new_v_w_ffn2_up', 'new_v_w_ffn2_down', 'new_v_w_ple_proj', 'new_v_w_ple_gate', 'new_v_g_ple_post']
TWIN_LEAF_KINDS = {'loss': 'loss', 'grad_x': 'grad_x', 'grad_g_ffn1_pre': 'grad_w', 'grad_g_ffn1_post': 'grad_w', 'grad_w_ffn1_gate': 'grad_w', 'grad_w_ffn1_up': 'grad_w', 'grad_w_ffn1_down': 'grad_w', 'grad_g_mix_pre': 'grad_w', 'grad_g_mix_post': 'grad_w', 'grad_w_in': 'grad_w', 'grad_g_out_sb': 'grad_w', 'grad_g_out_ch': 'grad_w', 'grad_rel_bias': 'grad_w', 'grad_w_out': 'grad_w', 'grad_g_ffn2_pre': 'grad_w', 'grad_g_ffn2_post': 'grad_w', 'grad_w_ffn2_gate': 'grad_w', 'grad_w_ffn2_up': 'grad_w', 'grad_w_ffn2_down': 'grad_w', 'grad_w_ple_proj': 'grad_w', 'grad_w_ple_gate': 'grad_w', 'grad_g_ple_post': 'grad_w', 'delta_g_ffn1_pre': 'delta_w', 'delta_g_ffn1_post': 'delta_w', 'delta_w_ffn1_gate': 'delta_w', 'delta_w_ffn1_up': 'delta_w', 'delta_w_ffn1_down': 'delta_w', 'delta_g_mix_pre': 'delta_w', 'delta_g_mix_post': 'delta_w', 'delta_w_in': 'delta_w', 'delta_g_out_sb': 'delta_w', 'delta_g_out_ch': 'delta_w', 'delta_rel_bias': 'delta_w', 'delta_w_out': 'delta_w', 'delta_g_ffn2_pre': 'delta_w', 'delta_g_ffn2_post': 'delta_w', 'delta_w_ffn2_gate': 'delta_w', 'delta_w_ffn2_up': 'delta_w', 'delta_w_ffn2_down': 'delta_w', 'delta_w_ple_proj': 'delta_w', 'delta_w_ple_gate': 'delta_w', 'delta_g_ple_post': 'delta_w', 'new_m_g_ffn1_pre': 'new_m', 'new_m_g_ffn1_post': 'new_m', 'new_m_w_ffn1_gate': 'new_m', 'new_m_w_ffn1_up': 'new_m', 'new_m_w_ffn1_down': 'new_m', 'new_m_g_mix_pre': 'new_m', 'new_m_g_mix_post': 'new_m', 'new_m_w_in': 'new_m', 'new_m_g_out_sb': 'new_m', 'new_m_g_out_ch': 'new_m', 'new_m_rel_bias': 'new_m', 'new_m_w_out': 'new_m', 'new_m_g_ffn2_pre': 'new_m', 'new_m_g_ffn2_post': 'new_m', 'new_m_w_ffn2_gate': 'new_m', 'new_m_w_ffn2_up': 'new_m', 'new_m_w_ffn2_down': 'new_m', 'new_m_w_ple_proj': 'new_m', 'new_m_w_ple_gate': 'new_m', 'new_m_g_ple_post': 'new_m', 'new_v_g_ffn1_pre': 'new_v', 'new_v_g_ffn1_post': 'new_v', 'new_v_w_ffn1_gate': 'new_v', 'new_v_w_ffn1_up': 'new_v', 'new_v_w_ffn1_down': 'new_v', 'new_v_g_mix_pre': 'new_v', 'new_v_g_mix_post': 'new_v', 'new_v_w_in': 'new_v', 'new_v_g_out_sb': 'new_v', 'new_v_g_out_ch': 'new_v', 'new_v_rel_bias': 'new_v', 'new_v_w_out': 'new_v', 'new_v_g_ffn2_pre': 'new_v', 'new_v_g_ffn2_post': 'new_v', 'new_v_w_ffn2_gate': 'new_v', 'new_v_w_ffn2_up': 'new_v', 'new_v_w_ffn2_down': 'new_v', 'new_v_w_ple_proj': 'new_v', 'new_v_w_ple_gate': 'new_v', 'new_v_g_ple_post': 'new_v'}


def _forward(args):
    return _fwd_reference(*[args[k] for k in FWD_PARAMS])


def _output_shape():
    out = _jax.eval_shape(lambda: _forward(_fwd_setup_inputs(0)))
    return out.shape, out.dtype

N_MICROBATCH = 1
ADAM_LR = 0.001
ADAM_B1 = 0.9
ADAM_B2 = 0.999
ADAM_EPS = 1e-08
ADAM_WD = 0.01
ADAM_STEP = 10
PER_EXAMPLE_BATCH_AXIS = {'x': 0, 'p': 1, 'loss_target': 0}
SHARED_INPUTS = []
_WEIGHT_DTYPES = {'g_ffn1_pre': _jnp.float32, 'g_ffn1_post': _jnp.float32, 'w_ffn1_gate': _jnp.float32, 'w_ffn1_up': _jnp.float32, 'w_ffn1_down': _jnp.float32, 'g_mix_pre': _jnp.float32, 'g_mix_post': _jnp.float32, 'w_in': _jnp.float32, 'g_out_sb': _jnp.float32, 'g_out_ch': _jnp.float32, 'rel_bias': _jnp.float32, 'w_out': _jnp.float32, 'g_ffn2_pre': _jnp.float32, 'g_ffn2_post': _jnp.float32, 'w_ffn2_gate': _jnp.float32, 'w_ffn2_up': _jnp.float32, 'w_ffn2_down': _jnp.float32, 'w_ple_proj': _jnp.float32, 'w_ple_gate': _jnp.float32, 'g_ple_post': _jnp.float32}
MOMENT_SCALE = {'g_ffn1_pre': 3.975701e-01, 'g_ffn1_post': 3.939009e+00, 'w_ffn1_gate': 1.546449e-01, 'w_ffn1_up': 1.558221e-01, 'w_ffn1_down': 2.597467e-01, 'g_mix_pre': 4.926112e-01, 'g_mix_post': 1.598907e+01, 'w_in': 2.781245e-01, 'g_out_sb': 3.198348e-01, 'g_out_ch': 3.787912e-01, 'rel_bias': 1.254859e-01, 'w_out': 3.440595e-01, 'g_ffn2_pre': 2.378106e-01, 'g_ffn2_post': 3.991170e+00, 'w_ffn2_gate': 9.135008e-02, 'w_ffn2_up': 1.157085e-01, 'w_ffn2_down': 1.918578e-01, 'w_ple_proj': 2.429843e-01, 'w_ple_gate': 1.282562e-01, 'g_ple_post': 1.609216e+01}


def _to_microbatches(a, axis):
    t = _jnp.moveaxis(a, axis, 0)
    t = t.reshape((N_MICROBATCH, t.shape[0] // N_MICROBATCH) + t.shape[1:])
    return _jnp.moveaxis(t, 1, axis + 1)


def setup_inputs(seed: int = 0) -> dict:
    inp = _fwd_setup_inputs(seed)
    key = _jax.random.fold_in(_jax.random.key(seed), 7919)
    shape, _ = _output_shape()
    out = dict(inp)
    out["loss_target"] = _jax.random.normal(_jax.random.fold_in(key, 0), shape, _jnp.float32)
    for i, name in enumerate(TWIN_WEIGHTS):
        w = inp[name].astype(_jnp.float32)
        if MOMENT_SCALE is None:
            s = _jnp.sqrt(_jnp.mean(_jnp.square(w)) + 1e-30)
        else:
            s = MOMENT_SCALE[name]
        km, kv = _jax.random.split(_jax.random.fold_in(key, i + 1))
        out[name] = w
        out["m_" + name] = s * _jax.random.normal(km, w.shape, _jnp.float32)
        out["v_" + name] = (s * s) * _jax.random.uniform(kv, w.shape, _jnp.float32, 0.5, 1.5)
    if N_MICROBATCH > 1:
        for name, axis in PER_EXAMPLE_BATCH_AXIS.items():
            out[name] = _to_microbatches(out[name], axis)
    return {'x': out['x'], 'p': out['p'], 'g_ffn1_pre': out['g_ffn1_pre'], 'g_ffn1_post': out['g_ffn1_post'], 'w_ffn1_gate': out['w_ffn1_gate'], 'w_ffn1_up': out['w_ffn1_up'], 'w_ffn1_down': out['w_ffn1_down'], 'g_mix_pre': out['g_mix_pre'], 'g_mix_post': out['g_mix_post'], 'w_in': out['w_in'], 'g_out_sb': out['g_out_sb'], 'g_out_ch': out['g_out_ch'], 'rel_bias': out['rel_bias'], 'w_out': out['w_out'], 'g_ffn2_pre': out['g_ffn2_pre'], 'g_ffn2_post': out['g_ffn2_post'], 'w_ffn2_gate': out['w_ffn2_gate'], 'w_ffn2_up': out['w_ffn2_up'], 'w_ffn2_down': out['w_ffn2_down'], 'w_ple_proj': out['w_ple_proj'], 'w_ple_gate': out['w_ple_gate'], 'g_ple_post': out['g_ple_post'], 'loss_target': out['loss_target'], 'm_g_ffn1_pre': out['m_g_ffn1_pre'], 'm_g_ffn1_post': out['m_g_ffn1_post'], 'm_w_ffn1_gate': out['m_w_ffn1_gate'], 'm_w_ffn1_up': out['m_w_ffn1_up'], 'm_w_ffn1_down': out['m_w_ffn1_down'], 'm_g_mix_pre': out['m_g_mix_pre'], 'm_g_mix_post': out['m_g_mix_post'], 'm_w_in': out['m_w_in'], 'm_g_out_sb': out['m_g_out_sb'], 'm_g_out_ch': out['m_g_out_ch'], 'm_rel_bias': out['m_rel_bias'], 'm_w_out': out['m_w_out'], 'm_g_ffn2_pre': out['m_g_ffn2_pre'], 'm_g_ffn2_post': out['m_g_ffn2_post'], 'm_w_ffn2_gate': out['m_w_ffn2_gate'], 'm_w_ffn2_up': out['m_w_ffn2_up'], 'm_w_ffn2_down': out['m_w_ffn2_down'], 'm_w_ple_proj': out['m_w_ple_proj'], 'm_w_ple_gate': out['m_w_ple_gate'], 'm_g_ple_post': out['m_g_ple_post'], 'v_g_ffn1_pre': out['v_g_ffn1_pre'], 'v_g_ffn1_post': out['v_g_ffn1_post'], 'v_w_ffn1_gate': out['v_w_ffn1_gate'], 'v_w_ffn1_up': out['v_w_ffn1_up'], 'v_w_ffn1_down': out['v_w_ffn1_down'], 'v_g_mix_pre': out['v_g_mix_pre'], 'v_g_mix_post': out['v_g_mix_post'], 'v_w_in': out['v_w_in'], 'v_g_out_sb': out['v_g_out_sb'], 'v_g_out_ch': out['v_g_out_ch'], 'v_rel_bias': out['v_rel_bias'], 'v_w_out': out['v_w_out'], 'v_g_ffn2_pre': out['v_g_ffn2_pre'], 'v_g_ffn2_post': out['v_g_ffn2_post'], 'v_w_ffn2_gate': out['v_w_ffn2_gate'], 'v_w_ffn2_up': out['v_w_ffn2_up'], 'v_w_ffn2_down': out['v_w_ffn2_down'], 'v_w_ple_proj': out['v_w_ple_proj'], 'v_w_ple_gate': out['v_w_ple_gate'], 'v_g_ple_post': out['v_g_ple_post']}


def _loss(weights, diff, rest, loss_target):
    with _jax.named_scope("forward"):
        args = {**rest, TWIN_DIFF_INPUT: diff, **{k: w.astype(_WEIGHT_DTYPES[k]) for k, w in weights.items()}}
        y = _forward(args)
    with _jax.named_scope("loss_head"):
        err = _jnp.square(y.astype(_jnp.float32) - loss_target)
        return 0.5 * _jnp.sum(_jnp.mean(err, axis=-1)) if err.ndim else 0.5 * err


def _adamw(w, g, m, v):
    m = ADAM_B1 * m + (1.0 - ADAM_B1) * g
    v = ADAM_B2 * v + (1.0 - ADAM_B2) * _jnp.square(g)
    m_hat = m / (1.0 - ADAM_B1 ** ADAM_STEP)
    v_hat = v / (1.0 - ADAM_B2 ** ADAM_STEP)
    delta = -ADAM_LR * (m_hat / (_jnp.sqrt(v_hat) + ADAM_EPS) + ADAM_WD * w)
    return delta, m, v


def reference(x, p, g_ffn1_pre, g_ffn1_post, w_ffn1_gate, w_ffn1_up, w_ffn1_down, g_mix_pre, g_mix_post, w_in, g_out_sb, g_out_ch, rel_bias, w_out, g_ffn2_pre, g_ffn2_post, w_ffn2_gate, w_ffn2_up, w_ffn2_down, w_ple_proj, w_ple_gate, g_ple_post, loss_target, m_g_ffn1_pre, m_g_ffn1_post, m_w_ffn1_gate, m_w_ffn1_up, m_w_ffn1_down, m_g_mix_pre, m_g_mix_post, m_w_in, m_g_out_sb, m_g_out_ch, m_rel_bias, m_w_out, m_g_ffn2_pre, m_g_ffn2_post, m_w_ffn2_gate, m_w_ffn2_up, m_w_ffn2_down, m_w_ple_proj, m_w_ple_gate, m_g_ple_post, v_g_ffn1_pre, v_g_ffn1_post, v_w_ffn1_gate, v_w_ffn1_up, v_w_ffn1_down, v_g_mix_pre, v_g_mix_post, v_w_in, v_g_out_sb, v_g_out_ch, v_rel_bias, v_w_out, v_g_ffn2_pre, v_g_ffn2_post, v_w_ffn2_gate, v_w_ffn2_up, v_w_ffn2_down, v_w_ple_proj, v_w_ple_gate, v_g_ple_post):
    given = dict(x=x, p=p, g_ffn1_pre=g_ffn1_pre, g_ffn1_post=g_ffn1_post, w_ffn1_gate=w_ffn1_gate, w_ffn1_up=w_ffn1_up, w_ffn1_down=w_ffn1_down, g_mix_pre=g_mix_pre, g_mix_post=g_mix_post, w_in=w_in, g_out_sb=g_out_sb, g_out_ch=g_out_ch, rel_bias=rel_bias, w_out=w_out, g_ffn2_pre=g_ffn2_pre, g_ffn2_post=g_ffn2_post, w_ffn2_gate=w_ffn2_gate, w_ffn2_up=w_ffn2_up, w_ffn2_down=w_ffn2_down, w_ple_proj=w_ple_proj, w_ple_gate=w_ple_gate, g_ple_post=g_ple_post, loss_target=loss_target, m_g_ffn1_pre=m_g_ffn1_pre, m_g_ffn1_post=m_g_ffn1_post, m_w_ffn1_gate=m_w_ffn1_gate, m_w_ffn1_up=m_w_ffn1_up, m_w_ffn1_down=m_w_ffn1_down, m_g_mix_pre=m_g_mix_pre, m_g_mix_post=m_g_mix_post, m_w_in=m_w_in, m_g_out_sb=m_g_out_sb, m_g_out_ch=m_g_out_ch, m_rel_bias=m_rel_bias, m_w_out=m_w_out, m_g_ffn2_pre=m_g_ffn2_pre, m_g_ffn2_post=m_g_ffn2_post, m_w_ffn2_gate=m_w_ffn2_gate, m_w_ffn2_up=m_w_ffn2_up, m_w_ffn2_down=m_w_ffn2_down, m_w_ple_proj=m_w_ple_proj, m_w_ple_gate=m_w_ple_gate, m_g_ple_post=m_g_ple_post, v_g_ffn1_pre=v_g_ffn1_pre, v_g_ffn1_post=v_g_ffn1_post, v_w_ffn1_gate=v_w_ffn1_gate, v_w_ffn1_up=v_w_ffn1_up, v_w_ffn1_down=v_w_ffn1_down, v_g_mix_pre=v_g_mix_pre, v_g_mix_post=v_g_mix_post, v_w_in=v_w_in, v_g_out_sb=v_g_out_sb, v_g_out_ch=v_g_out_ch, v_rel_bias=v_rel_bias, v_w_out=v_w_out, v_g_ffn2_pre=v_g_ffn2_pre, v_g_ffn2_post=v_g_ffn2_post, v_w_ffn2_gate=v_w_ffn2_gate, v_w_ffn2_up=v_w_ffn2_up, v_w_ffn2_down=v_w_ffn2_down, v_w_ple_proj=v_w_ple_proj, v_w_ple_gate=v_w_ple_gate, v_g_ple_post=v_g_ple_post)
    weights = {n: given[n] for n in TWIN_WEIGHTS}
    shared = {n: given[n] for n in SHARED_INPUTS}
    per_example = {n: given[n] for n in ['x', 'p']}
    grad_fn = _jax.value_and_grad(_loss, argnums=(0, 1))

    def one_microbatch(ex, loss_target):
        ex = dict(ex)
        diff = ex.pop(TWIN_DIFF_INPUT)
        return grad_fn(weights, diff, {**shared, **ex}, loss_target)

    if N_MICROBATCH == 1:
        loss, (grad_w, grad_x) = one_microbatch(per_example, given["loss_target"])
    else:
        def body(carry, xs):
            loss_sum, grad_sum = carry
            l_k, (gw_k, gx_k) = one_microbatch(xs[0], xs[1])
            with _jax.named_scope("update"):
                return (loss_sum + l_k, _jax.tree.map(_jnp.add, grad_sum, gw_k)), gx_k

        init = (_jnp.zeros((), _jnp.float32), _jax.tree.map(_jnp.zeros_like, weights))
        (loss, grad_w), grad_x = _jax.lax.scan(body, init, (per_example, given["loss_target"]))
    with _jax.named_scope("update"):
        delta_w, new_m, new_v = {}, {}, {}
        for n in TWIN_WEIGHTS:
            delta_w[n], new_m[n], new_v[n] = _adamw(weights[n], grad_w[n], given["m_" + n], given["v_" + n])
    return (loss, grad_x, *[grad_w[n] for n in TWIN_WEIGHTS], *[delta_w[n] for n in TWIN_WEIGHTS],
            *[new_m[n] for n in TWIN_WEIGHTS], *[new_v[n] for n in TWIN_WEIGHTS])
```

```python
import functools

import jax
import jax.numpy as jnp
from jax import lax
from jax.experimental import pallas as pl
from jax.experimental.pallas import tpu as pltpu

F32 = jnp.float32
BF16 = jnp.bfloat16

N_DEV = 8
D_MODEL = 1024
D_FF = 2816
FF_SHARD = D_FF // N_DEV
FF_SHARD_PAD = 384
D_FF_PAD = FF_SHARD_PAD * N_DEV
QKV_WIDTH = 3 * D_MODEL
QKV_SHARD = QKV_WIDTH // N_DEV
PLE_DIM = 256
ROW_SHARD = D_MODEL // N_DEV
HEAD_DIM = 64
PAIR = 2 * HEAD_DIM
N_PAIRS = 4
CHUNK = 64
LOOKBACK = 8
REL_CLIP = 128
N_REL = 2 * REL_CLIP + 1
CH_QB = 256
CH_LOOK = LOOKBACK * CHUNK
CH_WIN = CH_LOOK + CH_QB
SB_BLK = 256
EPS = 1e-6
NEG_INF = -1e30
ATT_SCALE = HEAD_DIM ** -0.5
ADAM_LR = 0.001
ADAM_B1 = 0.9
ADAM_B2 = 0.999
ADAM_EPS = 1e-08
ADAM_WD = 0.01
ADAM_STEP = 10
VMEM_LIMIT_BYTES = 48 * 1024 * 1024
MESH = pl.DeviceIdType.MESH

ANY = pl.BlockSpec(memory_space=pl.ANY)
VMEM = pl.BlockSpec(memory_space=pltpu.VMEM)


def _params(*sem):
    return pltpu.CompilerParams(dimension_semantics=sem or None,
                                vmem_limit_bytes=VMEM_LIMIT_BYTES)


def _sds(shape, dtype=F32):
    return jax.ShapeDtypeStruct(shape, dtype)


def _bf(x):
    return x.astype(BF16)


def _dot(a, b):
    return jnp.dot(_bf(a), _bf(b), preferred_element_type=F32)


def _dot_nt(a, b):
    return lax.dot_general(_bf(a), _bf(b), (((1,), (1,)), ((), ())),
                           preferred_element_type=F32)


def _dot_tn(a, b):
    return lax.dot_general(_bf(a), _bf(b), (((0,), (0,)), ((), ())),
                           preferred_element_type=F32)


def _sigmoid(x):
    return 1.0 / (1.0 + jnp.exp(-x))


def _softplus(x):
    return jnp.maximum(x, 0.0) + jnp.log(1.0 + jnp.exp(-jnp.abs(x)))


def _rstd(x):
    return lax.rsqrt(jnp.mean(x * x, axis=-1, keepdims=True) + EPS)


def _rms(x, g):
    return x * _rstd(x) * g


def _rms_bwd(dy, x, g):
    r = _rstd(x)
    w = dy * g
    dx = r * (w - x * (r * r) * jnp.mean(w * x, axis=-1, keepdims=True))
    dg = jnp.sum(dy * (x * r), axis=0, keepdims=True)
    return dx, dg


def _dot_exact01(x, u):
    hi = _bf(x)
    lo = _bf(x - hi.astype(F32))
    return (jnp.dot(hi, u, preferred_element_type=F32)
            + jnp.dot(lo, u, preferred_element_type=F32))


def _head_masks():
    lane = lax.broadcasted_iota(jnp.int32, (1, PAIR), 1)
    return lane < HEAD_DIM, lane >= HEAD_DIM


def _ffn_fwd(x, g_pre, g_post, wg, wu, wd, *, name):
    t = x.shape[0]
    tm, tj = 512, 512
    ni, nj = t // tm, D_FF_PAD // tj

    def body(x_ref, gpre_ref, gpost_ref, wg_ref, wu_ref, wd_ref,
             h_ref, n_ref, a_ref, b_ref, f_ref, acc_ref):
        j = pl.program_id(1)

        @pl.when(j == 0)
        def _():
            n_ref[...] = _bf(_rms(x_ref[...], gpre_ref[...]))
            acc_ref[...] = jnp.zeros_like(acc_ref)

        n = n_ref[...]
        a = jnp.dot(n, wg_ref[...], preferred_element_type=F32)
        b = jnp.dot(n, wu_ref[...], preferred_element_type=F32)
        a_ref[...] = a
        b_ref[...] = b
        hmid = a * _sigmoid(a) * b
        acc_ref[...] += jnp.dot(_bf(hmid), wd_ref[...], preferred_element_type=F32)

        @pl.when(j == nj - 1)
        def _():
            f = acc_ref[...]
            f_ref[...] = f
            h_ref[...] = x_ref[...] + 0.5 * _rms(f, gpost_ref[...])

    row = pl.BlockSpec((tm, D_MODEL), lambda i, j: (i, 0))
    gain = pl.BlockSpec((1, D_MODEL), lambda i, j: (0, 0))
    col = pl.BlockSpec((tm, tj), lambda i, j: (i, j))
    return pl.pallas_call(
        body, name=name, grid=(ni, nj),
        in_specs=[row, gain, gain,
                  pl.BlockSpec((D_MODEL, tj), lambda i, j: (0, j)),
                  pl.BlockSpec((D_MODEL, tj), lambda i, j: (0, j)),
                  pl.BlockSpec((tj, D_MODEL), lambda i, j: (j, 0))],
        out_specs=[row, row, col, col, row],
        out_shape=[_sds((t, D_MODEL)), _sds((t, D_MODEL), BF16),
                   _sds((t, D_FF_PAD)), _sds((t, D_FF_PAD)), _sds((t, D_MODEL))],
        scratch_shapes=[pltpu.VMEM((tm, D_MODEL), F32)],
        compiler_params=_params("arbitrary", "arbitrary"),
    )(x, g_pre, g_post, wg, wu, wd)


def _ffn_bwd(n, df, a, b, wg, wu, wd, *, name):
    t = n.shape[0]
    tj, ts = 256, 512
    nj, ns = D_FF_PAD // tj, t // ts

    def body(n_hbm, df_hbm, a_ref, b_ref, wg_ref, wu_ref, wd_ref,
             dwg_ref, dwu_ref, dwd_ref, dn_hbm,
             n_v, df_v, dn_v, ag, au, ad, sem):
        j = pl.program_id(0)

        @pl.when(j == 0)
        def _():
            c1 = pltpu.make_async_copy(n_hbm, n_v, sem.at[0])
            c2 = pltpu.make_async_copy(df_hbm, df_v, sem.at[1])
            c1.start()
            c2.start()
            dn_v[...] = jnp.zeros_like(dn_v)
            c1.wait()
            c2.wait()

        ag[...] = jnp.zeros_like(ag)
        au[...] = jnp.zeros_like(au)
        ad[...] = jnp.zeros_like(ad)
        wgj, wuj, wdj = wg_ref[...], wu_ref[...], wd_ref[...]
        for s in range(ns):
            rows = pl.ds(s * ts, ts)
            av, bv = a_ref[rows, :], b_ref[rows, :]
            sig = _sigmoid(av)
            silu = av * sig
            dfr = df_v[rows, :]
            nr = n_v[rows, :]
            dhmid = _dot_nt(dfr, wdj)
            da = dhmid * bv * (sig * (1.0 + av * (1.0 - sig)))
            db = dhmid * silu
            ad[...] += _dot_tn(silu * bv, dfr)
            ag[...] += _dot_tn(nr, da)
            au[...] += _dot_tn(nr, db)
            dn_v[rows, :] += _dot_nt(da, wgj) + _dot_nt(db, wuj)
        dwg_ref[...] = _bf(ag[...])
        dwu_ref[...] = _bf(au[...])
        dwd_ref[...] = _bf(ad[...])

        @pl.when(j == nj - 1)
        def _():
            c = pltpu.make_async_copy(dn_v, dn_hbm, sem.at[0])
            c.start()
            c.wait()

    colw = pl.BlockSpec((D_MODEL, tj), lambda j: (0, j))
    roww = pl.BlockSpec((tj, D_MODEL), lambda j: (j, 0))
    act = pl.BlockSpec((t, tj), lambda j: (0, j))
    return pl.pallas_call(
        body, name=name, grid=(nj,),
        in_specs=[ANY, ANY, act, act, colw, colw, roww],
        out_specs=[colw, colw, roww, ANY],
        out_shape=[_sds((D_MODEL, D_FF_PAD), BF16), _sds((D_MODEL, D_FF_PAD), BF16),
                   _sds((D_FF_PAD, D_MODEL), BF16), _sds((t, D_MODEL))],
        scratch_shapes=[pltpu.VMEM((t, D_MODEL), BF16), pltpu.VMEM((t, D_MODEL), BF16),
                        pltpu.VMEM((t, D_MODEL), F32),
                        pltpu.VMEM((D_MODEL, tj), F32), pltpu.VMEM((D_MODEL, tj), F32),
                        pltpu.VMEM((tj, D_MODEL), F32), pltpu.SemaphoreType.DMA((2,))],
        compiler_params=_params("arbitrary"),
    )(n, df, a, b, wg, wu, wd)


def _junction(dres, pre=None, post=None, *, name):
    t = dres.shape[0]
    tm = 512
    ni = t // tm
    n_in = 1 + (3 if pre else 0) + (2 if post else 0)
    coef = post[2] if post else None

    def body(*refs):
        ins, outs = list(refs[:n_in]), list(refs[n_in:])
        i = pl.program_id(0)
        dh = ins.pop(0)[...]
        if pre:
            dn_ref, x_ref, gpre_ref = ins.pop(0), ins.pop(0), ins.pop(0)
            dh_ref, dgpre_ref = outs.pop(0), outs.pop(0)
            dx, dg = _rms_bwd(dn_ref[...], x_ref[...], gpre_ref[...])
            dh = dh + dx
            dh_ref[...] = dh

            @pl.when(i == 0)
            def _():
                dgpre_ref[...] = jnp.zeros_like(dgpre_ref)
            dgpre_ref[...] += dg
        if post:
            f_ref, gpost_ref = ins.pop(0), ins.pop(0)
            df_ref, dgpost_ref = outs.pop(0), outs.pop(0)
            df, dg = _rms_bwd(coef * dh, f_ref[...], gpost_ref[...])
            df_ref[...] = _bf(df)

            @pl.when(i == 0)
            def _():
                dgpost_ref[...] = jnp.zeros_like(dgpost_ref)
            dgpost_ref[...] += dg

    row = pl.BlockSpec((tm, D_MODEL), lambda i: (i, 0))
    gain = pl.BlockSpec((1, D_MODEL), lambda i: (0, 0))
    args, in_specs, out_specs, out_shape = [dres], [row], [], []
    if pre:
        args += list(pre)
        in_specs += [row, row, gain]
        out_specs += [row, gain]
        out_shape += [_sds((t, D_MODEL)), _sds((1, D_MODEL))]
    if post:
        args += [post[0], post[1]]
        in_specs += [row, gain]
        out_specs += [row, gain]
        out_shape += [_sds((t, D_MODEL), BF16), _sds((1, D_MODEL))]
    return pl.pallas_call(
        body, name=name, grid=(ni,), in_specs=in_specs, out_specs=out_specs,
        out_shape=out_shape, compiler_params=_params("arbitrary"),
    )(*args)


def _qkv_fwd(h, g, win, *, name):
    t = h.shape[0]
    tm, tn = 512, 768
    ni, nj = t // tm, QKV_WIDTH // tn

    def body(h_ref, g_ref, w_ref, qkv_ref, u_ref):
        @pl.when(pl.program_id(1) == 0)
        def _():
            u_ref[...] = _bf(_rms(h_ref[...], g_ref[...]))
        qkv_ref[...] = jnp.dot(u_ref[...], w_ref[...], preferred_element_type=F32)

    row = pl.BlockSpec((tm, D_MODEL), lambda i, j: (i, 0))
    return pl.pallas_call(
        body, name=name, grid=(ni, nj),
        in_specs=[row, pl.BlockSpec((1, D_MODEL), lambda i, j: (0, 0)),
                  pl.BlockSpec((D_MODEL, tn), lambda i, j: (0, j))],
        out_specs=[pl.BlockSpec((tm, tn), lambda i, j: (i, j)), row],
        out_shape=[_sds((t, QKV_WIDTH)), _sds((t, D_MODEL), BF16)],
        compiler_params=_params("arbitrary", "arbitrary"),
    )(h, g, win)


def _qkv_bwd(dq, dk, dv, u, win, *, name):
    t = u.shape[0]
    tn, ts = 512, 512
    nj, ns = QKV_WIDTH // tn, t // ts

    def body(dq_ref, dk_ref, dv_ref, u_ref, w_ref, dw_ref, du_hbm, du_v, acc_ref, sem):
        j = pl.program_id(0)

        @pl.when(j == 0)
        def _():
            du_v[...] = jnp.zeros_like(du_v)

        wj = w_ref[...]
        for role, d_ref in enumerate((dq_ref, dk_ref, dv_ref)):
            @pl.when(j % 3 == role)
            def _():
                acc_ref[...] = jnp.zeros_like(acc_ref)
                for s in range(ns):
                    rows = pl.ds(s * ts, ts)
                    dcol = d_ref[rows, :]
                    acc_ref[...] += _dot_tn(u_ref[rows, :], dcol)
                    du_v[rows, :] += _dot_nt(dcol, wj)
                dw_ref[...] = _bf(acc_ref[...])

        @pl.when(j == nj - 1)
        def _():
            c = pltpu.make_async_copy(du_v, du_hbm, sem)
            c.start()
            c.wait()

    colw = pl.BlockSpec((D_MODEL, tn), lambda j: (0, j))
    grp = pl.BlockSpec((t, tn), lambda j: (0, j // 3))
    return pl.pallas_call(
        body, name=name, grid=(nj,),
        in_specs=[grp, grp, grp, pl.BlockSpec((t, D_MODEL), lambda j: (0, 0)), colw],
        out_specs=[colw, ANY],
        out_shape=[_sds((D_MODEL, QKV_WIDTH), BF16), _sds((t, D_MODEL))],
        scratch_shapes=[pltpu.VMEM((t, D_MODEL), F32), pltpu.VMEM((D_MODEL, tn), F32),
                        pltpu.SemaphoreType.DMA],
        compiler_params=_params("arbitrary"),
    )(dq, dk, dv, u, win)


def _sb_tile(qm, kj, q0, k0):
    z = _dot_nt(qm, kj) * ATT_SCALE
    rows = q0 + lax.broadcasted_iota(jnp.int32, z.shape, 0)
    cols = k0 + lax.broadcasted_iota(jnp.int32, z.shape, 1)
    return z, cols < rows, _softplus(z)


def _tri(n, inclusive):
    r = lax.broadcasted_iota(jnp.int32, (n, n), 0)
    c = lax.broadcasted_iota(jnp.int32, (n, n), 1)
    return jnp.where((r >= c) if inclusive else (r > c), 1.0, 0.0).astype(BF16)


def _sb_fwd(qkv, *, name):
    t = qkv.shape[0]
    blk = SB_BLK
    ni = t // blk

    def body(q_ref, k_ref, v_ref, o_ref, ltot_ref):
        i = pl.program_id(1)
        q0 = i * blk
        u_strict = _tri(blk, False)
        q = q_ref[...]
        outs, totals = [], []
        for hm in _head_masks():
            qm = jnp.where(hm, q, 0.0)

            def step(jj, carry):
                acc, c_l = carry
                k0 = pl.multiple_of((i - jj) * blk, blk)
                kj = k_ref[pl.ds(k0, blk), :]
                vj = v_ref[pl.ds(k0, blk), :]
                z, mask, sp = _sb_tile(qm, kj, q0, k0)
                lf = jnp.where(mask, -sp, 0.0)
                later = _dot_exact01(lf, u_strict) + c_l
                a = jnp.where(mask, jnp.exp(z - sp + later), 0.0)
                acc = acc + _dot(a, vj)
                return acc, c_l + jnp.sum(lf, axis=1, keepdims=True)

            acc, c_l = lax.fori_loop(
                0, i + 1, step,
                (jnp.zeros((blk, PAIR), F32), jnp.zeros((blk, 1), F32)))
            outs.append(acc)
            totals.append(c_l)
        lo = _head_masks()[0]
        o_ref[...] = jnp.where(lo, outs[0], outs[1])
        ltot_ref[...] = jnp.where(lo, totals[0], totals[1])

    blkspec = pl.BlockSpec((blk, PAIR), lambda p, i: (i, p))
    return pl.pallas_call(
        body, name=name, grid=(N_PAIRS, ni),
        in_specs=[blkspec,
                  pl.BlockSpec((t, PAIR), lambda p, i: (0, N_PAIRS + p)),
                  pl.BlockSpec((t, PAIR), lambda p, i: (0, 2 * N_PAIRS + p))],
        out_specs=[blkspec, blkspec],
        out_shape=[_sds((t, D_MODEL)), _sds((t, D_MODEL // 2))],
        compiler_params=_params("arbitrary", "arbitrary"),
    )(qkv, qkv, qkv)


def _sb_bwd(qkv, ltot, do, *, name):
    t = qkv.shape[0]
    blk = SB_BLK
    ni = t // blk

    def body(q_ref, k_ref, v_ref, lt_ref, do_ref, dq_ref, dkout_ref, dvout_ref, dk_ref, dv_ref):
        i = pl.program_id(1)
        q0 = i * blk

        @pl.when(i == 0)
        def _():
            dk_ref[...] = jnp.zeros_like(dk_ref)
            dv_ref[...] = jnp.zeros_like(dv_ref)

        r = lax.broadcasted_iota(jnp.int32, (blk, blk), 0)
        c = lax.broadcasted_iota(jnp.int32, (blk, blk), 1)
        u_upto = jnp.where(r <= c, 1.0, 0.0).astype(BF16)
        u_before = jnp.where(r < c, 1.0, 0.0).astype(BF16)
        lane = lax.broadcasted_iota(jnp.int32, (1, PAIR), 1)
        q, lt_blk, do_blk = q_ref[...], lt_ref[...], do_ref[...]
        dqs = []
        for h, hm in enumerate(_head_masks()):
            qm = _bf(jnp.where(hm, q, 0.0))
            dom = _bf(jnp.where(hm, do_blk, 0.0))
            total = jnp.sum(jnp.where(lane == h * HEAD_DIM, lt_blk, 0.0), axis=1, keepdims=True)

            def step(j, carry):
                dq_acc, c_l, c_g = carry
                k0 = pl.multiple_of(j * blk, blk)
                krows = pl.ds(k0, blk)
                kj = k_ref[krows, :]
                vj = v_ref[krows, :]
                z, mask, sp = _sb_tile(qm, kj, q0, k0)
                sig = jnp.exp(z - sp)
                lf = jnp.where(mask, -sp, 0.0)
                later = total - (_dot_exact01(lf, u_upto) + c_l)
                a = jnp.where(mask, jnp.exp(z - sp + later), 0.0)
                g = a * _dot_nt(dom, vj)
                g_before = _dot_exact01(g, u_before) + c_g
                dz = jnp.where(mask, g * (1.0 - sig) - g_before * sig, 0.0) * ATT_SCALE
                dq_acc = dq_acc + _dot(dz, kj)
                dk_ref[krows, :] += _dot_tn(dz, qm)
                dv_ref[krows, :] += _dot_tn(a, dom)
                return (dq_acc, c_l + jnp.sum(lf, axis=1, keepdims=True),
                        c_g + jnp.sum(g, axis=1, keepdims=True))

            dq_acc, _, _ = lax.fori_loop(
                0, i + 1, step,
                (jnp.zeros((blk, PAIR), F32), jnp.zeros((blk, 1), F32),
                 jnp.zeros((blk, 1), F32)))
            dqs.append(dq_acc)
        dq_ref[...] = _bf(jnp.where(_head_masks()[0], dqs[0], dqs[1]))

        @pl.when(i == ni - 1)
        def _():
            dkout_ref[...] = _bf(dk_ref[...])
            dvout_ref[...] = _bf(dv_ref[...])

    blkspec = lambda off: pl.BlockSpec((blk, PAIR), lambda p, i: (i, off + p))
    full = lambda off: pl.BlockSpec((t, PAIR), lambda p, i: (0, off + p))
    return pl.pallas_call(
        body, name=name, grid=(N_PAIRS, ni),
        in_specs=[blkspec(0), full(N_PAIRS), full(2 * N_PAIRS), blkspec(0), blkspec(0)],
        out_specs=[blkspec(0), full(0), full(0)],
        out_shape=[_sds((t, D_MODEL), BF16)] * 3,
        scratch_shapes=[pltpu.VMEM((t, PAIR), F32), pltpu.VMEM((t, PAIR), F32)],
        compiler_params=_params("arbitrary", "arbitrary"),
    )(qkv, qkv, qkv, ltot, do)


def _ch_mask(i):
    r = lax.broadcasted_iota(jnp.int32, (CH_QB, CH_WIN), 0)
    c = lax.broadcasted_iota(jnp.int32, (CH_QB, CH_WIN), 1)
    qc = LOOKBACK + lax.shift_right_arithmetic(r, 6)
    kc = lax.shift_right_arithmetic(c, 6)
    first = i * (CH_QB // CHUNK) - LOOKBACK
    return (kc <= qc) & (kc >= qc - LOOKBACK) & (kc + first >= 0)


def _ch_probs(qm, kw, bias_h, mask):
    z = _dot_nt(qm, kw) * ATT_SCALE + bias_h
    z = jnp.where(mask, z, NEG_INF)
    e = jnp.exp(z - jnp.max(z, axis=1, keepdims=True))
    return e / jnp.sum(e, axis=1, keepdims=True)


def _ch_fill(pad_ref, src_ref, t):
    pad_ref[pl.ds(0, CH_LOOK), :] = jnp.zeros((CH_LOOK, PAIR), BF16)
    pad_ref[pl.ds(CH_LOOK, t), :] = _bf(src_ref[...])


def _ch_fwd(qkv, bias, o_in, *, name):
    t = qkv.shape[0]
    ni = t // CH_QB

    def body(q_ref, k_ref, v_ref, bias_ref, _alias, o_ref, kpad, vpad):
        i = pl.program_id(1)

        @pl.when(i == 0)
        def _():
            _ch_fill(kpad, k_ref, t)
            _ch_fill(vpad, v_ref, t)

        win = pl.ds(pl.multiple_of(i * CH_QB, CH_QB), CH_WIN)
        kw, vw = kpad[win, :], vpad[win, :]
        mask = _ch_mask(i)
        q = q_ref[...]
        outs = []
        for h, hm in enumerate(_head_masks()):
            p = _ch_probs(jnp.where(hm, q, 0.0), kw, bias_ref[h], mask)
            outs.append(_dot(p, vw))
        o_ref[...] = jnp.where(_head_masks()[0], outs[0], outs[1])

    full = lambda off: pl.BlockSpec((t, PAIR), lambda p, i: (0, off + p))
    return pl.pallas_call(
        body, name=name, grid=(N_PAIRS, ni),
        in_specs=[pl.BlockSpec((CH_QB, PAIR), lambda p, i: (i, 3 * N_PAIRS + p)),
                  full(4 * N_PAIRS), full(5 * N_PAIRS),
                  pl.BlockSpec((2, CH_QB, CH_WIN), lambda p, i: (p, 0, 0)), ANY],
        out_specs=pl.BlockSpec((CH_QB, PAIR), lambda p, i: (i, N_PAIRS + p)),
        out_shape=_sds((t, D_MODEL)),
        scratch_shapes=[pltpu.VMEM((t + CH_LOOK, PAIR), BF16)] * 2,
        input_output_aliases={4: 0},
        compiler_params=_params("arbitrary", "arbitrary"),
    )(qkv, qkv, qkv, bias, o_in)


def _ch_bwd(qkv, bias, o, do, dq_in, dk_in, dv_in, *, name):
    t = qkv.shape[0]
    ni = t // CH_QB

    def body(q_ref, k_ref, v_ref, bias_ref, o_ref, do_ref, _a0, _a1, _a2,
             dq_ref, dkout_ref, dvout_ref, dbias_ref, kpad, vpad, dkpad, dvpad):
        i = pl.program_id(1)

        @pl.when(i == 0)
        def _():
            _ch_fill(kpad, k_ref, t)
            _ch_fill(vpad, v_ref, t)
            dkpad[...] = jnp.zeros_like(dkpad)
            dvpad[...] = jnp.zeros_like(dvpad)
            dbias_ref[...] = jnp.zeros_like(dbias_ref)

        win = pl.ds(pl.multiple_of(i * CH_QB, CH_QB), CH_WIN)
        kw, vw = kpad[win, :], vpad[win, :]
        mask = _ch_mask(i)
        q, o_blk, do_blk = q_ref[...], o_ref[...], do_ref[...]
        dqs = []
        for h, hm in enumerate(_head_masks()):
            qm = _bf(jnp.where(hm, q, 0.0))
            dom = jnp.where(hm, do_blk, 0.0)
            delta = jnp.sum(dom * o_blk, axis=1, keepdims=True)
            dom = _bf(dom)
            p = _ch_probs(qm, kw, bias_ref[h], mask)
            ds = p * (_dot_nt(dom, vw) - delta)
            dbias_ref[h] += ds
            dsz = ds * ATT_SCALE
            dqs.append(_dot(dsz, kw))
            dkpad[win, :] += _dot_tn(dsz, qm)
            dvpad[win, :] += _dot_tn(p, dom)
        dq_ref[...] = _bf(jnp.where(_head_masks()[0], dqs[0], dqs[1]))

        @pl.when(i == ni - 1)
        def _():
            dkout_ref[...] = _bf(dkpad[pl.ds(CH_LOOK, t), :])
            dvout_ref[...] = _bf(dvpad[pl.ds(CH_LOOK, t), :])

    blkspec = lambda off: pl.BlockSpec((CH_QB, PAIR), lambda p, i: (i, off + p))
    full = lambda off: pl.BlockSpec((t, PAIR), lambda p, i: (0, off + p))
    bias_spec = pl.BlockSpec((2, CH_QB, CH_WIN), lambda p, i: (p, 0, 0))
    return pl.pallas_call(
        body, name=name, grid=(N_PAIRS, ni),
        in_specs=[blkspec(3 * N_PAIRS), full(4 * N_PAIRS), full(5 * N_PAIRS), bias_spec,
                  blkspec(N_PAIRS), blkspec(N_PAIRS), ANY, ANY, ANY],
        out_specs=[blkspec(N_PAIRS), full(N_PAIRS), full(N_PAIRS), bias_spec],
        out_shape=[_sds((t, D_MODEL), BF16)] * 3 + [_sds((2 * N_PAIRS, CH_QB, CH_WIN))],
        scratch_shapes=[pltpu.VMEM((t + CH_LOOK, PAIR), BF16)] * 2
        + [pltpu.VMEM((t + CH_LOOK, PAIR), F32)] * 2,
        input_output_aliases={6: 0, 7: 1, 8: 2},
        compiler_params=_params("arbitrary", "arbitrary"),
    )(qkv, qkv, qkv, bias, o, do, dq_in, dk_in, dv_in)


def _bias_expand(fvec, *, name):
    n_heads = fvec.shape[0]

    def body(f_ref, o_ref, rows8):
        row = f_ref[0]
        for r in range(8):
            rows8[pl.ds(r, 1), :] = pltpu.roll(row, r, 1)
        base = rows8[...]
        for blk in range(CH_QB // 8):
            o_ref[0, pl.ds(8 * blk, 8), :] = pltpu.roll(base, 8 * blk, 1)

    return pl.pallas_call(
        body, name=name, grid=(n_heads,),
        in_specs=[pl.BlockSpec((1, 1, CH_WIN), lambda h: (h, 0, 0))],
        out_specs=pl.BlockSpec((1, CH_QB, CH_WIN), lambda h: (h, 0, 0)),
        out_shape=_sds((n_heads, CH_QB, CH_WIN)),
        scratch_shapes=[pltpu.VMEM((8, CH_WIN), F32)],
        compiler_params=_params("arbitrary"),
    )(fvec)


def _bias_grad(dbias, *, name):
    n_heads = dbias.shape[0]
    first = CH_LOOK - REL_CLIP

    def body(d_ref, o_ref, acc8):
        acc = jnp.zeros((8, CH_WIN), F32)
        for blk in range(CH_QB // 8):
            acc = acc + pltpu.roll(d_ref[0, pl.ds(8 * blk, 8), :], (CH_WIN - 8 * blk) % CH_WIN, 1)
        acc8[...] = acc
        dvec = jnp.zeros((1, CH_WIN), F32)
        for r in range(8):
            dvec = dvec + pltpu.roll(acc8[pl.ds(r, 1), :], (CH_WIN - r) % CH_WIN, 1)
        lane = lax.broadcasted_iota(jnp.int32, (1, CH_WIN), 1)
        clipped = (lane <= first) | (lane >= first + REL_CLIP + CHUNK)
        total = jnp.sum(jnp.where(clipped, dvec, 0.0), axis=1, keepdims=True)
        o_ref[0] = jnp.where(lane == first, total, dvec)

    return pl.pallas_call(
        body, name=name, grid=(n_heads,),
        in_specs=[pl.BlockSpec((1, CH_QB, CH_WIN), lambda h: (h, 0, 0))],
        out_specs=pl.BlockSpec((1, 1, CH_WIN), lambda h: (h, 0, 0)),
        out_shape=_sds((n_heads, 1, CH_WIN)),
        scratch_shapes=[pltpu.VMEM((8, CH_WIN), F32)],
        compiler_params=_params("arbitrary"),
    )(dbias)


def _out_fwd(o, h1, g_sb, g_ch, g_post, wout, *, name):
    t = o.shape[0]
    tm = 512
    half = D_MODEL // 2

    def body(o_ref, h_ref, gsb_ref, gch_ref, gpost_ref, w_ref, h2_ref, mixed_ref, y_ref):
        ov = o_ref[...]
        mixed = jnp.concatenate([_rms(ov[:, :half], gsb_ref[...]),
                                 _rms(ov[:, half:], gch_ref[...])], axis=1)
        mixed_ref[...] = _bf(mixed)
        y = _dot(mixed, w_ref[...])
        y_ref[...] = y
        h2_ref[...] = h_ref[...] + _rms(y, gpost_ref[...])

    row = pl.BlockSpec((tm, D_MODEL), lambda i: (i, 0))
    gain = lambda n: pl.BlockSpec((1, n), lambda i: (0, 0))
    return pl.pallas_call(
        body, name=name, grid=(t // tm,),
        in_specs=[row, row, gain(half), gain(half), gain(D_MODEL),
                  pl.BlockSpec((D_MODEL, D_MODEL), lambda i: (0, 0))],
        out_specs=[row, row, row],
        out_shape=[_sds((t, D_MODEL)), _sds((t, D_MODEL), BF16), _sds((t, D_MODEL))],
        compiler_params=_params("arbitrary"),
    )(o, h1, g_sb, g_ch, g_post, wout)


def _out_bwd(dy, mixed, o, g_sb, g_ch, wout, *, name):
    t = o.shape[0]
    tm = 512
    ni = t // tm
    half = D_MODEL // 2

    def body(dy_ref, mixed_ref, o_ref, gsb_ref, gch_ref, w_ref,
             dw_ref, do_ref, dgsb_ref, dgch_ref, acc_ref):
        i = pl.program_id(0)

        @pl.when(i == 0)
        def _():
            acc_ref[...] = jnp.zeros_like(acc_ref)
            dgsb_ref[...] = jnp.zeros_like(dgsb_ref)
            dgch_ref[...] = jnp.zeros_like(dgch_ref)

        dyv = dy_ref[...]
        acc_ref[...] += _dot_tn(mixed_ref[...], dyv)
        dm = _dot_nt(dyv, w_ref[...])
        ov = o_ref[...]
        doa, dga = _rms_bwd(dm[:, :half], ov[:, :half], gsb_ref[...])
        dob, dgb = _rms_bwd(dm[:, half:], ov[:, half:], gch_ref[...])
        do_ref[...] = jnp.concatenate([doa, dob], axis=1)
        dgsb_ref[...] += dga
        dgch_ref[...] += dgb

        @pl.when(i == ni - 1)
        def _():
            dw_ref[...] = _bf(acc_ref[...])

    row = pl.BlockSpec((tm, D_MODEL), lambda i: (i, 0))
    gain = pl.BlockSpec((1, half), lambda i: (0, 0))
    sq = pl.BlockSpec((D_MODEL, D_MODEL), lambda i: (0, 0))
    return pl.pallas_call(
        body, name=name, grid=(ni,),
        in_specs=[row, row, row, gain, gain, sq],
        out_specs=[sq, row, gain, gain],
        out_shape=[_sds((D_MODEL, D_MODEL), BF16), _sds((t, D_MODEL)),
                   _sds((1, half)), _sds((1, half))],
        scratch_shapes=[pltpu.VMEM((D_MODEL, D_MODEL), F32)],
        compiler_params=_params("arbitrary"),
    )(dy, mixed, o, g_sb, g_ch, wout)


def _ple(p, h3, target, wp, wgate, g, *, name):
    t = h3.shape[0]
    tm = 512
    ni = t // tm

    def body(p_ref, h_ref, tgt_ref, wp_ref, wg_ref, g_ref,
             loss_ref, dres_ref, dwp_ref, dwg_ref, dg_ref, accp, accg):
        i = pl.program_id(0)

        @pl.when(i == 0)
        def _():
            loss_ref[...] = jnp.zeros_like(loss_ref)
            dg_ref[...] = jnp.zeros_like(dg_ref)
            accp[...] = jnp.zeros_like(accp)
            accg[...] = jnp.zeros_like(accg)

        pv, hv, gv = p_ref[...], h_ref[...], g_ref[...]
        pe = _dot(pv, wp_ref[...])
        sig = _sigmoid(_dot(hv, wg_ref[...]))
        e = pe * sig
        err = hv + _rms(e, gv) - tgt_ref[...]
        tok = jnp.mean(err * err, axis=-1, keepdims=True)
        loss_ref[...] += 0.5 * jnp.sum(tok, axis=0, keepdims=True)
        dh4 = err * (1.0 / D_MODEL)
        de, dg = _rms_bwd(dh4, e, gv)
        dg_ref[...] += dg
        dpe = de * sig
        dgt = de * pe * sig * (1.0 - sig)
        accp[...] += _dot_tn(pv, dpe)
        accg[...] += _dot_tn(hv, dgt)
        dres_ref[...] = dh4 + _dot_nt(dgt, wg_ref[...])

        @pl.when(i == ni - 1)
        def _():
            dwp_ref[...] = _bf(accp[...])
            dwg_ref[...] = _bf(accg[...])

    row = pl.BlockSpec((tm, D_MODEL), lambda i: (i, 0))
    const = lambda r, c: pl.BlockSpec((r, c), lambda i: (0, 0))
    return pl.pallas_call(
        body, name=name, grid=(ni,),
        in_specs=[pl.BlockSpec((tm, PLE_DIM), lambda i: (i, 0)), row, row,
                  const(PLE_DIM, D_MODEL), const(D_MODEL, D_MODEL), const(1, D_MODEL)],
        out_specs=[const(1, 128), row, const(PLE_DIM, D_MODEL), const(D_MODEL, D_MODEL),
                   const(1, D_MODEL)],
        out_shape=[_sds((1, 128)), _sds((t, D_MODEL)), _sds((PLE_DIM, D_MODEL), BF16),
                   _sds((D_MODEL, D_MODEL), BF16), _sds((1, D_MODEL))],
        scratch_shapes=[pltpu.VMEM((PLE_DIM, D_MODEL), F32), pltpu.VMEM((D_MODEL, D_MODEL), F32)],
        compiler_params=_params("arbitrary"),
    )(p, h3, target, wp, wgate, g)


def _rel_bias_to_fvec(rel_bias):
    rev = rel_bias[:, ::-1]
    n_heads = rel_bias.shape[0]
    first = CH_LOOK - REL_CLIP
    n_var = REL_CLIP + CHUNK
    clipped = rev[:, :1]
    fvec = jnp.concatenate([jnp.broadcast_to(clipped, (n_heads, first)), rev[:, :n_var],
                            jnp.broadcast_to(clipped, (n_heads, CH_WIN - first - n_var))], axis=1)
    return fvec.reshape(n_heads, 1, CH_WIN)


def _fvec_grad_to_rel_bias(dfvec):
    first = CH_LOOK - REL_CLIP
    n_var = REL_CLIP + CHUNK
    rev = jnp.pad(dfvec[:, 0, first:first + n_var], ((0, 0), (0, N_REL - n_var)))
    return rev[:, ::-1]


def _local_step(x, p, target, g, w, fvec):
    h1, n1, a1, b1, f1 = _ffn_fwd(x, g["ffn1_pre"], g["ffn1_post"],
                                  w["ffn1_gate"], w["ffn1_up"], w["ffn1_down"], name="ffn1_fwd")
    qkv, u = _qkv_fwd(h1, g["mix_pre"], w["in"], name="qkv_fwd")
    bias = _bias_expand(fvec, name="bias_expand")
    o, ltot = _sb_fwd(qkv, name="sb_fwd")
    o = _ch_fwd(qkv, bias, o, name="ch_fwd")
    h2, mixed, y = _out_fwd(o, h1, g["out_sb"], g["out_ch"], g["mix_post"], w["out"], name="out_fwd")
    h3, n2, a2, b2, f2 = _ffn_fwd(h2, g["ffn2_pre"], g["ffn2_post"],
                                  w["ffn2_gate"], w["ffn2_up"], w["ffn2_down"], name="ffn2_fwd")
    loss, dh3, dwp, dwgate, dg_ple = _ple(p, h3, target, w["ple_proj"], w["ple_gate"],
                                          g["ple_post"], name="ple")

    df2, dg_ffn2_post = _junction(dh3, post=(f2, g["ffn2_post"], 0.5), name="junction3")
    dwg2, dwu2, dwd2, dn2 = _ffn_bwd(n2, df2, a2, b2, w["ffn2_gate"], w["ffn2_up"],
                                     w["ffn2_down"], name="ffn2_bwd")
    dh2, dg_ffn2_pre, dy, dg_mix_post = _junction(
        dh3, pre=(dn2, h2, g["ffn2_pre"]), post=(y, g["mix_post"], 1.0), name="junction2")
    dwout, do, dg_sb, dg_ch = _out_bwd(dy, mixed, o, g["out_sb"], g["out_ch"], w["out"],
                                       name="out_bwd")
    dq, dk, dv = _sb_bwd(qkv, ltot, do, name="sb_bwd")
    dq, dk, dv, dbias = _ch_bwd(qkv, bias, o, do, dq, dk, dv, name="ch_bwd")
    dfvec = _bias_grad(dbias, name="bias_grad")
    dwin, du = _qkv_bwd(dq, dk, dv, u, w["in"], name="qkv_bwd")
    dh1, dg_mix_pre, df1, dg_ffn1_post = _junction(
        dh2, pre=(du, h1, g["mix_pre"]), post=(f1, g["ffn1_post"], 0.5), name="junction1")
    dwg1, dwu1, dwd1, dn1 = _ffn_bwd(n1, df1, a1, b1, w["ffn1_gate"], w["ffn1_up"],
                                     w["ffn1_down"], name="ffn1_bwd")
    dx, dg_ffn1_pre = _junction(dh1, pre=(dn1, x, g["ffn1_pre"]), name="junction0")

    dw = {"ffn1_gate": dwg1, "ffn1_up": dwu1, "ffn1_down": dwd1, "in": dwin, "out": dwout,
          "ffn2_gate": dwg2, "ffn2_up": dwu2, "ffn2_down": dwd2,
          "ple_proj": dwp, "ple_gate": dwgate}
    dg = {"ffn1_pre": dg_ffn1_pre, "ffn1_post": dg_ffn1_post, "mix_pre": dg_mix_pre,
          "mix_post": dg_mix_post, "out_sb": dg_sb, "out_ch": dg_ch,
          "ffn2_pre": dg_ffn2_pre, "ffn2_post": dg_ffn2_post, "ple_post": dg_ple}
    return loss, dx, dw, dg, dfvec


_WEIGHTS = (
    ("ffn1_gate", "col", FF_SHARD, FF_SHARD_PAD, D_MODEL),
    ("ffn1_up", "col", FF_SHARD, FF_SHARD_PAD, D_MODEL),
    ("ffn1_down", "row", FF_SHARD, FF_SHARD_PAD, D_MODEL),
    ("in", "col", QKV_SHARD, QKV_SHARD, D_MODEL),
    ("out", "row", ROW_SHARD, ROW_SHARD, D_MODEL),
    ("ffn2_gate", "col", FF_SHARD, FF_SHARD_PAD, D_MODEL),
    ("ffn2_up", "col", FF_SHARD, FF_SHARD_PAD, D_MODEL),
    ("ffn2_down", "row", FF_SHARD, FF_SHARD_PAD, D_MODEL),
    ("ple_proj", "col", ROW_SHARD, ROW_SHARD, PLE_DIM),
    ("ple_gate", "row", ROW_SHARD, ROW_SHARD, D_MODEL),
)


def _shard_shape(kind, size, other):
    return (other, size) if kind == "col" else (size, other)


def _window(ref, kind, start, size):
    return ref.at[:, pl.ds(start, size)] if kind == "col" else ref.at[pl.ds(start, size), :]


def _device_tuple(k):
    return (k // 4, (k // 2) % 2, k % 2)


def _my_index():
    return 4 * lax.axis_index("x") + 2 * lax.axis_index("y") + lax.axis_index("c")


def _gather_weights(shards):
    nw = len(_WEIGHTS)

    def body(*refs):
        ins, outs, stages = refs[:nw], refs[nw:2 * nw], refs[2 * nw:3 * nw]
        send, recv, loc = refs[3 * nw:]
        me = _my_index()
        for (_, kind, valid, pad, _), src, stage in zip(_WEIGHTS, ins, stages):
            if pad != valid:
                stage[...] = jnp.zeros_like(stage)
            if kind == "col":
                stage[:, pl.ds(0, valid)] = _bf(src[...])
            else:
                stage[pl.ds(0, valid), :] = _bf(src[...])
        for k in range(N_DEV):
            @pl.when(me == k)
            def _():
                for w, (_, kind, _, pad, _) in enumerate(_WEIGHTS):
                    dst = _window(outs[w], kind, k * pad, pad)
                    pltpu.make_async_copy(stages[w], dst, loc.at[w]).start()
                    for peer in range(N_DEV):
                        if peer != k:
                            pltpu.make_async_remote_copy(
                                src_ref=stages[w], dst_ref=dst, send_sem=send.at[w],
                                recv_sem=recv.at[w], device_id=_device_tuple(peer),
                                device_id_type=MESH).start()
        for w, (_, kind, _, pad, _) in enumerate(_WEIGHTS):
            pltpu.make_async_copy(stages[w], _window(outs[w], kind, 0, pad), loc.at[w]).wait()
            seven = _window(outs[w], kind, 0, (N_DEV - 1) * pad)
            pltpu.make_async_remote_copy(
                src_ref=seven, dst_ref=seven, send_sem=send.at[w], recv_sem=recv.at[w],
                device_id=_device_tuple(0), device_id_type=MESH).wait()

    return pl.pallas_call(
        body, name="gather_weights",
        in_specs=[VMEM] * nw, out_specs=[ANY] * nw,
        out_shape=[_sds(_shard_shape(kind, N_DEV * pad, other), BF16)
                   for _, kind, _, pad, other in _WEIGHTS],
        scratch_shapes=[pltpu.VMEM(_shard_shape(kind, pad, other), BF16)
                        for _, kind, _, pad, other in _WEIGHTS]
        + [pltpu.SemaphoreType.DMA((nw,))] * 3,
        compiler_params=pltpu.CompilerParams(vmem_limit_bytes=VMEM_LIMIT_BYTES),
    )(*shards)


def _scatter_grads(grads):
    nw = len(_WEIGHTS)

    def body(*refs):
        ins, outs = refs[:nw], refs[nw:2 * nw]
        send, recv, loc = refs[2 * nw:]
        me = _my_index()
        for k in range(N_DEV):
            @pl.when(me == k)
            def _():
                for w, (_, kind, _, pad, _) in enumerate(_WEIGHTS):
                    pltpu.make_async_copy(_window(ins[w], kind, k * pad, pad),
                                          outs[w].at[k], loc.at[w]).start()

            @pl.when(me != k)
            def _():
                for w, (_, kind, _, pad, _) in enumerate(_WEIGHTS):
                    pltpu.make_async_remote_copy(
                        src_ref=_window(ins[w], kind, k * pad, pad), dst_ref=outs[w].at[me],
                        send_sem=send.at[w], recv_sem=recv.at[w],
                        device_id=_device_tuple(k), device_id_type=MESH).start()
        for w in range(nw):
            pltpu.make_async_copy(outs[w].at[0], outs[w].at[1], loc.at[w]).wait()
            seven = outs[w].at[pl.ds(0, N_DEV - 1)]
            pltpu.make_async_remote_copy(
                src_ref=seven, dst_ref=seven, send_sem=send.at[w], recv_sem=recv.at[w],
                device_id=_device_tuple(0), device_id_type=MESH).wait()

    return pl.pallas_call(
        body, name="scatter_grads",
        in_specs=[ANY] * nw, out_specs=[ANY] * nw,
        out_shape=[_sds((N_DEV,) + _shard_shape(kind, pad, other), BF16)
                   for _, kind, _, pad, other in _WEIGHTS],
        scratch_shapes=[pltpu.SemaphoreType.DMA((nw,))] * 3,
    )(*grads)


def _allreduce_small(small):
    shape = small.shape

    def body(in_ref, out_ref, gath, send, recv):
        me = _my_index()
        for k in range(N_DEV):
            @pl.when(me != k)
            def _():
                pltpu.make_async_remote_copy(
                    src_ref=in_ref, dst_ref=gath.at[me], send_sem=send, recv_sem=recv,
                    device_id=_device_tuple(k), device_id_type=MESH).start()

            @pl.when(me == k)
            def _():
                gath[k] = in_ref[...]
        seven = gath.at[pl.ds(0, N_DEV - 1)]
        pltpu.make_async_remote_copy(
            src_ref=seven, dst_ref=seven, send_sem=send, recv_sem=recv,
            device_id=_device_tuple(0), device_id_type=MESH).wait()
        total = gath[0]
        for s in range(1, N_DEV):
            total = total + gath[s]
        out_ref[...] = total

    return pl.pallas_call(
        body, name="allreduce_small",
        in_specs=[VMEM], out_specs=VMEM, out_shape=_sds(shape),
        scratch_shapes=[pltpu.VMEM((N_DEV,) + shape, F32),
                        pltpu.SemaphoreType.DMA, pltpu.SemaphoreType.DMA],
    )(small)


def _adamw(w, m, v, g, *, kind=None, name):
    shape = w.shape

    def body(w_ref, m_ref, v_ref, g_ref, grad_ref, delta_ref, nm_ref, nv_ref):
        if kind is None:
            grad = g_ref[...]
        else:
            valid = (slice(None), pl.ds(0, shape[1])) if kind == "col" else (pl.ds(0, shape[0]), slice(None))
            grad = g_ref[(0,) + valid].astype(F32)
            for s in range(1, N_DEV):
                grad = grad + g_ref[(s,) + valid].astype(F32)
        new_m = ADAM_B1 * m_ref[...] + (1.0 - ADAM_B1) * grad
        new_v = ADAM_B2 * v_ref[...] + (1.0 - ADAM_B2) * (grad * grad)
        m_hat = new_m / (1.0 - ADAM_B1 ** ADAM_STEP)
        v_hat = new_v / (1.0 - ADAM_B2 ** ADAM_STEP)
        grad_ref[...] = grad
        delta_ref[...] = -ADAM_LR * (m_hat / (jnp.sqrt(v_hat) + ADAM_EPS) + ADAM_WD * w_ref[...])
        nm_ref[...] = new_m
        nv_ref[...] = new_v

    return pl.pallas_call(
        body, name=name, in_specs=[VMEM] * 4, out_specs=[VMEM] * 4,
        out_shape=[_sds(shape)] * 4,
        compiler_params=pltpu.CompilerParams(vmem_limit_bytes=VMEM_LIMIT_BYTES),
    )(w, m, v, g)


_GAINS = ("ffn1_pre", "ffn1_post", "mix_pre", "mix_post", "ffn2_pre", "ffn2_post", "ple_post")
_SMALL_ROWS = 16


def _stack_gains(get):
    return jnp.concatenate([get(n) for n in _GAINS]
                           + [jnp.concatenate([get("out_sb"), get("out_ch")], axis=1)], axis=0)


def kernel(x, p, g_ffn1_pre, g_ffn1_post, w_ffn1_gate, w_ffn1_up, w_ffn1_down, g_mix_pre, g_mix_post, w_in, g_out_sb, g_out_ch, rel_bias, w_out, g_ffn2_pre, g_ffn2_post, w_ffn2_gate, w_ffn2_up, w_ffn2_down, w_ple_proj, w_ple_gate, g_ple_post, loss_target, m_g_ffn1_pre, m_g_ffn1_post, m_w_ffn1_gate, m_w_ffn1_up, m_w_ffn1_down, m_g_mix_pre, m_g_mix_post, m_w_in, m_g_out_sb, m_g_out_ch, m_rel_bias, m_w_out, m_g_ffn2_pre, m_g_ffn2_post, m_w_ffn2_gate, m_w_ffn2_up, m_w_ffn2_down, m_w_ple_proj, m_w_ple_gate, m_g_ple_post, v_g_ffn1_pre, v_g_ffn1_post, v_w_ffn1_gate, v_w_ffn1_up, v_w_ffn1_down, v_g_mix_pre, v_g_mix_post, v_w_in, v_g_out_sb, v_g_out_ch, v_rel_bias, v_w_out, v_g_ffn2_pre, v_g_ffn2_post, v_w_ffn2_gate, v_w_ffn2_up, v_w_ffn2_down, v_w_ple_proj, v_w_ple_gate, v_g_ple_post):
    given = dict(locals())
    wnames = [n for n, *_ in _WEIGHTS]
    wkey = lambda n: "rel_bias" if n == "rel_bias" else "w_" + n

    gathered = _gather_weights([given["w_" + n][0] for n in wnames])
    full = dict(zip(wnames, gathered))

    gains = {n: given["g_" + n] for n in _GAINS + ("out_sb", "out_ch")}
    fvec = _rel_bias_to_fvec(rel_bias[0])
    loss, dx, dw, dg, dfvec = _local_step(x[0], p[0, 0], loss_target[0], gains, full, fvec)

    recv = dict(zip(wnames, _scatter_grads([dw[n] for n in wnames])))
    dfv = jnp.pad(dfvec[:, 0, :], ((0, 0), (0, D_MODEL - CH_WIN)))
    small = _allreduce_small(jnp.concatenate([_stack_gains(lambda n: dg[n]), dfv], axis=0))

    results = {}
    for n, kind, *_ in _WEIGHTS:
        key = "w_" + n
        results[key] = _adamw(given[key][0], given["m_" + key][0], given["v_" + key][0],
                              recv[n], kind=kind, name="adamw_" + n)
    stacked = _adamw(_stack_gains(lambda n: given["g_" + n]),
                     _stack_gains(lambda n: given["m_g_" + n]),
                     _stack_gains(lambda n: given["v_g_" + n]),
                     small[:N_DEV], name="adamw_gains")
    half = D_MODEL // 2
    for r, n in enumerate(_GAINS):
        results["g_" + n] = [a[r:r + 1] for a in stacked]
    results["g_out_sb"] = [a[N_DEV - 1:N_DEV, :half] for a in stacked]
    results["g_out_ch"] = [a[N_DEV - 1:N_DEV, half:] for a in stacked]
    d_rel = _fvec_grad_to_rel_bias(small[N_DEV:, :CH_WIN].reshape(N_DEV, 1, CH_WIN))
    results["rel_bias"] = _adamw(rel_bias[0], m_rel_bias[0], v_rel_bias[0], d_rel,
                                 name="adamw_rel_bias")

    order = ("g_ffn1_pre", "g_ffn1_post", "w_ffn1_gate", "w_ffn1_up", "w_ffn1_down",
             "g_mix_pre", "g_mix_post", "w_in", "g_out_sb", "g_out_ch", "rel_bias", "w_out",
             "g_ffn2_pre", "g_ffn2_post", "w_ffn2_gate", "w_ffn2_up", "w_ffn2_down",
             "w_ple_proj", "w_ple_gate", "g_ple_post")

    def leaf(name, idx):
        a = results[name][idx]
        return a if name.startswith("g_") else a[None]

    total_loss = lax.psum(loss[0, 0], ("x", "y", "c"))
    return (total_loss, dx[None],
            *[leaf(n, 0) for n in order], *[leaf(n, 1) for n in order],
            *[leaf(n, 2) for n in order], *[leaf(n, 3) for n in order])
```

```python
import functools

import jax
import jax.numpy as jnp
from jax import lax
from jax.experimental import pallas as pl
from jax.experimental.pallas import tpu as pltpu

F32 = jnp.float32
BF16 = jnp.bfloat16

N_DEV = 8
D_MODEL = 1024
D_FF = 2816
FF_SHARD = D_FF // N_DEV
FF_SHARD_PAD = 384
D_FF_PAD = FF_SHARD_PAD * N_DEV
QKV_WIDTH = 3 * D_MODEL
QKV_SHARD = QKV_WIDTH // N_DEV
PLE_DIM = 256
ROW_SHARD = D_MODEL // N_DEV
HEAD_DIM = 64
PAIR = 2 * HEAD_DIM
N_PAIRS = 4
CHUNK = 64
LOOKBACK = 8
REL_CLIP = 128
N_REL = 2 * REL_CLIP + 1
CH_QB = 256
CH_LOOK = LOOKBACK * CHUNK
CH_WIN = CH_LOOK + CH_QB
SB_BLK = 256
EPS = 1e-6
NEG_INF = -1e30
ATT_SCALE = HEAD_DIM ** -0.5
ADAM_LR = 0.001
ADAM_B1 = 0.9
ADAM_B2 = 0.999
ADAM_EPS = 1e-08
ADAM_WD = 0.01
ADAM_STEP = 10
VMEM_LIMIT_BYTES = 48 * 1024 * 1024
MESH = pl.DeviceIdType.MESH

ANY = pl.BlockSpec(memory_space=pl.ANY)
VMEM = pl.BlockSpec(memory_space=pltpu.VMEM)


def _params(*sem):
    return pltpu.CompilerParams(dimension_semantics=sem or None,
                                vmem_limit_bytes=VMEM_LIMIT_BYTES)


def _sds(shape, dtype=F32):
    return jax.ShapeDtypeStruct(shape, dtype)


def _bf(x):
    return x.astype(BF16)


def _dot(a, b):
    return jnp.dot(_bf(a), _bf(b), preferred_element_type=F32)


def _dot_nt(a, b):
    return lax.dot_general(_bf(a), _bf(b), (((1,), (1,)), ((), ())),
                           preferred_element_type=F32)


def _dot_tn(a, b):
    return lax.dot_general(_bf(a), _bf(b), (((0,), (0,)), ((), ())),
                           preferred_element_type=F32)


def _sigmoid(x):
    return 1.0 / (1.0 + jnp.exp(-x))


def _softplus(x):
    return jnp.maximum(x, 0.0) + jnp.log(1.0 + jnp.exp(-jnp.abs(x)))


def _rstd(x):
    return lax.rsqrt(jnp.mean(x * x, axis=-1, keepdims=True) + EPS)


def _rms(x, g):
    return x * _rstd(x) * g


def _rms_bwd(dy, x, g):
    r = _rstd(x)
    w = dy * g
    dx = r * (w - x * (r * r) * jnp.mean(w * x, axis=-1, keepdims=True))
    dg = jnp.sum(dy * (x * r), axis=0, keepdims=True)
    return dx, dg


def _dot_exact01(x, u):
    hi = _bf(x)
    lo = _bf(x - hi.astype(F32))
    return (jnp.dot(hi, u, preferred_element_type=F32)
            + jnp.dot(lo, u, preferred_element_type=F32))


def _head_masks():
    lane = lax.broadcasted_iota(jnp.int32, (1, PAIR), 1)
    return lane < HEAD_DIM, lane >= HEAD_DIM


def _ffn_fwd(x, g_pre, g_post, wg, wu, wd, *, name):
    t = x.shape[0]
    tm, tj = 512, 512
    ni, nj = t // tm, D_FF_PAD // tj

    def body(x_ref, gpre_ref, gpost_ref, wg_ref, wu_ref, wd_ref,
             h_ref, n_ref, a_ref, b_ref, f_ref, acc_ref):
        j = pl.program_id(1)

        @pl.when(j == 0)
        def _():
            n_ref[...] = _bf(_rms(x_ref[...], gpre_ref[...]))
            acc_ref[...] = jnp.zeros_like(acc_ref)

        n = n_ref[...]
        a = _dot_nt(n, wg_ref[...])
        b = _dot_nt(n, wu_ref[...])
        a_ref[...] = a
        b_ref[...] = b
        hmid = a * _sigmoid(a) * b
        acc_ref[...] += jnp.dot(_bf(hmid), wd_ref[...], preferred_element_type=F32)

        @pl.when(j == nj - 1)
        def _():
            f = acc_ref[...]
            f_ref[...] = f
            h_ref[...] = x_ref[...] + 0.5 * _rms(f, gpost_ref[...])

    row = pl.BlockSpec((tm, D_MODEL), lambda i, j: (i, 0))
    gain = pl.BlockSpec((1, D_MODEL), lambda i, j: (0, 0))
    col = pl.BlockSpec((tm, tj), lambda i, j: (i, j))
    wtile = pl.BlockSpec((tj, D_MODEL), lambda i, j: (j, 0))
    return pl.pallas_call(
        body, name=name, grid=(ni, nj),
        in_specs=[row, gain, gain, wtile, wtile, wtile],
        out_specs=[row, row, col, col, row],
        out_shape=[_sds((t, D_MODEL)), _sds((t, D_MODEL), BF16),
                   _sds((t, D_FF_PAD)), _sds((t, D_FF_PAD)), _sds((t, D_MODEL))],
        scratch_shapes=[pltpu.VMEM((tm, D_MODEL), F32)],
        compiler_params=_params("arbitrary", "arbitrary"),
    )(x, g_pre, g_post, wg, wu, wd)


def _ffn_bwd(n, df, a, b, wg, wu, wd, *, name):
    t = n.shape[0]
    tj, ts = 256, 512
    nj, ns = D_FF_PAD // tj, t // ts

    def body(n_hbm, df_hbm, a_ref, b_ref, wg_ref, wu_ref, wd_ref,
             dwg_ref, dwu_ref, dwd_ref, dn_hbm,
             n_v, df_v, dn_v, ag, au, ad, sem):
        j = pl.program_id(0)

        @pl.when(j == 0)
        def _():
            c1 = pltpu.make_async_copy(n_hbm, n_v, sem.at[0])
            c2 = pltpu.make_async_copy(df_hbm, df_v, sem.at[1])
            c1.start()
            c2.start()
            dn_v[...] = jnp.zeros_like(dn_v)
            c1.wait()
            c2.wait()

        ag[...] = jnp.zeros_like(ag)
        au[...] = jnp.zeros_like(au)
        ad[...] = jnp.zeros_like(ad)
        wgj, wuj, wdj = wg_ref[...], wu_ref[...], wd_ref[...]
        for s in range(ns):
            rows = pl.ds(s * ts, ts)
            av, bv = a_ref[rows, :], b_ref[rows, :]
            sig = _sigmoid(av)
            silu = av * sig
            dfr = df_v[rows, :]
            nr = n_v[rows, :]
            dhmid = _dot_nt(dfr, wdj)
            da = dhmid * bv * (sig * (1.0 + av * (1.0 - sig)))
            db = dhmid * silu
            ad[...] += _dot_tn(silu * bv, dfr)
            ag[...] += _dot_tn(da, nr)
            au[...] += _dot_tn(db, nr)
            dn_v[rows, :] += _dot(da, wgj) + _dot(db, wuj)
        dwg_ref[...] = _bf(ag[...])
        dwu_ref[...] = _bf(au[...])
        dwd_ref[...] = _bf(ad[...])

        @pl.when(j == nj - 1)
        def _():
            c = pltpu.make_async_copy(dn_v, dn_hbm, sem.at[0])
            c.start()
            c.wait()

    roww = pl.BlockSpec((tj, D_MODEL), lambda j: (j, 0))
    act = pl.BlockSpec((t, tj), lambda j: (0, j))
    return pl.pallas_call(
        body, name=name, grid=(nj,),
        in_specs=[ANY, ANY, act, act, roww, roww, roww],
        out_specs=[roww, roww, roww, ANY],
        out_shape=[_sds((D_FF_PAD, D_MODEL), BF16)] * 3 + [_sds((t, D_MODEL))],
        scratch_shapes=[pltpu.VMEM((t, D_MODEL), BF16), pltpu.VMEM((t, D_MODEL), BF16),
                        pltpu.VMEM((t, D_MODEL), F32)]
        + [pltpu.VMEM((tj, D_MODEL), F32)] * 3 + [pltpu.SemaphoreType.DMA((2,))],
        compiler_params=_params("arbitrary"),
    )(n, df, a, b, wg, wu, wd)


def _junction(dres, pre=None, post=None, *, name):
    t = dres.shape[0]
    tm = 512
    ni = t // tm
    n_in = 1 + (3 if pre else 0) + (2 if post else 0)
    coef = post[2] if post else None

    def body(*refs):
        ins, outs = list(refs[:n_in]), list(refs[n_in:])
        i = pl.program_id(0)
        dh = ins.pop(0)[...]
        if pre:
            dn_ref, x_ref, gpre_ref = ins.pop(0), ins.pop(0), ins.pop(0)
            dh_ref, dgpre_ref = outs.pop(0), outs.pop(0)
            dx, dg = _rms_bwd(dn_ref[...], x_ref[...], gpre_ref[...])
            dh = dh + dx
            dh_ref[...] = dh

            @pl.when(i == 0)
            def _():
                dgpre_ref[...] = jnp.zeros_like(dgpre_ref)
            dgpre_ref[...] += dg
        if post:
            f_ref, gpost_ref = ins.pop(0), ins.pop(0)
            df_ref, dgpost_ref = outs.pop(0), outs.pop(0)
            df, dg = _rms_bwd(coef * dh, f_ref[...], gpost_ref[...])
            df_ref[...] = _bf(df)

            @pl.when(i == 0)
            def _():
                dgpost_ref[...] = jnp.zeros_like(dgpost_ref)
            dgpost_ref[...] += dg

    row = pl.BlockSpec((tm, D_MODEL), lambda i: (i, 0))
    gain = pl.BlockSpec((1, D_MODEL), lambda i: (0, 0))
    args, in_specs, out_specs, out_shape = [dres], [row], [], []
    if pre:
        args += list(pre)
        in_specs += [row, row, gain]
        out_specs += [row, gain]
        out_shape += [_sds((t, D_MODEL)), _sds((1, D_MODEL))]
    if post:
        args += [post[0], post[1]]
        in_specs += [row, gain]
        out_specs += [row, gain]
        out_shape += [_sds((t, D_MODEL), BF16), _sds((1, D_MODEL))]
    return pl.pallas_call(
        body, name=name, grid=(ni,), in_specs=in_specs, out_specs=out_specs,
        out_shape=out_shape, compiler_params=_params("arbitrary"),
    )(*args)


def _qkv_fwd(h, g, win, *, name):
    t = h.shape[0]
    tm, tn = 512, 768
    ni, nj = t // tm, QKV_WIDTH // tn

    def body(h_ref, g_ref, w_ref, qkv_ref, u_ref):
        @pl.when(pl.program_id(1) == 0)
        def _():
            u_ref[...] = _bf(_rms(h_ref[...], g_ref[...]))
        qkv_ref[...] = jnp.dot(u_ref[...], w_ref[...], preferred_element_type=F32)

    row = pl.BlockSpec((tm, D_MODEL), lambda i, j: (i, 0))
    return pl.pallas_call(
        body, name=name, grid=(ni, nj),
        in_specs=[row, pl.BlockSpec((1, D_MODEL), lambda i, j: (0, 0)),
                  pl.BlockSpec((D_MODEL, tn), lambda i, j: (0, j))],
        out_specs=[pl.BlockSpec((tm, tn), lambda i, j: (i, j)), row],
        out_shape=[_sds((t, QKV_WIDTH)), _sds((t, D_MODEL), BF16)],
        compiler_params=_params("arbitrary", "arbitrary"),
    )(h, g, win)


def _qkv_bwd(dq, dk, dv, u, win, *, name):
    t = u.shape[0]
    tn, ts = 512, 512
    nj, ns = QKV_WIDTH // tn, t // ts

    def body(dq_ref, dk_ref, dv_ref, u_ref, w_ref, dw_ref, du_hbm, du_v, acc_ref, sem):
        j = pl.program_id(0)

        @pl.when(j == 0)
        def _():
            du_v[...] = jnp.zeros_like(du_v)

        wj = w_ref[...]
        for role, d_ref in enumerate((dq_ref, dk_ref, dv_ref)):
            @pl.when(j % 3 == role)
            def _():
                acc_ref[...] = jnp.zeros_like(acc_ref)
                for s in range(ns):
                    rows = pl.ds(s * ts, ts)
                    dcol = d_ref[rows, :]
                    acc_ref[...] += _dot_tn(u_ref[rows, :], dcol)
                    du_v[rows, :] += _dot_nt(dcol, wj)
                dw_ref[...] = _bf(acc_ref[...])

        @pl.when(j == nj - 1)
        def _():
            c = pltpu.make_async_copy(du_v, du_hbm, sem)
            c.start()
            c.wait()

    colw = pl.BlockSpec((D_MODEL, tn), lambda j: (0, j))
    grp = pl.BlockSpec((t, tn), lambda j: (0, j // 3))
    return pl.pallas_call(
        body, name=name, grid=(nj,),
        in_specs=[grp, grp, grp, pl.BlockSpec((t, D_MODEL), lambda j: (0, 0)), colw],
        out_specs=[colw, ANY],
        out_shape=[_sds((D_MODEL, QKV_WIDTH), BF16), _sds((t, D_MODEL))],
        scratch_shapes=[pltpu.VMEM((t, D_MODEL), F32), pltpu.VMEM((D_MODEL, tn), F32),
                        pltpu.SemaphoreType.DMA],
        compiler_params=_params("arbitrary"),
    )(dq, dk, dv, u, win)


def _sb_tile(qm, kj, q0, k0):
    z = _dot_nt(qm, kj) * ATT_SCALE
    rows = q0 + lax.broadcasted_iota(jnp.int32, z.shape, 0)
    cols = k0 + lax.broadcasted_iota(jnp.int32, z.shape, 1)
    return z, cols < rows, _softplus(z)


def _tri(n, inclusive):
    r = lax.broadcasted_iota(jnp.int32, (n, n), 0)
    c = lax.broadcasted_iota(jnp.int32, (n, n), 1)
    return jnp.where((r >= c) if inclusive else (r > c), 1.0, 0.0).astype(BF16)


def _sb_fwd(qkv, *, name):
    t = qkv.shape[0]
    blk = SB_BLK
    ni = t // blk

    def body(q_ref, k_ref, v_ref, o_ref, ltot_ref):
        i = pl.program_id(1)
        q0 = i * blk
        u_strict = _tri(blk, False)
        q = q_ref[...]
        outs, totals = [], []
        for hm in _head_masks():
            qm = jnp.where(hm, q, 0.0)

            def step(jj, carry):
                acc, c_l = carry
                k0 = pl.multiple_of((i - jj) * blk, blk)
                kj = k_ref[pl.ds(k0, blk), :]
                vj = v_ref[pl.ds(k0, blk), :]
                z, mask, sp = _sb_tile(qm, kj, q0, k0)
                lf = jnp.where(mask, -sp, 0.0)
                later = _dot_exact01(lf, u_strict) + c_l
                a = jnp.where(mask, jnp.exp(z - sp + later), 0.0)
                acc = acc + _dot(a, vj)
                return acc, c_l + jnp.sum(lf, axis=1, keepdims=True)

            acc, c_l = lax.fori_loop(
                0, i + 1, step,
                (jnp.zeros((blk, PAIR), F32), jnp.zeros((blk, 1), F32)))
            outs.append(acc)
            totals.append(c_l)
        lo = _head_masks()[0]
        o_ref[...] = jnp.where(lo, outs[0], outs[1])
        ltot_ref[...] = jnp.where(lo, totals[0], totals[1])

    blkspec = pl.BlockSpec((blk, PAIR), lambda p, i: (i, p))
    return pl.pallas_call(
        body, name=name, grid=(N_PAIRS, ni),
        in_specs=[blkspec,
                  pl.BlockSpec((t, PAIR), lambda p, i: (0, N_PAIRS + p)),
                  pl.BlockSpec((t, PAIR), lambda p, i: (0, 2 * N_PAIRS + p))],
        out_specs=[blkspec, blkspec],
        out_shape=[_sds((t, D_MODEL)), _sds((t, D_MODEL // 2))],
        compiler_params=_params("arbitrary", "arbitrary"),
    )(qkv, qkv, qkv)


def _sb_bwd(qkv, ltot, do, *, name):
    t = qkv.shape[0]
    blk = SB_BLK
    ni = t // blk

    def body(q_ref, k_ref, v_ref, lt_ref, do_ref, dq_ref, dkout_ref, dvout_ref, dk_ref, dv_ref):
        i = pl.program_id(1)
        q0 = i * blk

        @pl.when(i == 0)
        def _():
            dk_ref[...] = jnp.zeros_like(dk_ref)
            dv_ref[...] = jnp.zeros_like(dv_ref)

        r = lax.broadcasted_iota(jnp.int32, (blk, blk), 0)
        c = lax.broadcasted_iota(jnp.int32, (blk, blk), 1)
        u_upto = jnp.where(r <= c, 1.0, 0.0).astype(BF16)
        u_before = jnp.where(r < c, 1.0, 0.0).astype(BF16)
        lane = lax.broadcasted_iota(jnp.int32, (1, PAIR), 1)
        q, lt_blk, do_blk = q_ref[...], lt_ref[...], do_ref[...]
        dqs = []
        for h, hm in enumerate(_head_masks()):
            qm = _bf(jnp.where(hm, q, 0.0))
            dom = _bf(jnp.where(hm, do_blk, 0.0))
            total = jnp.sum(jnp.where(lane == h * HEAD_DIM, lt_blk, 0.0), axis=1, keepdims=True)

            def step(j, carry):
                dq_acc, c_l, c_g = carry
                k0 = pl.multiple_of(j * blk, blk)
                krows = pl.ds(k0, blk)
                kj = k_ref[krows, :]
                vj = v_ref[krows, :]
                z, mask, sp = _sb_tile(qm, kj, q0, k0)
                sig = jnp.exp(z - sp)
                lf = jnp.where(mask, -sp, 0.0)
                later = total - (_dot_exact01(lf, u_upto) + c_l)
                a = jnp.where(mask, jnp.exp(z - sp + later), 0.0)
                g = a * _dot_nt(dom, vj)
                g_before = _dot_exact01(g, u_before) + c_g
                dz = jnp.where(mask, g * (1.0 - sig) - g_before * sig, 0.0) * ATT_SCALE
                dq_acc = dq_acc + _dot(dz, kj)
                dk_ref[krows, :] += _dot_tn(dz, qm)
                dv_ref[krows, :] += _dot_tn(a, dom)
                return (dq_acc, c_l + jnp.sum(lf, axis=1, keepdims=True),
                        c_g + jnp.sum(g, axis=1, keepdims=True))

            dq_acc, _, _ = lax.fori_loop(
                0, i + 1, step,
                (jnp.zeros((blk, PAIR), F32), jnp.zeros((blk, 1), F32),
                 jnp.zeros((blk, 1), F32)))
            dqs.append(dq_acc)
        dq_ref[...] = _bf(jnp.where(_head_masks()[0], dqs[0], dqs[1]))

        @pl.when(i == ni - 1)
        def _():
            dkout_ref[...] = _bf(dk_ref[...])
            dvout_ref[...] = _bf(dv_ref[...])

    blkspec = lambda off: pl.BlockSpec((blk, PAIR), lambda p, i: (i, off + p))
    full = lambda off: pl.BlockSpec((t, PAIR), lambda p, i: (0, off + p))
    return pl.pallas_call(
        body, name=name, grid=(N_PAIRS, ni),
        in_specs=[blkspec(0), full(N_PAIRS), full(2 * N_PAIRS), blkspec(0), blkspec(0)],
        out_specs=[blkspec(0), full(0), full(0)],
        out_shape=[_sds((t, D_MODEL), BF16)] * 3,
        scratch_shapes=[pltpu.VMEM((t, PAIR), F32), pltpu.VMEM((t, PAIR), F32)],
        compiler_params=_params("arbitrary", "arbitrary"),
    )(qkv, qkv, qkv, ltot, do)


def _ch_mask(i):
    r = lax.broadcasted_iota(jnp.int32, (CH_QB, CH_WIN), 0)
    c = lax.broadcasted_iota(jnp.int32, (CH_QB, CH_WIN), 1)
    qc = LOOKBACK + lax.shift_right_arithmetic(r, 6)
    kc = lax.shift_right_arithmetic(c, 6)
    first = i * (CH_QB // CHUNK) - LOOKBACK
    return (kc <= qc) & (kc >= qc - LOOKBACK) & (kc + first >= 0)


def _ch_probs(qm, kw, bias_h, mask):
    z = _dot_nt(qm, kw) * ATT_SCALE + bias_h
    z = jnp.where(mask, z, NEG_INF)
    e = jnp.exp(z - jnp.max(z, axis=1, keepdims=True))
    return e / jnp.sum(e, axis=1, keepdims=True)


def _ch_fill(pad_ref, src_ref, t):
    pad_ref[pl.ds(0, CH_LOOK), :] = jnp.zeros((CH_LOOK, PAIR), BF16)
    pad_ref[pl.ds(CH_LOOK, t), :] = _bf(src_ref[...])


def _ch_fwd(qkv, bias, o_in, *, name):
    t = qkv.shape[0]
    ni = t // CH_QB

    def body(q_ref, k_ref, v_ref, bias_ref, _alias, o_ref, kpad, vpad):
        i = pl.program_id(1)

        @pl.when(i == 0)
        def _():
            _ch_fill(kpad, k_ref, t)
            _ch_fill(vpad, v_ref, t)

        win = pl.ds(pl.multiple_of(i * CH_QB, CH_QB), CH_WIN)
        kw, vw = kpad[win, :], vpad[win, :]
        mask = _ch_mask(i)
        q = q_ref[...]
        outs = []
        for h, hm in enumerate(_head_masks()):
            p = _ch_probs(jnp.where(hm, q, 0.0), kw, bias_ref[h], mask)
            outs.append(_dot(p, vw))
        o_ref[...] = jnp.where(_head_masks()[0], outs[0], outs[1])

    full = lambda off: pl.BlockSpec((t, PAIR), lambda p, i: (0, off + p))
    return pl.pallas_call(
        body, name=name, grid=(N_PAIRS, ni),
        in_specs=[pl.BlockSpec((CH_QB, PAIR), lambda p, i: (i, 3 * N_PAIRS + p)),
                  full(4 * N_PAIRS), full(5 * N_PAIRS),
                  pl.BlockSpec((2, CH_QB, CH_WIN), lambda p, i: (p, 0, 0)), ANY],
        out_specs=pl.BlockSpec((CH_QB, PAIR), lambda p, i: (i, N_PAIRS + p)),
        out_shape=_sds((t, D_MODEL)),
        scratch_shapes=[pltpu.VMEM((t + CH_LOOK, PAIR), BF16)] * 2,
        input_output_aliases={4: 0},
        compiler_params=_params("arbitrary", "arbitrary"),
    )(qkv, qkv, qkv, bias, o_in)


def _ch_bwd(qkv, bias, o, do, dq_in, dk_in, dv_in, *, name):
    t = qkv.shape[0]
    ni = t // CH_QB

    def body(q_ref, k_ref, v_ref, bias_ref, o_ref, do_ref, _a0, _a1, _a2,
             dq_ref, dkout_ref, dvout_ref, dbias_ref, kpad, vpad, dkpad, dvpad):
        i = pl.program_id(1)

        @pl.when(i == 0)
        def _():
            _ch_fill(kpad, k_ref, t)
            _ch_fill(vpad, v_ref, t)
            dkpad[...] = jnp.zeros_like(dkpad)
            dvpad[...] = jnp.zeros_like(dvpad)
            dbias_ref[...] = jnp.zeros_like(dbias_ref)

        win = pl.ds(pl.multiple_of(i * CH_QB, CH_QB), CH_WIN)
        kw, vw = kpad[win, :], vpad[win, :]
        mask = _ch_mask(i)
        q, o_blk, do_blk = q_ref[...], o_ref[...], do_ref[...]
        dqs = []
        for h, hm in enumerate(_head_masks()):
            qm = _bf(jnp.where(hm, q, 0.0))
            dom = jnp.where(hm, do_blk, 0.0)
            delta = jnp.sum(dom * o_blk, axis=1, keepdims=True)
            dom = _bf(dom)
            p = _ch_probs(qm, kw, bias_ref[h], mask)
            ds = p * (_dot_nt(dom, vw) - delta)
            dbias_ref[h] += ds
            dsz = ds * ATT_SCALE
            dqs.append(_dot(dsz, kw))
            dkpad[win, :] += _dot_tn(dsz, qm)
            dvpad[win, :] += _dot_tn(p, dom)
        dq_ref[...] = _bf(jnp.where(_head_masks()[0], dqs[0], dqs[1]))

        @pl.when(i == ni - 1)
        def _():
            dkout_ref[...] = _bf(dkpad[pl.ds(CH_LOOK, t), :])
            dvout_ref[...] = _bf(dvpad[pl.ds(CH_LOOK, t), :])

    blkspec = lambda off: pl.BlockSpec((CH_QB, PAIR), lambda p, i: (i, off + p))
    full = lambda off: pl.BlockSpec((t, PAIR), lambda p, i: (0, off + p))
    bias_spec = pl.BlockSpec((2, CH_QB, CH_WIN), lambda p, i: (p, 0, 0))
    return pl.pallas_call(
        body, name=name, grid=(N_PAIRS, ni),
        in_specs=[blkspec(3 * N_PAIRS), full(4 * N_PAIRS), full(5 * N_PAIRS), bias_spec,
                  blkspec(N_PAIRS), blkspec(N_PAIRS), ANY, ANY, ANY],
        out_specs=[blkspec(N_PAIRS), full(N_PAIRS), full(N_PAIRS), bias_spec],
        out_shape=[_sds((t, D_MODEL), BF16)] * 3 + [_sds((2 * N_PAIRS, CH_QB, CH_WIN))],
        scratch_shapes=[pltpu.VMEM((t + CH_LOOK, PAIR), BF16)] * 2
        + [pltpu.VMEM((t + CH_LOOK, PAIR), F32)] * 2,
        input_output_aliases={6: 0, 7: 1, 8: 2},
        compiler_params=_params("arbitrary", "arbitrary"),
    )(qkv, qkv, qkv, bias, o, do, dq_in, dk_in, dv_in)


def _bias_expand(fvec, *, name):
    n_heads = fvec.shape[0]

    def body(f_ref, o_ref, rows8):
        row = f_ref[0]
        for r in range(8):
            rows8[pl.ds(r, 1), :] = pltpu.roll(row, r, 1)
        base = rows8[...]
        for blk in range(CH_QB // 8):
            o_ref[0, pl.ds(8 * blk, 8), :] = pltpu.roll(base, 8 * blk, 1)

    return pl.pallas_call(
        body, name=name, grid=(n_heads,),
        in_specs=[pl.BlockSpec((1, 1, CH_WIN), lambda h: (h, 0, 0))],
        out_specs=pl.BlockSpec((1, CH_QB, CH_WIN), lambda h: (h, 0, 0)),
        out_shape=_sds((n_heads, CH_QB, CH_WIN)),
        scratch_shapes=[pltpu.VMEM((8, CH_WIN), F32)],
        compiler_params=_params("arbitrary"),
    )(fvec)


def _bias_grad(dbias, *, name):
    n_heads = dbias.shape[0]
    first = CH_LOOK - REL_CLIP

    def body(d_ref, o_ref, acc8):
        acc = jnp.zeros((8, CH_WIN), F32)
        for blk in range(CH_QB // 8):
            acc = acc + pltpu.roll(d_ref[0, pl.ds(8 * blk, 8), :], (CH_WIN - 8 * blk) % CH_WIN, 1)
        acc8[...] = acc
        dvec = jnp.zeros((1, CH_WIN), F32)
        for r in range(8):
            dvec = dvec + pltpu.roll(acc8[pl.ds(r, 1), :], (CH_WIN - r) % CH_WIN, 1)
        lane = lax.broadcasted_iota(jnp.int32, (1, CH_WIN), 1)
        clipped = (lane <= first) | (lane >= first + REL_CLIP + CHUNK)
        total = jnp.sum(jnp.where(clipped, dvec, 0.0), axis=1, keepdims=True)
        o_ref[0] = jnp.where(lane == first, total, dvec)

    return pl.pallas_call(
        body, name=name, grid=(n_heads,),
        in_specs=[pl.BlockSpec((1, CH_QB, CH_WIN), lambda h: (h, 0, 0))],
        out_specs=pl.BlockSpec((1, 1, CH_WIN), lambda h: (h, 0, 0)),
        out_shape=_sds((n_heads, 1, CH_WIN)),
        scratch_shapes=[pltpu.VMEM((8, CH_WIN), F32)],
        compiler_params=_params("arbitrary"),
    )(dbias)


def _out_fwd(o, h1, g_sb, g_ch, g_post, wout, *, name):
    t = o.shape[0]
    tm = 512
    half = D_MODEL // 2

    def body(o_ref, h_ref, gsb_ref, gch_ref, gpost_ref, w_ref, h2_ref, mixed_ref, y_ref):
        ov = o_ref[...]
        mixed = jnp.concatenate([_rms(ov[:, :half], gsb_ref[...]),
                                 _rms(ov[:, half:], gch_ref[...])], axis=1)
        mixed_ref[...] = _bf(mixed)
        y = _dot(mixed, w_ref[...])
        y_ref[...] = y
        h2_ref[...] = h_ref[...] + _rms(y, gpost_ref[...])

    row = pl.BlockSpec((tm, D_MODEL), lambda i: (i, 0))
    gain = lambda n: pl.BlockSpec((1, n), lambda i: (0, 0))
    return pl.pallas_call(
        body, name=name, grid=(t // tm,),
        in_specs=[row, row, gain(half), gain(half), gain(D_MODEL),
                  pl.BlockSpec((D_MODEL, D_MODEL), lambda i: (0, 0))],
        out_specs=[row, row, row],
        out_shape=[_sds((t, D_MODEL)), _sds((t, D_MODEL), BF16), _sds((t, D_MODEL))],
        compiler_params=_params("arbitrary"),
    )(o, h1, g_sb, g_ch, g_post, wout)


def _out_bwd(dy, mixed, o, g_sb, g_ch, wout, *, name):
    t = o.shape[0]
    tm = 512
    ni = t // tm
    half = D_MODEL // 2

    def body(dy_ref, mixed_ref, o_ref, gsb_ref, gch_ref, w_ref,
             dw_ref, do_ref, dgsb_ref, dgch_ref, acc_ref):
        i = pl.program_id(0)

        @pl.when(i == 0)
        def _():
            acc_ref[...] = jnp.zeros_like(acc_ref)
            dgsb_ref[...] = jnp.zeros_like(dgsb_ref)
            dgch_ref[...] = jnp.zeros_like(dgch_ref)

        dyv = dy_ref[...]
        acc_ref[...] += _dot_tn(mixed_ref[...], dyv)
        dm = _dot_nt(dyv, w_ref[...])
        ov = o_ref[...]
        doa, dga = _rms_bwd(dm[:, :half], ov[:, :half], gsb_ref[...])
        dob, dgb = _rms_bwd(dm[:, half:], ov[:, half:], gch_ref[...])
        do_ref[...] = jnp.concatenate([doa, dob], axis=1)
        dgsb_ref[...] += dga
        dgch_ref[...] += dgb

        @pl.when(i == ni - 1)
        def _():
            dw_ref[...] = _bf(acc_ref[...])

    row = pl.BlockSpec((tm, D_MODEL), lambda i: (i, 0))
    gain = pl.BlockSpec((1, half), lambda i: (0, 0))
    sq = pl.BlockSpec((D_MODEL, D_MODEL), lambda i: (0, 0))
    return pl.pallas_call(
        body, name=name, grid=(ni,),
        in_specs=[row, row, row, gain, gain, sq],
        out_specs=[sq, row, gain, gain],
        out_shape=[_sds((D_MODEL, D_MODEL), BF16), _sds((t, D_MODEL)),
                   _sds((1, half)), _sds((1, half))],
        scratch_shapes=[pltpu.VMEM((D_MODEL, D_MODEL), F32)],
        compiler_params=_params("arbitrary"),
    )(dy, mixed, o, g_sb, g_ch, wout)


def _ple(p, h3, target, wp, wgate, g, *, name):
    t = h3.shape[0]
    tm = 512
    ni = t // tm

    def body(p_ref, h_ref, tgt_ref, wp_ref, wg_ref, g_ref,
             loss_ref, dres_ref, dwp_ref, dwg_ref, dg_ref, accp, accg):
        i = pl.program_id(0)

        @pl.when(i == 0)
        def _():
            loss_ref[...] = jnp.zeros_like(loss_ref)
            dg_ref[...] = jnp.zeros_like(dg_ref)
            accp[...] = jnp.zeros_like(accp)
            accg[...] = jnp.zeros_like(accg)

        pv, hv, gv = p_ref[...], h_ref[...], g_ref[...]
        pe = _dot(pv, wp_ref[...])
        sig = _sigmoid(_dot(hv, wg_ref[...]))
        e = pe * sig
        err = hv + _rms(e, gv) - tgt_ref[...]
        tok = jnp.mean(err * err, axis=-1, keepdims=True)
        loss_ref[...] += 0.5 * jnp.sum(tok, axis=0, keepdims=True)
        dh4 = err * (1.0 / D_MODEL)
        de, dg = _rms_bwd(dh4, e, gv)
        dg_ref[...] += dg
        dpe = de * sig
        dgt = de * pe * sig * (1.0 - sig)
        accp[...] += _dot_tn(pv, dpe)
        accg[...] += _dot_tn(hv, dgt)
        dres_ref[...] = dh4 + _dot_nt(dgt, wg_ref[...])

        @pl.when(i == ni - 1)
        def _():
            dwp_ref[...] = _bf(accp[...])
            dwg_ref[...] = _bf(accg[...])

    row = pl.BlockSpec((tm, D_MODEL), lambda i: (i, 0))
    const = lambda r, c: pl.BlockSpec((r, c), lambda i: (0, 0))
    return pl.pallas_call(
        body, name=name, grid=(ni,),
        in_specs=[pl.BlockSpec((tm, PLE_DIM), lambda i: (i, 0)), row, row,
                  const(PLE_DIM, D_MODEL), const(D_MODEL, D_MODEL), const(1, D_MODEL)],
        out_specs=[const(1, 128), row, const(PLE_DIM, D_MODEL), const(D_MODEL, D_MODEL),
                   const(1, D_MODEL)],
        out_shape=[_sds((1, 128)), _sds((t, D_MODEL)), _sds((PLE_DIM, D_MODEL), BF16),
                   _sds((D_MODEL, D_MODEL), BF16), _sds((1, D_MODEL))],
        scratch_shapes=[pltpu.VMEM((PLE_DIM, D_MODEL), F32), pltpu.VMEM((D_MODEL, D_MODEL), F32)],
        compiler_params=_params("arbitrary"),
    )(p, h3, target, wp, wgate, g)


def _rel_bias_to_fvec(rel_bias):
    rev = rel_bias[:, ::-1]
    n_heads = rel_bias.shape[0]
    first = CH_LOOK - REL_CLIP
    n_var = REL_CLIP + CHUNK
    clipped = rev[:, :1]
    fvec = jnp.concatenate([jnp.broadcast_to(clipped, (n_heads, first)), rev[:, :n_var],
                            jnp.broadcast_to(clipped, (n_heads, CH_WIN - first - n_var))], axis=1)
    return fvec.reshape(n_heads, 1, CH_WIN)


def _fvec_grad_to_rel_bias(dfvec):
    first = CH_LOOK - REL_CLIP
    n_var = REL_CLIP + CHUNK
    rev = jnp.pad(dfvec[:, 0, first:first + n_var], ((0, 0), (0, N_REL - n_var)))
    return rev[:, ::-1]


def _local_step(x, p, target, g, weights_for, grads_done, fvec):
    w = dict(weights_for(0, x))
    h1, n1, a1, b1, f1 = _ffn_fwd(x, g["ffn1_pre"], g["ffn1_post"],
                                  w["ffn1_gate"], w["ffn1_up"], w["ffn1_down"], name="ffn1_fwd")
    w.update(weights_for(1, h1))
    qkv, u = _qkv_fwd(h1, g["mix_pre"], w["in"], name="qkv_fwd")
    bias = _bias_expand(fvec, name="bias_expand")
    o, ltot = _sb_fwd(qkv, name="sb_fwd")
    o = _ch_fwd(qkv, bias, o, name="ch_fwd")
    h2, mixed, y = _out_fwd(o, h1, g["out_sb"], g["out_ch"], g["mix_post"], w["out"], name="out_fwd")
    w.update(weights_for(2, h2))
    h3, n2, a2, b2, f2 = _ffn_fwd(h2, g["ffn2_pre"], g["ffn2_post"],
                                  w["ffn2_gate"], w["ffn2_up"], w["ffn2_down"], name="ffn2_fwd")
    loss, dh3, dwp, dwgate, dg_ple = _ple(p, h3, target, w["ple_proj"], w["ple_gate"],
                                          g["ple_post"], name="ple")
    tie = grads_done(0, {"ple_proj": dwp, "ple_gate": dwgate})

    df2, dg_ffn2_post = _junction(dh3, post=(f2, g["ffn2_post"] + tie, 0.5), name="junction3")
    dwg2, dwu2, dwd2, dn2 = _ffn_bwd(n2, df2, a2, b2, w["ffn2_gate"], w["ffn2_up"],
                                     w["ffn2_down"], name="ffn2_bwd")
    tie = grads_done(1, {"ffn2_gate": dwg2, "ffn2_up": dwu2, "ffn2_down": dwd2})
    dh2, dg_ffn2_pre, dy, dg_mix_post = _junction(
        dh3, pre=(dn2, h2, g["ffn2_pre"] + tie), post=(y, g["mix_post"], 1.0), name="junction2")
    dwout, do, dg_sb, dg_ch = _out_bwd(dy, mixed, o, g["out_sb"], g["out_ch"], w["out"],
                                       name="out_bwd")
    dq, dk, dv = _sb_bwd(qkv, ltot, do, name="sb_bwd")
    dq, dk, dv, dbias = _ch_bwd(qkv, bias, o, do, dq, dk, dv, name="ch_bwd")
    dfvec = _bias_grad(dbias, name="bias_grad")
    dwin, du = _qkv_bwd(dq, dk, dv, u, w["in"], name="qkv_bwd")
    tie = grads_done(2, {"out": dwout, "in": dwin})
    dh1, dg_mix_pre, df1, dg_ffn1_post = _junction(
        dh2, pre=(du, h1, g["mix_pre"] + tie), post=(f1, g["ffn1_post"], 0.5), name="junction1")
    dwg1, dwu1, dwd1, dn1 = _ffn_bwd(n1, df1, a1, b1, w["ffn1_gate"], w["ffn1_up"],
                                     w["ffn1_down"], name="ffn1_bwd")
    tie = grads_done(3, {"ffn1_gate": dwg1, "ffn1_up": dwu1, "ffn1_down": dwd1})
    dx, dg_ffn1_pre = _junction(dh1, pre=(dn1, x, g["ffn1_pre"] + tie), name="junction0")

    dg = {"ffn1_pre": dg_ffn1_pre, "ffn1_post": dg_ffn1_post, "mix_pre": dg_mix_pre,
          "mix_post": dg_mix_post, "out_sb": dg_sb, "out_ch": dg_ch,
          "ffn2_pre": dg_ffn2_pre, "ffn2_post": dg_ffn2_post, "ple_post": dg_ple}
    return loss, dx, dg, dfvec


_WEIGHTS = (
    ("ffn1_gate", "row", FF_SHARD, FF_SHARD_PAD, D_MODEL),
    ("ffn1_up", "row", FF_SHARD, FF_SHARD_PAD, D_MODEL),
    ("ffn1_down", "row", FF_SHARD, FF_SHARD_PAD, D_MODEL),
    ("in", "col", QKV_SHARD, QKV_SHARD, D_MODEL),
    ("out", "row", ROW_SHARD, ROW_SHARD, D_MODEL),
    ("ffn2_gate", "row", FF_SHARD, FF_SHARD_PAD, D_MODEL),
    ("ffn2_up", "row", FF_SHARD, FF_SHARD_PAD, D_MODEL),
    ("ffn2_down", "row", FF_SHARD, FF_SHARD_PAD, D_MODEL),
    ("ple_proj", "col", ROW_SHARD, ROW_SHARD, PLE_DIM),
    ("ple_gate", "row", ROW_SHARD, ROW_SHARD, D_MODEL),
)
_TRANSPOSED = ("ffn1_gate", "ffn1_up", "ffn2_gate", "ffn2_up")
_SPEC = {n: (kind, valid, pad, other) for n, kind, valid, pad, other in _WEIGHTS}
_GATHER_STAGES = (("ffn1_gate", "ffn1_up", "ffn1_down"), ("in", "out"),
                  ("ffn2_gate", "ffn2_up", "ffn2_down", "ple_proj", "ple_gate"))
_SCATTER_STAGES = (("ple_proj", "ple_gate"), ("ffn2_gate", "ffn2_up", "ffn2_down"),
                   ("out", "in"), ("ffn1_gate", "ffn1_up", "ffn1_down"))
HBM = pl.BlockSpec(memory_space=pltpu.HBM)
SEM = pl.BlockSpec(memory_space=pltpu.SEMAPHORE)
EFFECT = pltpu.SideEffectType.DATAFLOW_SIDE_EFFECTING


def _shard_shape(kind, size, other):
    return (other, size) if kind == "col" else (size, other)


def _window(ref, kind, start, size):
    return ref.at[:, pl.ds(start, size)] if kind == "col" else ref.at[pl.ds(start, size), :]


def _device_tuple(k):
    return (k // 4, (k // 2) % 2, k % 2)


def _my_index():
    return 4 * lax.axis_index("x") + 2 * lax.axis_index("y") + lax.axis_index("c")


def _pack_weights(shards):
    nw = len(_WEIGHTS)

    def body(*refs):
        ins, packed, full = refs[:nw], refs[nw:2 * nw], refs[2 * nw:3 * nw]
        sem = refs[3 * nw]
        me = _my_index()
        for (_, kind, valid, pad, _), src, dst in zip(_WEIGHTS, ins, packed):
            if pad != valid:
                dst[...] = jnp.zeros_like(dst)
            if kind == "col":
                dst[:, pl.ds(0, valid)] = _bf(src[...])
            else:
                dst[pl.ds(0, valid), :] = _bf(src[...])
        for k in range(N_DEV):
            @pl.when(me == k)
            def _():
                for w, (_, kind, _, pad, _) in enumerate(_WEIGHTS):
                    pltpu.make_async_copy(packed[w], _window(full[w], kind, k * pad, pad),
                                          sem.at[w]).start()
        for w, (_, kind, _, pad, _) in enumerate(_WEIGHTS):
            pltpu.make_async_copy(packed[w], _window(full[w], kind, 0, pad), sem.at[w]).wait()

    outs = pl.pallas_call(
        body, name="pack_weights",
        in_specs=[VMEM] * nw, out_specs=[VMEM] * nw + [ANY] * nw,
        out_shape=[_sds(_shard_shape(kind, pad, other), BF16) for _, kind, _, pad, other in _WEIGHTS]
        + [_sds(_shard_shape(kind, N_DEV * pad, other), BF16) for _, kind, _, pad, other in _WEIGHTS],
        scratch_shapes=[pltpu.SemaphoreType.DMA((nw,))],
        compiler_params=pltpu.CompilerParams(vmem_limit_bytes=VMEM_LIMIT_BYTES),
    )(*shards)
    names = [n for n, *_ in _WEIGHTS]
    return dict(zip(names, outs[:nw])), dict(zip(names, outs[nw:]))


def _hbm(a):
    return pltpu.with_memory_space_constraint(a, pltpu.HBM)


def _split_start(name, n, body_copies, sources, lands):
    def body(*refs):
        src, land = refs[:n], refs[n:2 * n]
        send, recv = refs[2 * n], refs[2 * n + 1]
        token = refs[-1]
        body_copies(src, land, send, recv)
        token[...] = jnp.zeros_like(token)

    arrays = list(sources) + list(lands)
    out = pl.pallas_call(
        body, name=name,
        out_shape=(pltpu.SemaphoreType.DMA((n,)), pltpu.SemaphoreType.DMA((n,)),
                   *[pltpu.HBM(a.shape, a.dtype) for a in arrays], _sds((8, 128))),
        in_specs=[HBM] * (2 * n), out_specs=(SEM, SEM, *[HBM] * (2 * n), VMEM),
        input_output_aliases={i: 2 + i for i in range(2 * n)},
        compiler_params=pltpu.CompilerParams(has_side_effects=EFFECT),
    )(*[_hbm(a) for a in arrays])
    return out[0], out[1], out[2:2 + n], out[2 + n:2 + 2 * n], out[-1]


def _split_wait(name, n, seven_of, send, recv, sources, lands, after):
    def body(*refs):
        land = refs[n:2 * n]
        send_ref, recv_ref = refs[2 * n], refs[2 * n + 1]
        myself = (lax.axis_index("x"), lax.axis_index("y"), lax.axis_index("c"))
        for w in range(n):
            seven = seven_of(w, land[w])
            copy = pltpu.make_async_remote_copy(
                src_ref=seven, dst_ref=seven, send_sem=send_ref.at[w], recv_sem=recv_ref.at[w],
                device_id=myself, device_id_type=MESH)
            copy.wait_send()
            copy.wait_recv()

    arrays = list(sources) + list(lands)
    out = pl.pallas_call(
        body, name=name,
        out_shape=[pltpu.HBM(a.shape, a.dtype) for a in arrays],
        in_specs=[HBM] * (2 * n) + [SEM, SEM, ANY], out_specs=[HBM] * (2 * n),
        input_output_aliases={i: i for i in range(2 * n)},
        compiler_params=pltpu.CompilerParams(has_side_effects=EFFECT),
    )(*arrays, send, recv, after)
    return out[n:]


def _gather_start(stage, names, packed, full):
    def copies(src, land, send, recv):
        me = _my_index()
        for k in range(N_DEV):
            @pl.when(me == k)
            def _():
                for w, name in enumerate(names):
                    kind, _, pad, _ = _SPEC[name]
                    dst = _window(land[w], kind, k * pad, pad)
                    for peer in range(N_DEV):
                        if peer != k:
                            pltpu.make_async_remote_copy(
                                src_ref=src[w], dst_ref=dst, send_sem=send.at[w],
                                recv_sem=recv.at[w], device_id=_device_tuple(peer),
                                device_id_type=MESH).start()

    return _split_start(f"gather_start{stage}", len(names), copies,
                        [packed[n] for n in names], [full[n] for n in names])


def _gather_wait(stage, names, started, after):
    send, recv, src, land, _ = started

    def seven_of(w, ref):
        kind, _, pad, _ = _SPEC[names[w]]
        return _window(ref, kind, 0, (N_DEV - 1) * pad)

    return dict(zip(names, _split_wait(f"gather_wait{stage}", len(names), seven_of,
                                       send, recv, src, land, after)))


def _own_slots(stage, names, grads):
    n = len(names)

    def body(*refs):
        ins, outs, sem = refs[:n], refs[n:2 * n], refs[2 * n]
        me = _my_index()
        for k in range(N_DEV):
            @pl.when(me == k)
            def _():
                for w, name in enumerate(names):
                    kind, _, pad, _ = _SPEC[name]
                    pltpu.make_async_copy(_window(ins[w], kind, k * pad, pad), outs[w].at[k],
                                          sem.at[w]).start()
        for w in range(n):
            pltpu.make_async_copy(outs[w].at[0], outs[w].at[1], sem.at[w]).wait()

    return pl.pallas_call(
        body, name=f"own_slots{stage}", in_specs=[ANY] * n, out_specs=[ANY] * n,
        out_shape=[_sds((N_DEV,) + _shard_shape(_SPEC[m][0], _SPEC[m][2], _SPEC[m][3]), BF16)
                   for m in names],
        scratch_shapes=[pltpu.SemaphoreType.DMA((n,))],
    )(*grads)


def _scatter_start(stage, names, grads, lands):
    def copies(src, land, send, recv):
        me = _my_index()
        for k in range(N_DEV):
            @pl.when(me != k)
            def _():
                for w, name in enumerate(names):
                    kind, _, pad, _ = _SPEC[name]
                    pltpu.make_async_remote_copy(
                        src_ref=_window(src[w], kind, k * pad, pad), dst_ref=land[w].at[me],
                        send_sem=send.at[w], recv_sem=recv.at[w],
                        device_id=_device_tuple(k), device_id_type=MESH).start()

    return _split_start(f"scatter_start{stage}", len(names), copies, grads, lands)


def _scatter_wait(stage, names, started, after):
    send, recv, src, land, _ = started
    seven_of = lambda w, ref: ref.at[pl.ds(0, N_DEV - 1)]
    return dict(zip(names, _split_wait(f"scatter_wait{stage}", len(names), seven_of,
                                       send, recv, src, land, after)))


def _tie(a, *tokens):
    for tok in tokens:
        a = a + tok[:1, :1]
    return a


def _allreduce_small(small):
    shape = small.shape

    def body(in_ref, out_ref, gath, send, recv):
        me = _my_index()
        for k in range(N_DEV):
            @pl.when(me != k)
            def _():
                pltpu.make_async_remote_copy(
                    src_ref=in_ref, dst_ref=gath.at[me], send_sem=send, recv_sem=recv,
                    device_id=_device_tuple(k), device_id_type=MESH).start()

            @pl.when(me == k)
            def _():
                gath[k] = in_ref[...]
        seven = gath.at[pl.ds(0, N_DEV - 1)]
        pltpu.make_async_remote_copy(
            src_ref=seven, dst_ref=seven, send_sem=send, recv_sem=recv,
            device_id=_device_tuple(0), device_id_type=MESH).wait()
        total = gath[0]
        for s in range(1, N_DEV):
            total = total + gath[s]
        out_ref[...] = total

    return pl.pallas_call(
        body, name="allreduce_small",
        in_specs=[VMEM], out_specs=VMEM, out_shape=_sds(shape),
        scratch_shapes=[pltpu.VMEM((N_DEV,) + shape, F32),
                        pltpu.SemaphoreType.DMA, pltpu.SemaphoreType.DMA],
    )(small)


def _adamw(w, m, v, g, *, kind=None, name):
    shape = w.shape

    def body(w_ref, m_ref, v_ref, g_ref, grad_ref, delta_ref, nm_ref, nv_ref):
        if kind is None:
            grad = g_ref[...]
        else:
            valid = (slice(None), pl.ds(0, shape[1])) if kind == "col" else (pl.ds(0, shape[0]), slice(None))
            grad = g_ref[(0,) + valid].astype(F32)
            for s in range(1, N_DEV):
                grad = grad + g_ref[(s,) + valid].astype(F32)
        new_m = ADAM_B1 * m_ref[...] + (1.0 - ADAM_B1) * grad
        new_v = ADAM_B2 * v_ref[...] + (1.0 - ADAM_B2) * (grad * grad)
        m_hat = new_m / (1.0 - ADAM_B1 ** ADAM_STEP)
        v_hat = new_v / (1.0 - ADAM_B2 ** ADAM_STEP)
        grad_ref[...] = grad
        delta_ref[...] = -ADAM_LR * (m_hat / (jnp.sqrt(v_hat) + ADAM_EPS) + ADAM_WD * w_ref[...])
        nm_ref[...] = new_m
        nv_ref[...] = new_v

    return pl.pallas_call(
        body, name=name, in_specs=[VMEM] * 4, out_specs=[VMEM] * 4,
        out_shape=[_sds(shape)] * 4,
        compiler_params=pltpu.CompilerParams(vmem_limit_bytes=VMEM_LIMIT_BYTES),
    )(w, m, v, g)


_GAINS = ("ffn1_pre", "ffn1_post", "mix_pre", "mix_post", "ffn2_pre", "ffn2_post", "ple_post")
_SMALL_ROWS = 16


def _stack_gains(get):
    return jnp.concatenate([get(n) for n in _GAINS]
                           + [jnp.concatenate([get("out_sb"), get("out_ch")], axis=1)], axis=0)


def kernel(x, p, g_ffn1_pre, g_ffn1_post, w_ffn1_gate, w_ffn1_up, w_ffn1_down, g_mix_pre, g_mix_post, w_in, g_out_sb, g_out_ch, rel_bias, w_out, g_ffn2_pre, g_ffn2_post, w_ffn2_gate, w_ffn2_up, w_ffn2_down, w_ple_proj, w_ple_gate, g_ple_post, loss_target, m_g_ffn1_pre, m_g_ffn1_post, m_w_ffn1_gate, m_w_ffn1_up, m_w_ffn1_down, m_g_mix_pre, m_g_mix_post, m_w_in, m_g_out_sb, m_g_out_ch, m_rel_bias, m_w_out, m_g_ffn2_pre, m_g_ffn2_post, m_w_ffn2_gate, m_w_ffn2_up, m_w_ffn2_down, m_w_ple_proj, m_w_ple_gate, m_g_ple_post, v_g_ffn1_pre, v_g_ffn1_post, v_w_ffn1_gate, v_w_ffn1_up, v_w_ffn1_down, v_g_mix_pre, v_g_mix_post, v_w_in, v_g_out_sb, v_g_out_ch, v_rel_bias, v_w_out, v_g_ffn2_pre, v_g_ffn2_post, v_w_ffn2_gate, v_w_ffn2_up, v_w_ffn2_down, v_w_ple_proj, v_w_ple_gate, v_g_ple_post):
    given = dict(locals())
    wnames = [n for n, *_ in _WEIGHTS]

    def shard(prefix, n):
        a = given[prefix + "w_" + n][0]
        return a.T if n in _TRANSPOSED else a

    packed, full = _pack_weights([shard("", n) for n in wnames])
    gathers = [_gather_start(s, names, packed, full) for s, names in enumerate(_GATHER_STAGES)]

    def weights_for(stage, after):
        return _gather_wait(stage, _GATHER_STAGES[stage], gathers[stage], after)

    scatters = {}

    def grads_done(stage, grads):
        names = _SCATTER_STAGES[stage]
        dws = [grads[n] for n in names]
        scatters[stage] = _scatter_start(stage, names, dws, _own_slots(stage, names, dws))
        return scatters[stage][-1][:1, :1]

    gains = {n: given["g_" + n] for n in _GAINS + ("out_sb", "out_ch")}
    gains["ffn1_pre"] = _tie(g_ffn1_pre, *[s[-1] for s in gathers])
    fvec = _rel_bias_to_fvec(rel_bias[0])
    loss, dx, dg, dfvec = _local_step(x[0], p[0, 0], loss_target[0], gains,
                                      weights_for, grads_done, fvec)

    results = {}

    def finish(stage, after):
        names = _SCATTER_STAGES[stage]
        recv = _scatter_wait(stage, names, scatters[stage], after)
        for n in names:
            out = _adamw(shard("", n), shard("m_", n), shard("v_", n), recv[n],
                         kind=_SPEC[n][0], name="adamw_" + n)
            results["w_" + n] = [a.T for a in out] if n in _TRANSPOSED else out
        return results["w_" + names[-1]][0]

    after = dx
    for stage in range(len(_SCATTER_STAGES) - 1):
        after = finish(stage, after)
    dfv = jnp.pad(dfvec[:, 0, :], ((0, 0), (0, D_MODEL - CH_WIN)))
    small = _allreduce_small(jnp.concatenate([_stack_gains(lambda n: dg[n]), dfv], axis=0))
    stacked = _adamw(_stack_gains(lambda n: given["g_" + n]),
                     _stack_gains(lambda n: given["m_g_" + n]),
                     _stack_gains(lambda n: given["v_g_" + n]),
                     small[:N_DEV], name="adamw_gains")
    finish(len(_SCATTER_STAGES) - 1, stacked[0])
    half = D_MODEL // 2
    for r, n in enumerate(_GAINS):
        results["g_" + n] = [a[r:r + 1] for a in stacked]
    results["g_out_sb"] = [a[N_DEV - 1:N_DEV, :half] for a in stacked]
    results["g_out_ch"] = [a[N_DEV - 1:N_DEV, half:] for a in stacked]
    d_rel = _fvec_grad_to_rel_bias(small[N_DEV:, :CH_WIN].reshape(N_DEV, 1, CH_WIN))
    results["rel_bias"] = _adamw(rel_bias[0], m_rel_bias[0], v_rel_bias[0], d_rel,
                                 name="adamw_rel_bias")

    order = ("g_ffn1_pre", "g_ffn1_post", "w_ffn1_gate", "w_ffn1_up", "w_ffn1_down",
             "g_mix_pre", "g_mix_post", "w_in", "g_out_sb", "g_out_ch", "rel_bias", "w_out",
             "g_ffn2_pre", "g_ffn2_post", "w_ffn2_gate", "w_ffn2_up", "w_ffn2_down",
             "w_ple_proj", "w_ple_gate", "g_ple_post")

    def leaf(name, idx):
        a = results[name][idx]
        return a if name.startswith("g_") else a[None]

    total_loss = lax.psum(loss[0, 0], ("x", "y", "c"))
    return (total_loss, dx[None],
            *[leaf(n, 0) for n in order], *[leaf(n, 1) for n in order],
            *[leaf(n, 2) for n in order], *[leaf(n, 3) for n in order])
```

```python
import functools

import jax
import jax.numpy as jnp
from jax import lax
from jax.experimental import pallas as pl
from jax.experimental.pallas import tpu as pltpu

F32 = jnp.float32
BF16 = jnp.bfloat16

N_DEV = 8
D_MODEL = 1024
D_FF = 2816
FF_SHARD = D_FF // N_DEV
FF_SHARD_PAD = 384
D_FF_PAD = FF_SHARD_PAD * N_DEV
QKV_WIDTH = 3 * D_MODEL
QKV_SHARD = QKV_WIDTH // N_DEV
PLE_DIM = 256
ROW_SHARD = D_MODEL // N_DEV
HEAD_DIM = 64
PAIR = 2 * HEAD_DIM
N_PAIRS = 4
CHUNK = 64
LOOKBACK = 8
REL_CLIP = 128
N_REL = 2 * REL_CLIP + 1
CH_QB = 256
CH_LOOK = LOOKBACK * CHUNK
CH_WIN = CH_LOOK + CH_QB
SB_BLK = 256
EPS = 1e-6
NEG_INF = -1e30
ATT_SCALE = HEAD_DIM ** -0.5
ADAM_LR = 0.001
ADAM_B1 = 0.9
ADAM_B2 = 0.999
ADAM_EPS = 1e-08
ADAM_WD = 0.01
ADAM_STEP = 10
VMEM_LIMIT_BYTES = 48 * 1024 * 1024
MESH = pl.DeviceIdType.MESH

ANY = pl.BlockSpec(memory_space=pl.ANY)
VMEM = pl.BlockSpec(memory_space=pltpu.VMEM)


def _params(*sem):
    return pltpu.CompilerParams(dimension_semantics=sem or None,
                                vmem_limit_bytes=VMEM_LIMIT_BYTES)


def _sds(shape, dtype=F32):
    return jax.ShapeDtypeStruct(shape, dtype)


def _bf(x):
    return x.astype(BF16)


def _dot(a, b):
    return jnp.dot(_bf(a), _bf(b), preferred_element_type=F32)


def _dot_nt(a, b):
    return lax.dot_general(_bf(a), _bf(b), (((1,), (1,)), ((), ())),
                           preferred_element_type=F32)


def _dot_tn(a, b):
    return lax.dot_general(_bf(a), _bf(b), (((0,), (0,)), ((), ())),
                           preferred_element_type=F32)


def _sigmoid(x):
    return 1.0 / (1.0 + jnp.exp(-x))


def _softplus(x):
    return jnp.maximum(x, 0.0) + jnp.log(1.0 + jnp.exp(-jnp.abs(x)))


def _rstd(x):
    return lax.rsqrt(jnp.mean(x * x, axis=-1, keepdims=True) + EPS)


def _rms(x, g):
    return x * _rstd(x) * g


def _rms_bwd(dy, x, g):
    r = _rstd(x)
    w = dy * g
    dx = r * (w - x * (r * r) * jnp.mean(w * x, axis=-1, keepdims=True))
    dg = jnp.sum(dy * (x * r), axis=0, keepdims=True)
    return dx, dg


def _dot_exact01(x, u):
    hi = _bf(x)
    lo = _bf(x - hi.astype(F32))
    return (jnp.dot(hi, u, preferred_element_type=F32)
            + jnp.dot(lo, u, preferred_element_type=F32))


def _head_masks():
    lane = lax.broadcasted_iota(jnp.int32, (1, PAIR), 1)
    return lane < HEAD_DIM, lane >= HEAD_DIM


def _ffn_fwd(x, g_pre, g_post, wg, wu, wd, *, name):
    t = x.shape[0]
    tm, tj = 512, 512
    ni, nj = t // tm, D_FF_PAD // tj

    def body(x_ref, gpre_ref, gpost_ref, wg_ref, wu_ref, wd_ref,
             h_ref, n_ref, a_ref, b_ref, f_ref, acc_ref):
        j = pl.program_id(1)

        @pl.when(j == 0)
        def _():
            n_ref[...] = _bf(_rms(x_ref[...], gpre_ref[...]))
            acc_ref[...] = jnp.zeros_like(acc_ref)

        n = n_ref[...]
        a = _dot_nt(n, wg_ref[...])
        b = _dot_nt(n, wu_ref[...])
        a_ref[...] = a
        b_ref[...] = b
        hmid = a * _sigmoid(a) * b
        acc_ref[...] += jnp.dot(_bf(hmid), wd_ref[...], preferred_element_type=F32)

        @pl.when(j == nj - 1)
        def _():
            f = acc_ref[...]
            f_ref[...] = f
            h_ref[...] = x_ref[...] + 0.5 * _rms(f, gpost_ref[...])

    row = pl.BlockSpec((tm, D_MODEL), lambda i, j: (i, 0))
    gain = pl.BlockSpec((1, D_MODEL), lambda i, j: (0, 0))
    col = pl.BlockSpec((tm, tj), lambda i, j: (i, j))
    wtile = pl.BlockSpec((tj, D_MODEL), lambda i, j: (j, 0))
    return pl.pallas_call(
        body, name=name, grid=(ni, nj),
        in_specs=[row, gain, gain, wtile, wtile, wtile],
        out_specs=[row, row, col, col, row],
        out_shape=[_sds((t, D_MODEL)), _sds((t, D_MODEL), BF16),
                   _sds((t, D_FF_PAD)), _sds((t, D_FF_PAD)), _sds((t, D_MODEL))],
        scratch_shapes=[pltpu.VMEM((tm, D_MODEL), F32)],
        compiler_params=_params("arbitrary", "arbitrary"),
    )(x, g_pre, g_post, wg, wu, wd)


def _ffn_bwd(n, df, a, b, wg, wu, wd, *, name):
    t = n.shape[0]
    tj, ts = 256, 512
    nj, ns = D_FF_PAD // tj, t // ts

    def body(n_hbm, df_hbm, a_ref, b_ref, wg_ref, wu_ref, wd_ref,
             dwg_ref, dwu_ref, dwd_ref, dn_hbm,
             n_v, df_v, dn_v, ag, au, ad, sem):
        j = pl.program_id(0)

        @pl.when(j == 0)
        def _():
            c1 = pltpu.make_async_copy(n_hbm, n_v, sem.at[0])
            c2 = pltpu.make_async_copy(df_hbm, df_v, sem.at[1])
            c1.start()
            c2.start()
            dn_v[...] = jnp.zeros_like(dn_v)
            c1.wait()
            c2.wait()

        ag[...] = jnp.zeros_like(ag)
        au[...] = jnp.zeros_like(au)
        ad[...] = jnp.zeros_like(ad)
        wgj, wuj, wdj = wg_ref[...], wu_ref[...], wd_ref[...]
        for s in range(ns):
            rows = pl.ds(s * ts, ts)
            av, bv = a_ref[rows, :], b_ref[rows, :]
            sig = _sigmoid(av)
            silu = av * sig
            dfr = df_v[rows, :]
            nr = n_v[rows, :]
            dhmid = _dot_nt(dfr, wdj)
            da = dhmid * bv * (sig * (1.0 + av * (1.0 - sig)))
            db = dhmid * silu
            ad[...] += _dot_tn(silu * bv, dfr)
            ag[...] += _dot_tn(da, nr)
            au[...] += _dot_tn(db, nr)
            dn_v[rows, :] += _dot(da, wgj) + _dot(db, wuj)
        dwg_ref[...] = _bf(ag[...])
        dwu_ref[...] = _bf(au[...])
        dwd_ref[...] = _bf(ad[...])

        @pl.when(j == nj - 1)
        def _():
            c = pltpu.make_async_copy(dn_v, dn_hbm, sem.at[0])
            c.start()
            c.wait()

    roww = pl.BlockSpec((tj, D_MODEL), lambda j: (j, 0))
    act = pl.BlockSpec((t, tj), lambda j: (0, j))
    return pl.pallas_call(
        body, name=name, grid=(nj,),
        in_specs=[ANY, ANY, act, act, roww, roww, roww],
        out_specs=[roww, roww, roww, ANY],
        out_shape=[_sds((D_FF_PAD, D_MODEL), BF16)] * 3 + [_sds((t, D_MODEL))],
        scratch_shapes=[pltpu.VMEM((t, D_MODEL), BF16), pltpu.VMEM((t, D_MODEL), BF16),
                        pltpu.VMEM((t, D_MODEL), F32)]
        + [pltpu.VMEM((tj, D_MODEL), F32)] * 3 + [pltpu.SemaphoreType.DMA((2,))],
        compiler_params=_params("arbitrary"),
    )(n, df, a, b, wg, wu, wd)


def _junction(dres, pre=None, post=None, *, name):
    t = dres.shape[0]
    tm = 512
    ni = t // tm
    n_in = 1 + (3 if pre else 0) + (2 if post else 0)
    coef = post[2] if post else None

    def body(*refs):
        ins, outs = list(refs[:n_in]), list(refs[n_in:])
        i = pl.program_id(0)
        dh = ins.pop(0)[...]
        if pre:
            dn_ref, x_ref, gpre_ref = ins.pop(0), ins.pop(0), ins.pop(0)
            dh_ref, dgpre_ref = outs.pop(0), outs.pop(0)
            dx, dg = _rms_bwd(dn_ref[...], x_ref[...], gpre_ref[...])
            dh = dh + dx
            dh_ref[...] = dh

            @pl.when(i == 0)
            def _():
                dgpre_ref[...] = jnp.zeros_like(dgpre_ref)
            dgpre_ref[...] += dg
        if post:
            f_ref, gpost_ref = ins.pop(0), ins.pop(0)
            df_ref, dgpost_ref = outs.pop(0), outs.pop(0)
            df, dg = _rms_bwd(coef * dh, f_ref[...], gpost_ref[...])
            df_ref[...] = _bf(df)

            @pl.when(i == 0)
            def _():
                dgpost_ref[...] = jnp.zeros_like(dgpost_ref)
            dgpost_ref[...] += dg

    row = pl.BlockSpec((tm, D_MODEL), lambda i: (i, 0))
    gain = pl.BlockSpec((1, D_MODEL), lambda i: (0, 0))
    args, in_specs, out_specs, out_shape = [dres], [row], [], []
    if pre:
        args += list(pre)
        in_specs += [row, row, gain]
        out_specs += [row, gain]
        out_shape += [_sds((t, D_MODEL)), _sds((1, D_MODEL))]
    if post:
        args += [post[0], post[1]]
        in_specs += [row, gain]
        out_specs += [row, gain]
        out_shape += [_sds((t, D_MODEL), BF16), _sds((1, D_MODEL))]
    return pl.pallas_call(
        body, name=name, grid=(ni,), in_specs=in_specs, out_specs=out_specs,
        out_shape=out_shape, compiler_params=_params("arbitrary"),
    )(*args)


def _qkv_fwd(h, g, win, *, name):
    t = h.shape[0]
    tm, tn = 512, 768
    ni, nj = t // tm, QKV_WIDTH // tn

    def body(h_ref, g_ref, w_ref, qkv_ref, u_ref):
        @pl.when(pl.program_id(1) == 0)
        def _():
            u_ref[...] = _bf(_rms(h_ref[...], g_ref[...]))
        qkv_ref[...] = jnp.dot(u_ref[...], w_ref[...], preferred_element_type=F32)

    row = pl.BlockSpec((tm, D_MODEL), lambda i, j: (i, 0))
    return pl.pallas_call(
        body, name=name, grid=(ni, nj),
        in_specs=[row, pl.BlockSpec((1, D_MODEL), lambda i, j: (0, 0)),
                  pl.BlockSpec((D_MODEL, tn), lambda i, j: (0, j))],
        out_specs=[pl.BlockSpec((tm, tn), lambda i, j: (i, j)), row],
        out_shape=[_sds((t, QKV_WIDTH)), _sds((t, D_MODEL), BF16)],
        compiler_params=_params("arbitrary", "arbitrary"),
    )(h, g, win)


def _qkv_bwd(dq, dk, dv, u, win, *, name):
    t = u.shape[0]
    tn, ts = 512, 512
    nj, ns = QKV_WIDTH // tn, t // ts

    def body(dq_ref, dk_ref, dv_ref, u_ref, w_ref, dw_ref, du_hbm, du_v, acc_ref, sem):
        j = pl.program_id(0)

        @pl.when(j == 0)
        def _():
            du_v[...] = jnp.zeros_like(du_v)

        wj = w_ref[...]
        for role, d_ref in enumerate((dq_ref, dk_ref, dv_ref)):
            @pl.when(j % 3 == role)
            def _():
                acc_ref[...] = jnp.zeros_like(acc_ref)
                for s in range(ns):
                    rows = pl.ds(s * ts, ts)
                    dcol = d_ref[rows, :]
                    acc_ref[...] += _dot_tn(u_ref[rows, :], dcol)
                    du_v[rows, :] += _dot_nt(dcol, wj)
                dw_ref[...] = _bf(acc_ref[...])

        @pl.when(j == nj - 1)
        def _():
            c = pltpu.make_async_copy(du_v, du_hbm, sem)
            c.start()
            c.wait()

    colw = pl.BlockSpec((D_MODEL, tn), lambda j: (0, j))
    grp = pl.BlockSpec((t, tn), lambda j: (0, j // 3))
    return pl.pallas_call(
        body, name=name, grid=(nj,),
        in_specs=[grp, grp, grp, pl.BlockSpec((t, D_MODEL), lambda j: (0, 0)), colw],
        out_specs=[colw, ANY],
        out_shape=[_sds((D_MODEL, QKV_WIDTH), BF16), _sds((t, D_MODEL))],
        scratch_shapes=[pltpu.VMEM((t, D_MODEL), F32), pltpu.VMEM((D_MODEL, tn), F32),
                        pltpu.SemaphoreType.DMA],
        compiler_params=_params("arbitrary"),
    )(dq, dk, dv, u, win)


def _sb_tile(qm, kj, q0, k0):
    z = _dot_nt(qm, kj) * ATT_SCALE
    rows = q0 + lax.broadcasted_iota(jnp.int32, z.shape, 0)
    cols = k0 + lax.broadcasted_iota(jnp.int32, z.shape, 1)
    return z, cols < rows, _softplus(z)


def _tri(n, inclusive):
    r = lax.broadcasted_iota(jnp.int32, (n, n), 0)
    c = lax.broadcasted_iota(jnp.int32, (n, n), 1)
    return jnp.where((r >= c) if inclusive else (r > c), 1.0, 0.0).astype(BF16)


def _sb_fwd(qkv, *, name):
    t = qkv.shape[0]
    blk = SB_BLK
    ni = t // blk

    def body(q_ref, k_ref, v_ref, o_ref, ltot_ref):
        i = pl.program_id(1)
        q0 = i * blk
        u_strict = _tri(blk, False)
        q = q_ref[...]
        outs, totals = [], []
        for hm in _head_masks():
            qm = jnp.where(hm, q, 0.0)

            def step(jj, carry):
                acc, c_l = carry
                k0 = pl.multiple_of((i - jj) * blk, blk)
                kj = k_ref[pl.ds(k0, blk), :]
                vj = v_ref[pl.ds(k0, blk), :]
                z, mask, sp = _sb_tile(qm, kj, q0, k0)
                lf = jnp.where(mask, -sp, 0.0)
                later = _dot_exact01(lf, u_strict) + c_l
                a = jnp.where(mask, jnp.exp(z - sp + later), 0.0)
                acc = acc + _dot(a, vj)
                return acc, c_l + jnp.sum(lf, axis=1, keepdims=True)

            acc, c_l = lax.fori_loop(
                0, i + 1, step,
                (jnp.zeros((blk, PAIR), F32), jnp.zeros((blk, 1), F32)))
            outs.append(acc)
            totals.append(c_l)
        lo = _head_masks()[0]
        o_ref[...] = jnp.where(lo, outs[0], outs[1])
        ltot_ref[...] = jnp.where(lo, totals[0], totals[1])

    blkspec = pl.BlockSpec((blk, PAIR), lambda p, i: (i, p))
    return pl.pallas_call(
        body, name=name, grid=(N_PAIRS, ni),
        in_specs=[blkspec,
                  pl.BlockSpec((t, PAIR), lambda p, i: (0, N_PAIRS + p)),
                  pl.BlockSpec((t, PAIR), lambda p, i: (0, 2 * N_PAIRS + p))],
        out_specs=[blkspec, blkspec],
        out_shape=[_sds((t, D_MODEL)), _sds((t, D_MODEL // 2))],
        compiler_params=_params("arbitrary", "arbitrary"),
    )(qkv, qkv, qkv)


def _sb_bwd(qkv, ltot, do, *, name):
    t = qkv.shape[0]
    blk = SB_BLK
    ni = t // blk

    def body(q_ref, k_ref, v_ref, lt_ref, do_ref, dq_ref, dkout_ref, dvout_ref, dk_ref, dv_ref):
        i = pl.program_id(1)
        q0 = i * blk

        @pl.when(i == 0)
        def _():
            dk_ref[...] = jnp.zeros_like(dk_ref)
            dv_ref[...] = jnp.zeros_like(dv_ref)

        r = lax.broadcasted_iota(jnp.int32, (blk, blk), 0)
        c = lax.broadcasted_iota(jnp.int32, (blk, blk), 1)
        u_upto = jnp.where(r <= c, 1.0, 0.0).astype(BF16)
        u_before = jnp.where(r < c, 1.0, 0.0).astype(BF16)
        lane = lax.broadcasted_iota(jnp.int32, (1, PAIR), 1)
        q, lt_blk, do_blk = q_ref[...], lt_ref[...], do_ref[...]
        dqs = []
        for h, hm in enumerate(_head_masks()):
            qm = _bf(jnp.where(hm, q, 0.0))
            dom = _bf(jnp.where(hm, do_blk, 0.0))
            total = jnp.sum(jnp.where(lane == h * HEAD_DIM, lt_blk, 0.0), axis=1, keepdims=True)

            def step(j, carry):
                dq_acc, c_l, c_g = carry
                k0 = pl.multiple_of(j * blk, blk)
                krows = pl.ds(k0, blk)
                kj = k_ref[krows, :]
                vj = v_ref[krows, :]
                z, mask, sp = _sb_tile(qm, kj, q0, k0)
                sig = jnp.exp(z - sp)
                lf = jnp.where(mask, -sp, 0.0)
                later = total - (_dot_exact01(lf, u_upto) + c_l)
                a = jnp.where(mask, jnp.exp(z - sp + later), 0.0)
                g = a * _dot_nt(dom, vj)
                g_before = _dot_exact01(g, u_before) + c_g
                dz = jnp.where(mask, g * (1.0 - sig) - g_before * sig, 0.0) * ATT_SCALE
                dq_acc = dq_acc + _dot(dz, kj)
                dk_ref[krows, :] += _dot_tn(dz, qm)
                dv_ref[krows, :] += _dot_tn(a, dom)
                return (dq_acc, c_l + jnp.sum(lf, axis=1, keepdims=True),
                        c_g + jnp.sum(g, axis=1, keepdims=True))

            dq_acc, _, _ = lax.fori_loop(
                0, i + 1, step,
                (jnp.zeros((blk, PAIR), F32), jnp.zeros((blk, 1), F32),
                 jnp.zeros((blk, 1), F32)))
            dqs.append(dq_acc)
        dq_ref[...] = _bf(jnp.where(_head_masks()[0], dqs[0], dqs[1]))

        @pl.when(i == ni - 1)
        def _():
            dkout_ref[...] = _bf(dk_ref[...])
            dvout_ref[...] = _bf(dv_ref[...])

    blkspec = lambda off: pl.BlockSpec((blk, PAIR), lambda p, i: (i, off + p))
    full = lambda off: pl.BlockSpec((t, PAIR), lambda p, i: (0, off + p))
    return pl.pallas_call(
        body, name=name, grid=(N_PAIRS, ni),
        in_specs=[blkspec(0), full(N_PAIRS), full(2 * N_PAIRS), blkspec(0), blkspec(0)],
        out_specs=[blkspec(0), full(0), full(0)],
        out_shape=[_sds((t, D_MODEL), BF16)] * 3,
        scratch_shapes=[pltpu.VMEM((t, PAIR), F32), pltpu.VMEM((t, PAIR), F32)],
        compiler_params=_params("arbitrary", "arbitrary"),
    )(qkv, qkv, qkv, ltot, do)


def _ch_mask(i):
    r = lax.broadcasted_iota(jnp.int32, (CH_QB, CH_WIN), 0)
    c = lax.broadcasted_iota(jnp.int32, (CH_QB, CH_WIN), 1)
    qc = LOOKBACK + lax.shift_right_arithmetic(r, 6)
    kc = lax.shift_right_arithmetic(c, 6)
    first = i * (CH_QB // CHUNK) - LOOKBACK
    return (kc <= qc) & (kc >= qc - LOOKBACK) & (kc + first >= 0)


def _ch_probs(qm, kw, bias_h, mask):
    z = _dot_nt(qm, kw) * ATT_SCALE + bias_h
    z = jnp.where(mask, z, NEG_INF)
    e = jnp.exp(z - jnp.max(z, axis=1, keepdims=True))
    return e / jnp.sum(e, axis=1, keepdims=True)


def _ch_fill(pad_ref, src_ref, t):
    pad_ref[pl.ds(0, CH_LOOK), :] = jnp.zeros((CH_LOOK, PAIR), BF16)
    pad_ref[pl.ds(CH_LOOK, t), :] = _bf(src_ref[...])


def _ch_fwd(qkv, bias, o_in, *, name):
    t = qkv.shape[0]
    ni = t // CH_QB

    def body(q_ref, k_ref, v_ref, bias_ref, _alias, o_ref, kpad, vpad):
        i = pl.program_id(1)

        @pl.when(i == 0)
        def _():
            _ch_fill(kpad, k_ref, t)
            _ch_fill(vpad, v_ref, t)

        win = pl.ds(pl.multiple_of(i * CH_QB, CH_QB), CH_WIN)
        kw, vw = kpad[win, :], vpad[win, :]
        mask = _ch_mask(i)
        q = q_ref[...]
        outs = []
        for h, hm in enumerate(_head_masks()):
            p = _ch_probs(jnp.where(hm, q, 0.0), kw, bias_ref[h], mask)
            outs.append(_dot(p, vw))
        o_ref[...] = jnp.where(_head_masks()[0], outs[0], outs[1])

    full = lambda off: pl.BlockSpec((t, PAIR), lambda p, i: (0, off + p))
    return pl.pallas_call(
        body, name=name, grid=(N_PAIRS, ni),
        in_specs=[pl.BlockSpec((CH_QB, PAIR), lambda p, i: (i, 3 * N_PAIRS + p)),
                  full(4 * N_PAIRS), full(5 * N_PAIRS),
                  pl.BlockSpec((2, CH_QB, CH_WIN), lambda p, i: (p, 0, 0)), ANY],
        out_specs=pl.BlockSpec((CH_QB, PAIR), lambda p, i: (i, N_PAIRS + p)),
        out_shape=_sds((t, D_MODEL)),
        scratch_shapes=[pltpu.VMEM((t + CH_LOOK, PAIR), BF16)] * 2,
        input_output_aliases={4: 0},
        compiler_params=_params("arbitrary", "arbitrary"),
    )(qkv, qkv, qkv, bias, o_in)


def _ch_bwd(qkv, bias, o, do, dq_in, dk_in, dv_in, *, name):
    t = qkv.shape[0]
    ni = t // CH_QB

    def body(q_ref, k_ref, v_ref, bias_ref, o_ref, do_ref, _a0, _a1, _a2,
             dq_ref, dkout_ref, dvout_ref, dbias_ref, kpad, vpad, dkpad, dvpad):
        i = pl.program_id(1)

        @pl.when(i == 0)
        def _():
            _ch_fill(kpad, k_ref, t)
            _ch_fill(vpad, v_ref, t)
            dkpad[...] = jnp.zeros_like(dkpad)
            dvpad[...] = jnp.zeros_like(dvpad)
            dbias_ref[...] = jnp.zeros_like(dbias_ref)

        win = pl.ds(pl.multiple_of(i * CH_QB, CH_QB), CH_WIN)
        kw, vw = kpad[win, :], vpad[win, :]
        mask = _ch_mask(i)
        q, o_blk, do_blk = q_ref[...], o_ref[...], do_ref[...]
        dqs = []
        for h, hm in enumerate(_head_masks()):
            qm = _bf(jnp.where(hm, q, 0.0))
            dom = jnp.where(hm, do_blk, 0.0)
            delta = jnp.sum(dom * o_blk, axis=1, keepdims=True)
            dom = _bf(dom)
            p = _ch_probs(qm, kw, bias_ref[h], mask)
            ds = p * (_dot_nt(dom, vw) - delta)
            dbias_ref[h] += ds
            dsz = ds * ATT_SCALE
            dqs.append(_dot(dsz, kw))
            dkpad[win, :] += _dot_tn(dsz, qm)
            dvpad[win, :] += _dot_tn(p, dom)
        dq_ref[...] = _bf(jnp.where(_head_masks()[0], dqs[0], dqs[1]))

        @pl.when(i == ni - 1)
        def _():
            dkout_ref[...] = _bf(dkpad[pl.ds(CH_LOOK, t), :])
            dvout_ref[...] = _bf(dvpad[pl.ds(CH_LOOK, t), :])

    blkspec = lambda off: pl.BlockSpec((CH_QB, PAIR), lambda p, i: (i, off + p))
    full = lambda off: pl.BlockSpec((t, PAIR), lambda p, i: (0, off + p))
    bias_spec = pl.BlockSpec((2, CH_QB, CH_WIN), lambda p, i: (p, 0, 0))
    return pl.pallas_call(
        body, name=name, grid=(N_PAIRS, ni),
        in_specs=[blkspec(3 * N_PAIRS), full(4 * N_PAIRS), full(5 * N_PAIRS), bias_spec,
                  blkspec(N_PAIRS), blkspec(N_PAIRS), ANY, ANY, ANY],
        out_specs=[blkspec(N_PAIRS), full(N_PAIRS), full(N_PAIRS), bias_spec],
        out_shape=[_sds((t, D_MODEL), BF16)] * 3 + [_sds((2 * N_PAIRS, CH_QB, CH_WIN))],
        scratch_shapes=[pltpu.VMEM((t + CH_LOOK, PAIR), BF16)] * 2
        + [pltpu.VMEM((t + CH_LOOK, PAIR), F32)] * 2,
        input_output_aliases={6: 0, 7: 1, 8: 2},
        compiler_params=_params("arbitrary", "arbitrary"),
    )(qkv, qkv, qkv, bias, o, do, dq_in, dk_in, dv_in)


def _bias_expand(fvec, *, name):
    n_heads = fvec.shape[0]

    def body(f_ref, o_ref, rows8):
        row = f_ref[0]
        for r in range(8):
            rows8[pl.ds(r, 1), :] = pltpu.roll(row, r, 1)
        base = rows8[...]
        for blk in range(CH_QB // 8):
            o_ref[0, pl.ds(8 * blk, 8), :] = pltpu.roll(base, 8 * blk, 1)

    return pl.pallas_call(
        body, name=name, grid=(n_heads,),
        in_specs=[pl.BlockSpec((1, 1, CH_WIN), lambda h: (h, 0, 0))],
        out_specs=pl.BlockSpec((1, CH_QB, CH_WIN), lambda h: (h, 0, 0)),
        out_shape=_sds((n_heads, CH_QB, CH_WIN)),
        scratch_shapes=[pltpu.VMEM((8, CH_WIN), F32)],
        compiler_params=_params("arbitrary"),
    )(fvec)


def _bias_grad(dbias, *, name):
    n_heads = dbias.shape[0]
    first = CH_LOOK - REL_CLIP

    def body(d_ref, o_ref, acc8):
        acc = jnp.zeros((8, CH_WIN), F32)
        for blk in range(CH_QB // 8):
            acc = acc + pltpu.roll(d_ref[0, pl.ds(8 * blk, 8), :], (CH_WIN - 8 * blk) % CH_WIN, 1)
        acc8[...] = acc
        dvec = jnp.zeros((1, CH_WIN), F32)
        for r in range(8):
            dvec = dvec + pltpu.roll(acc8[pl.ds(r, 1), :], (CH_WIN - r) % CH_WIN, 1)
        lane = lax.broadcasted_iota(jnp.int32, (1, CH_WIN), 1)
        clipped = (lane <= first) | (lane >= first + REL_CLIP + CHUNK)
        total = jnp.sum(jnp.where(clipped, dvec, 0.0), axis=1, keepdims=True)
        o_ref[0] = jnp.where(lane == first, total, dvec)

    return pl.pallas_call(
        body, name=name, grid=(n_heads,),
        in_specs=[pl.BlockSpec((1, CH_QB, CH_WIN), lambda h: (h, 0, 0))],
        out_specs=pl.BlockSpec((1, 1, CH_WIN), lambda h: (h, 0, 0)),
        out_shape=_sds((n_heads, 1, CH_WIN)),
        scratch_shapes=[pltpu.VMEM((8, CH_WIN), F32)],
        compiler_params=_params("arbitrary"),
    )(dbias)


def _out_fwd(o, h1, g_sb, g_ch, g_post, wout, *, name):
    t = o.shape[0]
    tm = 512
    half = D_MODEL // 2

    def body(o_ref, h_ref, gsb_ref, gch_ref, gpost_ref, w_ref, h2_ref, mixed_ref, y_ref):
        ov = o_ref[...]
        mixed = jnp.concatenate([_rms(ov[:, :half], gsb_ref[...]),
                                 _rms(ov[:, half:], gch_ref[...])], axis=1)
        mixed_ref[...] = _bf(mixed)
        y = _dot(mixed, w_ref[...])
        y_ref[...] = y
        h2_ref[...] = h_ref[...] + _rms(y, gpost_ref[...])

    row = pl.BlockSpec((tm, D_MODEL), lambda i: (i, 0))
    gain = lambda n: pl.BlockSpec((1, n), lambda i: (0, 0))
    return pl.pallas_call(
        body, name=name, grid=(t // tm,),
        in_specs=[row, row, gain(half), gain(half), gain(D_MODEL),
                  pl.BlockSpec((D_MODEL, D_MODEL), lambda i: (0, 0))],
        out_specs=[row, row, row],
        out_shape=[_sds((t, D_MODEL)), _sds((t, D_MODEL), BF16), _sds((t, D_MODEL))],
        compiler_params=_params("arbitrary"),
    )(o, h1, g_sb, g_ch, g_post, wout)


def _out_bwd(dy, mixed, o, g_sb, g_ch, wout, *, name):
    t = o.shape[0]
    tm = 512
    ni = t // tm
    half = D_MODEL // 2

    def body(dy_ref, mixed_ref, o_ref, gsb_ref, gch_ref, w_ref,
             dw_ref, do_ref, dgsb_ref, dgch_ref, acc_ref):
        i = pl.program_id(0)

        @pl.when(i == 0)
        def _():
            acc_ref[...] = jnp.zeros_like(acc_ref)
            dgsb_ref[...] = jnp.zeros_like(dgsb_ref)
            dgch_ref[...] = jnp.zeros_like(dgch_ref)

        dyv = dy_ref[...]
        acc_ref[...] += _dot_tn(mixed_ref[...], dyv)
        dm = _dot_nt(dyv, w_ref[...])
        ov = o_ref[...]
        doa, dga = _rms_bwd(dm[:, :half], ov[:, :half], gsb_ref[...])
        dob, dgb = _rms_bwd(dm[:, half:], ov[:, half:], gch_ref[...])
        do_ref[...] = jnp.concatenate([doa, dob], axis=1)
        dgsb_ref[...] += dga
        dgch_ref[...] += dgb

        @pl.when(i == ni - 1)
        def _():
            dw_ref[...] = _bf(acc_ref[...])

    row = pl.BlockSpec((tm, D_MODEL), lambda i: (i, 0))
    gain = pl.BlockSpec((1, half), lambda i: (0, 0))
    sq = pl.BlockSpec((D_MODEL, D_MODEL), lambda i: (0, 0))
    return pl.pallas_call(
        body, name=name, grid=(ni,),
        in_specs=[row, row, row, gain, gain, sq],
        out_specs=[sq, row, gain, gain],
        out_shape=[_sds((D_MODEL, D_MODEL), BF16), _sds((t, D_MODEL)),
                   _sds((1, half)), _sds((1, half))],
        scratch_shapes=[pltpu.VMEM((D_MODEL, D_MODEL), F32)],
        compiler_params=_params("arbitrary"),
    )(dy, mixed, o, g_sb, g_ch, wout)


def _ple(p, h3, target, wp, wgate, g, *, name):
    t = h3.shape[0]
    tm = 512
    ni = t // tm

    def body(p_ref, h_ref, tgt_ref, wp_ref, wg_ref, g_ref,
             loss_ref, dres_ref, dwp_ref, dwg_ref, dg_ref, accp, accg):
        i = pl.program_id(0)

        @pl.when(i == 0)
        def _():
            loss_ref[...] = jnp.zeros_like(loss_ref)
            dg_ref[...] = jnp.zeros_like(dg_ref)
            accp[...] = jnp.zeros_like(accp)
            accg[...] = jnp.zeros_like(accg)

        pv, hv, gv = p_ref[...], h_ref[...], g_ref[...]
        pe = _dot(pv, wp_ref[...])
        sig = _sigmoid(_dot(hv, wg_ref[...]))
        e = pe * sig
        err = hv + _rms(e, gv) - tgt_ref[...]
        tok = jnp.mean(err * err, axis=-1, keepdims=True)
        loss_ref[...] += 0.5 * jnp.sum(tok, axis=0, keepdims=True)
        dh4 = err * (1.0 / D_MODEL)
        de, dg = _rms_bwd(dh4, e, gv)
        dg_ref[...] += dg
        dpe = de * sig
        dgt = de * pe * sig * (1.0 - sig)
        accp[...] += _dot_tn(pv, dpe)
        accg[...] += _dot_tn(hv, dgt)
        dres_ref[...] = dh4 + _dot_nt(dgt, wg_ref[...])

        @pl.when(i == ni - 1)
        def _():
            dwp_ref[...] = _bf(accp[...])
            dwg_ref[...] = _bf(accg[...])

    row = pl.BlockSpec((tm, D_MODEL), lambda i: (i, 0))
    const = lambda r, c: pl.BlockSpec((r, c), lambda i: (0, 0))
    return pl.pallas_call(
        body, name=name, grid=(ni,),
        in_specs=[pl.BlockSpec((tm, PLE_DIM), lambda i: (i, 0)), row, row,
                  const(PLE_DIM, D_MODEL), const(D_MODEL, D_MODEL), const(1, D_MODEL)],
        out_specs=[const(1, 128), row, const(PLE_DIM, D_MODEL), const(D_MODEL, D_MODEL),
                   const(1, D_MODEL)],
        out_shape=[_sds((1, 128)), _sds((t, D_MODEL)), _sds((PLE_DIM, D_MODEL), BF16),
                   _sds((D_MODEL, D_MODEL), BF16), _sds((1, D_MODEL))],
        scratch_shapes=[pltpu.VMEM((PLE_DIM, D_MODEL), F32), pltpu.VMEM((D_MODEL, D_MODEL), F32)],
        compiler_params=_params("arbitrary"),
    )(p, h3, target, wp, wgate, g)


def _rel_bias_to_fvec(rel_bias):
    rev = rel_bias[:, ::-1]
    n_heads = rel_bias.shape[0]
    first = CH_LOOK - REL_CLIP
    n_var = REL_CLIP + CHUNK
    clipped = rev[:, :1]
    fvec = jnp.concatenate([jnp.broadcast_to(clipped, (n_heads, first)), rev[:, :n_var],
                            jnp.broadcast_to(clipped, (n_heads, CH_WIN - first - n_var))], axis=1)
    return fvec.reshape(n_heads, 1, CH_WIN)


def _fvec_grad_to_rel_bias(dfvec):
    first = CH_LOOK - REL_CLIP
    n_var = REL_CLIP + CHUNK
    rev = jnp.pad(dfvec[:, 0, first:first + n_var], ((0, 0), (0, N_REL - n_var)))
    return rev[:, ::-1]


def _local_step(x, p, target, g, weights_for, grads_done, fvec):
    w, tie = weights_for(0, x)
    w = dict(w)
    h1, n1, a1, b1, f1 = _ffn_fwd(x, g["ffn1_pre"] + tie, g["ffn1_post"],
                                  w["ffn1_gate"], w["ffn1_up"], w["ffn1_down"], name="ffn1_fwd")
    more, tie = weights_for(1, h1)
    w.update(more)
    qkv, u = _qkv_fwd(h1, g["mix_pre"] + tie, w["in"], name="qkv_fwd")
    bias = _bias_expand(fvec, name="bias_expand")
    o, ltot = _sb_fwd(qkv, name="sb_fwd")
    o = _ch_fwd(qkv, bias, o, name="ch_fwd")
    h2, mixed, y = _out_fwd(o, h1, g["out_sb"], g["out_ch"], g["mix_post"], w["out"], name="out_fwd")
    w.update(weights_for(2, h2)[0])
    h3, n2, a2, b2, f2 = _ffn_fwd(h2, g["ffn2_pre"], g["ffn2_post"],
                                  w["ffn2_gate"], w["ffn2_up"], w["ffn2_down"], name="ffn2_fwd")
    loss, dh3, dwp, dwgate, dg_ple = _ple(p, h3, target, w["ple_proj"], w["ple_gate"],
                                          g["ple_post"], name="ple")
    tie = grads_done(0, {"ple_proj": dwp, "ple_gate": dwgate})

    df2, dg_ffn2_post = _junction(dh3, post=(f2, g["ffn2_post"] + tie, 0.5), name="junction3")
    dwg2, dwu2, dwd2, dn2 = _ffn_bwd(n2, df2, a2, b2, w["ffn2_gate"], w["ffn2_up"],
                                     w["ffn2_down"], name="ffn2_bwd")
    tie = grads_done(1, {"ffn2_gate": dwg2, "ffn2_up": dwu2, "ffn2_down": dwd2})
    dh2, dg_ffn2_pre, dy, dg_mix_post = _junction(
        dh3, pre=(dn2, h2, g["ffn2_pre"] + tie), post=(y, g["mix_post"], 1.0), name="junction2")
    dwout, do, dg_sb, dg_ch = _out_bwd(dy, mixed, o, g["out_sb"], g["out_ch"], w["out"],
                                       name="out_bwd")
    dq, dk, dv = _sb_bwd(qkv, ltot, do, name="sb_bwd")
    dq, dk, dv, dbias = _ch_bwd(qkv, bias, o, do, dq, dk, dv, name="ch_bwd")
    dfvec = _bias_grad(dbias, name="bias_grad")
    dwin, du = _qkv_bwd(dq, dk, dv, u, w["in"], name="qkv_bwd")
    tie = grads_done(2, {"out": dwout, "in": dwin})
    dh1, dg_mix_pre, df1, dg_ffn1_post = _junction(
        dh2, pre=(du, h1, g["mix_pre"] + tie), post=(f1, g["ffn1_post"], 0.5), name="junction1")
    dwg1, dwu1, dwd1, dn1 = _ffn_bwd(n1, df1, a1, b1, w["ffn1_gate"], w["ffn1_up"],
                                     w["ffn1_down"], name="ffn1_bwd")
    tie = grads_done(3, {"ffn1_gate": dwg1, "ffn1_up": dwu1, "ffn1_down": dwd1})
    dx, dg_ffn1_pre = _junction(dh1, pre=(dn1, x, g["ffn1_pre"] + tie), name="junction0")

    dg = {"ffn1_pre": dg_ffn1_pre, "ffn1_post": dg_ffn1_post, "mix_pre": dg_mix_pre,
          "mix_post": dg_mix_post, "out_sb": dg_sb, "out_ch": dg_ch,
          "ffn2_pre": dg_ffn2_pre, "ffn2_post": dg_ffn2_post, "ple_post": dg_ple}
    return loss, dx, dg, dfvec


_WEIGHTS = (
    ("ffn1_gate", "row", FF_SHARD, FF_SHARD_PAD, D_MODEL),
    ("ffn1_up", "row", FF_SHARD, FF_SHARD_PAD, D_MODEL),
    ("ffn1_down", "row", FF_SHARD, FF_SHARD_PAD, D_MODEL),
    ("in", "col", QKV_SHARD, QKV_SHARD, D_MODEL),
    ("out", "row", ROW_SHARD, ROW_SHARD, D_MODEL),
    ("ffn2_gate", "row", FF_SHARD, FF_SHARD_PAD, D_MODEL),
    ("ffn2_up", "row", FF_SHARD, FF_SHARD_PAD, D_MODEL),
    ("ffn2_down", "row", FF_SHARD, FF_SHARD_PAD, D_MODEL),
    ("ple_proj", "col", ROW_SHARD, ROW_SHARD, PLE_DIM),
    ("ple_gate", "row", ROW_SHARD, ROW_SHARD, D_MODEL),
)
_TRANSPOSED = ("ffn1_gate", "ffn1_up", "ffn2_gate", "ffn2_up")
_SPEC = {n: (kind, valid, pad, other) for n, kind, valid, pad, other in _WEIGHTS}
_GATHER_STAGES = (("ffn1_gate", "ffn1_up", "ffn1_down"), ("in", "out"),
                  ("ffn2_gate", "ffn2_up", "ffn2_down", "ple_proj", "ple_gate"))
_SCATTER_STAGES = (("ple_proj", "ple_gate"), ("ffn2_gate", "ffn2_up", "ffn2_down"),
                   ("out", "in"), ("ffn1_gate", "ffn1_up", "ffn1_down"))
HBM = pl.BlockSpec(memory_space=pltpu.HBM)
SEM = pl.BlockSpec(memory_space=pltpu.SEMAPHORE)
EFFECT = pltpu.SideEffectType.DATAFLOW_SIDE_EFFECTING


def _shard_shape(kind, size, other):
    return (other, size) if kind == "col" else (size, other)


def _window(ref, kind, start, size):
    return ref.at[:, pl.ds(start, size)] if kind == "col" else ref.at[pl.ds(start, size), :]


def _device_tuple(k):
    return (k // 4, (k // 2) % 2, k % 2)


def _my_index():
    return 4 * lax.axis_index("x") + 2 * lax.axis_index("y") + lax.axis_index("c")


def _pack_weights(shards):
    nw = len(_WEIGHTS)

    def body(*refs):
        ins, packed, full = refs[:nw], refs[nw:2 * nw], refs[2 * nw:3 * nw]
        sem = refs[3 * nw]
        me = _my_index()
        for (_, kind, valid, pad, _), src, dst in zip(_WEIGHTS, ins, packed):
            if pad != valid:
                dst[...] = jnp.zeros_like(dst)
            if kind == "col":
                dst[:, pl.ds(0, valid)] = _bf(src[...])
            else:
                dst[pl.ds(0, valid), :] = _bf(src[...])
        for k in range(N_DEV):
            @pl.when(me == k)
            def _():
                for w, (_, kind, _, pad, _) in enumerate(_WEIGHTS):
                    pltpu.make_async_copy(packed[w], _window(full[w], kind, k * pad, pad),
                                          sem.at[w]).start()
        for w, (_, kind, _, pad, _) in enumerate(_WEIGHTS):
            pltpu.make_async_copy(packed[w], _window(full[w], kind, 0, pad), sem.at[w]).wait()

    outs = pl.pallas_call(
        body, name="pack_weights",
        in_specs=[VMEM] * nw, out_specs=[VMEM] * nw + [ANY] * nw,
        out_shape=[_sds(_shard_shape(kind, pad, other), BF16) for _, kind, _, pad, other in _WEIGHTS]
        + [_sds(_shard_shape(kind, N_DEV * pad, other), BF16) for _, kind, _, pad, other in _WEIGHTS],
        scratch_shapes=[pltpu.SemaphoreType.DMA((nw,))],
        compiler_params=pltpu.CompilerParams(vmem_limit_bytes=VMEM_LIMIT_BYTES),
    )(*shards)
    names = [n for n, *_ in _WEIGHTS]
    return dict(zip(names, outs[:nw])), dict(zip(names, outs[nw:]))


def _hbm(a):
    return pltpu.with_memory_space_constraint(a, pltpu.HBM)


def _split_start(name, n, body_copies, sources, lands, after):
    def body(*refs):
        src, land = refs[:n], refs[n:2 * n]
        send, recv = refs[2 * n + 1], refs[2 * n + 2]
        token = refs[-1]
        body_copies(src, land, send, recv)
        token[...] = jnp.zeros_like(token)

    arrays = list(sources) + list(lands)
    out = pl.pallas_call(
        body, name=name,
        out_shape=(pltpu.SemaphoreType.DMA((n,)), pltpu.SemaphoreType.DMA((n,)),
                   *[pltpu.HBM(a.shape, a.dtype) for a in arrays], _sds((8, 128))),
        in_specs=[HBM] * (2 * n) + [ANY], out_specs=(SEM, SEM, *[HBM] * (2 * n), VMEM),
        input_output_aliases={i: 2 + i for i in range(2 * n)},
        compiler_params=pltpu.CompilerParams(has_side_effects=EFFECT),
    )(*[_hbm(a) for a in arrays], after)
    return out[0], out[1], out[2:2 + n], out[2 + n:2 + 2 * n], out[-1]


def _split_wait(name, n, seven_of, send, recv, sources, lands, after, keep_sources=False):
    def body(*refs):
        land = refs[n:2 * n]
        send_ref, recv_ref = refs[2 * n], refs[2 * n + 1]
        myself = (lax.axis_index("x"), lax.axis_index("y"), lax.axis_index("c"))
        for w in range(n):
            seven = seven_of(w, land[w])
            copy = pltpu.make_async_remote_copy(
                src_ref=seven, dst_ref=seven, send_sem=send_ref.at[w], recv_sem=recv_ref.at[w],
                device_id=myself, device_id_type=MESH)
            copy.wait_send()
            copy.wait_recv()

    arrays = list(sources) + list(lands)
    out = pl.pallas_call(
        body, name=name,
        out_shape=[pltpu.HBM(a.shape, a.dtype) for a in arrays],
        in_specs=[HBM] * (2 * n) + [SEM, SEM, ANY], out_specs=[HBM] * (2 * n),
        input_output_aliases={i: i for i in range(2 * n)},
        compiler_params=pltpu.CompilerParams(has_side_effects=EFFECT),
    )(*arrays, send, recv, after)
    return out if keep_sources else out[n:]


def _gather_start(stage, names, packed, full, after):
    def copies(src, land, send, recv):
        me = _my_index()
        for k in range(N_DEV):
            @pl.when(me == k)
            def _():
                for w, name in enumerate(names):
                    kind, _, pad, _ = _SPEC[name]
                    dst = _window(land[w], kind, k * pad, pad)
                    for peer in range(N_DEV):
                        if peer != k:
                            pltpu.make_async_remote_copy(
                                src_ref=src[w], dst_ref=dst, send_sem=send.at[w],
                                recv_sem=recv.at[w], device_id=_device_tuple(peer),
                                device_id_type=MESH).start()

    return _split_start(f"gather_start{stage}", len(names), copies,
                        [packed[n] for n in names], [full[n] for n in names], after)


def _gather_wait(stage, names, started, after):
    send, recv, src, land, _ = started

    def seven_of(w, ref):
        kind, _, pad, _ = _SPEC[names[w]]
        return _window(ref, kind, 0, (N_DEV - 1) * pad)

    return dict(zip(names, _split_wait(f"gather_wait{stage}", len(names), seven_of,
                                       send, recv, src, land, after)))


def _scatter_start(stage, names, grads):
    def copies(src, land, send, recv):
        me = _my_index()
        for k in range(N_DEV):
            @pl.when(me != k)
            def _():
                slot = lax.rem(me + (N_DEV - 1 - k), N_DEV)
                for w, name in enumerate(names):
                    kind, _, pad, _ = _SPEC[name]
                    pltpu.make_async_remote_copy(
                        src_ref=_window(src[w], kind, k * pad, pad), dst_ref=land[w].at[slot],
                        send_sem=send.at[w], recv_sem=recv.at[w],
                        device_id=_device_tuple(k), device_id_type=MESH).start()

    lands = [lax.empty((N_DEV - 1,) + _shard_shape(_SPEC[m][0], _SPEC[m][2], _SPEC[m][3]), BF16)
             for m in names]
    return _split_start(f"scatter_start{stage}", len(names), copies, grads, lands, grads[0])


def _scatter_wait(stage, names, started, after):
    send, recv, src, land, _ = started
    n = len(names)
    out = _split_wait(f"scatter_wait{stage}", n, lambda w, ref: ref, send, recv, src, land, after,
                      keep_sources=True)
    return dict(zip(names, out[:n])), dict(zip(names, out[n:]))


def _tie(a, *tokens):
    for tok in tokens:
        a = a + tok[:1, :1]
    return a


def _allreduce_small(small, after):
    shape = small.shape

    def body(in_ref, _after, out_ref, gath, send, recv):
        me = _my_index()
        for k in range(N_DEV):
            @pl.when(me != k)
            def _():
                pltpu.make_async_remote_copy(
                    src_ref=in_ref, dst_ref=gath.at[me], send_sem=send, recv_sem=recv,
                    device_id=_device_tuple(k), device_id_type=MESH).start()

            @pl.when(me == k)
            def _():
                gath[k] = in_ref[...]
        seven = gath.at[pl.ds(0, N_DEV - 1)]
        pltpu.make_async_remote_copy(
            src_ref=seven, dst_ref=seven, send_sem=send, recv_sem=recv,
            device_id=_device_tuple(0), device_id_type=MESH).wait()
        total = gath[0]
        for s in range(1, N_DEV):
            total = total + gath[s]
        out_ref[...] = total

    return pl.pallas_call(
        body, name="allreduce_small",
        in_specs=[VMEM, ANY], out_specs=VMEM, out_shape=_sds(shape),
        scratch_shapes=[pltpu.VMEM((N_DEV,) + shape, F32),
                        pltpu.SemaphoreType.DMA, pltpu.SemaphoreType.DMA],
    )(small, after)


def _adam_update(w_ref, m_ref, v_ref, grad, grad_ref, delta_ref, nm_ref, nv_ref):
    new_m = ADAM_B1 * m_ref[...] + (1.0 - ADAM_B1) * grad
    new_v = ADAM_B2 * v_ref[...] + (1.0 - ADAM_B2) * (grad * grad)
    m_hat = new_m / (1.0 - ADAM_B1 ** ADAM_STEP)
    v_hat = new_v / (1.0 - ADAM_B2 ** ADAM_STEP)
    grad_ref[...] = grad
    delta_ref[...] = -ADAM_LR * (m_hat / (jnp.sqrt(v_hat) + ADAM_EPS) + ADAM_WD * w_ref[...])
    nm_ref[...] = new_m
    nv_ref[...] = new_v


def _adamw(w, m, v, g, *, name):
    def body(w_ref, m_ref, v_ref, g_ref, *outs):
        _adam_update(w_ref, m_ref, v_ref, g_ref[...], *outs)

    return pl.pallas_call(
        body, name=name, in_specs=[VMEM] * 4, out_specs=[VMEM] * 4,
        out_shape=[_sds(w.shape)] * 4,
    )(w, m, v, g)


def _adamw_shard(w, m, v, land, dw_full, *, kind, pad, name):
    shape = w.shape
    other = shape[0] if kind == "col" else shape[1]

    def body(w_ref, m_ref, v_ref, land_ref, own_ref, *outs):
        valid = ((slice(None), pl.ds(0, shape[1])) if kind == "col"
                 else (pl.ds(0, shape[0]), slice(None)))
        grad = own_ref[valid].astype(F32)
        for s in range(N_DEV - 1):
            grad = grad + land_ref[(s,) + valid].astype(F32)
        _adam_update(w_ref, m_ref, v_ref, grad, *outs)

    whole = lambda a: pl.BlockSpec(a.shape, lambda i: (0,) * a.ndim)
    own = pl.BlockSpec(_shard_shape(kind, pad, other),
                       (lambda i: (0, _my_index())) if kind == "col" else (lambda i: (_my_index(), 0)))
    return pl.pallas_call(
        body, name=name, grid=(1,),
        in_specs=[whole(w), whole(m), whole(v), whole(land), own],
        out_specs=[whole(w)] * 4, out_shape=[_sds(shape)] * 4,
        compiler_params=_params("arbitrary"),
    )(w, m, v, land, dw_full)


_GAINS = ("ffn1_pre", "ffn1_post", "mix_pre", "mix_post", "ffn2_pre", "ffn2_post", "ple_post")
_SMALL_ROWS = 16


def _stack_gains(get):
    return jnp.concatenate([get(n) for n in _GAINS]
                           + [jnp.concatenate([get("out_sb"), get("out_ch")], axis=1)], axis=0)


def kernel(x, p, g_ffn1_pre, g_ffn1_post, w_ffn1_gate, w_ffn1_up, w_ffn1_down, g_mix_pre, g_mix_post, w_in, g_out_sb, g_out_ch, rel_bias, w_out, g_ffn2_pre, g_ffn2_post, w_ffn2_gate, w_ffn2_up, w_ffn2_down, w_ple_proj, w_ple_gate, g_ple_post, loss_target, m_g_ffn1_pre, m_g_ffn1_post, m_w_ffn1_gate, m_w_ffn1_up, m_w_ffn1_down, m_g_mix_pre, m_g_mix_post, m_w_in, m_g_out_sb, m_g_out_ch, m_rel_bias, m_w_out, m_g_ffn2_pre, m_g_ffn2_post, m_w_ffn2_gate, m_w_ffn2_up, m_w_ffn2_down, m_w_ple_proj, m_w_ple_gate, m_g_ple_post, v_g_ffn1_pre, v_g_ffn1_post, v_w_ffn1_gate, v_w_ffn1_up, v_w_ffn1_down, v_g_mix_pre, v_g_mix_post, v_w_in, v_g_out_sb, v_g_out_ch, v_rel_bias, v_w_out, v_g_ffn2_pre, v_g_ffn2_post, v_w_ffn2_gate, v_w_ffn2_up, v_w_ffn2_down, v_w_ple_proj, v_w_ple_gate, v_g_ple_post):
    given = dict(locals())
    wnames = [n for n, *_ in _WEIGHTS]

    def shard(prefix, n):
        a = given[prefix + "w_" + n][0]
        return a.T if n in _TRANSPOSED else a

    packed, full = _pack_weights([shard("", n) for n in wnames])
    first = _GATHER_STAGES[0]
    gathers = {0: _gather_start(0, first, packed, full, packed[first[0]])}

    def weights_for(stage, after):
        names = _GATHER_STAGES[stage]
        ws = _gather_wait(stage, names, gathers[stage], after)
        if stage + 1 == len(_GATHER_STAGES):
            return ws, jnp.zeros((1, 1), F32)
        gathers[stage + 1] = _gather_start(stage + 1, _GATHER_STAGES[stage + 1], packed, full,
                                           ws[names[0]])
        return ws, gathers[stage + 1][-1][:1, :1]

    scatters = {}

    def grads_done(stage, grads):
        names = _SCATTER_STAGES[stage]
        scatters[stage] = _scatter_start(stage, names, [grads[n] for n in names])
        return scatters[stage][-1][:1, :1]

    gains = {n: given["g_" + n] for n in _GAINS + ("out_sb", "out_ch")}
    fvec = _rel_bias_to_fvec(rel_bias[0])
    loss, dx, dg, dfvec = _local_step(x[0], p[0, 0], loss_target[0], gains,
                                      weights_for, grads_done, fvec)

    results = {}

    def finish(stage, after):
        names = _SCATTER_STAGES[stage]
        dws, lands = _scatter_wait(stage, names, scatters[stage], after)
        for n in names:
            kind, _, pad, _ = _SPEC[n]
            out = _adamw_shard(shard("", n), shard("m_", n), shard("v_", n), lands[n], dws[n],
                               kind=kind, pad=pad, name="adamw_" + n)
            results["w_" + n] = [a.T for a in out] if n in _TRANSPOSED else out
        return results["w_" + names[-1]][0]

    after = dx
    for stage in range(len(_SCATTER_STAGES)):
        after = finish(stage, after)
    dfv = jnp.pad(dfvec[:, 0, :], ((0, 0), (0, D_MODEL - CH_WIN)))
    small = _allreduce_small(jnp.concatenate([_stack_gains(lambda n: dg[n]), dfv], axis=0), after)
    stacked = _adamw(_stack_gains(lambda n: given["g_" + n]),
                     _stack_gains(lambda n: given["m_g_" + n]),
                     _stack_gains(lambda n: given["v_g_" + n]),
                     small[:N_DEV], name="adamw_gains")
    half = D_MODEL // 2
    for r, n in enumerate(_GAINS):
        results["g_" + n] = [a[r:r + 1] for a in stacked]
    results["g_out_sb"] = [a[N_DEV - 1:N_DEV, :half] for a in stacked]
    results["g_out_ch"] = [a[N_DEV - 1:N_DEV, half:] for a in stacked]
    d_rel = _fvec_grad_to_rel_bias(small[N_DEV:, :CH_WIN].reshape(N_DEV, 1, CH_WIN))
    results["rel_bias"] = _adamw(rel_bias[0], m_rel_bias[0], v_rel_bias[0], d_rel,
                                 name="adamw_rel_bias")

    order = ("g_ffn1_pre", "g_ffn1_post", "w_ffn1_gate", "w_ffn1_up", "w_ffn1_down",
             "g_mix_pre", "g_mix_post", "w_in", "g_out_sb", "g_out_ch", "rel_bias", "w_out",
             "g_ffn2_pre", "g_ffn2_post", "w_ffn2_gate", "w_ffn2_up", "w_ffn2_down",
             "w_ple_proj", "w_ple_gate", "g_ple_post")

    def leaf(name, idx):
        a = results[name][idx]
        return a if name.startswith("g_") else a[None]

    total_loss = lax.psum(loss[0, 0], ("x", "y", "c"))
    return (total_loss, dx[None],
            *[leaf(n, 0) for n in order], *[leaf(n, 1) for n in order],
            *[leaf(n, 2) for n in order], *[leaf(n, 3) for n in order])
```

```python
import functools

import jax
import jax.numpy as jnp
from jax import lax
from jax.experimental import pallas as pl
from jax.experimental.pallas import tpu as pltpu

F32 = jnp.float32
BF16 = jnp.bfloat16

N_DEV = 8
D_MODEL = 1024
D_FF = 2816
FF_SHARD = D_FF // N_DEV
FF_SHARD_PAD = 384
D_FF_PAD = FF_SHARD_PAD * N_DEV
QKV_WIDTH = 3 * D_MODEL
QKV_SHARD = QKV_WIDTH // N_DEV
PLE_DIM = 256
ROW_SHARD = D_MODEL // N_DEV
HEAD_DIM = 64
PAIR = 2 * HEAD_DIM
N_PAIRS = 4
CHUNK = 64
LOOKBACK = 8
REL_CLIP = 128
N_REL = 2 * REL_CLIP + 1
CH_QB = 256
CH_LOOK = LOOKBACK * CHUNK
CH_WIN = CH_LOOK + CH_QB
SB_BLK = 256
EPS = 1e-6
NEG_INF = -1e30
ATT_SCALE = HEAD_DIM ** -0.5
ADAM_LR = 0.001
ADAM_B1 = 0.9
ADAM_B2 = 0.999
ADAM_EPS = 1e-08
ADAM_WD = 0.01
ADAM_STEP = 10
VMEM_LIMIT_BYTES = 48 * 1024 * 1024
MESH = pl.DeviceIdType.MESH

ANY = pl.BlockSpec(memory_space=pl.ANY)
VMEM = pl.BlockSpec(memory_space=pltpu.VMEM)


def _params(*sem):
    return pltpu.CompilerParams(dimension_semantics=sem or None,
                                vmem_limit_bytes=VMEM_LIMIT_BYTES)


def _sds(shape, dtype=F32):
    return jax.ShapeDtypeStruct(shape, dtype)


def _bf(x):
    return x.astype(BF16)


def _dot(a, b):
    return jnp.dot(_bf(a), _bf(b), preferred_element_type=F32)


def _dot_nt(a, b):
    return lax.dot_general(_bf(a), _bf(b), (((1,), (1,)), ((), ())),
                           preferred_element_type=F32)


def _dot_tn(a, b):
    return lax.dot_general(_bf(a), _bf(b), (((0,), (0,)), ((), ())),
                           preferred_element_type=F32)


def _sigmoid(x):
    return 1.0 / (1.0 + jnp.exp(-x))


def _softplus(x):
    return jnp.maximum(x, 0.0) + jnp.log(1.0 + jnp.exp(-jnp.abs(x)))


def _rstd(x):
    return lax.rsqrt(jnp.mean(x * x, axis=-1, keepdims=True) + EPS)


def _rms(x, g):
    return x * _rstd(x) * g


def _rms_bwd(dy, x, g):
    r = _rstd(x)
    w = dy * g
    dx = r * (w - x * (r * r) * jnp.mean(w * x, axis=-1, keepdims=True))
    dg = jnp.sum(dy * (x * r), axis=0, keepdims=True)
    return dx, dg


def _dot_exact01(x, u):
    hi = _bf(x)
    lo = _bf(x - hi.astype(F32))
    return (jnp.dot(hi, u, preferred_element_type=F32)
            + jnp.dot(lo, u, preferred_element_type=F32))


def _head_masks():
    lane = lax.broadcasted_iota(jnp.int32, (1, PAIR), 1)
    return lane < HEAD_DIM, lane >= HEAD_DIM


def _ffn_fwd(x, g_pre, g_post, wg, wu, wd, *, name):
    t = x.shape[0]
    tm, tj = 512, 512
    ni, nj = t // tm, D_FF_PAD // tj

    def body(x_ref, gpre_ref, gpost_ref, wg_ref, wu_ref, wd_ref,
             h_ref, n_ref, a_ref, b_ref, f_ref, acc_ref):
        j = pl.program_id(1)

        @pl.when(j == 0)
        def _():
            n_ref[...] = _bf(_rms(x_ref[...], gpre_ref[...]))
            acc_ref[...] = jnp.zeros_like(acc_ref)

        n = n_ref[...]
        a = _dot_nt(n, wg_ref[...])
        b = _dot_nt(n, wu_ref[...])
        a_ref[...] = a
        b_ref[...] = b
        hmid = a * _sigmoid(a) * b
        acc_ref[...] += jnp.dot(_bf(hmid), wd_ref[...], preferred_element_type=F32)

        @pl.when(j == nj - 1)
        def _():
            f = acc_ref[...]
            f_ref[...] = f
            h_ref[...] = x_ref[...] + 0.5 * _rms(f, gpost_ref[...])

    row = pl.BlockSpec((tm, D_MODEL), lambda i, j: (i, 0))
    gain = pl.BlockSpec((1, D_MODEL), lambda i, j: (0, 0))
    col = pl.BlockSpec((tm, tj), lambda i, j: (i, j))
    wtile = pl.BlockSpec((tj, D_MODEL), lambda i, j: (j, 0))
    return pl.pallas_call(
        body, name=name, grid=(ni, nj),
        in_specs=[row, gain, gain, wtile, wtile, wtile],
        out_specs=[row, row, col, col, row],
        out_shape=[_sds((t, D_MODEL)), _sds((t, D_MODEL), BF16),
                   _sds((t, D_FF_PAD)), _sds((t, D_FF_PAD)), _sds((t, D_MODEL))],
        scratch_shapes=[pltpu.VMEM((tm, D_MODEL), F32)],
        compiler_params=_params("arbitrary", "arbitrary"),
    )(x, g_pre, g_post, wg, wu, wd)


def _ffn_bwd(n, df, a, b, wg, wu, wd, *, name):
    t = n.shape[0]
    tj, ts = 256, 512
    nj, ns = D_FF_PAD // tj, t // ts

    def body(n_hbm, df_hbm, a_ref, b_ref, wg_ref, wu_ref, wd_ref,
             dwg_ref, dwu_ref, dwd_ref, dn_hbm,
             n_v, df_v, dn_v, ag, au, ad, sem):
        j = pl.program_id(0)

        @pl.when(j == 0)
        def _():
            c1 = pltpu.make_async_copy(n_hbm, n_v, sem.at[0])
            c2 = pltpu.make_async_copy(df_hbm, df_v, sem.at[1])
            c1.start()
            c2.start()
            dn_v[...] = jnp.zeros_like(dn_v)
            c1.wait()
            c2.wait()

        ag[...] = jnp.zeros_like(ag)
        au[...] = jnp.zeros_like(au)
        ad[...] = jnp.zeros_like(ad)
        wgj, wuj, wdj = wg_ref[...], wu_ref[...], wd_ref[...]
        for s in range(ns):
            rows = pl.ds(s * ts, ts)
            av, bv = a_ref[rows, :], b_ref[rows, :]
            sig = _sigmoid(av)
            silu = av * sig
            dfr = df_v[rows, :]
            nr = n_v[rows, :]
            dhmid = _dot_nt(dfr, wdj)
            da = dhmid * bv * (sig * (1.0 + av * (1.0 - sig)))
            db = dhmid * silu
            ad[...] += _dot_tn(silu * bv, dfr)
            ag[...] += _dot_tn(da, nr)
            au[...] += _dot_tn(db, nr)
            dn_v[rows, :] += _dot(da, wgj) + _dot(db, wuj)
        dwg_ref[...] = _bf(ag[...])
        dwu_ref[...] = _bf(au[...])
        dwd_ref[...] = _bf(ad[...])

        @pl.when(j == nj - 1)
        def _():
            c = pltpu.make_async_copy(dn_v, dn_hbm, sem.at[0])
            c.start()
            c.wait()

    roww = pl.BlockSpec((tj, D_MODEL), lambda j: (j, 0))
    act = pl.BlockSpec((t, tj), lambda j: (0, j))
    return pl.pallas_call(
        body, name=name, grid=(nj,),
        in_specs=[ANY, ANY, act, act, roww, roww, roww],
        out_specs=[roww, roww, roww, ANY],
        out_shape=[_sds((D_FF_PAD, D_MODEL), BF16)] * 3 + [_sds((t, D_MODEL))],
        scratch_shapes=[pltpu.VMEM((t, D_MODEL), BF16), pltpu.VMEM((t, D_MODEL), BF16),
                        pltpu.VMEM((t, D_MODEL), F32)]
        + [pltpu.VMEM((tj, D_MODEL), F32)] * 3 + [pltpu.SemaphoreType.DMA((2,))],
        compiler_params=_params("arbitrary"),
    )(n, df, a, b, wg, wu, wd)


def _junction(dres, pre=None, post=None, *, name):
    t = dres.shape[0]
    tm = 512
    ni = t // tm
    n_in = 1 + (3 if pre else 0) + (2 if post else 0)
    coef = post[2] if post else None

    def body(*refs):
        ins, outs = list(refs[:n_in]), list(refs[n_in:])
        i = pl.program_id(0)
        dh = ins.pop(0)[...]
        if pre:
            dn_ref, x_ref, gpre_ref = ins.pop(0), ins.pop(0), ins.pop(0)
            dh_ref, dgpre_ref = outs.pop(0), outs.pop(0)
            dx, dg = _rms_bwd(dn_ref[...], x_ref[...], gpre_ref[...])
            dh = dh + dx
            dh_ref[...] = dh

            @pl.when(i == 0)
            def _():
                dgpre_ref[...] = jnp.zeros_like(dgpre_ref)
            dgpre_ref[...] += dg
        if post:
            f_ref, gpost_ref = ins.pop(0), ins.pop(0)
            df_ref, dgpost_ref = outs.pop(0), outs.pop(0)
            df, dg = _rms_bwd(coef * dh, f_ref[...], gpost_ref[...])
            df_ref[...] = _bf(df)

            @pl.when(i == 0)
            def _():
                dgpost_ref[...] = jnp.zeros_like(dgpost_ref)
            dgpost_ref[...] += dg

    row = pl.BlockSpec((tm, D_MODEL), lambda i: (i, 0))
    gain = pl.BlockSpec((1, D_MODEL), lambda i: (0, 0))
    args, in_specs, out_specs, out_shape = [dres], [row], [], []
    if pre:
        args += list(pre)
        in_specs += [row, row, gain]
        out_specs += [row, gain]
        out_shape += [_sds((t, D_MODEL)), _sds((1, D_MODEL))]
    if post:
        args += [post[0], post[1]]
        in_specs += [row, gain]
        out_specs += [row, gain]
        out_shape += [_sds((t, D_MODEL), BF16), _sds((1, D_MODEL))]
    return pl.pallas_call(
        body, name=name, grid=(ni,), in_specs=in_specs, out_specs=out_specs,
        out_shape=out_shape, compiler_params=_params("arbitrary"),
    )(*args)


def _qkv_fwd(h, g, win, *, name):
    t = h.shape[0]
    tm, tn = 512, 768
    ni, nj = t // tm, QKV_WIDTH // tn

    def body(h_ref, g_ref, w_ref, qkv_ref, u_ref):
        @pl.when(pl.program_id(1) == 0)
        def _():
            u_ref[...] = _bf(_rms(h_ref[...], g_ref[...]))
        qkv_ref[...] = jnp.dot(u_ref[...], w_ref[...], preferred_element_type=F32)

    row = pl.BlockSpec((tm, D_MODEL), lambda i, j: (i, 0))
    return pl.pallas_call(
        body, name=name, grid=(ni, nj),
        in_specs=[row, pl.BlockSpec((1, D_MODEL), lambda i, j: (0, 0)),
                  pl.BlockSpec((D_MODEL, tn), lambda i, j: (0, j))],
        out_specs=[pl.BlockSpec((tm, tn), lambda i, j: (i, j)), row],
        out_shape=[_sds((t, QKV_WIDTH)), _sds((t, D_MODEL), BF16)],
        compiler_params=_params("arbitrary", "arbitrary"),
    )(h, g, win)


def _qkv_bwd(dq, dk, dv, u, win, *, name):
    t = u.shape[0]
    tn, ts = 512, 512
    nj, ns = QKV_WIDTH // tn, t // ts

    def body(dq_ref, dk_ref, dv_ref, u_ref, w_ref, dw_ref, du_hbm, du_v, acc_ref, sem):
        j = pl.program_id(0)

        @pl.when(j == 0)
        def _():
            du_v[...] = jnp.zeros_like(du_v)

        wj = w_ref[...]
        for role, d_ref in enumerate((dq_ref, dk_ref, dv_ref)):
            @pl.when(j % 3 == role)
            def _():
                acc_ref[...] = jnp.zeros_like(acc_ref)
                for s in range(ns):
                    rows = pl.ds(s * ts, ts)
                    dcol = d_ref[rows, :]
                    acc_ref[...] += _dot_tn(u_ref[rows, :], dcol)
                    du_v[rows, :] += _dot_nt(dcol, wj)
                dw_ref[...] = _bf(acc_ref[...])

        @pl.when(j == nj - 1)
        def _():
            c = pltpu.make_async_copy(du_v, du_hbm, sem)
            c.start()
            c.wait()

    colw = pl.BlockSpec((D_MODEL, tn), lambda j: (0, j))
    grp = pl.BlockSpec((t, tn), lambda j: (0, j // 3))
    return pl.pallas_call(
        body, name=name, grid=(nj,),
        in_specs=[grp, grp, grp, pl.BlockSpec((t, D_MODEL), lambda j: (0, 0)), colw],
        out_specs=[colw, ANY],
        out_shape=[_sds((D_MODEL, QKV_WIDTH), BF16), _sds((t, D_MODEL))],
        scratch_shapes=[pltpu.VMEM((t, D_MODEL), F32), pltpu.VMEM((D_MODEL, tn), F32),
                        pltpu.SemaphoreType.DMA],
        compiler_params=_params("arbitrary"),
    )(dq, dk, dv, u, win)


def _sb_stack(x):
    lo, hi = _head_masks()
    return jnp.concatenate([jnp.where(lo, x, 0.0), jnp.where(hi, x, 0.0)], axis=0)


def _sb_unstack(x2, blk):
    return jnp.where(_head_masks()[0], x2[:blk], x2[blk:])


def _sb_diag_mask(blk):
    r = lax.broadcasted_iota(jnp.int32, (2 * blk, blk), 0) & (blk - 1)
    c = lax.broadcasted_iota(jnp.int32, (2 * blk, blk), 1)
    return c < r


def _tri(n, keep):
    r = lax.broadcasted_iota(jnp.int32, (n, n), 0)
    c = lax.broadcasted_iota(jnp.int32, (n, n), 1)
    return jnp.where(keep(r, c), 1.0, 0.0).astype(BF16)


def _cumsum01(x, u):
    m = x.shape[0]
    hi = _bf(x)
    lo = _bf(x - hi.astype(F32))
    both = jnp.dot(jnp.concatenate([hi, lo], axis=0), u, preferred_element_type=F32)
    return both[:m] + both[m:]


def _sb_fwd(qkv, *, name):
    t = qkv.shape[0]
    blk = SB_BLK
    ni = t // blk

    def body(q_ref, k_ref, v_ref, o_ref, ltot_ref):
        i = pl.program_id(1)
        u_after = _tri(blk, lambda r, c: r > c)
        q2 = _bf(_sb_stack(q_ref[...] * ATT_SCALE))

        def tile(k0, mask, acc, c_l):
            kj = k_ref[pl.ds(k0, blk), :]
            vj = v_ref[pl.ds(k0, blk), :]
            z = _dot_nt(q2, kj)
            sp = _softplus(z)
            lf = -sp if mask is None else jnp.where(mask, -sp, 0.0)
            a = jnp.exp(z - sp + _cumsum01(lf, u_after) + c_l)
            if mask is not None:
                a = jnp.where(mask, a, 0.0)
            return acc + _dot(a, vj), c_l + jnp.sum(lf, axis=1, keepdims=True)

        carry = tile(pl.multiple_of(i * blk, blk), _sb_diag_mask(blk),
                     jnp.zeros((2 * blk, PAIR), F32), jnp.zeros((2 * blk, 1), F32))
        acc, c_l = lax.fori_loop(
            1, i + 1,
            lambda jj, c: tile(pl.multiple_of((i - jj) * blk, blk), None, *c), carry)
        o_ref[...] = _sb_unstack(acc, blk)
        ltot_ref[...] = _sb_unstack(jnp.broadcast_to(c_l, (2 * blk, PAIR)), blk)

    blkspec = pl.BlockSpec((blk, PAIR), lambda p, i: (i, p))
    return pl.pallas_call(
        body, name=name, grid=(N_PAIRS, ni),
        in_specs=[blkspec,
                  pl.BlockSpec((t, PAIR), lambda p, i: (0, N_PAIRS + p)),
                  pl.BlockSpec((t, PAIR), lambda p, i: (0, 2 * N_PAIRS + p))],
        out_specs=[blkspec, blkspec],
        out_shape=[_sds((t, D_MODEL)), _sds((t, D_MODEL // 2))],
        compiler_params=_params("arbitrary", "arbitrary"),
    )(qkv, qkv, qkv)


def _sb_bwd(qkv, ltot, do, *, name):
    t = qkv.shape[0]
    blk = SB_BLK
    ni = t // blk

    def body(q_ref, k_ref, v_ref, lt_ref, do_ref, dq_ref, dkout_ref, dvout_ref, dk_ref, dv_ref):
        i = pl.program_id(1)

        @pl.when(i == 0)
        def _():
            dk_ref[...] = jnp.zeros_like(dk_ref)
            dv_ref[...] = jnp.zeros_like(dv_ref)

        u_upto = _tri(blk, lambda r, c: r <= c)
        u_before = _tri(blk, lambda r, c: r < c)
        lane = lax.broadcasted_iota(jnp.int32, (1, PAIR), 1)
        q2 = _bf(_sb_stack(q_ref[...] * ATT_SCALE))
        do2 = _bf(_sb_stack(do_ref[...]))
        lt_blk = lt_ref[...]
        total = jnp.concatenate(
            [jnp.sum(jnp.where(lane == h * HEAD_DIM, lt_blk, 0.0), axis=1, keepdims=True)
             for h in range(2)], axis=0)

        def tile(k0, mask, dq_acc, c_l, c_g):
            krows = pl.ds(k0, blk)
            kj = k_ref[krows, :]
            vj = v_ref[krows, :]
            z = _dot_nt(q2, kj)
            sp = _softplus(z)
            sig = jnp.exp(z - sp)
            lf = -sp if mask is None else jnp.where(mask, -sp, 0.0)
            a = jnp.exp(z - sp + total - (_cumsum01(lf, u_upto) + c_l))
            if mask is not None:
                a = jnp.where(mask, a, 0.0)
            g = a * _dot_nt(do2, vj)
            g_before = jnp.dot(_bf(g), u_before, preferred_element_type=F32) + c_g
            dz = g * (1.0 - sig) - g_before * sig
            if mask is not None:
                dz = jnp.where(mask, dz, 0.0)
            dk_ref[krows, :] += _dot_tn(dz, q2)
            dv_ref[krows, :] += _dot_tn(a, do2)
            return (dq_acc + _dot(dz, kj), c_l + jnp.sum(lf, axis=1, keepdims=True),
                    c_g + jnp.sum(g, axis=1, keepdims=True))

        carry = lax.fori_loop(
            0, i, lambda j, c: tile(pl.multiple_of(j * blk, blk), None, *c),
            (jnp.zeros((2 * blk, PAIR), F32), jnp.zeros((2 * blk, 1), F32),
             jnp.zeros((2 * blk, 1), F32)))
        dq_acc, _, _ = tile(pl.multiple_of(i * blk, blk), _sb_diag_mask(blk), *carry)
        dq_ref[...] = _bf(_sb_unstack(dq_acc, blk) * ATT_SCALE)

        @pl.when(i == ni - 1)
        def _():
            dkout_ref[...] = _bf(dk_ref[...])
            dvout_ref[...] = _bf(dv_ref[...])

    blkspec = lambda off: pl.BlockSpec((blk, PAIR), lambda p, i: (i, off + p))
    full = lambda off: pl.BlockSpec((t, PAIR), lambda p, i: (0, off + p))
    return pl.pallas_call(
        body, name=name, grid=(N_PAIRS, ni),
        in_specs=[blkspec(0), full(N_PAIRS), full(2 * N_PAIRS), blkspec(0), blkspec(0)],
        out_specs=[blkspec(0), full(0), full(0)],
        out_shape=[_sds((t, D_MODEL), BF16)] * 3,
        scratch_shapes=[pltpu.VMEM((t, PAIR), F32), pltpu.VMEM((t, PAIR), F32)],
        compiler_params=_params("arbitrary", "arbitrary"),
    )(qkv, qkv, qkv, ltot, do)


def _ch_mask(i):
    r = lax.broadcasted_iota(jnp.int32, (CH_QB, CH_WIN), 0)
    c = lax.broadcasted_iota(jnp.int32, (CH_QB, CH_WIN), 1)
    qc = LOOKBACK + lax.shift_right_arithmetic(r, 6)
    kc = lax.shift_right_arithmetic(c, 6)
    first = i * (CH_QB // CHUNK) - LOOKBACK
    return (kc <= qc) & (kc >= qc - LOOKBACK) & (kc + first >= 0)


def _ch_probs(qm, kw, bias_h, mask):
    z = _dot_nt(qm, kw) * ATT_SCALE + bias_h
    z = jnp.where(mask, z, NEG_INF)
    e = jnp.exp(z - jnp.max(z, axis=1, keepdims=True))
    return e / jnp.sum(e, axis=1, keepdims=True)


def _ch_fill(pad_ref, src_ref, t):
    pad_ref[pl.ds(0, CH_LOOK), :] = jnp.zeros((CH_LOOK, PAIR), BF16)
    pad_ref[pl.ds(CH_LOOK, t), :] = _bf(src_ref[...])


def _ch_fwd(qkv, bias, o_in, *, name):
    t = qkv.shape[0]
    ni = t // CH_QB

    def body(q_ref, k_ref, v_ref, bias_ref, _alias, o_ref, kpad, vpad):
        i = pl.program_id(1)

        @pl.when(i == 0)
        def _():
            _ch_fill(kpad, k_ref, t)
            _ch_fill(vpad, v_ref, t)

        win = pl.ds(pl.multiple_of(i * CH_QB, CH_QB), CH_WIN)
        kw, vw = kpad[win, :], vpad[win, :]
        mask = _ch_mask(i)
        q = q_ref[...]
        outs = []
        for h, hm in enumerate(_head_masks()):
            p = _ch_probs(jnp.where(hm, q, 0.0), kw, bias_ref[h], mask)
            outs.append(_dot(p, vw))
        o_ref[...] = jnp.where(_head_masks()[0], outs[0], outs[1])

    full = lambda off: pl.BlockSpec((t, PAIR), lambda p, i: (0, off + p))
    return pl.pallas_call(
        body, name=name, grid=(N_PAIRS, ni),
        in_specs=[pl.BlockSpec((CH_QB, PAIR), lambda p, i: (i, 3 * N_PAIRS + p)),
                  full(4 * N_PAIRS), full(5 * N_PAIRS),
                  pl.BlockSpec((2, CH_QB, CH_WIN), lambda p, i: (p, 0, 0)), ANY],
        out_specs=pl.BlockSpec((CH_QB, PAIR), lambda p, i: (i, N_PAIRS + p)),
        out_shape=_sds((t, D_MODEL)),
        scratch_shapes=[pltpu.VMEM((t + CH_LOOK, PAIR), BF16)] * 2,
        input_output_aliases={4: 0},
        compiler_params=_params("arbitrary", "arbitrary"),
    )(qkv, qkv, qkv, bias, o_in)


def _ch_bwd(qkv, bias, o, do, dq_in, dk_in, dv_in, *, name):
    t = qkv.shape[0]
    ni = t // CH_QB

    def body(q_ref, k_ref, v_ref, bias_ref, o_ref, do_ref, _a0, _a1, _a2,
             dq_ref, dkout_ref, dvout_ref, dbias_ref, kpad, vpad, dkpad, dvpad):
        i = pl.program_id(1)

        @pl.when(i == 0)
        def _():
            _ch_fill(kpad, k_ref, t)
            _ch_fill(vpad, v_ref, t)
            dkpad[...] = jnp.zeros_like(dkpad)
            dvpad[...] = jnp.zeros_like(dvpad)
            dbias_ref[...] = jnp.zeros_like(dbias_ref)

        win = pl.ds(pl.multiple_of(i * CH_QB, CH_QB), CH_WIN)
        kw, vw = kpad[win, :], vpad[win, :]
        mask = _ch_mask(i)
        q, o_blk, do_blk = q_ref[...], o_ref[...], do_ref[...]
        dqs = []
        for h, hm in enumerate(_head_masks()):
            qm = _bf(jnp.where(hm, q, 0.0))
            dom = jnp.where(hm, do_blk, 0.0)
            delta = jnp.sum(dom * o_blk, axis=1, keepdims=True)
            dom = _bf(dom)
            p = _ch_probs(qm, kw, bias_ref[h], mask)
            ds = p * (_dot_nt(dom, vw) - delta)
            dbias_ref[h] += ds
            dsz = ds * ATT_SCALE
            dqs.append(_dot(dsz, kw))
            dkpad[win, :] += _dot_tn(dsz, qm)
            dvpad[win, :] += _dot_tn(p, dom)
        dq_ref[...] = _bf(jnp.where(_head_masks()[0], dqs[0], dqs[1]))

        @pl.when(i == ni - 1)
        def _():
            dkout_ref[...] = _bf(dkpad[pl.ds(CH_LOOK, t), :])
            dvout_ref[...] = _bf(dvpad[pl.ds(CH_LOOK, t), :])

    blkspec = lambda off: pl.BlockSpec((CH_QB, PAIR), lambda p, i: (i, off + p))
    full = lambda off: pl.BlockSpec((t, PAIR), lambda p, i: (0, off + p))
    bias_spec = pl.BlockSpec((2, CH_QB, CH_WIN), lambda p, i: (p, 0, 0))
    return pl.pallas_call(
        body, name=name, grid=(N_PAIRS, ni),
        in_specs=[blkspec(3 * N_PAIRS), full(4 * N_PAIRS), full(5 * N_PAIRS), bias_spec,
                  blkspec(N_PAIRS), blkspec(N_PAIRS), ANY, ANY, ANY],
        out_specs=[blkspec(N_PAIRS), full(N_PAIRS), full(N_PAIRS), bias_spec],
        out_shape=[_sds((t, D_MODEL), BF16)] * 3 + [_sds((2 * N_PAIRS, CH_QB, CH_WIN))],
        scratch_shapes=[pltpu.VMEM((t + CH_LOOK, PAIR), BF16)] * 2
        + [pltpu.VMEM((t + CH_LOOK, PAIR), F32)] * 2,
        input_output_aliases={6: 0, 7: 1, 8: 2},
        compiler_params=_params("arbitrary", "arbitrary"),
    )(qkv, qkv, qkv, bias, o, do, dq_in, dk_in, dv_in)


def _bias_expand(fvec, *, name):
    n_heads = fvec.shape[0]

    def body(f_ref, o_ref, rows8):
        row = f_ref[0]
        for r in range(8):
            rows8[pl.ds(r, 1), :] = pltpu.roll(row, r, 1)
        base = rows8[...]
        for blk in range(CH_QB // 8):
            o_ref[0, pl.ds(8 * blk, 8), :] = pltpu.roll(base, 8 * blk, 1)

    return pl.pallas_call(
        body, name=name, grid=(n_heads,),
        in_specs=[pl.BlockSpec((1, 1, CH_WIN), lambda h: (h, 0, 0))],
        out_specs=pl.BlockSpec((1, CH_QB, CH_WIN), lambda h: (h, 0, 0)),
        out_shape=_sds((n_heads, CH_QB, CH_WIN)),
        scratch_shapes=[pltpu.VMEM((8, CH_WIN), F32)],
        compiler_params=_params("arbitrary"),
    )(fvec)


def _bias_grad(dbias, *, name):
    n_heads = dbias.shape[0]
    first = CH_LOOK - REL_CLIP

    def body(d_ref, o_ref, acc8):
        acc = jnp.zeros((8, CH_WIN), F32)
        for blk in range(CH_QB // 8):
            acc = acc + pltpu.roll(d_ref[0, pl.ds(8 * blk, 8), :], (CH_WIN - 8 * blk) % CH_WIN, 1)
        acc8[...] = acc
        dvec = jnp.zeros((1, CH_WIN), F32)
        for r in range(8):
            dvec = dvec + pltpu.roll(acc8[pl.ds(r, 1), :], (CH_WIN - r) % CH_WIN, 1)
        lane = lax.broadcasted_iota(jnp.int32, (1, CH_WIN), 1)
        clipped = (lane <= first) | (lane >= first + REL_CLIP + CHUNK)
        total = jnp.sum(jnp.where(clipped, dvec, 0.0), axis=1, keepdims=True)
        o_ref[0] = jnp.where(lane == first, total, dvec)

    return pl.pallas_call(
        body, name=name, grid=(n_heads,),
        in_specs=[pl.BlockSpec((1, CH_QB, CH_WIN), lambda h: (h, 0, 0))],
        out_specs=pl.BlockSpec((1, 1, CH_WIN), lambda h: (h, 0, 0)),
        out_shape=_sds((n_heads, 1, CH_WIN)),
        scratch_shapes=[pltpu.VMEM((8, CH_WIN), F32)],
        compiler_params=_params("arbitrary"),
    )(dbias)


def _out_fwd(o, h1, g_sb, g_ch, g_post, wout, *, name):
    t = o.shape[0]
    tm = 512
    half = D_MODEL // 2

    def body(o_ref, h_ref, gsb_ref, gch_ref, gpost_ref, w_ref, h2_ref, mixed_ref, y_ref):
        ov = o_ref[...]
        mixed = jnp.concatenate([_rms(ov[:, :half], gsb_ref[...]),
                                 _rms(ov[:, half:], gch_ref[...])], axis=1)
        mixed_ref[...] = _bf(mixed)
        y = _dot(mixed, w_ref[...])
        y_ref[...] = y
        h2_ref[...] = h_ref[...] + _rms(y, gpost_ref[...])

    row = pl.BlockSpec((tm, D_MODEL), lambda i: (i, 0))
    gain = lambda n: pl.BlockSpec((1, n), lambda i: (0, 0))
    return pl.pallas_call(
        body, name=name, grid=(t // tm,),
        in_specs=[row, row, gain(half), gain(half), gain(D_MODEL),
                  pl.BlockSpec((D_MODEL, D_MODEL), lambda i: (0, 0))],
        out_specs=[row, row, row],
        out_shape=[_sds((t, D_MODEL)), _sds((t, D_MODEL), BF16), _sds((t, D_MODEL))],
        compiler_params=_params("arbitrary"),
    )(o, h1, g_sb, g_ch, g_post, wout)


def _out_bwd(dy, mixed, o, g_sb, g_ch, wout, *, name):
    t = o.shape[0]
    tm = 512
    ni = t // tm
    half = D_MODEL // 2

    def body(dy_ref, mixed_ref, o_ref, gsb_ref, gch_ref, w_ref,
             dw_ref, do_ref, dgsb_ref, dgch_ref, acc_ref):
        i = pl.program_id(0)

        @pl.when(i == 0)
        def _():
            acc_ref[...] = jnp.zeros_like(acc_ref)
            dgsb_ref[...] = jnp.zeros_like(dgsb_ref)
            dgch_ref[...] = jnp.zeros_like(dgch_ref)

        dyv = dy_ref[...]
        acc_ref[...] += _dot_tn(mixed_ref[...], dyv)
        dm = _dot_nt(dyv, w_ref[...])
        ov = o_ref[...]
        doa, dga = _rms_bwd(dm[:, :half], ov[:, :half], gsb_ref[...])
        dob, dgb = _rms_bwd(dm[:, half:], ov[:, half:], gch_ref[...])
        do_ref[...] = jnp.concatenate([doa, dob], axis=1)
        dgsb_ref[...] += dga
        dgch_ref[...] += dgb

        @pl.when(i == ni - 1)
        def _():
            dw_ref[...] = _bf(acc_ref[...])

    row = pl.BlockSpec((tm, D_MODEL), lambda i: (i, 0))
    gain = pl.BlockSpec((1, half), lambda i: (0, 0))
    sq = pl.BlockSpec((D_MODEL, D_MODEL), lambda i: (0, 0))
    return pl.pallas_call(
        body, name=name, grid=(ni,),
        in_specs=[row, row, row, gain, gain, sq],
        out_specs=[sq, row, gain, gain],
        out_shape=[_sds((D_MODEL, D_MODEL), BF16), _sds((t, D_MODEL)),
                   _sds((1, half)), _sds((1, half))],
        scratch_shapes=[pltpu.VMEM((D_MODEL, D_MODEL), F32)],
        compiler_params=_params("arbitrary"),
    )(dy, mixed, o, g_sb, g_ch, wout)


def _ple(p, h3, target, wp, wgate, g, *, name):
    t = h3.shape[0]
    tm = 512
    ni = t // tm

    def body(p_ref, h_ref, tgt_ref, wp_ref, wg_ref, g_ref,
             loss_ref, dres_ref, dwp_ref, dwg_ref, dg_ref, accp, accg):
        i = pl.program_id(0)

        @pl.when(i == 0)
        def _():
            loss_ref[...] = jnp.zeros_like(loss_ref)
            dg_ref[...] = jnp.zeros_like(dg_ref)
            accp[...] = jnp.zeros_like(accp)
            accg[...] = jnp.zeros_like(accg)

        pv, hv, gv = p_ref[...], h_ref[...], g_ref[...]
        pe = _dot(pv, wp_ref[...])
        sig = _sigmoid(_dot(hv, wg_ref[...]))
        e = pe * sig
        err = hv + _rms(e, gv) - tgt_ref[...]
        tok = jnp.mean(err * err, axis=-1, keepdims=True)
        loss_ref[...] += 0.5 * jnp.sum(tok, axis=0, keepdims=True)
        dh4 = err * (1.0 / D_MODEL)
        de, dg = _rms_bwd(dh4, e, gv)
        dg_ref[...] += dg
        dpe = de * sig
        dgt = de * pe * sig * (1.0 - sig)
        accp[...] += _dot_tn(pv, dpe)
        accg[...] += _dot_tn(hv, dgt)
        dres_ref[...] = dh4 + _dot_nt(dgt, wg_ref[...])

        @pl.when(i == ni - 1)
        def _():
            dwp_ref[...] = _bf(accp[...])
            dwg_ref[...] = _bf(accg[...])

    row = pl.BlockSpec((tm, D_MODEL), lambda i: (i, 0))
    const = lambda r, c: pl.BlockSpec((r, c), lambda i: (0, 0))
    return pl.pallas_call(
        body, name=name, grid=(ni,),
        in_specs=[pl.BlockSpec((tm, PLE_DIM), lambda i: (i, 0)), row, row,
                  const(PLE_DIM, D_MODEL), const(D_MODEL, D_MODEL), const(1, D_MODEL)],
        out_specs=[const(1, 128), row, const(PLE_DIM, D_MODEL), const(D_MODEL, D_MODEL),
                   const(1, D_MODEL)],
        out_shape=[_sds((1, 128)), _sds((t, D_MODEL)), _sds((PLE_DIM, D_MODEL), BF16),
                   _sds((D_MODEL, D_MODEL), BF16), _sds((1, D_MODEL))],
        scratch_shapes=[pltpu.VMEM((PLE_DIM, D_MODEL), F32), pltpu.VMEM((D_MODEL, D_MODEL), F32)],
        compiler_params=_params("arbitrary"),
    )(p, h3, target, wp, wgate, g)


def _rel_bias_to_fvec(rel_bias):
    rev = rel_bias[:, ::-1]
    n_heads = rel_bias.shape[0]
    first = CH_LOOK - REL_CLIP
    n_var = REL_CLIP + CHUNK
    clipped = rev[:, :1]
    fvec = jnp.concatenate([jnp.broadcast_to(clipped, (n_heads, first)), rev[:, :n_var],
                            jnp.broadcast_to(clipped, (n_heads, CH_WIN - first - n_var))], axis=1)
    return fvec.reshape(n_heads, 1, CH_WIN)


def _fvec_grad_to_rel_bias(dfvec):
    first = CH_LOOK - REL_CLIP
    n_var = REL_CLIP + CHUNK
    rev = jnp.pad(dfvec[:, 0, first:first + n_var], ((0, 0), (0, N_REL - n_var)))
    return rev[:, ::-1]


def _local_step(x, p, target, g, weights_for, grads_done, fvec):
    w, tie = weights_for(0, x)
    w = dict(w)
    h1, n1, a1, b1, f1 = _ffn_fwd(x, g["ffn1_pre"] + tie, g["ffn1_post"],
                                  w["ffn1_gate"], w["ffn1_up"], w["ffn1_down"], name="ffn1_fwd")
    more, tie = weights_for(1, h1)
    w.update(more)
    qkv, u = _qkv_fwd(h1, g["mix_pre"] + tie, w["in"], name="qkv_fwd")
    bias = _bias_expand(fvec, name="bias_expand")
    o, ltot = _sb_fwd(qkv, name="sb_fwd")
    o = _ch_fwd(qkv, bias, o, name="ch_fwd")
    h2, mixed, y = _out_fwd(o, h1, g["out_sb"], g["out_ch"], g["mix_post"], w["out"], name="out_fwd")
    w.update(weights_for(2, h2)[0])
    h3, n2, a2, b2, f2 = _ffn_fwd(h2, g["ffn2_pre"], g["ffn2_post"],
                                  w["ffn2_gate"], w["ffn2_up"], w["ffn2_down"], name="ffn2_fwd")
    loss, dh3, dwp, dwgate, dg_ple = _ple(p, h3, target, w["ple_proj"], w["ple_gate"],
                                          g["ple_post"], name="ple")
    tie = grads_done(0, {"ple_proj": dwp, "ple_gate": dwgate})

    df2, dg_ffn2_post = _junction(dh3, post=(f2, g["ffn2_post"] + tie, 0.5), name="junction3")
    dwg2, dwu2, dwd2, dn2 = _ffn_bwd(n2, df2, a2, b2, w["ffn2_gate"], w["ffn2_up"],
                                     w["ffn2_down"], name="ffn2_bwd")
    tie = grads_done(1, {"ffn2_gate": dwg2, "ffn2_up": dwu2, "ffn2_down": dwd2})
    dh2, dg_ffn2_pre, dy, dg_mix_post = _junction(
        dh3, pre=(dn2, h2, g["ffn2_pre"] + tie), post=(y, g["mix_post"], 1.0), name="junction2")
    dwout, do, dg_sb, dg_ch = _out_bwd(dy, mixed, o, g["out_sb"], g["out_ch"], w["out"],
                                       name="out_bwd")
    dq, dk, dv = _sb_bwd(qkv, ltot, do, name="sb_bwd")
    dq, dk, dv, dbias = _ch_bwd(qkv, bias, o, do, dq, dk, dv, name="ch_bwd")
    dfvec = _bias_grad(dbias, name="bias_grad")
    dwin, du = _qkv_bwd(dq, dk, dv, u, w["in"], name="qkv_bwd")
    tie = grads_done(2, {"out": dwout, "in": dwin})
    dh1, dg_mix_pre, df1, dg_ffn1_post = _junction(
        dh2, pre=(du, h1, g["mix_pre"] + tie), post=(f1, g["ffn1_post"], 0.5), name="junction1")
    dwg1, dwu1, dwd1, dn1 = _ffn_bwd(n1, df1, a1, b1, w["ffn1_gate"], w["ffn1_up"],
                                     w["ffn1_down"], name="ffn1_bwd")
    tie = grads_done(3, {"ffn1_gate": dwg1, "ffn1_up": dwu1, "ffn1_down": dwd1})
    dx, dg_ffn1_pre = _junction(dh1, pre=(dn1, x, g["ffn1_pre"] + tie), name="junction0")

    dg = {"ffn1_pre": dg_ffn1_pre, "ffn1_post": dg_ffn1_post, "mix_pre": dg_mix_pre,
          "mix_post": dg_mix_post, "out_sb": dg_sb, "out_ch": dg_ch,
          "ffn2_pre": dg_ffn2_pre, "ffn2_post": dg_ffn2_post, "ple_post": dg_ple}
    return loss, dx, dg, dfvec


_WEIGHTS = (
    ("ffn1_gate", "row", FF_SHARD, FF_SHARD_PAD, D_MODEL),
    ("ffn1_up", "row", FF_SHARD, FF_SHARD_PAD, D_MODEL),
    ("ffn1_down", "row", FF_SHARD, FF_SHARD_PAD, D_MODEL),
    ("in", "col", QKV_SHARD, QKV_SHARD, D_MODEL),
    ("out", "row", ROW_SHARD, ROW_SHARD, D_MODEL),
    ("ffn2_gate", "row", FF_SHARD, FF_SHARD_PAD, D_MODEL),
    ("ffn2_up", "row", FF_SHARD, FF_SHARD_PAD, D_MODEL),
    ("ffn2_down", "row", FF_SHARD, FF_SHARD_PAD, D_MODEL),
    ("ple_proj", "col", ROW_SHARD, ROW_SHARD, PLE_DIM),
    ("ple_gate", "row", ROW_SHARD, ROW_SHARD, D_MODEL),
)
_TRANSPOSED = ("ffn1_gate", "ffn1_up", "ffn2_gate", "ffn2_up")
_SPEC = {n: (kind, valid, pad, other) for n, kind, valid, pad, other in _WEIGHTS}
_GATHER_STAGES = (("ffn1_gate", "ffn1_up", "ffn1_down"), ("in", "out"),
                  ("ffn2_gate", "ffn2_up", "ffn2_down", "ple_proj", "ple_gate"))
_SCATTER_STAGES = (("ple_proj", "ple_gate"), ("ffn2_gate", "ffn2_up", "ffn2_down"),
                   ("out", "in"), ("ffn1_gate", "ffn1_up", "ffn1_down"))
HBM = pl.BlockSpec(memory_space=pltpu.HBM)
SEM = pl.BlockSpec(memory_space=pltpu.SEMAPHORE)
EFFECT = pltpu.SideEffectType.DATAFLOW_SIDE_EFFECTING


def _shard_shape(kind, size, other):
    return (other, size) if kind == "col" else (size, other)


def _window(ref, kind, start, size):
    return ref.at[:, pl.ds(start, size)] if kind == "col" else ref.at[pl.ds(start, size), :]


def _device_tuple(k):
    return (k // 4, (k // 2) % 2, k % 2)


def _my_index():
    return 4 * lax.axis_index("x") + 2 * lax.axis_index("y") + lax.axis_index("c")


def _pack_weights(shards):
    nw = len(_WEIGHTS)

    def body(*refs):
        ins, packed, full = refs[:nw], refs[nw:2 * nw], refs[2 * nw:3 * nw]
        sem = refs[3 * nw]
        me = _my_index()
        for (_, kind, valid, pad, _), src, dst in zip(_WEIGHTS, ins, packed):
            if pad != valid:
                dst[...] = jnp.zeros_like(dst)
            if kind == "col":
                dst[:, pl.ds(0, valid)] = _bf(src[...])
            else:
                dst[pl.ds(0, valid), :] = _bf(src[...])
        for k in range(N_DEV):
            @pl.when(me == k)
            def _():
                for w, (_, kind, _, pad, _) in enumerate(_WEIGHTS):
                    pltpu.make_async_copy(packed[w], _window(full[w], kind, k * pad, pad),
                                          sem.at[w]).start()
        for w, (_, kind, _, pad, _) in enumerate(_WEIGHTS):
            pltpu.make_async_copy(packed[w], _window(full[w], kind, 0, pad), sem.at[w]).wait()

    outs = pl.pallas_call(
        body, name="pack_weights",
        in_specs=[VMEM] * nw, out_specs=[VMEM] * nw + [ANY] * nw,
        out_shape=[_sds(_shard_shape(kind, pad, other), BF16) for _, kind, _, pad, other in _WEIGHTS]
        + [_sds(_shard_shape(kind, N_DEV * pad, other), BF16) for _, kind, _, pad, other in _WEIGHTS],
        scratch_shapes=[pltpu.SemaphoreType.DMA((nw,))],
        compiler_params=pltpu.CompilerParams(vmem_limit_bytes=VMEM_LIMIT_BYTES),
    )(*shards)
    names = [n for n, *_ in _WEIGHTS]
    return dict(zip(names, outs[:nw])), dict(zip(names, outs[nw:]))


def _hbm(a):
    return pltpu.with_memory_space_constraint(a, pltpu.HBM)


def _split_start(name, n, body_copies, sources, lands, after):
    def body(*refs):
        src, land = refs[:n], refs[n:2 * n]
        send, recv = refs[2 * n + 1], refs[2 * n + 2]
        token = refs[-1]
        body_copies(src, land, send, recv)
        token[...] = jnp.zeros_like(token)

    arrays = list(sources) + list(lands)
    out = pl.pallas_call(
        body, name=name,
        out_shape=(pltpu.SemaphoreType.DMA((n,)), pltpu.SemaphoreType.DMA((n,)),
                   *[pltpu.HBM(a.shape, a.dtype) for a in arrays], _sds((8, 128))),
        in_specs=[HBM] * (2 * n) + [ANY], out_specs=(SEM, SEM, *[HBM] * (2 * n), VMEM),
        input_output_aliases={i: 2 + i for i in range(2 * n)},
        compiler_params=pltpu.CompilerParams(has_side_effects=EFFECT),
    )(*[_hbm(a) for a in arrays], after)
    return out[0], out[1], out[2:2 + n], out[2 + n:2 + 2 * n], out[-1]


def _split_wait(name, n, seven_of, send, recv, sources, lands, after, keep_sources=False):
    def body(*refs):
        land = refs[n:2 * n]
        send_ref, recv_ref = refs[2 * n], refs[2 * n + 1]
        myself = (lax.axis_index("x"), lax.axis_index("y"), lax.axis_index("c"))
        for w in range(n):
            seven = seven_of(w, land[w])
            copy = pltpu.make_async_remote_copy(
                src_ref=seven, dst_ref=seven, send_sem=send_ref.at[w], recv_sem=recv_ref.at[w],
                device_id=myself, device_id_type=MESH)
            copy.wait_send()
            copy.wait_recv()

    arrays = list(sources) + list(lands)
    out = pl.pallas_call(
        body, name=name,
        out_shape=[pltpu.HBM(a.shape, a.dtype) for a in arrays],
        in_specs=[HBM] * (2 * n) + [SEM, SEM, ANY], out_specs=[HBM] * (2 * n),
        input_output_aliases={i: i for i in range(2 * n)},
        compiler_params=pltpu.CompilerParams(has_side_effects=EFFECT),
    )(*arrays, send, recv, after)
    return out if keep_sources else out[n:]


def _gather_start(stage, names, packed, full, after):
    def copies(src, land, send, recv):
        me = _my_index()
        for k in range(N_DEV):
            @pl.when(me == k)
            def _():
                for w, name in enumerate(names):
                    kind, _, pad, _ = _SPEC[name]
                    dst = _window(land[w], kind, k * pad, pad)
                    for peer in range(N_DEV):
                        if peer != k:
                            pltpu.make_async_remote_copy(
                                src_ref=src[w], dst_ref=dst, send_sem=send.at[w],
                                recv_sem=recv.at[w], device_id=_device_tuple(peer),
                                device_id_type=MESH).start()

    return _split_start(f"gather_start{stage}", len(names), copies,
                        [packed[n] for n in names], [full[n] for n in names], after)


def _gather_wait(stage, names, started, after):
    send, recv, src, land, _ = started

    def seven_of(w, ref):
        kind, _, pad, _ = _SPEC[names[w]]
        return _window(ref, kind, 0, (N_DEV - 1) * pad)

    return dict(zip(names, _split_wait(f"gather_wait{stage}", len(names), seven_of,
                                       send, recv, src, land, after)))


def _scatter_start(stage, names, grads):
    def copies(src, land, send, recv):
        me = _my_index()
        for k in range(N_DEV):
            @pl.when(me != k)
            def _():
                slot = lax.rem(me + (N_DEV - 1 - k), N_DEV)
                for w, name in enumerate(names):
                    kind, _, pad, _ = _SPEC[name]
                    pltpu.make_async_remote_copy(
                        src_ref=_window(src[w], kind, k * pad, pad), dst_ref=land[w].at[slot],
                        send_sem=send.at[w], recv_sem=recv.at[w],
                        device_id=_device_tuple(k), device_id_type=MESH).start()

    lands = [lax.empty((N_DEV - 1,) + _shard_shape(_SPEC[m][0], _SPEC[m][2], _SPEC[m][3]), BF16)
             for m in names]
    return _split_start(f"scatter_start{stage}", len(names), copies, grads, lands, grads[0])


def _scatter_wait(stage, names, started, after):
    send, recv, src, land, _ = started
    n = len(names)
    out = _split_wait(f"scatter_wait{stage}", n, lambda w, ref: ref, send, recv, src, land, after,
                      keep_sources=True)
    return dict(zip(names, out[:n])), dict(zip(names, out[n:]))


def _tie(a, *tokens):
    for tok in tokens:
        a = a + tok[:1, :1]
    return a


def _allreduce_small(small, after):
    shape = small.shape

    def body(in_ref, _after, out_ref, gath, send, recv):
        me = _my_index()
        for k in range(N_DEV):
            @pl.when(me != k)
            def _():
                pltpu.make_async_remote_copy(
                    src_ref=in_ref, dst_ref=gath.at[me], send_sem=send, recv_sem=recv,
                    device_id=_device_tuple(k), device_id_type=MESH).start()

            @pl.when(me == k)
            def _():
                gath[k] = in_ref[...]
        seven = gath.at[pl.ds(0, N_DEV - 1)]
        pltpu.make_async_remote_copy(
            src_ref=seven, dst_ref=seven, send_sem=send, recv_sem=recv,
            device_id=_device_tuple(0), device_id_type=MESH).wait()
        total = gath[0]
        for s in range(1, N_DEV):
            total = total + gath[s]
        out_ref[...] = total

    return pl.pallas_call(
        body, name="allreduce_small",
        in_specs=[VMEM, ANY], out_specs=VMEM, out_shape=_sds(shape),
        scratch_shapes=[pltpu.VMEM((N_DEV,) + shape, F32),
                        pltpu.SemaphoreType.DMA, pltpu.SemaphoreType.DMA],
    )(small, after)


def _adam_update(w_ref, m_ref, v_ref, grad, grad_ref, delta_ref, nm_ref, nv_ref):
    new_m = ADAM_B1 * m_ref[...] + (1.0 - ADAM_B1) * grad
    new_v = ADAM_B2 * v_ref[...] + (1.0 - ADAM_B2) * (grad * grad)
    m_hat = new_m / (1.0 - ADAM_B1 ** ADAM_STEP)
    v_hat = new_v / (1.0 - ADAM_B2 ** ADAM_STEP)
    grad_ref[...] = grad
    delta_ref[...] = -ADAM_LR * (m_hat / (jnp.sqrt(v_hat) + ADAM_EPS) + ADAM_WD * w_ref[...])
    nm_ref[...] = new_m
    nv_ref[...] = new_v


def _adamw(w, m, v, g, *, name):
    def body(w_ref, m_ref, v_ref, g_ref, *outs):
        _adam_update(w_ref, m_ref, v_ref, g_ref[...], *outs)

    return pl.pallas_call(
        body, name=name, in_specs=[VMEM] * 4, out_specs=[VMEM] * 4,
        out_shape=[_sds(w.shape)] * 4,
    )(w, m, v, g)


def _adamw_shard(w, m, v, land, dw_full, *, kind, pad, name):
    shape = w.shape
    other = shape[0] if kind == "col" else shape[1]

    def body(w_ref, m_ref, v_ref, land_ref, own_ref, *outs):
        valid = ((slice(None), pl.ds(0, shape[1])) if kind == "col"
                 else (pl.ds(0, shape[0]), slice(None)))
        grad = own_ref[valid].astype(F32)
        for s in range(N_DEV - 1):
            grad = grad + land_ref[(s,) + valid].astype(F32)
        _adam_update(w_ref, m_ref, v_ref, grad, *outs)

    whole = lambda a: pl.BlockSpec(a.shape, lambda i: (0,) * a.ndim)
    own = pl.BlockSpec(_shard_shape(kind, pad, other),
                       (lambda i: (0, _my_index())) if kind == "col" else (lambda i: (_my_index(), 0)))
    return pl.pallas_call(
        body, name=name, grid=(1,),
        in_specs=[whole(w), whole(m), whole(v), whole(land), own],
        out_specs=[whole(w)] * 4, out_shape=[_sds(shape)] * 4,
        compiler_params=_params("arbitrary"),
    )(w, m, v, land, dw_full)


_GAINS = ("ffn1_pre", "ffn1_post", "mix_pre", "mix_post", "ffn2_pre", "ffn2_post", "ple_post")
_SMALL_ROWS = 16


def _stack_gains(get):
    return jnp.concatenate([get(n) for n in _GAINS]
                           + [jnp.concatenate([get("out_sb"), get("out_ch")], axis=1)], axis=0)


def kernel(x, p, g_ffn1_pre, g_ffn1_post, w_ffn1_gate, w_ffn1_up, w_ffn1_down, g_mix_pre, g_mix_post, w_in, g_out_sb, g_out_ch, rel_bias, w_out, g_ffn2_pre, g_ffn2_post, w_ffn2_gate, w_ffn2_up, w_ffn2_down, w_ple_proj, w_ple_gate, g_ple_post, loss_target, m_g_ffn1_pre, m_g_ffn1_post, m_w_ffn1_gate, m_w_ffn1_up, m_w_ffn1_down, m_g_mix_pre, m_g_mix_post, m_w_in, m_g_out_sb, m_g_out_ch, m_rel_bias, m_w_out, m_g_ffn2_pre, m_g_ffn2_post, m_w_ffn2_gate, m_w_ffn2_up, m_w_ffn2_down, m_w_ple_proj, m_w_ple_gate, m_g_ple_post, v_g_ffn1_pre, v_g_ffn1_post, v_w_ffn1_gate, v_w_ffn1_up, v_w_ffn1_down, v_g_mix_pre, v_g_mix_post, v_w_in, v_g_out_sb, v_g_out_ch, v_rel_bias, v_w_out, v_g_ffn2_pre, v_g_ffn2_post, v_w_ffn2_gate, v_w_ffn2_up, v_w_ffn2_down, v_w_ple_proj, v_w_ple_gate, v_g_ple_post):
    given = dict(locals())
    wnames = [n for n, *_ in _WEIGHTS]

    def shard(prefix, n):
        a = given[prefix + "w_" + n][0]
        return a.T if n in _TRANSPOSED else a

    packed, full = _pack_weights([shard("", n) for n in wnames])
    first = _GATHER_STAGES[0]
    gathers = {0: _gather_start(0, first, packed, full, packed[first[0]])}

    def weights_for(stage, after):
        names = _GATHER_STAGES[stage]
        ws = _gather_wait(stage, names, gathers[stage], after)
        if stage + 1 == len(_GATHER_STAGES):
            return ws, jnp.zeros((1, 1), F32)
        gathers[stage + 1] = _gather_start(stage + 1, _GATHER_STAGES[stage + 1], packed, full,
                                           ws[names[0]])
        return ws, gathers[stage + 1][-1][:1, :1]

    scatters = {}

    def grads_done(stage, grads):
        names = _SCATTER_STAGES[stage]
        scatters[stage] = _scatter_start(stage, names, [grads[n] for n in names])
        return scatters[stage][-1][:1, :1]

    gains = {n: given["g_" + n] for n in _GAINS + ("out_sb", "out_ch")}
    fvec = _rel_bias_to_fvec(rel_bias[0])
    loss, dx, dg, dfvec = _local_step(x[0], p[0, 0], loss_target[0], gains,
                                      weights_for, grads_done, fvec)

    results = {}

    def finish(stage, after):
        names = _SCATTER_STAGES[stage]
        dws, lands = _scatter_wait(stage, names, scatters[stage], after)
        for n in names:
            kind, _, pad, _ = _SPEC[n]
            out = _adamw_shard(shard("", n), shard("m_", n), shard("v_", n), lands[n], dws[n],
                               kind=kind, pad=pad, name="adamw_" + n)
            results["w_" + n] = [a.T for a in out] if n in _TRANSPOSED else out
        return results["w_" + names[-1]][0]

    after = dx
    for stage in range(len(_SCATTER_STAGES)):
        after = finish(stage, after)
    dfv = jnp.pad(dfvec[:, 0, :], ((0, 0), (0, D_MODEL - CH_WIN)))
    small = _allreduce_small(jnp.concatenate([_stack_gains(lambda n: dg[n]), dfv], axis=0), after)
    stacked = _adamw(_stack_gains(lambda n: given["g_" + n]),
                     _stack_gains(lambda n: given["m_g_" + n]),
                     _stack_gains(lambda n: given["v_g_" + n]),
                     small[:N_DEV], name="adamw_gains")
    half = D_MODEL // 2
    for r, n in enumerate(_GAINS):
        results["g_" + n] = [a[r:r + 1] for a in stacked]
    results["g_out_sb"] = [a[N_DEV - 1:N_DEV, :half] for a in stacked]
    results["g_out_ch"] = [a[N_DEV - 1:N_DEV, half:] for a in stacked]
    d_rel = _fvec_grad_to_rel_bias(small[N_DEV:, :CH_WIN].reshape(N_DEV, 1, CH_WIN))
    results["rel_bias"] = _adamw(rel_bias[0], m_rel_bias[0], v_rel_bias[0], d_rel,
                                 name="adamw_rel_bias")

    order = ("g_ffn1_pre", "g_ffn1_post", "w_ffn1_gate", "w_ffn1_up", "w_ffn1_down",
             "g_mix_pre", "g_mix_post", "w_in", "g_out_sb", "g_out_ch", "rel_bias", "w_out",
             "g_ffn2_pre", "g_ffn2_post", "w_ffn2_gate", "w_ffn2_up", "w_ffn2_down",
             "w_ple_proj", "w_ple_gate", "g_ple_post")

    def leaf(name, idx):
        a = results[name][idx]
        return a if name.startswith("g_") else a[None]

    total_loss = lax.psum(loss[0, 0], ("x", "y", "c"))
    return (total_loss, dx[None],
            *[leaf(n, 0) for n in order], *[leaf(n, 1) for n in order],
            *[leaf(n, 2) for n in order], *[leaf(n, 3) for n in order])
```

```python
import functools

import jax
import jax.numpy as jnp
from jax import lax
from jax.experimental import pallas as pl
from jax.experimental.pallas import tpu as pltpu

F32 = jnp.float32
BF16 = jnp.bfloat16

N_DEV = 8
D_MODEL = 1024
D_FF = 2816
FF_SHARD = D_FF // N_DEV
FF_SHARD_PAD = 384
D_FF_PAD = FF_SHARD_PAD * N_DEV
QKV_WIDTH = 3 * D_MODEL
QKV_SHARD = QKV_WIDTH // N_DEV
PLE_DIM = 256
ROW_SHARD = D_MODEL // N_DEV
HEAD_DIM = 64
PAIR = 2 * HEAD_DIM
N_PAIRS = 4
CHUNK = 64
LOOKBACK = 8
REL_CLIP = 128
N_REL = 2 * REL_CLIP + 1
CH_QB = 256
CH_LOOK = LOOKBACK * CHUNK
CH_WIN = CH_LOOK + CH_QB
SB_BLK = 256
EPS = 1e-6
NEG_INF = -1e30
ATT_SCALE = HEAD_DIM ** -0.5
ADAM_LR = 0.001
ADAM_B1 = 0.9
ADAM_B2 = 0.999
ADAM_EPS = 1e-08
ADAM_WD = 0.01
ADAM_STEP = 10
VMEM_LIMIT_BYTES = 48 * 1024 * 1024
MESH = pl.DeviceIdType.MESH

ANY = pl.BlockSpec(memory_space=pl.ANY)
VMEM = pl.BlockSpec(memory_space=pltpu.VMEM)


def _params(*sem):
    return pltpu.CompilerParams(dimension_semantics=sem or None,
                                vmem_limit_bytes=VMEM_LIMIT_BYTES)


def _sds(shape, dtype=F32):
    return jax.ShapeDtypeStruct(shape, dtype)


def _bf(x):
    return x.astype(BF16)


def _dot(a, b):
    return jnp.dot(_bf(a), _bf(b), preferred_element_type=F32)


def _dot_nt(a, b):
    return lax.dot_general(_bf(a), _bf(b), (((1,), (1,)), ((), ())),
                           preferred_element_type=F32)


def _dot_tn(a, b):
    return lax.dot_general(_bf(a), _bf(b), (((0,), (0,)), ((), ())),
                           preferred_element_type=F32)


def _sigmoid(x):
    return 1.0 / (1.0 + jnp.exp(-x))


def _softplus(x):
    return jnp.maximum(x, 0.0) + jnp.log(1.0 + jnp.exp(-jnp.abs(x)))


def _rstd(x):
    return lax.rsqrt(jnp.mean(x * x, axis=-1, keepdims=True) + EPS)


def _rms(x, g):
    return x * _rstd(x) * g


def _rms_bwd(dy, x, g):
    r = _rstd(x)
    w = dy * g
    dx = r * (w - x * (r * r) * jnp.mean(w * x, axis=-1, keepdims=True))
    dg = jnp.sum(dy * (x * r), axis=0, keepdims=True)
    return dx, dg


def _dot_exact01(x, u):
    hi = _bf(x)
    lo = _bf(x - hi.astype(F32))
    return (jnp.dot(hi, u, preferred_element_type=F32)
            + jnp.dot(lo, u, preferred_element_type=F32))


def _head_masks():
    lane = lax.broadcasted_iota(jnp.int32, (1, PAIR), 1)
    return lane < HEAD_DIM, lane >= HEAD_DIM


def _ffn_fwd(x, g_pre, g_post, wg, wu, wd, *, name):
    t = x.shape[0]
    tm, tj = 512, 512
    ni, nj = t // tm, D_FF_PAD // tj

    def body(x_ref, gpre_ref, gpost_ref, wg_ref, wu_ref, wd_ref,
             h_ref, n_ref, a_ref, b_ref, f_ref, acc_ref):
        j = pl.program_id(1)

        @pl.when(j == 0)
        def _():
            n_ref[...] = _bf(_rms(x_ref[...], gpre_ref[...]))
            acc_ref[...] = jnp.zeros_like(acc_ref)

        n = n_ref[...]
        a = _dot_nt(n, wg_ref[...])
        b = _dot_nt(n, wu_ref[...])
        a_ref[...] = a
        b_ref[...] = b
        hmid = a * _sigmoid(a) * b
        acc_ref[...] += jnp.dot(_bf(hmid), wd_ref[...], preferred_element_type=F32)

        @pl.when(j == nj - 1)
        def _():
            f = acc_ref[...]
            f_ref[...] = f
            h_ref[...] = x_ref[...] + 0.5 * _rms(f, gpost_ref[...])

    row = pl.BlockSpec((tm, D_MODEL), lambda i, j: (i, 0))
    gain = pl.BlockSpec((1, D_MODEL), lambda i, j: (0, 0))
    col = pl.BlockSpec((tm, tj), lambda i, j: (i, j))
    wtile = pl.BlockSpec((tj, D_MODEL), lambda i, j: (j, 0))
    return pl.pallas_call(
        body, name=name, grid=(ni, nj),
        in_specs=[row, gain, gain, wtile, wtile, wtile],
        out_specs=[row, row, col, col, row],
        out_shape=[_sds((t, D_MODEL)), _sds((t, D_MODEL), BF16),
                   _sds((t, D_FF_PAD)), _sds((t, D_FF_PAD)), _sds((t, D_MODEL))],
        scratch_shapes=[pltpu.VMEM((tm, D_MODEL), F32)],
        compiler_params=_params("arbitrary", "arbitrary"),
    )(x, g_pre, g_post, wg, wu, wd)


def _ffn_bwd(n, df, a, b, wg, wu, wd, *, name):
    t = n.shape[0]
    tj, ts = 256, 512
    nj, ns = D_FF_PAD // tj, t // ts

    def body(n_hbm, df_hbm, a_ref, b_ref, wg_ref, wu_ref, wd_ref,
             dwg_ref, dwu_ref, dwd_ref, dn_hbm,
             n_v, df_v, dn_v, ag, au, ad, sem):
        j = pl.program_id(0)

        @pl.when(j == 0)
        def _():
            c1 = pltpu.make_async_copy(n_hbm, n_v, sem.at[0])
            c2 = pltpu.make_async_copy(df_hbm, df_v, sem.at[1])
            c1.start()
            c2.start()
            dn_v[...] = jnp.zeros_like(dn_v)
            c1.wait()
            c2.wait()

        ag[...] = jnp.zeros_like(ag)
        au[...] = jnp.zeros_like(au)
        ad[...] = jnp.zeros_like(ad)
        wgj, wuj, wdj = wg_ref[...], wu_ref[...], wd_ref[...]
        for s in range(ns):
            rows = pl.ds(s * ts, ts)
            av, bv = a_ref[rows, :], b_ref[rows, :]
            sig = _sigmoid(av)
            silu = av * sig
            dfr = df_v[rows, :]
            nr = n_v[rows, :]
            dhmid = _dot_nt(dfr, wdj)
            da = dhmid * bv * (sig * (1.0 + av * (1.0 - sig)))
            db = dhmid * silu
            ad[...] += _dot_tn(silu * bv, dfr)
            ag[...] += _dot_tn(da, nr)
            au[...] += _dot_tn(db, nr)
            dn_v[rows, :] += _dot(da, wgj) + _dot(db, wuj)
        dwg_ref[...] = _bf(ag[...])
        dwu_ref[...] = _bf(au[...])
        dwd_ref[...] = _bf(ad[...])

        @pl.when(j == nj - 1)
        def _():
            c = pltpu.make_async_copy(dn_v, dn_hbm, sem.at[0])
            c.start()
            c.wait()

    roww = pl.BlockSpec((tj, D_MODEL), lambda j: (j, 0))
    act = pl.BlockSpec((t, tj), lambda j: (0, j))
    return pl.pallas_call(
        body, name=name, grid=(nj,),
        in_specs=[ANY, ANY, act, act, roww, roww, roww],
        out_specs=[roww, roww, roww, ANY],
        out_shape=[_sds((D_FF_PAD, D_MODEL), BF16)] * 3 + [_sds((t, D_MODEL))],
        scratch_shapes=[pltpu.VMEM((t, D_MODEL), BF16), pltpu.VMEM((t, D_MODEL), BF16),
                        pltpu.VMEM((t, D_MODEL), F32)]
        + [pltpu.VMEM((tj, D_MODEL), F32)] * 3 + [pltpu.SemaphoreType.DMA((2,))],
        compiler_params=_params("arbitrary"),
    )(n, df, a, b, wg, wu, wd)


def _junction(dres, pre=None, post=None, *, name):
    t = dres.shape[0]
    tm = 512
    ni = t // tm
    n_in = 1 + (3 if pre else 0) + (2 if post else 0)
    coef = post[2] if post else None

    def body(*refs):
        ins, outs = list(refs[:n_in]), list(refs[n_in:])
        i = pl.program_id(0)
        dh = ins.pop(0)[...]
        if pre:
            dn_ref, x_ref, gpre_ref = ins.pop(0), ins.pop(0), ins.pop(0)
            dh_ref, dgpre_ref = outs.pop(0), outs.pop(0)
            dx, dg = _rms_bwd(dn_ref[...], x_ref[...], gpre_ref[...])
            dh = dh + dx
            dh_ref[...] = dh

            @pl.when(i == 0)
            def _():
                dgpre_ref[...] = jnp.zeros_like(dgpre_ref)
            dgpre_ref[...] += dg
        if post:
            f_ref, gpost_ref = ins.pop(0), ins.pop(0)
            df_ref, dgpost_ref = outs.pop(0), outs.pop(0)
            df, dg = _rms_bwd(coef * dh, f_ref[...], gpost_ref[...])
            df_ref[...] = _bf(df)

            @pl.when(i == 0)
            def _():
                dgpost_ref[...] = jnp.zeros_like(dgpost_ref)
            dgpost_ref[...] += dg

    row = pl.BlockSpec((tm, D_MODEL), lambda i: (i, 0))
    gain = pl.BlockSpec((1, D_MODEL), lambda i: (0, 0))
    args, in_specs, out_specs, out_shape = [dres], [row], [], []
    if pre:
        args += list(pre)
        in_specs += [row, row, gain]
        out_specs += [row, gain]
        out_shape += [_sds((t, D_MODEL)), _sds((1, D_MODEL))]
    if post:
        args += [post[0], post[1]]
        in_specs += [row, gain]
        out_specs += [row, gain]
        out_shape += [_sds((t, D_MODEL), BF16), _sds((1, D_MODEL))]
    return pl.pallas_call(
        body, name=name, grid=(ni,), in_specs=in_specs, out_specs=out_specs,
        out_shape=out_shape, compiler_params=_params("arbitrary"),
    )(*args)


def _qkv_fwd(h, g, win, *, name):
    t = h.shape[0]
    tm, tn = 512, 768
    ni, nj = t // tm, QKV_WIDTH // tn

    def body(h_ref, g_ref, w_ref, qkv_ref, u_ref):
        @pl.when(pl.program_id(1) == 0)
        def _():
            u_ref[...] = _bf(_rms(h_ref[...], g_ref[...]))
        qkv_ref[...] = jnp.dot(u_ref[...], w_ref[...], preferred_element_type=F32)

    row = pl.BlockSpec((tm, D_MODEL), lambda i, j: (i, 0))
    return pl.pallas_call(
        body, name=name, grid=(ni, nj),
        in_specs=[row, pl.BlockSpec((1, D_MODEL), lambda i, j: (0, 0)),
                  pl.BlockSpec((D_MODEL, tn), lambda i, j: (0, j))],
        out_specs=[pl.BlockSpec((tm, tn), lambda i, j: (i, j)), row],
        out_shape=[_sds((t, QKV_WIDTH)), _sds((t, D_MODEL), BF16)],
        compiler_params=_params("arbitrary", "arbitrary"),
    )(h, g, win)


def _qkv_bwd(dq, dk, dv, u, win, *, name):
    t = u.shape[0]
    tn, ts = 512, 512
    nj, ns = QKV_WIDTH // tn, t // ts

    def body(dq_ref, dk_ref, dv_ref, u_ref, w_ref, dw_ref, du_hbm, du_v, acc_ref, sem):
        j = pl.program_id(0)

        @pl.when(j == 0)
        def _():
            du_v[...] = jnp.zeros_like(du_v)

        wj = w_ref[...]
        for role, d_ref in enumerate((dq_ref, dk_ref, dv_ref)):
            @pl.when(j % 3 == role)
            def _():
                acc_ref[...] = jnp.zeros_like(acc_ref)
                for s in range(ns):
                    rows = pl.ds(s * ts, ts)
                    dcol = d_ref[rows, :]
                    acc_ref[...] += _dot_tn(u_ref[rows, :], dcol)
                    du_v[rows, :] += _dot_nt(dcol, wj)
                dw_ref[...] = _bf(acc_ref[...])

        @pl.when(j == nj - 1)
        def _():
            c = pltpu.make_async_copy(du_v, du_hbm, sem)
            c.start()
            c.wait()

    colw = pl.BlockSpec((D_MODEL, tn), lambda j: (0, j))
    grp = pl.BlockSpec((t, tn), lambda j: (0, j // 3))
    return pl.pallas_call(
        body, name=name, grid=(nj,),
        in_specs=[grp, grp, grp, pl.BlockSpec((t, D_MODEL), lambda j: (0, 0)), colw],
        out_specs=[colw, ANY],
        out_shape=[_sds((D_MODEL, QKV_WIDTH), BF16), _sds((t, D_MODEL))],
        scratch_shapes=[pltpu.VMEM((t, D_MODEL), F32), pltpu.VMEM((D_MODEL, tn), F32),
                        pltpu.SemaphoreType.DMA],
        compiler_params=_params("arbitrary"),
    )(dq, dk, dv, u, win)


def _sb_stack(x):
    lo, hi = _head_masks()
    return jnp.concatenate([jnp.where(lo, x, 0.0), jnp.where(hi, x, 0.0)], axis=0)


def _sb_unstack(x2, blk):
    return jnp.where(_head_masks()[0], x2[:blk], x2[blk:])


def _sb_diag_mask(blk):
    r = lax.broadcasted_iota(jnp.int32, (2 * blk, blk), 0) & (blk - 1)
    c = lax.broadcasted_iota(jnp.int32, (2 * blk, blk), 1)
    return c < r


def _tri(n, keep):
    r = lax.broadcasted_iota(jnp.int32, (n, n), 0)
    c = lax.broadcasted_iota(jnp.int32, (n, n), 1)
    return jnp.where(keep(r, c), 1.0, 0.0).astype(BF16)


def _cumsum01(x, u):
    m = x.shape[0]
    hi = _bf(x)
    lo = _bf(x - hi.astype(F32))
    both = jnp.dot(jnp.concatenate([hi, lo], axis=0), u, preferred_element_type=F32)
    return both[:m] + both[m:]


def _sb_fwd(qkv, *, name):
    t = qkv.shape[0]
    blk = SB_BLK
    ni = t // blk

    def body(q_ref, k_ref, v_ref, o_ref, ltot_ref):
        i = pl.program_id(1)
        u_after = _tri(blk, lambda r, c: r > c)
        q2 = _bf(_sb_stack(q_ref[...] * ATT_SCALE))

        def tile(k0, mask, acc, c_l):
            kj = k_ref[pl.ds(k0, blk), :]
            vj = v_ref[pl.ds(k0, blk), :]
            z = _dot_nt(q2, kj)
            sp = _softplus(z)
            lf = -sp if mask is None else jnp.where(mask, -sp, 0.0)
            a = jnp.exp(z - sp + _cumsum01(lf, u_after) + c_l)
            if mask is not None:
                a = jnp.where(mask, a, 0.0)
            return acc + _dot(a, vj), c_l + jnp.sum(lf, axis=1, keepdims=True)

        carry = tile(pl.multiple_of(i * blk, blk), _sb_diag_mask(blk),
                     jnp.zeros((2 * blk, PAIR), F32), jnp.zeros((2 * blk, 1), F32))
        acc, c_l = lax.fori_loop(
            1, i + 1,
            lambda jj, c: tile(pl.multiple_of((i - jj) * blk, blk), None, *c), carry)
        o_ref[...] = _sb_unstack(acc, blk)
        ltot_ref[...] = _sb_unstack(jnp.broadcast_to(c_l, (2 * blk, PAIR)), blk)

    blkspec = pl.BlockSpec((blk, PAIR), lambda p, i: (i, p))
    return pl.pallas_call(
        body, name=name, grid=(N_PAIRS, ni),
        in_specs=[blkspec,
                  pl.BlockSpec((t, PAIR), lambda p, i: (0, N_PAIRS + p)),
                  pl.BlockSpec((t, PAIR), lambda p, i: (0, 2 * N_PAIRS + p))],
        out_specs=[blkspec, blkspec],
        out_shape=[_sds((t, D_MODEL)), _sds((t, D_MODEL // 2))],
        compiler_params=_params("arbitrary", "arbitrary"),
    )(qkv, qkv, qkv)


def _sb_bwd(qkv, ltot, do, *, name):
    t = qkv.shape[0]
    blk = SB_BLK
    ni = t // blk

    def body(q_ref, k_ref, v_ref, lt_ref, do_ref, dq_ref, dkout_ref, dvout_ref, dk_ref, dv_ref):
        i = pl.program_id(1)

        @pl.when(i == 0)
        def _():
            dk_ref[...] = jnp.zeros_like(dk_ref)
            dv_ref[...] = jnp.zeros_like(dv_ref)

        u_upto = _tri(blk, lambda r, c: r <= c)
        u_before = _tri(blk, lambda r, c: r < c)
        lane = lax.broadcasted_iota(jnp.int32, (1, PAIR), 1)
        q2 = _bf(_sb_stack(q_ref[...] * ATT_SCALE))
        do2 = _bf(_sb_stack(do_ref[...]))
        lt_blk = lt_ref[...]
        total = jnp.concatenate(
            [jnp.sum(jnp.where(lane == h * HEAD_DIM, lt_blk, 0.0), axis=1, keepdims=True)
             for h in range(2)], axis=0)

        def tile(k0, mask, dq_acc, c_l, c_g):
            krows = pl.ds(k0, blk)
            kj = k_ref[krows, :]
            vj = v_ref[krows, :]
            z = _dot_nt(q2, kj)
            sp = _softplus(z)
            sig = jnp.exp(z - sp)
            lf = -sp if mask is None else jnp.where(mask, -sp, 0.0)
            a = jnp.exp(z - sp + total - (_cumsum01(lf, u_upto) + c_l))
            if mask is not None:
                a = jnp.where(mask, a, 0.0)
            g = a * _dot_nt(do2, vj)
            g_before = jnp.dot(_bf(g), u_before, preferred_element_type=F32) + c_g
            dz = g * (1.0 - sig) - g_before * sig
            if mask is not None:
                dz = jnp.where(mask, dz, 0.0)
            dk_ref[krows, :] += _dot_tn(dz, q2)
            dv_ref[krows, :] += _dot_tn(a, do2)
            return (dq_acc + _dot(dz, kj), c_l + jnp.sum(lf, axis=1, keepdims=True),
                    c_g + jnp.sum(g, axis=1, keepdims=True))

        carry = lax.fori_loop(
            0, i, lambda j, c: tile(pl.multiple_of(j * blk, blk), None, *c),
            (jnp.zeros((2 * blk, PAIR), F32), jnp.zeros((2 * blk, 1), F32),
             jnp.zeros((2 * blk, 1), F32)))
        dq_acc, _, _ = tile(pl.multiple_of(i * blk, blk), _sb_diag_mask(blk), *carry)
        dq_ref[...] = _bf(_sb_unstack(dq_acc, blk) * ATT_SCALE)

        @pl.when(i == ni - 1)
        def _():
            dkout_ref[...] = _bf(dk_ref[...])
            dvout_ref[...] = _bf(dv_ref[...])

    blkspec = lambda off: pl.BlockSpec((blk, PAIR), lambda p, i: (i, off + p))
    full = lambda off: pl.BlockSpec((t, PAIR), lambda p, i: (0, off + p))
    return pl.pallas_call(
        body, name=name, grid=(N_PAIRS, ni),
        in_specs=[blkspec(0), full(N_PAIRS), full(2 * N_PAIRS), blkspec(0), blkspec(0)],
        out_specs=[blkspec(0), full(0), full(0)],
        out_shape=[_sds((t, D_MODEL), BF16)] * 3,
        scratch_shapes=[pltpu.VMEM((t, PAIR), F32), pltpu.VMEM((t, PAIR), F32)],
        compiler_params=_params("arbitrary", "arbitrary"),
    )(qkv, qkv, qkv, ltot, do)


def _ch_mask(i):
    r = lax.broadcasted_iota(jnp.int32, (CH_QB, CH_WIN), 0)
    c = lax.broadcasted_iota(jnp.int32, (CH_QB, CH_WIN), 1)
    qc = LOOKBACK + lax.shift_right_arithmetic(r, 6)
    kc = lax.shift_right_arithmetic(c, 6)
    first = i * (CH_QB // CHUNK) - LOOKBACK
    return (kc <= qc) & (kc >= qc - LOOKBACK) & (kc + first >= 0)


def _ch_probs(qm, kw, bias_h, mask):
    z = _dot_nt(qm, kw) * ATT_SCALE + bias_h
    z = jnp.where(mask, z, NEG_INF)
    e = jnp.exp(z - jnp.max(z, axis=1, keepdims=True))
    return e / jnp.sum(e, axis=1, keepdims=True)


def _ch_fill(pad_ref, src_ref, t):
    pad_ref[pl.ds(0, CH_LOOK), :] = jnp.zeros((CH_LOOK, PAIR), BF16)
    pad_ref[pl.ds(CH_LOOK, t), :] = _bf(src_ref[...])


def _ch_fwd(qkv, bias, o_in, *, name):
    t = qkv.shape[0]
    ni = t // CH_QB

    def body(q_ref, k_ref, v_ref, bias_ref, _alias, o_ref, kpad, vpad):
        i = pl.program_id(1)

        @pl.when(i == 0)
        def _():
            _ch_fill(kpad, k_ref, t)
            _ch_fill(vpad, v_ref, t)

        win = pl.ds(pl.multiple_of(i * CH_QB, CH_QB), CH_WIN)
        kw, vw = kpad[win, :], vpad[win, :]
        mask = _ch_mask(i)
        q = q_ref[...]
        outs = []
        for h, hm in enumerate(_head_masks()):
            p = _ch_probs(jnp.where(hm, q, 0.0), kw, bias_ref[h], mask)
            outs.append(_dot(p, vw))
        o_ref[...] = jnp.where(_head_masks()[0], outs[0], outs[1])

    full = lambda off: pl.BlockSpec((t, PAIR), lambda p, i: (0, off + p))
    return pl.pallas_call(
        body, name=name, grid=(N_PAIRS, ni),
        in_specs=[pl.BlockSpec((CH_QB, PAIR), lambda p, i: (i, 3 * N_PAIRS + p)),
                  full(4 * N_PAIRS), full(5 * N_PAIRS),
                  pl.BlockSpec((2, CH_QB, CH_WIN), lambda p, i: (p, 0, 0)), ANY],
        out_specs=pl.BlockSpec((CH_QB, PAIR), lambda p, i: (i, N_PAIRS + p)),
        out_shape=_sds((t, D_MODEL)),
        scratch_shapes=[pltpu.VMEM((t + CH_LOOK, PAIR), BF16)] * 2,
        input_output_aliases={4: 0},
        compiler_params=_params("arbitrary", "arbitrary"),
    )(qkv, qkv, qkv, bias, o_in)


def _ch_bwd(qkv, bias, o, do, dq_in, dk_in, dv_in, *, name):
    t = qkv.shape[0]
    ni = t // CH_QB

    def body(q_ref, k_ref, v_ref, bias_ref, o_ref, do_ref, _a0, _a1, _a2,
             dq_ref, dkout_ref, dvout_ref, dbias_ref, kpad, vpad, dkpad, dvpad):
        i = pl.program_id(1)

        @pl.when(i == 0)
        def _():
            _ch_fill(kpad, k_ref, t)
            _ch_fill(vpad, v_ref, t)
            dkpad[...] = jnp.zeros_like(dkpad)
            dvpad[...] = jnp.zeros_like(dvpad)
            dbias_ref[...] = jnp.zeros_like(dbias_ref)

        win = pl.ds(pl.multiple_of(i * CH_QB, CH_QB), CH_WIN)
        kw, vw = kpad[win, :], vpad[win, :]
        mask = _ch_mask(i)
        q, o_blk, do_blk = q_ref[...], o_ref[...], do_ref[...]
        dqs = []
        for h, hm in enumerate(_head_masks()):
            qm = _bf(jnp.where(hm, q, 0.0))
            dom = jnp.where(hm, do_blk, 0.0)
            delta = jnp.sum(dom * o_blk, axis=1, keepdims=True)
            dom = _bf(dom)
            p = _ch_probs(qm, kw, bias_ref[h], mask)
            ds = p * (_dot_nt(dom, vw) - delta)
            dbias_ref[h] += ds
            dsz = ds * ATT_SCALE
            dqs.append(_dot(dsz, kw))
            dkpad[win, :] += _dot_tn(dsz, qm)
            dvpad[win, :] += _dot_tn(p, dom)
        dq_ref[...] = _bf(jnp.where(_head_masks()[0], dqs[0], dqs[1]))

        @pl.when(i == ni - 1)
        def _():
            dkout_ref[...] = _bf(dkpad[pl.ds(CH_LOOK, t), :])
            dvout_ref[...] = _bf(dvpad[pl.ds(CH_LOOK, t), :])

    blkspec = lambda off: pl.BlockSpec((CH_QB, PAIR), lambda p, i: (i, off + p))
    full = lambda off: pl.BlockSpec((t, PAIR), lambda p, i: (0, off + p))
    bias_spec = pl.BlockSpec((2, CH_QB, CH_WIN), lambda p, i: (p, 0, 0))
    return pl.pallas_call(
        body, name=name, grid=(N_PAIRS, ni),
        in_specs=[blkspec(3 * N_PAIRS), full(4 * N_PAIRS), full(5 * N_PAIRS), bias_spec,
                  blkspec(N_PAIRS), blkspec(N_PAIRS), ANY, ANY, ANY],
        out_specs=[blkspec(N_PAIRS), full(N_PAIRS), full(N_PAIRS), bias_spec],
        out_shape=[_sds((t, D_MODEL), BF16)] * 3 + [_sds((2 * N_PAIRS, CH_QB, CH_WIN))],
        scratch_shapes=[pltpu.VMEM((t + CH_LOOK, PAIR), BF16)] * 2
        + [pltpu.VMEM((t + CH_LOOK, PAIR), F32)] * 2,
        input_output_aliases={6: 0, 7: 1, 8: 2},
        compiler_params=_params("arbitrary", "arbitrary"),
    )(qkv, qkv, qkv, bias, o, do, dq_in, dk_in, dv_in)


def _bias_expand(fvec, *, name):
    n_heads = fvec.shape[0]

    def body(f_ref, o_ref, rows8):
        row = f_ref[0]
        for r in range(8):
            rows8[pl.ds(r, 1), :] = pltpu.roll(row, r, 1)
        base = rows8[...]
        for blk in range(CH_QB // 8):
            o_ref[0, pl.ds(8 * blk, 8), :] = pltpu.roll(base, 8 * blk, 1)

    return pl.pallas_call(
        body, name=name, grid=(n_heads,),
        in_specs=[pl.BlockSpec((1, 1, CH_WIN), lambda h: (h, 0, 0))],
        out_specs=pl.BlockSpec((1, CH_QB, CH_WIN), lambda h: (h, 0, 0)),
        out_shape=_sds((n_heads, CH_QB, CH_WIN)),
        scratch_shapes=[pltpu.VMEM((8, CH_WIN), F32)],
        compiler_params=_params("arbitrary"),
    )(fvec)


def _bias_grad(dbias, *, name):
    n_heads = dbias.shape[0]
    first = CH_LOOK - REL_CLIP

    def body(d_ref, o_ref, acc8):
        acc = jnp.zeros((8, CH_WIN), F32)
        for blk in range(CH_QB // 8):
            acc = acc + pltpu.roll(d_ref[0, pl.ds(8 * blk, 8), :], (CH_WIN - 8 * blk) % CH_WIN, 1)
        acc8[...] = acc
        dvec = jnp.zeros((1, CH_WIN), F32)
        for r in range(8):
            dvec = dvec + pltpu.roll(acc8[pl.ds(r, 1), :], (CH_WIN - r) % CH_WIN, 1)
        lane = lax.broadcasted_iota(jnp.int32, (1, CH_WIN), 1)
        clipped = (lane <= first) | (lane >= first + REL_CLIP + CHUNK)
        total = jnp.sum(jnp.where(clipped, dvec, 0.0), axis=1, keepdims=True)
        o_ref[0] = jnp.where(lane == first, total, dvec)

    return pl.pallas_call(
        body, name=name, grid=(n_heads,),
        in_specs=[pl.BlockSpec((1, CH_QB, CH_WIN), lambda h: (h, 0, 0))],
        out_specs=pl.BlockSpec((1, 1, CH_WIN), lambda h: (h, 0, 0)),
        out_shape=_sds((n_heads, 1, CH_WIN)),
        scratch_shapes=[pltpu.VMEM((8, CH_WIN), F32)],
        compiler_params=_params("arbitrary"),
    )(dbias)


def _out_fwd(o, h1, g_sb, g_ch, g_post, wout, *, name):
    t = o.shape[0]
    tm = 512
    half = D_MODEL // 2

    def body(o_ref, h_ref, gsb_ref, gch_ref, gpost_ref, w_ref, h2_ref, mixed_ref, y_ref):
        ov = o_ref[...]
        mixed = jnp.concatenate([_rms(ov[:, :half], gsb_ref[...]),
                                 _rms(ov[:, half:], gch_ref[...])], axis=1)
        mixed_ref[...] = _bf(mixed)
        y = _dot(mixed, w_ref[...])
        y_ref[...] = y
        h2_ref[...] = h_ref[...] + _rms(y, gpost_ref[...])

    row = pl.BlockSpec((tm, D_MODEL), lambda i: (i, 0))
    gain = lambda n: pl.BlockSpec((1, n), lambda i: (0, 0))
    return pl.pallas_call(
        body, name=name, grid=(t // tm,),
        in_specs=[row, row, gain(half), gain(half), gain(D_MODEL),
                  pl.BlockSpec((D_MODEL, D_MODEL), lambda i: (0, 0))],
        out_specs=[row, row, row],
        out_shape=[_sds((t, D_MODEL)), _sds((t, D_MODEL), BF16), _sds((t, D_MODEL))],
        compiler_params=_params("arbitrary"),
    )(o, h1, g_sb, g_ch, g_post, wout)


def _out_bwd(dy, mixed, o, g_sb, g_ch, wout, *, name):
    t = o.shape[0]
    tm = 512
    ni = t // tm
    half = D_MODEL // 2

    def body(dy_ref, mixed_ref, o_ref, gsb_ref, gch_ref, w_ref,
             dw_ref, do_ref, dgsb_ref, dgch_ref, acc_ref):
        i = pl.program_id(0)

        @pl.when(i == 0)
        def _():
            acc_ref[...] = jnp.zeros_like(acc_ref)
            dgsb_ref[...] = jnp.zeros_like(dgsb_ref)
            dgch_ref[...] = jnp.zeros_like(dgch_ref)

        dyv = dy_ref[...]
        acc_ref[...] += _dot_tn(mixed_ref[...], dyv)
        dm = _dot_nt(dyv, w_ref[...])
        ov = o_ref[...]
        doa, dga = _rms_bwd(dm[:, :half], ov[:, :half], gsb_ref[...])
        dob, dgb = _rms_bwd(dm[:, half:], ov[:, half:], gch_ref[...])
        do_ref[...] = jnp.concatenate([doa, dob], axis=1)
        dgsb_ref[...] += dga
        dgch_ref[...] += dgb

        @pl.when(i == ni - 1)
        def _():
            dw_ref[...] = _bf(acc_ref[...])

    row = pl.BlockSpec((tm, D_MODEL), lambda i: (i, 0))
    gain = pl.BlockSpec((1, half), lambda i: (0, 0))
    sq = pl.BlockSpec((D_MODEL, D_MODEL), lambda i: (0, 0))
    return pl.pallas_call(
        body, name=name, grid=(ni,),
        in_specs=[row, row, row, gain, gain, sq],
        out_specs=[sq, row, gain, gain],
        out_shape=[_sds((D_MODEL, D_MODEL), BF16), _sds((t, D_MODEL)),
                   _sds((1, half)), _sds((1, half))],
        scratch_shapes=[pltpu.VMEM((D_MODEL, D_MODEL), F32)],
        compiler_params=_params("arbitrary"),
    )(dy, mixed, o, g_sb, g_ch, wout)


def _ple(p, h3, target, wp, wgate, g, *, name):
    t = h3.shape[0]
    tm = 512
    ni = t // tm

    def body(p_ref, h_ref, tgt_ref, wp_ref, wg_ref, g_ref,
             loss_ref, dres_ref, dwp_ref, dwg_ref, dg_ref, accp, accg):
        i = pl.program_id(0)

        @pl.when(i == 0)
        def _():
            loss_ref[...] = jnp.zeros_like(loss_ref)
            dg_ref[...] = jnp.zeros_like(dg_ref)
            accp[...] = jnp.zeros_like(accp)
            accg[...] = jnp.zeros_like(accg)

        pv, hv, gv = p_ref[...], h_ref[...], g_ref[...]
        pe = _dot(pv, wp_ref[...])
        sig = _sigmoid(_dot(hv, wg_ref[...]))
        e = pe * sig
        err = hv + _rms(e, gv) - tgt_ref[...]
        tok = jnp.mean(err * err, axis=-1, keepdims=True)
        loss_ref[...] += 0.5 * jnp.sum(tok, axis=0, keepdims=True)
        dh4 = err * (1.0 / D_MODEL)
        de, dg = _rms_bwd(dh4, e, gv)
        dg_ref[...] += dg
        dpe = de * sig
        dgt = de * pe * sig * (1.0 - sig)
        accp[...] += _dot_tn(pv, dpe)
        accg[...] += _dot_tn(hv, dgt)
        dres_ref[...] = dh4 + _dot_nt(dgt, wg_ref[...])

        @pl.when(i == ni - 1)
        def _():
            dwp_ref[...] = _bf(accp[...])
            dwg_ref[...] = _bf(accg[...])

    row = pl.BlockSpec((tm, D_MODEL), lambda i: (i, 0))
    const = lambda r, c: pl.BlockSpec((r, c), lambda i: (0, 0))
    return pl.pallas_call(
        body, name=name, grid=(ni,),
        in_specs=[pl.BlockSpec((tm, PLE_DIM), lambda i: (i, 0)), row, row,
                  const(PLE_DIM, D_MODEL), const(D_MODEL, D_MODEL), const(1, D_MODEL)],
        out_specs=[const(1, 128), row, const(PLE_DIM, D_MODEL), const(D_MODEL, D_MODEL),
                   const(1, D_MODEL)],
        out_shape=[_sds((1, 128)), _sds((t, D_MODEL)), _sds((PLE_DIM, D_MODEL), BF16),
                   _sds((D_MODEL, D_MODEL), BF16), _sds((1, D_MODEL))],
        scratch_shapes=[pltpu.VMEM((PLE_DIM, D_MODEL), F32), pltpu.VMEM((D_MODEL, D_MODEL), F32)],
        compiler_params=_params("arbitrary"),
    )(p, h3, target, wp, wgate, g)


def _rel_bias_to_fvec(rel_bias):
    rev = rel_bias[:, ::-1]
    n_heads = rel_bias.shape[0]
    first = CH_LOOK - REL_CLIP
    n_var = REL_CLIP + CHUNK
    clipped = rev[:, :1]
    fvec = jnp.concatenate([jnp.broadcast_to(clipped, (n_heads, first)), rev[:, :n_var],
                            jnp.broadcast_to(clipped, (n_heads, CH_WIN - first - n_var))], axis=1)
    return fvec.reshape(n_heads, 1, CH_WIN)


def _fvec_grad_to_rel_bias(dfvec):
    first = CH_LOOK - REL_CLIP
    n_var = REL_CLIP + CHUNK
    rev = jnp.pad(dfvec[:, 0, first:first + n_var], ((0, 0), (0, N_REL - n_var)))
    return rev[:, ::-1]


def _local_step(x, p, target, g, weights_for, grads_done, fvec):
    w, tie = weights_for(0, x)
    w = dict(w)
    h1, n1, a1, b1, f1 = _ffn_fwd(x, g["ffn1_pre"] + tie, g["ffn1_post"],
                                  w["ffn1_gate"], w["ffn1_up"], w["ffn1_down"], name="ffn1_fwd")
    more, tie = weights_for(1, h1)
    w.update(more)
    qkv, u = _qkv_fwd(h1, g["mix_pre"] + tie, w["in"], name="qkv_fwd")
    bias = _bias_expand(fvec, name="bias_expand")
    o, ltot = _sb_fwd(qkv, name="sb_fwd")
    o = _ch_fwd(qkv, bias, o, name="ch_fwd")
    h2, mixed, y = _out_fwd(o, h1, g["out_sb"], g["out_ch"], g["mix_post"], w["out"], name="out_fwd")
    w.update(weights_for(2, h2)[0])
    h3, n2, a2, b2, f2 = _ffn_fwd(h2, g["ffn2_pre"], g["ffn2_post"],
                                  w["ffn2_gate"], w["ffn2_up"], w["ffn2_down"], name="ffn2_fwd")
    loss, dh3, dwp, dwgate, dg_ple = _ple(p, h3, target, w["ple_proj"], w["ple_gate"],
                                          g["ple_post"], name="ple")
    tie = grads_done(0, {"ple_proj": dwp, "ple_gate": dwgate})

    df2, dg_ffn2_post = _junction(dh3, post=(f2, g["ffn2_post"] + tie, 0.5), name="junction3")
    dwg2, dwu2, dwd2, dn2 = _ffn_bwd(n2, df2, a2, b2, w["ffn2_gate"], w["ffn2_up"],
                                     w["ffn2_down"], name="ffn2_bwd")
    tie = grads_done(1, {"ffn2_gate": dwg2, "ffn2_up": dwu2, "ffn2_down": dwd2})
    dh2, dg_ffn2_pre, dy, dg_mix_post = _junction(
        dh3, pre=(dn2, h2, g["ffn2_pre"] + tie), post=(y, g["mix_post"], 1.0), name="junction2")
    dwout, do, dg_sb, dg_ch = _out_bwd(dy, mixed, o, g["out_sb"], g["out_ch"], w["out"],
                                       name="out_bwd")
    dq, dk, dv = _sb_bwd(qkv, ltot, do, name="sb_bwd")
    dq, dk, dv, dbias = _ch_bwd(qkv, bias, o, do, dq, dk, dv, name="ch_bwd")
    dfvec = _bias_grad(dbias, name="bias_grad")
    dwin, du = _qkv_bwd(dq, dk, dv, u, w["in"], name="qkv_bwd")
    tie = grads_done(2, {"out": dwout, "in": dwin})
    dh1, dg_mix_pre, df1, dg_ffn1_post = _junction(
        dh2, pre=(du, h1, g["mix_pre"] + tie), post=(f1, g["ffn1_post"], 0.5), name="junction1")
    dwg1, dwu1, dwd1, dn1 = _ffn_bwd(n1, df1, a1, b1, w["ffn1_gate"], w["ffn1_up"],
                                     w["ffn1_down"], name="ffn1_bwd")
    tie = grads_done(3, {"ffn1_gate": dwg1, "ffn1_up": dwu1, "ffn1_down": dwd1})
    dx, dg_ffn1_pre = _junction(dh1, pre=(dn1, x, g["ffn1_pre"] + tie), name="junction0")

    dg = {"ffn1_pre": dg_ffn1_pre, "ffn1_post": dg_ffn1_post, "mix_pre": dg_mix_pre,
          "mix_post": dg_mix_post, "out_sb": dg_sb, "out_ch": dg_ch,
          "ffn2_pre": dg_ffn2_pre, "ffn2_post": dg_ffn2_post, "ple_post": dg_ple}
    return loss, dx, dg, dfvec


_WEIGHTS = (
    ("ffn1_gate", "row", FF_SHARD, FF_SHARD_PAD, D_MODEL),
    ("ffn1_up", "row", FF_SHARD, FF_SHARD_PAD, D_MODEL),
    ("ffn1_down", "row", FF_SHARD, FF_SHARD_PAD, D_MODEL),
    ("in", "col", QKV_SHARD, QKV_SHARD, D_MODEL),
    ("out", "row", ROW_SHARD, ROW_SHARD, D_MODEL),
    ("ffn2_gate", "row", FF_SHARD, FF_SHARD_PAD, D_MODEL),
    ("ffn2_up", "row", FF_SHARD, FF_SHARD_PAD, D_MODEL),
    ("ffn2_down", "row", FF_SHARD, FF_SHARD_PAD, D_MODEL),
    ("ple_proj", "col", ROW_SHARD, ROW_SHARD, PLE_DIM),
    ("ple_gate", "row", ROW_SHARD, ROW_SHARD, D_MODEL),
)
_TRANSPOSED = ("ffn1_gate", "ffn1_up", "ffn2_gate", "ffn2_up")
_SPEC = {n: (kind, valid, pad, other) for n, kind, valid, pad, other in _WEIGHTS}
_GATHER_STAGES = (("ffn1_gate", "ffn1_up", "ffn1_down"), ("in", "out"),
                  ("ffn2_gate", "ffn2_up", "ffn2_down", "ple_proj", "ple_gate"))
_SCATTER_STAGES = (("ple_proj", "ple_gate"), ("ffn2_gate", "ffn2_up", "ffn2_down"),
                   ("out", "in"), ("ffn1_gate", "ffn1_up", "ffn1_down"))
HBM = pl.BlockSpec(memory_space=pltpu.HBM)
SEM = pl.BlockSpec(memory_space=pltpu.SEMAPHORE)
EFFECT = pltpu.SideEffectType.DATAFLOW_SIDE_EFFECTING


def _shard_shape(kind, size, other):
    return (other, size) if kind == "col" else (size, other)


def _window(ref, kind, start, size):
    return ref.at[:, pl.ds(start, size)] if kind == "col" else ref.at[pl.ds(start, size), :]


def _device_tuple(k):
    return (k // 4, (k // 2) % 2, k % 2)


def _my_index():
    return 4 * lax.axis_index("x") + 2 * lax.axis_index("y") + lax.axis_index("c")


def _pack_weights(shards):
    nw = len(_WEIGHTS)

    def body(*refs):
        ins, packed, full = refs[:nw], refs[nw:2 * nw], refs[2 * nw:3 * nw]
        sem = refs[3 * nw]
        me = _my_index()
        for (_, kind, valid, pad, _), src, dst in zip(_WEIGHTS, ins, packed):
            if pad != valid:
                dst[...] = jnp.zeros_like(dst)
            if kind == "col":
                dst[:, pl.ds(0, valid)] = _bf(src[...])
            else:
                dst[pl.ds(0, valid), :] = _bf(src[...])
        for k in range(N_DEV):
            @pl.when(me == k)
            def _():
                for w, (_, kind, _, pad, _) in enumerate(_WEIGHTS):
                    pltpu.make_async_copy(packed[w], _window(full[w], kind, k * pad, pad),
                                          sem.at[w]).start()
        for w, (_, kind, _, pad, _) in enumerate(_WEIGHTS):
            pltpu.make_async_copy(packed[w], _window(full[w], kind, 0, pad), sem.at[w]).wait()

    outs = pl.pallas_call(
        body, name="pack_weights",
        in_specs=[VMEM] * nw, out_specs=[VMEM] * nw + [ANY] * nw,
        out_shape=[_sds(_shard_shape(kind, pad, other), BF16) for _, kind, _, pad, other in _WEIGHTS]
        + [_sds(_shard_shape(kind, N_DEV * pad, other), BF16) for _, kind, _, pad, other in _WEIGHTS],
        scratch_shapes=[pltpu.SemaphoreType.DMA((nw,))],
        compiler_params=pltpu.CompilerParams(vmem_limit_bytes=VMEM_LIMIT_BYTES),
    )(*shards)
    names = [n for n, *_ in _WEIGHTS]
    return dict(zip(names, outs[:nw])), dict(zip(names, outs[nw:]))


def _hbm(a):
    return pltpu.with_memory_space_constraint(a, pltpu.HBM)


def _split_start(name, n, body_copies, sources, lands, after):
    arrays = list(sources) + list(lands)
    ns, na = len(sources), len(arrays)

    def body(*refs):
        src, land = refs[:ns], refs[ns:na]
        send, recv = refs[na + 1], refs[na + 2]
        token = refs[-1]
        body_copies(src, land, send, recv)
        token[...] = jnp.zeros_like(token)

    out = pl.pallas_call(
        body, name=name,
        out_shape=(pltpu.SemaphoreType.DMA((n,)), pltpu.SemaphoreType.DMA((n,)),
                   *[pltpu.HBM(a.shape, a.dtype) for a in arrays], _sds((8, 128))),
        in_specs=[HBM] * na + [ANY], out_specs=(SEM, SEM, *[HBM] * na, VMEM),
        input_output_aliases={i: 2 + i for i in range(na)},
        compiler_params=pltpu.CompilerParams(has_side_effects=EFFECT),
    )(*[_hbm(a) for a in arrays], after)
    return out[0], out[1], out[2:2 + ns], out[2 + ns:2 + na], out[-1]


def _split_wait(name, n, seven_of, send, recv, sources, lands, after, keep_sources=False):
    arrays = list(sources) + list(lands)
    ns, na = len(sources), len(arrays)

    def body(*refs):
        land = refs[ns:na]
        send_ref, recv_ref = refs[na], refs[na + 1]
        myself = (lax.axis_index("x"), lax.axis_index("y"), lax.axis_index("c"))
        for w in range(n):
            seven = seven_of(w, land[w])
            copy = pltpu.make_async_remote_copy(
                src_ref=seven, dst_ref=seven, send_sem=send_ref.at[w], recv_sem=recv_ref.at[w],
                device_id=myself, device_id_type=MESH)
            copy.wait_send()
            copy.wait_recv()

    out = pl.pallas_call(
        body, name=name,
        out_shape=[pltpu.HBM(a.shape, a.dtype) for a in arrays],
        in_specs=[HBM] * na + [SEM, SEM, ANY], out_specs=[HBM] * na,
        input_output_aliases={i: i for i in range(na)},
        compiler_params=pltpu.CompilerParams(has_side_effects=EFFECT),
    )(*arrays, send, recv, after)
    return out if keep_sources else out[ns:]


_ALL_PEERS = (1, 2, 3, 4, 5, 6, 7)
_NEAR_PEERS = (1, 2, 4, 6)
_FAR_CHIPS = (2, 4, 6)


def _gather_start(stage, names, packed, full, after, peers=_ALL_PEERS):
    def copies(src, land, send, recv):
        me = _my_index()
        for k in range(N_DEV):
            @pl.when(me == k)
            def _():
                for w, name in enumerate(names):
                    kind, _, pad, _ = _SPEC[name]
                    dst = _window(land[w], kind, k * pad, pad)
                    for mask in peers:
                        pltpu.make_async_remote_copy(
                            src_ref=src[w], dst_ref=dst, send_sem=send.at[w],
                            recv_sem=recv.at[w], device_id=_device_tuple(k ^ mask),
                            device_id_type=MESH).start()

    return _split_start(f"gather_start{stage}", len(names), copies,
                        [packed[n] for n in names], [full[n] for n in names], after)


def _gather_wait(stage, names, started, after, count=N_DEV - 1):
    send, recv, src, land, _ = started

    def bytes_of(w, ref):
        kind, _, pad, _ = _SPEC[names[w]]
        return _window(ref, kind, 0, count * pad)

    return dict(zip(names, _split_wait(f"gather_wait{stage}", len(names), bytes_of,
                                       send, recv, src, land, after)))


def _relay_start(stage, names, full, after):
    def copies(_, land, send, recv):
        me = _my_index()
        for k in range(N_DEV):
            @pl.when(me == k)
            def _():
                for w, name in enumerate(names):
                    kind, _, pad, _ = _SPEC[name]
                    for mask in _FAR_CHIPS:
                        win = _window(land[w], kind, (k ^ mask) * pad, pad)
                        pltpu.make_async_remote_copy(
                            src_ref=win, dst_ref=win, send_sem=send.at[w], recv_sem=recv.at[w],
                            device_id=_device_tuple(k ^ 1), device_id_type=MESH).start()

    return _split_start(f"relay_start{stage}", len(names), copies, [],
                        [full[n] for n in names], after)


def _scatter_start(stage, names, grads):
    def copies(src, land, send, recv):
        me = _my_index()
        for k in range(N_DEV):
            @pl.when(me != k)
            def _():
                slot = lax.rem(me + (N_DEV - 1 - k), N_DEV)
                for w, name in enumerate(names):
                    kind, _, pad, _ = _SPEC[name]
                    pltpu.make_async_remote_copy(
                        src_ref=_window(src[w], kind, k * pad, pad), dst_ref=land[w].at[slot],
                        send_sem=send.at[w], recv_sem=recv.at[w],
                        device_id=_device_tuple(k), device_id_type=MESH).start()

    lands = [lax.empty((N_DEV - 1,) + _shard_shape(_SPEC[m][0], _SPEC[m][2], _SPEC[m][3]), BF16)
             for m in names]
    return _split_start(f"scatter_start{stage}", len(names), copies, grads, lands, grads[0])


def _scatter_wait(stage, names, started, after):
    send, recv, src, land, _ = started
    n = len(names)
    out = _split_wait(f"scatter_wait{stage}", n, lambda w, ref: ref, send, recv, src, land, after,
                      keep_sources=True)
    return dict(zip(names, out[:n])), dict(zip(names, out[n:]))


def _tie(a, *tokens):
    for tok in tokens:
        a = a + tok[:1, :1]
    return a


def _allreduce_small(small, after):
    shape = small.shape

    def body(in_ref, _after, out_ref, gath, send, recv):
        me = _my_index()
        for k in range(N_DEV):
            @pl.when(me != k)
            def _():
                pltpu.make_async_remote_copy(
                    src_ref=in_ref, dst_ref=gath.at[me], send_sem=send, recv_sem=recv,
                    device_id=_device_tuple(k), device_id_type=MESH).start()

            @pl.when(me == k)
            def _():
                gath[k] = in_ref[...]
        seven = gath.at[pl.ds(0, N_DEV - 1)]
        pltpu.make_async_remote_copy(
            src_ref=seven, dst_ref=seven, send_sem=send, recv_sem=recv,
            device_id=_device_tuple(0), device_id_type=MESH).wait()
        total = gath[0]
        for s in range(1, N_DEV):
            total = total + gath[s]
        out_ref[...] = total

    return pl.pallas_call(
        body, name="allreduce_small",
        in_specs=[VMEM, ANY], out_specs=VMEM, out_shape=_sds(shape),
        scratch_shapes=[pltpu.VMEM((N_DEV,) + shape, F32),
                        pltpu.SemaphoreType.DMA, pltpu.SemaphoreType.DMA],
    )(small, after)


def _adam_update(w_ref, m_ref, v_ref, grad, grad_ref, delta_ref, nm_ref, nv_ref):
    new_m = ADAM_B1 * m_ref[...] + (1.0 - ADAM_B1) * grad
    new_v = ADAM_B2 * v_ref[...] + (1.0 - ADAM_B2) * (grad * grad)
    m_hat = new_m / (1.0 - ADAM_B1 ** ADAM_STEP)
    v_hat = new_v / (1.0 - ADAM_B2 ** ADAM_STEP)
    grad_ref[...] = grad
    delta_ref[...] = -ADAM_LR * (m_hat / (jnp.sqrt(v_hat) + ADAM_EPS) + ADAM_WD * w_ref[...])
    nm_ref[...] = new_m
    nv_ref[...] = new_v


def _adamw(w, m, v, g, *, name):
    def body(w_ref, m_ref, v_ref, g_ref, *outs):
        _adam_update(w_ref, m_ref, v_ref, g_ref[...], *outs)

    return pl.pallas_call(
        body, name=name, in_specs=[VMEM] * 4, out_specs=[VMEM] * 4,
        out_shape=[_sds(w.shape)] * 4,
    )(w, m, v, g)


def _adamw_shard(w, m, v, land, dw_full, *, kind, pad, name):
    shape = w.shape
    other = shape[0] if kind == "col" else shape[1]

    def body(w_ref, m_ref, v_ref, land_ref, own_ref, *outs):
        valid = ((slice(None), pl.ds(0, shape[1])) if kind == "col"
                 else (pl.ds(0, shape[0]), slice(None)))
        grad = own_ref[valid].astype(F32)
        for s in range(N_DEV - 1):
            grad = grad + land_ref[(s,) + valid].astype(F32)
        _adam_update(w_ref, m_ref, v_ref, grad, *outs)

    whole = lambda a: pl.BlockSpec(a.shape, lambda i: (0,) * a.ndim)
    own = pl.BlockSpec(_shard_shape(kind, pad, other),
                       (lambda i: (0, _my_index())) if kind == "col" else (lambda i: (_my_index(), 0)))
    return pl.pallas_call(
        body, name=name, grid=(1,),
        in_specs=[whole(w), whole(m), whole(v), whole(land), own],
        out_specs=[whole(w)] * 4, out_shape=[_sds(shape)] * 4,
        compiler_params=_params("arbitrary"),
    )(w, m, v, land, dw_full)


_GAINS = ("ffn1_pre", "ffn1_post", "mix_pre", "mix_post", "ffn2_pre", "ffn2_post", "ple_post")
_SMALL_ROWS = 16


def _stack_gains(get):
    return jnp.concatenate([get(n) for n in _GAINS]
                           + [jnp.concatenate([get("out_sb"), get("out_ch")], axis=1)], axis=0)


def kernel(x, p, g_ffn1_pre, g_ffn1_post, w_ffn1_gate, w_ffn1_up, w_ffn1_down, g_mix_pre, g_mix_post, w_in, g_out_sb, g_out_ch, rel_bias, w_out, g_ffn2_pre, g_ffn2_post, w_ffn2_gate, w_ffn2_up, w_ffn2_down, w_ple_proj, w_ple_gate, g_ple_post, loss_target, m_g_ffn1_pre, m_g_ffn1_post, m_w_ffn1_gate, m_w_ffn1_up, m_w_ffn1_down, m_g_mix_pre, m_g_mix_post, m_w_in, m_g_out_sb, m_g_out_ch, m_rel_bias, m_w_out, m_g_ffn2_pre, m_g_ffn2_post, m_w_ffn2_gate, m_w_ffn2_up, m_w_ffn2_down, m_w_ple_proj, m_w_ple_gate, m_g_ple_post, v_g_ffn1_pre, v_g_ffn1_post, v_w_ffn1_gate, v_w_ffn1_up, v_w_ffn1_down, v_g_mix_pre, v_g_mix_post, v_w_in, v_g_out_sb, v_g_out_ch, v_rel_bias, v_w_out, v_g_ffn2_pre, v_g_ffn2_post, v_w_ffn2_gate, v_w_ffn2_up, v_w_ffn2_down, v_w_ple_proj, v_w_ple_gate, v_g_ple_post):
    given = dict(locals())
    wnames = [n for n, *_ in _WEIGHTS]

    def shard(prefix, n):
        a = given[prefix + "w_" + n][0]
        return a.T if n in _TRANSPOSED else a

    packed, full = _pack_weights([shard("", n) for n in wnames])
    first = _GATHER_STAGES[0]
    gathers = {0: _gather_start(0, first, packed, full, packed[first[0]], peers=_NEAR_PEERS)}

    def weights_for(stage, after):
        names = _GATHER_STAGES[stage]
        if stage == 0:
            near = _gather_wait(0, names, gathers[0], after, count=len(_NEAR_PEERS))
            relay = _relay_start(0, names, near, near[names[0]])
            ws = _gather_wait("0r", names, relay, relay[-1], count=len(_FAR_CHIPS))
        else:
            ws = _gather_wait(stage, names, gathers[stage], after)
        if stage + 1 == len(_GATHER_STAGES):
            return ws, jnp.zeros((1, 1), F32)
        gathers[stage + 1] = _gather_start(stage + 1, _GATHER_STAGES[stage + 1], packed, full,
                                           ws[names[0]])
        return ws, gathers[stage + 1][-1][:1, :1]

    scatters = {}

    def grads_done(stage, grads):
        names = _SCATTER_STAGES[stage]
        scatters[stage] = _scatter_start(stage, names, [grads[n] for n in names])
        return scatters[stage][-1][:1, :1]

    gains = {n: given["g_" + n] for n in _GAINS + ("out_sb", "out_ch")}
    fvec = _rel_bias_to_fvec(rel_bias[0])
    loss, dx, dg, dfvec = _local_step(x[0], p[0, 0], loss_target[0], gains,
                                      weights_for, grads_done, fvec)

    results = {}

    def finish(stage, after):
        names = _SCATTER_STAGES[stage]
        dws, lands = _scatter_wait(stage, names, scatters[stage], after)
        for n in names:
            kind, _, pad, _ = _SPEC[n]
            out = _adamw_shard(shard("", n), shard("m_", n), shard("v_", n), lands[n], dws[n],
                               kind=kind, pad=pad, name="adamw_" + n)
            results["w_" + n] = [a.T for a in out] if n in _TRANSPOSED else out
        return results["w_" + names[-1]][0]

    after = dx
    for stage in range(len(_SCATTER_STAGES)):
        after = finish(stage, after)
    dfv = jnp.pad(dfvec[:, 0, :], ((0, 0), (0, D_MODEL - CH_WIN)))
    small = _allreduce_small(jnp.concatenate([_stack_gains(lambda n: dg[n]), dfv], axis=0), after)
    stacked = _adamw(_stack_gains(lambda n: given["g_" + n]),
                     _stack_gains(lambda n: given["m_g_" + n]),
                     _stack_gains(lambda n: given["v_g_" + n]),
                     small[:N_DEV], name="adamw_gains")
    half = D_MODEL // 2
    for r, n in enumerate(_GAINS):
        results["g_" + n] = [a[r:r + 1] for a in stacked]
    results["g_out_sb"] = [a[N_DEV - 1:N_DEV, :half] for a in stacked]
    results["g_out_ch"] = [a[N_DEV - 1:N_DEV, half:] for a in stacked]
    d_rel = _fvec_grad_to_rel_bias(small[N_DEV:, :CH_WIN].reshape(N_DEV, 1, CH_WIN))
    results["rel_bias"] = _adamw(rel_bias[0], m_rel_bias[0], v_rel_bias[0], d_rel,
                                 name="adamw_rel_bias")

    order = ("g_ffn1_pre", "g_ffn1_post", "w_ffn1_gate", "w_ffn1_up", "w_ffn1_down",
             "g_mix_pre", "g_mix_post", "w_in", "g_out_sb", "g_out_ch", "rel_bias", "w_out",
             "g_ffn2_pre", "g_ffn2_post", "w_ffn2_gate", "w_ffn2_up", "w_ffn2_down",
             "w_ple_proj", "w_ple_gate", "g_ple_post")

    def leaf(name, idx):
        a = results[name][idx]
        return a if name.startswith("g_") else a[None]

    total_loss = lax.psum(loss[0, 0], ("x", "y", "c"))
    return (total_loss, dx[None],
            *[leaf(n, 0) for n in order], *[leaf(n, 1) for n in order],
            *[leaf(n, 2) for n in order], *[leaf(n, 3) for n in order])
```

```python
import functools

import jax
import jax.numpy as jnp
from jax import lax
from jax.experimental import pallas as pl
from jax.experimental.pallas import tpu as pltpu

F32 = jnp.float32
BF16 = jnp.bfloat16

N_DEV = 8
D_MODEL = 1024
D_FF = 2816
FF_SHARD = D_FF // N_DEV
FF_SHARD_PAD = 384
D_FF_PAD = FF_SHARD_PAD * N_DEV
QKV_WIDTH = 3 * D_MODEL
QKV_SHARD = QKV_WIDTH // N_DEV
PLE_DIM = 256
ROW_SHARD = D_MODEL // N_DEV
HEAD_DIM = 64
PAIR = 2 * HEAD_DIM
N_PAIRS = 4
CHUNK = 64
LOOKBACK = 8
REL_CLIP = 128
N_REL = 2 * REL_CLIP + 1
CH_QB = 256
CH_LOOK = LOOKBACK * CHUNK
CH_WIN = CH_LOOK + CH_QB
SB_BLK = 256
EPS = 1e-6
NEG_INF = -1e30
ATT_SCALE = HEAD_DIM ** -0.5
ADAM_LR = 0.001
ADAM_B1 = 0.9
ADAM_B2 = 0.999
ADAM_EPS = 1e-08
ADAM_WD = 0.01
ADAM_STEP = 10
VMEM_LIMIT_BYTES = 48 * 1024 * 1024
MESH = pl.DeviceIdType.MESH

ANY = pl.BlockSpec(memory_space=pl.ANY)
VMEM = pl.BlockSpec(memory_space=pltpu.VMEM)


def _params(*sem):
    return pltpu.CompilerParams(dimension_semantics=sem or None,
                                vmem_limit_bytes=VMEM_LIMIT_BYTES)


def _sds(shape, dtype=F32):
    return jax.ShapeDtypeStruct(shape, dtype)


def _bf(x):
    return x.astype(BF16)


def _dot(a, b):
    return jnp.dot(_bf(a), _bf(b), preferred_element_type=F32)


def _dot_nt(a, b):
    return lax.dot_general(_bf(a), _bf(b), (((1,), (1,)), ((), ())),
                           preferred_element_type=F32)


def _dot_tn(a, b):
    return lax.dot_general(_bf(a), _bf(b), (((0,), (0,)), ((), ())),
                           preferred_element_type=F32)


def _sigmoid(x):
    return 1.0 / (1.0 + jnp.exp(-x))


def _softplus(x):
    return jnp.maximum(x, 0.0) + jnp.log(1.0 + jnp.exp(-jnp.abs(x)))


def _rstd(x):
    return lax.rsqrt(jnp.mean(x * x, axis=-1, keepdims=True) + EPS)


def _rms(x, g):
    return x * _rstd(x) * g


def _rms_bwd(dy, x, g):
    r = _rstd(x)
    w = dy * g
    dx = r * (w - x * (r * r) * jnp.mean(w * x, axis=-1, keepdims=True))
    dg = jnp.sum(dy * (x * r), axis=0, keepdims=True)
    return dx, dg


def _dot_exact01(x, u):
    hi = _bf(x)
    lo = _bf(x - hi.astype(F32))
    return (jnp.dot(hi, u, preferred_element_type=F32)
            + jnp.dot(lo, u, preferred_element_type=F32))


def _head_masks():
    lane = lax.broadcasted_iota(jnp.int32, (1, PAIR), 1)
    return lane < HEAD_DIM, lane >= HEAD_DIM


def _ffn_fwd(x, g_pre, g_post, wg, wu, wd, *, name):
    t = x.shape[0]
    tm, tj = 512, 512
    ni, nj = t // tm, D_FF_PAD // tj

    def body(x_ref, gpre_ref, gpost_ref, wg_ref, wu_ref, wd_ref,
             h_ref, n_ref, a_ref, b_ref, f_ref, acc_ref):
        j = pl.program_id(1)

        @pl.when(j == 0)
        def _():
            n_ref[...] = _bf(_rms(x_ref[...], gpre_ref[...]))
            acc_ref[...] = jnp.zeros_like(acc_ref)

        n = n_ref[...]
        a = _dot_nt(n, wg_ref[...])
        b = _dot_nt(n, wu_ref[...])
        a_ref[...] = a
        b_ref[...] = b
        hmid = a * _sigmoid(a) * b
        acc_ref[...] += jnp.dot(_bf(hmid), wd_ref[...], preferred_element_type=F32)

        @pl.when(j == nj - 1)
        def _():
            f = acc_ref[...]
            f_ref[...] = f
            h_ref[...] = x_ref[...] + 0.5 * _rms(f, gpost_ref[...])

    row = pl.BlockSpec((tm, D_MODEL), lambda i, j: (i, 0))
    gain = pl.BlockSpec((1, D_MODEL), lambda i, j: (0, 0))
    col = pl.BlockSpec((tm, tj), lambda i, j: (i, j))
    wtile = pl.BlockSpec((tj, D_MODEL), lambda i, j: (j, 0))
    return pl.pallas_call(
        body, name=name, grid=(ni, nj),
        in_specs=[row, gain, gain, wtile, wtile, wtile],
        out_specs=[row, row, col, col, row],
        out_shape=[_sds((t, D_MODEL)), _sds((t, D_MODEL), BF16),
                   _sds((t, D_FF_PAD)), _sds((t, D_FF_PAD)), _sds((t, D_MODEL))],
        scratch_shapes=[pltpu.VMEM((tm, D_MODEL), F32)],
        compiler_params=_params("arbitrary", "arbitrary"),
    )(x, g_pre, g_post, wg, wu, wd)


def _ffn_bwd(n, df, a, b, wg, wu, wd, *, name):
    t = n.shape[0]
    tj, ts = 256, 512
    nj, ns = D_FF_PAD // tj, t // ts

    def body(n_hbm, df_hbm, a_ref, b_ref, wg_ref, wu_ref, wd_ref,
             dwg_ref, dwu_ref, dwd_ref, dn_hbm,
             n_v, df_v, dn_v, ag, au, ad, sem):
        j = pl.program_id(0)

        @pl.when(j == 0)
        def _():
            c1 = pltpu.make_async_copy(n_hbm, n_v, sem.at[0])
            c2 = pltpu.make_async_copy(df_hbm, df_v, sem.at[1])
            c1.start()
            c2.start()
            dn_v[...] = jnp.zeros_like(dn_v)
            c1.wait()
            c2.wait()

        ag[...] = jnp.zeros_like(ag)
        au[...] = jnp.zeros_like(au)
        ad[...] = jnp.zeros_like(ad)
        wgj, wuj, wdj = wg_ref[...], wu_ref[...], wd_ref[...]
        for s in range(ns):
            rows = pl.ds(s * ts, ts)
            av, bv = a_ref[rows, :], b_ref[rows, :]
            sig = _sigmoid(av)
            silu = av * sig
            dfr = df_v[rows, :]
            nr = n_v[rows, :]
            dhmid = _dot_nt(dfr, wdj)
            da = dhmid * bv * (sig * (1.0 + av * (1.0 - sig)))
            db = dhmid * silu
            ad[...] += _dot_tn(silu * bv, dfr)
            ag[...] += _dot_tn(da, nr)
            au[...] += _dot_tn(db, nr)
            dn_v[rows, :] += _dot(da, wgj) + _dot(db, wuj)
        dwg_ref[...] = _bf(ag[...])
        dwu_ref[...] = _bf(au[...])
        dwd_ref[...] = _bf(ad[...])

        @pl.when(j == nj - 1)
        def _():
            c = pltpu.make_async_copy(dn_v, dn_hbm, sem.at[0])
            c.start()
            c.wait()

    roww = pl.BlockSpec((tj, D_MODEL), lambda j: (j, 0))
    act = pl.BlockSpec((t, tj), lambda j: (0, j))
    return pl.pallas_call(
        body, name=name, grid=(nj,),
        in_specs=[ANY, ANY, act, act, roww, roww, roww],
        out_specs=[roww, roww, roww, ANY],
        out_shape=[_sds((D_FF_PAD, D_MODEL), BF16)] * 3 + [_sds((t, D_MODEL))],
        scratch_shapes=[pltpu.VMEM((t, D_MODEL), BF16), pltpu.VMEM((t, D_MODEL), BF16),
                        pltpu.VMEM((t, D_MODEL), F32)]
        + [pltpu.VMEM((tj, D_MODEL), F32)] * 3 + [pltpu.SemaphoreType.DMA((2,))],
        compiler_params=_params("arbitrary"),
    )(n, df, a, b, wg, wu, wd)


def _junction(dres, pre=None, post=None, *, name):
    t = dres.shape[0]
    tm = 512
    ni = t // tm
    n_in = 1 + (3 if pre else 0) + (2 if post else 0)
    coef = post[2] if post else None

    def body(*refs):
        ins, outs = list(refs[:n_in]), list(refs[n_in:])
        i = pl.program_id(0)
        dh = ins.pop(0)[...]
        if pre:
            dn_ref, x_ref, gpre_ref = ins.pop(0), ins.pop(0), ins.pop(0)
            dh_ref, dgpre_ref = outs.pop(0), outs.pop(0)
            dx, dg = _rms_bwd(dn_ref[...], x_ref[...], gpre_ref[...])
            dh = dh + dx
            dh_ref[...] = dh

            @pl.when(i == 0)
            def _():
                dgpre_ref[...] = jnp.zeros_like(dgpre_ref)
            dgpre_ref[...] += dg
        if post:
            f_ref, gpost_ref = ins.pop(0), ins.pop(0)
            df_ref, dgpost_ref = outs.pop(0), outs.pop(0)
            df, dg = _rms_bwd(coef * dh, f_ref[...], gpost_ref[...])
            df_ref[...] = _bf(df)

            @pl.when(i == 0)
            def _():
                dgpost_ref[...] = jnp.zeros_like(dgpost_ref)
            dgpost_ref[...] += dg

    row = pl.BlockSpec((tm, D_MODEL), lambda i: (i, 0))
    gain = pl.BlockSpec((1, D_MODEL), lambda i: (0, 0))
    args, in_specs, out_specs, out_shape = [dres], [row], [], []
    if pre:
        args += list(pre)
        in_specs += [row, row, gain]
        out_specs += [row, gain]
        out_shape += [_sds((t, D_MODEL)), _sds((1, D_MODEL))]
    if post:
        args += [post[0], post[1]]
        in_specs += [row, gain]
        out_specs += [row, gain]
        out_shape += [_sds((t, D_MODEL), BF16), _sds((1, D_MODEL))]
    return pl.pallas_call(
        body, name=name, grid=(ni,), in_specs=in_specs, out_specs=out_specs,
        out_shape=out_shape, compiler_params=_params("arbitrary"),
    )(*args)


def _qkv_fwd(h, g, win, *, name):
    t = h.shape[0]
    tm, tn = 512, 768
    ni, nj = t // tm, QKV_WIDTH // tn

    def body(h_ref, g_ref, w_ref, qkv_ref, u_ref):
        @pl.when(pl.program_id(1) == 0)
        def _():
            u_ref[...] = _bf(_rms(h_ref[...], g_ref[...]))
        qkv_ref[...] = jnp.dot(u_ref[...], w_ref[...], preferred_element_type=F32)

    row = pl.BlockSpec((tm, D_MODEL), lambda i, j: (i, 0))
    return pl.pallas_call(
        body, name=name, grid=(ni, nj),
        in_specs=[row, pl.BlockSpec((1, D_MODEL), lambda i, j: (0, 0)),
                  pl.BlockSpec((D_MODEL, tn), lambda i, j: (0, j))],
        out_specs=[pl.BlockSpec((tm, tn), lambda i, j: (i, j)), row],
        out_shape=[_sds((t, QKV_WIDTH)), _sds((t, D_MODEL), BF16)],
        compiler_params=_params("arbitrary", "arbitrary"),
    )(h, g, win)


def _qkv_bwd(dq, dk, dv, u, win, *, name):
    t = u.shape[0]
    tn, ts = 512, 512
    nj, ns = QKV_WIDTH // tn, t // ts

    def body(dq_ref, dk_ref, dv_ref, u_ref, w_ref, dw_ref, du_hbm, du_v, acc_ref, sem):
        j = pl.program_id(0)

        @pl.when(j == 0)
        def _():
            du_v[...] = jnp.zeros_like(du_v)

        wj = w_ref[...]
        for role, d_ref in enumerate((dq_ref, dk_ref, dv_ref)):
            @pl.when(j % 3 == role)
            def _():
                acc_ref[...] = jnp.zeros_like(acc_ref)
                for s in range(ns):
                    rows = pl.ds(s * ts, ts)
                    dcol = d_ref[rows, :]
                    acc_ref[...] += _dot_tn(u_ref[rows, :], dcol)
                    du_v[rows, :] += _dot_nt(dcol, wj)
                dw_ref[...] = _bf(acc_ref[...])

        @pl.when(j == nj - 1)
        def _():
            c = pltpu.make_async_copy(du_v, du_hbm, sem)
            c.start()
            c.wait()

    colw = pl.BlockSpec((D_MODEL, tn), lambda j: (0, j))
    grp = pl.BlockSpec((t, tn), lambda j: (0, j // 3))
    return pl.pallas_call(
        body, name=name, grid=(nj,),
        in_specs=[grp, grp, grp, pl.BlockSpec((t, D_MODEL), lambda j: (0, 0)), colw],
        out_specs=[colw, ANY],
        out_shape=[_sds((D_MODEL, QKV_WIDTH), BF16), _sds((t, D_MODEL))],
        scratch_shapes=[pltpu.VMEM((t, D_MODEL), F32), pltpu.VMEM((D_MODEL, tn), F32),
                        pltpu.SemaphoreType.DMA],
        compiler_params=_params("arbitrary"),
    )(dq, dk, dv, u, win)


def _sb_stack(x):
    lo, hi = _head_masks()
    return jnp.concatenate([jnp.where(lo, x, 0.0), jnp.where(hi, x, 0.0)], axis=0)


def _sb_unstack(x2, blk):
    return jnp.where(_head_masks()[0], x2[:blk], x2[blk:])


def _sb_diag_mask(blk):
    r = lax.broadcasted_iota(jnp.int32, (2 * blk, blk), 0) & (blk - 1)
    c = lax.broadcasted_iota(jnp.int32, (2 * blk, blk), 1)
    return c < r


def _tri(n, keep):
    r = lax.broadcasted_iota(jnp.int32, (n, n), 0)
    c = lax.broadcasted_iota(jnp.int32, (n, n), 1)
    return jnp.where(keep(r, c), 1.0, 0.0).astype(BF16)


def _cumsum01(x, u):
    m = x.shape[0]
    hi = _bf(x)
    lo = _bf(x - hi.astype(F32))
    both = jnp.dot(jnp.concatenate([hi, lo], axis=0), u, preferred_element_type=F32)
    return both[:m] + both[m:]


def _sb_fwd(qkv, *, name):
    t = qkv.shape[0]
    blk = SB_BLK
    ni = t // blk

    def body(q_ref, k_ref, v_ref, o_ref, ltot_ref):
        i = pl.program_id(1)
        u_after = _tri(blk, lambda r, c: r > c)
        q2 = _bf(_sb_stack(q_ref[...] * ATT_SCALE))

        def tile(k0, mask, acc, c_l):
            kj = k_ref[pl.ds(k0, blk), :]
            vj = v_ref[pl.ds(k0, blk), :]
            z = _dot_nt(q2, kj)
            sp = _softplus(z)
            lf = -sp if mask is None else jnp.where(mask, -sp, 0.0)
            a = jnp.exp(z - sp + _cumsum01(lf, u_after) + c_l)
            if mask is not None:
                a = jnp.where(mask, a, 0.0)
            return acc + _dot(a, vj), c_l + jnp.sum(lf, axis=1, keepdims=True)

        carry = tile(pl.multiple_of(i * blk, blk), _sb_diag_mask(blk),
                     jnp.zeros((2 * blk, PAIR), F32), jnp.zeros((2 * blk, 1), F32))
        acc, c_l = lax.fori_loop(
            1, i + 1,
            lambda jj, c: tile(pl.multiple_of((i - jj) * blk, blk), None, *c), carry)
        o_ref[...] = _sb_unstack(acc, blk)
        ltot_ref[...] = _sb_unstack(jnp.broadcast_to(c_l, (2 * blk, PAIR)), blk)

    blkspec = pl.BlockSpec((blk, PAIR), lambda p, i: (i, p))
    return pl.pallas_call(
        body, name=name, grid=(N_PAIRS, ni),
        in_specs=[blkspec,
                  pl.BlockSpec((t, PAIR), lambda p, i: (0, N_PAIRS + p)),
                  pl.BlockSpec((t, PAIR), lambda p, i: (0, 2 * N_PAIRS + p))],
        out_specs=[blkspec, blkspec],
        out_shape=[_sds((t, D_MODEL)), _sds((t, D_MODEL // 2))],
        compiler_params=_params("arbitrary", "arbitrary"),
    )(qkv, qkv, qkv)


def _sb_bwd(qkv, ltot, do, *, name):
    t = qkv.shape[0]
    blk = SB_BLK
    ni = t // blk

    def body(q_ref, k_ref, v_ref, lt_ref, do_ref, dq_ref, dkout_ref, dvout_ref, dk_ref, dv_ref):
        i = pl.program_id(1)

        @pl.when(i == 0)
        def _():
            dk_ref[...] = jnp.zeros_like(dk_ref)
            dv_ref[...] = jnp.zeros_like(dv_ref)

        u_upto = _tri(blk, lambda r, c: r <= c)
        u_before = _tri(blk, lambda r, c: r < c)
        lane = lax.broadcasted_iota(jnp.int32, (1, PAIR), 1)
        q2 = _bf(_sb_stack(q_ref[...] * ATT_SCALE))
        do2 = _bf(_sb_stack(do_ref[...]))
        lt_blk = lt_ref[...]
        total = jnp.concatenate(
            [jnp.sum(jnp.where(lane == h * HEAD_DIM, lt_blk, 0.0), axis=1, keepdims=True)
             for h in range(2)], axis=0)

        def tile(k0, mask, dq_acc, c_l, c_g):
            krows = pl.ds(k0, blk)
            kj = k_ref[krows, :]
            vj = v_ref[krows, :]
            z = _dot_nt(q2, kj)
            sp = _softplus(z)
            sig = jnp.exp(z - sp)
            lf = -sp if mask is None else jnp.where(mask, -sp, 0.0)
            a = jnp.exp(z - sp + total - (_cumsum01(lf, u_upto) + c_l))
            if mask is not None:
                a = jnp.where(mask, a, 0.0)
            g = a * _dot_nt(do2, vj)
            g_before = jnp.dot(_bf(g), u_before, preferred_element_type=F32) + c_g
            dz = g * (1.0 - sig) - g_before * sig
            if mask is not None:
                dz = jnp.where(mask, dz, 0.0)
            dk_ref[krows, :] += _dot_tn(dz, q2)
            dv_ref[krows, :] += _dot_tn(a, do2)
            return (dq_acc + _dot(dz, kj), c_l + jnp.sum(lf, axis=1, keepdims=True),
                    c_g + jnp.sum(g, axis=1, keepdims=True))

        carry = lax.fori_loop(
            0, i, lambda j, c: tile(pl.multiple_of(j * blk, blk), None, *c),
            (jnp.zeros((2 * blk, PAIR), F32), jnp.zeros((2 * blk, 1), F32),
             jnp.zeros((2 * blk, 1), F32)))
        dq_acc, _, _ = tile(pl.multiple_of(i * blk, blk), _sb_diag_mask(blk), *carry)
        dq_ref[...] = _bf(_sb_unstack(dq_acc, blk) * ATT_SCALE)

        @pl.when(i == ni - 1)
        def _():
            dkout_ref[...] = _bf(dk_ref[...])
            dvout_ref[...] = _bf(dv_ref[...])

    blkspec = lambda off: pl.BlockSpec((blk, PAIR), lambda p, i: (i, off + p))
    full = lambda off: pl.BlockSpec((t, PAIR), lambda p, i: (0, off + p))
    return pl.pallas_call(
        body, name=name, grid=(N_PAIRS, ni),
        in_specs=[blkspec(0), full(N_PAIRS), full(2 * N_PAIRS), blkspec(0), blkspec(0)],
        out_specs=[blkspec(0), full(0), full(0)],
        out_shape=[_sds((t, D_MODEL), BF16)] * 3,
        scratch_shapes=[pltpu.VMEM((t, PAIR), F32), pltpu.VMEM((t, PAIR), F32)],
        compiler_params=_params("arbitrary", "arbitrary"),
    )(qkv, qkv, qkv, ltot, do)


def _ch_mask(i):
    r = lax.broadcasted_iota(jnp.int32, (CH_QB, CH_WIN), 0)
    c = lax.broadcasted_iota(jnp.int32, (CH_QB, CH_WIN), 1)
    qc = LOOKBACK + lax.shift_right_arithmetic(r, 6)
    kc = lax.shift_right_arithmetic(c, 6)
    first = i * (CH_QB // CHUNK) - LOOKBACK
    return (kc <= qc) & (kc >= qc - LOOKBACK) & (kc + first >= 0)


def _ch_probs(qm, kw, bias_h, mask):
    z = _dot_nt(qm, kw) * ATT_SCALE + bias_h
    z = jnp.where(mask, z, NEG_INF)
    e = jnp.exp(z - jnp.max(z, axis=1, keepdims=True))
    return e / jnp.sum(e, axis=1, keepdims=True)


def _ch_fill(pad_ref, src_ref, t):
    pad_ref[pl.ds(0, CH_LOOK), :] = jnp.zeros((CH_LOOK, PAIR), BF16)
    pad_ref[pl.ds(CH_LOOK, t), :] = _bf(src_ref[...])


def _ch_fwd(qkv, bias, o_in, *, name):
    t = qkv.shape[0]
    ni = t // CH_QB

    def body(q_ref, k_ref, v_ref, bias_ref, _alias, o_ref, kpad, vpad):
        i = pl.program_id(1)

        @pl.when(i == 0)
        def _():
            _ch_fill(kpad, k_ref, t)
            _ch_fill(vpad, v_ref, t)

        win = pl.ds(pl.multiple_of(i * CH_QB, CH_QB), CH_WIN)
        kw, vw = kpad[win, :], vpad[win, :]
        mask = _ch_mask(i)
        q = q_ref[...]
        outs = []
        for h, hm in enumerate(_head_masks()):
            p = _ch_probs(jnp.where(hm, q, 0.0), kw, bias_ref[h], mask)
            outs.append(_dot(p, vw))
        o_ref[...] = jnp.where(_head_masks()[0], outs[0], outs[1])

    full = lambda off: pl.BlockSpec((t, PAIR), lambda p, i: (0, off + p))
    return pl.pallas_call(
        body, name=name, grid=(N_PAIRS, ni),
        in_specs=[pl.BlockSpec((CH_QB, PAIR), lambda p, i: (i, 3 * N_PAIRS + p)),
                  full(4 * N_PAIRS), full(5 * N_PAIRS),
                  pl.BlockSpec((2, CH_QB, CH_WIN), lambda p, i: (p, 0, 0)), ANY],
        out_specs=pl.BlockSpec((CH_QB, PAIR), lambda p, i: (i, N_PAIRS + p)),
        out_shape=_sds((t, D_MODEL)),
        scratch_shapes=[pltpu.VMEM((t + CH_LOOK, PAIR), BF16)] * 2,
        input_output_aliases={4: 0},
        compiler_params=_params("arbitrary", "arbitrary"),
    )(qkv, qkv, qkv, bias, o_in)


def _ch_bwd(qkv, bias, o, do, dq_in, dk_in, dv_in, *, name):
    t = qkv.shape[0]
    ni = t // CH_QB

    def body(q_ref, k_ref, v_ref, bias_ref, o_ref, do_ref, _a0, _a1, _a2,
             dq_ref, dkout_ref, dvout_ref, dbias_ref, kpad, vpad, dkpad, dvpad):
        i = pl.program_id(1)

        @pl.when(i == 0)
        def _():
            _ch_fill(kpad, k_ref, t)
            _ch_fill(vpad, v_ref, t)
            dkpad[...] = jnp.zeros_like(dkpad)
            dvpad[...] = jnp.zeros_like(dvpad)
            dbias_ref[...] = jnp.zeros_like(dbias_ref)

        win = pl.ds(pl.multiple_of(i * CH_QB, CH_QB), CH_WIN)
        kw, vw = kpad[win, :], vpad[win, :]
        mask = _ch_mask(i)
        q, o_blk, do_blk = q_ref[...], o_ref[...], do_ref[...]
        dqs = []
        for h, hm in enumerate(_head_masks()):
            qm = _bf(jnp.where(hm, q, 0.0))
            dom = jnp.where(hm, do_blk, 0.0)
            delta = jnp.sum(dom * o_blk, axis=1, keepdims=True)
            dom = _bf(dom)
            p = _ch_probs(qm, kw, bias_ref[h], mask)
            ds = p * (_dot_nt(dom, vw) - delta)
            dbias_ref[h] += ds
            dsz = ds * ATT_SCALE
            dqs.append(_dot(dsz, kw))
            dkpad[win, :] += _dot_tn(dsz, qm)
            dvpad[win, :] += _dot_tn(p, dom)
        dq_ref[...] = _bf(jnp.where(_head_masks()[0], dqs[0], dqs[1]))

        @pl.when(i == ni - 1)
        def _():
            dkout_ref[...] = _bf(dkpad[pl.ds(CH_LOOK, t), :])
            dvout_ref[...] = _bf(dvpad[pl.ds(CH_LOOK, t), :])

    blkspec = lambda off: pl.BlockSpec((CH_QB, PAIR), lambda p, i: (i, off + p))
    full = lambda off: pl.BlockSpec((t, PAIR), lambda p, i: (0, off + p))
    bias_spec = pl.BlockSpec((2, CH_QB, CH_WIN), lambda p, i: (p, 0, 0))
    return pl.pallas_call(
        body, name=name, grid=(N_PAIRS, ni),
        in_specs=[blkspec(3 * N_PAIRS), full(4 * N_PAIRS), full(5 * N_PAIRS), bias_spec,
                  blkspec(N_PAIRS), blkspec(N_PAIRS), ANY, ANY, ANY],
        out_specs=[blkspec(N_PAIRS), full(N_PAIRS), full(N_PAIRS), bias_spec],
        out_shape=[_sds((t, D_MODEL), BF16)] * 3 + [_sds((2 * N_PAIRS, CH_QB, CH_WIN))],
        scratch_shapes=[pltpu.VMEM((t + CH_LOOK, PAIR), BF16)] * 2
        + [pltpu.VMEM((t + CH_LOOK, PAIR), F32)] * 2,
        input_output_aliases={6: 0, 7: 1, 8: 2},
        compiler_params=_params("arbitrary", "arbitrary"),
    )(qkv, qkv, qkv, bias, o, do, dq_in, dk_in, dv_in)


def _bias_expand(fvec, *, name):
    n_heads = fvec.shape[0]

    def body(f_ref, o_ref, rows8):
        row = f_ref[0]
        for r in range(8):
            rows8[pl.ds(r, 1), :] = pltpu.roll(row, r, 1)
        base = rows8[...]
        for blk in range(CH_QB // 8):
            o_ref[0, pl.ds(8 * blk, 8), :] = pltpu.roll(base, 8 * blk, 1)

    return pl.pallas_call(
        body, name=name, grid=(n_heads,),
        in_specs=[pl.BlockSpec((1, 1, CH_WIN), lambda h: (h, 0, 0))],
        out_specs=pl.BlockSpec((1, CH_QB, CH_WIN), lambda h: (h, 0, 0)),
        out_shape=_sds((n_heads, CH_QB, CH_WIN)),
        scratch_shapes=[pltpu.VMEM((8, CH_WIN), F32)],
        compiler_params=_params("arbitrary"),
    )(fvec)


def _bias_grad(dbias, *, name):
    n_heads = dbias.shape[0]
    first = CH_LOOK - REL_CLIP

    def body(d_ref, o_ref, acc8):
        acc = jnp.zeros((8, CH_WIN), F32)
        for blk in range(CH_QB // 8):
            acc = acc + pltpu.roll(d_ref[0, pl.ds(8 * blk, 8), :], (CH_WIN - 8 * blk) % CH_WIN, 1)
        acc8[...] = acc
        dvec = jnp.zeros((1, CH_WIN), F32)
        for r in range(8):
            dvec = dvec + pltpu.roll(acc8[pl.ds(r, 1), :], (CH_WIN - r) % CH_WIN, 1)
        lane = lax.broadcasted_iota(jnp.int32, (1, CH_WIN), 1)
        clipped = (lane <= first) | (lane >= first + REL_CLIP + CHUNK)
        total = jnp.sum(jnp.where(clipped, dvec, 0.0), axis=1, keepdims=True)
        o_ref[0] = jnp.where(lane == first, total, dvec)

    return pl.pallas_call(
        body, name=name, grid=(n_heads,),
        in_specs=[pl.BlockSpec((1, CH_QB, CH_WIN), lambda h: (h, 0, 0))],
        out_specs=pl.BlockSpec((1, 1, CH_WIN), lambda h: (h, 0, 0)),
        out_shape=_sds((n_heads, 1, CH_WIN)),
        scratch_shapes=[pltpu.VMEM((8, CH_WIN), F32)],
        compiler_params=_params("arbitrary"),
    )(dbias)


def _out_fwd(o, h1, g_sb, g_ch, g_post, wout, *, name):
    t = o.shape[0]
    tm = 512
    half = D_MODEL // 2

    def body(o_ref, h_ref, gsb_ref, gch_ref, gpost_ref, w_ref, h2_ref, mixed_ref, y_ref):
        ov = o_ref[...]
        mixed = jnp.concatenate([_rms(ov[:, :half], gsb_ref[...]),
                                 _rms(ov[:, half:], gch_ref[...])], axis=1)
        mixed_ref[...] = _bf(mixed)
        y = _dot(mixed, w_ref[...])
        y_ref[...] = y
        h2_ref[...] = h_ref[...] + _rms(y, gpost_ref[...])

    row = pl.BlockSpec((tm, D_MODEL), lambda i: (i, 0))
    gain = lambda n: pl.BlockSpec((1, n), lambda i: (0, 0))
    return pl.pallas_call(
        body, name=name, grid=(t // tm,),
        in_specs=[row, row, gain(half), gain(half), gain(D_MODEL),
                  pl.BlockSpec((D_MODEL, D_MODEL), lambda i: (0, 0))],
        out_specs=[row, row, row],
        out_shape=[_sds((t, D_MODEL)), _sds((t, D_MODEL), BF16), _sds((t, D_MODEL))],
        compiler_params=_params("arbitrary"),
    )(o, h1, g_sb, g_ch, g_post, wout)


def _out_bwd(dy, mixed, o, g_sb, g_ch, wout, *, name):
    t = o.shape[0]
    tm = 512
    ni = t // tm
    half = D_MODEL // 2

    def body(dy_ref, mixed_ref, o_ref, gsb_ref, gch_ref, w_ref,
             dw_ref, do_ref, dgsb_ref, dgch_ref, acc_ref):
        i = pl.program_id(0)

        @pl.when(i == 0)
        def _():
            acc_ref[...] = jnp.zeros_like(acc_ref)
            dgsb_ref[...] = jnp.zeros_like(dgsb_ref)
            dgch_ref[...] = jnp.zeros_like(dgch_ref)

        dyv = dy_ref[...]
        acc_ref[...] += _dot_tn(mixed_ref[...], dyv)
        dm = _dot_nt(dyv, w_ref[...])
        ov = o_ref[...]
        doa, dga = _rms_bwd(dm[:, :half], ov[:, :half], gsb_ref[...])
        dob, dgb = _rms_bwd(dm[:, half:], ov[:, half:], gch_ref[...])
        do_ref[...] = jnp.concatenate([doa, dob], axis=1)
        dgsb_ref[...] += dga
        dgch_ref[...] += dgb

        @pl.when(i == ni - 1)
        def _():
            dw_ref[...] = _bf(acc_ref[...])

    row = pl.BlockSpec((tm, D_MODEL), lambda i: (i, 0))
    gain = pl.BlockSpec((1, half), lambda i: (0, 0))
    sq = pl.BlockSpec((D_MODEL, D_MODEL), lambda i: (0, 0))
    return pl.pallas_call(
        body, name=name, grid=(ni,),
        in_specs=[row, row, row, gain, gain, sq],
        out_specs=[sq, row, gain, gain],
        out_shape=[_sds((D_MODEL, D_MODEL), BF16), _sds((t, D_MODEL)),
                   _sds((1, half)), _sds((1, half))],
        scratch_shapes=[pltpu.VMEM((D_MODEL, D_MODEL), F32)],
        compiler_params=_params("arbitrary"),
    )(dy, mixed, o, g_sb, g_ch, wout)


def _ple(p, h3, target, wp, wgate, g, *, name):
    t = h3.shape[0]
    tm = 512
    ni = t // tm

    def body(p_ref, h_ref, tgt_ref, wp_ref, wg_ref, g_ref,
             loss_ref, dres_ref, dwp_ref, dwg_ref, dg_ref, accp, accg):
        i = pl.program_id(0)

        @pl.when(i == 0)
        def _():
            loss_ref[...] = jnp.zeros_like(loss_ref)
            dg_ref[...] = jnp.zeros_like(dg_ref)
            accp[...] = jnp.zeros_like(accp)
            accg[...] = jnp.zeros_like(accg)

        pv, hv, gv = p_ref[...], h_ref[...], g_ref[...]
        pe = _dot(pv, wp_ref[...])
        sig = _sigmoid(_dot(hv, wg_ref[...]))
        e = pe * sig
        err = hv + _rms(e, gv) - tgt_ref[...]
        tok = jnp.mean(err * err, axis=-1, keepdims=True)
        loss_ref[...] += 0.5 * jnp.sum(tok, axis=0, keepdims=True)
        dh4 = err * (1.0 / D_MODEL)
        de, dg = _rms_bwd(dh4, e, gv)
        dg_ref[...] += dg
        dpe = de * sig
        dgt = de * pe * sig * (1.0 - sig)
        accp[...] += _dot_tn(pv, dpe)
        accg[...] += _dot_tn(hv, dgt)
        dres_ref[...] = dh4 + _dot_nt(dgt, wg_ref[...])

        @pl.when(i == ni - 1)
        def _():
            dwp_ref[...] = _bf(accp[...])
            dwg_ref[...] = _bf(accg[...])

    row = pl.BlockSpec((tm, D_MODEL), lambda i: (i, 0))
    const = lambda r, c: pl.BlockSpec((r, c), lambda i: (0, 0))
    return pl.pallas_call(
        body, name=name, grid=(ni,),
        in_specs=[pl.BlockSpec((tm, PLE_DIM), lambda i: (i, 0)), row, row,
                  const(PLE_DIM, D_MODEL), const(D_MODEL, D_MODEL), const(1, D_MODEL)],
        out_specs=[const(1, 128), row, const(PLE_DIM, D_MODEL), const(D_MODEL, D_MODEL),
                   const(1, D_MODEL)],
        out_shape=[_sds((1, 128)), _sds((t, D_MODEL)), _sds((PLE_DIM, D_MODEL), BF16),
                   _sds((D_MODEL, D_MODEL), BF16), _sds((1, D_MODEL))],
        scratch_shapes=[pltpu.VMEM((PLE_DIM, D_MODEL), F32), pltpu.VMEM((D_MODEL, D_MODEL), F32)],
        compiler_params=_params("arbitrary"),
    )(p, h3, target, wp, wgate, g)


def _rel_bias_to_fvec(rel_bias):
    rev = rel_bias[:, ::-1]
    n_heads = rel_bias.shape[0]
    first = CH_LOOK - REL_CLIP
    n_var = REL_CLIP + CHUNK
    clipped = rev[:, :1]
    fvec = jnp.concatenate([jnp.broadcast_to(clipped, (n_heads, first)), rev[:, :n_var],
                            jnp.broadcast_to(clipped, (n_heads, CH_WIN - first - n_var))], axis=1)
    return fvec.reshape(n_heads, 1, CH_WIN)


def _fvec_grad_to_rel_bias(dfvec):
    first = CH_LOOK - REL_CLIP
    n_var = REL_CLIP + CHUNK
    rev = jnp.pad(dfvec[:, 0, first:first + n_var], ((0, 0), (0, N_REL - n_var)))
    return rev[:, ::-1]


def _local_step(x, p, target, g, weights_for, grads_done, fvec):
    w, tie = weights_for(0, x)
    w = dict(w)
    h1, n1, a1, b1, f1 = _ffn_fwd(x, g["ffn1_pre"] + tie, g["ffn1_post"],
                                  w["ffn1_gate"], w["ffn1_up"], w["ffn1_down"], name="ffn1_fwd")
    more, tie = weights_for(1, h1)
    w.update(more)
    qkv, u = _qkv_fwd(h1, g["mix_pre"] + tie, w["in"], name="qkv_fwd")
    bias = _bias_expand(fvec, name="bias_expand")
    o, ltot = _sb_fwd(qkv, name="sb_fwd")
    o = _ch_fwd(qkv, bias, o, name="ch_fwd")
    h2, mixed, y = _out_fwd(o, h1, g["out_sb"], g["out_ch"], g["mix_post"], w["out"], name="out_fwd")
    w.update(weights_for(2, h2)[0])
    h3, n2, a2, b2, f2 = _ffn_fwd(h2, g["ffn2_pre"], g["ffn2_post"],
                                  w["ffn2_gate"], w["ffn2_up"], w["ffn2_down"], name="ffn2_fwd")
    loss, dh3, dwp, dwgate, dg_ple = _ple(p, h3, target, w["ple_proj"], w["ple_gate"],
                                          g["ple_post"], name="ple")
    tie = grads_done(0, {"ple_proj": dwp, "ple_gate": dwgate})

    df2, dg_ffn2_post = _junction(dh3, post=(f2, g["ffn2_post"] + tie, 0.5), name="junction3")
    dwg2, dwu2, dwd2, dn2 = _ffn_bwd(n2, df2, a2, b2, w["ffn2_gate"], w["ffn2_up"],
                                     w["ffn2_down"], name="ffn2_bwd")
    tie = grads_done(1, {"ffn2_gate": dwg2, "ffn2_up": dwu2, "ffn2_down": dwd2})
    dh2, dg_ffn2_pre, dy, dg_mix_post = _junction(
        dh3, pre=(dn2, h2, g["ffn2_pre"] + tie), post=(y, g["mix_post"], 1.0), name="junction2")
    dwout, do, dg_sb, dg_ch = _out_bwd(dy, mixed, o, g["out_sb"], g["out_ch"], w["out"],
                                       name="out_bwd")
    dq, dk, dv = _sb_bwd(qkv, ltot, do, name="sb_bwd")
    dq, dk, dv, dbias = _ch_bwd(qkv, bias, o, do, dq, dk, dv, name="ch_bwd")
    dfvec = _bias_grad(dbias, name="bias_grad")
    dwin, du = _qkv_bwd(dq, dk, dv, u, w["in"], name="qkv_bwd")
    tie = grads_done(2, {"out": dwout, "in": dwin})
    dh1, dg_mix_pre, df1, dg_ffn1_post = _junction(
        dh2, pre=(du, h1, g["mix_pre"] + tie), post=(f1, g["ffn1_post"], 0.5), name="junction1")
    dwg1, dwu1, dwd1, dn1 = _ffn_bwd(n1, df1, a1, b1, w["ffn1_gate"], w["ffn1_up"],
                                     w["ffn1_down"], name="ffn1_bwd")
    tie = grads_done(3, {"ffn1_gate": dwg1, "ffn1_up": dwu1, "ffn1_down": dwd1})
    dx, dg_ffn1_pre = _junction(dh1, pre=(dn1, x, g["ffn1_pre"] + tie), name="junction0")

    dg = {"ffn1_pre": dg_ffn1_pre, "ffn1_post": dg_ffn1_post, "mix_pre": dg_mix_pre,
          "mix_post": dg_mix_post, "out_sb": dg_sb, "out_ch": dg_ch,
          "ffn2_pre": dg_ffn2_pre, "ffn2_post": dg_ffn2_post, "ple_post": dg_ple}
    return loss, dx, dg, dfvec


_WEIGHTS = (
    ("ffn1_gate", "row", FF_SHARD, FF_SHARD_PAD, D_MODEL),
    ("ffn1_up", "row", FF_SHARD, FF_SHARD_PAD, D_MODEL),
    ("ffn1_down", "row", FF_SHARD, FF_SHARD_PAD, D_MODEL),
    ("in", "col", QKV_SHARD, QKV_SHARD, D_MODEL),
    ("out", "row", ROW_SHARD, ROW_SHARD, D_MODEL),
    ("ffn2_gate", "row", FF_SHARD, FF_SHARD_PAD, D_MODEL),
    ("ffn2_up", "row", FF_SHARD, FF_SHARD_PAD, D_MODEL),
    ("ffn2_down", "row", FF_SHARD, FF_SHARD_PAD, D_MODEL),
    ("ple_proj", "col", ROW_SHARD, ROW_SHARD, PLE_DIM),
    ("ple_gate", "row", ROW_SHARD, ROW_SHARD, D_MODEL),
)
_TRANSPOSED = ("ffn1_gate", "ffn1_up", "ffn2_gate", "ffn2_up")
_SPEC = {n: (kind, valid, pad, other) for n, kind, valid, pad, other in _WEIGHTS}
_GATHER_STAGES = (("ffn1_gate", "ffn1_up", "ffn1_down"), ("in", "out"),
                  ("ffn2_gate", "ffn2_up", "ffn2_down", "ple_proj", "ple_gate"))
_SCATTER_STAGES = (("ple_proj", "ple_gate"), ("ffn2_gate", "ffn2_up", "ffn2_down"),
                   ("out", "in"), ("ffn1_gate", "ffn1_up", "ffn1_down"))
HBM = pl.BlockSpec(memory_space=pltpu.HBM)
SEM = pl.BlockSpec(memory_space=pltpu.SEMAPHORE)
EFFECT = pltpu.SideEffectType.DATAFLOW_SIDE_EFFECTING


def _shard_shape(kind, size, other):
    return (other, size) if kind == "col" else (size, other)


def _window(ref, kind, start, size):
    return ref.at[:, pl.ds(start, size)] if kind == "col" else ref.at[pl.ds(start, size), :]


def _device_tuple(k):
    return (k // 4, (k // 2) % 2, k % 2)


def _my_index():
    return 4 * lax.axis_index("x") + 2 * lax.axis_index("y") + lax.axis_index("c")


def _pack_weights(shards):
    nw = len(_WEIGHTS)

    def body(*refs):
        ins, packed, full = refs[:nw], refs[nw:2 * nw], refs[2 * nw:3 * nw]
        sem = refs[3 * nw]
        me = _my_index()
        for (_, kind, valid, pad, _), src, dst in zip(_WEIGHTS, ins, packed):
            if pad != valid:
                dst[...] = jnp.zeros_like(dst)
            if kind == "col":
                dst[:, pl.ds(0, valid)] = _bf(src[...])
            else:
                dst[pl.ds(0, valid), :] = _bf(src[...])
        for k in range(N_DEV):
            @pl.when(me == k)
            def _():
                for w, (_, kind, _, pad, _) in enumerate(_WEIGHTS):
                    pltpu.make_async_copy(packed[w], _window(full[w], kind, k * pad, pad),
                                          sem.at[w]).start()
        for w, (_, kind, _, pad, _) in enumerate(_WEIGHTS):
            pltpu.make_async_copy(packed[w], _window(full[w], kind, 0, pad), sem.at[w]).wait()

    outs = pl.pallas_call(
        body, name="pack_weights",
        in_specs=[VMEM] * nw, out_specs=[VMEM] * nw + [ANY] * nw,
        out_shape=[_sds(_shard_shape(kind, pad, other), BF16) for _, kind, _, pad, other in _WEIGHTS]
        + [_sds(_shard_shape(kind, N_DEV * pad, other), BF16) for _, kind, _, pad, other in _WEIGHTS],
        scratch_shapes=[pltpu.SemaphoreType.DMA((nw,))],
        compiler_params=pltpu.CompilerParams(vmem_limit_bytes=VMEM_LIMIT_BYTES),
    )(*shards)
    names = [n for n, *_ in _WEIGHTS]
    return dict(zip(names, outs[:nw])), dict(zip(names, outs[nw:]))


def _hbm(a):
    return pltpu.with_memory_space_constraint(a, pltpu.HBM)


def _split_start(name, n, body_copies, sources, lands, after):
    arrays = list(sources) + list(lands)
    ns, na = len(sources), len(arrays)

    def body(*refs):
        src, land = refs[:ns], refs[ns:na]
        send, recv = refs[na + 1], refs[na + 2]
        token = refs[-1]
        body_copies(src, land, send, recv)
        token[...] = jnp.zeros_like(token)

    out = pl.pallas_call(
        body, name=name,
        out_shape=(pltpu.SemaphoreType.DMA((n,)), pltpu.SemaphoreType.DMA((n,)),
                   *[pltpu.HBM(a.shape, a.dtype) for a in arrays], _sds((8, 128))),
        in_specs=[HBM] * na + [ANY], out_specs=(SEM, SEM, *[HBM] * na, VMEM),
        input_output_aliases={i: 2 + i for i in range(na)},
        compiler_params=pltpu.CompilerParams(has_side_effects=EFFECT),
    )(*[_hbm(a) for a in arrays], after)
    return out[0], out[1], out[2:2 + ns], out[2 + ns:2 + na], out[-1]


def _split_wait(name, n, seven_of, send, recv, sources, lands, after, keep_sources=False):
    arrays = list(sources) + list(lands)
    ns, na = len(sources), len(arrays)

    def body(*refs):
        land = refs[ns:na]
        send_ref, recv_ref = refs[na], refs[na + 1]
        myself = (lax.axis_index("x"), lax.axis_index("y"), lax.axis_index("c"))
        for w in range(n):
            seven = seven_of(w, land[w])
            copy = pltpu.make_async_remote_copy(
                src_ref=seven, dst_ref=seven, send_sem=send_ref.at[w], recv_sem=recv_ref.at[w],
                device_id=myself, device_id_type=MESH)
            copy.wait_send()
            copy.wait_recv()

    out = pl.pallas_call(
        body, name=name,
        out_shape=[pltpu.HBM(a.shape, a.dtype) for a in arrays],
        in_specs=[HBM] * na + [SEM, SEM, ANY], out_specs=[HBM] * na,
        input_output_aliases={i: i for i in range(na)},
        compiler_params=pltpu.CompilerParams(has_side_effects=EFFECT),
    )(*arrays, send, recv, after)
    return out if keep_sources else out[ns:]


_ALL_PEERS = (1, 2, 3, 4, 5, 6, 7)
_NEAR_PEERS = (1, 2, 4, 6)
_FAR_CHIPS = (2, 4, 6)


def _gather_start(stage, names, packed, full, after, peers=_ALL_PEERS):
    def copies(src, land, send, recv):
        me = _my_index()
        for k in range(N_DEV):
            @pl.when(me == k)
            def _():
                for w, name in enumerate(names):
                    kind, _, pad, _ = _SPEC[name]
                    dst = _window(land[w], kind, k * pad, pad)
                    for mask in peers:
                        pltpu.make_async_remote_copy(
                            src_ref=src[w], dst_ref=dst, send_sem=send.at[w],
                            recv_sem=recv.at[w], device_id=_device_tuple(k ^ mask),
                            device_id_type=MESH).start()

    return _split_start(f"gather_start{stage}", len(names), copies,
                        [packed[n] for n in names], [full[n] for n in names], after)


def _gather_wait(stage, names, started, after, count=N_DEV - 1):
    send, recv, src, land, _ = started

    def bytes_of(w, ref):
        kind, _, pad, _ = _SPEC[names[w]]
        return _window(ref, kind, 0, count * pad)

    return dict(zip(names, _split_wait(f"gather_wait{stage}", len(names), bytes_of,
                                       send, recv, src, land, after)))


def _relay_start(stage, names, full, after):
    def copies(_, land, send, recv):
        me = _my_index()
        for k in range(N_DEV):
            @pl.when(me == k)
            def _():
                for w, name in enumerate(names):
                    kind, _, pad, _ = _SPEC[name]
                    for mask in _FAR_CHIPS:
                        win = _window(land[w], kind, (k ^ mask) * pad, pad)
                        pltpu.make_async_remote_copy(
                            src_ref=win, dst_ref=win, send_sem=send.at[w], recv_sem=recv.at[w],
                            device_id=_device_tuple(k ^ 1), device_id_type=MESH).start()

    return _split_start(f"relay_start{stage}", len(names), copies, [],
                        [full[n] for n in names], after)


def _scatter_start(stage, names, grads):
    def copies(src, land, send, recv):
        me = _my_index()
        for k in range(N_DEV):
            @pl.when(me != k)
            def _():
                slot = lax.rem(me + (N_DEV - 1 - k), N_DEV)
                for w, name in enumerate(names):
                    kind, _, pad, _ = _SPEC[name]
                    pltpu.make_async_remote_copy(
                        src_ref=_window(src[w], kind, k * pad, pad), dst_ref=land[w].at[slot],
                        send_sem=send.at[w], recv_sem=recv.at[w],
                        device_id=_device_tuple(k), device_id_type=MESH).start()

    lands = [lax.empty((N_DEV - 1,) + _shard_shape(_SPEC[m][0], _SPEC[m][2], _SPEC[m][3]), BF16)
             for m in names]
    return _split_start(f"scatter_start{stage}", len(names), copies, grads, lands, grads[0])


def _scatter_wait(stage, names, started, after):
    send, recv, src, land, _ = started
    n = len(names)
    out = _split_wait(f"scatter_wait{stage}", n, lambda w, ref: ref, send, recv, src, land, after,
                      keep_sources=True)
    return dict(zip(names, out[:n])), dict(zip(names, out[n:]))


N_CHIPS = N_DEV // 2


def _pair_start(stage, names, grads):
    def copies(src, land, send, recv):
        me = _my_index()
        for k in range(N_DEV):
            @pl.when(me == k)
            def _():
                for w, name in enumerate(names):
                    kind, _, pad, _ = _SPEC[name]
                    for chip in range(N_CHIPS):
                        j = 2 * chip + ((k ^ 1) & 1)
                        pltpu.make_async_remote_copy(
                            src_ref=_window(src[w], kind, j * pad, pad), dst_ref=land[w].at[chip],
                            send_sem=send.at[w], recv_sem=recv.at[w],
                            device_id=_device_tuple(k ^ 1), device_id_type=MESH).start()

    lands = [lax.empty((N_CHIPS,) + _shard_shape(_SPEC[m][0], _SPEC[m][2], _SPEC[m][3]), BF16)
             for m in names]
    return _split_start(f"pair_start{stage}", len(names), copies, grads, lands, grads[0])


def _pair_sum(dw_full, pair, *, pad, name):
    other = dw_full.shape[1]

    def body(own_ref, pair_ref, out_ref):
        out_ref[0] = _bf(own_ref[...].astype(F32) + pair_ref[0].astype(F32))

    slot = pl.BlockSpec((1, pad, other), lambda q: (q, 0, 0))
    return pl.pallas_call(
        body, name=name, grid=(N_CHIPS,),
        in_specs=[pl.BlockSpec((pad, other), lambda q: (2 * q + lax.axis_index("c"), 0)), slot],
        out_specs=slot, out_shape=_sds((N_CHIPS, pad, other), BF16),
        compiler_params=_params("arbitrary"),
    )(dw_full, pair)


def _chip_start(stage, names, sums, after):
    def copies(src, land, send, recv):
        me = _my_index()
        my_chip = lax.shift_right_logical(me, 1)
        for k in range(N_DEV):
            @pl.when((me != k) & (((me ^ k) & 1) == 0))
            def _():
                slot = lax.rem(my_chip + (N_CHIPS - 1 - k // 2), N_CHIPS)
                for w in range(len(names)):
                    pltpu.make_async_remote_copy(
                        src_ref=src[w].at[k // 2], dst_ref=land[w].at[slot],
                        send_sem=send.at[w], recv_sem=recv.at[w],
                        device_id=_device_tuple(k), device_id_type=MESH).start()

    lands = [lax.empty((N_CHIPS - 1,) + a.shape[1:], BF16) for a in sums]
    return _split_start(f"chip_start{stage}", len(names), copies, sums, lands, after)


def _adamw_chip(w, m, v, land, sums, *, name):
    shape = w.shape

    def body(w_ref, m_ref, v_ref, land_ref, own_ref, *outs):
        rows = pl.ds(0, shape[0])
        grad = own_ref[0, rows, :].astype(F32)
        for s in range(N_CHIPS - 1):
            grad = grad + land_ref[s, rows, :].astype(F32)
        _adam_update(w_ref, m_ref, v_ref, grad, *outs)

    whole = lambda a: pl.BlockSpec(a.shape, lambda i: (0,) * a.ndim)
    own = pl.BlockSpec((1,) + sums.shape[1:],
                       lambda i: (2 * lax.axis_index("x") + lax.axis_index("y"), 0, 0))
    return pl.pallas_call(
        body, name=name, grid=(1,),
        in_specs=[whole(w), whole(m), whole(v), whole(land), own],
        out_specs=[whole(w)] * 4, out_shape=[_sds(shape)] * 4,
        compiler_params=_params("arbitrary"),
    )(w, m, v, land, sums)


def _allreduce_small(small, after):
    shape = small.shape

    def body(in_ref, _after, out_ref, gath, send, recv):
        me = _my_index()
        for k in range(N_DEV):
            @pl.when(me != k)
            def _():
                pltpu.make_async_remote_copy(
                    src_ref=in_ref, dst_ref=gath.at[me], send_sem=send, recv_sem=recv,
                    device_id=_device_tuple(k), device_id_type=MESH).start()

            @pl.when(me == k)
            def _():
                gath[k] = in_ref[...]
        seven = gath.at[pl.ds(0, N_DEV - 1)]
        pltpu.make_async_remote_copy(
            src_ref=seven, dst_ref=seven, send_sem=send, recv_sem=recv,
            device_id=_device_tuple(0), device_id_type=MESH).wait()
        total = gath[0]
        for s in range(1, N_DEV):
            total = total + gath[s]
        out_ref[...] = total

    return pl.pallas_call(
        body, name="allreduce_small",
        in_specs=[VMEM, ANY], out_specs=VMEM, out_shape=_sds(shape),
        scratch_shapes=[pltpu.VMEM((N_DEV,) + shape, F32),
                        pltpu.SemaphoreType.DMA, pltpu.SemaphoreType.DMA],
    )(small, after)


def _adam_update(w_ref, m_ref, v_ref, grad, grad_ref, delta_ref, nm_ref, nv_ref):
    new_m = ADAM_B1 * m_ref[...] + (1.0 - ADAM_B1) * grad
    new_v = ADAM_B2 * v_ref[...] + (1.0 - ADAM_B2) * (grad * grad)
    m_hat = new_m / (1.0 - ADAM_B1 ** ADAM_STEP)
    v_hat = new_v / (1.0 - ADAM_B2 ** ADAM_STEP)
    grad_ref[...] = grad
    delta_ref[...] = -ADAM_LR * (m_hat / (jnp.sqrt(v_hat) + ADAM_EPS) + ADAM_WD * w_ref[...])
    nm_ref[...] = new_m
    nv_ref[...] = new_v


def _adamw(w, m, v, g, *, name):
    def body(w_ref, m_ref, v_ref, g_ref, *outs):
        _adam_update(w_ref, m_ref, v_ref, g_ref[...], *outs)

    return pl.pallas_call(
        body, name=name, in_specs=[VMEM] * 4, out_specs=[VMEM] * 4,
        out_shape=[_sds(w.shape)] * 4,
    )(w, m, v, g)


def _adamw_shard(w, m, v, land, dw_full, *, kind, pad, name):
    shape = w.shape
    other = shape[0] if kind == "col" else shape[1]

    def body(w_ref, m_ref, v_ref, land_ref, own_ref, *outs):
        valid = ((slice(None), pl.ds(0, shape[1])) if kind == "col"
                 else (pl.ds(0, shape[0]), slice(None)))
        grad = own_ref[valid].astype(F32)
        for s in range(N_DEV - 1):
            grad = grad + land_ref[(s,) + valid].astype(F32)
        _adam_update(w_ref, m_ref, v_ref, grad, *outs)

    whole = lambda a: pl.BlockSpec(a.shape, lambda i: (0,) * a.ndim)
    own = pl.BlockSpec(_shard_shape(kind, pad, other),
                       (lambda i: (0, _my_index())) if kind == "col" else (lambda i: (_my_index(), 0)))
    return pl.pallas_call(
        body, name=name, grid=(1,),
        in_specs=[whole(w), whole(m), whole(v), whole(land), own],
        out_specs=[whole(w)] * 4, out_shape=[_sds(shape)] * 4,
        compiler_params=_params("arbitrary"),
    )(w, m, v, land, dw_full)


_GAINS = ("ffn1_pre", "ffn1_post", "mix_pre", "mix_post", "ffn2_pre", "ffn2_post", "ple_post")
_SMALL_ROWS = 16


def _stack_gains(get):
    return jnp.concatenate([get(n) for n in _GAINS]
                           + [jnp.concatenate([get("out_sb"), get("out_ch")], axis=1)], axis=0)


def kernel(x, p, g_ffn1_pre, g_ffn1_post, w_ffn1_gate, w_ffn1_up, w_ffn1_down, g_mix_pre, g_mix_post, w_in, g_out_sb, g_out_ch, rel_bias, w_out, g_ffn2_pre, g_ffn2_post, w_ffn2_gate, w_ffn2_up, w_ffn2_down, w_ple_proj, w_ple_gate, g_ple_post, loss_target, m_g_ffn1_pre, m_g_ffn1_post, m_w_ffn1_gate, m_w_ffn1_up, m_w_ffn1_down, m_g_mix_pre, m_g_mix_post, m_w_in, m_g_out_sb, m_g_out_ch, m_rel_bias, m_w_out, m_g_ffn2_pre, m_g_ffn2_post, m_w_ffn2_gate, m_w_ffn2_up, m_w_ffn2_down, m_w_ple_proj, m_w_ple_gate, m_g_ple_post, v_g_ffn1_pre, v_g_ffn1_post, v_w_ffn1_gate, v_w_ffn1_up, v_w_ffn1_down, v_g_mix_pre, v_g_mix_post, v_w_in, v_g_out_sb, v_g_out_ch, v_rel_bias, v_w_out, v_g_ffn2_pre, v_g_ffn2_post, v_w_ffn2_gate, v_w_ffn2_up, v_w_ffn2_down, v_w_ple_proj, v_w_ple_gate, v_g_ple_post):
    given = dict(locals())
    wnames = [n for n, *_ in _WEIGHTS]

    def shard(prefix, n):
        a = given[prefix + "w_" + n][0]
        return a.T if n in _TRANSPOSED else a

    packed, full = _pack_weights([shard("", n) for n in wnames])
    first = _GATHER_STAGES[0]
    gathers = {0: _gather_start(0, first, packed, full, packed[first[0]], peers=_NEAR_PEERS)}

    def weights_for(stage, after):
        names = _GATHER_STAGES[stage]
        if stage == 0:
            near = _gather_wait(0, names, gathers[0], after, count=len(_NEAR_PEERS))
            relay = _relay_start(0, names, near, near[names[0]])
            ws = _gather_wait("0r", names, relay, relay[-1], count=len(_FAR_CHIPS))
        else:
            ws = _gather_wait(stage, names, gathers[stage], after)
        if stage + 1 == len(_GATHER_STAGES):
            return ws, jnp.zeros((1, 1), F32)
        gathers[stage + 1] = _gather_start(stage + 1, _GATHER_STAGES[stage + 1], packed, full,
                                           ws[names[0]])
        return ws, gathers[stage + 1][-1][:1, :1]

    scatters = {}

    last = len(_SCATTER_STAGES) - 1

    def grads_done(stage, grads):
        names = _SCATTER_STAGES[stage]
        start = _pair_start if stage == last else _scatter_start
        scatters[stage] = start(stage, names, [grads[n] for n in names])
        return scatters[stage][-1][:1, :1]

    gains = {n: given["g_" + n] for n in _GAINS + ("out_sb", "out_ch")}
    fvec = _rel_bias_to_fvec(rel_bias[0])
    loss, dx, dg, dfvec = _local_step(x[0], p[0, 0], loss_target[0], gains,
                                      weights_for, grads_done, fvec)

    results = {}

    def finish(stage, after):
        names = _SCATTER_STAGES[stage]
        dws, lands = _scatter_wait(stage, names, scatters[stage], after)
        for n in names:
            kind, _, pad, _ = _SPEC[n]
            out = _adamw_shard(shard("", n), shard("m_", n), shard("v_", n), lands[n], dws[n],
                               kind=kind, pad=pad, name="adamw_" + n)
            results["w_" + n] = [a.T for a in out] if n in _TRANSPOSED else out
        return results["w_" + names[-1]][0]

    names = _SCATTER_STAGES[last]
    whole = lambda w, ref: ref
    send, recv, src, land, _ = scatters[last]
    out = _split_wait(f"pair_wait{last}", len(names), whole, send, recv, src, land, dx,
                      keep_sources=True)
    sums = [_pair_sum(dwf, pair, pad=_SPEC[n][2], name="pair_sum_" + n)
            for n, dwf, pair in zip(names, out[:len(names)], out[len(names):])]
    send, recv, src, land, after = _chip_start(last, names, sums, sums[0])
    for stage in range(last):
        after = finish(stage, after)
    out = _split_wait(f"chip_wait{last}", len(names), whole, send, recv, src, land, after,
                      keep_sources=True)
    for n, own, landed in zip(names, out[:len(names)], out[len(names):]):
        res = _adamw_chip(shard("", n), shard("m_", n), shard("v_", n), landed, own,
                          name="adamw_" + n)
        results["w_" + n] = [a.T for a in res] if n in _TRANSPOSED else res
        after = res[0]
    dfv = jnp.pad(dfvec[:, 0, :], ((0, 0), (0, D_MODEL - CH_WIN)))
    small = _allreduce_small(jnp.concatenate([_stack_gains(lambda n: dg[n]), dfv], axis=0), after)
    stacked = _adamw(_stack_gains(lambda n: given["g_" + n]),
                     _stack_gains(lambda n: given["m_g_" + n]),
                     _stack_gains(lambda n: given["v_g_" + n]),
                     small[:N_DEV], name="adamw_gains")
    half = D_MODEL // 2
    for r, n in enumerate(_GAINS):
        results["g_" + n] = [a[r:r + 1] for a in stacked]
    results["g_out_sb"] = [a[N_DEV - 1:N_DEV, :half] for a in stacked]
    results["g_out_ch"] = [a[N_DEV - 1:N_DEV, half:] for a in stacked]
    d_rel = _fvec_grad_to_rel_bias(small[N_DEV:, :CH_WIN].reshape(N_DEV, 1, CH_WIN))
    results["rel_bias"] = _adamw(rel_bias[0], m_rel_bias[0], v_rel_bias[0], d_rel,
                                 name="adamw_rel_bias")

    order = ("g_ffn1_pre", "g_ffn1_post", "w_ffn1_gate", "w_ffn1_up", "w_ffn1_down",
             "g_mix_pre", "g_mix_post", "w_in", "g_out_sb", "g_out_ch", "rel_bias", "w_out",
             "g_ffn2_pre", "g_ffn2_post", "w_ffn2_gate", "w_ffn2_up", "w_ffn2_down",
             "w_ple_proj", "w_ple_gate", "g_ple_post")

    def leaf(name, idx):
        a = results[name][idx]
        return a if name.startswith("g_") else a[None]

    total_loss = lax.psum(loss[0, 0], ("x", "y", "c"))
    return (total_loss, dx[None],
            *[leaf(n, 0) for n in order], *[leaf(n, 1) for n in order],
            *[leaf(n, 2) for n in order], *[leaf(n, 3) for n in order])
```

```python
import jax
import jax.numpy as jnp
from jax import lax
from jax.experimental import pallas as pl
from jax.experimental.pallas import tpu as pltpu

F32 = jnp.float32
BF16 = jnp.bfloat16

N_DEV = 8
D_MODEL = 1024
D_FF = 2816
FF_SHARD = D_FF // N_DEV
FF_SHARD_PAD = 384
D_FF_PAD = FF_SHARD_PAD * N_DEV
QKV_WIDTH = 3 * D_MODEL
QKV_SHARD = QKV_WIDTH // N_DEV
PLE_DIM = 256
ROW_SHARD = D_MODEL // N_DEV
HEAD_DIM = 64
PAIR = 2 * HEAD_DIM
N_PAIRS = 4
CHUNK = 64
LOOKBACK = 8
REL_CLIP = 128
N_REL = 2 * REL_CLIP + 1
CH_QB = 256
CH_LOOK = LOOKBACK * CHUNK
CH_WIN = CH_LOOK + CH_QB
SB_BLK = 256
SB_GROUP = 2
SB_LANES = tuple(slice(g * 128, (g + 1) * 128) for g in range(SB_GROUP))
EPS = 1e-6
NEG_INF = -1e30
ATT_SCALE = HEAD_DIM ** -0.5
ADAM_LR = 0.001
ADAM_B1 = 0.9
ADAM_B2 = 0.999
ADAM_EPS = 1e-08
ADAM_WD = 0.01
ADAM_STEP = 10
VMEM_LIMIT_BYTES = 48 * 1024 * 1024
MESH = pl.DeviceIdType.MESH

ANY = pl.BlockSpec(memory_space=pl.ANY)
VMEM = pl.BlockSpec(memory_space=pltpu.VMEM)


def _params(*sem):
    return pltpu.CompilerParams(dimension_semantics=sem or None,
                                vmem_limit_bytes=VMEM_LIMIT_BYTES)


def _sds(shape, dtype=F32):
    return jax.ShapeDtypeStruct(shape, dtype)


def _bf(x):
    return x.astype(BF16)


def _dot(a, b):
    return jnp.dot(_bf(a), _bf(b), preferred_element_type=F32)


def _dot_nt(a, b):
    return lax.dot_general(_bf(a), _bf(b), (((1,), (1,)), ((), ())),
                           preferred_element_type=F32)


def _dot_tn(a, b):
    return lax.dot_general(_bf(a), _bf(b), (((0,), (0,)), ((), ())),
                           preferred_element_type=F32)


def _sigmoid(x):
    return 1.0 / (1.0 + jnp.exp(-x))


def _softplus(x):
    return jnp.maximum(x, 0.0) + jnp.log(1.0 + jnp.exp(-jnp.abs(x)))


def _rstd(x):
    return lax.rsqrt(jnp.mean(x * x, axis=-1, keepdims=True) + EPS)


def _rms(x, g):
    return x * _rstd(x) * g


def _rms_bwd(dy, x, g):
    r = _rstd(x)
    w = dy * g
    dx = r * (w - x * (r * r) * jnp.mean(w * x, axis=-1, keepdims=True))
    dg = jnp.sum(dy * (x * r), axis=0, keepdims=True)
    return dx, dg


def _head_masks():
    lane = lax.broadcasted_iota(jnp.int32, (1, PAIR), 1)
    return lane < HEAD_DIM, lane >= HEAD_DIM


def _ffn_fwd(x, g_pre, g_post, wg, wu, wd, *, name):
    t = x.shape[0]
    tm, tj = 512, 512
    ni, nj = t // tm, D_FF_PAD // tj

    def body(x_ref, gpre_ref, gpost_ref, wg_ref, wu_ref, wd_ref,
             h_ref, n_ref, a_ref, b_ref, f_ref, acc_ref):
        j = pl.program_id(1)

        @pl.when(j == 0)
        def _():
            n_ref[...] = _bf(_rms(x_ref[...], gpre_ref[...]))
            acc_ref[...] = jnp.zeros_like(acc_ref)

        n = n_ref[...]
        a = _dot_nt(n, wg_ref[...])
        b = _dot_nt(n, wu_ref[...])
        a_ref[...] = a
        b_ref[...] = b
        hmid = a * _sigmoid(a) * b
        acc_ref[...] += jnp.dot(_bf(hmid), wd_ref[...], preferred_element_type=F32)

        @pl.when(j == nj - 1)
        def _():
            f = acc_ref[...]
            f_ref[...] = f
            h_ref[...] = x_ref[...] + 0.5 * _rms(f, gpost_ref[...])

    row = pl.BlockSpec((tm, D_MODEL), lambda i, j: (i, 0))
    gain = pl.BlockSpec((1, D_MODEL), lambda i, j: (0, 0))
    col = pl.BlockSpec((tm, tj), lambda i, j: (i, j))
    wtile = pl.BlockSpec((tj, D_MODEL), lambda i, j: (j, 0))
    return pl.pallas_call(
        body, name=name, grid=(ni, nj),
        in_specs=[row, gain, gain, wtile, wtile, wtile],
        out_specs=[row, row, col, col, row],
        out_shape=[_sds((t, D_MODEL)), _sds((t, D_MODEL), BF16),
                   _sds((t, D_FF_PAD)), _sds((t, D_FF_PAD)), _sds((t, D_MODEL))],
        scratch_shapes=[pltpu.VMEM((tm, D_MODEL), F32)],
        compiler_params=_params("arbitrary", "arbitrary"),
    )(x, g_pre, g_post, wg, wu, wd)


def _ffn_bwd(n, df, a, b, wg, wu, wd, *, name):
    t = n.shape[0]
    tj, ts = 256, 512
    nj, ns = D_FF_PAD // tj, t // ts

    def body(n_hbm, df_hbm, a_ref, b_ref, wg_ref, wu_ref, wd_ref,
             dwg_ref, dwu_ref, dwd_ref, dn_hbm,
             n_v, df_v, dn_v, ag, au, ad, sem):
        j = pl.program_id(0)

        @pl.when(j == 0)
        def _():
            c1 = pltpu.make_async_copy(n_hbm, n_v, sem.at[0])
            c2 = pltpu.make_async_copy(df_hbm, df_v, sem.at[1])
            c1.start()
            c2.start()
            dn_v[...] = jnp.zeros_like(dn_v)
            c1.wait()
            c2.wait()

        ag[...] = jnp.zeros_like(ag)
        au[...] = jnp.zeros_like(au)
        ad[...] = jnp.zeros_like(ad)
        wgj, wuj, wdj = wg_ref[...], wu_ref[...], wd_ref[...]
        for s in range(ns):
            rows = pl.ds(s * ts, ts)
            av, bv = a_ref[rows, :], b_ref[rows, :]
            sig = _sigmoid(av)
            silu = av * sig
            dfr = df_v[rows, :]
            nr = n_v[rows, :]
            dhmid = _dot_nt(dfr, wdj)
            da = dhmid * bv * (sig * (1.0 + av * (1.0 - sig)))
            db = dhmid * silu
            ad[...] += _dot_tn(silu * bv, dfr)
            ag[...] += _dot_tn(da, nr)
            au[...] += _dot_tn(db, nr)
            dn_v[rows, :] += _dot(da, wgj) + _dot(db, wuj)
        dwg_ref[...] = _bf(ag[...])
        dwu_ref[...] = _bf(au[...])
        dwd_ref[...] = _bf(ad[...])

        @pl.when(j == nj - 1)
        def _():
            c = pltpu.make_async_copy(dn_v, dn_hbm, sem.at[0])
            c.start()
            c.wait()

    roww = pl.BlockSpec((tj, D_MODEL), lambda j: (j, 0))
    act = pl.BlockSpec((t, tj), lambda j: (0, j))
    return pl.pallas_call(
        body, name=name, grid=(nj,),
        in_specs=[ANY, ANY, act, act, roww, roww, roww],
        out_specs=[roww, roww, roww, ANY],
        out_shape=[_sds((D_FF_PAD, D_MODEL), BF16)] * 3 + [_sds((t, D_MODEL))],
        scratch_shapes=[pltpu.VMEM((t, D_MODEL), BF16), pltpu.VMEM((t, D_MODEL), BF16),
                        pltpu.VMEM((t, D_MODEL), F32)]
        + [pltpu.VMEM((tj, D_MODEL), F32)] * 3 + [pltpu.SemaphoreType.DMA((2,))],
        compiler_params=_params("arbitrary"),
    )(n, df, a, b, wg, wu, wd)


def _junction(dres, pre=None, post=None, *, name):
    t = dres.shape[0]
    tm = 512
    ni = t // tm
    n_in = 1 + (3 if pre else 0) + (2 if post else 0)
    coef = post[2] if post else None

    def body(*refs):
        ins, outs = list(refs[:n_in]), list(refs[n_in:])
        i = pl.program_id(0)
        dh = ins.pop(0)[...]
        if pre:
            dn_ref, x_ref, gpre_ref = ins.pop(0), ins.pop(0), ins.pop(0)
            dh_ref, dgpre_ref = outs.pop(0), outs.pop(0)
            dx, dg = _rms_bwd(dn_ref[...], x_ref[...], gpre_ref[...])
            dh = dh + dx
            dh_ref[...] = dh

            @pl.when(i == 0)
            def _():
                dgpre_ref[...] = jnp.zeros_like(dgpre_ref)
            dgpre_ref[...] += dg
        if post:
            f_ref, gpost_ref = ins.pop(0), ins.pop(0)
            df_ref, dgpost_ref = outs.pop(0), outs.pop(0)
            df, dg = _rms_bwd(coef * dh, f_ref[...], gpost_ref[...])
            df_ref[...] = _bf(df)

            @pl.when(i == 0)
            def _():
                dgpost_ref[...] = jnp.zeros_like(dgpost_ref)
            dgpost_ref[...] += dg

    row = pl.BlockSpec((tm, D_MODEL), lambda i: (i, 0))
    gain = pl.BlockSpec((1, D_MODEL), lambda i: (0, 0))
    args, in_specs, out_specs, out_shape = [dres], [row], [], []
    if pre:
        args += list(pre)
        in_specs += [row, row, gain]
        out_specs += [row, gain]
        out_shape += [_sds((t, D_MODEL)), _sds((1, D_MODEL))]
    if post:
        args += [post[0], post[1]]
        in_specs += [row, gain]
        out_specs += [row, gain]
        out_shape += [_sds((t, D_MODEL), BF16), _sds((1, D_MODEL))]
    return pl.pallas_call(
        body, name=name, grid=(ni,), in_specs=in_specs, out_specs=out_specs,
        out_shape=out_shape, compiler_params=_params("arbitrary"),
    )(*args)


def _qkv_fwd(h, g, win, *, name):
    t = h.shape[0]
    tm, tn = 512, 768
    ni, nj = t // tm, QKV_WIDTH // tn

    def body(h_ref, g_ref, w_ref, qkv_ref, u_ref):
        @pl.when(pl.program_id(1) == 0)
        def _():
            u_ref[...] = _bf(_rms(h_ref[...], g_ref[...]))
        qkv_ref[...] = jnp.dot(u_ref[...], w_ref[...], preferred_element_type=F32)

    row = pl.BlockSpec((tm, D_MODEL), lambda i, j: (i, 0))
    return pl.pallas_call(
        body, name=name, grid=(ni, nj),
        in_specs=[row, pl.BlockSpec((1, D_MODEL), lambda i, j: (0, 0)),
                  pl.BlockSpec((D_MODEL, tn), lambda i, j: (0, j))],
        out_specs=[pl.BlockSpec((tm, tn), lambda i, j: (i, j)), row],
        out_shape=[_sds((t, QKV_WIDTH)), _sds((t, D_MODEL), BF16)],
        compiler_params=_params("arbitrary", "arbitrary"),
    )(h, g, win)


def _qkv_bwd(dq, dk, dv, u, win, *, name):
    t = u.shape[0]
    tn, ts = 512, 512
    nj, ns = QKV_WIDTH // tn, t // ts

    def body(dq_ref, dk_ref, dv_ref, u_ref, w_ref, dw_ref, du_hbm, du_v, acc_ref, sem):
        j = pl.program_id(0)

        @pl.when(j == 0)
        def _():
            du_v[...] = jnp.zeros_like(du_v)

        wj = w_ref[...]
        for role, d_ref in enumerate((dq_ref, dk_ref, dv_ref)):
            @pl.when(j % 3 == role)
            def _():
                acc_ref[...] = jnp.zeros_like(acc_ref)
                for s in range(ns):
                    rows = pl.ds(s * ts, ts)
                    dcol = d_ref[rows, :]
                    acc_ref[...] += _dot_tn(u_ref[rows, :], dcol)
                    du_v[rows, :] += _dot_nt(dcol, wj)
                dw_ref[...] = _bf(acc_ref[...])

        @pl.when(j == nj - 1)
        def _():
            c = pltpu.make_async_copy(du_v, du_hbm, sem)
            c.start()
            c.wait()

    colw = pl.BlockSpec((D_MODEL, tn), lambda j: (0, j))
    grp = pl.BlockSpec((t, tn), lambda j: (0, j // 3))
    return pl.pallas_call(
        body, name=name, grid=(nj,),
        in_specs=[grp, grp, grp, pl.BlockSpec((t, D_MODEL), lambda j: (0, 0)), colw],
        out_specs=[colw, ANY],
        out_shape=[_sds((D_MODEL, QKV_WIDTH), BF16), _sds((t, D_MODEL))],
        scratch_shapes=[pltpu.VMEM((t, D_MODEL), F32), pltpu.VMEM((D_MODEL, tn), F32),
                        pltpu.SemaphoreType.DMA],
        compiler_params=_params("arbitrary"),
    )(dq, dk, dv, u, win)


def _sb_stack(x):
    lo, hi = _head_masks()
    return jnp.concatenate([jnp.where(lo, x, 0.0), jnp.where(hi, x, 0.0)], axis=0)


def _sb_unstack(x2, blk):
    return jnp.where(_head_masks()[0], x2[:blk], x2[blk:])


def _sb_diag_mask(blk):
    r = lax.broadcasted_iota(jnp.int32, (2 * blk, blk), 0) & (blk - 1)
    c = lax.broadcasted_iota(jnp.int32, (2 * blk, blk), 1)
    return c < r


def _tri(n, keep):
    r = lax.broadcasted_iota(jnp.int32, (n, n), 0)
    c = lax.broadcasted_iota(jnp.int32, (n, n), 1)
    return jnp.where(keep(r, c), 1.0, 0.0).astype(BF16)


def _cumsum01(x, u):
    m = x.shape[0]
    hi = _bf(x)
    lo = _bf(x - hi.astype(F32))
    both = jnp.dot(jnp.concatenate([hi, lo], axis=0), u, preferred_element_type=F32)
    return both[:m] + both[m:]


def _sb_fwd(qkv, *, name):
    t = qkv.shape[0]
    blk = SB_BLK
    ni = t // blk

    def body(q_ref, k_ref, v_ref, o_ref, ltot_ref):
        i = pl.program_id(1)
        u_after = _tri(blk, lambda r, c: r > c)
        q2 = [_bf(_sb_stack(q_ref[:, lanes] * ATT_SCALE)) for lanes in SB_LANES]

        def tile(g, k0, mask, acc, c_l):
            kj = k_ref[pl.ds(k0, blk), SB_LANES[g]]
            vj = v_ref[pl.ds(k0, blk), SB_LANES[g]]
            z = _dot_nt(q2[g], kj)
            sp = _softplus(z)
            lf = -sp if mask is None else jnp.where(mask, -sp, 0.0)
            a = jnp.exp(z - sp + _cumsum01(lf, u_after) + c_l)
            if mask is not None:
                a = jnp.where(mask, a, 0.0)
            return acc + _dot(a, vj), c_l + jnp.sum(lf, axis=1, keepdims=True)

        def tiles(k0, mask, carry):
            return tuple(tile(g, k0, mask, *carry[g]) for g in range(SB_GROUP))

        zero = (jnp.zeros((2 * blk, PAIR), F32), jnp.zeros((2 * blk, 1), F32))
        carry = tiles(pl.multiple_of(i * blk, blk), _sb_diag_mask(blk), (zero,) * SB_GROUP)
        carry = lax.fori_loop(
            1, i + 1, lambda jj, c: tiles(pl.multiple_of((i - jj) * blk, blk), None, c), carry)
        for g, (acc, c_l) in enumerate(carry):
            o_ref[:, SB_LANES[g]] = _sb_unstack(acc, blk)
            ltot_ref[:, SB_LANES[g]] = _sb_unstack(jnp.broadcast_to(c_l, (2 * blk, PAIR)), blk)

    width = SB_GROUP * PAIR
    blkspec = pl.BlockSpec((blk, width), lambda p, i: (i, p))
    n_steps = N_PAIRS // SB_GROUP
    return pl.pallas_call(
        body, name=name, grid=(n_steps, ni),
        in_specs=[blkspec,
                  pl.BlockSpec((t, width), lambda p, i: (0, n_steps + p)),
                  pl.BlockSpec((t, width), lambda p, i: (0, 2 * n_steps + p))],
        out_specs=[blkspec, blkspec],
        out_shape=[_sds((t, D_MODEL)), _sds((t, D_MODEL // 2))],
        compiler_params=_params("arbitrary", "arbitrary"),
    )(qkv, qkv, qkv)


def _sb_bwd(qkv, ltot, do, *, name):
    t = qkv.shape[0]
    blk = SB_BLK
    ni = t // blk

    def body(q_ref, k_ref, v_ref, lt_ref, do_ref, dq_ref, dkout_ref, dvout_ref, dk_ref, dv_ref):
        i = pl.program_id(1)

        @pl.when(i == 0)
        def _():
            dk_ref[...] = jnp.zeros_like(dk_ref)
            dv_ref[...] = jnp.zeros_like(dv_ref)

        u_upto = _tri(blk, lambda r, c: r <= c)
        u_before = _tri(blk, lambda r, c: r < c)
        lane = lax.broadcasted_iota(jnp.int32, (1, PAIR), 1)
        q2 = [_bf(_sb_stack(q_ref[:, lanes] * ATT_SCALE)) for lanes in SB_LANES]
        do2 = [_bf(_sb_stack(do_ref[:, lanes])) for lanes in SB_LANES]
        total = [jnp.concatenate(
            [jnp.sum(jnp.where(lane == h * HEAD_DIM, lt_ref[:, lanes], 0.0), axis=1, keepdims=True)
             for h in range(2)], axis=0) for lanes in SB_LANES]

        def tile(g, k0, mask, dq_acc, c_l, c_g):
            krows = pl.ds(k0, blk)
            kj = k_ref[krows, SB_LANES[g]]
            vj = v_ref[krows, SB_LANES[g]]
            z = _dot_nt(q2[g], kj)
            sp = _softplus(z)
            sig = jnp.exp(z - sp)
            lf = -sp if mask is None else jnp.where(mask, -sp, 0.0)
            a = jnp.exp(z - sp + total[g] - (_cumsum01(lf, u_upto) + c_l))
            if mask is not None:
                a = jnp.where(mask, a, 0.0)
            gw = a * _dot_nt(do2[g], vj)
            g_before = jnp.dot(_bf(gw), u_before, preferred_element_type=F32) + c_g
            dz = gw * (1.0 - sig) - g_before * sig
            if mask is not None:
                dz = jnp.where(mask, dz, 0.0)
            dk_ref[krows, SB_LANES[g]] += _dot_tn(dz, q2[g])
            dv_ref[krows, SB_LANES[g]] += _dot_tn(a, do2[g])
            return (dq_acc + _dot(dz, kj), c_l + jnp.sum(lf, axis=1, keepdims=True),
                    c_g + jnp.sum(gw, axis=1, keepdims=True))

        def tiles(k0, mask, carry):
            return tuple(tile(g, k0, mask, *carry[g]) for g in range(SB_GROUP))

        zero = (jnp.zeros((2 * blk, PAIR), F32), jnp.zeros((2 * blk, 1), F32),
                jnp.zeros((2 * blk, 1), F32))
        carry = lax.fori_loop(
            0, i, lambda j, c: tiles(pl.multiple_of(j * blk, blk), None, c), (zero,) * SB_GROUP)
        carry = tiles(pl.multiple_of(i * blk, blk), _sb_diag_mask(blk), carry)
        for g, (dq_acc, _, _) in enumerate(carry):
            dq_ref[:, SB_LANES[g]] = _bf(_sb_unstack(dq_acc, blk) * ATT_SCALE)

        @pl.when(i == ni - 1)
        def _():
            dkout_ref[...] = _bf(dk_ref[...])
            dvout_ref[...] = _bf(dv_ref[...])

    width = SB_GROUP * PAIR
    n_steps = N_PAIRS // SB_GROUP
    blkspec = lambda off: pl.BlockSpec((blk, width), lambda p, i: (i, off + p))
    full = lambda off: pl.BlockSpec((t, width), lambda p, i: (0, off + p))
    return pl.pallas_call(
        body, name=name, grid=(n_steps, ni),
        in_specs=[blkspec(0), full(n_steps), full(2 * n_steps), blkspec(0), blkspec(0)],
        out_specs=[blkspec(0), full(0), full(0)],
        out_shape=[_sds((t, D_MODEL), BF16)] * 3,
        scratch_shapes=[pltpu.VMEM((t, width), F32), pltpu.VMEM((t, width), F32)],
        compiler_params=_params("arbitrary", "arbitrary"),
    )(qkv, qkv, qkv, ltot, do)


def _ch_mask(i):
    r = lax.broadcasted_iota(jnp.int32, (CH_QB, CH_WIN), 0)
    c = lax.broadcasted_iota(jnp.int32, (CH_QB, CH_WIN), 1)
    qc = LOOKBACK + lax.shift_right_arithmetic(r, 6)
    kc = lax.shift_right_arithmetic(c, 6)
    first = i * (CH_QB // CHUNK) - LOOKBACK
    return (kc <= qc) & (kc >= qc - LOOKBACK) & (kc + first >= 0)


def _ch_probs(qm, kw, bias_h, mask):
    z = _dot_nt(qm, kw) * ATT_SCALE + bias_h
    z = jnp.where(mask, z, NEG_INF)
    e = jnp.exp(z - jnp.max(z, axis=1, keepdims=True))
    return e / jnp.sum(e, axis=1, keepdims=True)


def _ch_fill(pad_ref, src_ref, t):
    pad_ref[pl.ds(0, CH_LOOK), :] = jnp.zeros((CH_LOOK, PAIR), BF16)
    pad_ref[pl.ds(CH_LOOK, t), :] = _bf(src_ref[...])


def _ch_fwd(qkv, bias, o_in, *, name):
    t = qkv.shape[0]
    ni = t // CH_QB

    def body(q_ref, k_ref, v_ref, bias_ref, _alias, o_ref, kpad, vpad):
        i = pl.program_id(1)

        @pl.when(i == 0)
        def _():
            _ch_fill(kpad, k_ref, t)
            _ch_fill(vpad, v_ref, t)

        win = pl.ds(pl.multiple_of(i * CH_QB, CH_QB), CH_WIN)
        kw, vw = kpad[win, :], vpad[win, :]
        mask = _ch_mask(i)
        q = q_ref[...]
        outs = []
        for h, hm in enumerate(_head_masks()):
            p = _ch_probs(jnp.where(hm, q, 0.0), kw, bias_ref[h], mask)
            outs.append(_dot(p, vw))
        o_ref[...] = jnp.where(_head_masks()[0], outs[0], outs[1])

    full = lambda off: pl.BlockSpec((t, PAIR), lambda p, i: (0, off + p))
    return pl.pallas_call(
        body, name=name, grid=(N_PAIRS, ni),
        in_specs=[pl.BlockSpec((CH_QB, PAIR), lambda p, i: (i, 3 * N_PAIRS + p)),
                  full(4 * N_PAIRS), full(5 * N_PAIRS),
                  pl.BlockSpec((2, CH_QB, CH_WIN), lambda p, i: (p, 0, 0)), ANY],
        out_specs=pl.BlockSpec((CH_QB, PAIR), lambda p, i: (i, N_PAIRS + p)),
        out_shape=_sds((t, D_MODEL)),
        scratch_shapes=[pltpu.VMEM((t + CH_LOOK, PAIR), BF16)] * 2,
        input_output_aliases={4: 0},
        compiler_params=_params("arbitrary", "arbitrary"),
    )(qkv, qkv, qkv, bias, o_in)


def _ch_bwd(qkv, bias, o, do, dq_in, dk_in, dv_in, *, name):
    t = qkv.shape[0]
    ni = t // CH_QB

    def body(q_ref, k_ref, v_ref, bias_ref, o_ref, do_ref, _a0, _a1, _a2,
             dq_ref, dkout_ref, dvout_ref, dbias_ref, kpad, vpad, dkpad, dvpad):
        i = pl.program_id(1)

        @pl.when(i == 0)
        def _():
            _ch_fill(kpad, k_ref, t)
            _ch_fill(vpad, v_ref, t)
            dkpad[...] = jnp.zeros_like(dkpad)
            dvpad[...] = jnp.zeros_like(dvpad)
            dbias_ref[...] = jnp.zeros_like(dbias_ref)

        win = pl.ds(pl.multiple_of(i * CH_QB, CH_QB), CH_WIN)
        kw, vw = kpad[win, :], vpad[win, :]
        mask = _ch_mask(i)
        q, o_blk, do_blk = q_ref[...], o_ref[...], do_ref[...]
        dqs = []
        for h, hm in enumerate(_head_masks()):
            qm = _bf(jnp.where(hm, q, 0.0))
            dom = jnp.where(hm, do_blk, 0.0)
            delta = jnp.sum(dom * o_blk, axis=1, keepdims=True)
            dom = _bf(dom)
            p = _ch_probs(qm, kw, bias_ref[h], mask)
            ds = p * (_dot_nt(dom, vw) - delta)
            dbias_ref[h] += ds
            dsz = ds * ATT_SCALE
            dqs.append(_dot(dsz, kw))
            dkpad[win, :] += _dot_tn(dsz, qm)
            dvpad[win, :] += _dot_tn(p, dom)
        dq_ref[...] = _bf(jnp.where(_head_masks()[0], dqs[0], dqs[1]))

        @pl.when(i == ni - 1)
        def _():
            dkout_ref[...] = _bf(dkpad[pl.ds(CH_LOOK, t), :])
            dvout_ref[...] = _bf(dvpad[pl.ds(CH_LOOK, t), :])

    blkspec = lambda off: pl.BlockSpec((CH_QB, PAIR), lambda p, i: (i, off + p))
    full = lambda off: pl.BlockSpec((t, PAIR), lambda p, i: (0, off + p))
    bias_spec = pl.BlockSpec((2, CH_QB, CH_WIN), lambda p, i: (p, 0, 0))
    return pl.pallas_call(
        body, name=name, grid=(N_PAIRS, ni),
        in_specs=[blkspec(3 * N_PAIRS), full(4 * N_PAIRS), full(5 * N_PAIRS), bias_spec,
                  blkspec(N_PAIRS), blkspec(N_PAIRS), ANY, ANY, ANY],
        out_specs=[blkspec(N_PAIRS), full(N_PAIRS), full(N_PAIRS), bias_spec],
        out_shape=[_sds((t, D_MODEL), BF16)] * 3 + [_sds((2 * N_PAIRS, CH_QB, CH_WIN))],
        scratch_shapes=[pltpu.VMEM((t + CH_LOOK, PAIR), BF16)] * 2
        + [pltpu.VMEM((t + CH_LOOK, PAIR), F32)] * 2,
        input_output_aliases={6: 0, 7: 1, 8: 2},
        compiler_params=_params("arbitrary", "arbitrary"),
    )(qkv, qkv, qkv, bias, o, do, dq_in, dk_in, dv_in)


def _bias_expand(fvec, *, name):
    n_heads = fvec.shape[0]

    def body(f_ref, o_ref, rows8):
        row = f_ref[0]
        for r in range(8):
            rows8[pl.ds(r, 1), :] = pltpu.roll(row, r, 1)
        base = rows8[...]
        for blk in range(CH_QB // 8):
            o_ref[0, pl.ds(8 * blk, 8), :] = pltpu.roll(base, 8 * blk, 1)

    return pl.pallas_call(
        body, name=name, grid=(n_heads,),
        in_specs=[pl.BlockSpec((1, 1, CH_WIN), lambda h: (h, 0, 0))],
        out_specs=pl.BlockSpec((1, CH_QB, CH_WIN), lambda h: (h, 0, 0)),
        out_shape=_sds((n_heads, CH_QB, CH_WIN)),
        scratch_shapes=[pltpu.VMEM((8, CH_WIN), F32)],
        compiler_params=_params("arbitrary"),
    )(fvec)


def _bias_grad(dbias, *, name):
    n_heads = dbias.shape[0]
    first = CH_LOOK - REL_CLIP

    def body(d_ref, o_ref, acc8):
        acc = jnp.zeros((8, CH_WIN), F32)
        for blk in range(CH_QB // 8):
            acc = acc + pltpu.roll(d_ref[0, pl.ds(8 * blk, 8), :], (CH_WIN - 8 * blk) % CH_WIN, 1)
        acc8[...] = acc
        dvec = jnp.zeros((1, CH_WIN), F32)
        for r in range(8):
            dvec = dvec + pltpu.roll(acc8[pl.ds(r, 1), :], (CH_WIN - r) % CH_WIN, 1)
        lane = lax.broadcasted_iota(jnp.int32, (1, CH_WIN), 1)
        clipped = (lane <= first) | (lane >= first + REL_CLIP + CHUNK)
        total = jnp.sum(jnp.where(clipped, dvec, 0.0), axis=1, keepdims=True)
        o_ref[0] = jnp.where(lane == first, total, dvec)

    return pl.pallas_call(
        body, name=name, grid=(n_heads,),
        in_specs=[pl.BlockSpec((1, CH_QB, CH_WIN), lambda h: (h, 0, 0))],
        out_specs=pl.BlockSpec((1, 1, CH_WIN), lambda h: (h, 0, 0)),
        out_shape=_sds((n_heads, 1, CH_WIN)),
        scratch_shapes=[pltpu.VMEM((8, CH_WIN), F32)],
        compiler_params=_params("arbitrary"),
    )(dbias)


def _out_fwd(o, h1, g_sb, g_ch, g_post, wout, *, name):
    t = o.shape[0]
    tm = 512
    half = D_MODEL // 2

    def body(o_ref, h_ref, gsb_ref, gch_ref, gpost_ref, w_ref, h2_ref, mixed_ref, y_ref):
        ov = o_ref[...]
        mixed = jnp.concatenate([_rms(ov[:, :half], gsb_ref[...]),
                                 _rms(ov[:, half:], gch_ref[...])], axis=1)
        mixed_ref[...] = _bf(mixed)
        y = _dot(mixed, w_ref[...])
        y_ref[...] = y
        h2_ref[...] = h_ref[...] + _rms(y, gpost_ref[...])

    row = pl.BlockSpec((tm, D_MODEL), lambda i: (i, 0))
    gain = lambda n: pl.BlockSpec((1, n), lambda i: (0, 0))
    return pl.pallas_call(
        body, name=name, grid=(t // tm,),
        in_specs=[row, row, gain(half), gain(half), gain(D_MODEL),
                  pl.BlockSpec((D_MODEL, D_MODEL), lambda i: (0, 0))],
        out_specs=[row, row, row],
        out_shape=[_sds((t, D_MODEL)), _sds((t, D_MODEL), BF16), _sds((t, D_MODEL))],
        compiler_params=_params("arbitrary"),
    )(o, h1, g_sb, g_ch, g_post, wout)


def _out_bwd(dy, mixed, o, g_sb, g_ch, wout, *, name):
    t = o.shape[0]
    tm = 512
    ni = t // tm
    half = D_MODEL // 2

    def body(dy_ref, mixed_ref, o_ref, gsb_ref, gch_ref, w_ref,
             dw_ref, do_ref, dgsb_ref, dgch_ref, acc_ref):
        i = pl.program_id(0)

        @pl.when(i == 0)
        def _():
            acc_ref[...] = jnp.zeros_like(acc_ref)
            dgsb_ref[...] = jnp.zeros_like(dgsb_ref)
            dgch_ref[...] = jnp.zeros_like(dgch_ref)

        dyv = dy_ref[...]
        acc_ref[...] += _dot_tn(mixed_ref[...], dyv)
        dm = _dot_nt(dyv, w_ref[...])
        ov = o_ref[...]
        doa, dga = _rms_bwd(dm[:, :half], ov[:, :half], gsb_ref[...])
        dob, dgb = _rms_bwd(dm[:, half:], ov[:, half:], gch_ref[...])
        do_ref[...] = jnp.concatenate([doa, dob], axis=1)
        dgsb_ref[...] += dga
        dgch_ref[...] += dgb

        @pl.when(i == ni - 1)
        def _():
            dw_ref[...] = _bf(acc_ref[...])

    row = pl.BlockSpec((tm, D_MODEL), lambda i: (i, 0))
    gain = pl.BlockSpec((1, half), lambda i: (0, 0))
    sq = pl.BlockSpec((D_MODEL, D_MODEL), lambda i: (0, 0))
    return pl.pallas_call(
        body, name=name, grid=(ni,),
        in_specs=[row, row, row, gain, gain, sq],
        out_specs=[sq, row, gain, gain],
        out_shape=[_sds((D_MODEL, D_MODEL), BF16), _sds((t, D_MODEL)),
                   _sds((1, half)), _sds((1, half))],
        scratch_shapes=[pltpu.VMEM((D_MODEL, D_MODEL), F32)],
        compiler_params=_params("arbitrary"),
    )(dy, mixed, o, g_sb, g_ch, wout)


def _ple(p, h3, target, wp, wgate, g, *, name):
    t = h3.shape[0]
    tm = 512
    ni = t // tm

    def body(p_ref, h_ref, tgt_ref, wp_ref, wg_ref, g_ref,
             loss_ref, dres_ref, dwp_ref, dwg_ref, dg_ref, accp, accg):
        i = pl.program_id(0)

        @pl.when(i == 0)
        def _():
            loss_ref[...] = jnp.zeros_like(loss_ref)
            dg_ref[...] = jnp.zeros_like(dg_ref)
            accp[...] = jnp.zeros_like(accp)
            accg[...] = jnp.zeros_like(accg)

        pv, hv, gv = p_ref[...], h_ref[...], g_ref[...]
        pe = _dot(pv, wp_ref[...])
        sig = _sigmoid(_dot(hv, wg_ref[...]))
        e = pe * sig
        err = hv + _rms(e, gv) - tgt_ref[...]
        tok = jnp.mean(err * err, axis=-1, keepdims=True)
        loss_ref[...] += 0.5 * jnp.sum(tok, axis=0, keepdims=True)
        dh4 = err * (1.0 / D_MODEL)
        de, dg = _rms_bwd(dh4, e, gv)
        dg_ref[...] += dg
        dpe = de * sig
        dgt = de * pe * sig * (1.0 - sig)
        accp[...] += _dot_tn(pv, dpe)
        accg[...] += _dot_tn(hv, dgt)
        dres_ref[...] = dh4 + _dot_nt(dgt, wg_ref[...])

        @pl.when(i == ni - 1)
        def _():
            dwp_ref[...] = _bf(accp[...])
            dwg_ref[...] = _bf(accg[...])

    row = pl.BlockSpec((tm, D_MODEL), lambda i: (i, 0))
    const = lambda r, c: pl.BlockSpec((r, c), lambda i: (0, 0))
    return pl.pallas_call(
        body, name=name, grid=(ni,),
        in_specs=[pl.BlockSpec((tm, PLE_DIM), lambda i: (i, 0)), row, row,
                  const(PLE_DIM, D_MODEL), const(D_MODEL, D_MODEL), const(1, D_MODEL)],
        out_specs=[const(1, 128), row, const(PLE_DIM, D_MODEL), const(D_MODEL, D_MODEL),
                   const(1, D_MODEL)],
        out_shape=[_sds((1, 128)), _sds((t, D_MODEL)), _sds((PLE_DIM, D_MODEL), BF16),
                   _sds((D_MODEL, D_MODEL), BF16), _sds((1, D_MODEL))],
        scratch_shapes=[pltpu.VMEM((PLE_DIM, D_MODEL), F32), pltpu.VMEM((D_MODEL, D_MODEL), F32)],
        compiler_params=_params("arbitrary"),
    )(p, h3, target, wp, wgate, g)


def _rel_bias_to_fvec(rel_bias):
    rev = rel_bias[:, ::-1]
    n_heads = rel_bias.shape[0]
    first = CH_LOOK - REL_CLIP
    n_var = REL_CLIP + CHUNK
    clipped = rev[:, :1]
    fvec = jnp.concatenate([jnp.broadcast_to(clipped, (n_heads, first)), rev[:, :n_var],
                            jnp.broadcast_to(clipped, (n_heads, CH_WIN - first - n_var))], axis=1)
    return fvec.reshape(n_heads, 1, CH_WIN)


def _fvec_grad_to_rel_bias(dfvec):
    first = CH_LOOK - REL_CLIP
    n_var = REL_CLIP + CHUNK
    rev = jnp.pad(dfvec[:, 0, first:first + n_var], ((0, 0), (0, N_REL - n_var)))
    return rev[:, ::-1]


def _local_step(x, p, target, g, weights_for, grads_done, fvec):
    w, tie = weights_for(0, x)
    w = dict(w)
    h1, n1, a1, b1, f1 = _ffn_fwd(x, g["ffn1_pre"] + tie, g["ffn1_post"],
                                  w["ffn1_gate"], w["ffn1_up"], w["ffn1_down"], name="ffn1_fwd")
    more, tie = weights_for(1, h1)
    w.update(more)
    qkv, u = _qkv_fwd(h1, g["mix_pre"] + tie, w["in"], name="qkv_fwd")
    bias = _bias_expand(fvec, name="bias_expand")
    o, ltot = _sb_fwd(qkv, name="sb_fwd")
    o = _ch_fwd(qkv, bias, o, name="ch_fwd")
    h2, mixed, y = _out_fwd(o, h1, g["out_sb"], g["out_ch"], g["mix_post"], w["out"], name="out_fwd")
    w.update(weights_for(2, h2)[0])
    h3, n2, a2, b2, f2 = _ffn_fwd(h2, g["ffn2_pre"], g["ffn2_post"],
                                  w["ffn2_gate"], w["ffn2_up"], w["ffn2_down"], name="ffn2_fwd")
    loss, dh3, dwp, dwgate, dg_ple = _ple(p, h3, target, w["ple_proj"], w["ple_gate"],
                                          g["ple_post"], name="ple")
    tie = grads_done(0, {"ple_proj": dwp, "ple_gate": dwgate})

    df2, dg_ffn2_post = _junction(dh3, post=(f2, g["ffn2_post"] + tie, 0.5), name="junction3")
    dwg2, dwu2, dwd2, dn2 = _ffn_bwd(n2, df2, a2, b2, w["ffn2_gate"], w["ffn2_up"],
                                     w["ffn2_down"], name="ffn2_bwd")
    tie = grads_done(1, {"ffn2_gate": dwg2, "ffn2_up": dwu2, "ffn2_down": dwd2})
    dh2, dg_ffn2_pre, dy, dg_mix_post = _junction(
        dh3, pre=(dn2, h2, g["ffn2_pre"] + tie), post=(y, g["mix_post"], 1.0), name="junction2")
    dwout, do, dg_sb, dg_ch = _out_bwd(dy, mixed, o, g["out_sb"], g["out_ch"], w["out"],
                                       name="out_bwd")
    dq, dk, dv = _sb_bwd(qkv, ltot, do, name="sb_bwd")
    dq, dk, dv, dbias = _ch_bwd(qkv, bias, o, do, dq, dk, dv, name="ch_bwd")
    dfvec = _bias_grad(dbias, name="bias_grad")
    dwin, du = _qkv_bwd(dq, dk, dv, u, w["in"], name="qkv_bwd")
    tie = grads_done(2, {"out": dwout, "in": dwin})
    dh1, dg_mix_pre, df1, dg_ffn1_post = _junction(
        dh2, pre=(du, h1, g["mix_pre"] + tie), post=(f1, g["ffn1_post"], 0.5), name="junction1")
    dwg1, dwu1, dwd1, dn1 = _ffn_bwd(n1, df1, a1, b1, w["ffn1_gate"], w["ffn1_up"],
                                     w["ffn1_down"], name="ffn1_bwd")
    tie = grads_done(3, {"ffn1_gate": dwg1, "ffn1_up": dwu1, "ffn1_down": dwd1})
    dx, dg_ffn1_pre = _junction(dh1, pre=(dn1, x, g["ffn1_pre"] + tie), name="junction0")

    dg = {"ffn1_pre": dg_ffn1_pre, "ffn1_post": dg_ffn1_post, "mix_pre": dg_mix_pre,
          "mix_post": dg_mix_post, "out_sb": dg_sb, "out_ch": dg_ch,
          "ffn2_pre": dg_ffn2_pre, "ffn2_post": dg_ffn2_post, "ple_post": dg_ple}
    return loss, dx, dg, dfvec


_WEIGHTS = (
    ("ffn1_gate", "row", FF_SHARD, FF_SHARD_PAD, D_MODEL),
    ("ffn1_up", "row", FF_SHARD, FF_SHARD_PAD, D_MODEL),
    ("ffn1_down", "row", FF_SHARD, FF_SHARD_PAD, D_MODEL),
    ("in", "col", QKV_SHARD, QKV_SHARD, D_MODEL),
    ("out", "row", ROW_SHARD, ROW_SHARD, D_MODEL),
    ("ffn2_gate", "row", FF_SHARD, FF_SHARD_PAD, D_MODEL),
    ("ffn2_up", "row", FF_SHARD, FF_SHARD_PAD, D_MODEL),
    ("ffn2_down", "row", FF_SHARD, FF_SHARD_PAD, D_MODEL),
    ("ple_proj", "col", ROW_SHARD, ROW_SHARD, PLE_DIM),
    ("ple_gate", "row", ROW_SHARD, ROW_SHARD, D_MODEL),
)
_TRANSPOSED = ("ffn1_gate", "ffn1_up", "ffn2_gate", "ffn2_up")
_SPEC = {n: (kind, valid, pad, other) for n, kind, valid, pad, other in _WEIGHTS}
_GATHER_STAGES = (("ffn1_gate", "ffn1_up", "ffn1_down"), ("in", "out"),
                  ("ffn2_gate", "ffn2_up", "ffn2_down", "ple_proj", "ple_gate"))
_SCATTER_STAGES = (("ple_proj", "ple_gate"), ("ffn2_gate", "ffn2_up", "ffn2_down"),
                   ("out", "in"), ("ffn1_gate", "ffn1_up", "ffn1_down"))
HBM = pl.BlockSpec(memory_space=pltpu.HBM)
SEM = pl.BlockSpec(memory_space=pltpu.SEMAPHORE)
EFFECT = pltpu.SideEffectType.DATAFLOW_SIDE_EFFECTING


def _shard_shape(kind, size, other):
    return (other, size) if kind == "col" else (size, other)


def _window(ref, kind, start, size):
    return ref.at[:, pl.ds(start, size)] if kind == "col" else ref.at[pl.ds(start, size), :]


def _device_tuple(k):
    return (k // 4, (k // 2) % 2, k % 2)


def _my_index():
    return 4 * lax.axis_index("x") + 2 * lax.axis_index("y") + lax.axis_index("c")


def _pack_weights(shards):
    nw = len(_WEIGHTS)

    def body(*refs):
        ins, packed, full = refs[:nw], refs[nw:2 * nw], refs[2 * nw:3 * nw]
        sem = refs[3 * nw]
        me = _my_index()
        for (_, kind, valid, pad, _), src, dst in zip(_WEIGHTS, ins, packed):
            if pad != valid:
                dst[...] = jnp.zeros_like(dst)
            if kind == "col":
                dst[:, pl.ds(0, valid)] = _bf(src[...])
            else:
                dst[pl.ds(0, valid), :] = _bf(src[...])
        for k in range(N_DEV):
            @pl.when(me == k)
            def _():
                for w, (_, kind, _, pad, _) in enumerate(_WEIGHTS):
                    pltpu.make_async_copy(packed[w], _window(full[w], kind, k * pad, pad),
                                          sem.at[w]).start()
        for w, (_, kind, _, pad, _) in enumerate(_WEIGHTS):
            pltpu.make_async_copy(packed[w], _window(full[w], kind, 0, pad), sem.at[w]).wait()

    outs = pl.pallas_call(
        body, name="pack_weights",
        in_specs=[VMEM] * nw, out_specs=[VMEM] * nw + [ANY] * nw,
        out_shape=[_sds(_shard_shape(kind, pad, other), BF16) for _, kind, _, pad, other in _WEIGHTS]
        + [_sds(_shard_shape(kind, N_DEV * pad, other), BF16) for _, kind, _, pad, other in _WEIGHTS],
        scratch_shapes=[pltpu.SemaphoreType.DMA((nw,))],
        compiler_params=pltpu.CompilerParams(vmem_limit_bytes=VMEM_LIMIT_BYTES),
    )(*shards)
    names = [n for n, *_ in _WEIGHTS]
    return dict(zip(names, outs[:nw])), dict(zip(names, outs[nw:]))


def _hbm(a):
    return pltpu.with_memory_space_constraint(a, pltpu.HBM)


def _split_start(name, n, body_copies, sources, lands, after):
    arrays = list(sources) + list(lands)
    ns, na = len(sources), len(arrays)

    def body(*refs):
        src, land = refs[:ns], refs[ns:na]
        send, recv = refs[na + 1], refs[na + 2]
        token = refs[-1]
        body_copies(src, land, send, recv)
        token[...] = jnp.zeros_like(token)

    out = pl.pallas_call(
        body, name=name,
        out_shape=(pltpu.SemaphoreType.DMA((n,)), pltpu.SemaphoreType.DMA((n,)),
                   *[pltpu.HBM(a.shape, a.dtype) for a in arrays], _sds((8, 128))),
        in_specs=[HBM] * na + [ANY], out_specs=(SEM, SEM, *[HBM] * na, VMEM),
        input_output_aliases={i: 2 + i for i in range(na)},
        compiler_params=pltpu.CompilerParams(has_side_effects=EFFECT),
    )(*[_hbm(a) for a in arrays], after)
    return out[0], out[1], out[2:2 + ns], out[2 + ns:2 + na], out[-1]


def _split_wait(name, n, seven_of, send, recv, sources, lands, after, keep_sources=False):
    arrays = list(sources) + list(lands)
    ns, na = len(sources), len(arrays)

    def body(*refs):
        land = refs[ns:na]
        send_ref, recv_ref = refs[na], refs[na + 1]
        myself = (lax.axis_index("x"), lax.axis_index("y"), lax.axis_index("c"))
        for w in range(n):
            seven = seven_of(w, land[w])
            copy = pltpu.make_async_remote_copy(
                src_ref=seven, dst_ref=seven, send_sem=send_ref.at[w], recv_sem=recv_ref.at[w],
                device_id=myself, device_id_type=MESH)
            copy.wait_send()
            copy.wait_recv()

    out = pl.pallas_call(
        body, name=name,
        out_shape=[pltpu.HBM(a.shape, a.dtype) for a in arrays],
        in_specs=[HBM] * na + [SEM, SEM, ANY], out_specs=[HBM] * na,
        input_output_aliases={i: i for i in range(na)},
        compiler_params=pltpu.CompilerParams(has_side_effects=EFFECT),
    )(*arrays, send, recv, after)
    return out if keep_sources else out[ns:]


_ALL_PEERS = (1, 2, 3, 4, 5, 6, 7)
_NEAR_PEERS = (1, 2, 4, 6)
_FAR_CHIPS = (2, 4, 6)


def _gather_start(stage, names, packed, full, after, peers=_ALL_PEERS):
    def copies(src, land, send, recv):
        me = _my_index()
        for k in range(N_DEV):
            @pl.when(me == k)
            def _():
                for w, name in enumerate(names):
                    kind, _, pad, _ = _SPEC[name]
                    dst = _window(land[w], kind, k * pad, pad)
                    for mask in peers:
                        pltpu.make_async_remote_copy(
                            src_ref=src[w], dst_ref=dst, send_sem=send.at[w],
                            recv_sem=recv.at[w], device_id=_device_tuple(k ^ mask),
                            device_id_type=MESH).start()

    return _split_start(f"gather_start{stage}", len(names), copies,
                        [packed[n] for n in names], [full[n] for n in names], after)


def _gather_wait(stage, names, started, after, count=N_DEV - 1):
    send, recv, src, land, _ = started

    def bytes_of(w, ref):
        kind, _, pad, _ = _SPEC[names[w]]
        return _window(ref, kind, 0, count * pad)

    return dict(zip(names, _split_wait(f"gather_wait{stage}", len(names), bytes_of,
                                       send, recv, src, land, after)))


def _relay_start(stage, names, full, after):
    def copies(_, land, send, recv):
        me = _my_index()
        for k in range(N_DEV):
            @pl.when(me == k)
            def _():
                for w, name in enumerate(names):
                    kind, _, pad, _ = _SPEC[name]
                    for mask in _FAR_CHIPS:
                        win = _window(land[w], kind, (k ^ mask) * pad, pad)
                        pltpu.make_async_remote_copy(
                            src_ref=win, dst_ref=win, send_sem=send.at[w], recv_sem=recv.at[w],
                            device_id=_device_tuple(k ^ 1), device_id_type=MESH).start()

    return _split_start(f"relay_start{stage}", len(names), copies, [],
                        [full[n] for n in names], after)


def _scatter_start(stage, names, grads, after):
    def copies(src, land, send, recv):
        me = _my_index()
        for k in range(N_DEV):
            @pl.when(me != k)
            def _():
                slot = lax.rem(me + (N_DEV - 1 - k), N_DEV)
                for w, name in enumerate(names):
                    kind, _, pad, _ = _SPEC[name]
                    pltpu.make_async_remote_copy(
                        src_ref=_window(src[w], kind, k * pad, pad), dst_ref=land[w].at[slot],
                        send_sem=send.at[w], recv_sem=recv.at[w],
                        device_id=_device_tuple(k), device_id_type=MESH).start()

    lands = [lax.empty((N_DEV - 1,) + _shard_shape(_SPEC[m][0], _SPEC[m][2], _SPEC[m][3]), BF16)
             for m in names]
    return _split_start(f"scatter_start{stage}", len(names), copies, grads, lands, after)


def _scatter_wait(stage, names, started, after):
    send, recv, src, land, _ = started
    n = len(names)
    out = _split_wait(f"scatter_wait{stage}", n, lambda w, ref: ref, send, recv, src, land, after,
                      keep_sources=True)
    return dict(zip(names, out[:n])), dict(zip(names, out[n:]))


N_CHIPS = N_DEV // 2


def _pair_start(stage, names, grads, after):
    def copies(src, land, send, recv):
        me = _my_index()
        for k in range(N_DEV):
            @pl.when(me == k)
            def _():
                for w, name in enumerate(names):
                    kind, _, pad, _ = _SPEC[name]
                    for chip in range(N_CHIPS):
                        j = 2 * chip + ((k ^ 1) & 1)
                        pltpu.make_async_remote_copy(
                            src_ref=_window(src[w], kind, j * pad, pad), dst_ref=land[w].at[chip],
                            send_sem=send.at[w], recv_sem=recv.at[w],
                            device_id=_device_tuple(k ^ 1), device_id_type=MESH).start()

    lands = [lax.empty((N_CHIPS,) + _shard_shape(_SPEC[m][0], _SPEC[m][2], _SPEC[m][3]), BF16)
             for m in names]
    return _split_start(f"pair_start{stage}", len(names), copies, grads, lands, after)


def _pair_sum(dw_full, pair, *, pad, name):
    other = dw_full.shape[1]

    def body(own_ref, pair_ref, out_ref):
        out_ref[0] = _bf(own_ref[...].astype(F32) + pair_ref[0].astype(F32))

    slot = pl.BlockSpec((1, pad, other), lambda q: (q, 0, 0))
    return pl.pallas_call(
        body, name=name, grid=(N_CHIPS,),
        in_specs=[pl.BlockSpec((pad, other), lambda q: (2 * q + lax.axis_index("c"), 0)), slot],
        out_specs=slot, out_shape=_sds((N_CHIPS, pad, other), BF16),
        compiler_params=_params("arbitrary"),
    )(dw_full, pair)


def _chip_start(stage, names, sums, after):
    def copies(src, land, send, recv):
        me = _my_index()
        my_chip = lax.shift_right_logical(me, 1)
        for k in range(N_DEV):
            @pl.when((me != k) & (((me ^ k) & 1) == 0))
            def _():
                slot = lax.rem(my_chip + (N_CHIPS - 1 - k // 2), N_CHIPS)
                for w in range(len(names)):
                    pltpu.make_async_remote_copy(
                        src_ref=src[w].at[k // 2], dst_ref=land[w].at[slot],
                        send_sem=send.at[w], recv_sem=recv.at[w],
                        device_id=_device_tuple(k), device_id_type=MESH).start()

    lands = [lax.empty((N_CHIPS - 1,) + a.shape[1:], BF16) for a in sums]
    return _split_start(f"chip_start{stage}", len(names), copies, sums, lands, after)


def _adamw_chip(w, m, v, land, sums, *, name):
    shape = w.shape

    def body(w_ref, m_ref, v_ref, land_ref, own_ref, *outs):
        rows = pl.ds(0, shape[0])
        grad = own_ref[0, rows, :].astype(F32)
        for s in range(N_CHIPS - 1):
            grad = grad + land_ref[s, rows, :].astype(F32)
        _adam_update(w_ref, m_ref, v_ref, grad, *outs)

    whole = lambda a: pl.BlockSpec(a.shape, lambda i: (0,) * a.ndim)
    own = pl.BlockSpec((1,) + sums.shape[1:],
                       lambda i: (2 * lax.axis_index("x") + lax.axis_index("y"), 0, 0))
    return pl.pallas_call(
        body, name=name, grid=(1,),
        in_specs=[whole(w), whole(m), whole(v), whole(land), own],
        out_specs=[whole(w)] * 4, out_shape=[_sds(shape)] * 4,
        compiler_params=_params("arbitrary"),
    )(w, m, v, land, sums)


def _allreduce_small(small, after):
    shape = small.shape

    def body(in_ref, _after, out_ref, gath, send, recv):
        me = _my_index()
        for k in range(N_DEV):
            @pl.when(me != k)
            def _():
                pltpu.make_async_remote_copy(
                    src_ref=in_ref, dst_ref=gath.at[me], send_sem=send, recv_sem=recv,
                    device_id=_device_tuple(k), device_id_type=MESH).start()

            @pl.when(me == k)
            def _():
                gath[k] = in_ref[...]
        seven = gath.at[pl.ds(0, N_DEV - 1)]
        pltpu.make_async_remote_copy(
            src_ref=seven, dst_ref=seven, send_sem=send, recv_sem=recv,
            device_id=_device_tuple(0), device_id_type=MESH).wait()
        total = gath[0]
        for s in range(1, N_DEV):
            total = total + gath[s]
        out_ref[...] = total

    return pl.pallas_call(
        body, name="allreduce_small",
        in_specs=[VMEM, ANY], out_specs=VMEM, out_shape=_sds(shape),
        scratch_shapes=[pltpu.VMEM((N_DEV,) + shape, F32),
                        pltpu.SemaphoreType.DMA, pltpu.SemaphoreType.DMA],
    )(small, after)


def _adam_update(w_ref, m_ref, v_ref, grad, grad_ref, delta_ref, nm_ref, nv_ref):
    new_m = ADAM_B1 * m_ref[...] + (1.0 - ADAM_B1) * grad
    new_v = ADAM_B2 * v_ref[...] + (1.0 - ADAM_B2) * (grad * grad)
    m_hat = new_m / (1.0 - ADAM_B1 ** ADAM_STEP)
    v_hat = new_v / (1.0 - ADAM_B2 ** ADAM_STEP)
    grad_ref[...] = grad
    delta_ref[...] = -ADAM_LR * (m_hat / (jnp.sqrt(v_hat) + ADAM_EPS) + ADAM_WD * w_ref[...])
    nm_ref[...] = new_m
    nv_ref[...] = new_v


def _adamw(w, m, v, g, *, name):
    def body(w_ref, m_ref, v_ref, g_ref, *outs):
        _adam_update(w_ref, m_ref, v_ref, g_ref[...], *outs)

    return pl.pallas_call(
        body, name=name, in_specs=[VMEM] * 4, out_specs=[VMEM] * 4,
        out_shape=[_sds(w.shape)] * 4,
    )(w, m, v, g)


def _adamw_shard(w, m, v, land, dw_full, *, kind, pad, name):
    shape = w.shape
    other = shape[0] if kind == "col" else shape[1]

    def body(w_ref, m_ref, v_ref, land_ref, own_ref, *outs):
        valid = ((slice(None), pl.ds(0, shape[1])) if kind == "col"
                 else (pl.ds(0, shape[0]), slice(None)))
        grad = own_ref[valid].astype(F32)
        for s in range(N_DEV - 1):
            grad = grad + land_ref[(s,) + valid].astype(F32)
        _adam_update(w_ref, m_ref, v_ref, grad, *outs)

    whole = lambda a: pl.BlockSpec(a.shape, lambda i: (0,) * a.ndim)
    own = pl.BlockSpec(_shard_shape(kind, pad, other),
                       (lambda i: (0, _my_index())) if kind == "col" else (lambda i: (_my_index(), 0)))
    return pl.pallas_call(
        body, name=name, grid=(1,),
        in_specs=[whole(w), whole(m), whole(v), whole(land), own],
        out_specs=[whole(w)] * 4, out_shape=[_sds(shape)] * 4,
        compiler_params=_params("arbitrary"),
    )(w, m, v, land, dw_full)


_GAINS = ("ffn1_pre", "ffn1_post", "mix_pre", "mix_post", "ffn2_pre", "ffn2_post", "ple_post")
_SMALL_ROWS = 16


def _stack_gains(get):
    return jnp.concatenate([get(n) for n in _GAINS]
                           + [jnp.concatenate([get("out_sb"), get("out_ch")], axis=1)], axis=0)


def kernel(x, p, g_ffn1_pre, g_ffn1_post, w_ffn1_gate, w_ffn1_up, w_ffn1_down, g_mix_pre, g_mix_post, w_in, g_out_sb, g_out_ch, rel_bias, w_out, g_ffn2_pre, g_ffn2_post, w_ffn2_gate, w_ffn2_up, w_ffn2_down, w_ple_proj, w_ple_gate, g_ple_post, loss_target, m_g_ffn1_pre, m_g_ffn1_post, m_w_ffn1_gate, m_w_ffn1_up, m_w_ffn1_down, m_g_mix_pre, m_g_mix_post, m_w_in, m_g_out_sb, m_g_out_ch, m_rel_bias, m_w_out, m_g_ffn2_pre, m_g_ffn2_post, m_w_ffn2_gate, m_w_ffn2_up, m_w_ffn2_down, m_w_ple_proj, m_w_ple_gate, m_g_ple_post, v_g_ffn1_pre, v_g_ffn1_post, v_w_ffn1_gate, v_w_ffn1_up, v_w_ffn1_down, v_g_mix_pre, v_g_mix_post, v_w_in, v_g_out_sb, v_g_out_ch, v_rel_bias, v_w_out, v_g_ffn2_pre, v_g_ffn2_post, v_w_ffn2_gate, v_w_ffn2_up, v_w_ffn2_down, v_w_ple_proj, v_w_ple_gate, v_g_ple_post):
    given = dict(locals())
    wnames = [n for n, *_ in _WEIGHTS]

    def shard(prefix, n):
        a = given[prefix + "w_" + n][0]
        return a.T if n in _TRANSPOSED else a

    packed, full = _pack_weights([shard("", n) for n in wnames])
    first = _GATHER_STAGES[0]
    anchor = x[0]
    gathers = {0: _gather_start(0, first, packed, full, anchor, peers=_NEAR_PEERS)}

    def weights_for(stage, after):
        names = _GATHER_STAGES[stage]
        if stage == 0:
            near = _gather_wait(0, names, gathers[0], after, count=len(_NEAR_PEERS))
            relay = _relay_start(0, names, near, near[names[0]])
            ws = _gather_wait("0r", names, relay, relay[-1], count=len(_FAR_CHIPS))
        else:
            ws = _gather_wait(stage, names, gathers[stage], after)
        if stage + 1 == len(_GATHER_STAGES):
            return ws, jnp.zeros((1, 1), F32)
        gathers[stage + 1] = _gather_start(stage + 1, _GATHER_STAGES[stage + 1], packed, full,
                                           ws[names[0]])
        return ws, gathers[stage + 1][-1][:1, :1]

    scatters = {}

    last = len(_SCATTER_STAGES) - 1

    def grads_done(stage, grads):
        names = _SCATTER_STAGES[stage]
        start = _pair_start if stage == last else _scatter_start
        scatters[stage] = start(stage, names, [grads[n] for n in names], anchor)
        return scatters[stage][-1][:1, :1]

    gains = {n: given["g_" + n] for n in _GAINS + ("out_sb", "out_ch")}
    fvec = _rel_bias_to_fvec(rel_bias[0])
    loss, dx, dg, dfvec = _local_step(x[0], p[0, 0], loss_target[0], gains,
                                      weights_for, grads_done, fvec)

    results = {}

    def finish(stage, after):
        names = _SCATTER_STAGES[stage]
        dws, lands = _scatter_wait(stage, names, scatters[stage], after)
        for n in names:
            kind, _, pad, _ = _SPEC[n]
            out = _adamw_shard(shard("", n), shard("m_", n), shard("v_", n), lands[n], dws[n],
                               kind=kind, pad=pad, name="adamw_" + n)
            results["w_" + n] = [a.T for a in out] if n in _TRANSPOSED else out
        return results["w_" + names[-1]][0]

    names = _SCATTER_STAGES[last]
    whole = lambda w, ref: ref
    send, recv, src, land, _ = scatters[last]
    out = _split_wait(f"pair_wait{last}", len(names), whole, send, recv, src, land, dx,
                      keep_sources=True)
    sums = [_pair_sum(dwf, pair, pad=_SPEC[n][2], name="pair_sum_" + n)
            for n, dwf, pair in zip(names, out[:len(names)], out[len(names):])]
    send, recv, src, land, after = _chip_start(last, names, sums, anchor)
    for stage in range(last):
        after = finish(stage, after)
    out = _split_wait(f"chip_wait{last}", len(names), whole, send, recv, src, land, after,
                      keep_sources=True)
    for n, own, landed in zip(names, out[:len(names)], out[len(names):]):
        res = _adamw_chip(shard("", n), shard("m_", n), shard("v_", n), landed, own,
                          name="adamw_" + n)
        results["w_" + n] = [a.T for a in res] if n in _TRANSPOSED else res
        after = res[0]
    dfv = jnp.pad(dfvec[:, 0, :], ((0, 0), (0, D_MODEL - CH_WIN)))
    small = _allreduce_small(jnp.concatenate([_stack_gains(lambda n: dg[n]), dfv], axis=0), after)
    stacked = _adamw(_stack_gains(lambda n: given["g_" + n]),
                     _stack_gains(lambda n: given["m_g_" + n]),
                     _stack_gains(lambda n: given["v_g_" + n]),
                     small[:N_DEV], name="adamw_gains")
    half = D_MODEL // 2
    for r, n in enumerate(_GAINS):
        results["g_" + n] = [a[r:r + 1] for a in stacked]
    results["g_out_sb"] = [a[N_DEV - 1:N_DEV, :half] for a in stacked]
    results["g_out_ch"] = [a[N_DEV - 1:N_DEV, half:] for a in stacked]
    d_rel = _fvec_grad_to_rel_bias(small[N_DEV:, :CH_WIN].reshape(N_DEV, 1, CH_WIN))
    results["rel_bias"] = _adamw(rel_bias[0], m_rel_bias[0], v_rel_bias[0], d_rel,
                                 name="adamw_rel_bias")

    order = ("g_ffn1_pre", "g_ffn1_post", "w_ffn1_gate", "w_ffn1_up", "w_ffn1_down",
             "g_mix_pre", "g_mix_post", "w_in", "g_out_sb", "g_out_ch", "rel_bias", "w_out",
             "g_ffn2_pre", "g_ffn2_post", "w_ffn2_gate", "w_ffn2_up", "w_ffn2_down",
             "w_ple_proj", "w_ple_gate", "g_ple_post")

    def leaf(name, idx):
        a = results[name][idx]
        return a if name.startswith("g_") else a[None]

    total_loss = lax.psum(loss[0, 0], ("x", "y", "c"))
    return (total_loss, dx[None],
            *[leaf(n, 0) for n in order], *[leaf(n, 1) for n in order],
            *[leaf(n, 2) for n in order], *[leaf(n, 3) for n in order])
```

```python
import jax
import jax.numpy as jnp
from jax import lax
from jax.experimental import pallas as pl
from jax.experimental.pallas import tpu as pltpu

F32 = jnp.float32
BF16 = jnp.bfloat16

N_DEV = 8
D_MODEL = 1024
D_FF = 2816
FF_SHARD = D_FF // N_DEV
FF_SHARD_PAD = 384
D_FF_PAD = FF_SHARD_PAD * N_DEV
QKV_WIDTH = 3 * D_MODEL
QKV_SHARD = QKV_WIDTH // N_DEV
PLE_DIM = 256
ROW_SHARD = D_MODEL // N_DEV
HEAD_DIM = 64
PAIR = 2 * HEAD_DIM
N_PAIRS = 4
CHUNK = 64
LOOKBACK = 8
REL_CLIP = 128
N_REL = 2 * REL_CLIP + 1
CH_QB = 256
CH_LOOK = LOOKBACK * CHUNK
CH_WIN = CH_LOOK + CH_QB
SB_BLK = 256
SB_GROUP = 2
SB_LANES = tuple(slice(g * 128, (g + 1) * 128) for g in range(SB_GROUP))
EPS = 1e-6
NEG_INF = -1e30
ATT_SCALE = HEAD_DIM ** -0.5
ADAM_LR = 0.001
ADAM_B1 = 0.9
ADAM_B2 = 0.999
ADAM_EPS = 1e-08
ADAM_WD = 0.01
ADAM_STEP = 10
VMEM_LIMIT_BYTES = 48 * 1024 * 1024
MESH = pl.DeviceIdType.MESH

ANY = pl.BlockSpec(memory_space=pl.ANY)
VMEM = pl.BlockSpec(memory_space=pltpu.VMEM)


def _params(*sem):
    return pltpu.CompilerParams(dimension_semantics=sem or None,
                                vmem_limit_bytes=VMEM_LIMIT_BYTES)


def _sds(shape, dtype=F32):
    return jax.ShapeDtypeStruct(shape, dtype)


def _bf(x):
    return x.astype(BF16)


def _dot(a, b):
    return jnp.dot(_bf(a), _bf(b), preferred_element_type=F32)


def _dot_nt(a, b):
    return lax.dot_general(_bf(a), _bf(b), (((1,), (1,)), ((), ())),
                           preferred_element_type=F32)


def _dot_tn(a, b):
    return lax.dot_general(_bf(a), _bf(b), (((0,), (0,)), ((), ())),
                           preferred_element_type=F32)


def _sigmoid(x):
    return 1.0 / (1.0 + jnp.exp(-x))


def _softplus(x):
    return jnp.maximum(x, 0.0) + jnp.log(1.0 + jnp.exp(-jnp.abs(x)))


def _rstd(x):
    return lax.rsqrt(jnp.mean(x * x, axis=-1, keepdims=True) + EPS)


def _rms(x, g):
    return x * _rstd(x) * g


def _rms_bwd(dy, x, g):
    r = _rstd(x)
    w = dy * g
    dx = r * (w - x * (r * r) * jnp.mean(w * x, axis=-1, keepdims=True))
    dg = jnp.sum(dy * (x * r), axis=0, keepdims=True)
    return dx, dg


def _head_masks():
    lane = lax.broadcasted_iota(jnp.int32, (1, PAIR), 1)
    return lane < HEAD_DIM, lane >= HEAD_DIM


def _ffn_fwd(x, g_pre, g_post, wg, wu, wd, *, name):
    t = x.shape[0]
    tm, tj = 512, 1024
    ni, nj = t // tm, D_FF_PAD // tj

    def body(x_ref, gpre_ref, gpost_ref, wg_ref, wu_ref, wd_ref,
             h_ref, n_ref, a_ref, b_ref, f_ref, acc_ref):
        j = pl.program_id(1)

        @pl.when(j == 0)
        def _():
            n_ref[...] = _bf(_rms(x_ref[...], gpre_ref[...]))
            acc_ref[...] = jnp.zeros_like(acc_ref)

        n = n_ref[...]
        a = _dot_nt(n, wg_ref[...])
        b = _dot_nt(n, wu_ref[...])
        a_ref[...] = a
        b_ref[...] = b
        hmid = a * _sigmoid(a) * b
        acc_ref[...] += jnp.dot(_bf(hmid), wd_ref[...], preferred_element_type=F32)

        @pl.when(j == nj - 1)
        def _():
            f = acc_ref[...]
            f_ref[...] = f
            h_ref[...] = x_ref[...] + 0.5 * _rms(f, gpost_ref[...])

    row = pl.BlockSpec((tm, D_MODEL), lambda i, j: (i, 0))
    gain = pl.BlockSpec((1, D_MODEL), lambda i, j: (0, 0))
    col = pl.BlockSpec((tm, tj), lambda i, j: (i, j))
    wtile = pl.BlockSpec((tj, D_MODEL), lambda i, j: (j, 0))
    return pl.pallas_call(
        body, name=name, grid=(ni, nj),
        in_specs=[row, gain, gain, wtile, wtile, wtile],
        out_specs=[row, row, col, col, row],
        out_shape=[_sds((t, D_MODEL)), _sds((t, D_MODEL), BF16),
                   _sds((t, D_FF_PAD)), _sds((t, D_FF_PAD)), _sds((t, D_MODEL))],
        scratch_shapes=[pltpu.VMEM((tm, D_MODEL), F32)],
        compiler_params=_params("arbitrary", "arbitrary"),
    )(x, g_pre, g_post, wg, wu, wd)


def _ffn_bwd(n, df, a, b, wg, wu, wd, *, name):
    t = n.shape[0]
    tj, ts = 256, 512
    nj, ns = D_FF_PAD // tj, t // ts

    def body(n_hbm, df_hbm, a_ref, b_ref, wg_ref, wu_ref, wd_ref,
             dwg_ref, dwu_ref, dwd_ref, dn_hbm,
             n_v, df_v, dn_v, ag, au, ad, sem):
        j = pl.program_id(0)

        @pl.when(j == 0)
        def _():
            c1 = pltpu.make_async_copy(n_hbm, n_v, sem.at[0])
            c2 = pltpu.make_async_copy(df_hbm, df_v, sem.at[1])
            c1.start()
            c2.start()
            dn_v[...] = jnp.zeros_like(dn_v)
            c1.wait()
            c2.wait()

        ag[...] = jnp.zeros_like(ag)
        au[...] = jnp.zeros_like(au)
        ad[...] = jnp.zeros_like(ad)
        wgj, wuj, wdj = wg_ref[...], wu_ref[...], wd_ref[...]
        for s in range(ns):
            rows = pl.ds(s * ts, ts)
            av, bv = a_ref[rows, :], b_ref[rows, :]
            sig = _sigmoid(av)
            silu = av * sig
            dfr = df_v[rows, :]
            nr = n_v[rows, :]
            dhmid = _dot_nt(dfr, wdj)
            da = dhmid * bv * (sig * (1.0 + av * (1.0 - sig)))
            db = dhmid * silu
            ad[...] += _dot_tn(silu * bv, dfr)
            ag[...] += _dot_tn(da, nr)
            au[...] += _dot_tn(db, nr)
            dn_v[rows, :] += _dot(da, wgj) + _dot(db, wuj)
        dwg_ref[...] = _bf(ag[...])
        dwu_ref[...] = _bf(au[...])
        dwd_ref[...] = _bf(ad[...])

        @pl.when(j == nj - 1)
        def _():
            c = pltpu.make_async_copy(dn_v, dn_hbm, sem.at[0])
            c.start()
            c.wait()

    roww = pl.BlockSpec((tj, D_MODEL), lambda j: (j, 0))
    act = pl.BlockSpec((t, tj), lambda j: (0, j))
    return pl.pallas_call(
        body, name=name, grid=(nj,),
        in_specs=[ANY, ANY, act, act, roww, roww, roww],
        out_specs=[roww, roww, roww, ANY],
        out_shape=[_sds((D_FF_PAD, D_MODEL), BF16)] * 3 + [_sds((t, D_MODEL))],
        scratch_shapes=[pltpu.VMEM((t, D_MODEL), BF16), pltpu.VMEM((t, D_MODEL), BF16),
                        pltpu.VMEM((t, D_MODEL), F32)]
        + [pltpu.VMEM((tj, D_MODEL), F32)] * 3 + [pltpu.SemaphoreType.DMA((2,))],
        compiler_params=_params("arbitrary"),
    )(n, df, a, b, wg, wu, wd)


def _junction(dres, pre=None, post=None, *, name):
    t = dres.shape[0]
    tm = 512
    ni = t // tm
    n_in = 1 + (3 if pre else 0) + (2 if post else 0)
    coef = post[2] if post else None

    def body(*refs):
        ins, outs = list(refs[:n_in]), list(refs[n_in:])
        i = pl.program_id(0)
        dh = ins.pop(0)[...]
        if pre:
            dn_ref, x_ref, gpre_ref = ins.pop(0), ins.pop(0), ins.pop(0)
            dh_ref, dgpre_ref = outs.pop(0), outs.pop(0)
            dx, dg = _rms_bwd(dn_ref[...], x_ref[...], gpre_ref[...])
            dh = dh + dx
            dh_ref[...] = dh

            @pl.when(i == 0)
            def _():
                dgpre_ref[...] = jnp.zeros_like(dgpre_ref)
            dgpre_ref[...] += dg
        if post:
            f_ref, gpost_ref = ins.pop(0), ins.pop(0)
            df_ref, dgpost_ref = outs.pop(0), outs.pop(0)
            df, dg = _rms_bwd(coef * dh, f_ref[...], gpost_ref[...])
            df_ref[...] = _bf(df)

            @pl.when(i == 0)
            def _():
                dgpost_ref[...] = jnp.zeros_like(dgpost_ref)
            dgpost_ref[...] += dg

    row = pl.BlockSpec((tm, D_MODEL), lambda i: (i, 0))
    gain = pl.BlockSpec((1, D_MODEL), lambda i: (0, 0))
    args, in_specs, out_specs, out_shape = [dres], [row], [], []
    if pre:
        args += list(pre)
        in_specs += [row, row, gain]
        out_specs += [row, gain]
        out_shape += [_sds((t, D_MODEL)), _sds((1, D_MODEL))]
    if post:
        args += [post[0], post[1]]
        in_specs += [row, gain]
        out_specs += [row, gain]
        out_shape += [_sds((t, D_MODEL), BF16), _sds((1, D_MODEL))]
    return pl.pallas_call(
        body, name=name, grid=(ni,), in_specs=in_specs, out_specs=out_specs,
        out_shape=out_shape, compiler_params=_params("arbitrary"),
    )(*args)


def _qkv_fwd(h, g, win, *, name):
    t = h.shape[0]
    tm, tn = 512, 768
    ni, nj = t // tm, QKV_WIDTH // tn

    def body(h_ref, g_ref, w_ref, qkv_ref, u_ref):
        @pl.when(pl.program_id(1) == 0)
        def _():
            u_ref[...] = _bf(_rms(h_ref[...], g_ref[...]))
        qkv_ref[...] = jnp.dot(u_ref[...], w_ref[...], preferred_element_type=F32)

    row = pl.BlockSpec((tm, D_MODEL), lambda i, j: (i, 0))
    return pl.pallas_call(
        body, name=name, grid=(ni, nj),
        in_specs=[row, pl.BlockSpec((1, D_MODEL), lambda i, j: (0, 0)),
                  pl.BlockSpec((D_MODEL, tn), lambda i, j: (0, j))],
        out_specs=[pl.BlockSpec((tm, tn), lambda i, j: (i, j)), row],
        out_shape=[_sds((t, QKV_WIDTH)), _sds((t, D_MODEL), BF16)],
        compiler_params=_params("arbitrary", "arbitrary"),
    )(h, g, win)


def _qkv_bwd(dq, dk, dv, u, win, *, name):
    t = u.shape[0]
    tn, ts = 512, 512
    nj, ns = QKV_WIDTH // tn, t // ts

    def body(dq_ref, dk_ref, dv_ref, u_ref, w_ref, dw_ref, du_hbm, du_v, acc_ref, sem):
        j = pl.program_id(0)

        @pl.when(j == 0)
        def _():
            du_v[...] = jnp.zeros_like(du_v)

        wj = w_ref[...]
        for role, d_ref in enumerate((dq_ref, dk_ref, dv_ref)):
            @pl.when(j % 3 == role)
            def _():
                acc_ref[...] = jnp.zeros_like(acc_ref)
                for s in range(ns):
                    rows = pl.ds(s * ts, ts)
                    dcol = d_ref[rows, :]
                    acc_ref[...] += _dot_tn(u_ref[rows, :], dcol)
                    du_v[rows, :] += _dot_nt(dcol, wj)
                dw_ref[...] = _bf(acc_ref[...])

        @pl.when(j == nj - 1)
        def _():
            c = pltpu.make_async_copy(du_v, du_hbm, sem)
            c.start()
            c.wait()

    colw = pl.BlockSpec((D_MODEL, tn), lambda j: (0, j))
    grp = pl.BlockSpec((t, tn), lambda j: (0, j // 3))
    return pl.pallas_call(
        body, name=name, grid=(nj,),
        in_specs=[grp, grp, grp, pl.BlockSpec((t, D_MODEL), lambda j: (0, 0)), colw],
        out_specs=[colw, ANY],
        out_shape=[_sds((D_MODEL, QKV_WIDTH), BF16), _sds((t, D_MODEL))],
        scratch_shapes=[pltpu.VMEM((t, D_MODEL), F32), pltpu.VMEM((D_MODEL, tn), F32),
                        pltpu.SemaphoreType.DMA],
        compiler_params=_params("arbitrary"),
    )(dq, dk, dv, u, win)


def _sb_stack(x):
    lo, hi = _head_masks()
    return jnp.concatenate([jnp.where(lo, x, 0.0), jnp.where(hi, x, 0.0)], axis=0)


def _sb_unstack(x2, blk):
    return jnp.where(_head_masks()[0], x2[:blk], x2[blk:])


def _sb_diag_mask(blk):
    r = lax.broadcasted_iota(jnp.int32, (2 * blk, blk), 0) & (blk - 1)
    c = lax.broadcasted_iota(jnp.int32, (2 * blk, blk), 1)
    return c < r


def _tri(n, keep):
    r = lax.broadcasted_iota(jnp.int32, (n, n), 0)
    c = lax.broadcasted_iota(jnp.int32, (n, n), 1)
    return jnp.where(keep(r, c), 1.0, 0.0).astype(BF16)


def _cumsum01(x, u):
    m = x.shape[0]
    hi = _bf(x)
    lo = _bf(x - hi.astype(F32))
    both = jnp.dot(jnp.concatenate([hi, lo], axis=0), u, preferred_element_type=F32)
    return both[:m] + both[m:]


def _sb_fwd(qkv, *, name):
    t = qkv.shape[0]
    blk = SB_BLK
    ni = t // blk

    def body(q_ref, k_ref, v_ref, o_ref, ltot_ref):
        i = pl.program_id(1)
        u_after = _tri(blk, lambda r, c: r > c)
        q2 = [_bf(_sb_stack(q_ref[:, lanes] * ATT_SCALE)) for lanes in SB_LANES]

        def tile(g, k0, mask, acc, c_l):
            kj = k_ref[pl.ds(k0, blk), SB_LANES[g]]
            vj = v_ref[pl.ds(k0, blk), SB_LANES[g]]
            z = _dot_nt(q2[g], kj)
            sp = _softplus(z)
            lf = -sp if mask is None else jnp.where(mask, -sp, 0.0)
            a = jnp.exp(z - sp + _cumsum01(lf, u_after) + c_l)
            if mask is not None:
                a = jnp.where(mask, a, 0.0)
            return acc + _dot(a, vj), c_l + jnp.sum(lf, axis=1, keepdims=True)

        def tiles(k0, mask, carry):
            return tuple(tile(g, k0, mask, *carry[g]) for g in range(SB_GROUP))

        zero = (jnp.zeros((2 * blk, PAIR), F32), jnp.zeros((2 * blk, 1), F32))
        carry = tiles(pl.multiple_of(i * blk, blk), _sb_diag_mask(blk), (zero,) * SB_GROUP)
        carry = lax.fori_loop(
            1, i + 1, lambda jj, c: tiles(pl.multiple_of((i - jj) * blk, blk), None, c), carry)
        for g, (acc, c_l) in enumerate(carry):
            o_ref[:, SB_LANES[g]] = _sb_unstack(acc, blk)
            ltot_ref[:, SB_LANES[g]] = _sb_unstack(jnp.broadcast_to(c_l, (2 * blk, PAIR)), blk)

    width = SB_GROUP * PAIR
    blkspec = pl.BlockSpec((blk, width), lambda p, i: (i, p))
    n_steps = N_PAIRS // SB_GROUP
    return pl.pallas_call(
        body, name=name, grid=(n_steps, ni),
        in_specs=[blkspec,
                  pl.BlockSpec((t, width), lambda p, i: (0, n_steps + p)),
                  pl.BlockSpec((t, width), lambda p, i: (0, 2 * n_steps + p))],
        out_specs=[blkspec, blkspec],
        out_shape=[_sds((t, D_MODEL)), _sds((t, D_MODEL // 2))],
        compiler_params=_params("arbitrary", "arbitrary"),
    )(qkv, qkv, qkv)


def _sb_bwd(qkv, ltot, do, *, name):
    t = qkv.shape[0]
    blk = SB_BLK
    ni = t // blk

    def body(q_ref, k_ref, v_ref, lt_ref, do_ref, dq_ref, dkout_ref, dvout_ref, dk_ref, dv_ref):
        i = pl.program_id(1)

        @pl.when(i == 0)
        def _():
            dk_ref[...] = jnp.zeros_like(dk_ref)
            dv_ref[...] = jnp.zeros_like(dv_ref)

        u_upto = _tri(blk, lambda r, c: r <= c)
        u_before = _tri(blk, lambda r, c: r < c)
        lane = lax.broadcasted_iota(jnp.int32, (1, PAIR), 1)
        q2 = [_bf(_sb_stack(q_ref[:, lanes] * ATT_SCALE)) for lanes in SB_LANES]
        do2 = [_bf(_sb_stack(do_ref[:, lanes])) for lanes in SB_LANES]
        total = [jnp.concatenate(
            [jnp.sum(jnp.where(lane == h * HEAD_DIM, lt_ref[:, lanes], 0.0), axis=1, keepdims=True)
             for h in range(2)], axis=0) for lanes in SB_LANES]

        def tile(g, k0, mask, dq_acc, c_l, c_g):
            krows = pl.ds(k0, blk)
            kj = k_ref[krows, SB_LANES[g]]
            vj = v_ref[krows, SB_LANES[g]]
            z = _dot_nt(q2[g], kj)
            sp = _softplus(z)
            sig = jnp.exp(z - sp)
            lf = -sp if mask is None else jnp.where(mask, -sp, 0.0)
            a = jnp.exp(z - sp + total[g] - (_cumsum01(lf, u_upto) + c_l))
            if mask is not None:
                a = jnp.where(mask, a, 0.0)
            gw = a * _dot_nt(do2[g], vj)
            g_before = jnp.dot(_bf(gw), u_before, preferred_element_type=F32) + c_g
            dz = gw * (1.0 - sig) - g_before * sig
            if mask is not None:
                dz = jnp.where(mask, dz, 0.0)
            dk_ref[krows, SB_LANES[g]] += _dot_tn(dz, q2[g])
            dv_ref[krows, SB_LANES[g]] += _dot_tn(a, do2[g])
            return (dq_acc + _dot(dz, kj), c_l + jnp.sum(lf, axis=1, keepdims=True),
                    c_g + jnp.sum(gw, axis=1, keepdims=True))

        def tiles(k0, mask, carry):
            return tuple(tile(g, k0, mask, *carry[g]) for g in range(SB_GROUP))

        zero = (jnp.zeros((2 * blk, PAIR), F32), jnp.zeros((2 * blk, 1), F32),
                jnp.zeros((2 * blk, 1), F32))
        carry = lax.fori_loop(
            0, i, lambda j, c: tiles(pl.multiple_of(j * blk, blk), None, c), (zero,) * SB_GROUP)
        carry = tiles(pl.multiple_of(i * blk, blk), _sb_diag_mask(blk), carry)
        for g, (dq_acc, _, _) in enumerate(carry):
            dq_ref[:, SB_LANES[g]] = _bf(_sb_unstack(dq_acc, blk) * ATT_SCALE)

        @pl.when(i == ni - 1)
        def _():
            dkout_ref[...] = _bf(dk_ref[...])
            dvout_ref[...] = _bf(dv_ref[...])

    width = SB_GROUP * PAIR
    n_steps = N_PAIRS // SB_GROUP
    blkspec = lambda off: pl.BlockSpec((blk, width), lambda p, i: (i, off + p))
    full = lambda off: pl.BlockSpec((t, width), lambda p, i: (0, off + p))
    return pl.pallas_call(
        body, name=name, grid=(n_steps, ni),
        in_specs=[blkspec(0), full(n_steps), full(2 * n_steps), blkspec(0), blkspec(0)],
        out_specs=[blkspec(0), full(0), full(0)],
        out_shape=[_sds((t, D_MODEL), BF16)] * 3,
        scratch_shapes=[pltpu.VMEM((t, width), F32), pltpu.VMEM((t, width), F32)],
        compiler_params=_params("arbitrary", "arbitrary"),
    )(qkv, qkv, qkv, ltot, do)


def _ch_mask(i):
    r = lax.broadcasted_iota(jnp.int32, (CH_QB, CH_WIN), 0)
    c = lax.broadcasted_iota(jnp.int32, (CH_QB, CH_WIN), 1)
    qc = LOOKBACK + lax.shift_right_arithmetic(r, 6)
    kc = lax.shift_right_arithmetic(c, 6)
    first = i * (CH_QB // CHUNK) - LOOKBACK
    return (kc <= qc) & (kc >= qc - LOOKBACK) & (kc + first >= 0)


def _ch_probs(qm, kw, bias_h, mask):
    z = _dot_nt(qm, kw) * ATT_SCALE + bias_h
    z = jnp.where(mask, z, NEG_INF)
    e = jnp.exp(z - jnp.max(z, axis=1, keepdims=True))
    return e / jnp.sum(e, axis=1, keepdims=True)


def _ch_fill(pad_ref, src_ref, t):
    pad_ref[pl.ds(0, CH_LOOK), :] = jnp.zeros((CH_LOOK, PAIR), BF16)
    pad_ref[pl.ds(CH_LOOK, t), :] = _bf(src_ref[...])


def _ch_fwd(qkv, bias, o_in, *, name):
    t = qkv.shape[0]
    ni = t // CH_QB

    def body(q_ref, k_ref, v_ref, bias_ref, _alias, o_ref, kpad, vpad):
        i = pl.program_id(1)

        @pl.when(i == 0)
        def _():
            _ch_fill(kpad, k_ref, t)
            _ch_fill(vpad, v_ref, t)

        win = pl.ds(pl.multiple_of(i * CH_QB, CH_QB), CH_WIN)
        kw, vw = kpad[win, :], vpad[win, :]
        mask = _ch_mask(i)
        q = q_ref[...]
        outs = []
        for h, hm in enumerate(_head_masks()):
            p = _ch_probs(jnp.where(hm, q, 0.0), kw, bias_ref[h], mask)
            outs.append(_dot(p, vw))
        o_ref[...] = jnp.where(_head_masks()[0], outs[0], outs[1])

    full = lambda off: pl.BlockSpec((t, PAIR), lambda p, i: (0, off + p))
    return pl.pallas_call(
        body, name=name, grid=(N_PAIRS, ni),
        in_specs=[pl.BlockSpec((CH_QB, PAIR), lambda p, i: (i, 3 * N_PAIRS + p)),
                  full(4 * N_PAIRS), full(5 * N_PAIRS),
                  pl.BlockSpec((2, CH_QB, CH_WIN), lambda p, i: (p, 0, 0)), ANY],
        out_specs=pl.BlockSpec((CH_QB, PAIR), lambda p, i: (i, N_PAIRS + p)),
        out_shape=_sds((t, D_MODEL)),
        scratch_shapes=[pltpu.VMEM((t + CH_LOOK, PAIR), BF16)] * 2,
        input_output_aliases={4: 0},
        compiler_params=_params("arbitrary", "arbitrary"),
    )(qkv, qkv, qkv, bias, o_in)


def _ch_bwd(qkv, bias, o, do, dq_in, dk_in, dv_in, *, name):
    t = qkv.shape[0]
    ni = t // CH_QB

    def body(q_ref, k_ref, v_ref, bias_ref, o_ref, do_ref, _a0, _a1, _a2,
             dq_ref, dkout_ref, dvout_ref, dbias_ref, kpad, vpad, dkpad, dvpad):
        i = pl.program_id(1)

        @pl.when(i == 0)
        def _():
            _ch_fill(kpad, k_ref, t)
            _ch_fill(vpad, v_ref, t)
            dkpad[...] = jnp.zeros_like(dkpad)
            dvpad[...] = jnp.zeros_like(dvpad)
            dbias_ref[...] = jnp.zeros_like(dbias_ref)

        win = pl.ds(pl.multiple_of(i * CH_QB, CH_QB), CH_WIN)
        kw, vw = kpad[win, :], vpad[win, :]
        mask = _ch_mask(i)
        q, o_blk, do_blk = q_ref[...], o_ref[...], do_ref[...]
        dqs = []
        for h, hm in enumerate(_head_masks()):
            qm = _bf(jnp.where(hm, q, 0.0))
            dom = jnp.where(hm, do_blk, 0.0)
            delta = jnp.sum(dom * o_blk, axis=1, keepdims=True)
            dom = _bf(dom)
            p = _ch_probs(qm, kw, bias_ref[h], mask)
            ds = p * (_dot_nt(dom, vw) - delta)
            dbias_ref[h] += ds
            dsz = ds * ATT_SCALE
            dqs.append(_dot(dsz, kw))
            dkpad[win, :] += _dot_tn(dsz, qm)
            dvpad[win, :] += _dot_tn(p, dom)
        dq_ref[...] = _bf(jnp.where(_head_masks()[0], dqs[0], dqs[1]))

        @pl.when(i == ni - 1)
        def _():
            dkout_ref[...] = _bf(dkpad[pl.ds(CH_LOOK, t), :])
            dvout_ref[...] = _bf(dvpad[pl.ds(CH_LOOK, t), :])

    blkspec = lambda off: pl.BlockSpec((CH_QB, PAIR), lambda p, i: (i, off + p))
    full = lambda off: pl.BlockSpec((t, PAIR), lambda p, i: (0, off + p))
    bias_spec = pl.BlockSpec((2, CH_QB, CH_WIN), lambda p, i: (p, 0, 0))
    return pl.pallas_call(
        body, name=name, grid=(N_PAIRS, ni),
        in_specs=[blkspec(3 * N_PAIRS), full(4 * N_PAIRS), full(5 * N_PAIRS), bias_spec,
                  blkspec(N_PAIRS), blkspec(N_PAIRS), ANY, ANY, ANY],
        out_specs=[blkspec(N_PAIRS), full(N_PAIRS), full(N_PAIRS), bias_spec],
        out_shape=[_sds((t, D_MODEL), BF16)] * 3 + [_sds((2 * N_PAIRS, CH_QB, CH_WIN))],
        scratch_shapes=[pltpu.VMEM((t + CH_LOOK, PAIR), BF16)] * 2
        + [pltpu.VMEM((t + CH_LOOK, PAIR), F32)] * 2,
        input_output_aliases={6: 0, 7: 1, 8: 2},
        compiler_params=_params("arbitrary", "arbitrary"),
    )(qkv, qkv, qkv, bias, o, do, dq_in, dk_in, dv_in)


def _bias_expand(fvec, *, name):
    n_heads = fvec.shape[0]

    def body(f_ref, o_ref, rows8):
        row = f_ref[0]
        for r in range(8):
            rows8[pl.ds(r, 1), :] = pltpu.roll(row, r, 1)
        base = rows8[...]
        for blk in range(CH_QB // 8):
            o_ref[0, pl.ds(8 * blk, 8), :] = pltpu.roll(base, 8 * blk, 1)

    return pl.pallas_call(
        body, name=name, grid=(n_heads,),
        in_specs=[pl.BlockSpec((1, 1, CH_WIN), lambda h: (h, 0, 0))],
        out_specs=pl.BlockSpec((1, CH_QB, CH_WIN), lambda h: (h, 0, 0)),
        out_shape=_sds((n_heads, CH_QB, CH_WIN)),
        scratch_shapes=[pltpu.VMEM((8, CH_WIN), F32)],
        compiler_params=_params("arbitrary"),
    )(fvec)


def _bias_grad(dbias, *, name):
    n_heads = dbias.shape[0]
    first = CH_LOOK - REL_CLIP

    def body(d_ref, o_ref, acc8):
        acc = jnp.zeros((8, CH_WIN), F32)
        for blk in range(CH_QB // 8):
            acc = acc + pltpu.roll(d_ref[0, pl.ds(8 * blk, 8), :], (CH_WIN - 8 * blk) % CH_WIN, 1)
        acc8[...] = acc
        dvec = jnp.zeros((1, CH_WIN), F32)
        for r in range(8):
            dvec = dvec + pltpu.roll(acc8[pl.ds(r, 1), :], (CH_WIN - r) % CH_WIN, 1)
        lane = lax.broadcasted_iota(jnp.int32, (1, CH_WIN), 1)
        clipped = (lane <= first) | (lane >= first + REL_CLIP + CHUNK)
        total = jnp.sum(jnp.where(clipped, dvec, 0.0), axis=1, keepdims=True)
        o_ref[0] = jnp.where(lane == first, total, dvec)

    return pl.pallas_call(
        body, name=name, grid=(n_heads,),
        in_specs=[pl.BlockSpec((1, CH_QB, CH_WIN), lambda h: (h, 0, 0))],
        out_specs=pl.BlockSpec((1, 1, CH_WIN), lambda h: (h, 0, 0)),
        out_shape=_sds((n_heads, 1, CH_WIN)),
        scratch_shapes=[pltpu.VMEM((8, CH_WIN), F32)],
        compiler_params=_params("arbitrary"),
    )(dbias)


def _out_fwd(o, h1, g_sb, g_ch, g_post, wout, *, name):
    t = o.shape[0]
    tm = 512
    half = D_MODEL // 2

    def body(o_ref, h_ref, gsb_ref, gch_ref, gpost_ref, w_ref, h2_ref, mixed_ref, y_ref):
        ov = o_ref[...]
        mixed = jnp.concatenate([_rms(ov[:, :half], gsb_ref[...]),
                                 _rms(ov[:, half:], gch_ref[...])], axis=1)
        mixed_ref[...] = _bf(mixed)
        y = _dot(mixed, w_ref[...])
        y_ref[...] = y
        h2_ref[...] = h_ref[...] + _rms(y, gpost_ref[...])

    row = pl.BlockSpec((tm, D_MODEL), lambda i: (i, 0))
    gain = lambda n: pl.BlockSpec((1, n), lambda i: (0, 0))
    return pl.pallas_call(
        body, name=name, grid=(t // tm,),
        in_specs=[row, row, gain(half), gain(half), gain(D_MODEL),
                  pl.BlockSpec((D_MODEL, D_MODEL), lambda i: (0, 0))],
        out_specs=[row, row, row],
        out_shape=[_sds((t, D_MODEL)), _sds((t, D_MODEL), BF16), _sds((t, D_MODEL))],
        compiler_params=_params("arbitrary"),
    )(o, h1, g_sb, g_ch, g_post, wout)


def _out_bwd(dy, mixed, o, g_sb, g_ch, wout, *, name):
    t = o.shape[0]
    tm = 512
    ni = t // tm
    half = D_MODEL // 2

    def body(dy_ref, mixed_ref, o_ref, gsb_ref, gch_ref, w_ref,
             dw_ref, do_ref, dgsb_ref, dgch_ref, acc_ref):
        i = pl.program_id(0)

        @pl.when(i == 0)
        def _():
            acc_ref[...] = jnp.zeros_like(acc_ref)
            dgsb_ref[...] = jnp.zeros_like(dgsb_ref)
            dgch_ref[...] = jnp.zeros_like(dgch_ref)

        dyv = dy_ref[...]
        acc_ref[...] += _dot_tn(mixed_ref[...], dyv)
        dm = _dot_nt(dyv, w_ref[...])
        ov = o_ref[...]
        doa, dga = _rms_bwd(dm[:, :half], ov[:, :half], gsb_ref[...])
        dob, dgb = _rms_bwd(dm[:, half:], ov[:, half:], gch_ref[...])
        do_ref[...] = jnp.concatenate([doa, dob], axis=1)
        dgsb_ref[...] += dga
        dgch_ref[...] += dgb

        @pl.when(i == ni - 1)
        def _():
            dw_ref[...] = _bf(acc_ref[...])

    row = pl.BlockSpec((tm, D_MODEL), lambda i: (i, 0))
    gain = pl.BlockSpec((1, half), lambda i: (0, 0))
    sq = pl.BlockSpec((D_MODEL, D_MODEL), lambda i: (0, 0))
    return pl.pallas_call(
        body, name=name, grid=(ni,),
        in_specs=[row, row, row, gain, gain, sq],
        out_specs=[sq, row, gain, gain],
        out_shape=[_sds((D_MODEL, D_MODEL), BF16), _sds((t, D_MODEL)),
                   _sds((1, half)), _sds((1, half))],
        scratch_shapes=[pltpu.VMEM((D_MODEL, D_MODEL), F32)],
        compiler_params=_params("arbitrary"),
    )(dy, mixed, o, g_sb, g_ch, wout)


def _ple(p, h3, target, wp, wgate, g, *, name):
    t = h3.shape[0]
    tm = 512
    ni = t // tm

    def body(p_ref, h_ref, tgt_ref, wp_ref, wg_ref, g_ref,
             loss_ref, dres_ref, dwp_ref, dwg_ref, dg_ref, accp, accg):
        i = pl.program_id(0)

        @pl.when(i == 0)
        def _():
            loss_ref[...] = jnp.zeros_like(loss_ref)
            dg_ref[...] = jnp.zeros_like(dg_ref)
            accp[...] = jnp.zeros_like(accp)
            accg[...] = jnp.zeros_like(accg)

        pv, hv, gv = p_ref[...], h_ref[...], g_ref[...]
        pe = _dot(pv, wp_ref[...])
        sig = _sigmoid(_dot(hv, wg_ref[...]))
        e = pe * sig
        err = hv + _rms(e, gv) - tgt_ref[...]
        tok = jnp.mean(err * err, axis=-1, keepdims=True)
        loss_ref[...] += 0.5 * jnp.sum(tok, axis=0, keepdims=True)
        dh4 = err * (1.0 / D_MODEL)
        de, dg = _rms_bwd(dh4, e, gv)
        dg_ref[...] += dg
        dpe = de * sig
        dgt = de * pe * sig * (1.0 - sig)
        accp[...] += _dot_tn(pv, dpe)
        accg[...] += _dot_tn(hv, dgt)
        dres_ref[...] = dh4 + _dot_nt(dgt, wg_ref[...])

        @pl.when(i == ni - 1)
        def _():
            dwp_ref[...] = _bf(accp[...])
            dwg_ref[...] = _bf(accg[...])

    row = pl.BlockSpec((tm, D_MODEL), lambda i: (i, 0))
    const = lambda r, c: pl.BlockSpec((r, c), lambda i: (0, 0))
    return pl.pallas_call(
        body, name=name, grid=(ni,),
        in_specs=[pl.BlockSpec((tm, PLE_DIM), lambda i: (i, 0)), row, row,
                  const(PLE_DIM, D_MODEL), const(D_MODEL, D_MODEL), const(1, D_MODEL)],
        out_specs=[const(1, 128), row, const(PLE_DIM, D_MODEL), const(D_MODEL, D_MODEL),
                   const(1, D_MODEL)],
        out_shape=[_sds((1, 128)), _sds((t, D_MODEL)), _sds((PLE_DIM, D_MODEL), BF16),
                   _sds((D_MODEL, D_MODEL), BF16), _sds((1, D_MODEL))],
        scratch_shapes=[pltpu.VMEM((PLE_DIM, D_MODEL), F32), pltpu.VMEM((D_MODEL, D_MODEL), F32)],
        compiler_params=_params("arbitrary"),
    )(p, h3, target, wp, wgate, g)


def _rel_bias_to_fvec(rel_bias):
    rev = rel_bias[:, ::-1]
    n_heads = rel_bias.shape[0]
    first = CH_LOOK - REL_CLIP
    n_var = REL_CLIP + CHUNK
    clipped = rev[:, :1]
    fvec = jnp.concatenate([jnp.broadcast_to(clipped, (n_heads, first)), rev[:, :n_var],
                            jnp.broadcast_to(clipped, (n_heads, CH_WIN - first - n_var))], axis=1)
    return fvec.reshape(n_heads, 1, CH_WIN)


def _fvec_grad_to_rel_bias(dfvec):
    first = CH_LOOK - REL_CLIP
    n_var = REL_CLIP + CHUNK
    rev = jnp.pad(dfvec[:, 0, first:first + n_var], ((0, 0), (0, N_REL - n_var)))
    return rev[:, ::-1]


def _local_step(x, p, target, g, weights_for, grads_done, fvec):
    w, tie = weights_for(0, x)
    w = dict(w)
    h1, n1, a1, b1, f1 = _ffn_fwd(x, g["ffn1_pre"] + tie, g["ffn1_post"],
                                  w["ffn1_gate"], w["ffn1_up"], w["ffn1_down"], name="ffn1_fwd")
    more, tie = weights_for(1, h1)
    w.update(more)
    qkv, u = _qkv_fwd(h1, g["mix_pre"] + tie, w["in"], name="qkv_fwd")
    bias = _bias_expand(fvec, name="bias_expand")
    o, ltot = _sb_fwd(qkv, name="sb_fwd")
    o = _ch_fwd(qkv, bias, o, name="ch_fwd")
    h2, mixed, y = _out_fwd(o, h1, g["out_sb"], g["out_ch"], g["mix_post"], w["out"], name="out_fwd")
    w.update(weights_for(2, h2)[0])
    h3, n2, a2, b2, f2 = _ffn_fwd(h2, g["ffn2_pre"], g["ffn2_post"],
                                  w["ffn2_gate"], w["ffn2_up"], w["ffn2_down"], name="ffn2_fwd")
    loss, dh3, dwp, dwgate, dg_ple = _ple(p, h3, target, w["ple_proj"], w["ple_gate"],
                                          g["ple_post"], name="ple")
    tie = grads_done(0, {"ple_proj": dwp, "ple_gate": dwgate})

    df2, dg_ffn2_post = _junction(dh3, post=(f2, g["ffn2_post"] + tie, 0.5), name="junction3")
    dwg2, dwu2, dwd2, dn2 = _ffn_bwd(n2, df2, a2, b2, w["ffn2_gate"], w["ffn2_up"],
                                     w["ffn2_down"], name="ffn2_bwd")
    tie = grads_done(1, {"ffn2_gate": dwg2, "ffn2_up": dwu2, "ffn2_down": dwd2})
    dh2, dg_ffn2_pre, dy, dg_mix_post = _junction(
        dh3, pre=(dn2, h2, g["ffn2_pre"] + tie), post=(y, g["mix_post"], 1.0), name="junction2")
    dwout, do, dg_sb, dg_ch = _out_bwd(dy, mixed, o, g["out_sb"], g["out_ch"], w["out"],
                                       name="out_bwd")
    dq, dk, dv = _sb_bwd(qkv, ltot, do, name="sb_bwd")
    dq, dk, dv, dbias = _ch_bwd(qkv, bias, o, do, dq, dk, dv, name="ch_bwd")
    dfvec = _bias_grad(dbias, name="bias_grad")
    dwin, du = _qkv_bwd(dq, dk, dv, u, w["in"], name="qkv_bwd")
    tie = grads_done(2, {"out": dwout, "in": dwin})
    dh1, dg_mix_pre, df1, dg_ffn1_post = _junction(
        dh2, pre=(du, h1, g["mix_pre"] + tie), post=(f1, g["ffn1_post"], 0.5), name="junction1")
    dwg1, dwu1, dwd1, dn1 = _ffn_bwd(n1, df1, a1, b1, w["ffn1_gate"], w["ffn1_up"],
                                     w["ffn1_down"], name="ffn1_bwd")
    tie = grads_done(3, {"ffn1_gate": dwg1, "ffn1_up": dwu1, "ffn1_down": dwd1})
    dx, dg_ffn1_pre = _junction(dh1, pre=(dn1, x, g["ffn1_pre"] + tie), name="junction0")

    dg = {"ffn1_pre": dg_ffn1_pre, "ffn1_post": dg_ffn1_post, "mix_pre": dg_mix_pre,
          "mix_post": dg_mix_post, "out_sb": dg_sb, "out_ch": dg_ch,
          "ffn2_pre": dg_ffn2_pre, "ffn2_post": dg_ffn2_post, "ple_post": dg_ple}
    return loss, dx, dg, dfvec


_WEIGHTS = (
    ("ffn1_gate", "row", FF_SHARD, FF_SHARD_PAD, D_MODEL),
    ("ffn1_up", "row", FF_SHARD, FF_SHARD_PAD, D_MODEL),
    ("ffn1_down", "row", FF_SHARD, FF_SHARD_PAD, D_MODEL),
    ("in", "col", QKV_SHARD, QKV_SHARD, D_MODEL),
    ("out", "row", ROW_SHARD, ROW_SHARD, D_MODEL),
    ("ffn2_gate", "row", FF_SHARD, FF_SHARD_PAD, D_MODEL),
    ("ffn2_up", "row", FF_SHARD, FF_SHARD_PAD, D_MODEL),
    ("ffn2_down", "row", FF_SHARD, FF_SHARD_PAD, D_MODEL),
    ("ple_proj", "col", ROW_SHARD, ROW_SHARD, PLE_DIM),
    ("ple_gate", "row", ROW_SHARD, ROW_SHARD, D_MODEL),
)
_TRANSPOSED = ("ffn1_gate", "ffn1_up", "ffn2_gate", "ffn2_up")
_SPEC = {n: (kind, valid, pad, other) for n, kind, valid, pad, other in _WEIGHTS}
_GATHER_STAGES = (("ffn1_gate", "ffn1_up", "ffn1_down"), ("in", "out"),
                  ("ffn2_gate", "ffn2_up", "ffn2_down", "ple_proj", "ple_gate"))
_SCATTER_STAGES = (("ple_proj", "ple_gate"), ("ffn2_gate", "ffn2_up", "ffn2_down"),
                   ("out", "in"), ("ffn1_gate", "ffn1_up", "ffn1_down"))
HBM = pl.BlockSpec(memory_space=pltpu.HBM)
SEM = pl.BlockSpec(memory_space=pltpu.SEMAPHORE)
EFFECT = pltpu.SideEffectType.DATAFLOW_SIDE_EFFECTING


def _shard_shape(kind, size, other):
    return (other, size) if kind == "col" else (size, other)


def _window(ref, kind, start, size):
    return ref.at[:, pl.ds(start, size)] if kind == "col" else ref.at[pl.ds(start, size), :]


def _device_tuple(k):
    return (k // 4, (k // 2) % 2, k % 2)


def _my_index():
    return 4 * lax.axis_index("x") + 2 * lax.axis_index("y") + lax.axis_index("c")


def _pack_weights(shards):
    nw = len(_WEIGHTS)

    def body(*refs):
        ins, packed, full = refs[:nw], refs[nw:2 * nw], refs[2 * nw:3 * nw]
        sem = refs[3 * nw]
        me = _my_index()
        for (_, kind, valid, pad, _), src, dst in zip(_WEIGHTS, ins, packed):
            if pad != valid:
                dst[...] = jnp.zeros_like(dst)
            if kind == "col":
                dst[:, pl.ds(0, valid)] = _bf(src[...])
            else:
                dst[pl.ds(0, valid), :] = _bf(src[...])
        for k in range(N_DEV):
            @pl.when(me == k)
            def _():
                for w, (_, kind, _, pad, _) in enumerate(_WEIGHTS):
                    pltpu.make_async_copy(packed[w], _window(full[w], kind, k * pad, pad),
                                          sem.at[w]).start()
        for w, (_, kind, _, pad, _) in enumerate(_WEIGHTS):
            pltpu.make_async_copy(packed[w], _window(full[w], kind, 0, pad), sem.at[w]).wait()

    whole = lambda shape: pl.BlockSpec(shape, lambda i: (0, 0))
    packed_shapes = [_shard_shape(kind, pad, other) for _, kind, _, pad, other in _WEIGHTS]
    outs = pl.pallas_call(
        body, name="pack_weights", grid=(1,),
        in_specs=[whole(a.shape) for a in shards],
        out_specs=[whole(s) for s in packed_shapes] + [ANY] * nw,
        out_shape=[_sds(s, BF16) for s in packed_shapes]
        + [_sds(_shard_shape(kind, N_DEV * pad, other), BF16) for _, kind, _, pad, other in _WEIGHTS],
        scratch_shapes=[pltpu.SemaphoreType.DMA((nw,))],
        compiler_params=_params("arbitrary"),
    )(*shards)
    names = [n for n, *_ in _WEIGHTS]
    return dict(zip(names, outs[:nw])), dict(zip(names, outs[nw:]))


def _hbm(a):
    return pltpu.with_memory_space_constraint(a, pltpu.HBM)


def _split_start(name, n, body_copies, sources, lands, after):
    arrays = list(sources) + list(lands)
    ns, na = len(sources), len(arrays)

    def body(*refs):
        src, land = refs[:ns], refs[ns:na]
        send, recv = refs[na + 1], refs[na + 2]
        token = refs[-1]
        body_copies(src, land, send, recv)
        token[...] = jnp.zeros_like(token)

    out = pl.pallas_call(
        body, name=name,
        out_shape=(pltpu.SemaphoreType.DMA((n,)), pltpu.SemaphoreType.DMA((n,)),
                   *[pltpu.HBM(a.shape, a.dtype) for a in arrays], _sds((8, 128))),
        in_specs=[HBM] * na + [ANY], out_specs=(SEM, SEM, *[HBM] * na, VMEM),
        input_output_aliases={i: 2 + i for i in range(na)},
        compiler_params=pltpu.CompilerParams(has_side_effects=EFFECT),
    )(*[_hbm(a) for a in arrays], after)
    return out[0], out[1], out[2:2 + ns], out[2 + ns:2 + na], out[-1]


def _split_wait(name, n, seven_of, send, recv, sources, lands, after, keep_sources=False):
    arrays = list(sources) + list(lands)
    ns, na = len(sources), len(arrays)

    def body(*refs):
        land = refs[ns:na]
        send_ref, recv_ref = refs[na], refs[na + 1]
        myself = (lax.axis_index("x"), lax.axis_index("y"), lax.axis_index("c"))
        for w in range(n):
            seven = seven_of(w, land[w])
            copy = pltpu.make_async_remote_copy(
                src_ref=seven, dst_ref=seven, send_sem=send_ref.at[w], recv_sem=recv_ref.at[w],
                device_id=myself, device_id_type=MESH)
            copy.wait_send()
            copy.wait_recv()

    out = pl.pallas_call(
        body, name=name,
        out_shape=[pltpu.HBM(a.shape, a.dtype) for a in arrays],
        in_specs=[HBM] * na + [SEM, SEM, ANY], out_specs=[HBM] * na,
        input_output_aliases={i: i for i in range(na)},
        compiler_params=pltpu.CompilerParams(has_side_effects=EFFECT),
    )(*arrays, send, recv, after)
    return out if keep_sources else out[ns:]


_ALL_PEERS = (1, 2, 3, 4, 5, 6, 7)
_NEAR_PEERS = (1, 2, 4, 6)
_FAR_CHIPS = (2, 4, 6)


def _gather_start(stage, names, packed, full, after, peers=_ALL_PEERS):
    def copies(src, land, send, recv):
        me = _my_index()
        for k in range(N_DEV):
            @pl.when(me == k)
            def _():
                for w, name in enumerate(names):
                    kind, _, pad, _ = _SPEC[name]
                    dst = _window(land[w], kind, k * pad, pad)
                    for mask in peers:
                        pltpu.make_async_remote_copy(
                            src_ref=src[w], dst_ref=dst, send_sem=send.at[w],
                            recv_sem=recv.at[w], device_id=_device_tuple(k ^ mask),
                            device_id_type=MESH).start()

    return _split_start(f"gather_start{stage}", len(names), copies,
                        [packed[n] for n in names], [full[n] for n in names], after)


def _gather_wait(stage, names, started, after, count=N_DEV - 1):
    send, recv, src, land, _ = started

    def bytes_of(w, ref):
        kind, _, pad, _ = _SPEC[names[w]]
        return _window(ref, kind, 0, count * pad)

    return dict(zip(names, _split_wait(f"gather_wait{stage}", len(names), bytes_of,
                                       send, recv, src, land, after)))


def _relay_start(stage, names, full, after):
    def copies(_, land, send, recv):
        me = _my_index()
        for k in range(N_DEV):
            @pl.when(me == k)
            def _():
                for w, name in enumerate(names):
                    kind, _, pad, _ = _SPEC[name]
                    for mask in _FAR_CHIPS:
                        win = _window(land[w], kind, (k ^ mask) * pad, pad)
                        pltpu.make_async_remote_copy(
                            src_ref=win, dst_ref=win, send_sem=send.at[w], recv_sem=recv.at[w],
                            device_id=_device_tuple(k ^ 1), device_id_type=MESH).start()

    return _split_start(f"relay_start{stage}", len(names), copies, [],
                        [full[n] for n in names], after)


def _scatter_start(stage, names, grads, after):
    def copies(src, land, send, recv):
        me = _my_index()
        for k in range(N_DEV):
            @pl.when(me != k)
            def _():
                slot = lax.rem(me + (N_DEV - 1 - k), N_DEV)
                for w, name in enumerate(names):
                    kind, _, pad, _ = _SPEC[name]
                    pltpu.make_async_remote_copy(
                        src_ref=_window(src[w], kind, k * pad, pad), dst_ref=land[w].at[slot],
                        send_sem=send.at[w], recv_sem=recv.at[w],
                        device_id=_device_tuple(k), device_id_type=MESH).start()

    lands = [lax.empty((N_DEV - 1,) + _shard_shape(_SPEC[m][0], _SPEC[m][2], _SPEC[m][3]), BF16)
             for m in names]
    return _split_start(f"scatter_start{stage}", len(names), copies, grads, lands, after)


def _scatter_wait(stage, names, started, after):
    send, recv, src, land, _ = started
    n = len(names)
    out = _split_wait(f"scatter_wait{stage}", n, lambda w, ref: ref, send, recv, src, land, after,
                      keep_sources=True)
    return dict(zip(names, out[:n])), dict(zip(names, out[n:]))


N_CHIPS = N_DEV // 2


def _pair_start(stage, names, grads, after):
    def copies(src, land, send, recv):
        me = _my_index()
        for k in range(N_DEV):
            @pl.when(me == k)
            def _():
                for w, name in enumerate(names):
                    kind, _, pad, _ = _SPEC[name]
                    for chip in range(N_CHIPS):
                        j = 2 * chip + ((k ^ 1) & 1)
                        pltpu.make_async_remote_copy(
                            src_ref=_window(src[w], kind, j * pad, pad), dst_ref=land[w].at[chip],
                            send_sem=send.at[w], recv_sem=recv.at[w],
                            device_id=_device_tuple(k ^ 1), device_id_type=MESH).start()

    lands = [lax.empty((N_CHIPS,) + _shard_shape(_SPEC[m][0], _SPEC[m][2], _SPEC[m][3]), BF16)
             for m in names]
    return _split_start(f"pair_start{stage}", len(names), copies, grads, lands, after)


def _pair_sum(dw_full, pair, *, pad, name):
    other = dw_full.shape[1]

    def body(own_ref, pair_ref, out_ref):
        out_ref[0] = _bf(own_ref[...].astype(F32) + pair_ref[0].astype(F32))

    slot = pl.BlockSpec((1, pad, other), lambda q: (q, 0, 0))
    return pl.pallas_call(
        body, name=name, grid=(N_CHIPS,),
        in_specs=[pl.BlockSpec((pad, other), lambda q: (2 * q + lax.axis_index("c"), 0)), slot],
        out_specs=slot, out_shape=_sds((N_CHIPS, pad, other), BF16),
        compiler_params=_params("arbitrary"),
    )(dw_full, pair)


def _chip_start(stage, names, sums, after):
    def copies(src, land, send, recv):
        me = _my_index()
        my_chip = lax.shift_right_logical(me, 1)
        for k in range(N_DEV):
            @pl.when((me != k) & (((me ^ k) & 1) == 0))
            def _():
                slot = lax.rem(my_chip + (N_CHIPS - 1 - k // 2), N_CHIPS)
                for w in range(len(names)):
                    pltpu.make_async_remote_copy(
                        src_ref=src[w].at[k // 2], dst_ref=land[w].at[slot],
                        send_sem=send.at[w], recv_sem=recv.at[w],
                        device_id=_device_tuple(k), device_id_type=MESH).start()

    lands = [lax.empty((N_CHIPS - 1,) + a.shape[1:], BF16) for a in sums]
    return _split_start(f"chip_start{stage}", len(names), copies, sums, lands, after)


def _adamw_chip(w, m, v, land, sums, *, name):
    shape = w.shape

    def body(w_ref, m_ref, v_ref, land_ref, own_ref, *outs):
        rows = pl.ds(0, shape[0])
        grad = own_ref[0, rows, :].astype(F32)
        for s in range(N_CHIPS - 1):
            grad = grad + land_ref[s, rows, :].astype(F32)
        _adam_update(w_ref, m_ref, v_ref, grad, *outs)

    whole = lambda a: pl.BlockSpec(a.shape, lambda i: (0,) * a.ndim)
    own = pl.BlockSpec((1,) + sums.shape[1:],
                       lambda i: (2 * lax.axis_index("x") + lax.axis_index("y"), 0, 0))
    return pl.pallas_call(
        body, name=name, grid=(1,),
        in_specs=[whole(w), whole(m), whole(v), whole(land), own],
        out_specs=[whole(w)] * 4, out_shape=[_sds(shape)] * 4,
        compiler_params=_params("arbitrary"),
    )(w, m, v, land, sums)


def _allreduce_small(small, after):
    shape = small.shape

    def body(in_ref, _after, out_ref, gath, send, recv):
        me = _my_index()
        for k in range(N_DEV):
            @pl.when(me != k)
            def _():
                pltpu.make_async_remote_copy(
                    src_ref=in_ref, dst_ref=gath.at[me], send_sem=send, recv_sem=recv,
                    device_id=_device_tuple(k), device_id_type=MESH).start()

            @pl.when(me == k)
            def _():
                gath[k] = in_ref[...]
        seven = gath.at[pl.ds(0, N_DEV - 1)]
        pltpu.make_async_remote_copy(
            src_ref=seven, dst_ref=seven, send_sem=send, recv_sem=recv,
            device_id=_device_tuple(0), device_id_type=MESH).wait()
        total = gath[0]
        for s in range(1, N_DEV):
            total = total + gath[s]
        out_ref[...] = total

    return pl.pallas_call(
        body, name="allreduce_small",
        in_specs=[VMEM, ANY], out_specs=VMEM, out_shape=_sds(shape),
        scratch_shapes=[pltpu.VMEM((N_DEV,) + shape, F32),
                        pltpu.SemaphoreType.DMA, pltpu.SemaphoreType.DMA],
    )(small, after)


def _adam_update(w_ref, m_ref, v_ref, grad, grad_ref, delta_ref, nm_ref, nv_ref):
    new_m = ADAM_B1 * m_ref[...] + (1.0 - ADAM_B1) * grad
    new_v = ADAM_B2 * v_ref[...] + (1.0 - ADAM_B2) * (grad * grad)
    m_hat = new_m / (1.0 - ADAM_B1 ** ADAM_STEP)
    v_hat = new_v / (1.0 - ADAM_B2 ** ADAM_STEP)
    grad_ref[...] = grad
    delta_ref[...] = -ADAM_LR * (m_hat / (jnp.sqrt(v_hat) + ADAM_EPS) + ADAM_WD * w_ref[...])
    nm_ref[...] = new_m
    nv_ref[...] = new_v


def _adamw(w, m, v, g, *, name):
    def body(w_ref, m_ref, v_ref, g_ref, *outs):
        _adam_update(w_ref, m_ref, v_ref, g_ref[...], *outs)

    whole = pl.BlockSpec(w.shape, lambda i: (0,) * w.ndim)
    return pl.pallas_call(
        body, name=name, grid=(1,), in_specs=[whole] * 4, out_specs=[whole] * 4,
        out_shape=[_sds(w.shape)] * 4, compiler_params=_params("arbitrary"),
    )(w, m, v, g)


def _adamw_gains(small, params):
    n = len(params)

    def body(small_ref, *refs):
        ins, outs = refs[:3 * n], refs[3 * n:]
        for r in range(n):
            width = ins[3 * r].shape[1]
            if width == D_MODEL:
                grad = small_ref[pl.ds(r, 1), :]
            else:
                grad = small_ref[pl.ds(len(_GAINS), 1), pl.ds((r - len(_GAINS)) * width, width)]
            _adam_update(*ins[3 * r:3 * r + 3], grad, *outs[4 * r:4 * r + 4])

    whole = lambda a: pl.BlockSpec(a.shape, lambda i: (0, 0))
    flat = [a for group in params for a in group]
    return pl.pallas_call(
        body, name="adamw_gains", grid=(1,),
        in_specs=[whole(small)] + [whole(a) for a in flat],
        out_specs=[whole(w) for w, _, _ in params for _ in range(4)],
        out_shape=[_sds(w.shape) for w, _, _ in params for _ in range(4)],
        compiler_params=_params("arbitrary"),
    )(small, *flat)


def _adamw_shard(w, m, v, land, dw_full, *, kind, pad, name):
    shape = w.shape
    other = shape[0] if kind == "col" else shape[1]

    def body(w_ref, m_ref, v_ref, land_ref, own_ref, *outs):
        valid = ((slice(None), pl.ds(0, shape[1])) if kind == "col"
                 else (pl.ds(0, shape[0]), slice(None)))
        grad = own_ref[valid].astype(F32)
        for s in range(N_DEV - 1):
            grad = grad + land_ref[(s,) + valid].astype(F32)
        _adam_update(w_ref, m_ref, v_ref, grad, *outs)

    whole = lambda a: pl.BlockSpec(a.shape, lambda i: (0,) * a.ndim)
    own = pl.BlockSpec(_shard_shape(kind, pad, other),
                       (lambda i: (0, _my_index())) if kind == "col" else (lambda i: (_my_index(), 0)))
    return pl.pallas_call(
        body, name=name, grid=(1,),
        in_specs=[whole(w), whole(m), whole(v), whole(land), own],
        out_specs=[whole(w)] * 4, out_shape=[_sds(shape)] * 4,
        compiler_params=_params("arbitrary"),
    )(w, m, v, land, dw_full)


_GAINS = ("ffn1_pre", "ffn1_post", "mix_pre", "mix_post", "ffn2_pre", "ffn2_post", "ple_post")
_SMALL_ROWS = 16


def _stack_gains(get):
    return jnp.concatenate([get(n) for n in _GAINS]
                           + [jnp.concatenate([get("out_sb"), get("out_ch")], axis=1)], axis=0)


def kernel(x, p, g_ffn1_pre, g_ffn1_post, w_ffn1_gate, w_ffn1_up, w_ffn1_down, g_mix_pre, g_mix_post, w_in, g_out_sb, g_out_ch, rel_bias, w_out, g_ffn2_pre, g_ffn2_post, w_ffn2_gate, w_ffn2_up, w_ffn2_down, w_ple_proj, w_ple_gate, g_ple_post, loss_target, m_g_ffn1_pre, m_g_ffn1_post, m_w_ffn1_gate, m_w_ffn1_up, m_w_ffn1_down, m_g_mix_pre, m_g_mix_post, m_w_in, m_g_out_sb, m_g_out_ch, m_rel_bias, m_w_out, m_g_ffn2_pre, m_g_ffn2_post, m_w_ffn2_gate, m_w_ffn2_up, m_w_ffn2_down, m_w_ple_proj, m_w_ple_gate, m_g_ple_post, v_g_ffn1_pre, v_g_ffn1_post, v_w_ffn1_gate, v_w_ffn1_up, v_w_ffn1_down, v_g_mix_pre, v_g_mix_post, v_w_in, v_g_out_sb, v_g_out_ch, v_rel_bias, v_w_out, v_g_ffn2_pre, v_g_ffn2_post, v_w_ffn2_gate, v_w_ffn2_up, v_w_ffn2_down, v_w_ple_proj, v_w_ple_gate, v_g_ple_post):
    given = dict(locals())
    wnames = [n for n, *_ in _WEIGHTS]

    def shard(prefix, n):
        a = given[prefix + "w_" + n][0]
        return a.T if n in _TRANSPOSED else a

    packed, full = _pack_weights([shard("", n) for n in wnames])
    first = _GATHER_STAGES[0]
    anchor = x[0]
    gathers = {0: _gather_start(0, first, packed, full, anchor, peers=_NEAR_PEERS)}

    def weights_for(stage, after):
        names = _GATHER_STAGES[stage]
        if stage == 0:
            near = _gather_wait(0, names, gathers[0], after, count=len(_NEAR_PEERS))
            relay = _relay_start(0, names, near, near[names[0]])
            ws = _gather_wait("0r", names, relay, relay[-1], count=len(_FAR_CHIPS))
        else:
            ws = _gather_wait(stage, names, gathers[stage], after)
        if stage + 1 == len(_GATHER_STAGES):
            return ws, jnp.zeros((1, 1), F32)
        gathers[stage + 1] = _gather_start(stage + 1, _GATHER_STAGES[stage + 1], packed, full,
                                           ws[names[0]])
        return ws, gathers[stage + 1][-1][:1, :1]

    scatters = {}

    last = len(_SCATTER_STAGES) - 1

    def grads_done(stage, grads):
        names = _SCATTER_STAGES[stage]
        start = _pair_start if stage == last else _scatter_start
        scatters[stage] = start(stage, names, [grads[n] for n in names], anchor)
        return scatters[stage][-1][:1, :1]

    gains = {n: given["g_" + n] for n in _GAINS + ("out_sb", "out_ch")}
    fvec = _rel_bias_to_fvec(rel_bias[0])
    loss, dx, dg, dfvec = _local_step(x[0], p[0, 0], loss_target[0], gains,
                                      weights_for, grads_done, fvec)

    results = {}

    def finish(stage, after):
        names = _SCATTER_STAGES[stage]
        dws, lands = _scatter_wait(stage, names, scatters[stage], after)
        for n in names:
            kind, _, pad, _ = _SPEC[n]
            out = _adamw_shard(shard("", n), shard("m_", n), shard("v_", n), lands[n], dws[n],
                               kind=kind, pad=pad, name="adamw_" + n)
            results["w_" + n] = [a.T for a in out] if n in _TRANSPOSED else out
        return results["w_" + names[-1]][0]

    names = _SCATTER_STAGES[last]
    whole = lambda w, ref: ref
    send, recv, src, land, _ = scatters[last]
    out = _split_wait(f"pair_wait{last}", len(names), whole, send, recv, src, land, dx,
                      keep_sources=True)
    sums = [_pair_sum(dwf, pair, pad=_SPEC[n][2], name="pair_sum_" + n)
            for n, dwf, pair in zip(names, out[:len(names)], out[len(names):])]
    send, recv, src, land, after = _chip_start(last, names, sums, anchor)
    for stage in range(last):
        after = finish(stage, after)
    out = _split_wait(f"chip_wait{last}", len(names), whole, send, recv, src, land, after,
                      keep_sources=True)
    for n, own, landed in zip(names, out[:len(names)], out[len(names):]):
        res = _adamw_chip(shard("", n), shard("m_", n), shard("v_", n), landed, own,
                          name="adamw_" + n)
        results["w_" + n] = [a.T for a in res] if n in _TRANSPOSED else res
        after = res[0]
    loss_col = jnp.pad(loss[:, :1], ((0, N_DEV - 1), (0, D_MODEL - CH_WIN - 1)))
    dfv = jnp.concatenate([dfvec[:, 0, :], loss_col], axis=1)
    small = _allreduce_small(jnp.concatenate([_stack_gains(lambda n: dg[n]), dfv], axis=0), after)
    gain_names = _GAINS + ("out_sb", "out_ch")
    gain_out = _adamw_gains(small, [(given["g_" + n], given["m_g_" + n], given["v_g_" + n])
                                    for n in gain_names])
    for r, n in enumerate(gain_names):
        results["g_" + n] = gain_out[4 * r:4 * r + 4]
    d_rel = _fvec_grad_to_rel_bias(small[N_DEV:, :CH_WIN].reshape(N_DEV, 1, CH_WIN))
    results["rel_bias"] = _adamw(rel_bias[0], m_rel_bias[0], v_rel_bias[0], d_rel,
                                 name="adamw_rel_bias")

    order = ("g_ffn1_pre", "g_ffn1_post", "w_ffn1_gate", "w_ffn1_up", "w_ffn1_down",
             "g_mix_pre", "g_mix_post", "w_in", "g_out_sb", "g_out_ch", "rel_bias", "w_out",
             "g_ffn2_pre", "g_ffn2_post", "w_ffn2_gate", "w_ffn2_up", "w_ffn2_down",
             "w_ple_proj", "w_ple_gate", "g_ple_post")

    def leaf(name, idx):
        a = results[name][idx]
        return a if name.startswith("g_") else a[None]

    total_loss = small[N_DEV, CH_WIN]
    return (total_loss, dx[None],
            *[leaf(n, 0) for n in order], *[leaf(n, 1) for n in order],
            *[leaf(n, 2) for n in order], *[leaf(n, 3) for n in order])
```

```python
import jax
import jax.numpy as jnp
from jax import lax
from jax.experimental import pallas as pl
from jax.experimental.pallas import tpu as pltpu

F32 = jnp.float32
BF16 = jnp.bfloat16

N_DEV = 8
D_MODEL = 1024
D_FF = 2816
FF_SHARD = D_FF // N_DEV
FF_SHARD_PAD = 384
D_FF_PAD = FF_SHARD_PAD * N_DEV
QKV_WIDTH = 3 * D_MODEL
QKV_SHARD = QKV_WIDTH // N_DEV
PLE_DIM = 256
ROW_SHARD = D_MODEL // N_DEV
HEAD_DIM = 64
PAIR = 2 * HEAD_DIM
N_PAIRS = 4
CHUNK = 64
LOOKBACK = 8
REL_CLIP = 128
N_REL = 2 * REL_CLIP + 1
CH_QB = 256
CH_LOOK = LOOKBACK * CHUNK
CH_WIN = CH_LOOK + CH_QB
SB_BLK = 256
SB_GROUP = 2
SB_LANES = tuple(slice(g * 128, (g + 1) * 128) for g in range(SB_GROUP))
EPS = 1e-6
NEG_INF = -1e30
ATT_SCALE = HEAD_DIM ** -0.5
ADAM_LR = 0.001
ADAM_B1 = 0.9
ADAM_B2 = 0.999
ADAM_EPS = 1e-08
ADAM_WD = 0.01
ADAM_STEP = 10
VMEM_LIMIT_BYTES = 48 * 1024 * 1024
MESH = pl.DeviceIdType.MESH

ANY = pl.BlockSpec(memory_space=pl.ANY)
VMEM = pl.BlockSpec(memory_space=pltpu.VMEM)


def _params(*sem):
    return pltpu.CompilerParams(dimension_semantics=sem or None,
                                vmem_limit_bytes=VMEM_LIMIT_BYTES)


def _sds(shape, dtype=F32):
    return jax.ShapeDtypeStruct(shape, dtype)


def _bf(x):
    return x.astype(BF16)


def _dot(a, b):
    return jnp.dot(_bf(a), _bf(b), preferred_element_type=F32)


def _dot_nt(a, b):
    return lax.dot_general(_bf(a), _bf(b), (((1,), (1,)), ((), ())),
                           preferred_element_type=F32)


def _dot_tn(a, b):
    return lax.dot_general(_bf(a), _bf(b), (((0,), (0,)), ((), ())),
                           preferred_element_type=F32)


def _sigmoid(x):
    return 1.0 / (1.0 + jnp.exp(-x))


def _softplus(x):
    return jnp.maximum(x, 0.0) + jnp.log(1.0 + jnp.exp(-jnp.abs(x)))


def _rstd(x):
    return lax.rsqrt(jnp.mean(x * x, axis=-1, keepdims=True) + EPS)


def _rms(x, g):
    return x * _rstd(x) * g


def _rms_bwd(dy, x, g):
    r = _rstd(x)
    w = dy * g
    dx = r * (w - x * (r * r) * jnp.mean(w * x, axis=-1, keepdims=True))
    dg = jnp.sum(dy * (x * r), axis=0, keepdims=True)
    return dx, dg


def _head_masks():
    lane = lax.broadcasted_iota(jnp.int32, (1, PAIR), 1)
    return lane < HEAD_DIM, lane >= HEAD_DIM


def _ffn_fwd(x, g_pre, g_post, wg, wu, wd, *, name):
    t = x.shape[0]
    tm, tj = 512, 1024
    ni, nj = t // tm, D_FF_PAD // tj

    def body(x_ref, gpre_ref, gpost_ref, wg_ref, wu_ref, wd_ref,
             h_ref, n_ref, a_ref, b_ref, f_ref, acc_ref):
        j = pl.program_id(1)

        @pl.when(j == 0)
        def _():
            n_ref[...] = _bf(_rms(x_ref[...], gpre_ref[...]))
            acc_ref[...] = jnp.zeros_like(acc_ref)

        n = n_ref[...]
        a = _dot_nt(n, wg_ref[...])
        b = _dot_nt(n, wu_ref[...])
        a_ref[...] = a
        b_ref[...] = b
        hmid = a * _sigmoid(a) * b
        acc_ref[...] += jnp.dot(_bf(hmid), wd_ref[...], preferred_element_type=F32)

        @pl.when(j == nj - 1)
        def _():
            f = acc_ref[...]
            f_ref[...] = f
            h_ref[...] = x_ref[...] + 0.5 * _rms(f, gpost_ref[...])

    row = pl.BlockSpec((tm, D_MODEL), lambda i, j: (i, 0))
    gain = pl.BlockSpec((1, D_MODEL), lambda i, j: (0, 0))
    col = pl.BlockSpec((tm, tj), lambda i, j: (i, j))
    wtile = pl.BlockSpec((tj, D_MODEL), lambda i, j: (j, 0))
    return pl.pallas_call(
        body, name=name, grid=(ni, nj),
        in_specs=[row, gain, gain, wtile, wtile, wtile],
        out_specs=[row, row, col, col, row],
        out_shape=[_sds((t, D_MODEL)), _sds((t, D_MODEL), BF16),
                   _sds((t, D_FF_PAD)), _sds((t, D_FF_PAD)), _sds((t, D_MODEL))],
        scratch_shapes=[pltpu.VMEM((tm, D_MODEL), F32)],
        compiler_params=_params("arbitrary", "arbitrary"),
    )(x, g_pre, g_post, wg, wu, wd)


def _ffn_bwd(n, df, a, b, wg, wu, wd, *, name):
    t = n.shape[0]
    tj, tm, ts = 256, t, 512
    nj, ni, ns = D_FF_PAD // tj, t // tm, tm // ts

    def body(n_hbm, df_hbm, a_ref, b_ref, wg_ref, wu_ref, wd_ref,
             dwg_ref, dwu_ref, dwd_ref, dn_hbm,
             n_v, df_v, dn_v, ag, au, ad, sem):
        j, i = pl.program_id(0), pl.program_id(1)

        @pl.when((j == 0) & (i == 0))
        def _():
            c1 = pltpu.make_async_copy(n_hbm, n_v, sem.at[0])
            c2 = pltpu.make_async_copy(df_hbm, df_v, sem.at[1])
            c1.start()
            c2.start()
            dn_v[...] = jnp.zeros_like(dn_v)
            c1.wait()
            c2.wait()

        @pl.when(i == 0)
        def _():
            ag[...] = jnp.zeros_like(ag)
            au[...] = jnp.zeros_like(au)
            ad[...] = jnp.zeros_like(ad)

        wgj, wuj, wdj = wg_ref[...], wu_ref[...], wd_ref[...]
        for s in range(ns):
            local = pl.ds(s * ts, ts)
            rows = pl.ds(pl.multiple_of(i * tm + s * ts, ts), ts)
            av, bv = a_ref[local, :], b_ref[local, :]
            sig = _sigmoid(av)
            silu = av * sig
            dfr = df_v[rows, :]
            nr = n_v[rows, :]
            dhmid = _dot_nt(dfr, wdj)
            da = dhmid * bv * (sig * (1.0 + av * (1.0 - sig)))
            db = dhmid * silu
            ad[...] += _dot_tn(silu * bv, dfr)
            ag[...] += _dot_tn(da, nr)
            au[...] += _dot_tn(db, nr)
            dn_v[rows, :] += _dot(da, wgj) + _dot(db, wuj)

        @pl.when(i == ni - 1)
        def _():
            dwg_ref[...] = _bf(ag[...])
            dwu_ref[...] = _bf(au[...])
            dwd_ref[...] = _bf(ad[...])

        @pl.when((j == nj - 1) & (i == ni - 1))
        def _():
            c = pltpu.make_async_copy(dn_v, dn_hbm, sem.at[0])
            c.start()
            c.wait()

    roww = pl.BlockSpec((tj, D_MODEL), lambda j, i: (j, 0))
    act = pl.BlockSpec((tm, tj), lambda j, i: (i, j))
    return pl.pallas_call(
        body, name=name, grid=(nj, ni),
        in_specs=[ANY, ANY, act, act, roww, roww, roww],
        out_specs=[roww, roww, roww, ANY],
        out_shape=[_sds((D_FF_PAD, D_MODEL), BF16)] * 3 + [_sds((t, D_MODEL))],
        scratch_shapes=[pltpu.VMEM((t, D_MODEL), BF16), pltpu.VMEM((t, D_MODEL), BF16),
                        pltpu.VMEM((t, D_MODEL), F32)]
        + [pltpu.VMEM((tj, D_MODEL), F32)] * 3 + [pltpu.SemaphoreType.DMA((2,))],
        compiler_params=_params("arbitrary", "arbitrary"),
    )(n, df, a, b, wg, wu, wd)


def _junction(dres, pre=None, post=None, *, name):
    t = dres.shape[0]
    tm = 512
    ni = t // tm
    n_in = 1 + (3 if pre else 0) + (2 if post else 0)
    coef = post[2] if post else None

    def body(*refs):
        ins, outs = list(refs[:n_in]), list(refs[n_in:])
        i = pl.program_id(0)
        dh = ins.pop(0)[...]
        if pre:
            dn_ref, x_ref, gpre_ref = ins.pop(0), ins.pop(0), ins.pop(0)
            dh_ref, dgpre_ref = outs.pop(0), outs.pop(0)
            dx, dg = _rms_bwd(dn_ref[...], x_ref[...], gpre_ref[...])
            dh = dh + dx
            dh_ref[...] = dh

            @pl.when(i == 0)
            def _():
                dgpre_ref[...] = jnp.zeros_like(dgpre_ref)
            dgpre_ref[...] += dg
        if post:
            f_ref, gpost_ref = ins.pop(0), ins.pop(0)
            df_ref, dgpost_ref = outs.pop(0), outs.pop(0)
            df, dg = _rms_bwd(coef * dh, f_ref[...], gpost_ref[...])
            df_ref[...] = _bf(df)

            @pl.when(i == 0)
            def _():
                dgpost_ref[...] = jnp.zeros_like(dgpost_ref)
            dgpost_ref[...] += dg

    row = pl.BlockSpec((tm, D_MODEL), lambda i: (i, 0))
    gain = pl.BlockSpec((1, D_MODEL), lambda i: (0, 0))
    args, in_specs, out_specs, out_shape = [dres], [row], [], []
    if pre:
        args += list(pre)
        in_specs += [row, row, gain]
        out_specs += [row, gain]
        out_shape += [_sds((t, D_MODEL)), _sds((1, D_MODEL))]
    if post:
        args += [post[0], post[1]]
        in_specs += [row, gain]
        out_specs += [row, gain]
        out_shape += [_sds((t, D_MODEL), BF16), _sds((1, D_MODEL))]
    return pl.pallas_call(
        body, name=name, grid=(ni,), in_specs=in_specs, out_specs=out_specs,
        out_shape=out_shape, compiler_params=_params("arbitrary"),
    )(*args)


def _qkv_fwd(h, g, win, *, name):
    t = h.shape[0]
    tm, tn = min(1024, t), 1024
    ni, nj = t // tm, QKV_WIDTH // tn

    def body(h_ref, g_ref, w_ref, qkv_ref, u_ref):
        @pl.when(pl.program_id(1) == 0)
        def _():
            u_ref[...] = _bf(_rms(h_ref[...], g_ref[...]))
        qkv_ref[...] = jnp.dot(u_ref[...], w_ref[...], preferred_element_type=F32)

    row = pl.BlockSpec((tm, D_MODEL), lambda i, j: (i, 0))
    return pl.pallas_call(
        body, name=name, grid=(ni, nj),
        in_specs=[row, pl.BlockSpec((1, D_MODEL), lambda i, j: (0, 0)),
                  pl.BlockSpec((D_MODEL, tn), lambda i, j: (0, j))],
        out_specs=[pl.BlockSpec((tm, tn), lambda i, j: (i, j)), row],
        out_shape=[_sds((t, QKV_WIDTH)), _sds((t, D_MODEL), BF16)],
        compiler_params=_params("arbitrary", "arbitrary"),
    )(h, g, win)


def _qkv_bwd(dq, dk, dv, u, win, *, name):
    t = u.shape[0]
    tn, ts = 512, 512
    nj, ns = QKV_WIDTH // tn, t // ts

    def body(dq_ref, dk_ref, dv_ref, u_ref, w_ref, dw_ref, du_hbm, du_v, acc_ref, sem):
        j = pl.program_id(0)

        @pl.when(j == 0)
        def _():
            du_v[...] = jnp.zeros_like(du_v)

        wj = w_ref[...]
        for role, d_ref in enumerate((dq_ref, dk_ref, dv_ref)):
            @pl.when(j % 3 == role)
            def _():
                acc_ref[...] = jnp.zeros_like(acc_ref)
                for s in range(ns):
                    rows = pl.ds(s * ts, ts)
                    dcol = d_ref[rows, :]
                    acc_ref[...] += _dot_tn(u_ref[rows, :], dcol)
                    du_v[rows, :] += _dot_nt(dcol, wj)
                dw_ref[...] = _bf(acc_ref[...])

        @pl.when(j == nj - 1)
        def _():
            c = pltpu.make_async_copy(du_v, du_hbm, sem)
            c.start()
            c.wait()

    colw = pl.BlockSpec((D_MODEL, tn), lambda j: (0, j))
    grp = pl.BlockSpec((t, tn), lambda j: (0, j // 3))
    return pl.pallas_call(
        body, name=name, grid=(nj,),
        in_specs=[grp, grp, grp, pl.BlockSpec((t, D_MODEL), lambda j: (0, 0)), colw],
        out_specs=[colw, ANY],
        out_shape=[_sds((D_MODEL, QKV_WIDTH), BF16), _sds((t, D_MODEL))],
        scratch_shapes=[pltpu.VMEM((t, D_MODEL), F32), pltpu.VMEM((D_MODEL, tn), F32),
                        pltpu.SemaphoreType.DMA],
        compiler_params=_params("arbitrary"),
    )(dq, dk, dv, u, win)


def _sb_stack(x):
    lo, hi = _head_masks()
    return jnp.concatenate([jnp.where(lo, x, 0.0), jnp.where(hi, x, 0.0)], axis=0)


def _sb_unstack(x2, blk):
    return jnp.where(_head_masks()[0], x2[:blk], x2[blk:])


def _sb_diag_mask(blk):
    r = lax.broadcasted_iota(jnp.int32, (2 * blk, blk), 0) & (blk - 1)
    c = lax.broadcasted_iota(jnp.int32, (2 * blk, blk), 1)
    return c < r


def _tri(n, keep):
    r = lax.broadcasted_iota(jnp.int32, (n, n), 0)
    c = lax.broadcasted_iota(jnp.int32, (n, n), 1)
    return jnp.where(keep(r, c), 1.0, 0.0).astype(BF16)


def _cumsum01(x, u):
    m = x.shape[0]
    hi = _bf(x)
    lo = _bf(x - hi.astype(F32))
    both = jnp.dot(jnp.concatenate([hi, lo], axis=0), u, preferred_element_type=F32)
    return both[:m] + both[m:]


def _sb_fwd(qkv, *, name):
    t = qkv.shape[0]
    blk = SB_BLK
    ni = t // blk

    def body(q_ref, k_ref, v_ref, o_ref, ltot_ref):
        i = pl.program_id(1)
        u_after = _tri(blk, lambda r, c: r > c)
        q2 = [_bf(_sb_stack(q_ref[:, lanes] * ATT_SCALE)) for lanes in SB_LANES]

        def tile(g, k0, mask, acc, c_l):
            kj = k_ref[pl.ds(k0, blk), SB_LANES[g]]
            vj = v_ref[pl.ds(k0, blk), SB_LANES[g]]
            z = _dot_nt(q2[g], kj)
            sp = _softplus(z)
            lf = -sp if mask is None else jnp.where(mask, -sp, 0.0)
            a = jnp.exp(z - sp + _cumsum01(lf, u_after) + c_l)
            if mask is not None:
                a = jnp.where(mask, a, 0.0)
            return acc + _dot(a, vj), c_l + jnp.sum(lf, axis=1, keepdims=True)

        def tiles(k0, mask, carry):
            return tuple(tile(g, k0, mask, *carry[g]) for g in range(SB_GROUP))

        zero = (jnp.zeros((2 * blk, PAIR), F32), jnp.zeros((2 * blk, 1), F32))
        carry = tiles(pl.multiple_of(i * blk, blk), _sb_diag_mask(blk), (zero,) * SB_GROUP)
        carry = lax.fori_loop(
            1, i + 1, lambda jj, c: tiles(pl.multiple_of((i - jj) * blk, blk), None, c), carry)
        for g, (acc, c_l) in enumerate(carry):
            o_ref[:, SB_LANES[g]] = _sb_unstack(acc, blk)
            ltot_ref[:, SB_LANES[g]] = _sb_unstack(jnp.broadcast_to(c_l, (2 * blk, PAIR)), blk)

    width = SB_GROUP * PAIR
    blkspec = pl.BlockSpec((blk, width), lambda p, i: (i, p))
    n_steps = N_PAIRS // SB_GROUP
    return pl.pallas_call(
        body, name=name, grid=(n_steps, ni),
        in_specs=[blkspec,
                  pl.BlockSpec((t, width), lambda p, i: (0, n_steps + p)),
                  pl.BlockSpec((t, width), lambda p, i: (0, 2 * n_steps + p))],
        out_specs=[blkspec, blkspec],
        out_shape=[_sds((t, D_MODEL)), _sds((t, D_MODEL // 2))],
        compiler_params=_params("arbitrary", "arbitrary"),
    )(qkv, qkv, qkv)


def _sb_bwd(qkv, ltot, do, *, name):
    t = qkv.shape[0]
    blk = SB_BLK
    ni = t // blk

    def body(q_ref, k_ref, v_ref, lt_ref, do_ref, dq_ref, dkout_ref, dvout_ref, dk_ref, dv_ref):
        i = pl.program_id(1)

        @pl.when(i == 0)
        def _():
            dk_ref[...] = jnp.zeros_like(dk_ref)
            dv_ref[...] = jnp.zeros_like(dv_ref)

        u_upto = _tri(blk, lambda r, c: r <= c)
        u_before = _tri(blk, lambda r, c: r < c)
        lane = lax.broadcasted_iota(jnp.int32, (1, PAIR), 1)
        q2 = [_bf(_sb_stack(q_ref[:, lanes] * ATT_SCALE)) for lanes in SB_LANES]
        do2 = [_bf(_sb_stack(do_ref[:, lanes])) for lanes in SB_LANES]
        total = [jnp.concatenate(
            [jnp.sum(jnp.where(lane == h * HEAD_DIM, lt_ref[:, lanes], 0.0), axis=1, keepdims=True)
             for h in range(2)], axis=0) for lanes in SB_LANES]

        def tile(g, k0, mask, dq_acc, c_l, c_g):
            krows = pl.ds(k0, blk)
            kj = k_ref[krows, SB_LANES[g]]
            vj = v_ref[krows, SB_LANES[g]]
            z = _dot_nt(q2[g], kj)
            sp = _softplus(z)
            sig = jnp.exp(z - sp)
            lf = -sp if mask is None else jnp.where(mask, -sp, 0.0)
            a = jnp.exp(z - sp + total[g] - (_cumsum01(lf, u_upto) + c_l))
            if mask is not None:
                a = jnp.where(mask, a, 0.0)
            gw = a * _dot_nt(do2[g], vj)
            g_before = jnp.dot(_bf(gw), u_before, preferred_element_type=F32) + c_g
            dz = gw * (1.0 - sig) - g_before * sig
            if mask is not None:
                dz = jnp.where(mask, dz, 0.0)
            dk_ref[krows, SB_LANES[g]] += _dot_tn(dz, q2[g])
            dv_ref[krows, SB_LANES[g]] += _dot_tn(a, do2[g])
            return (dq_acc + _dot(dz, kj), c_l + jnp.sum(lf, axis=1, keepdims=True),
                    c_g + jnp.sum(gw, axis=1, keepdims=True))

        def tiles(k0, mask, carry):
            return tuple(tile(g, k0, mask, *carry[g]) for g in range(SB_GROUP))

        zero = (jnp.zeros((2 * blk, PAIR), F32), jnp.zeros((2 * blk, 1), F32),
                jnp.zeros((2 * blk, 1), F32))
        carry = lax.fori_loop(
            0, i, lambda j, c: tiles(pl.multiple_of(j * blk, blk), None, c), (zero,) * SB_GROUP)
        carry = tiles(pl.multiple_of(i * blk, blk), _sb_diag_mask(blk), carry)
        for g, (dq_acc, _, _) in enumerate(carry):
            dq_ref[:, SB_LANES[g]] = _bf(_sb_unstack(dq_acc, blk) * ATT_SCALE)

        @pl.when(i == ni - 1)
        def _():
            dkout_ref[...] = _bf(dk_ref[...])
            dvout_ref[...] = _bf(dv_ref[...])

    width = SB_GROUP * PAIR
    n_steps = N_PAIRS // SB_GROUP
    blkspec = lambda off: pl.BlockSpec((blk, width), lambda p, i: (i, off + p))
    full = lambda off: pl.BlockSpec((t, width), lambda p, i: (0, off + p))
    return pl.pallas_call(
        body, name=name, grid=(n_steps, ni),
        in_specs=[blkspec(0), full(n_steps), full(2 * n_steps), blkspec(0), blkspec(0)],
        out_specs=[blkspec(0), full(0), full(0)],
        out_shape=[_sds((t, D_MODEL), BF16)] * 3,
        scratch_shapes=[pltpu.VMEM((t, width), F32), pltpu.VMEM((t, width), F32)],
        compiler_params=_params("arbitrary", "arbitrary"),
    )(qkv, qkv, qkv, ltot, do)


def _ch_mask(i):
    r = lax.broadcasted_iota(jnp.int32, (CH_QB, CH_WIN), 0)
    c = lax.broadcasted_iota(jnp.int32, (CH_QB, CH_WIN), 1)
    qc = LOOKBACK + lax.shift_right_arithmetic(r, 6)
    kc = lax.shift_right_arithmetic(c, 6)
    first = i * (CH_QB // CHUNK) - LOOKBACK
    return (kc <= qc) & (kc >= qc - LOOKBACK) & (kc + first >= 0)


def _ch_probs(qm, kw, bias_h, mask):
    z = _dot_nt(qm, kw) * ATT_SCALE + bias_h
    z = jnp.where(mask, z, NEG_INF)
    e = jnp.exp(z - jnp.max(z, axis=1, keepdims=True))
    return e / jnp.sum(e, axis=1, keepdims=True)


def _ch_fill(pad_ref, src_ref, t):
    pad_ref[pl.ds(0, CH_LOOK), :] = jnp.zeros((CH_LOOK, PAIR), BF16)
    pad_ref[pl.ds(CH_LOOK, t), :] = _bf(src_ref[...])


def _ch_fwd(qkv, bias, o_in, *, name):
    t = qkv.shape[0]
    ni = t // CH_QB

    def body(q_ref, k_ref, v_ref, bias_ref, _alias, o_ref, kpad, vpad):
        i = pl.program_id(1)

        @pl.when(i == 0)
        def _():
            _ch_fill(kpad, k_ref, t)
            _ch_fill(vpad, v_ref, t)

        win = pl.ds(pl.multiple_of(i * CH_QB, CH_QB), CH_WIN)
        kw, vw = kpad[win, :], vpad[win, :]
        mask = _ch_mask(i)
        q = q_ref[...]
        outs = []
        for h, hm in enumerate(_head_masks()):
            p = _ch_probs(jnp.where(hm, q, 0.0), kw, bias_ref[h], mask)
            outs.append(_dot(p, vw))
        o_ref[...] = jnp.where(_head_masks()[0], outs[0], outs[1])

    full = lambda off: pl.BlockSpec((t, PAIR), lambda p, i: (0, off + p))
    return pl.pallas_call(
        body, name=name, grid=(N_PAIRS, ni),
        in_specs=[pl.BlockSpec((CH_QB, PAIR), lambda p, i: (i, 3 * N_PAIRS + p)),
                  full(4 * N_PAIRS), full(5 * N_PAIRS),
                  pl.BlockSpec((2, CH_QB, CH_WIN), lambda p, i: (p, 0, 0)), ANY],
        out_specs=pl.BlockSpec((CH_QB, PAIR), lambda p, i: (i, N_PAIRS + p)),
        out_shape=_sds((t, D_MODEL)),
        scratch_shapes=[pltpu.VMEM((t + CH_LOOK, PAIR), BF16)] * 2,
        input_output_aliases={4: 0},
        compiler_params=_params("arbitrary", "arbitrary"),
    )(qkv, qkv, qkv, bias, o_in)


def _ch_bwd(qkv, bias, o, do, dq_in, dk_in, dv_in, *, name):
    t = qkv.shape[0]
    ni = t // CH_QB

    def body(q_ref, k_ref, v_ref, bias_ref, o_ref, do_ref, _a0, _a1, _a2,
             dq_ref, dkout_ref, dvout_ref, dbias_ref, kpad, vpad, dkpad, dvpad):
        i = pl.program_id(1)

        @pl.when(i == 0)
        def _():
            _ch_fill(kpad, k_ref, t)
            _ch_fill(vpad, v_ref, t)
            dkpad[...] = jnp.zeros_like(dkpad)
            dvpad[...] = jnp.zeros_like(dvpad)
            dbias_ref[...] = jnp.zeros_like(dbias_ref)

        win = pl.ds(pl.multiple_of(i * CH_QB, CH_QB), CH_WIN)
        kw, vw = kpad[win, :], vpad[win, :]
        mask = _ch_mask(i)
        q, o_blk, do_blk = q_ref[...], o_ref[...], do_ref[...]
        dqs = []
        for h, hm in enumerate(_head_masks()):
            qm = _bf(jnp.where(hm, q, 0.0))
            dom = jnp.where(hm, do_blk, 0.0)
            delta = jnp.sum(dom * o_blk, axis=1, keepdims=True)
            dom = _bf(dom)
            p = _ch_probs(qm, kw, bias_ref[h], mask)
            ds = p * (_dot_nt(dom, vw) - delta)
            dbias_ref[h] += ds
            dsz = ds * ATT_SCALE
            dqs.append(_dot(dsz, kw))
            dkpad[win, :] += _dot_tn(dsz, qm)
            dvpad[win, :] += _dot_tn(p, dom)
        dq_ref[...] = _bf(jnp.where(_head_masks()[0], dqs[0], dqs[1]))

        @pl.when(i == ni - 1)
        def _():
            dkout_ref[...] = _bf(dkpad[pl.ds(CH_LOOK, t), :])
            dvout_ref[...] = _bf(dvpad[pl.ds(CH_LOOK, t), :])

    blkspec = lambda off: pl.BlockSpec((CH_QB, PAIR), lambda p, i: (i, off + p))
    full = lambda off: pl.BlockSpec((t, PAIR), lambda p, i: (0, off + p))
    bias_spec = pl.BlockSpec((2, CH_QB, CH_WIN), lambda p, i: (p, 0, 0))
    return pl.pallas_call(
        body, name=name, grid=(N_PAIRS, ni),
        in_specs=[blkspec(3 * N_PAIRS), full(4 * N_PAIRS), full(5 * N_PAIRS), bias_spec,
                  blkspec(N_PAIRS), blkspec(N_PAIRS), ANY, ANY, ANY],
        out_specs=[blkspec(N_PAIRS), full(N_PAIRS), full(N_PAIRS), bias_spec],
        out_shape=[_sds((t, D_MODEL), BF16)] * 3 + [_sds((2 * N_PAIRS, CH_QB, CH_WIN))],
        scratch_shapes=[pltpu.VMEM((t + CH_LOOK, PAIR), BF16)] * 2
        + [pltpu.VMEM((t + CH_LOOK, PAIR), F32)] * 2,
        input_output_aliases={6: 0, 7: 1, 8: 2},
        compiler_params=_params("arbitrary", "arbitrary"),
    )(qkv, qkv, qkv, bias, o, do, dq_in, dk_in, dv_in)


def _bias_expand(fvec, *, name):
    n_heads = fvec.shape[0]

    def body(f_ref, o_ref, rows8):
        row = f_ref[0]
        for r in range(8):
            rows8[pl.ds(r, 1), :] = pltpu.roll(row, r, 1)
        base = rows8[...]
        for blk in range(CH_QB // 8):
            o_ref[0, pl.ds(8 * blk, 8), :] = pltpu.roll(base, 8 * blk, 1)

    return pl.pallas_call(
        body, name=name, grid=(n_heads,),
        in_specs=[pl.BlockSpec((1, 1, CH_WIN), lambda h: (h, 0, 0))],
        out_specs=pl.BlockSpec((1, CH_QB, CH_WIN), lambda h: (h, 0, 0)),
        out_shape=_sds((n_heads, CH_QB, CH_WIN)),
        scratch_shapes=[pltpu.VMEM((8, CH_WIN), F32)],
        compiler_params=_params("arbitrary"),
    )(fvec)


def _bias_grad(dbias, *, name):
    n_heads = dbias.shape[0]
    first = CH_LOOK - REL_CLIP

    def body(d_ref, o_ref, acc8):
        acc = jnp.zeros((8, CH_WIN), F32)
        for blk in range(CH_QB // 8):
            acc = acc + pltpu.roll(d_ref[0, pl.ds(8 * blk, 8), :], (CH_WIN - 8 * blk) % CH_WIN, 1)
        acc8[...] = acc
        dvec = jnp.zeros((1, CH_WIN), F32)
        for r in range(8):
            dvec = dvec + pltpu.roll(acc8[pl.ds(r, 1), :], (CH_WIN - r) % CH_WIN, 1)
        lane = lax.broadcasted_iota(jnp.int32, (1, CH_WIN), 1)
        clipped = (lane <= first) | (lane >= first + REL_CLIP + CHUNK)
        total = jnp.sum(jnp.where(clipped, dvec, 0.0), axis=1, keepdims=True)
        o_ref[0] = jnp.where(lane == first, total, dvec)

    return pl.pallas_call(
        body, name=name, grid=(n_heads,),
        in_specs=[pl.BlockSpec((1, CH_QB, CH_WIN), lambda h: (h, 0, 0))],
        out_specs=pl.BlockSpec((1, 1, CH_WIN), lambda h: (h, 0, 0)),
        out_shape=_sds((n_heads, 1, CH_WIN)),
        scratch_shapes=[pltpu.VMEM((8, CH_WIN), F32)],
        compiler_params=_params("arbitrary"),
    )(dbias)


def _out_fwd(o, h1, g_sb, g_ch, g_post, wout, *, name):
    t = o.shape[0]
    tm = 512
    half = D_MODEL // 2

    def body(o_ref, h_ref, gsb_ref, gch_ref, gpost_ref, w_ref, h2_ref, mixed_ref, y_ref):
        ov = o_ref[...]
        mixed = jnp.concatenate([_rms(ov[:, :half], gsb_ref[...]),
                                 _rms(ov[:, half:], gch_ref[...])], axis=1)
        mixed_ref[...] = _bf(mixed)
        y = _dot(mixed, w_ref[...])
        y_ref[...] = y
        h2_ref[...] = h_ref[...] + _rms(y, gpost_ref[...])

    row = pl.BlockSpec((tm, D_MODEL), lambda i: (i, 0))
    gain = lambda n: pl.BlockSpec((1, n), lambda i: (0, 0))
    return pl.pallas_call(
        body, name=name, grid=(t // tm,),
        in_specs=[row, row, gain(half), gain(half), gain(D_MODEL),
                  pl.BlockSpec((D_MODEL, D_MODEL), lambda i: (0, 0))],
        out_specs=[row, row, row],
        out_shape=[_sds((t, D_MODEL)), _sds((t, D_MODEL), BF16), _sds((t, D_MODEL))],
        compiler_params=_params("arbitrary"),
    )(o, h1, g_sb, g_ch, g_post, wout)


def _out_bwd(dy, mixed, o, g_sb, g_ch, wout, *, name):
    t = o.shape[0]
    tm = 512
    ni = t // tm
    half = D_MODEL // 2

    def body(dy_ref, mixed_ref, o_ref, gsb_ref, gch_ref, w_ref,
             dw_ref, do_ref, dgsb_ref, dgch_ref, acc_ref):
        i = pl.program_id(0)

        @pl.when(i == 0)
        def _():
            acc_ref[...] = jnp.zeros_like(acc_ref)
            dgsb_ref[...] = jnp.zeros_like(dgsb_ref)
            dgch_ref[...] = jnp.zeros_like(dgch_ref)

        dyv = dy_ref[...]
        acc_ref[...] += _dot_tn(mixed_ref[...], dyv)
        dm = _dot_nt(dyv, w_ref[...])
        ov = o_ref[...]
        doa, dga = _rms_bwd(dm[:, :half], ov[:, :half], gsb_ref[...])
        dob, dgb = _rms_bwd(dm[:, half:], ov[:, half:], gch_ref[...])
        do_ref[...] = jnp.concatenate([doa, dob], axis=1)
        dgsb_ref[...] += dga
        dgch_ref[...] += dgb

        @pl.when(i == ni - 1)
        def _():
            dw_ref[...] = _bf(acc_ref[...])

    row = pl.BlockSpec((tm, D_MODEL), lambda i: (i, 0))
    gain = pl.BlockSpec((1, half), lambda i: (0, 0))
    sq = pl.BlockSpec((D_MODEL, D_MODEL), lambda i: (0, 0))
    return pl.pallas_call(
        body, name=name, grid=(ni,),
        in_specs=[row, row, row, gain, gain, sq],
        out_specs=[sq, row, gain, gain],
        out_shape=[_sds((D_MODEL, D_MODEL), BF16), _sds((t, D_MODEL)),
                   _sds((1, half)), _sds((1, half))],
        scratch_shapes=[pltpu.VMEM((D_MODEL, D_MODEL), F32)],
        compiler_params=_params("arbitrary"),
    )(dy, mixed, o, g_sb, g_ch, wout)


def _ple(p, h3, target, wp, wgate, g, *, name):
    t = h3.shape[0]
    tm = 512
    ni = t // tm

    def body(p_ref, h_ref, tgt_ref, wp_ref, wg_ref, g_ref,
             loss_ref, dres_ref, dwp_ref, dwg_ref, dg_ref, accp, accg):
        i = pl.program_id(0)

        @pl.when(i == 0)
        def _():
            loss_ref[...] = jnp.zeros_like(loss_ref)
            dg_ref[...] = jnp.zeros_like(dg_ref)
            accp[...] = jnp.zeros_like(accp)
            accg[...] = jnp.zeros_like(accg)

        pv, hv, gv = p_ref[...], h_ref[...], g_ref[...]
        pe = _dot(pv, wp_ref[...])
        sig = _sigmoid(_dot(hv, wg_ref[...]))
        e = pe * sig
        err = hv + _rms(e, gv) - tgt_ref[...]
        tok = jnp.mean(err * err, axis=-1, keepdims=True)
        loss_ref[...] += 0.5 * jnp.sum(tok, axis=0, keepdims=True)
        dh4 = err * (1.0 / D_MODEL)
        de, dg = _rms_bwd(dh4, e, gv)
        dg_ref[...] += dg
        dpe = de * sig
        dgt = de * pe * sig * (1.0 - sig)
        accp[...] += _dot_tn(pv, dpe)
        accg[...] += _dot_tn(hv, dgt)
        dres_ref[...] = dh4 + _dot_nt(dgt, wg_ref[...])

        @pl.when(i == ni - 1)
        def _():
            dwp_ref[...] = _bf(accp[...])
            dwg_ref[...] = _bf(accg[...])

    row = pl.BlockSpec((tm, D_MODEL), lambda i: (i, 0))
    const = lambda r, c: pl.BlockSpec((r, c), lambda i: (0, 0))
    return pl.pallas_call(
        body, name=name, grid=(ni,),
        in_specs=[pl.BlockSpec((tm, PLE_DIM), lambda i: (i, 0)), row, row,
                  const(PLE_DIM, D_MODEL), const(D_MODEL, D_MODEL), const(1, D_MODEL)],
        out_specs=[const(1, 128), row, const(PLE_DIM, D_MODEL), const(D_MODEL, D_MODEL),
                   const(1, D_MODEL)],
        out_shape=[_sds((1, 128)), _sds((t, D_MODEL)), _sds((PLE_DIM, D_MODEL), BF16),
                   _sds((D_MODEL, D_MODEL), BF16), _sds((1, D_MODEL))],
        scratch_shapes=[pltpu.VMEM((PLE_DIM, D_MODEL), F32), pltpu.VMEM((D_MODEL, D_MODEL), F32)],
        compiler_params=_params("arbitrary"),
    )(p, h3, target, wp, wgate, g)


def _rel_bias_to_fvec(rel_bias):
    rev = rel_bias[:, ::-1]
    n_heads = rel_bias.shape[0]
    first = CH_LOOK - REL_CLIP
    n_var = REL_CLIP + CHUNK
    clipped = rev[:, :1]
    fvec = jnp.concatenate([jnp.broadcast_to(clipped, (n_heads, first)), rev[:, :n_var],
                            jnp.broadcast_to(clipped, (n_heads, CH_WIN - first - n_var))], axis=1)
    return fvec.reshape(n_heads, 1, CH_WIN)


def _fvec_grad_to_rel_bias(dfvec):
    first = CH_LOOK - REL_CLIP
    n_var = REL_CLIP + CHUNK
    rev = jnp.pad(dfvec[:, 0, first:first + n_var], ((0, 0), (0, N_REL - n_var)))
    return rev[:, ::-1]


def _local_step(x, p, target, g, weights_for, grads_done, fvec):
    w, tie = weights_for(0, x)
    w = dict(w)
    h1, n1, a1, b1, f1 = _ffn_fwd(x, g["ffn1_pre"] + tie, g["ffn1_post"],
                                  w["ffn1_gate"], w["ffn1_up"], w["ffn1_down"], name="ffn1_fwd")
    more, tie = weights_for(1, h1)
    w.update(more)
    qkv, u = _qkv_fwd(h1, g["mix_pre"] + tie, w["in"], name="qkv_fwd")
    bias = _bias_expand(fvec, name="bias_expand")
    o, ltot = _sb_fwd(qkv, name="sb_fwd")
    o = _ch_fwd(qkv, bias, o, name="ch_fwd")
    h2, mixed, y = _out_fwd(o, h1, g["out_sb"], g["out_ch"], g["mix_post"], w["out"], name="out_fwd")
    w.update(weights_for(2, h2)[0])
    h3, n2, a2, b2, f2 = _ffn_fwd(h2, g["ffn2_pre"], g["ffn2_post"],
                                  w["ffn2_gate"], w["ffn2_up"], w["ffn2_down"], name="ffn2_fwd")
    loss, dh3, dwp, dwgate, dg_ple = _ple(p, h3, target, w["ple_proj"], w["ple_gate"],
                                          g["ple_post"], name="ple")
    tie = grads_done(0, {"ple_proj": dwp, "ple_gate": dwgate})

    df2, dg_ffn2_post = _junction(dh3, post=(f2, g["ffn2_post"] + tie, 0.5), name="junction3")
    dwg2, dwu2, dwd2, dn2 = _ffn_bwd(n2, df2, a2, b2, w["ffn2_gate"], w["ffn2_up"],
                                     w["ffn2_down"], name="ffn2_bwd")
    tie = grads_done(1, {"ffn2_gate": dwg2, "ffn2_up": dwu2, "ffn2_down": dwd2})
    dh2, dg_ffn2_pre, dy, dg_mix_post = _junction(
        dh3, pre=(dn2, h2, g["ffn2_pre"] + tie), post=(y, g["mix_post"], 1.0), name="junction2")
    dwout, do, dg_sb, dg_ch = _out_bwd(dy, mixed, o, g["out_sb"], g["out_ch"], w["out"],
                                       name="out_bwd")
    dq, dk, dv = _sb_bwd(qkv, ltot, do, name="sb_bwd")
    dq, dk, dv, dbias = _ch_bwd(qkv, bias, o, do, dq, dk, dv, name="ch_bwd")
    dfvec = _bias_grad(dbias, name="bias_grad")
    dwin, du = _qkv_bwd(dq, dk, dv, u, w["in"], name="qkv_bwd")
    tie = grads_done(2, {"out": dwout, "in": dwin})
    dh1, dg_mix_pre, df1, dg_ffn1_post = _junction(
        dh2, pre=(du, h1, g["mix_pre"] + tie), post=(f1, g["ffn1_post"], 0.5), name="junction1")
    dwg1, dwu1, dwd1, dn1 = _ffn_bwd(n1, df1, a1, b1, w["ffn1_gate"], w["ffn1_up"],
                                     w["ffn1_down"], name="ffn1_bwd")
    tie = grads_done(3, {"ffn1_gate": dwg1, "ffn1_up": dwu1, "ffn1_down": dwd1})
    dx, dg_ffn1_pre = _junction(dh1, pre=(dn1, x, g["ffn1_pre"] + tie), name="junction0")

    dg = {"ffn1_pre": dg_ffn1_pre, "ffn1_post": dg_ffn1_post, "mix_pre": dg_mix_pre,
          "mix_post": dg_mix_post, "out_sb": dg_sb, "out_ch": dg_ch,
          "ffn2_pre": dg_ffn2_pre, "ffn2_post": dg_ffn2_post, "ple_post": dg_ple}
    return loss, dx, dg, dfvec


_WEIGHTS = (
    ("ffn1_gate", "row", FF_SHARD, FF_SHARD_PAD, D_MODEL),
    ("ffn1_up", "row", FF_SHARD, FF_SHARD_PAD, D_MODEL),
    ("ffn1_down", "row", FF_SHARD, FF_SHARD_PAD, D_MODEL),
    ("in", "col", QKV_SHARD, QKV_SHARD, D_MODEL),
    ("out", "row", ROW_SHARD, ROW_SHARD, D_MODEL),
    ("ffn2_gate", "row", FF_SHARD, FF_SHARD_PAD, D_MODEL),
    ("ffn2_up", "row", FF_SHARD, FF_SHARD_PAD, D_MODEL),
    ("ffn2_down", "row", FF_SHARD, FF_SHARD_PAD, D_MODEL),
    ("ple_proj", "col", ROW_SHARD, ROW_SHARD, PLE_DIM),
    ("ple_gate", "row", ROW_SHARD, ROW_SHARD, D_MODEL),
)
_TRANSPOSED = ("ffn1_gate", "ffn1_up", "ffn2_gate", "ffn2_up")
_SPEC = {n: (kind, valid, pad, other) for n, kind, valid, pad, other in _WEIGHTS}
_GATHER_STAGES = (("ffn1_gate", "ffn1_up", "ffn1_down"), ("in", "out"),
                  ("ffn2_gate", "ffn2_up", "ffn2_down", "ple_proj", "ple_gate"))
_SCATTER_STAGES = (("ple_proj", "ple_gate"), ("ffn2_gate", "ffn2_up", "ffn2_down"),
                   ("out", "in"), ("ffn1_gate", "ffn1_up", "ffn1_down"))
HBM = pl.BlockSpec(memory_space=pltpu.HBM)
SEM = pl.BlockSpec(memory_space=pltpu.SEMAPHORE)
EFFECT = pltpu.SideEffectType.DATAFLOW_SIDE_EFFECTING


def _shard_shape(kind, size, other):
    return (other, size) if kind == "col" else (size, other)


def _window(ref, kind, start, size):
    return ref.at[:, pl.ds(start, size)] if kind == "col" else ref.at[pl.ds(start, size), :]


def _device_tuple(k):
    return (k // 4, (k // 2) % 2, k % 2)


def _my_index():
    return 4 * lax.axis_index("x") + 2 * lax.axis_index("y") + lax.axis_index("c")


def _pack_weights(shards):
    nw = len(_WEIGHTS)

    def body(*refs):
        ins, packed, full = refs[:nw], refs[nw:2 * nw], refs[2 * nw:3 * nw]
        sem = refs[3 * nw]
        me = _my_index()
        for (_, kind, valid, pad, _), src, dst in zip(_WEIGHTS, ins, packed):
            if pad != valid:
                dst[...] = jnp.zeros_like(dst)
            if kind == "col":
                dst[:, pl.ds(0, valid)] = _bf(src[...])
            else:
                dst[pl.ds(0, valid), :] = _bf(src[...])
        for k in range(N_DEV):
            @pl.when(me == k)
            def _():
                for w, (_, kind, _, pad, _) in enumerate(_WEIGHTS):
                    pltpu.make_async_copy(packed[w], _window(full[w], kind, k * pad, pad),
                                          sem.at[w]).start()
        for w, (_, kind, _, pad, _) in enumerate(_WEIGHTS):
            pltpu.make_async_copy(packed[w], _window(full[w], kind, 0, pad), sem.at[w]).wait()

    whole = lambda shape: pl.BlockSpec(shape, lambda i: (0, 0))
    packed_shapes = [_shard_shape(kind, pad, other) for _, kind, _, pad, other in _WEIGHTS]
    outs = pl.pallas_call(
        body, name="pack_weights", grid=(1,),
        in_specs=[whole(a.shape) for a in shards],
        out_specs=[whole(s) for s in packed_shapes] + [ANY] * nw,
        out_shape=[_sds(s, BF16) for s in packed_shapes]
        + [_sds(_shard_shape(kind, N_DEV * pad, other), BF16) for _, kind, _, pad, other in _WEIGHTS],
        scratch_shapes=[pltpu.SemaphoreType.DMA((nw,))],
        compiler_params=_params("arbitrary"),
    )(*shards)
    names = [n for n, *_ in _WEIGHTS]
    return dict(zip(names, outs[:nw])), dict(zip(names, outs[nw:]))


def _hbm(a):
    return pltpu.with_memory_space_constraint(a, pltpu.HBM)


def _split_start(name, n, body_copies, sources, lands, after):
    arrays = list(sources) + list(lands)
    ns, na = len(sources), len(arrays)

    def body(*refs):
        src, land = refs[:ns], refs[ns:na]
        send, recv = refs[na + 1], refs[na + 2]
        token = refs[-1]
        body_copies(src, land, send, recv)
        token[...] = jnp.zeros_like(token)

    out = pl.pallas_call(
        body, name=name,
        out_shape=(pltpu.SemaphoreType.DMA((n,)), pltpu.SemaphoreType.DMA((n,)),
                   *[pltpu.HBM(a.shape, a.dtype) for a in arrays], _sds((8, 128))),
        in_specs=[HBM] * na + [ANY], out_specs=(SEM, SEM, *[HBM] * na, VMEM),
        input_output_aliases={i: 2 + i for i in range(na)},
        compiler_params=pltpu.CompilerParams(has_side_effects=EFFECT),
    )(*[_hbm(a) for a in arrays], after)
    return out[0], out[1], out[2:2 + ns], out[2 + ns:2 + na], out[-1]


def _split_wait(name, n, seven_of, send, recv, sources, lands, after, keep_sources=False):
    arrays = list(sources) + list(lands)
    ns, na = len(sources), len(arrays)

    def body(*refs):
        land = refs[ns:na]
        send_ref, recv_ref = refs[na], refs[na + 1]
        myself = (lax.axis_index("x"), lax.axis_index("y"), lax.axis_index("c"))
        for w in range(n):
            seven = seven_of(w, land[w])
            copy = pltpu.make_async_remote_copy(
                src_ref=seven, dst_ref=seven, send_sem=send_ref.at[w], recv_sem=recv_ref.at[w],
                device_id=myself, device_id_type=MESH)
            copy.wait_send()
            copy.wait_recv()

    out = pl.pallas_call(
        body, name=name,
        out_shape=[pltpu.HBM(a.shape, a.dtype) for a in arrays],
        in_specs=[HBM] * na + [SEM, SEM, ANY], out_specs=[HBM] * na,
        input_output_aliases={i: i for i in range(na)},
        compiler_params=pltpu.CompilerParams(has_side_effects=EFFECT),
    )(*arrays, send, recv, after)
    return out if keep_sources else out[ns:]


_ALL_PEERS = (1, 2, 3, 4, 5, 6, 7)
_NEAR_PEERS = (1, 2, 4, 6)
_FAR_CHIPS = (2, 4, 6)


def _gather_start(stage, names, packed, full, after, peers=_ALL_PEERS):
    def copies(src, land, send, recv):
        me = _my_index()
        for k in range(N_DEV):
            @pl.when(me == k)
            def _():
                for w, name in enumerate(names):
                    kind, _, pad, _ = _SPEC[name]
                    dst = _window(land[w], kind, k * pad, pad)
                    for mask in peers:
                        pltpu.make_async_remote_copy(
                            src_ref=src[w], dst_ref=dst, send_sem=send.at[w],
                            recv_sem=recv.at[w], device_id=_device_tuple(k ^ mask),
                            device_id_type=MESH).start()

    return _split_start(f"gather_start{stage}", len(names), copies,
                        [packed[n] for n in names], [full[n] for n in names], after)


def _gather_wait(stage, names, started, after, count=N_DEV - 1):
    send, recv, src, land, _ = started

    def bytes_of(w, ref):
        kind, _, pad, _ = _SPEC[names[w]]
        return _window(ref, kind, 0, count * pad)

    return dict(zip(names, _split_wait(f"gather_wait{stage}", len(names), bytes_of,
                                       send, recv, src, land, after)))


def _relay_start(stage, names, full, after):
    def copies(_, land, send, recv):
        me = _my_index()
        for k in range(N_DEV):
            @pl.when(me == k)
            def _():
                for w, name in enumerate(names):
                    kind, _, pad, _ = _SPEC[name]
                    for mask in _FAR_CHIPS:
                        win = _window(land[w], kind, (k ^ mask) * pad, pad)
                        pltpu.make_async_remote_copy(
                            src_ref=win, dst_ref=win, send_sem=send.at[w], recv_sem=recv.at[w],
                            device_id=_device_tuple(k ^ 1), device_id_type=MESH).start()

    return _split_start(f"relay_start{stage}", len(names), copies, [],
                        [full[n] for n in names], after)


def _scatter_start(stage, names, grads, after):
    def copies(src, land, send, recv):
        me = _my_index()
        for k in range(N_DEV):
            @pl.when(me != k)
            def _():
                slot = lax.rem(me + (N_DEV - 1 - k), N_DEV)
                for w, name in enumerate(names):
                    kind, _, pad, _ = _SPEC[name]
                    pltpu.make_async_remote_copy(
                        src_ref=_window(src[w], kind, k * pad, pad), dst_ref=land[w].at[slot],
                        send_sem=send.at[w], recv_sem=recv.at[w],
                        device_id=_device_tuple(k), device_id_type=MESH).start()

    lands = [lax.empty((N_DEV - 1,) + _shard_shape(_SPEC[m][0], _SPEC[m][2], _SPEC[m][3]), BF16)
             for m in names]
    return _split_start(f"scatter_start{stage}", len(names), copies, grads, lands, after)


def _scatter_wait(stage, names, started, after):
    send, recv, src, land, _ = started
    n = len(names)
    out = _split_wait(f"scatter_wait{stage}", n, lambda w, ref: ref, send, recv, src, land, after,
                      keep_sources=True)
    return dict(zip(names, out[:n])), dict(zip(names, out[n:]))


N_CHIPS = N_DEV // 2


def _pair_start(stage, names, grads, after):
    def copies(src, land, send, recv):
        me = _my_index()
        for k in range(N_DEV):
            @pl.when(me == k)
            def _():
                for w, name in enumerate(names):
                    kind, _, pad, _ = _SPEC[name]
                    for chip in range(N_CHIPS):
                        j = 2 * chip + ((k ^ 1) & 1)
                        pltpu.make_async_remote_copy(
                            src_ref=_window(src[w], kind, j * pad, pad), dst_ref=land[w].at[chip],
                            send_sem=send.at[w], recv_sem=recv.at[w],
                            device_id=_device_tuple(k ^ 1), device_id_type=MESH).start()

    lands = [lax.empty((N_CHIPS,) + _shard_shape(_SPEC[m][0], _SPEC[m][2], _SPEC[m][3]), BF16)
             for m in names]
    return _split_start(f"pair_start{stage}", len(names), copies, grads, lands, after)


def _pair_sum(dw_full, pair, *, pad, name):
    other = dw_full.shape[1]

    def body(own_ref, pair_ref, out_ref):
        out_ref[0] = _bf(own_ref[...].astype(F32) + pair_ref[0].astype(F32))

    slot = pl.BlockSpec((1, pad, other), lambda q: (q, 0, 0))
    return pl.pallas_call(
        body, name=name, grid=(N_CHIPS,),
        in_specs=[pl.BlockSpec((pad, other), lambda q: (2 * q + lax.axis_index("c"), 0)), slot],
        out_specs=slot, out_shape=_sds((N_CHIPS, pad, other), BF16),
        compiler_params=_params("arbitrary"),
    )(dw_full, pair)


def _chip_start(stage, names, sums, after):
    def copies(src, land, send, recv):
        me = _my_index()
        my_chip = lax.shift_right_logical(me, 1)
        for k in range(N_DEV):
            @pl.when((me != k) & (((me ^ k) & 1) == 0))
            def _():
                slot = lax.rem(my_chip + (N_CHIPS - 1 - k // 2), N_CHIPS)
                for w in range(len(names)):
                    pltpu.make_async_remote_copy(
                        src_ref=src[w].at[k // 2], dst_ref=land[w].at[slot],
                        send_sem=send.at[w], recv_sem=recv.at[w],
                        device_id=_device_tuple(k), device_id_type=MESH).start()

    lands = [lax.empty((N_CHIPS - 1,) + a.shape[1:], BF16) for a in sums]
    return _split_start(f"chip_start{stage}", len(names), copies, sums, lands, after)


def _adamw_chip(w, m, v, land, sums, *, name):
    shape = w.shape

    def body(w_ref, m_ref, v_ref, land_ref, own_ref, *outs):
        rows = pl.ds(0, shape[0])
        grad = own_ref[0, rows, :].astype(F32)
        for s in range(N_CHIPS - 1):
            grad = grad + land_ref[s, rows, :].astype(F32)
        _adam_update(w_ref, m_ref, v_ref, grad, *outs)

    whole = lambda a: pl.BlockSpec(a.shape, lambda i: (0,) * a.ndim)
    own = pl.BlockSpec((1,) + sums.shape[1:],
                       lambda i: (2 * lax.axis_index("x") + lax.axis_index("y"), 0, 0))
    return pl.pallas_call(
        body, name=name, grid=(1,),
        in_specs=[whole(w), whole(m), whole(v), whole(land), own],
        out_specs=[whole(w)] * 4, out_shape=[_sds(shape)] * 4,
        compiler_params=_params("arbitrary"),
    )(w, m, v, land, sums)


def _allreduce_small(small, after):
    shape = small.shape

    def body(in_ref, _after, out_ref, gath, send, recv):
        me = _my_index()
        for k in range(N_DEV):
            @pl.when(me != k)
            def _():
                pltpu.make_async_remote_copy(
                    src_ref=in_ref, dst_ref=gath.at[me], send_sem=send, recv_sem=recv,
                    device_id=_device_tuple(k), device_id_type=MESH).start()

            @pl.when(me == k)
            def _():
                gath[k] = in_ref[...]
        seven = gath.at[pl.ds(0, N_DEV - 1)]
        pltpu.make_async_remote_copy(
            src_ref=seven, dst_ref=seven, send_sem=send, recv_sem=recv,
            device_id=_device_tuple(0), device_id_type=MESH).wait()
        total = gath[0]
        for s in range(1, N_DEV):
            total = total + gath[s]
        out_ref[...] = total

    return pl.pallas_call(
        body, name="allreduce_small",
        in_specs=[VMEM, ANY], out_specs=VMEM, out_shape=_sds(shape),
        scratch_shapes=[pltpu.VMEM((N_DEV,) + shape, F32),
                        pltpu.SemaphoreType.DMA, pltpu.SemaphoreType.DMA],
    )(small, after)


def _adam_update(w_ref, m_ref, v_ref, grad, grad_ref, delta_ref, nm_ref, nv_ref):
    new_m = ADAM_B1 * m_ref[...] + (1.0 - ADAM_B1) * grad
    new_v = ADAM_B2 * v_ref[...] + (1.0 - ADAM_B2) * (grad * grad)
    m_hat = new_m / (1.0 - ADAM_B1 ** ADAM_STEP)
    v_hat = new_v / (1.0 - ADAM_B2 ** ADAM_STEP)
    grad_ref[...] = grad
    delta_ref[...] = -ADAM_LR * (m_hat / (jnp.sqrt(v_hat) + ADAM_EPS) + ADAM_WD * w_ref[...])
    nm_ref[...] = new_m
    nv_ref[...] = new_v


def _adamw(w, m, v, g, *, name):
    def body(w_ref, m_ref, v_ref, g_ref, *outs):
        _adam_update(w_ref, m_ref, v_ref, g_ref[...], *outs)

    whole = pl.BlockSpec(w.shape, lambda i: (0,) * w.ndim)
    return pl.pallas_call(
        body, name=name, grid=(1,), in_specs=[whole] * 4, out_specs=[whole] * 4,
        out_shape=[_sds(w.shape)] * 4, compiler_params=_params("arbitrary"),
    )(w, m, v, g)


def _adamw_gains(small, params):
    n = len(params)

    def body(small_ref, *refs):
        ins, outs = refs[:3 * n], refs[3 * n:]
        for r in range(n):
            width = ins[3 * r].shape[1]
            if width == D_MODEL:
                grad = small_ref[pl.ds(r, 1), :]
            else:
                grad = small_ref[pl.ds(len(_GAINS), 1), pl.ds((r - len(_GAINS)) * width, width)]
            _adam_update(*ins[3 * r:3 * r + 3], grad, *outs[4 * r:4 * r + 4])

    whole = lambda a: pl.BlockSpec(a.shape, lambda i: (0, 0))
    flat = [a for group in params for a in group]
    return pl.pallas_call(
        body, name="adamw_gains", grid=(1,),
        in_specs=[whole(small)] + [whole(a) for a in flat],
        out_specs=[whole(w) for w, _, _ in params for _ in range(4)],
        out_shape=[_sds(w.shape) for w, _, _ in params for _ in range(4)],
        compiler_params=_params("arbitrary"),
    )(small, *flat)


def _adamw_shard(w, m, v, land, dw_full, *, kind, pad, name):
    shape = w.shape
    other = shape[0] if kind == "col" else shape[1]

    def body(w_ref, m_ref, v_ref, land_ref, own_ref, *outs):
        valid = ((slice(None), pl.ds(0, shape[1])) if kind == "col"
                 else (pl.ds(0, shape[0]), slice(None)))
        grad = own_ref[valid].astype(F32)
        for s in range(N_DEV - 1):
            grad = grad + land_ref[(s,) + valid].astype(F32)
        _adam_update(w_ref, m_ref, v_ref, grad, *outs)

    whole = lambda a: pl.BlockSpec(a.shape, lambda i: (0,) * a.ndim)
    own = pl.BlockSpec(_shard_shape(kind, pad, other),
                       (lambda i: (0, _my_index())) if kind == "col" else (lambda i: (_my_index(), 0)))
    return pl.pallas_call(
        body, name=name, grid=(1,),
        in_specs=[whole(w), whole(m), whole(v), whole(land), own],
        out_specs=[whole(w)] * 4, out_shape=[_sds(shape)] * 4,
        compiler_params=_params("arbitrary"),
    )(w, m, v, land, dw_full)


_GAINS = ("ffn1_pre", "ffn1_post", "mix_pre", "mix_post", "ffn2_pre", "ffn2_post", "ple_post")
_SMALL_ROWS = 16


def _stack_gains(get):
    return jnp.concatenate([get(n) for n in _GAINS]
                           + [jnp.concatenate([get("out_sb"), get("out_ch")], axis=1)], axis=0)


def kernel(x, p, g_ffn1_pre, g_ffn1_post, w_ffn1_gate, w_ffn1_up, w_ffn1_down, g_mix_pre, g_mix_post, w_in, g_out_sb, g_out_ch, rel_bias, w_out, g_ffn2_pre, g_ffn2_post, w_ffn2_gate, w_ffn2_up, w_ffn2_down, w_ple_proj, w_ple_gate, g_ple_post, loss_target, m_g_ffn1_pre, m_g_ffn1_post, m_w_ffn1_gate, m_w_ffn1_up, m_w_ffn1_down, m_g_mix_pre, m_g_mix_post, m_w_in, m_g_out_sb, m_g_out_ch, m_rel_bias, m_w_out, m_g_ffn2_pre, m_g_ffn2_post, m_w_ffn2_gate, m_w_ffn2_up, m_w_ffn2_down, m_w_ple_proj, m_w_ple_gate, m_g_ple_post, v_g_ffn1_pre, v_g_ffn1_post, v_w_ffn1_gate, v_w_ffn1_up, v_w_ffn1_down, v_g_mix_pre, v_g_mix_post, v_w_in, v_g_out_sb, v_g_out_ch, v_rel_bias, v_w_out, v_g_ffn2_pre, v_g_ffn2_post, v_w_ffn2_gate, v_w_ffn2_up, v_w_ffn2_down, v_w_ple_proj, v_w_ple_gate, v_g_ple_post):
    given = dict(locals())
    wnames = [n for n, *_ in _WEIGHTS]

    def shard(prefix, n):
        a = given[prefix + "w_" + n][0]
        return a.T if n in _TRANSPOSED else a

    packed, full = _pack_weights([shard("", n) for n in wnames])
    first = _GATHER_STAGES[0]
    anchor = x[0]
    gathers = {0: _gather_start(0, first, packed, full, anchor, peers=_NEAR_PEERS)}

    def weights_for(stage, after):
        names = _GATHER_STAGES[stage]
        last_stage = stage + 1 == len(_GATHER_STAGES)
        if stage == 0:
            near = _gather_wait(0, names, gathers[0], after, count=len(_NEAR_PEERS))
            gathers[1] = _gather_start(1, _GATHER_STAGES[1], packed, full, near[names[0]])
            relay = _relay_start(0, names, near, gathers[1][-1])
            ws = _gather_wait("0r", names, relay, relay[-1], count=len(_FAR_CHIPS))
        else:
            ws = _gather_wait(stage, names, gathers[stage], after)
            if not last_stage:
                gathers[stage + 1] = _gather_start(stage + 1, _GATHER_STAGES[stage + 1], packed,
                                                   full, ws[names[0]])
        if last_stage:
            return ws, jnp.zeros((1, 1), F32)
        return ws, gathers[stage + 1][-1][:1, :1]

    scatters = {}

    last = len(_SCATTER_STAGES) - 1

    def grads_done(stage, grads):
        names = _SCATTER_STAGES[stage]
        start = _pair_start if stage == last else _scatter_start
        scatters[stage] = start(stage, names, [grads[n] for n in names], anchor)
        return scatters[stage][-1][:1, :1]

    gains = {n: given["g_" + n] for n in _GAINS + ("out_sb", "out_ch")}
    fvec = _rel_bias_to_fvec(rel_bias[0])
    loss, dx, dg, dfvec = _local_step(x[0], p[0, 0], loss_target[0], gains,
                                      weights_for, grads_done, fvec)

    results = {}

    def finish(stage, after):
        names = _SCATTER_STAGES[stage]
        dws, lands = _scatter_wait(stage, names, scatters[stage], after)
        for n in names:
            kind, _, pad, _ = _SPEC[n]
            out = _adamw_shard(shard("", n), shard("m_", n), shard("v_", n), lands[n], dws[n],
                               kind=kind, pad=pad, name="adamw_" + n)
            results["w_" + n] = [a.T for a in out] if n in _TRANSPOSED else out
        return results["w_" + names[-1]][0]

    names = _SCATTER_STAGES[last]
    whole = lambda w, ref: ref
    send, recv, src, land, _ = scatters[last]
    out = _split_wait(f"pair_wait{last}", len(names), whole, send, recv, src, land, dx,
                      keep_sources=True)
    sums = [_pair_sum(dwf, pair, pad=_SPEC[n][2], name="pair_sum_" + n)
            for n, dwf, pair in zip(names, out[:len(names)], out[len(names):])]
    send, recv, src, land, after = _chip_start(last, names, sums, anchor)
    for stage in range(last):
        after = finish(stage, after)
    out = _split_wait(f"chip_wait{last}", len(names), whole, send, recv, src, land, after,
                      keep_sources=True)
    for n, own, landed in zip(names, out[:len(names)], out[len(names):]):
        res = _adamw_chip(shard("", n), shard("m_", n), shard("v_", n), landed, own,
                          name="adamw_" + n)
        results["w_" + n] = [a.T for a in res] if n in _TRANSPOSED else res
        after = res[0]
    loss_col = jnp.pad(loss[:, :1], ((0, N_DEV - 1), (0, D_MODEL - CH_WIN - 1)))
    dfv = jnp.concatenate([dfvec[:, 0, :], loss_col], axis=1)
    small = _allreduce_small(jnp.concatenate([_stack_gains(lambda n: dg[n]), dfv], axis=0), after)
    gain_names = _GAINS + ("out_sb", "out_ch")
    gain_out = _adamw_gains(small, [(given["g_" + n], given["m_g_" + n], given["v_g_" + n])
                                    for n in gain_names])
    for r, n in enumerate(gain_names):
        results["g_" + n] = gain_out[4 * r:4 * r + 4]
    d_rel = _fvec_grad_to_rel_bias(small[N_DEV:, :CH_WIN].reshape(N_DEV, 1, CH_WIN))
    results["rel_bias"] = _adamw(rel_bias[0], m_rel_bias[0], v_rel_bias[0], d_rel,
                                 name="adamw_rel_bias")

    order = ("g_ffn1_pre", "g_ffn1_post", "w_ffn1_gate", "w_ffn1_up", "w_ffn1_down",
             "g_mix_pre", "g_mix_post", "w_in", "g_out_sb", "g_out_ch", "rel_bias", "w_out",
             "g_ffn2_pre", "g_ffn2_post", "w_ffn2_gate", "w_ffn2_up", "w_ffn2_down",
             "w_ple_proj", "w_ple_gate", "g_ple_post")

    def leaf(name, idx):
        a = results[name][idx]
        return a if name.startswith("g_") else a[None]

    total_loss = small[N_DEV, CH_WIN]
    return (total_loss, dx[None],
            *[leaf(n, 0) for n in order], *[leaf(n, 1) for n in order],
            *[leaf(n, 2) for n in order], *[leaf(n, 3) for n in order])
```

```python
import jax
import jax.numpy as jnp
from jax import lax
from jax.experimental import pallas as pl
from jax.experimental.pallas import tpu as pltpu

F32 = jnp.float32
BF16 = jnp.bfloat16

N_DEV = 8
D_MODEL = 1024
D_FF = 2816
FF_SHARD = D_FF // N_DEV
FF_SHARD_PAD = 384
D_FF_PAD = FF_SHARD_PAD * N_DEV
QKV_WIDTH = 3 * D_MODEL
QKV_SHARD = QKV_WIDTH // N_DEV
PLE_DIM = 256
ROW_SHARD = D_MODEL // N_DEV
HEAD_DIM = 64
PAIR = 2 * HEAD_DIM
N_PAIRS = 4
CHUNK = 64
LOOKBACK = 8
REL_CLIP = 128
N_REL = 2 * REL_CLIP + 1
CH_QB = 256
CH_LOOK = LOOKBACK * CHUNK
CH_WIN = CH_LOOK + CH_QB
SB_QB = 512
SB_KB = 256
SB_GROUP = 2
SB_LANES = tuple(slice(g * 128, (g + 1) * 128) for g in range(SB_GROUP))
EPS = 1e-6
NEG_INF = -1e30
ATT_SCALE = HEAD_DIM ** -0.5
ADAM_LR = 0.001
ADAM_B1 = 0.9
ADAM_B2 = 0.999
ADAM_EPS = 1e-08
ADAM_WD = 0.01
ADAM_STEP = 10
VMEM_LIMIT_BYTES = 48 * 1024 * 1024
MESH = pl.DeviceIdType.MESH

ANY = pl.BlockSpec(memory_space=pl.ANY)
VMEM = pl.BlockSpec(memory_space=pltpu.VMEM)


def _params(*sem):
    return pltpu.CompilerParams(dimension_semantics=sem or None,
                                vmem_limit_bytes=VMEM_LIMIT_BYTES)


def _sds(shape, dtype=F32):
    return jax.ShapeDtypeStruct(shape, dtype)


def _bf(x):
    return x.astype(BF16)


def _dot(a, b):
    return jnp.dot(_bf(a), _bf(b), preferred_element_type=F32)


def _dot_nt(a, b):
    return lax.dot_general(_bf(a), _bf(b), (((1,), (1,)), ((), ())),
                           preferred_element_type=F32)


def _dot_tn(a, b):
    return lax.dot_general(_bf(a), _bf(b), (((0,), (0,)), ((), ())),
                           preferred_element_type=F32)


def _sigmoid(x):
    return 1.0 / (1.0 + jnp.exp(-x))


def _softplus(x):
    return jnp.maximum(x, 0.0) + jnp.log(1.0 + jnp.exp(-jnp.abs(x)))


def _rstd(x):
    return lax.rsqrt(jnp.mean(x * x, axis=-1, keepdims=True) + EPS)


def _rms(x, g):
    return x * _rstd(x) * g


def _rms_bwd(dy, x, g):
    r = _rstd(x)
    w = dy * g
    dx = r * (w - x * (r * r) * jnp.mean(w * x, axis=-1, keepdims=True))
    dg = jnp.sum(dy * (x * r), axis=0, keepdims=True)
    return dx, dg


def _head_masks():
    lane = lax.broadcasted_iota(jnp.int32, (1, PAIR), 1)
    return lane < HEAD_DIM, lane >= HEAD_DIM


def _ffn_fwd(x, g_pre, g_post, wg, wu, wd, *, name):
    t = x.shape[0]
    tm, tj = 512, 1024
    ni, nj = t // tm, D_FF_PAD // tj

    def body(x_ref, gpre_ref, gpost_ref, wg_ref, wu_ref, wd_ref,
             h_ref, n_ref, a_ref, b_ref, f_ref, acc_ref):
        j = pl.program_id(1)

        @pl.when(j == 0)
        def _():
            n_ref[...] = _bf(_rms(x_ref[...], gpre_ref[...]))
            acc_ref[...] = jnp.zeros_like(acc_ref)

        n = n_ref[...]
        a = _dot_nt(n, wg_ref[...])
        b = _dot_nt(n, wu_ref[...])
        a_ref[...] = a
        b_ref[...] = b
        hmid = a * _sigmoid(a) * b
        acc_ref[...] += jnp.dot(_bf(hmid), wd_ref[...], preferred_element_type=F32)

        @pl.when(j == nj - 1)
        def _():
            f = acc_ref[...]
            f_ref[...] = f
            h_ref[...] = x_ref[...] + 0.5 * _rms(f, gpost_ref[...])

    row = pl.BlockSpec((tm, D_MODEL), lambda i, j: (i, 0))
    gain = pl.BlockSpec((1, D_MODEL), lambda i, j: (0, 0))
    col = pl.BlockSpec((tm, tj), lambda i, j: (i, j))
    wtile = pl.BlockSpec((tj, D_MODEL), lambda i, j: (j, 0))
    return pl.pallas_call(
        body, name=name, grid=(ni, nj),
        in_specs=[row, gain, gain, wtile, wtile, wtile],
        out_specs=[row, row, col, col, row],
        out_shape=[_sds((t, D_MODEL)), _sds((t, D_MODEL), BF16),
                   _sds((t, D_FF_PAD)), _sds((t, D_FF_PAD)), _sds((t, D_MODEL))],
        scratch_shapes=[pltpu.VMEM((tm, D_MODEL), F32)],
        compiler_params=_params("arbitrary", "arbitrary"),
    )(x, g_pre, g_post, wg, wu, wd)


def _ffn_bwd(n, df, a, b, wg, wu, wd, *, name):
    t = n.shape[0]
    tj, tm, ts = 256, t, 512
    nj, ni, ns = D_FF_PAD // tj, t // tm, tm // ts

    def body(n_hbm, df_hbm, a_ref, b_ref, wg_ref, wu_ref, wd_ref,
             dwg_ref, dwu_ref, dwd_ref, dn_hbm,
             n_v, df_v, dn_v, ag, au, ad, sem):
        j, i = pl.program_id(0), pl.program_id(1)

        @pl.when((j == 0) & (i == 0))
        def _():
            c1 = pltpu.make_async_copy(n_hbm, n_v, sem.at[0])
            c2 = pltpu.make_async_copy(df_hbm, df_v, sem.at[1])
            c1.start()
            c2.start()
            dn_v[...] = jnp.zeros_like(dn_v)
            c1.wait()
            c2.wait()

        @pl.when(i == 0)
        def _():
            ag[...] = jnp.zeros_like(ag)
            au[...] = jnp.zeros_like(au)
            ad[...] = jnp.zeros_like(ad)

        wgj, wuj, wdj = wg_ref[...], wu_ref[...], wd_ref[...]
        for s in range(ns):
            local = pl.ds(s * ts, ts)
            rows = pl.ds(pl.multiple_of(i * tm + s * ts, ts), ts)
            av, bv = a_ref[local, :], b_ref[local, :]
            sig = _sigmoid(av)
            silu = av * sig
            dfr = df_v[rows, :]
            nr = n_v[rows, :]
            dhmid = _dot_nt(dfr, wdj)
            da = dhmid * bv * (sig * (1.0 + av * (1.0 - sig)))
            db = dhmid * silu
            ad[...] += _dot_tn(silu * bv, dfr)
            ag[...] += _dot_tn(da, nr)
            au[...] += _dot_tn(db, nr)
            dn_v[rows, :] += _dot(da, wgj) + _dot(db, wuj)

        @pl.when(i == ni - 1)
        def _():
            dwg_ref[...] = _bf(ag[...])
            dwu_ref[...] = _bf(au[...])
            dwd_ref[...] = _bf(ad[...])

        @pl.when((j == nj - 1) & (i == ni - 1))
        def _():
            c = pltpu.make_async_copy(dn_v, dn_hbm, sem.at[0])
            c.start()
            c.wait()

    roww = pl.BlockSpec((tj, D_MODEL), lambda j, i: (j, 0))
    act = pl.BlockSpec((tm, tj), lambda j, i: (i, j))
    return pl.pallas_call(
        body, name=name, grid=(nj, ni),
        in_specs=[ANY, ANY, act, act, roww, roww, roww],
        out_specs=[roww, roww, roww, ANY],
        out_shape=[_sds((D_FF_PAD, D_MODEL), BF16)] * 3 + [_sds((t, D_MODEL))],
        scratch_shapes=[pltpu.VMEM((t, D_MODEL), BF16), pltpu.VMEM((t, D_MODEL), BF16),
                        pltpu.VMEM((t, D_MODEL), F32)]
        + [pltpu.VMEM((tj, D_MODEL), F32)] * 3 + [pltpu.SemaphoreType.DMA((2,))],
        compiler_params=_params("arbitrary", "arbitrary"),
    )(n, df, a, b, wg, wu, wd)


def _junction(dres, pre=None, post=None, *, name):
    t = dres.shape[0]
    tm = 512
    ni = t // tm
    n_in = 1 + (3 if pre else 0) + (2 if post else 0)
    coef = post[2] if post else None

    def body(*refs):
        ins, outs = list(refs[:n_in]), list(refs[n_in:])
        i = pl.program_id(0)
        dh = ins.pop(0)[...]
        if pre:
            dn_ref, x_ref, gpre_ref = ins.pop(0), ins.pop(0), ins.pop(0)
            dh_ref, dgpre_ref = outs.pop(0), outs.pop(0)
            dx, dg = _rms_bwd(dn_ref[...], x_ref[...], gpre_ref[...])
            dh = dh + dx
            dh_ref[...] = dh

            @pl.when(i == 0)
            def _():
                dgpre_ref[...] = jnp.zeros_like(dgpre_ref)
            dgpre_ref[...] += dg
        if post:
            f_ref, gpost_ref = ins.pop(0), ins.pop(0)
            df_ref, dgpost_ref = outs.pop(0), outs.pop(0)
            df, dg = _rms_bwd(coef * dh, f_ref[...], gpost_ref[...])
            df_ref[...] = _bf(df)

            @pl.when(i == 0)
            def _():
                dgpost_ref[...] = jnp.zeros_like(dgpost_ref)
            dgpost_ref[...] += dg

    row = pl.BlockSpec((tm, D_MODEL), lambda i: (i, 0))
    gain = pl.BlockSpec((1, D_MODEL), lambda i: (0, 0))
    args, in_specs, out_specs, out_shape = [dres], [row], [], []
    if pre:
        args += list(pre)
        in_specs += [row, row, gain]
        out_specs += [row, gain]
        out_shape += [_sds((t, D_MODEL)), _sds((1, D_MODEL))]
    if post:
        args += [post[0], post[1]]
        in_specs += [row, gain]
        out_specs += [row, gain]
        out_shape += [_sds((t, D_MODEL), BF16), _sds((1, D_MODEL))]
    return pl.pallas_call(
        body, name=name, grid=(ni,), in_specs=in_specs, out_specs=out_specs,
        out_shape=out_shape, compiler_params=_params("arbitrary"),
    )(*args)


def _qkv_fwd(h, g, win, *, name):
    t = h.shape[0]
    tm, tn = min(1024, t), 1024
    ni, nj = t // tm, QKV_WIDTH // tn

    def body(h_ref, g_ref, w_ref, qkv_ref, u_ref):
        @pl.when(pl.program_id(1) == 0)
        def _():
            u_ref[...] = _bf(_rms(h_ref[...], g_ref[...]))
        qkv_ref[...] = jnp.dot(u_ref[...], w_ref[...], preferred_element_type=F32)

    row = pl.BlockSpec((tm, D_MODEL), lambda i, j: (i, 0))
    return pl.pallas_call(
        body, name=name, grid=(ni, nj),
        in_specs=[row, pl.BlockSpec((1, D_MODEL), lambda i, j: (0, 0)),
                  pl.BlockSpec((D_MODEL, tn), lambda i, j: (0, j))],
        out_specs=[pl.BlockSpec((tm, tn), lambda i, j: (i, j)), row],
        out_shape=[_sds((t, QKV_WIDTH)), _sds((t, D_MODEL), BF16)],
        compiler_params=_params("arbitrary", "arbitrary"),
    )(h, g, win)


def _qkv_bwd(dq, dk, dv, u, win, *, name):
    t = u.shape[0]
    tn, ts = 512, 512
    nj, ns = QKV_WIDTH // tn, t // ts

    def body(dq_ref, dk_ref, dv_ref, u_ref, w_ref, dw_ref, du_hbm, du_v, acc_ref, sem):
        j = pl.program_id(0)

        @pl.when(j == 0)
        def _():
            du_v[...] = jnp.zeros_like(du_v)

        wj = w_ref[...]
        for role, d_ref in enumerate((dq_ref, dk_ref, dv_ref)):
            @pl.when(j % 3 == role)
            def _():
                acc_ref[...] = jnp.zeros_like(acc_ref)
                for s in range(ns):
                    rows = pl.ds(s * ts, ts)
                    dcol = d_ref[rows, :]
                    acc_ref[...] += _dot_tn(u_ref[rows, :], dcol)
                    du_v[rows, :] += _dot_nt(dcol, wj)
                dw_ref[...] = _bf(acc_ref[...])

        @pl.when(j == nj - 1)
        def _():
            c = pltpu.make_async_copy(du_v, du_hbm, sem)
            c.start()
            c.wait()

    colw = pl.BlockSpec((D_MODEL, tn), lambda j: (0, j))
    grp = pl.BlockSpec((t, tn), lambda j: (0, j // 3))
    return pl.pallas_call(
        body, name=name, grid=(nj,),
        in_specs=[grp, grp, grp, pl.BlockSpec((t, D_MODEL), lambda j: (0, 0)), colw],
        out_specs=[colw, ANY],
        out_shape=[_sds((D_MODEL, QKV_WIDTH), BF16), _sds((t, D_MODEL))],
        scratch_shapes=[pltpu.VMEM((t, D_MODEL), F32), pltpu.VMEM((D_MODEL, tn), F32),
                        pltpu.SemaphoreType.DMA],
        compiler_params=_params("arbitrary"),
    )(dq, dk, dv, u, win)


def _sb_stack(x):
    lo, hi = _head_masks()
    return jnp.concatenate([jnp.where(lo, x, 0.0), jnp.where(hi, x, 0.0)], axis=0)


def _sb_unstack(x2, blk):
    return jnp.where(_head_masks()[0], x2[:blk], x2[blk:])


def _sb_mask(qb, kb, offset):
    r = lax.broadcasted_iota(jnp.int32, (2 * qb, kb), 0) & (qb - 1)
    c = lax.broadcasted_iota(jnp.int32, (2 * qb, kb), 1) + offset
    return c < r


def _tri(n, keep):
    r = lax.broadcasted_iota(jnp.int32, (n, n), 0)
    c = lax.broadcasted_iota(jnp.int32, (n, n), 1)
    return jnp.where(keep(r, c), 1.0, 0.0).astype(BF16)


def _cumsum01(x, u):
    m = x.shape[0]
    hi = _bf(x)
    lo = _bf(x - hi.astype(F32))
    both = jnp.dot(jnp.concatenate([hi, lo], axis=0), u, preferred_element_type=F32)
    return both[:m] + both[m:]


def _sb_fwd(qkv, *, name):
    t = qkv.shape[0]
    blk, kb = min(SB_QB, t), SB_KB
    ni, per = t // blk, blk // kb

    def body(q_ref, k_ref, v_ref, o_ref, ltot_ref):
        i = pl.program_id(1)
        u_after = _tri(kb, lambda r, c: r > c)
        q2 = [_bf(_sb_stack(q_ref[:, lanes] * ATT_SCALE)) for lanes in SB_LANES]

        def tile(g, k0, mask, acc, c_l):
            kj = k_ref[pl.ds(k0, kb), SB_LANES[g]]
            vj = v_ref[pl.ds(k0, kb), SB_LANES[g]]
            z = _dot_nt(q2[g], kj)
            sp = _softplus(z)
            lf = -sp if mask is None else jnp.where(mask, -sp, 0.0)
            a = jnp.exp(z - sp + _cumsum01(lf, u_after) + c_l)
            if mask is not None:
                a = jnp.where(mask, a, 0.0)
            return acc + _dot(a, vj), c_l + jnp.sum(lf, axis=1, keepdims=True)

        def tiles(k0, mask, carry):
            return tuple(tile(g, k0, mask, *carry[g]) for g in range(SB_GROUP))

        carry = ((jnp.zeros((2 * blk, PAIR), F32), jnp.zeros((2 * blk, 1), F32)),) * SB_GROUP
        for d in reversed(range(per)):
            carry = tiles(pl.multiple_of(i * blk + d * kb, kb), _sb_mask(blk, kb, d * kb), carry)
        carry = lax.fori_loop(
            1, per * i + 1,
            lambda jj, c: tiles(pl.multiple_of((per * i - jj) * kb, kb), None, c), carry)
        for g, (acc, c_l) in enumerate(carry):
            o_ref[:, SB_LANES[g]] = _sb_unstack(acc, blk)
            ltot_ref[:, SB_LANES[g]] = _sb_unstack(jnp.broadcast_to(c_l, (2 * blk, PAIR)), blk)

    width = SB_GROUP * PAIR
    blkspec = pl.BlockSpec((blk, width), lambda p, i: (i, p))
    n_steps = N_PAIRS // SB_GROUP
    return pl.pallas_call(
        body, name=name, grid=(n_steps, ni),
        in_specs=[blkspec,
                  pl.BlockSpec((t, width), lambda p, i: (0, n_steps + p)),
                  pl.BlockSpec((t, width), lambda p, i: (0, 2 * n_steps + p))],
        out_specs=[blkspec, blkspec],
        out_shape=[_sds((t, D_MODEL)), _sds((t, D_MODEL // 2))],
        compiler_params=_params("arbitrary", "arbitrary"),
    )(qkv, qkv, qkv)


def _sb_bwd(qkv, ltot, do, *, name):
    t = qkv.shape[0]
    blk, kb = min(SB_QB, t), SB_KB
    ni, per = t // blk, blk // kb

    def body(q_ref, k_ref, v_ref, lt_ref, do_ref, dq_ref, dkout_ref, dvout_ref, dk_ref, dv_ref):
        i = pl.program_id(1)

        @pl.when(i == 0)
        def _():
            dk_ref[...] = jnp.zeros_like(dk_ref)
            dv_ref[...] = jnp.zeros_like(dv_ref)

        u_upto = _tri(kb, lambda r, c: r <= c)
        u_before = _tri(kb, lambda r, c: r < c)
        lane = lax.broadcasted_iota(jnp.int32, (1, PAIR), 1)
        q2 = [_bf(_sb_stack(q_ref[:, lanes] * ATT_SCALE)) for lanes in SB_LANES]
        do2 = [_bf(_sb_stack(do_ref[:, lanes])) for lanes in SB_LANES]
        total = [jnp.concatenate(
            [jnp.sum(jnp.where(lane == h * HEAD_DIM, lt_ref[:, lanes], 0.0), axis=1, keepdims=True)
             for h in range(2)], axis=0) for lanes in SB_LANES]

        def tile(g, k0, mask, dq_acc, c_l, c_g):
            krows = pl.ds(k0, kb)
            kj = k_ref[krows, SB_LANES[g]]
            vj = v_ref[krows, SB_LANES[g]]
            z = _dot_nt(q2[g], kj)
            sp = _softplus(z)
            sig = jnp.exp(z - sp)
            lf = -sp if mask is None else jnp.where(mask, -sp, 0.0)
            a = jnp.exp(z - sp + total[g] - (_cumsum01(lf, u_upto) + c_l))
            if mask is not None:
                a = jnp.where(mask, a, 0.0)
            gw = a * _dot_nt(do2[g], vj)
            g_before = jnp.dot(_bf(gw), u_before, preferred_element_type=F32) + c_g
            dz = gw * (1.0 - sig) - g_before * sig
            if mask is not None:
                dz = jnp.where(mask, dz, 0.0)
            dk_ref[krows, SB_LANES[g]] += _dot_tn(dz, q2[g])
            dv_ref[krows, SB_LANES[g]] += _dot_tn(a, do2[g])
            return (dq_acc + _dot(dz, kj), c_l + jnp.sum(lf, axis=1, keepdims=True),
                    c_g + jnp.sum(gw, axis=1, keepdims=True))

        def tiles(k0, mask, carry):
            return tuple(tile(g, k0, mask, *carry[g]) for g in range(SB_GROUP))

        zero = (jnp.zeros((2 * blk, PAIR), F32), jnp.zeros((2 * blk, 1), F32),
                jnp.zeros((2 * blk, 1), F32))
        carry = lax.fori_loop(
            0, per * i, lambda j, c: tiles(pl.multiple_of(j * kb, kb), None, c),
            (zero,) * SB_GROUP)
        for d in range(per):
            carry = tiles(pl.multiple_of(i * blk + d * kb, kb), _sb_mask(blk, kb, d * kb), carry)
        for g, (dq_acc, _, _) in enumerate(carry):
            dq_ref[:, SB_LANES[g]] = _bf(_sb_unstack(dq_acc, blk) * ATT_SCALE)

        @pl.when(i == ni - 1)
        def _():
            dkout_ref[...] = _bf(dk_ref[...])
            dvout_ref[...] = _bf(dv_ref[...])

    width = SB_GROUP * PAIR
    n_steps = N_PAIRS // SB_GROUP
    blkspec = lambda off: pl.BlockSpec((blk, width), lambda p, i: (i, off + p))
    full = lambda off: pl.BlockSpec((t, width), lambda p, i: (0, off + p))
    return pl.pallas_call(
        body, name=name, grid=(n_steps, ni),
        in_specs=[blkspec(0), full(n_steps), full(2 * n_steps), blkspec(0), blkspec(0)],
        out_specs=[blkspec(0), full(0), full(0)],
        out_shape=[_sds((t, D_MODEL), BF16)] * 3,
        scratch_shapes=[pltpu.VMEM((t, width), F32), pltpu.VMEM((t, width), F32)],
        compiler_params=_params("arbitrary", "arbitrary"),
    )(qkv, qkv, qkv, ltot, do)


def _ch_mask(i):
    r = lax.broadcasted_iota(jnp.int32, (CH_QB, CH_WIN), 0)
    c = lax.broadcasted_iota(jnp.int32, (CH_QB, CH_WIN), 1)
    qc = LOOKBACK + lax.shift_right_arithmetic(r, 6)
    kc = lax.shift_right_arithmetic(c, 6)
    first = i * (CH_QB // CHUNK) - LOOKBACK
    return (kc <= qc) & (kc >= qc - LOOKBACK) & (kc + first >= 0)


def _ch_probs(qm, kw, bias_h, mask):
    z = _dot_nt(qm, kw) * ATT_SCALE + bias_h
    z = jnp.where(mask, z, NEG_INF)
    e = jnp.exp(z - jnp.max(z, axis=1, keepdims=True))
    return e / jnp.sum(e, axis=1, keepdims=True)


def _ch_fill(pad_ref, src_ref, t):
    pad_ref[pl.ds(0, CH_LOOK), :] = jnp.zeros((CH_LOOK, PAIR), BF16)
    pad_ref[pl.ds(CH_LOOK, t), :] = _bf(src_ref[...])


def _ch_fwd(qkv, bias, o_in, *, name):
    t = qkv.shape[0]
    ni = t // CH_QB

    def body(q_ref, k_ref, v_ref, bias_ref, _alias, o_ref, kpad, vpad):
        i = pl.program_id(1)

        @pl.when(i == 0)
        def _():
            _ch_fill(kpad, k_ref, t)
            _ch_fill(vpad, v_ref, t)

        win = pl.ds(pl.multiple_of(i * CH_QB, CH_QB), CH_WIN)
        kw, vw = kpad[win, :], vpad[win, :]
        mask = _ch_mask(i)
        q = q_ref[...]
        outs = []
        for h, hm in enumerate(_head_masks()):
            p = _ch_probs(jnp.where(hm, q, 0.0), kw, bias_ref[h], mask)
            outs.append(_dot(p, vw))
        o_ref[...] = jnp.where(_head_masks()[0], outs[0], outs[1])

    full = lambda off: pl.BlockSpec((t, PAIR), lambda p, i: (0, off + p))
    return pl.pallas_call(
        body, name=name, grid=(N_PAIRS, ni),
        in_specs=[pl.BlockSpec((CH_QB, PAIR), lambda p, i: (i, 3 * N_PAIRS + p)),
                  full(4 * N_PAIRS), full(5 * N_PAIRS),
                  pl.BlockSpec((2, CH_QB, CH_WIN), lambda p, i: (p, 0, 0)), ANY],
        out_specs=pl.BlockSpec((CH_QB, PAIR), lambda p, i: (i, N_PAIRS + p)),
        out_shape=_sds((t, D_MODEL)),
        scratch_shapes=[pltpu.VMEM((t + CH_LOOK, PAIR), BF16)] * 2,
        input_output_aliases={4: 0},
        compiler_params=_params("arbitrary", "arbitrary"),
    )(qkv, qkv, qkv, bias, o_in)


def _ch_bwd(qkv, bias, o, do, dq_in, dk_in, dv_in, *, name):
    t = qkv.shape[0]
    ni = t // CH_QB

    def body(q_ref, k_ref, v_ref, bias_ref, o_ref, do_ref, _a0, _a1, _a2,
             dq_ref, dkout_ref, dvout_ref, dbias_ref, kpad, vpad, dkpad, dvpad):
        i = pl.program_id(1)

        @pl.when(i == 0)
        def _():
            _ch_fill(kpad, k_ref, t)
            _ch_fill(vpad, v_ref, t)
            dkpad[...] = jnp.zeros_like(dkpad)
            dvpad[...] = jnp.zeros_like(dvpad)
            dbias_ref[...] = jnp.zeros_like(dbias_ref)

        win = pl.ds(pl.multiple_of(i * CH_QB, CH_QB), CH_WIN)
        kw, vw = kpad[win, :], vpad[win, :]
        mask = _ch_mask(i)
        q, o_blk, do_blk = q_ref[...], o_ref[...], do_ref[...]
        dqs = []
        for h, hm in enumerate(_head_masks()):
            qm = _bf(jnp.where(hm, q, 0.0))
            dom = jnp.where(hm, do_blk, 0.0)
            delta = jnp.sum(dom * o_blk, axis=1, keepdims=True)
            dom = _bf(dom)
            p = _ch_probs(qm, kw, bias_ref[h], mask)
            ds = p * (_dot_nt(dom, vw) - delta)
            dbias_ref[h] += ds
            dsz = ds * ATT_SCALE
            dqs.append(_dot(dsz, kw))
            dkpad[win, :] += _dot_tn(dsz, qm)
            dvpad[win, :] += _dot_tn(p, dom)
        dq_ref[...] = _bf(jnp.where(_head_masks()[0], dqs[0], dqs[1]))

        @pl.when(i == ni - 1)
        def _():
            dkout_ref[...] = _bf(dkpad[pl.ds(CH_LOOK, t), :])
            dvout_ref[...] = _bf(dvpad[pl.ds(CH_LOOK, t), :])

    blkspec = lambda off: pl.BlockSpec((CH_QB, PAIR), lambda p, i: (i, off + p))
    full = lambda off: pl.BlockSpec((t, PAIR), lambda p, i: (0, off + p))
    bias_spec = pl.BlockSpec((2, CH_QB, CH_WIN), lambda p, i: (p, 0, 0))
    return pl.pallas_call(
        body, name=name, grid=(N_PAIRS, ni),
        in_specs=[blkspec(3 * N_PAIRS), full(4 * N_PAIRS), full(5 * N_PAIRS), bias_spec,
                  blkspec(N_PAIRS), blkspec(N_PAIRS), ANY, ANY, ANY],
        out_specs=[blkspec(N_PAIRS), full(N_PAIRS), full(N_PAIRS), bias_spec],
        out_shape=[_sds((t, D_MODEL), BF16)] * 3 + [_sds((2 * N_PAIRS, CH_QB, CH_WIN))],
        scratch_shapes=[pltpu.VMEM((t + CH_LOOK, PAIR), BF16)] * 2
        + [pltpu.VMEM((t + CH_LOOK, PAIR), F32)] * 2,
        input_output_aliases={6: 0, 7: 1, 8: 2},
        compiler_params=_params("arbitrary", "arbitrary"),
    )(qkv, qkv, qkv, bias, o, do, dq_in, dk_in, dv_in)


def _bias_expand(fvec, *, name):
    n_heads = fvec.shape[0]

    def body(f_ref, o_ref, rows8):
        row = f_ref[0]
        for r in range(8):
            rows8[pl.ds(r, 1), :] = pltpu.roll(row, r, 1)
        base = rows8[...]
        for blk in range(CH_QB // 8):
            o_ref[0, pl.ds(8 * blk, 8), :] = pltpu.roll(base, 8 * blk, 1)

    return pl.pallas_call(
        body, name=name, grid=(n_heads,),
        in_specs=[pl.BlockSpec((1, 1, CH_WIN), lambda h: (h, 0, 0))],
        out_specs=pl.BlockSpec((1, CH_QB, CH_WIN), lambda h: (h, 0, 0)),
        out_shape=_sds((n_heads, CH_QB, CH_WIN)),
        scratch_shapes=[pltpu.VMEM((8, CH_WIN), F32)],
        compiler_params=_params("arbitrary"),
    )(fvec)


def _bias_grad(dbias, *, name):
    n_heads = dbias.shape[0]
    first = CH_LOOK - REL_CLIP

    def body(d_ref, o_ref, acc8):
        acc = jnp.zeros((8, CH_WIN), F32)
        for blk in range(CH_QB // 8):
            acc = acc + pltpu.roll(d_ref[0, pl.ds(8 * blk, 8), :], (CH_WIN - 8 * blk) % CH_WIN, 1)
        acc8[...] = acc
        dvec = jnp.zeros((1, CH_WIN), F32)
        for r in range(8):
            dvec = dvec + pltpu.roll(acc8[pl.ds(r, 1), :], (CH_WIN - r) % CH_WIN, 1)
        lane = lax.broadcasted_iota(jnp.int32, (1, CH_WIN), 1)
        clipped = (lane <= first) | (lane >= first + REL_CLIP + CHUNK)
        total = jnp.sum(jnp.where(clipped, dvec, 0.0), axis=1, keepdims=True)
        o_ref[0] = jnp.where(lane == first, total, dvec)

    return pl.pallas_call(
        body, name=name, grid=(n_heads,),
        in_specs=[pl.BlockSpec((1, CH_QB, CH_WIN), lambda h: (h, 0, 0))],
        out_specs=pl.BlockSpec((1, 1, CH_WIN), lambda h: (h, 0, 0)),
        out_shape=_sds((n_heads, 1, CH_WIN)),
        scratch_shapes=[pltpu.VMEM((8, CH_WIN), F32)],
        compiler_params=_params("arbitrary"),
    )(dbias)


def _out_fwd(o, h1, g_sb, g_ch, g_post, wout, *, name):
    t = o.shape[0]
    tm = 512
    half = D_MODEL // 2

    def body(o_ref, h_ref, gsb_ref, gch_ref, gpost_ref, w_ref, h2_ref, mixed_ref, y_ref):
        ov = o_ref[...]
        mixed = jnp.concatenate([_rms(ov[:, :half], gsb_ref[...]),
                                 _rms(ov[:, half:], gch_ref[...])], axis=1)
        mixed_ref[...] = _bf(mixed)
        y = _dot(mixed, w_ref[...])
        y_ref[...] = y
        h2_ref[...] = h_ref[...] + _rms(y, gpost_ref[...])

    row = pl.BlockSpec((tm, D_MODEL), lambda i: (i, 0))
    gain = lambda n: pl.BlockSpec((1, n), lambda i: (0, 0))
    return pl.pallas_call(
        body, name=name, grid=(t // tm,),
        in_specs=[row, row, gain(half), gain(half), gain(D_MODEL),
                  pl.BlockSpec((D_MODEL, D_MODEL), lambda i: (0, 0))],
        out_specs=[row, row, row],
        out_shape=[_sds((t, D_MODEL)), _sds((t, D_MODEL), BF16), _sds((t, D_MODEL))],
        compiler_params=_params("arbitrary"),
    )(o, h1, g_sb, g_ch, g_post, wout)


def _out_bwd(dy, mixed, o, g_sb, g_ch, wout, *, name):
    t = o.shape[0]
    tm = 512
    ni = t // tm
    half = D_MODEL // 2

    def body(dy_ref, mixed_ref, o_ref, gsb_ref, gch_ref, w_ref,
             dw_ref, do_ref, dgsb_ref, dgch_ref, acc_ref):
        i = pl.program_id(0)

        @pl.when(i == 0)
        def _():
            acc_ref[...] = jnp.zeros_like(acc_ref)
            dgsb_ref[...] = jnp.zeros_like(dgsb_ref)
            dgch_ref[...] = jnp.zeros_like(dgch_ref)

        dyv = dy_ref[...]
        acc_ref[...] += _dot_tn(mixed_ref[...], dyv)
        dm = _dot_nt(dyv, w_ref[...])
        ov = o_ref[...]
        doa, dga = _rms_bwd(dm[:, :half], ov[:, :half], gsb_ref[...])
        dob, dgb = _rms_bwd(dm[:, half:], ov[:, half:], gch_ref[...])
        do_ref[...] = jnp.concatenate([doa, dob], axis=1)
        dgsb_ref[...] += dga
        dgch_ref[...] += dgb

        @pl.when(i == ni - 1)
        def _():
            dw_ref[...] = _bf(acc_ref[...])

    row = pl.BlockSpec((tm, D_MODEL), lambda i: (i, 0))
    gain = pl.BlockSpec((1, half), lambda i: (0, 0))
    sq = pl.BlockSpec((D_MODEL, D_MODEL), lambda i: (0, 0))
    return pl.pallas_call(
        body, name=name, grid=(ni,),
        in_specs=[row, row, row, gain, gain, sq],
        out_specs=[sq, row, gain, gain],
        out_shape=[_sds((D_MODEL, D_MODEL), BF16), _sds((t, D_MODEL)),
                   _sds((1, half)), _sds((1, half))],
        scratch_shapes=[pltpu.VMEM((D_MODEL, D_MODEL), F32)],
        compiler_params=_params("arbitrary"),
    )(dy, mixed, o, g_sb, g_ch, wout)


def _ple(p, h3, target, wp, wgate, g, *, name):
    t = h3.shape[0]
    tm = 512
    ni = t // tm

    def body(p_ref, h_ref, tgt_ref, wp_ref, wg_ref, g_ref,
             loss_ref, dres_ref, dwp_ref, dwg_ref, dg_ref, accp, accg):
        i = pl.program_id(0)

        @pl.when(i == 0)
        def _():
            loss_ref[...] = jnp.zeros_like(loss_ref)
            dg_ref[...] = jnp.zeros_like(dg_ref)
            accp[...] = jnp.zeros_like(accp)
            accg[...] = jnp.zeros_like(accg)

        pv, hv, gv = p_ref[...], h_ref[...], g_ref[...]
        pe = _dot(pv, wp_ref[...])
        sig = _sigmoid(_dot(hv, wg_ref[...]))
        e = pe * sig
        err = hv + _rms(e, gv) - tgt_ref[...]
        tok = jnp.mean(err * err, axis=-1, keepdims=True)
        loss_ref[...] += 0.5 * jnp.sum(tok, axis=0, keepdims=True)
        dh4 = err * (1.0 / D_MODEL)
        de, dg = _rms_bwd(dh4, e, gv)
        dg_ref[...] += dg
        dpe = de * sig
        dgt = de * pe * sig * (1.0 - sig)
        accp[...] += _dot_tn(pv, dpe)
        accg[...] += _dot_tn(hv, dgt)
        dres_ref[...] = dh4 + _dot_nt(dgt, wg_ref[...])

        @pl.when(i == ni - 1)
        def _():
            dwp_ref[...] = _bf(accp[...])
            dwg_ref[...] = _bf(accg[...])

    row = pl.BlockSpec((tm, D_MODEL), lambda i: (i, 0))
    const = lambda r, c: pl.BlockSpec((r, c), lambda i: (0, 0))
    return pl.pallas_call(
        body, name=name, grid=(ni,),
        in_specs=[pl.BlockSpec((tm, PLE_DIM), lambda i: (i, 0)), row, row,
                  const(PLE_DIM, D_MODEL), const(D_MODEL, D_MODEL), const(1, D_MODEL)],
        out_specs=[const(1, 128), row, const(PLE_DIM, D_MODEL), const(D_MODEL, D_MODEL),
                   const(1, D_MODEL)],
        out_shape=[_sds((1, 128)), _sds((t, D_MODEL)), _sds((PLE_DIM, D_MODEL), BF16),
                   _sds((D_MODEL, D_MODEL), BF16), _sds((1, D_MODEL))],
        scratch_shapes=[pltpu.VMEM((PLE_DIM, D_MODEL), F32), pltpu.VMEM((D_MODEL, D_MODEL), F32)],
        compiler_params=_params("arbitrary"),
    )(p, h3, target, wp, wgate, g)


def _rel_bias_to_fvec(rel_bias):
    rev = rel_bias[:, ::-1]
    n_heads = rel_bias.shape[0]
    first = CH_LOOK - REL_CLIP
    n_var = REL_CLIP + CHUNK
    clipped = rev[:, :1]
    fvec = jnp.concatenate([jnp.broadcast_to(clipped, (n_heads, first)), rev[:, :n_var],
                            jnp.broadcast_to(clipped, (n_heads, CH_WIN - first - n_var))], axis=1)
    return fvec.reshape(n_heads, 1, CH_WIN)


def _fvec_grad_to_rel_bias(dfvec):
    first = CH_LOOK - REL_CLIP
    n_var = REL_CLIP + CHUNK
    rev = jnp.pad(dfvec[:, 0, first:first + n_var], ((0, 0), (0, N_REL - n_var)))
    return rev[:, ::-1]


def _local_step(x, p, target, g, weights_for, grads_done, fvec):
    w, tie = weights_for(0, x)
    w = dict(w)
    h1, n1, a1, b1, f1 = _ffn_fwd(x, g["ffn1_pre"] + tie, g["ffn1_post"],
                                  w["ffn1_gate"], w["ffn1_up"], w["ffn1_down"], name="ffn1_fwd")
    more, tie = weights_for(1, h1)
    w.update(more)
    qkv, u = _qkv_fwd(h1, g["mix_pre"] + tie, w["in"], name="qkv_fwd")
    bias = _bias_expand(fvec, name="bias_expand")
    o, ltot = _sb_fwd(qkv, name="sb_fwd")
    o = _ch_fwd(qkv, bias, o, name="ch_fwd")
    h2, mixed, y = _out_fwd(o, h1, g["out_sb"], g["out_ch"], g["mix_post"], w["out"], name="out_fwd")
    w.update(weights_for(2, h2)[0])
    h3, n2, a2, b2, f2 = _ffn_fwd(h2, g["ffn2_pre"], g["ffn2_post"],
                                  w["ffn2_gate"], w["ffn2_up"], w["ffn2_down"], name="ffn2_fwd")
    loss, dh3, dwp, dwgate, dg_ple = _ple(p, h3, target, w["ple_proj"], w["ple_gate"],
                                          g["ple_post"], name="ple")
    tie = grads_done(0, {"ple_proj": dwp, "ple_gate": dwgate})

    df2, dg_ffn2_post = _junction(dh3, post=(f2, g["ffn2_post"] + tie, 0.5), name="junction3")
    dwg2, dwu2, dwd2, dn2 = _ffn_bwd(n2, df2, a2, b2, w["ffn2_gate"], w["ffn2_up"],
                                     w["ffn2_down"], name="ffn2_bwd")
    tie = grads_done(1, {"ffn2_gate": dwg2, "ffn2_up": dwu2, "ffn2_down": dwd2})
    dh2, dg_ffn2_pre, dy, dg_mix_post = _junction(
        dh3, pre=(dn2, h2, g["ffn2_pre"] + tie), post=(y, g["mix_post"], 1.0), name="junction2")
    dwout, do, dg_sb, dg_ch = _out_bwd(dy, mixed, o, g["out_sb"], g["out_ch"], w["out"],
                                       name="out_bwd")
    dq, dk, dv = _sb_bwd(qkv, ltot, do, name="sb_bwd")
    dq, dk, dv, dbias = _ch_bwd(qkv, bias, o, do, dq, dk, dv, name="ch_bwd")
    dfvec = _bias_grad(dbias, name="bias_grad")
    dwin, du = _qkv_bwd(dq, dk, dv, u, w["in"], name="qkv_bwd")
    tie = grads_done(2, {"out": dwout, "in": dwin})
    dh1, dg_mix_pre, df1, dg_ffn1_post = _junction(
        dh2, pre=(du, h1, g["mix_pre"] + tie), post=(f1, g["ffn1_post"], 0.5), name="junction1")
    dwg1, dwu1, dwd1, dn1 = _ffn_bwd(n1, df1, a1, b1, w["ffn1_gate"], w["ffn1_up"],
                                     w["ffn1_down"], name="ffn1_bwd")
    tie = grads_done(3, {"ffn1_gate": dwg1, "ffn1_up": dwu1, "ffn1_down": dwd1})
    dx, dg_ffn1_pre = _junction(dh1, pre=(dn1, x, g["ffn1_pre"] + tie), name="junction0")

    dg = {"ffn1_pre": dg_ffn1_pre, "ffn1_post": dg_ffn1_post, "mix_pre": dg_mix_pre,
          "mix_post": dg_mix_post, "out_sb": dg_sb, "out_ch": dg_ch,
          "ffn2_pre": dg_ffn2_pre, "ffn2_post": dg_ffn2_post, "ple_post": dg_ple}
    return loss, dx, dg, dfvec


_WEIGHTS = (
    ("ffn1_gate", "row", FF_SHARD, FF_SHARD_PAD, D_MODEL),
    ("ffn1_up", "row", FF_SHARD, FF_SHARD_PAD, D_MODEL),
    ("ffn1_down", "row", FF_SHARD, FF_SHARD_PAD, D_MODEL),
    ("in", "col", QKV_SHARD, QKV_SHARD, D_MODEL),
    ("out", "row", ROW_SHARD, ROW_SHARD, D_MODEL),
    ("ffn2_gate", "row", FF_SHARD, FF_SHARD_PAD, D_MODEL),
    ("ffn2_up", "row", FF_SHARD, FF_SHARD_PAD, D_MODEL),
    ("ffn2_down", "row", FF_SHARD, FF_SHARD_PAD, D_MODEL),
    ("ple_proj", "col", ROW_SHARD, ROW_SHARD, PLE_DIM),
    ("ple_gate", "row", ROW_SHARD, ROW_SHARD, D_MODEL),
)
_TRANSPOSED = ("ffn1_gate", "ffn1_up", "ffn2_gate", "ffn2_up")
_SPEC = {n: (kind, valid, pad, other) for n, kind, valid, pad, other in _WEIGHTS}
_GATHER_STAGES = (("ffn1_gate", "ffn1_up", "ffn1_down"), ("in", "out"),
                  ("ffn2_gate", "ffn2_up", "ffn2_down", "ple_proj", "ple_gate"))
_SCATTER_STAGES = (("ple_proj", "ple_gate"), ("ffn2_gate", "ffn2_up", "ffn2_down"),
                   ("out", "in"), ("ffn1_gate", "ffn1_up", "ffn1_down"))
HBM = pl.BlockSpec(memory_space=pltpu.HBM)
SEM = pl.BlockSpec(memory_space=pltpu.SEMAPHORE)
EFFECT = pltpu.SideEffectType.DATAFLOW_SIDE_EFFECTING


def _shard_shape(kind, size, other):
    return (other, size) if kind == "col" else (size, other)


def _window(ref, kind, start, size):
    return ref.at[:, pl.ds(start, size)] if kind == "col" else ref.at[pl.ds(start, size), :]


def _device_tuple(k):
    return (k // 4, (k // 2) % 2, k % 2)


def _my_index():
    return 4 * lax.axis_index("x") + 2 * lax.axis_index("y") + lax.axis_index("c")


def _pack_weights(shards):
    nw = len(_WEIGHTS)

    def body(*refs):
        ins, packed, full = refs[:nw], refs[nw:2 * nw], refs[2 * nw:3 * nw]
        sem = refs[3 * nw]
        me = _my_index()
        for (_, kind, valid, pad, _), src, dst in zip(_WEIGHTS, ins, packed):
            if pad != valid:
                dst[...] = jnp.zeros_like(dst)
            if kind == "col":
                dst[:, pl.ds(0, valid)] = _bf(src[...])
            else:
                dst[pl.ds(0, valid), :] = _bf(src[...])
        for k in range(N_DEV):
            @pl.when(me == k)
            def _():
                for w, (_, kind, _, pad, _) in enumerate(_WEIGHTS):
                    pltpu.make_async_copy(packed[w], _window(full[w], kind, k * pad, pad),
                                          sem.at[w]).start()
        for w, (_, kind, _, pad, _) in enumerate(_WEIGHTS):
            pltpu.make_async_copy(packed[w], _window(full[w], kind, 0, pad), sem.at[w]).wait()

    whole = lambda shape: pl.BlockSpec(shape, lambda i: (0, 0))
    packed_shapes = [_shard_shape(kind, pad, other) for _, kind, _, pad, other in _WEIGHTS]
    outs = pl.pallas_call(
        body, name="pack_weights", grid=(1,),
        in_specs=[whole(a.shape) for a in shards],
        out_specs=[whole(s) for s in packed_shapes] + [ANY] * nw,
        out_shape=[_sds(s, BF16) for s in packed_shapes]
        + [_sds(_shard_shape(kind, N_DEV * pad, other), BF16) for _, kind, _, pad, other in _WEIGHTS],
        scratch_shapes=[pltpu.SemaphoreType.DMA((nw,))],
        compiler_params=_params("arbitrary"),
    )(*shards)
    names = [n for n, *_ in _WEIGHTS]
    return dict(zip(names, outs[:nw])), dict(zip(names, outs[nw:]))


def _hbm(a):
    return pltpu.with_memory_space_constraint(a, pltpu.HBM)


def _split_start(name, n, body_copies, sources, lands, after):
    arrays = list(sources) + list(lands)
    ns, na = len(sources), len(arrays)

    def body(*refs):
        src, land = refs[:ns], refs[ns:na]
        send, recv = refs[na + 1], refs[na + 2]
        token = refs[-1]
        body_copies(src, land, send, recv)
        token[...] = jnp.zeros_like(token)

    out = pl.pallas_call(
        body, name=name,
        out_shape=(pltpu.SemaphoreType.DMA((n,)), pltpu.SemaphoreType.DMA((n,)),
                   *[pltpu.HBM(a.shape, a.dtype) for a in arrays], _sds((8, 128))),
        in_specs=[HBM] * na + [ANY], out_specs=(SEM, SEM, *[HBM] * na, VMEM),
        input_output_aliases={i: 2 + i for i in range(na)},
        compiler_params=pltpu.CompilerParams(has_side_effects=EFFECT),
    )(*[_hbm(a) for a in arrays], after)
    return out[0], out[1], out[2:2 + ns], out[2 + ns:2 + na], out[-1]


def _split_wait(name, n, seven_of, send, recv, sources, lands, after, keep_sources=False):
    arrays = list(sources) + list(lands)
    ns, na = len(sources), len(arrays)

    def body(*refs):
        land = refs[ns:na]
        send_ref, recv_ref = refs[na], refs[na + 1]
        myself = (lax.axis_index("x"), lax.axis_index("y"), lax.axis_index("c"))
        for w in range(n):
            seven = seven_of(w, land[w])
            copy = pltpu.make_async_remote_copy(
                src_ref=seven, dst_ref=seven, send_sem=send_ref.at[w], recv_sem=recv_ref.at[w],
                device_id=myself, device_id_type=MESH)
            copy.wait_send()
            copy.wait_recv()

    out = pl.pallas_call(
        body, name=name,
        out_shape=[pltpu.HBM(a.shape, a.dtype) for a in arrays],
        in_specs=[HBM] * na + [SEM, SEM, ANY], out_specs=[HBM] * na,
        input_output_aliases={i: i for i in range(na)},
        compiler_params=pltpu.CompilerParams(has_side_effects=EFFECT),
    )(*arrays, send, recv, after)
    return out if keep_sources else out[ns:]


_ALL_PEERS = (1, 2, 3, 4, 5, 6, 7)
_NEAR_PEERS = (1, 2, 4, 6)
_FAR_CHIPS = (2, 4, 6)


def _gather_start(stage, names, packed, full, after, peers=_ALL_PEERS):
    def copies(src, land, send, recv):
        me = _my_index()
        for k in range(N_DEV):
            @pl.when(me == k)
            def _():
                for w, name in enumerate(names):
                    kind, _, pad, _ = _SPEC[name]
                    dst = _window(land[w], kind, k * pad, pad)
                    for mask in peers:
                        pltpu.make_async_remote_copy(
                            src_ref=src[w], dst_ref=dst, send_sem=send.at[w],
                            recv_sem=recv.at[w], device_id=_device_tuple(k ^ mask),
                            device_id_type=MESH).start()

    return _split_start(f"gather_start{stage}", len(names), copies,
                        [packed[n] for n in names], [full[n] for n in names], after)


def _gather_wait(stage, names, started, after, count=N_DEV - 1):
    send, recv, src, land, _ = started

    def bytes_of(w, ref):
        kind, _, pad, _ = _SPEC[names[w]]
        return _window(ref, kind, 0, count * pad)

    return dict(zip(names, _split_wait(f"gather_wait{stage}", len(names), bytes_of,
                                       send, recv, src, land, after)))


def _relay_start(stage, names, full, after):
    def copies(_, land, send, recv):
        me = _my_index()
        for k in range(N_DEV):
            @pl.when(me == k)
            def _():
                for w, name in enumerate(names):
                    kind, _, pad, _ = _SPEC[name]
                    for mask in _FAR_CHIPS:
                        win = _window(land[w], kind, (k ^ mask) * pad, pad)
                        pltpu.make_async_remote_copy(
                            src_ref=win, dst_ref=win, send_sem=send.at[w], recv_sem=recv.at[w],
                            device_id=_device_tuple(k ^ 1), device_id_type=MESH).start()

    return _split_start(f"relay_start{stage}", len(names), copies, [],
                        [full[n] for n in names], after)


def _scatter_start(stage, names, grads, after):
    def copies(src, land, send, recv):
        me = _my_index()
        for k in range(N_DEV):
            @pl.when(me != k)
            def _():
                slot = lax.rem(me + (N_DEV - 1 - k), N_DEV)
                for w, name in enumerate(names):
                    kind, _, pad, _ = _SPEC[name]
                    pltpu.make_async_remote_copy(
                        src_ref=_window(src[w], kind, k * pad, pad), dst_ref=land[w].at[slot],
                        send_sem=send.at[w], recv_sem=recv.at[w],
                        device_id=_device_tuple(k), device_id_type=MESH).start()

    lands = [lax.empty((N_DEV - 1,) + _shard_shape(_SPEC[m][0], _SPEC[m][2], _SPEC[m][3]), BF16)
             for m in names]
    return _split_start(f"scatter_start{stage}", len(names), copies, grads, lands, after)


def _scatter_wait(stage, names, started, after):
    send, recv, src, land, _ = started
    n = len(names)
    out = _split_wait(f"scatter_wait{stage}", n, lambda w, ref: ref, send, recv, src, land, after,
                      keep_sources=True)
    return dict(zip(names, out[:n])), dict(zip(names, out[n:]))


N_CHIPS = N_DEV // 2


def _pair_start(stage, names, grads, after):
    def copies(src, land, send, recv):
        me = _my_index()
        for k in range(N_DEV):
            @pl.when(me == k)
            def _():
                for w, name in enumerate(names):
                    kind, _, pad, _ = _SPEC[name]
                    for chip in range(N_CHIPS):
                        j = 2 * chip + ((k ^ 1) & 1)
                        pltpu.make_async_remote_copy(
                            src_ref=_window(src[w], kind, j * pad, pad), dst_ref=land[w].at[chip],
                            send_sem=send.at[w], recv_sem=recv.at[w],
                            device_id=_device_tuple(k ^ 1), device_id_type=MESH).start()

    lands = [lax.empty((N_CHIPS,) + _shard_shape(_SPEC[m][0], _SPEC[m][2], _SPEC[m][3]), BF16)
             for m in names]
    return _split_start(f"pair_start{stage}", len(names), copies, grads, lands, after)


def _pair_sum(dw_full, pair, *, pad, name):
    other = dw_full.shape[1]

    def body(own_ref, pair_ref, out_ref):
        out_ref[0] = _bf(own_ref[...].astype(F32) + pair_ref[0].astype(F32))

    slot = pl.BlockSpec((1, pad, other), lambda q: (q, 0, 0))
    return pl.pallas_call(
        body, name=name, grid=(N_CHIPS,),
        in_specs=[pl.BlockSpec((pad, other), lambda q: (2 * q + lax.axis_index("c"), 0)), slot],
        out_specs=slot, out_shape=_sds((N_CHIPS, pad, other), BF16),
        compiler_params=_params("arbitrary"),
    )(dw_full, pair)


def _chip_start(stage, names, sums, after):
    def copies(src, land, send, recv):
        me = _my_index()
        my_chip = lax.shift_right_logical(me, 1)
        for k in range(N_DEV):
            @pl.when((me != k) & (((me ^ k) & 1) == 0))
            def _():
                slot = lax.rem(my_chip + (N_CHIPS - 1 - k // 2), N_CHIPS)
                for w in range(len(names)):
                    pltpu.make_async_remote_copy(
                        src_ref=src[w].at[k // 2], dst_ref=land[w].at[slot],
                        send_sem=send.at[w], recv_sem=recv.at[w],
                        device_id=_device_tuple(k), device_id_type=MESH).start()

    lands = [lax.empty((N_CHIPS - 1,) + a.shape[1:], BF16) for a in sums]
    return _split_start(f"chip_start{stage}", len(names), copies, sums, lands, after)


def _adamw_chip(w, m, v, land, sums, *, name):
    shape = w.shape

    def body(w_ref, m_ref, v_ref, land_ref, own_ref, *outs):
        rows = pl.ds(0, shape[0])
        grad = own_ref[0, rows, :].astype(F32)
        for s in range(N_CHIPS - 1):
            grad = grad + land_ref[s, rows, :].astype(F32)
        _adam_update(w_ref, m_ref, v_ref, grad, *outs)

    whole = lambda a: pl.BlockSpec(a.shape, lambda i: (0,) * a.ndim)
    own = pl.BlockSpec((1,) + sums.shape[1:],
                       lambda i: (2 * lax.axis_index("x") + lax.axis_index("y"), 0, 0))
    return pl.pallas_call(
        body, name=name, grid=(1,),
        in_specs=[whole(w), whole(m), whole(v), whole(land), own],
        out_specs=[whole(w)] * 4, out_shape=[_sds(shape)] * 4,
        compiler_params=_params("arbitrary"),
    )(w, m, v, land, sums)


def _allreduce_small(small, after):
    shape = small.shape

    def body(in_ref, _after, out_ref, gath, send, recv):
        me = _my_index()
        for k in range(N_DEV):
            @pl.when(me != k)
            def _():
                pltpu.make_async_remote_copy(
                    src_ref=in_ref, dst_ref=gath.at[me], send_sem=send, recv_sem=recv,
                    device_id=_device_tuple(k), device_id_type=MESH).start()

            @pl.when(me == k)
            def _():
                gath[k] = in_ref[...]
        seven = gath.at[pl.ds(0, N_DEV - 1)]
        pltpu.make_async_remote_copy(
            src_ref=seven, dst_ref=seven, send_sem=send, recv_sem=recv,
            device_id=_device_tuple(0), device_id_type=MESH).wait()
        total = gath[0]
        for s in range(1, N_DEV):
            total = total + gath[s]
        out_ref[...] = total

    return pl.pallas_call(
        body, name="allreduce_small",
        in_specs=[VMEM, ANY], out_specs=VMEM, out_shape=_sds(shape),
        scratch_shapes=[pltpu.VMEM((N_DEV,) + shape, F32),
                        pltpu.SemaphoreType.DMA, pltpu.SemaphoreType.DMA],
    )(small, after)


def _adam_update(w_ref, m_ref, v_ref, grad, grad_ref, delta_ref, nm_ref, nv_ref):
    new_m = ADAM_B1 * m_ref[...] + (1.0 - ADAM_B1) * grad
    new_v = ADAM_B2 * v_ref[...] + (1.0 - ADAM_B2) * (grad * grad)
    m_hat = new_m / (1.0 - ADAM_B1 ** ADAM_STEP)
    v_hat = new_v / (1.0 - ADAM_B2 ** ADAM_STEP)
    grad_ref[...] = grad
    delta_ref[...] = -ADAM_LR * (m_hat / (jnp.sqrt(v_hat) + ADAM_EPS) + ADAM_WD * w_ref[...])
    nm_ref[...] = new_m
    nv_ref[...] = new_v


def _adamw(w, m, v, g, *, name):
    def body(w_ref, m_ref, v_ref, g_ref, *outs):
        _adam_update(w_ref, m_ref, v_ref, g_ref[...], *outs)

    whole = pl.BlockSpec(w.shape, lambda i: (0,) * w.ndim)
    return pl.pallas_call(
        body, name=name, grid=(1,), in_specs=[whole] * 4, out_specs=[whole] * 4,
        out_shape=[_sds(w.shape)] * 4, compiler_params=_params("arbitrary"),
    )(w, m, v, g)


def _adamw_gains(small, params):
    n = len(params)

    def body(small_ref, *refs):
        ins, outs = refs[:3 * n], refs[3 * n:]
        for r in range(n):
            width = ins[3 * r].shape[1]
            if width == D_MODEL:
                grad = small_ref[pl.ds(r, 1), :]
            else:
                grad = small_ref[pl.ds(len(_GAINS), 1), pl.ds((r - len(_GAINS)) * width, width)]
            _adam_update(*ins[3 * r:3 * r + 3], grad, *outs[4 * r:4 * r + 4])

    whole = lambda a: pl.BlockSpec(a.shape, lambda i: (0, 0))
    flat = [a for group in params for a in group]
    return pl.pallas_call(
        body, name="adamw_gains", grid=(1,),
        in_specs=[whole(small)] + [whole(a) for a in flat],
        out_specs=[whole(w) for w, _, _ in params for _ in range(4)],
        out_shape=[_sds(w.shape) for w, _, _ in params for _ in range(4)],
        compiler_params=_params("arbitrary"),
    )(small, *flat)


def _adamw_shard(w, m, v, land, dw_full, *, kind, pad, name):
    shape = w.shape
    other = shape[0] if kind == "col" else shape[1]

    def body(w_ref, m_ref, v_ref, land_ref, own_ref, *outs):
        valid = ((slice(None), pl.ds(0, shape[1])) if kind == "col"
                 else (pl.ds(0, shape[0]), slice(None)))
        grad = own_ref[valid].astype(F32)
        for s in range(N_DEV - 1):
            grad = grad + land_ref[(s,) + valid].astype(F32)
        _adam_update(w_ref, m_ref, v_ref, grad, *outs)

    whole = lambda a: pl.BlockSpec(a.shape, lambda i: (0,) * a.ndim)
    own = pl.BlockSpec(_shard_shape(kind, pad, other),
                       (lambda i: (0, _my_index())) if kind == "col" else (lambda i: (_my_index(), 0)))
    return pl.pallas_call(
        body, name=name, grid=(1,),
        in_specs=[whole(w), whole(m), whole(v), whole(land), own],
        out_specs=[whole(w)] * 4, out_shape=[_sds(shape)] * 4,
        compiler_params=_params("arbitrary"),
    )(w, m, v, land, dw_full)


_GAINS = ("ffn1_pre", "ffn1_post", "mix_pre", "mix_post", "ffn2_pre", "ffn2_post", "ple_post")
_SMALL_ROWS = 16


def _stack_gains(get):
    return jnp.concatenate([get(n) for n in _GAINS]
                           + [jnp.concatenate([get("out_sb"), get("out_ch")], axis=1)], axis=0)


def kernel(x, p, g_ffn1_pre, g_ffn1_post, w_ffn1_gate, w_ffn1_up, w_ffn1_down, g_mix_pre, g_mix_post, w_in, g_out_sb, g_out_ch, rel_bias, w_out, g_ffn2_pre, g_ffn2_post, w_ffn2_gate, w_ffn2_up, w_ffn2_down, w_ple_proj, w_ple_gate, g_ple_post, loss_target, m_g_ffn1_pre, m_g_ffn1_post, m_w_ffn1_gate, m_w_ffn1_up, m_w_ffn1_down, m_g_mix_pre, m_g_mix_post, m_w_in, m_g_out_sb, m_g_out_ch, m_rel_bias, m_w_out, m_g_ffn2_pre, m_g_ffn2_post, m_w_ffn2_gate, m_w_ffn2_up, m_w_ffn2_down, m_w_ple_proj, m_w_ple_gate, m_g_ple_post, v_g_ffn1_pre, v_g_ffn1_post, v_w_ffn1_gate, v_w_ffn1_up, v_w_ffn1_down, v_g_mix_pre, v_g_mix_post, v_w_in, v_g_out_sb, v_g_out_ch, v_rel_bias, v_w_out, v_g_ffn2_pre, v_g_ffn2_post, v_w_ffn2_gate, v_w_ffn2_up, v_w_ffn2_down, v_w_ple_proj, v_w_ple_gate, v_g_ple_post):
    given = dict(locals())
    wnames = [n for n, *_ in _WEIGHTS]

    def shard(prefix, n):
        a = given[prefix + "w_" + n][0]
        return a.T if n in _TRANSPOSED else a

    packed, full = _pack_weights([shard("", n) for n in wnames])
    first = _GATHER_STAGES[0]
    anchor = x[0]
    gathers = {0: _gather_start(0, first, packed, full, anchor, peers=_NEAR_PEERS)}

    def weights_for(stage, after):
        names = _GATHER_STAGES[stage]
        last_stage = stage + 1 == len(_GATHER_STAGES)
        if stage == 0:
            near = _gather_wait(0, names, gathers[0], after, count=len(_NEAR_PEERS))
            gathers[1] = _gather_start(1, _GATHER_STAGES[1], packed, full, near[names[0]])
            relay = _relay_start(0, names, near, gathers[1][-1])
            ws = _gather_wait("0r", names, relay, relay[-1], count=len(_FAR_CHIPS))
        else:
            ws = _gather_wait(stage, names, gathers[stage], after)
            if not last_stage:
                gathers[stage + 1] = _gather_start(stage + 1, _GATHER_STAGES[stage + 1], packed,
                                                   full, ws[names[0]])
        if last_stage:
            return ws, jnp.zeros((1, 1), F32)
        return ws, gathers[stage + 1][-1][:1, :1]

    scatters = {}

    last = len(_SCATTER_STAGES) - 1

    def grads_done(stage, grads):
        names = _SCATTER_STAGES[stage]
        start = _pair_start if stage == last else _scatter_start
        scatters[stage] = start(stage, names, [grads[n] for n in names], anchor)
        return scatters[stage][-1][:1, :1]

    gains = {n: given["g_" + n] for n in _GAINS + ("out_sb", "out_ch")}
    fvec = _rel_bias_to_fvec(rel_bias[0])
    loss, dx, dg, dfvec = _local_step(x[0], p[0, 0], loss_target[0], gains,
                                      weights_for, grads_done, fvec)

    results = {}

    def finish(stage, after):
        names = _SCATTER_STAGES[stage]
        dws, lands = _scatter_wait(stage, names, scatters[stage], after)
        for n in names:
            kind, _, pad, _ = _SPEC[n]
            out = _adamw_shard(shard("", n), shard("m_", n), shard("v_", n), lands[n], dws[n],
                               kind=kind, pad=pad, name="adamw_" + n)
            results["w_" + n] = [a.T for a in out] if n in _TRANSPOSED else out
        return results["w_" + names[-1]][0]

    names = _SCATTER_STAGES[last]
    whole = lambda w, ref: ref
    send, recv, src, land, _ = scatters[last]
    out = _split_wait(f"pair_wait{last}", len(names), whole, send, recv, src, land, dx,
                      keep_sources=True)
    sums = [_pair_sum(dwf, pair, pad=_SPEC[n][2], name="pair_sum_" + n)
            for n, dwf, pair in zip(names, out[:len(names)], out[len(names):])]
    send, recv, src, land, after = _chip_start(last, names, sums, anchor)
    for stage in range(last):
        after = finish(stage, after)
    out = _split_wait(f"chip_wait{last}", len(names), whole, send, recv, src, land, after,
                      keep_sources=True)
    for n, own, landed in zip(names, out[:len(names)], out[len(names):]):
        res = _adamw_chip(shard("", n), shard("m_", n), shard("v_", n), landed, own,
                          name="adamw_" + n)
        results["w_" + n] = [a.T for a in res] if n in _TRANSPOSED else res
        after = res[0]
    loss_col = jnp.pad(loss[:, :1], ((0, N_DEV - 1), (0, D_MODEL - CH_WIN - 1)))
    dfv = jnp.concatenate([dfvec[:, 0, :], loss_col], axis=1)
    small = _allreduce_small(jnp.concatenate([_stack_gains(lambda n: dg[n]), dfv], axis=0), after)
    gain_names = _GAINS + ("out_sb", "out_ch")
    gain_out = _adamw_gains(small, [(given["g_" + n], given["m_g_" + n], given["v_g_" + n])
                                    for n in gain_names])
    for r, n in enumerate(gain_names):
        results["g_" + n] = gain_out[4 * r:4 * r + 4]
    d_rel = _fvec_grad_to_rel_bias(small[N_DEV:, :CH_WIN].reshape(N_DEV, 1, CH_WIN))
    results["rel_bias"] = _adamw(rel_bias[0], m_rel_bias[0], v_rel_bias[0], d_rel,
                                 name="adamw_rel_bias")

    order = ("g_ffn1_pre", "g_ffn1_post", "w_ffn1_gate", "w_ffn1_up", "w_ffn1_down",
             "g_mix_pre", "g_mix_post", "w_in", "g_out_sb", "g_out_ch", "rel_bias", "w_out",
             "g_ffn2_pre", "g_ffn2_post", "w_ffn2_gate", "w_ffn2_up", "w_ffn2_down",
             "w_ple_proj", "w_ple_gate", "g_ple_post")

    def leaf(name, idx):
        a = results[name][idx]
        return a if name.startswith("g_") else a[None]

    total_loss = small[N_DEV, CH_WIN]
    return (total_loss, dx[None],
            *[leaf(n, 0) for n in order], *[leaf(n, 1) for n in order],
            *[leaf(n, 2) for n in order], *[leaf(n, 3) for n in order])
```

```python
import jax
import jax.numpy as jnp
from jax import lax
from jax.experimental import pallas as pl
from jax.experimental.pallas import tpu as pltpu

F32 = jnp.float32
BF16 = jnp.bfloat16

N_DEV = 8
D_MODEL = 1024
D_FF = 2816
FF_SHARD = D_FF // N_DEV
FF_SHARD_PAD = 384
D_FF_PAD = FF_SHARD_PAD * N_DEV
QKV_WIDTH = 3 * D_MODEL
QKV_SHARD = QKV_WIDTH // N_DEV
PLE_DIM = 256
ROW_SHARD = D_MODEL // N_DEV
HEAD_DIM = 64
PAIR = 2 * HEAD_DIM
N_PAIRS = 4
CHUNK = 64
LOOKBACK = 8
REL_CLIP = 128
N_REL = 2 * REL_CLIP + 1
CH_QB = 256
CH_LOOK = LOOKBACK * CHUNK
CH_WIN = CH_LOOK + CH_QB
SB_QB = 512
SB_KB = 256
SB_GROUP = 2
SB_LANES = tuple(slice(g * 128, (g + 1) * 128) for g in range(SB_GROUP))
EPS = 1e-6
NEG_INF = -1e30
ATT_SCALE = HEAD_DIM ** -0.5
ADAM_LR = 0.001
ADAM_B1 = 0.9
ADAM_B2 = 0.999
ADAM_EPS = 1e-08
ADAM_WD = 0.01
ADAM_STEP = 10
VMEM_LIMIT_BYTES = 48 * 1024 * 1024
MESH = pl.DeviceIdType.MESH

ANY = pl.BlockSpec(memory_space=pl.ANY)
VMEM = pl.BlockSpec(memory_space=pltpu.VMEM)


def _params(*sem):
    return pltpu.CompilerParams(dimension_semantics=sem or None,
                                vmem_limit_bytes=VMEM_LIMIT_BYTES)


def _sds(shape, dtype=F32):
    return jax.ShapeDtypeStruct(shape, dtype)


def _bf(x):
    return x.astype(BF16)


def _dot(a, b):
    return jnp.dot(_bf(a), _bf(b), preferred_element_type=F32)


def _dot_nt(a, b):
    return lax.dot_general(_bf(a), _bf(b), (((1,), (1,)), ((), ())),
                           preferred_element_type=F32)


def _dot_tn(a, b):
    return lax.dot_general(_bf(a), _bf(b), (((0,), (0,)), ((), ())),
                           preferred_element_type=F32)


def _sigmoid(x):
    return 1.0 / (1.0 + jnp.exp(-x))


def _softplus(x):
    return jnp.maximum(x, 0.0) + jnp.log(1.0 + jnp.exp(-jnp.abs(x)))


def _rstd(x):
    return lax.rsqrt(jnp.mean(x * x, axis=-1, keepdims=True) + EPS)


def _rms(x, g):
    return x * _rstd(x) * g


def _rms_bwd(dy, x, g):
    r = _rstd(x)
    w = dy * g
    dx = r * (w - x * (r * r) * jnp.mean(w * x, axis=-1, keepdims=True))
    dg = jnp.sum(dy * (x * r), axis=0, keepdims=True)
    return dx, dg


def _head_masks():
    lane = lax.broadcasted_iota(jnp.int32, (1, PAIR), 1)
    return lane < HEAD_DIM, lane >= HEAD_DIM


def _ffn_fwd(x, g_pre, g_post, wg, wu, wd, *, name):
    t = x.shape[0]
    tm, tj = 512, 1024
    ni, nj = t // tm, D_FF_PAD // tj

    def body(x_ref, gpre_ref, gpost_ref, wg_ref, wu_ref, wd_ref,
             h_ref, n_ref, a_ref, b_ref, f_ref, acc_ref):
        j = pl.program_id(1)

        @pl.when(j == 0)
        def _():
            n_ref[...] = _bf(_rms(x_ref[...], gpre_ref[...]))
            acc_ref[...] = jnp.zeros_like(acc_ref)

        n = n_ref[...]
        a = _dot_nt(n, wg_ref[...])
        b = _dot_nt(n, wu_ref[...])
        a_ref[...] = a
        b_ref[...] = b
        hmid = a * _sigmoid(a) * b
        acc_ref[...] += jnp.dot(_bf(hmid), wd_ref[...], preferred_element_type=F32)

        @pl.when(j == nj - 1)
        def _():
            f = acc_ref[...]
            f_ref[...] = f
            h_ref[...] = x_ref[...] + 0.5 * _rms(f, gpost_ref[...])

    row = pl.BlockSpec((tm, D_MODEL), lambda i, j: (i, 0))
    gain = pl.BlockSpec((1, D_MODEL), lambda i, j: (0, 0))
    col = pl.BlockSpec((tm, tj), lambda i, j: (i, j))
    wtile = pl.BlockSpec((tj, D_MODEL), lambda i, j: (j, 0))
    return pl.pallas_call(
        body, name=name, grid=(ni, nj),
        in_specs=[row, gain, gain, wtile, wtile, wtile],
        out_specs=[row, row, col, col, row],
        out_shape=[_sds((t, D_MODEL)), _sds((t, D_MODEL), BF16),
                   _sds((t, D_FF_PAD)), _sds((t, D_FF_PAD)), _sds((t, D_MODEL))],
        scratch_shapes=[pltpu.VMEM((tm, D_MODEL), F32)],
        compiler_params=_params("arbitrary", "arbitrary"),
    )(x, g_pre, g_post, wg, wu, wd)


def _ffn_bwd(n, df, a, b, wg, wu, wd, *, name):
    t = n.shape[0]
    tj, tm, ts = 256, t, 512
    nj, ni, ns = D_FF_PAD // tj, t // tm, tm // ts

    def body(n_hbm, df_hbm, a_ref, b_ref, wg_ref, wu_ref, wd_ref,
             dwg_ref, dwu_ref, dwd_ref, dn_hbm,
             n_v, df_v, dn_v, ag, au, ad, sem):
        j, i = pl.program_id(0), pl.program_id(1)

        @pl.when((j == 0) & (i == 0))
        def _():
            c1 = pltpu.make_async_copy(n_hbm, n_v, sem.at[0])
            c2 = pltpu.make_async_copy(df_hbm, df_v, sem.at[1])
            c1.start()
            c2.start()
            dn_v[...] = jnp.zeros_like(dn_v)
            c1.wait()
            c2.wait()

        @pl.when(i == 0)
        def _():
            ag[...] = jnp.zeros_like(ag)
            au[...] = jnp.zeros_like(au)
            ad[...] = jnp.zeros_like(ad)

        wgj, wuj, wdj = wg_ref[...], wu_ref[...], wd_ref[...]
        for s in range(ns):
            local = pl.ds(s * ts, ts)
            rows = pl.ds(pl.multiple_of(i * tm + s * ts, ts), ts)
            av, bv = a_ref[local, :], b_ref[local, :]
            sig = _sigmoid(av)
            silu = av * sig
            dfr = df_v[rows, :]
            nr = n_v[rows, :]
            dhmid = _dot_nt(dfr, wdj)
            da = dhmid * bv * (sig * (1.0 + av * (1.0 - sig)))
            db = dhmid * silu
            ad[...] += _dot_tn(silu * bv, dfr)
            ag[...] += _dot_tn(da, nr)
            au[...] += _dot_tn(db, nr)
            dn_v[rows, :] += _dot(da, wgj) + _dot(db, wuj)

        @pl.when(i == ni - 1)
        def _():
            dwg_ref[...] = _bf(ag[...])
            dwu_ref[...] = _bf(au[...])
            dwd_ref[...] = _bf(ad[...])

        @pl.when((j == nj - 1) & (i == ni - 1))
        def _():
            c = pltpu.make_async_copy(dn_v, dn_hbm, sem.at[0])
            c.start()
            c.wait()

    roww = pl.BlockSpec((tj, D_MODEL), lambda j, i: (j, 0))
    act = pl.BlockSpec((tm, tj), lambda j, i: (i, j))
    return pl.pallas_call(
        body, name=name, grid=(nj, ni),
        in_specs=[ANY, ANY, act, act, roww, roww, roww],
        out_specs=[roww, roww, roww, ANY],
        out_shape=[_sds((D_FF_PAD, D_MODEL), BF16)] * 3 + [_sds((t, D_MODEL))],
        scratch_shapes=[pltpu.VMEM((t, D_MODEL), BF16), pltpu.VMEM((t, D_MODEL), BF16),
                        pltpu.VMEM((t, D_MODEL), F32)]
        + [pltpu.VMEM((tj, D_MODEL), F32)] * 3 + [pltpu.SemaphoreType.DMA((2,))],
        compiler_params=_params("arbitrary", "arbitrary"),
    )(n, df, a, b, wg, wu, wd)


def _junction(dres, pre=None, post=None, *, name):
    t = dres.shape[0]
    tm = 256
    ni = t // tm
    n_in = 1 + (3 if pre else 0) + (2 if post else 0)
    coef = post[2] if post else None

    def body(*refs):
        ins, outs = list(refs[:n_in]), list(refs[n_in:])
        i = pl.program_id(0)
        dh = ins.pop(0)[...]
        if pre:
            dn_ref, x_ref, gpre_ref = ins.pop(0), ins.pop(0), ins.pop(0)
            dh_ref, dgpre_ref = outs.pop(0), outs.pop(0)
            dx, dg = _rms_bwd(dn_ref[...], x_ref[...], gpre_ref[...])
            dh = dh + dx
            dh_ref[...] = dh

            @pl.when(i == 0)
            def _():
                dgpre_ref[...] = jnp.zeros_like(dgpre_ref)
            dgpre_ref[...] += dg
        if post:
            f_ref, gpost_ref = ins.pop(0), ins.pop(0)
            df_ref, dgpost_ref = outs.pop(0), outs.pop(0)
            df, dg = _rms_bwd(coef * dh, f_ref[...], gpost_ref[...])
            df_ref[...] = _bf(df)

            @pl.when(i == 0)
            def _():
                dgpost_ref[...] = jnp.zeros_like(dgpost_ref)
            dgpost_ref[...] += dg

    row = pl.BlockSpec((tm, D_MODEL), lambda i: (i, 0))
    gain = pl.BlockSpec((1, D_MODEL), lambda i: (0, 0))
    args, in_specs, out_specs, out_shape = [dres], [row], [], []
    if pre:
        args += list(pre)
        in_specs += [row, row, gain]
        out_specs += [row, gain]
        out_shape += [_sds((t, D_MODEL)), _sds((1, D_MODEL))]
    if post:
        args += [post[0], post[1]]
        in_specs += [row, gain]
        out_specs += [row, gain]
        out_shape += [_sds((t, D_MODEL), BF16), _sds((1, D_MODEL))]
    return pl.pallas_call(
        body, name=name, grid=(ni,), in_specs=in_specs, out_specs=out_specs,
        out_shape=out_shape, compiler_params=_params("arbitrary"),
    )(*args)


def _qkv_fwd(h, g, win, *, name):
    t = h.shape[0]
    tm, tn = min(1024, t), 1024
    ni, nj = t // tm, QKV_WIDTH // tn

    def body(h_ref, g_ref, w_ref, qkv_ref, u_ref):
        @pl.when(pl.program_id(1) == 0)
        def _():
            u_ref[...] = _bf(_rms(h_ref[...], g_ref[...]))
        qkv_ref[...] = jnp.dot(u_ref[...], w_ref[...], preferred_element_type=F32)

    row = pl.BlockSpec((tm, D_MODEL), lambda i, j: (i, 0))
    return pl.pallas_call(
        body, name=name, grid=(ni, nj),
        in_specs=[row, pl.BlockSpec((1, D_MODEL), lambda i, j: (0, 0)),
                  pl.BlockSpec((D_MODEL, tn), lambda i, j: (0, j))],
        out_specs=[pl.BlockSpec((tm, tn), lambda i, j: (i, j)), row],
        out_shape=[_sds((t, QKV_WIDTH)), _sds((t, D_MODEL), BF16)],
        compiler_params=_params("arbitrary", "arbitrary"),
    )(h, g, win)


def _qkv_bwd(dq, dk, dv, u, win, *, name):
    t = u.shape[0]
    tn, ts = 512, 512
    nj, ns = QKV_WIDTH // tn, t // ts

    def body(dq_ref, dk_ref, dv_ref, u_ref, w_ref, dw_ref, du_hbm, du_v, acc_ref, sem):
        j = pl.program_id(0)

        @pl.when(j == 0)
        def _():
            du_v[...] = jnp.zeros_like(du_v)

        wj = w_ref[...]
        for role, d_ref in enumerate((dq_ref, dk_ref, dv_ref)):
            @pl.when(j % 3 == role)
            def _():
                acc_ref[...] = jnp.zeros_like(acc_ref)
                for s in range(ns):
                    rows = pl.ds(s * ts, ts)
                    dcol = d_ref[rows, :]
                    acc_ref[...] += _dot_tn(u_ref[rows, :], dcol)
                    du_v[rows, :] += _dot_nt(dcol, wj)
                dw_ref[...] = _bf(acc_ref[...])

        @pl.when(j == nj - 1)
        def _():
            c = pltpu.make_async_copy(du_v, du_hbm, sem)
            c.start()
            c.wait()

    colw = pl.BlockSpec((D_MODEL, tn), lambda j: (0, j))
    grp = pl.BlockSpec((t, tn), lambda j: (0, j // 3))
    return pl.pallas_call(
        body, name=name, grid=(nj,),
        in_specs=[grp, grp, grp, pl.BlockSpec((t, D_MODEL), lambda j: (0, 0)), colw],
        out_specs=[colw, ANY],
        out_shape=[_sds((D_MODEL, QKV_WIDTH), BF16), _sds((t, D_MODEL))],
        scratch_shapes=[pltpu.VMEM((t, D_MODEL), F32), pltpu.VMEM((D_MODEL, tn), F32),
                        pltpu.SemaphoreType.DMA],
        compiler_params=_params("arbitrary"),
    )(dq, dk, dv, u, win)


def _sb_stack(x):
    lo, hi = _head_masks()
    return jnp.concatenate([jnp.where(lo, x, 0.0), jnp.where(hi, x, 0.0)], axis=0)


def _sb_unstack(x2, blk):
    return jnp.where(_head_masks()[0], x2[:blk], x2[blk:])


def _sb_mask(qb, kb, offset):
    r = lax.broadcasted_iota(jnp.int32, (2 * qb, kb), 0) & (qb - 1)
    c = lax.broadcasted_iota(jnp.int32, (2 * qb, kb), 1) + offset
    return c < r


def _tri(n, keep):
    r = lax.broadcasted_iota(jnp.int32, (n, n), 0)
    c = lax.broadcasted_iota(jnp.int32, (n, n), 1)
    return jnp.where(keep(r, c), 1.0, 0.0).astype(BF16)


def _cumsum01(x, u):
    m = x.shape[0]
    hi = _bf(x)
    lo = _bf(x - hi.astype(F32))
    both = jnp.dot(jnp.concatenate([hi, lo], axis=0), u, preferred_element_type=F32)
    return both[:m] + both[m:]


def _sb_fwd(qkv, *, name):
    t = qkv.shape[0]
    blk, kb = min(SB_QB, t), SB_KB
    ni, per = t // blk, blk // kb

    def body(q_ref, k_ref, v_ref, o_ref, ltot_ref):
        i = pl.program_id(1)
        u_after = _tri(kb, lambda r, c: r > c)
        q2 = [_bf(_sb_stack(q_ref[:, lanes] * ATT_SCALE)) for lanes in SB_LANES]

        def tile(g, k0, mask, acc, c_l):
            kj = k_ref[pl.ds(k0, kb), SB_LANES[g]]
            vj = v_ref[pl.ds(k0, kb), SB_LANES[g]]
            z = _dot_nt(q2[g], kj)
            sp = _softplus(z)
            lf = -sp if mask is None else jnp.where(mask, -sp, 0.0)
            a = jnp.exp(z - sp + _cumsum01(lf, u_after) + c_l)
            if mask is not None:
                a = jnp.where(mask, a, 0.0)
            return acc + _dot(a, vj), c_l + jnp.sum(lf, axis=1, keepdims=True)

        def tiles(k0, mask, carry):
            return tuple(tile(g, k0, mask, *carry[g]) for g in range(SB_GROUP))

        carry = ((jnp.zeros((2 * blk, PAIR), F32), jnp.zeros((2 * blk, 1), F32)),) * SB_GROUP
        for d in reversed(range(per)):
            carry = tiles(pl.multiple_of(i * blk + d * kb, kb), _sb_mask(blk, kb, d * kb), carry)
        carry = lax.fori_loop(
            1, per * i + 1,
            lambda jj, c: tiles(pl.multiple_of((per * i - jj) * kb, kb), None, c), carry)
        for g, (acc, c_l) in enumerate(carry):
            o_ref[:, SB_LANES[g]] = _sb_unstack(acc, blk)
            ltot_ref[:, SB_LANES[g]] = _sb_unstack(jnp.broadcast_to(c_l, (2 * blk, PAIR)), blk)

    width = SB_GROUP * PAIR
    blkspec = pl.BlockSpec((blk, width), lambda p, i: (i, p))
    n_steps = N_PAIRS // SB_GROUP
    return pl.pallas_call(
        body, name=name, grid=(n_steps, ni),
        in_specs=[blkspec,
                  pl.BlockSpec((t, width), lambda p, i: (0, n_steps + p)),
                  pl.BlockSpec((t, width), lambda p, i: (0, 2 * n_steps + p))],
        out_specs=[blkspec, blkspec],
        out_shape=[_sds((t, D_MODEL)), _sds((t, D_MODEL // 2))],
        compiler_params=_params("arbitrary", "arbitrary"),
    )(qkv, qkv, qkv)


def _sb_bwd(qkv, ltot, do, *, name):
    t = qkv.shape[0]
    blk, kb = min(SB_QB, t), SB_KB
    ni, per = t // blk, blk // kb

    def body(q_ref, k_ref, v_ref, lt_ref, do_ref, dq_ref, dkout_ref, dvout_ref, dk_ref, dv_ref):
        i = pl.program_id(1)

        @pl.when(i == 0)
        def _():
            dk_ref[...] = jnp.zeros_like(dk_ref)
            dv_ref[...] = jnp.zeros_like(dv_ref)

        u_upto = _tri(kb, lambda r, c: r <= c)
        u_before = _tri(kb, lambda r, c: r < c)
        lane = lax.broadcasted_iota(jnp.int32, (1, PAIR), 1)
        q2 = [_bf(_sb_stack(q_ref[:, lanes] * ATT_SCALE)) for lanes in SB_LANES]
        do2 = [_bf(_sb_stack(do_ref[:, lanes])) for lanes in SB_LANES]
        total = [jnp.concatenate(
            [jnp.sum(jnp.where(lane == h * HEAD_DIM, lt_ref[:, lanes], 0.0), axis=1, keepdims=True)
             for h in range(2)], axis=0) for lanes in SB_LANES]

        def tile(g, k0, mask, dq_acc, c_l, c_g):
            krows = pl.ds(k0, kb)
            kj = k_ref[krows, SB_LANES[g]]
            vj = v_ref[krows, SB_LANES[g]]
            z = _dot_nt(q2[g], kj)
            sp = _softplus(z)
            sig = jnp.exp(z - sp)
            lf = -sp if mask is None else jnp.where(mask, -sp, 0.0)
            a = jnp.exp(z - sp + total[g] - (_cumsum01(lf, u_upto) + c_l))
            if mask is not None:
                a = jnp.where(mask, a, 0.0)
            gw = a * _dot_nt(do2[g], vj)
            g_before = jnp.dot(_bf(gw), u_before, preferred_element_type=F32) + c_g
            dz = gw * (1.0 - sig) - g_before * sig
            if mask is not None:
                dz = jnp.where(mask, dz, 0.0)
            dk_ref[krows, SB_LANES[g]] += _dot_tn(dz, q2[g])
            dv_ref[krows, SB_LANES[g]] += _dot_tn(a, do2[g])
            return (dq_acc + _dot(dz, kj), c_l + jnp.sum(lf, axis=1, keepdims=True),
                    c_g + jnp.sum(gw, axis=1, keepdims=True))

        def tiles(k0, mask, carry):
            return tuple(tile(g, k0, mask, *carry[g]) for g in range(SB_GROUP))

        zero = (jnp.zeros((2 * blk, PAIR), F32), jnp.zeros((2 * blk, 1), F32),
                jnp.zeros((2 * blk, 1), F32))
        carry = lax.fori_loop(
            0, per * i, lambda j, c: tiles(pl.multiple_of(j * kb, kb), None, c),
            (zero,) * SB_GROUP)
        for d in range(per):
            carry = tiles(pl.multiple_of(i * blk + d * kb, kb), _sb_mask(blk, kb, d * kb), carry)
        for g, (dq_acc, _, _) in enumerate(carry):
            dq_ref[:, SB_LANES[g]] = _bf(_sb_unstack(dq_acc, blk) * ATT_SCALE)

        @pl.when(i == ni - 1)
        def _():
            dkout_ref[...] = _bf(dk_ref[...])
            dvout_ref[...] = _bf(dv_ref[...])

    width = SB_GROUP * PAIR
    n_steps = N_PAIRS // SB_GROUP
    blkspec = lambda off: pl.BlockSpec((blk, width), lambda p, i: (i, off + p))
    full = lambda off: pl.BlockSpec((t, width), lambda p, i: (0, off + p))
    return pl.pallas_call(
        body, name=name, grid=(n_steps, ni),
        in_specs=[blkspec(0), full(n_steps), full(2 * n_steps), blkspec(0), blkspec(0)],
        out_specs=[blkspec(0), full(0), full(0)],
        out_shape=[_sds((t, D_MODEL), BF16)] * 3,
        scratch_shapes=[pltpu.VMEM((t, width), F32), pltpu.VMEM((t, width), F32)],
        compiler_params=_params("arbitrary", "arbitrary"),
    )(qkv, qkv, qkv, ltot, do)


def _ch_mask(i):
    r = lax.broadcasted_iota(jnp.int32, (CH_QB, CH_WIN), 0)
    c = lax.broadcasted_iota(jnp.int32, (CH_QB, CH_WIN), 1)
    qc = LOOKBACK + lax.shift_right_arithmetic(r, 6)
    kc = lax.shift_right_arithmetic(c, 6)
    first = i * (CH_QB // CHUNK) - LOOKBACK
    return (kc <= qc) & (kc >= qc - LOOKBACK) & (kc + first >= 0)


def _ch_probs(qm, kw, bias_h, mask):
    z = _dot_nt(qm, kw) * ATT_SCALE + bias_h
    z = jnp.where(mask, z, NEG_INF)
    e = jnp.exp(z - jnp.max(z, axis=1, keepdims=True))
    return e / jnp.sum(e, axis=1, keepdims=True)


def _ch_fill(pad_ref, src_ref, t):
    pad_ref[pl.ds(0, CH_LOOK), :] = jnp.zeros((CH_LOOK, PAIR), BF16)
    pad_ref[pl.ds(CH_LOOK, t), :] = _bf(src_ref[...])


def _ch_fwd(qkv, bias, o_in, *, name):
    t = qkv.shape[0]
    ni = t // CH_QB

    def body(q_ref, k_ref, v_ref, bias_ref, _alias, o_ref, kpad, vpad):
        i = pl.program_id(1)

        @pl.when(i == 0)
        def _():
            _ch_fill(kpad, k_ref, t)
            _ch_fill(vpad, v_ref, t)

        win = pl.ds(pl.multiple_of(i * CH_QB, CH_QB), CH_WIN)
        kw, vw = kpad[win, :], vpad[win, :]
        mask = _ch_mask(i)
        q = q_ref[...]
        outs = []
        for h, hm in enumerate(_head_masks()):
            p = _ch_probs(jnp.where(hm, q, 0.0), kw, bias_ref[h], mask)
            outs.append(_dot(p, vw))
        o_ref[...] = jnp.where(_head_masks()[0], outs[0], outs[1])

    full = lambda off: pl.BlockSpec((t, PAIR), lambda p, i: (0, off + p))
    return pl.pallas_call(
        body, name=name, grid=(N_PAIRS, ni),
        in_specs=[pl.BlockSpec((CH_QB, PAIR), lambda p, i: (i, 3 * N_PAIRS + p)),
                  full(4 * N_PAIRS), full(5 * N_PAIRS),
                  pl.BlockSpec((2, CH_QB, CH_WIN), lambda p, i: (p, 0, 0)), ANY],
        out_specs=pl.BlockSpec((CH_QB, PAIR), lambda p, i: (i, N_PAIRS + p)),
        out_shape=_sds((t, D_MODEL)),
        scratch_shapes=[pltpu.VMEM((t + CH_LOOK, PAIR), BF16)] * 2,
        input_output_aliases={4: 0},
        compiler_params=_params("arbitrary", "arbitrary"),
    )(qkv, qkv, qkv, bias, o_in)


def _ch_bwd(qkv, bias, o, do, dq_in, dk_in, dv_in, *, name):
    t = qkv.shape[0]
    ni = t // CH_QB

    def body(q_ref, k_ref, v_ref, bias_ref, o_ref, do_ref, _a0, _a1, _a2,
             dq_ref, dkout_ref, dvout_ref, dbias_ref, kpad, vpad, dkpad, dvpad):
        i = pl.program_id(1)

        @pl.when(i == 0)
        def _():
            _ch_fill(kpad, k_ref, t)
            _ch_fill(vpad, v_ref, t)
            dkpad[...] = jnp.zeros_like(dkpad)
            dvpad[...] = jnp.zeros_like(dvpad)
            dbias_ref[...] = jnp.zeros_like(dbias_ref)

        win = pl.ds(pl.multiple_of(i * CH_QB, CH_QB), CH_WIN)
        kw, vw = kpad[win, :], vpad[win, :]
        mask = _ch_mask(i)
        q, o_blk, do_blk = q_ref[...], o_ref[...], do_ref[...]
        dqs = []
        for h, hm in enumerate(_head_masks()):
            qm = _bf(jnp.where(hm, q, 0.0))
            dom = jnp.where(hm, do_blk, 0.0)
            delta = jnp.sum(dom * o_blk, axis=1, keepdims=True)
            dom = _bf(dom)
            p = _ch_probs(qm, kw, bias_ref[h], mask)
            ds = p * (_dot_nt(dom, vw) - delta)
            dbias_ref[h] += ds
            dsz = ds * ATT_SCALE
            dqs.append(_dot(dsz, kw))
            dkpad[win, :] += _dot_tn(dsz, qm)
            dvpad[win, :] += _dot_tn(p, dom)
        dq_ref[...] = _bf(jnp.where(_head_masks()[0], dqs[0], dqs[1]))

        @pl.when(i == ni - 1)
        def _():
            dkout_ref[...] = _bf(dkpad[pl.ds(CH_LOOK, t), :])
            dvout_ref[...] = _bf(dvpad[pl.ds(CH_LOOK, t), :])

    blkspec = lambda off: pl.BlockSpec((CH_QB, PAIR), lambda p, i: (i, off + p))
    full = lambda off: pl.BlockSpec((t, PAIR), lambda p, i: (0, off + p))
    bias_spec = pl.BlockSpec((2, CH_QB, CH_WIN), lambda p, i: (p, 0, 0))
    return pl.pallas_call(
        body, name=name, grid=(N_PAIRS, ni),
        in_specs=[blkspec(3 * N_PAIRS), full(4 * N_PAIRS), full(5 * N_PAIRS), bias_spec,
                  blkspec(N_PAIRS), blkspec(N_PAIRS), ANY, ANY, ANY],
        out_specs=[blkspec(N_PAIRS), full(N_PAIRS), full(N_PAIRS), bias_spec],
        out_shape=[_sds((t, D_MODEL), BF16)] * 3 + [_sds((2 * N_PAIRS, CH_QB, CH_WIN))],
        scratch_shapes=[pltpu.VMEM((t + CH_LOOK, PAIR), BF16)] * 2
        + [pltpu.VMEM((t + CH_LOOK, PAIR), F32)] * 2,
        input_output_aliases={6: 0, 7: 1, 8: 2},
        compiler_params=_params("arbitrary", "arbitrary"),
    )(qkv, qkv, qkv, bias, o, do, dq_in, dk_in, dv_in)


def _bias_expand(fvec, *, name):
    n_heads = fvec.shape[0]

    def body(f_ref, o_ref, rows8):
        row = f_ref[0]
        for r in range(8):
            rows8[pl.ds(r, 1), :] = pltpu.roll(row, r, 1)
        base = rows8[...]
        for blk in range(CH_QB // 8):
            o_ref[0, pl.ds(8 * blk, 8), :] = pltpu.roll(base, 8 * blk, 1)

    return pl.pallas_call(
        body, name=name, grid=(n_heads,),
        in_specs=[pl.BlockSpec((1, 1, CH_WIN), lambda h: (h, 0, 0))],
        out_specs=pl.BlockSpec((1, CH_QB, CH_WIN), lambda h: (h, 0, 0)),
        out_shape=_sds((n_heads, CH_QB, CH_WIN)),
        scratch_shapes=[pltpu.VMEM((8, CH_WIN), F32)],
        compiler_params=_params("arbitrary"),
    )(fvec)


def _bias_grad(dbias, *, name):
    n_heads = dbias.shape[0]
    first = CH_LOOK - REL_CLIP

    def body(d_ref, o_ref, acc8):
        acc = jnp.zeros((8, CH_WIN), F32)
        for blk in range(CH_QB // 8):
            acc = acc + pltpu.roll(d_ref[0, pl.ds(8 * blk, 8), :], (CH_WIN - 8 * blk) % CH_WIN, 1)
        acc8[...] = acc
        dvec = jnp.zeros((1, CH_WIN), F32)
        for r in range(8):
            dvec = dvec + pltpu.roll(acc8[pl.ds(r, 1), :], (CH_WIN - r) % CH_WIN, 1)
        lane = lax.broadcasted_iota(jnp.int32, (1, CH_WIN), 1)
        clipped = (lane <= first) | (lane >= first + REL_CLIP + CHUNK)
        total = jnp.sum(jnp.where(clipped, dvec, 0.0), axis=1, keepdims=True)
        o_ref[0] = jnp.where(lane == first, total, dvec)

    return pl.pallas_call(
        body, name=name, grid=(n_heads,),
        in_specs=[pl.BlockSpec((1, CH_QB, CH_WIN), lambda h: (h, 0, 0))],
        out_specs=pl.BlockSpec((1, 1, CH_WIN), lambda h: (h, 0, 0)),
        out_shape=_sds((n_heads, 1, CH_WIN)),
        scratch_shapes=[pltpu.VMEM((8, CH_WIN), F32)],
        compiler_params=_params("arbitrary"),
    )(dbias)


def _out_fwd(o, h1, g_sb, g_ch, g_post, wout, *, name):
    t = o.shape[0]
    tm = 512
    half = D_MODEL // 2

    def body(o_ref, h_ref, gsb_ref, gch_ref, gpost_ref, w_ref, h2_ref, mixed_ref, y_ref):
        ov = o_ref[...]
        mixed = jnp.concatenate([_rms(ov[:, :half], gsb_ref[...]),
                                 _rms(ov[:, half:], gch_ref[...])], axis=1)
        mixed_ref[...] = _bf(mixed)
        y = _dot(mixed, w_ref[...])
        y_ref[...] = y
        h2_ref[...] = h_ref[...] + _rms(y, gpost_ref[...])

    row = pl.BlockSpec((tm, D_MODEL), lambda i: (i, 0))
    gain = lambda n: pl.BlockSpec((1, n), lambda i: (0, 0))
    return pl.pallas_call(
        body, name=name, grid=(t // tm,),
        in_specs=[row, row, gain(half), gain(half), gain(D_MODEL),
                  pl.BlockSpec((D_MODEL, D_MODEL), lambda i: (0, 0))],
        out_specs=[row, row, row],
        out_shape=[_sds((t, D_MODEL)), _sds((t, D_MODEL), BF16), _sds((t, D_MODEL))],
        compiler_params=_params("arbitrary"),
    )(o, h1, g_sb, g_ch, g_post, wout)


def _out_bwd(dy, mixed, o, g_sb, g_ch, wout, *, name):
    t = o.shape[0]
    tm = 512
    ni = t // tm
    half = D_MODEL // 2

    def body(dy_ref, mixed_ref, o_ref, gsb_ref, gch_ref, w_ref,
             dw_ref, do_ref, dgsb_ref, dgch_ref, acc_ref):
        i = pl.program_id(0)

        @pl.when(i == 0)
        def _():
            acc_ref[...] = jnp.zeros_like(acc_ref)
            dgsb_ref[...] = jnp.zeros_like(dgsb_ref)
            dgch_ref[...] = jnp.zeros_like(dgch_ref)

        dyv = dy_ref[...]
        acc_ref[...] += _dot_tn(mixed_ref[...], dyv)
        dm = _dot_nt(dyv, w_ref[...])
        ov = o_ref[...]
        doa, dga = _rms_bwd(dm[:, :half], ov[:, :half], gsb_ref[...])
        dob, dgb = _rms_bwd(dm[:, half:], ov[:, half:], gch_ref[...])
        do_ref[...] = jnp.concatenate([doa, dob], axis=1)
        dgsb_ref[...] += dga
        dgch_ref[...] += dgb

        @pl.when(i == ni - 1)
        def _():
            dw_ref[...] = _bf(acc_ref[...])

    row = pl.BlockSpec((tm, D_MODEL), lambda i: (i, 0))
    gain = pl.BlockSpec((1, half), lambda i: (0, 0))
    sq = pl.BlockSpec((D_MODEL, D_MODEL), lambda i: (0, 0))
    return pl.pallas_call(
        body, name=name, grid=(ni,),
        in_specs=[row, row, row, gain, gain, sq],
        out_specs=[sq, row, gain, gain],
        out_shape=[_sds((D_MODEL, D_MODEL), BF16), _sds((t, D_MODEL)),
                   _sds((1, half)), _sds((1, half))],
        scratch_shapes=[pltpu.VMEM((D_MODEL, D_MODEL), F32)],
        compiler_params=_params("arbitrary"),
    )(dy, mixed, o, g_sb, g_ch, wout)


def _ple(p, h3, target, wp, wgate, g, *, name):
    t = h3.shape[0]
    tm = 512
    ni = t // tm

    def body(p_ref, h_ref, tgt_ref, wp_ref, wg_ref, g_ref,
             loss_ref, dres_ref, dwp_ref, dwg_ref, dg_ref, accp, accg):
        i = pl.program_id(0)

        @pl.when(i == 0)
        def _():
            loss_ref[...] = jnp.zeros_like(loss_ref)
            dg_ref[...] = jnp.zeros_like(dg_ref)
            accp[...] = jnp.zeros_like(accp)
            accg[...] = jnp.zeros_like(accg)

        pv, hv, gv = p_ref[...], h_ref[...], g_ref[...]
        pe = _dot(pv, wp_ref[...])
        sig = _sigmoid(_dot(hv, wg_ref[...]))
        e = pe * sig
        err = hv + _rms(e, gv) - tgt_ref[...]
        tok = jnp.mean(err * err, axis=-1, keepdims=True)
        loss_ref[...] += 0.5 * jnp.sum(tok, axis=0, keepdims=True)
        dh4 = err * (1.0 / D_MODEL)
        de, dg = _rms_bwd(dh4, e, gv)
        dg_ref[...] += dg
        dpe = de * sig
        dgt = de * pe * sig * (1.0 - sig)
        accp[...] += _dot_tn(pv, dpe)
        accg[...] += _dot_tn(hv, dgt)
        dres_ref[...] = dh4 + _dot_nt(dgt, wg_ref[...])

        @pl.when(i == ni - 1)
        def _():
            dwp_ref[...] = _bf(accp[...])
            dwg_ref[...] = _bf(accg[...])

    row = pl.BlockSpec((tm, D_MODEL), lambda i: (i, 0))
    const = lambda r, c: pl.BlockSpec((r, c), lambda i: (0, 0))
    return pl.pallas_call(
        body, name=name, grid=(ni,),
        in_specs=[pl.BlockSpec((tm, PLE_DIM), lambda i: (i, 0)), row, row,
                  const(PLE_DIM, D_MODEL), const(D_MODEL, D_MODEL), const(1, D_MODEL)],
        out_specs=[const(1, 128), row, const(PLE_DIM, D_MODEL), const(D_MODEL, D_MODEL),
                   const(1, D_MODEL)],
        out_shape=[_sds((1, 128)), _sds((t, D_MODEL)), _sds((PLE_DIM, D_MODEL), BF16),
                   _sds((D_MODEL, D_MODEL), BF16), _sds((1, D_MODEL))],
        scratch_shapes=[pltpu.VMEM((PLE_DIM, D_MODEL), F32), pltpu.VMEM((D_MODEL, D_MODEL), F32)],
        compiler_params=_params("arbitrary"),
    )(p, h3, target, wp, wgate, g)


def _rel_bias_to_fvec(rel_bias):
    rev = rel_bias[:, ::-1]
    n_heads = rel_bias.shape[0]
    first = CH_LOOK - REL_CLIP
    n_var = REL_CLIP + CHUNK
    clipped = rev[:, :1]
    fvec = jnp.concatenate([jnp.broadcast_to(clipped, (n_heads, first)), rev[:, :n_var],
                            jnp.broadcast_to(clipped, (n_heads, CH_WIN - first - n_var))], axis=1)
    return fvec.reshape(n_heads, 1, CH_WIN)


def _fvec_grad_to_rel_bias(dfvec):
    first = CH_LOOK - REL_CLIP
    n_var = REL_CLIP + CHUNK
    rev = jnp.pad(dfvec[:, 0, first:first + n_var], ((0, 0), (0, N_REL - n_var)))
    return rev[:, ::-1]


def _local_step(x, p, target, g, weights_for, grads_done, fvec):
    w, tie = weights_for(0, x)
    w = dict(w)
    h1, n1, a1, b1, f1 = _ffn_fwd(x, g["ffn1_pre"] + tie, g["ffn1_post"],
                                  w["ffn1_gate"], w["ffn1_up"], w["ffn1_down"], name="ffn1_fwd")
    more, tie = weights_for(1, h1)
    w.update(more)
    qkv, u = _qkv_fwd(h1, g["mix_pre"] + tie, w["in"], name="qkv_fwd")
    bias = _bias_expand(fvec, name="bias_expand")
    o, ltot = _sb_fwd(qkv, name="sb_fwd")
    o = _ch_fwd(qkv, bias, o, name="ch_fwd")
    h2, mixed, y = _out_fwd(o, h1, g["out_sb"], g["out_ch"], g["mix_post"], w["out"], name="out_fwd")
    w.update(weights_for(2, h2)[0])
    h3, n2, a2, b2, f2 = _ffn_fwd(h2, g["ffn2_pre"], g["ffn2_post"],
                                  w["ffn2_gate"], w["ffn2_up"], w["ffn2_down"], name="ffn2_fwd")
    loss, dh3, dwp, dwgate, dg_ple = _ple(p, h3, target, w["ple_proj"], w["ple_gate"],
                                          g["ple_post"], name="ple")
    tie = grads_done(0, {"ple_proj": dwp, "ple_gate": dwgate})

    df2, dg_ffn2_post = _junction(dh3, post=(f2, g["ffn2_post"] + tie, 0.5), name="junction3")
    dwg2, dwu2, dwd2, dn2 = _ffn_bwd(n2, df2, a2, b2, w["ffn2_gate"], w["ffn2_up"],
                                     w["ffn2_down"], name="ffn2_bwd")
    tie = grads_done(1, {"ffn2_gate": dwg2, "ffn2_up": dwu2, "ffn2_down": dwd2})
    dh2, dg_ffn2_pre, dy, dg_mix_post = _junction(
        dh3, pre=(dn2, h2, g["ffn2_pre"] + tie), post=(y, g["mix_post"], 1.0), name="junction2")
    dwout, do, dg_sb, dg_ch = _out_bwd(dy, mixed, o, g["out_sb"], g["out_ch"], w["out"],
                                       name="out_bwd")
    dq, dk, dv = _sb_bwd(qkv, ltot, do, name="sb_bwd")
    dq, dk, dv, dbias = _ch_bwd(qkv, bias, o, do, dq, dk, dv, name="ch_bwd")
    dfvec = _bias_grad(dbias, name="bias_grad")
    dwin, du = _qkv_bwd(dq, dk, dv, u, w["in"], name="qkv_bwd")
    tie = grads_done(2, {"out": dwout, "in": dwin})
    dh1, dg_mix_pre, df1, dg_ffn1_post = _junction(
        dh2, pre=(du, h1, g["mix_pre"] + tie), post=(f1, g["ffn1_post"], 0.5), name="junction1")
    dwg1, dwu1, dwd1, dn1 = _ffn_bwd(n1, df1, a1, b1, w["ffn1_gate"], w["ffn1_up"],
                                     w["ffn1_down"], name="ffn1_bwd")
    tie = grads_done(3, {"ffn1_gate": dwg1, "ffn1_up": dwu1, "ffn1_down": dwd1})
    dx, dg_ffn1_pre = _junction(dh1, pre=(dn1, x, g["ffn1_pre"] + tie), name="junction0")

    dg = {"ffn1_pre": dg_ffn1_pre, "ffn1_post": dg_ffn1_post, "mix_pre": dg_mix_pre,
          "mix_post": dg_mix_post, "out_sb": dg_sb, "out_ch": dg_ch,
          "ffn2_pre": dg_ffn2_pre, "ffn2_post": dg_ffn2_post, "ple_post": dg_ple}
    return loss, dx, dg, dfvec


_WEIGHTS = (
    ("ffn1_gate", "row", FF_SHARD, FF_SHARD_PAD, D_MODEL),
    ("ffn1_up", "row", FF_SHARD, FF_SHARD_PAD, D_MODEL),
    ("ffn1_down", "row", FF_SHARD, FF_SHARD_PAD, D_MODEL),
    ("in", "col", QKV_SHARD, QKV_SHARD, D_MODEL),
    ("out", "row", ROW_SHARD, ROW_SHARD, D_MODEL),
    ("ffn2_gate", "row", FF_SHARD, FF_SHARD_PAD, D_MODEL),
    ("ffn2_up", "row", FF_SHARD, FF_SHARD_PAD, D_MODEL),
    ("ffn2_down", "row", FF_SHARD, FF_SHARD_PAD, D_MODEL),
    ("ple_proj", "col", ROW_SHARD, ROW_SHARD, PLE_DIM),
    ("ple_gate", "row", ROW_SHARD, ROW_SHARD, D_MODEL),
)
_TRANSPOSED = ("ffn1_gate", "ffn1_up", "ffn2_gate", "ffn2_up")
_SPEC = {n: (kind, valid, pad, other) for n, kind, valid, pad, other in _WEIGHTS}
_GATHER_STAGES = (("ffn1_gate", "ffn1_up", "ffn1_down"), ("in", "out"),
                  ("ffn2_gate", "ffn2_up", "ffn2_down", "ple_proj", "ple_gate"))
_SCATTER_STAGES = (("ple_proj", "ple_gate"), ("ffn2_gate", "ffn2_up", "ffn2_down"),
                   ("out", "in"), ("ffn1_gate", "ffn1_up", "ffn1_down"))
HBM = pl.BlockSpec(memory_space=pltpu.HBM)
SEM = pl.BlockSpec(memory_space=pltpu.SEMAPHORE)
EFFECT = pltpu.SideEffectType.DATAFLOW_SIDE_EFFECTING


def _shard_shape(kind, size, other):
    return (other, size) if kind == "col" else (size, other)


def _window(ref, kind, start, size):
    return ref.at[:, pl.ds(start, size)] if kind == "col" else ref.at[pl.ds(start, size), :]


def _device_tuple(k):
    return (k // 4, (k // 2) % 2, k % 2)


def _my_index():
    return 4 * lax.axis_index("x") + 2 * lax.axis_index("y") + lax.axis_index("c")


def _pack_weights(shards):
    nw = len(_WEIGHTS)

    def body(*refs):
        ins, packed, full = refs[:nw], refs[nw:2 * nw], refs[2 * nw:3 * nw]
        sem = refs[3 * nw]
        me = _my_index()
        for (_, kind, valid, pad, _), src, dst in zip(_WEIGHTS, ins, packed):
            if pad != valid:
                dst[...] = jnp.zeros_like(dst)
            if kind == "col":
                dst[:, pl.ds(0, valid)] = _bf(src[...])
            else:
                dst[pl.ds(0, valid), :] = _bf(src[...])
        for k in range(N_DEV):
            @pl.when(me == k)
            def _():
                for w, (_, kind, _, pad, _) in enumerate(_WEIGHTS):
                    pltpu.make_async_copy(packed[w], _window(full[w], kind, k * pad, pad),
                                          sem.at[w]).start()
        for w, (_, kind, _, pad, _) in enumerate(_WEIGHTS):
            pltpu.make_async_copy(packed[w], _window(full[w], kind, 0, pad), sem.at[w]).wait()

    whole = lambda shape: pl.BlockSpec(shape, lambda i: (0, 0))
    packed_shapes = [_shard_shape(kind, pad, other) for _, kind, _, pad, other in _WEIGHTS]
    outs = pl.pallas_call(
        body, name="pack_weights", grid=(1,),
        in_specs=[whole(a.shape) for a in shards],
        out_specs=[whole(s) for s in packed_shapes] + [ANY] * nw,
        out_shape=[_sds(s, BF16) for s in packed_shapes]
        + [_sds(_shard_shape(kind, N_DEV * pad, other), BF16) for _, kind, _, pad, other in _WEIGHTS],
        scratch_shapes=[pltpu.SemaphoreType.DMA((nw,))],
        compiler_params=_params("arbitrary"),
    )(*shards)
    names = [n for n, *_ in _WEIGHTS]
    return dict(zip(names, outs[:nw])), dict(zip(names, outs[nw:]))


def _hbm(a):
    return pltpu.with_memory_space_constraint(a, pltpu.HBM)


def _split_start(name, n, body_copies, sources, lands, after):
    arrays = list(sources) + list(lands)
    ns, na = len(sources), len(arrays)

    def body(*refs):
        src, land = refs[:ns], refs[ns:na]
        send, recv = refs[na + 1], refs[na + 2]
        token = refs[-1]
        body_copies(src, land, send, recv)
        token[...] = jnp.zeros_like(token)

    out = pl.pallas_call(
        body, name=name,
        out_shape=(pltpu.SemaphoreType.DMA((n,)), pltpu.SemaphoreType.DMA((n,)),
                   *[pltpu.HBM(a.shape, a.dtype) for a in arrays], _sds((8, 128))),
        in_specs=[HBM] * na + [ANY], out_specs=(SEM, SEM, *[HBM] * na, VMEM),
        input_output_aliases={i: 2 + i for i in range(na)},
        compiler_params=pltpu.CompilerParams(has_side_effects=EFFECT),
    )(*[_hbm(a) for a in arrays], after)
    return out[0], out[1], out[2:2 + ns], out[2 + ns:2 + na], out[-1]


def _split_wait(name, n, seven_of, send, recv, sources, lands, after, keep_sources=False):
    arrays = list(sources) + list(lands)
    ns, na = len(sources), len(arrays)

    def body(*refs):
        land = refs[ns:na]
        send_ref, recv_ref = refs[na], refs[na + 1]
        myself = (lax.axis_index("x"), lax.axis_index("y"), lax.axis_index("c"))
        for w in range(n):
            seven = seven_of(w, land[w])
            copy = pltpu.make_async_remote_copy(
                src_ref=seven, dst_ref=seven, send_sem=send_ref.at[w], recv_sem=recv_ref.at[w],
                device_id=myself, device_id_type=MESH)
            copy.wait_send()
            copy.wait_recv()

    out = pl.pallas_call(
        body, name=name,
        out_shape=[pltpu.HBM(a.shape, a.dtype) for a in arrays],
        in_specs=[HBM] * na + [SEM, SEM, ANY], out_specs=[HBM] * na,
        input_output_aliases={i: i for i in range(na)},
        compiler_params=pltpu.CompilerParams(has_side_effects=EFFECT),
    )(*arrays, send, recv, after)
    return out if keep_sources else out[ns:]


_ALL_PEERS = (1, 2, 3, 4, 5, 6, 7)
_NEAR_PEERS = (1, 2, 4, 6)
_FAR_CHIPS = (2, 4, 6)


def _gather_start(stage, names, packed, full, after, peers=_ALL_PEERS):
    def copies(src, land, send, recv):
        me = _my_index()
        for k in range(N_DEV):
            @pl.when(me == k)
            def _():
                for w, name in enumerate(names):
                    kind, _, pad, _ = _SPEC[name]
                    dst = _window(land[w], kind, k * pad, pad)
                    for mask in peers:
                        pltpu.make_async_remote_copy(
                            src_ref=src[w], dst_ref=dst, send_sem=send.at[w],
                            recv_sem=recv.at[w], device_id=_device_tuple(k ^ mask),
                            device_id_type=MESH).start()

    return _split_start(f"gather_start{stage}", len(names), copies,
                        [packed[n] for n in names], [full[n] for n in names], after)


def _gather_wait(stage, names, started, after, count=N_DEV - 1):
    send, recv, src, land, _ = started

    def bytes_of(w, ref):
        kind, _, pad, _ = _SPEC[names[w]]
        return _window(ref, kind, 0, count * pad)

    return dict(zip(names, _split_wait(f"gather_wait{stage}", len(names), bytes_of,
                                       send, recv, src, land, after)))


def _relay_start(stage, names, full, after):
    def copies(_, land, send, recv):
        me = _my_index()
        for k in range(N_DEV):
            @pl.when(me == k)
            def _():
                for w, name in enumerate(names):
                    kind, _, pad, _ = _SPEC[name]
                    for mask in _FAR_CHIPS:
                        win = _window(land[w], kind, (k ^ mask) * pad, pad)
                        pltpu.make_async_remote_copy(
                            src_ref=win, dst_ref=win, send_sem=send.at[w], recv_sem=recv.at[w],
                            device_id=_device_tuple(k ^ 1), device_id_type=MESH).start()

    return _split_start(f"relay_start{stage}", len(names), copies, [],
                        [full[n] for n in names], after)


def _scatter_start(stage, names, grads, after):
    def copies(src, land, send, recv):
        me = _my_index()
        for k in range(N_DEV):
            @pl.when(me != k)
            def _():
                slot = lax.rem(me + (N_DEV - 1 - k), N_DEV)
                for w, name in enumerate(names):
                    kind, _, pad, _ = _SPEC[name]
                    pltpu.make_async_remote_copy(
                        src_ref=_window(src[w], kind, k * pad, pad), dst_ref=land[w].at[slot],
                        send_sem=send.at[w], recv_sem=recv.at[w],
                        device_id=_device_tuple(k), device_id_type=MESH).start()

    lands = [lax.empty((N_DEV - 1,) + _shard_shape(_SPEC[m][0], _SPEC[m][2], _SPEC[m][3]), BF16)
             for m in names]
    return _split_start(f"scatter_start{stage}", len(names), copies, grads, lands, after)


def _scatter_wait(stage, names, started, after):
    send, recv, src, land, _ = started
    n = len(names)
    out = _split_wait(f"scatter_wait{stage}", n, lambda w, ref: ref, send, recv, src, land, after,
                      keep_sources=True)
    return dict(zip(names, out[:n])), dict(zip(names, out[n:]))


N_CHIPS = N_DEV // 2


def _pair_start(stage, names, grads, after):
    def copies(src, land, send, recv):
        me = _my_index()
        for k in range(N_DEV):
            @pl.when(me == k)
            def _():
                for w, name in enumerate(names):
                    kind, _, pad, _ = _SPEC[name]
                    for chip in range(N_CHIPS):
                        j = 2 * chip + ((k ^ 1) & 1)
                        pltpu.make_async_remote_copy(
                            src_ref=_window(src[w], kind, j * pad, pad), dst_ref=land[w].at[chip],
                            send_sem=send.at[w], recv_sem=recv.at[w],
                            device_id=_device_tuple(k ^ 1), device_id_type=MESH).start()

    lands = [lax.empty((N_CHIPS,) + _shard_shape(_SPEC[m][0], _SPEC[m][2], _SPEC[m][3]), BF16)
             for m in names]
    return _split_start(f"pair_start{stage}", len(names), copies, grads, lands, after)


def _pair_sum(dw_full, pair, *, pad, name):
    other = dw_full.shape[1]

    def body(own_ref, pair_ref, out_ref):
        out_ref[0] = _bf(own_ref[...].astype(F32) + pair_ref[0].astype(F32))

    slot = pl.BlockSpec((1, pad, other), lambda q: (q, 0, 0))
    return pl.pallas_call(
        body, name=name, grid=(N_CHIPS,),
        in_specs=[pl.BlockSpec((pad, other), lambda q: (2 * q + lax.axis_index("c"), 0)), slot],
        out_specs=slot, out_shape=_sds((N_CHIPS, pad, other), BF16),
        compiler_params=_params("arbitrary"),
    )(dw_full, pair)


def _chip_start(stage, names, sums, after):
    def copies(src, land, send, recv):
        me = _my_index()
        my_chip = lax.shift_right_logical(me, 1)
        for k in range(N_DEV):
            @pl.when((me != k) & (((me ^ k) & 1) == 0))
            def _():
                slot = lax.rem(my_chip + (N_CHIPS - 1 - k // 2), N_CHIPS)
                for w in range(len(names)):
                    pltpu.make_async_remote_copy(
                        src_ref=src[w].at[k // 2], dst_ref=land[w].at[slot],
                        send_sem=send.at[w], recv_sem=recv.at[w],
                        device_id=_device_tuple(k), device_id_type=MESH).start()

    lands = [lax.empty((N_CHIPS - 1,) + a.shape[1:], BF16) for a in sums]
    return _split_start(f"chip_start{stage}", len(names), copies, sums, lands, after)


def _adamw_chip(w, m, v, land, sums, *, name):
    shape = w.shape

    def body(w_ref, m_ref, v_ref, land_ref, own_ref, *outs):
        rows = pl.ds(0, shape[0])
        grad = own_ref[0, rows, :].astype(F32)
        for s in range(N_CHIPS - 1):
            grad = grad + land_ref[s, rows, :].astype(F32)
        _adam_update(w_ref, m_ref, v_ref, grad, *outs)

    whole = lambda a: pl.BlockSpec(a.shape, lambda i: (0,) * a.ndim)
    own = pl.BlockSpec((1,) + sums.shape[1:],
                       lambda i: (2 * lax.axis_index("x") + lax.axis_index("y"), 0, 0))
    return pl.pallas_call(
        body, name=name, grid=(1,),
        in_specs=[whole(w), whole(m), whole(v), whole(land), own],
        out_specs=[whole(w)] * 4, out_shape=[_sds(shape)] * 4,
        compiler_params=_params("arbitrary"),
    )(w, m, v, land, sums)


def _allreduce_small(small, after):
    shape = small.shape

    def body(in_ref, _after, out_ref, gath, send, recv):
        me = _my_index()
        for k in range(N_DEV):
            @pl.when(me != k)
            def _():
                pltpu.make_async_remote_copy(
                    src_ref=in_ref, dst_ref=gath.at[me], send_sem=send, recv_sem=recv,
                    device_id=_device_tuple(k), device_id_type=MESH).start()

            @pl.when(me == k)
            def _():
                gath[k] = in_ref[...]
        seven = gath.at[pl.ds(0, N_DEV - 1)]
        pltpu.make_async_remote_copy(
            src_ref=seven, dst_ref=seven, send_sem=send, recv_sem=recv,
            device_id=_device_tuple(0), device_id_type=MESH).wait()
        total = gath[0]
        for s in range(1, N_DEV):
            total = total + gath[s]
        out_ref[...] = total

    return pl.pallas_call(
        body, name="allreduce_small",
        in_specs=[VMEM, ANY], out_specs=VMEM, out_shape=_sds(shape),
        scratch_shapes=[pltpu.VMEM((N_DEV,) + shape, F32),
                        pltpu.SemaphoreType.DMA, pltpu.SemaphoreType.DMA],
    )(small, after)


def _adam_update(w_ref, m_ref, v_ref, grad, grad_ref, delta_ref, nm_ref, nv_ref):
    new_m = ADAM_B1 * m_ref[...] + (1.0 - ADAM_B1) * grad
    new_v = ADAM_B2 * v_ref[...] + (1.0 - ADAM_B2) * (grad * grad)
    m_hat = new_m / (1.0 - ADAM_B1 ** ADAM_STEP)
    v_hat = new_v / (1.0 - ADAM_B2 ** ADAM_STEP)
    grad_ref[...] = grad
    delta_ref[...] = -ADAM_LR * (m_hat / (jnp.sqrt(v_hat) + ADAM_EPS) + ADAM_WD * w_ref[...])
    nm_ref[...] = new_m
    nv_ref[...] = new_v


def _adamw(w, m, v, g, *, name):
    def body(w_ref, m_ref, v_ref, g_ref, *outs):
        _adam_update(w_ref, m_ref, v_ref, g_ref[...], *outs)

    whole = pl.BlockSpec(w.shape, lambda i: (0,) * w.ndim)
    return pl.pallas_call(
        body, name=name, grid=(1,), in_specs=[whole] * 4, out_specs=[whole] * 4,
        out_shape=[_sds(w.shape)] * 4, compiler_params=_params("arbitrary"),
    )(w, m, v, g)


def _adamw_gains(small, params):
    n = len(params)

    def body(small_ref, *refs):
        ins, outs = refs[:3 * n], refs[3 * n:]
        for r in range(n):
            width = ins[3 * r].shape[1]
            if width == D_MODEL:
                grad = small_ref[pl.ds(r, 1), :]
            else:
                grad = small_ref[pl.ds(len(_GAINS), 1), pl.ds((r - len(_GAINS)) * width, width)]
            _adam_update(*ins[3 * r:3 * r + 3], grad, *outs[4 * r:4 * r + 4])

    whole = lambda a: pl.BlockSpec(a.shape, lambda i: (0, 0))
    flat = [a for group in params for a in group]
    return pl.pallas_call(
        body, name="adamw_gains", grid=(1,),
        in_specs=[whole(small)] + [whole(a) for a in flat],
        out_specs=[whole(w) for w, _, _ in params for _ in range(4)],
        out_shape=[_sds(w.shape) for w, _, _ in params for _ in range(4)],
        compiler_params=_params("arbitrary"),
    )(small, *flat)


def _adamw_shard(w, m, v, land, dw_full, *, kind, pad, name):
    shape = w.shape
    other = shape[0] if kind == "col" else shape[1]

    def body(w_ref, m_ref, v_ref, land_ref, own_ref, *outs):
        valid = ((slice(None), pl.ds(0, shape[1])) if kind == "col"
                 else (pl.ds(0, shape[0]), slice(None)))
        grad = own_ref[valid].astype(F32)
        for s in range(N_DEV - 1):
            grad = grad + land_ref[(s,) + valid].astype(F32)
        _adam_update(w_ref, m_ref, v_ref, grad, *outs)

    whole = lambda a: pl.BlockSpec(a.shape, lambda i: (0,) * a.ndim)
    own = pl.BlockSpec(_shard_shape(kind, pad, other),
                       (lambda i: (0, _my_index())) if kind == "col" else (lambda i: (_my_index(), 0)))
    return pl.pallas_call(
        body, name=name, grid=(1,),
        in_specs=[whole(w), whole(m), whole(v), whole(land), own],
        out_specs=[whole(w)] * 4, out_shape=[_sds(shape)] * 4,
        compiler_params=_params("arbitrary"),
    )(w, m, v, land, dw_full)


_GAINS = ("ffn1_pre", "ffn1_post", "mix_pre", "mix_post", "ffn2_pre", "ffn2_post", "ple_post")
_SMALL_ROWS = 16


def _stack_gains(get):
    return jnp.concatenate([get(n) for n in _GAINS]
                           + [jnp.concatenate([get("out_sb"), get("out_ch")], axis=1)], axis=0)


def kernel(x, p, g_ffn1_pre, g_ffn1_post, w_ffn1_gate, w_ffn1_up, w_ffn1_down, g_mix_pre, g_mix_post, w_in, g_out_sb, g_out_ch, rel_bias, w_out, g_ffn2_pre, g_ffn2_post, w_ffn2_gate, w_ffn2_up, w_ffn2_down, w_ple_proj, w_ple_gate, g_ple_post, loss_target, m_g_ffn1_pre, m_g_ffn1_post, m_w_ffn1_gate, m_w_ffn1_up, m_w_ffn1_down, m_g_mix_pre, m_g_mix_post, m_w_in, m_g_out_sb, m_g_out_ch, m_rel_bias, m_w_out, m_g_ffn2_pre, m_g_ffn2_post, m_w_ffn2_gate, m_w_ffn2_up, m_w_ffn2_down, m_w_ple_proj, m_w_ple_gate, m_g_ple_post, v_g_ffn1_pre, v_g_ffn1_post, v_w_ffn1_gate, v_w_ffn1_up, v_w_ffn1_down, v_g_mix_pre, v_g_mix_post, v_w_in, v_g_out_sb, v_g_out_ch, v_rel_bias, v_w_out, v_g_ffn2_pre, v_g_ffn2_post, v_w_ffn2_gate, v_w_ffn2_up, v_w_ffn2_down, v_w_ple_proj, v_w_ple_gate, v_g_ple_post):
    given = dict(locals())
    wnames = [n for n, *_ in _WEIGHTS]

    def shard(prefix, n):
        a = given[prefix + "w_" + n][0]
        return a.T if n in _TRANSPOSED else a

    packed, full = _pack_weights([shard("", n) for n in wnames])
    first = _GATHER_STAGES[0]
    anchor = x[0]
    two_level = (0, 2)
    gathers = {}

    def start_stage(stage, after):
        peers = _NEAR_PEERS if stage in two_level else _ALL_PEERS
        gathers[stage] = _gather_start(stage, _GATHER_STAGES[stage], packed, full, after,
                                       peers=peers)

    start_stage(0, anchor)

    def weights_for(stage, after):
        names = _GATHER_STAGES[stage]
        last_stage = stage + 1 == len(_GATHER_STAGES)
        count = len(_NEAR_PEERS) if stage in two_level else N_DEV - 1
        ws = _gather_wait(stage, names, gathers[stage], after, count=count)
        if not last_stage:
            start_stage(stage + 1, ws[names[0]])
        if stage in two_level:
            relay = _relay_start(stage, names, ws, anchor if last_stage else gathers[stage + 1][-1])
            ws = _gather_wait(f"{stage}r", names, relay, relay[-1], count=len(_FAR_CHIPS))
        if last_stage:
            return ws, jnp.zeros((1, 1), F32)
        return ws, gathers[stage + 1][-1][:1, :1]

    scatters = {}

    last = len(_SCATTER_STAGES) - 1

    def grads_done(stage, grads):
        names = _SCATTER_STAGES[stage]
        start = _pair_start if stage == last else _scatter_start
        scatters[stage] = start(stage, names, [grads[n] for n in names], anchor)
        return scatters[stage][-1][:1, :1]

    gains = {n: given["g_" + n] for n in _GAINS + ("out_sb", "out_ch")}
    fvec = _rel_bias_to_fvec(rel_bias[0])
    loss, dx, dg, dfvec = _local_step(x[0], p[0, 0], loss_target[0], gains,
                                      weights_for, grads_done, fvec)

    results = {}

    def finish(stage, after):
        names = _SCATTER_STAGES[stage]
        dws, lands = _scatter_wait(stage, names, scatters[stage], after)
        for n in names:
            kind, _, pad, _ = _SPEC[n]
            out = _adamw_shard(shard("", n), shard("m_", n), shard("v_", n), lands[n], dws[n],
                               kind=kind, pad=pad, name="adamw_" + n)
            results["w_" + n] = [a.T for a in out] if n in _TRANSPOSED else out
        return results["w_" + names[-1]][0]

    names = _SCATTER_STAGES[last]
    whole = lambda w, ref: ref
    send, recv, src, land, _ = scatters[last]
    out = _split_wait(f"pair_wait{last}", len(names), whole, send, recv, src, land, dx,
                      keep_sources=True)
    sums = [_pair_sum(dwf, pair, pad=_SPEC[n][2], name="pair_sum_" + n)
            for n, dwf, pair in zip(names, out[:len(names)], out[len(names):])]
    send, recv, src, land, after = _chip_start(last, names, sums, anchor)
    for stage in range(last):
        after = finish(stage, after)
    out = _split_wait(f"chip_wait{last}", len(names), whole, send, recv, src, land, after,
                      keep_sources=True)
    for n, own, landed in zip(names, out[:len(names)], out[len(names):]):
        res = _adamw_chip(shard("", n), shard("m_", n), shard("v_", n), landed, own,
                          name="adamw_" + n)
        results["w_" + n] = [a.T for a in res] if n in _TRANSPOSED else res
        after = res[0]
    loss_col = jnp.pad(loss[:, :1], ((0, N_DEV - 1), (0, D_MODEL - CH_WIN - 1)))
    dfv = jnp.concatenate([dfvec[:, 0, :], loss_col], axis=1)
    small = _allreduce_small(jnp.concatenate([_stack_gains(lambda n: dg[n]), dfv], axis=0), after)
    gain_names = _GAINS + ("out_sb", "out_ch")
    gain_out = _adamw_gains(small, [(given["g_" + n], given["m_g_" + n], given["v_g_" + n])
                                    for n in gain_names])
    for r, n in enumerate(gain_names):
        results["g_" + n] = gain_out[4 * r:4 * r + 4]
    d_rel = _fvec_grad_to_rel_bias(small[N_DEV:, :CH_WIN].reshape(N_DEV, 1, CH_WIN))
    results["rel_bias"] = _adamw(rel_bias[0], m_rel_bias[0], v_rel_bias[0], d_rel,
                                 name="adamw_rel_bias")

    order = ("g_ffn1_pre", "g_ffn1_post", "w_ffn1_gate", "w_ffn1_up", "w_ffn1_down",
             "g_mix_pre", "g_mix_post", "w_in", "g_out_sb", "g_out_ch", "rel_bias", "w_out",
             "g_ffn2_pre", "g_ffn2_post", "w_ffn2_gate", "w_ffn2_up", "w_ffn2_down",
             "w_ple_proj", "w_ple_gate", "g_ple_post")

    def leaf(name, idx):
        a = results[name][idx]
        return a if name.startswith("g_") else a[None]

    total_loss = small[N_DEV, CH_WIN]
    return (total_loss, dx[None],
            *[leaf(n, 0) for n in order], *[leaf(n, 1) for n in order],
            *[leaf(n, 2) for n in order], *[leaf(n, 3) for n in order])
```

```python
import jax
import jax.numpy as jnp
from jax import lax
from jax.experimental import pallas as pl
from jax.experimental.pallas import tpu as pltpu

F32 = jnp.float32
BF16 = jnp.bfloat16

N_DEV = 8
D_MODEL = 1024
D_FF = 2816
FF_SHARD = D_FF // N_DEV
FF_SHARD_PAD = 384
D_FF_PAD = FF_SHARD_PAD * N_DEV
QKV_WIDTH = 3 * D_MODEL
QKV_SHARD = QKV_WIDTH // N_DEV
PLE_DIM = 256
ROW_SHARD = D_MODEL // N_DEV
HEAD_DIM = 64
PAIR = 2 * HEAD_DIM
N_PAIRS = 4
CHUNK = 64
LOOKBACK = 8
REL_CLIP = 128
N_REL = 2 * REL_CLIP + 1
CH_QB = 256
CH_LOOK = LOOKBACK * CHUNK
CH_WIN = CH_LOOK + CH_QB
SB_QB = 512
SB_KB = 256
SB_GROUP = 2
SB_LANES = tuple(slice(g * 128, (g + 1) * 128) for g in range(SB_GROUP))
EPS = 1e-6
NEG_INF = -1e30
ATT_SCALE = HEAD_DIM ** -0.5
ADAM_LR = 0.001
ADAM_B1 = 0.9
ADAM_B2 = 0.999
ADAM_EPS = 1e-08
ADAM_WD = 0.01
ADAM_STEP = 10
VMEM_LIMIT_BYTES = 48 * 1024 * 1024
MESH = pl.DeviceIdType.MESH

ANY = pl.BlockSpec(memory_space=pl.ANY)
VMEM = pl.BlockSpec(memory_space=pltpu.VMEM)


def _params(*sem):
    return pltpu.CompilerParams(dimension_semantics=sem or None,
                                vmem_limit_bytes=VMEM_LIMIT_BYTES)


def _sds(shape, dtype=F32):
    return jax.ShapeDtypeStruct(shape, dtype)


def _bf(x):
    return x.astype(BF16)


def _dot(a, b):
    return jnp.dot(_bf(a), _bf(b), preferred_element_type=F32)


def _dot_nt(a, b):
    return lax.dot_general(_bf(a), _bf(b), (((1,), (1,)), ((), ())),
                           preferred_element_type=F32)


def _dot_tn(a, b):
    return lax.dot_general(_bf(a), _bf(b), (((0,), (0,)), ((), ())),
                           preferred_element_type=F32)


def _sigmoid(x):
    return 1.0 / (1.0 + jnp.exp(-x))


def _softplus(x):
    return jnp.maximum(x, 0.0) + jnp.log(1.0 + jnp.exp(-jnp.abs(x)))


def _rstd(x):
    return lax.rsqrt(jnp.mean(x * x, axis=-1, keepdims=True) + EPS)


def _rms(x, g):
    return x * _rstd(x) * g


def _rms_bwd(dy, x, g):
    r = _rstd(x)
    w = dy * g
    dx = r * (w - x * (r * r) * jnp.mean(w * x, axis=-1, keepdims=True))
    dg = jnp.sum(dy * (x * r), axis=0, keepdims=True)
    return dx, dg


def _head_masks():
    lane = lax.broadcasted_iota(jnp.int32, (1, PAIR), 1)
    return lane < HEAD_DIM, lane >= HEAD_DIM


def _ffn_fwd(x, g_pre, g_post, wg, wu, wd, *, name):
    t = x.shape[0]
    tm, tj = 512, 1024
    ni, nj = t // tm, D_FF_PAD // tj

    def body(x_ref, gpre_ref, gpost_ref, wg_ref, wu_ref, wd_ref,
             h_ref, n_ref, a_ref, b_ref, f_ref, acc_ref):
        j = pl.program_id(1)

        @pl.when(j == 0)
        def _():
            n_ref[...] = _bf(_rms(x_ref[...], gpre_ref[...]))
            acc_ref[...] = jnp.zeros_like(acc_ref)

        n = n_ref[...]
        a = _dot_nt(n, wg_ref[...])
        b = _dot_nt(n, wu_ref[...])
        a_ref[...] = a
        b_ref[...] = b
        hmid = a * _sigmoid(a) * b
        acc_ref[...] += jnp.dot(_bf(hmid), wd_ref[...], preferred_element_type=F32)

        @pl.when(j == nj - 1)
        def _():
            f = acc_ref[...]
            f_ref[...] = f
            h_ref[...] = x_ref[...] + 0.5 * _rms(f, gpost_ref[...])

    row = pl.BlockSpec((tm, D_MODEL), lambda i, j: (i, 0))
    gain = pl.BlockSpec((1, D_MODEL), lambda i, j: (0, 0))
    col = pl.BlockSpec((tm, tj), lambda i, j: (i, j))
    wtile = pl.BlockSpec((tj, D_MODEL), lambda i, j: (j, 0))
    return pl.pallas_call(
        body, name=name, grid=(ni, nj),
        in_specs=[row, gain, gain, wtile, wtile, wtile],
        out_specs=[row, row, col, col, row],
        out_shape=[_sds((t, D_MODEL)), _sds((t, D_MODEL), BF16),
                   _sds((t, D_FF_PAD)), _sds((t, D_FF_PAD)), _sds((t, D_MODEL))],
        scratch_shapes=[pltpu.VMEM((tm, D_MODEL), F32)],
        compiler_params=_params("arbitrary", "arbitrary"),
    )(x, g_pre, g_post, wg, wu, wd)


def _ffn_bwd(n, df, a, b, wg, wu, wd, *, name):
    t = n.shape[0]
    tj, tm, ts = 256, t, 512
    nj, ni, ns = D_FF_PAD // tj, t // tm, tm // ts

    def body(n_hbm, df_hbm, a_ref, b_ref, wg_ref, wu_ref, wd_ref,
             dwg_ref, dwu_ref, dwd_ref, dn_hbm,
             n_v, df_v, dn_v, ag, au, ad, sem):
        j, i = pl.program_id(0), pl.program_id(1)

        @pl.when((j == 0) & (i == 0))
        def _():
            c1 = pltpu.make_async_copy(n_hbm, n_v, sem.at[0])
            c2 = pltpu.make_async_copy(df_hbm, df_v, sem.at[1])
            c1.start()
            c2.start()
            dn_v[...] = jnp.zeros_like(dn_v)
            c1.wait()
            c2.wait()

        @pl.when(i == 0)
        def _():
            ag[...] = jnp.zeros_like(ag)
            au[...] = jnp.zeros_like(au)
            ad[...] = jnp.zeros_like(ad)

        wgj, wuj, wdj = wg_ref[...], wu_ref[...], wd_ref[...]
        for s in range(ns):
            local = pl.ds(s * ts, ts)
            rows = pl.ds(pl.multiple_of(i * tm + s * ts, ts), ts)
            av, bv = a_ref[local, :], b_ref[local, :]
            sig = _sigmoid(av)
            silu = av * sig
            dfr = df_v[rows, :]
            nr = n_v[rows, :]
            dhmid = _dot_nt(dfr, wdj)
            da = dhmid * bv * (sig * (1.0 + av * (1.0 - sig)))
            db = dhmid * silu
            ad[...] += _dot_tn(silu * bv, dfr)
            ag[...] += _dot_tn(da, nr)
            au[...] += _dot_tn(db, nr)
            dn_v[rows, :] += _dot(da, wgj) + _dot(db, wuj)

        @pl.when(i == ni - 1)
        def _():
            dwg_ref[...] = _bf(ag[...])
            dwu_ref[...] = _bf(au[...])
            dwd_ref[...] = _bf(ad[...])

        @pl.when((j == nj - 1) & (i == ni - 1))
        def _():
            c = pltpu.make_async_copy(dn_v, dn_hbm, sem.at[0])
            c.start()
            c.wait()

    roww = pl.BlockSpec((tj, D_MODEL), lambda j, i: (j, 0))
    act = pl.BlockSpec((tm, tj), lambda j, i: (i, j))
    return pl.pallas_call(
        body, name=name, grid=(nj, ni),
        in_specs=[ANY, ANY, act, act, roww, roww, roww],
        out_specs=[roww, roww, roww, ANY],
        out_shape=[_sds((D_FF_PAD, D_MODEL), BF16)] * 3 + [_sds((t, D_MODEL))],
        scratch_shapes=[pltpu.VMEM((t, D_MODEL), BF16), pltpu.VMEM((t, D_MODEL), BF16),
                        pltpu.VMEM((t, D_MODEL), F32)]
        + [pltpu.VMEM((tj, D_MODEL), F32)] * 3 + [pltpu.SemaphoreType.DMA((2,))],
        compiler_params=_params("arbitrary", "arbitrary"),
    )(n, df, a, b, wg, wu, wd)


def _junction(dres, pre=None, post=None, *, name):
    t = dres.shape[0]
    tm = 512
    ni = t // tm
    n_in = 1 + (3 if pre else 0) + (2 if post else 0)
    coef = post[2] if post else None

    def body(*refs):
        ins, outs = list(refs[:n_in]), list(refs[n_in:])
        i = pl.program_id(0)
        dh = ins.pop(0)[...]
        if pre:
            dn_ref, x_ref, gpre_ref = ins.pop(0), ins.pop(0), ins.pop(0)
            dh_ref, dgpre_ref = outs.pop(0), outs.pop(0)
            dx, dg = _rms_bwd(dn_ref[...], x_ref[...], gpre_ref[...])
            dh = dh + dx
            dh_ref[...] = dh

            @pl.when(i == 0)
            def _():
                dgpre_ref[...] = jnp.zeros_like(dgpre_ref)
            dgpre_ref[...] += dg
        if post:
            f_ref, gpost_ref = ins.pop(0), ins.pop(0)
            df_ref, dgpost_ref = outs.pop(0), outs.pop(0)
            df, dg = _rms_bwd(coef * dh, f_ref[...], gpost_ref[...])
            df_ref[...] = _bf(df)

            @pl.when(i == 0)
            def _():
                dgpost_ref[...] = jnp.zeros_like(dgpost_ref)
            dgpost_ref[...] += dg

    row = pl.BlockSpec((tm, D_MODEL), lambda i: (i, 0))
    gain = pl.BlockSpec((1, D_MODEL), lambda i: (0, 0))
    args, in_specs, out_specs, out_shape = [dres], [row], [], []
    if pre:
        args += list(pre)
        in_specs += [row, row, gain]
        out_specs += [row, gain]
        out_shape += [_sds((t, D_MODEL)), _sds((1, D_MODEL))]
    if post:
        args += [post[0], post[1]]
        in_specs += [row, gain]
        out_specs += [row, gain]
        out_shape += [_sds((t, D_MODEL), BF16), _sds((1, D_MODEL))]
    return pl.pallas_call(
        body, name=name, grid=(ni,), in_specs=in_specs, out_specs=out_specs,
        out_shape=out_shape, compiler_params=_params("arbitrary"),
    )(*args)


def _qkv_fwd(h, g, win, *, name):
    t = h.shape[0]
    tm, tn = min(1024, t), 1024
    ni, nj = t // tm, QKV_WIDTH // tn

    def body(h_ref, g_ref, w_ref, qkv_ref, u_ref):
        @pl.when(pl.program_id(1) == 0)
        def _():
            u_ref[...] = _bf(_rms(h_ref[...], g_ref[...]))
        qkv_ref[...] = jnp.dot(u_ref[...], w_ref[...], preferred_element_type=F32)

    row = pl.BlockSpec((tm, D_MODEL), lambda i, j: (i, 0))
    return pl.pallas_call(
        body, name=name, grid=(ni, nj),
        in_specs=[row, pl.BlockSpec((1, D_MODEL), lambda i, j: (0, 0)),
                  pl.BlockSpec((D_MODEL, tn), lambda i, j: (0, j))],
        out_specs=[pl.BlockSpec((tm, tn), lambda i, j: (i, j)), row],
        out_shape=[_sds((t, QKV_WIDTH)), _sds((t, D_MODEL), BF16)],
        compiler_params=_params("arbitrary", "arbitrary"),
    )(h, g, win)


def _qkv_bwd(dq, dk, dv, u, win, *, name):
    t = u.shape[0]
    tn, ts = 512, 512
    nj, ns = QKV_WIDTH // tn, t // ts

    def body(dq_ref, dk_ref, dv_ref, u_ref, w_ref, dw_ref, du_hbm, du_v, acc_ref, sem):
        j = pl.program_id(0)

        @pl.when(j == 0)
        def _():
            du_v[...] = jnp.zeros_like(du_v)

        wj = w_ref[...]
        for role, d_ref in enumerate((dq_ref, dk_ref, dv_ref)):
            @pl.when(j % 3 == role)
            def _():
                acc_ref[...] = jnp.zeros_like(acc_ref)
                for s in range(ns):
                    rows = pl.ds(s * ts, ts)
                    dcol = d_ref[rows, :]
                    acc_ref[...] += _dot_tn(u_ref[rows, :], dcol)
                    du_v[rows, :] += _dot_nt(dcol, wj)
                dw_ref[...] = _bf(acc_ref[...])

        @pl.when(j == nj - 1)
        def _():
            c = pltpu.make_async_copy(du_v, du_hbm, sem)
            c.start()
            c.wait()

    colw = pl.BlockSpec((D_MODEL, tn), lambda j: (0, j))
    grp = pl.BlockSpec((t, tn), lambda j: (0, j // 3))
    return pl.pallas_call(
        body, name=name, grid=(nj,),
        in_specs=[grp, grp, grp, pl.BlockSpec((t, D_MODEL), lambda j: (0, 0)), colw],
        out_specs=[colw, ANY],
        out_shape=[_sds((D_MODEL, QKV_WIDTH), BF16), _sds((t, D_MODEL))],
        scratch_shapes=[pltpu.VMEM((t, D_MODEL), F32), pltpu.VMEM((D_MODEL, tn), F32),
                        pltpu.SemaphoreType.DMA],
        compiler_params=_params("arbitrary"),
    )(dq, dk, dv, u, win)


def _sb_stack(x):
    lo, hi = _head_masks()
    return jnp.concatenate([jnp.where(lo, x, 0.0), jnp.where(hi, x, 0.0)], axis=0)


def _sb_unstack(x2, blk):
    return jnp.where(_head_masks()[0], x2[:blk], x2[blk:])


def _sb_rows_from(x2, blk, r0):
    return x2 if r0 == 0 else jnp.concatenate([x2[r0:blk], x2[blk + r0:]], axis=0)


def _sb_rows_merge(full2, sub2, blk, r0):
    if r0 == 0:
        return sub2
    rows = blk - r0
    return jnp.concatenate([full2[:r0], sub2[:rows], full2[blk:blk + r0], sub2[rows:]], axis=0)


def _sb_mask(qb, kb, offset):
    r = lax.broadcasted_iota(jnp.int32, (2 * qb, kb), 0) & (qb - 1)
    c = lax.broadcasted_iota(jnp.int32, (2 * qb, kb), 1) + offset
    return c < r


def _tri(n, keep):
    r = lax.broadcasted_iota(jnp.int32, (n, n), 0)
    c = lax.broadcasted_iota(jnp.int32, (n, n), 1)
    return jnp.where(keep(r, c), 1.0, 0.0).astype(BF16)


def _cumsum01(x, u):
    m = x.shape[0]
    hi = _bf(x)
    lo = _bf(x - hi.astype(F32))
    both = jnp.dot(jnp.concatenate([hi, lo], axis=0), u, preferred_element_type=F32)
    return both[:m] + both[m:]


def _sb_fwd(qkv, *, name):
    t = qkv.shape[0]
    blk, kb = min(SB_QB, t), SB_KB
    ni, per = t // blk, blk // kb

    def body(q_ref, k_ref, v_ref, o_ref, ltot_ref):
        i = pl.program_id(1)
        u_after = _tri(kb, lambda r, c: r > c)
        q2 = [_bf(_sb_stack(q_ref[:, lanes] * ATT_SCALE)) for lanes in SB_LANES]

        def tile(g, k0, mask, acc, c_l):
            kj = k_ref[pl.ds(k0, kb), SB_LANES[g]]
            vj = v_ref[pl.ds(k0, kb), SB_LANES[g]]
            z = _dot_nt(q2[g], kj)
            sp = _softplus(z)
            lf = -sp if mask is None else jnp.where(mask, -sp, 0.0)
            a = jnp.exp(z - sp + _cumsum01(lf, u_after) + c_l)
            if mask is not None:
                a = jnp.where(mask, a, 0.0)
            return acc + _dot(a, vj), c_l + jnp.sum(lf, axis=1, keepdims=True)

        def tiles(k0, mask, carry):
            return tuple(tile(g, k0, mask, *carry[g]) for g in range(SB_GROUP))

        carry = ((jnp.zeros((2 * blk, PAIR), F32), jnp.zeros((2 * blk, 1), F32)),) * SB_GROUP
        for d in reversed(range(per)):
            carry = tiles(pl.multiple_of(i * blk + d * kb, kb), _sb_mask(blk, kb, d * kb), carry)
        carry = lax.fori_loop(
            1, per * i + 1,
            lambda jj, c: tiles(pl.multiple_of((per * i - jj) * kb, kb), None, c), carry)
        for g, (acc, c_l) in enumerate(carry):
            o_ref[:, SB_LANES[g]] = _sb_unstack(acc, blk)
            ltot_ref[:, SB_LANES[g]] = _sb_unstack(jnp.broadcast_to(c_l, (2 * blk, PAIR)), blk)

    width = SB_GROUP * PAIR
    blkspec = pl.BlockSpec((blk, width), lambda p, i: (i, p))
    n_steps = N_PAIRS // SB_GROUP
    return pl.pallas_call(
        body, name=name, grid=(n_steps, ni),
        in_specs=[blkspec,
                  pl.BlockSpec((t, width), lambda p, i: (0, n_steps + p)),
                  pl.BlockSpec((t, width), lambda p, i: (0, 2 * n_steps + p))],
        out_specs=[blkspec, blkspec],
        out_shape=[_sds((t, D_MODEL)), _sds((t, D_MODEL // 2))],
        compiler_params=_params("arbitrary", "arbitrary"),
    )(qkv, qkv, qkv)


def _sb_bwd(qkv, ltot, do, *, name):
    t = qkv.shape[0]
    blk, kb = min(SB_QB, t), SB_KB
    ni, per = t // blk, blk // kb

    def body(q_ref, k_ref, v_ref, lt_ref, do_ref, dq_ref, dkout_ref, dvout_ref, dk_ref, dv_ref):
        i = pl.program_id(1)

        @pl.when(i == 0)
        def _():
            dk_ref[...] = jnp.zeros_like(dk_ref)
            dv_ref[...] = jnp.zeros_like(dv_ref)

        u_upto = _tri(kb, lambda r, c: r <= c)
        u_before = _tri(kb, lambda r, c: r < c)
        lane = lax.broadcasted_iota(jnp.int32, (1, PAIR), 1)
        q2 = [_bf(_sb_stack(q_ref[:, lanes] * ATT_SCALE)) for lanes in SB_LANES]
        do2 = [_bf(_sb_stack(do_ref[:, lanes])) for lanes in SB_LANES]
        total = [jnp.concatenate(
            [jnp.sum(jnp.where(lane == h * HEAD_DIM, lt_ref[:, lanes], 0.0), axis=1, keepdims=True)
             for h in range(2)], axis=0) for lanes in SB_LANES]

        def tile(g, ops, k0, mask, dq_acc, c_l, c_g):
            qg, dog, tot = ops
            krows = pl.ds(k0, kb)
            kj = k_ref[krows, SB_LANES[g]]
            vj = v_ref[krows, SB_LANES[g]]
            z = _dot_nt(qg, kj)
            sp = _softplus(z)
            sig = jnp.exp(z - sp)
            lf = -sp if mask is None else jnp.where(mask, -sp, 0.0)
            a = jnp.exp(z - sp + tot - (_cumsum01(lf, u_upto) + c_l))
            if mask is not None:
                a = jnp.where(mask, a, 0.0)
            gw = a * _dot_nt(dog, vj)
            g_before = jnp.dot(_bf(gw), u_before, preferred_element_type=F32) + c_g
            dz = gw * (1.0 - sig) - g_before * sig
            if mask is not None:
                dz = jnp.where(mask, dz, 0.0)
            dk_ref[krows, SB_LANES[g]] += _dot_tn(dz, qg)
            dv_ref[krows, SB_LANES[g]] += _dot_tn(a, dog)
            return (dq_acc + _dot(dz, kj), c_l + jnp.sum(lf, axis=1, keepdims=True),
                    c_g + jnp.sum(gw, axis=1, keepdims=True))

        def tiles(ops, k0, mask, carry):
            return tuple(tile(g, ops[g], k0, mask, *carry[g]) for g in range(SB_GROUP))

        ops = tuple(zip(q2, do2, total))
        zero = (jnp.zeros((2 * blk, PAIR), F32), jnp.zeros((2 * blk, 1), F32),
                jnp.zeros((2 * blk, 1), F32))
        carry = lax.fori_loop(
            0, per * i, lambda j, c: tiles(ops, pl.multiple_of(j * kb, kb), None, c),
            (zero,) * SB_GROUP)
        for d in range(per):
            r0 = d * kb
            sub = tiles(tuple(tuple(_sb_rows_from(a, blk, r0) for a in o) for o in ops),
                        pl.multiple_of(i * blk + r0, kb), _sb_mask(blk - r0, kb, 0),
                        tuple(tuple(_sb_rows_from(a, blk, r0) for a in c) for c in carry))
            carry = tuple(tuple(_sb_rows_merge(a, s, blk, r0) for a, s in zip(c, cs))
                          for c, cs in zip(carry, sub))
        for g, (dq_acc, _, _) in enumerate(carry):
            dq_ref[:, SB_LANES[g]] = _bf(_sb_unstack(dq_acc, blk) * ATT_SCALE)

        @pl.when(i == ni - 1)
        def _():
            dkout_ref[...] = _bf(dk_ref[...])
            dvout_ref[...] = _bf(dv_ref[...])

    width = SB_GROUP * PAIR
    n_steps = N_PAIRS // SB_GROUP
    blkspec = lambda off: pl.BlockSpec((blk, width), lambda p, i: (i, off + p))
    full = lambda off: pl.BlockSpec((t, width), lambda p, i: (0, off + p))
    return pl.pallas_call(
        body, name=name, grid=(n_steps, ni),
        in_specs=[blkspec(0), full(n_steps), full(2 * n_steps), blkspec(0), blkspec(0)],
        out_specs=[blkspec(0), full(0), full(0)],
        out_shape=[_sds((t, D_MODEL), BF16)] * 3,
        scratch_shapes=[pltpu.VMEM((t, width), F32), pltpu.VMEM((t, width), F32)],
        compiler_params=_params("arbitrary", "arbitrary"),
    )(qkv, qkv, qkv, ltot, do)


def _ch_mask(i):
    r = lax.broadcasted_iota(jnp.int32, (CH_QB, CH_WIN), 0)
    c = lax.broadcasted_iota(jnp.int32, (CH_QB, CH_WIN), 1)
    qc = LOOKBACK + lax.shift_right_arithmetic(r, 6)
    kc = lax.shift_right_arithmetic(c, 6)
    first = i * (CH_QB // CHUNK) - LOOKBACK
    return (kc <= qc) & (kc >= qc - LOOKBACK) & (kc + first >= 0)


def _ch_probs(qm, kw, bias_h, mask):
    z = _dot_nt(qm, kw) * ATT_SCALE + bias_h
    z = jnp.where(mask, z, NEG_INF)
    e = jnp.exp(z - jnp.max(z, axis=1, keepdims=True))
    return e / jnp.sum(e, axis=1, keepdims=True)


def _ch_fill(pad_ref, src_ref, t):
    pad_ref[pl.ds(0, CH_LOOK), :] = jnp.zeros((CH_LOOK, PAIR), BF16)
    pad_ref[pl.ds(CH_LOOK, t), :] = _bf(src_ref[...])


def _ch_fwd(qkv, bias, o_in, *, name):
    t = qkv.shape[0]
    ni = t // CH_QB

    def body(q_ref, k_ref, v_ref, bias_ref, _alias, o_ref, kpad, vpad):
        i = pl.program_id(1)

        @pl.when(i == 0)
        def _():
            _ch_fill(kpad, k_ref, t)
            _ch_fill(vpad, v_ref, t)

        win = pl.ds(pl.multiple_of(i * CH_QB, CH_QB), CH_WIN)
        kw, vw = kpad[win, :], vpad[win, :]
        mask = _ch_mask(i)
        q = q_ref[...]
        outs = []
        for h, hm in enumerate(_head_masks()):
            p = _ch_probs(jnp.where(hm, q, 0.0), kw, bias_ref[h], mask)
            outs.append(_dot(p, vw))
        o_ref[...] = jnp.where(_head_masks()[0], outs[0], outs[1])

    full = lambda off: pl.BlockSpec((t, PAIR), lambda p, i: (0, off + p))
    return pl.pallas_call(
        body, name=name, grid=(N_PAIRS, ni),
        in_specs=[pl.BlockSpec((CH_QB, PAIR), lambda p, i: (i, 3 * N_PAIRS + p)),
                  full(4 * N_PAIRS), full(5 * N_PAIRS),
                  pl.BlockSpec((2, CH_QB, CH_WIN), lambda p, i: (p, 0, 0)), ANY],
        out_specs=pl.BlockSpec((CH_QB, PAIR), lambda p, i: (i, N_PAIRS + p)),
        out_shape=_sds((t, D_MODEL)),
        scratch_shapes=[pltpu.VMEM((t + CH_LOOK, PAIR), BF16)] * 2,
        input_output_aliases={4: 0},
        compiler_params=_params("arbitrary", "arbitrary"),
    )(qkv, qkv, qkv, bias, o_in)


def _ch_bwd(qkv, bias, o, do, dq_in, dk_in, dv_in, *, name):
    t = qkv.shape[0]
    ni = t // CH_QB

    def body(q_ref, k_ref, v_ref, bias_ref, o_ref, do_ref, _a0, _a1, _a2,
             dq_ref, dkout_ref, dvout_ref, dbias_ref, kpad, vpad, dkpad, dvpad):
        i = pl.program_id(1)

        @pl.when(i == 0)
        def _():
            _ch_fill(kpad, k_ref, t)
            _ch_fill(vpad, v_ref, t)
            dkpad[...] = jnp.zeros_like(dkpad)
            dvpad[...] = jnp.zeros_like(dvpad)
            dbias_ref[...] = jnp.zeros_like(dbias_ref)

        win = pl.ds(pl.multiple_of(i * CH_QB, CH_QB), CH_WIN)
        kw, vw = kpad[win, :], vpad[win, :]
        mask = _ch_mask(i)
        q, o_blk, do_blk = q_ref[...], o_ref[...], do_ref[...]
        dqs = []
        for h, hm in enumerate(_head_masks()):
            qm = _bf(jnp.where(hm, q, 0.0))
            dom = jnp.where(hm, do_blk, 0.0)
            delta = jnp.sum(dom * o_blk, axis=1, keepdims=True)
            dom = _bf(dom)
            p = _ch_probs(qm, kw, bias_ref[h], mask)
            ds = p * (_dot_nt(dom, vw) - delta)
            dbias_ref[h] += ds
            dsz = ds * ATT_SCALE
            dqs.append(_dot(dsz, kw))
            dkpad[win, :] += _dot_tn(dsz, qm)
            dvpad[win, :] += _dot_tn(p, dom)
        dq_ref[...] = _bf(jnp.where(_head_masks()[0], dqs[0], dqs[1]))

        @pl.when(i == ni - 1)
        def _():
            dkout_ref[...] = _bf(dkpad[pl.ds(CH_LOOK, t), :])
            dvout_ref[...] = _bf(dvpad[pl.ds(CH_LOOK, t), :])

    blkspec = lambda off: pl.BlockSpec((CH_QB, PAIR), lambda p, i: (i, off + p))
    full = lambda off: pl.BlockSpec((t, PAIR), lambda p, i: (0, off + p))
    bias_spec = pl.BlockSpec((2, CH_QB, CH_WIN), lambda p, i: (p, 0, 0))
    return pl.pallas_call(
        body, name=name, grid=(N_PAIRS, ni),
        in_specs=[blkspec(3 * N_PAIRS), full(4 * N_PAIRS), full(5 * N_PAIRS), bias_spec,
                  blkspec(N_PAIRS), blkspec(N_PAIRS), ANY, ANY, ANY],
        out_specs=[blkspec(N_PAIRS), full(N_PAIRS), full(N_PAIRS), bias_spec],
        out_shape=[_sds((t, D_MODEL), BF16)] * 3 + [_sds((2 * N_PAIRS, CH_QB, CH_WIN))],
        scratch_shapes=[pltpu.VMEM((t + CH_LOOK, PAIR), BF16)] * 2
        + [pltpu.VMEM((t + CH_LOOK, PAIR), F32)] * 2,
        input_output_aliases={6: 0, 7: 1, 8: 2},
        compiler_params=_params("arbitrary", "arbitrary"),
    )(qkv, qkv, qkv, bias, o, do, dq_in, dk_in, dv_in)


def _bias_expand(fvec, *, name):
    n_heads = fvec.shape[0]

    def body(f_ref, o_ref, rows8):
        row = f_ref[0]
        for r in range(8):
            rows8[pl.ds(r, 1), :] = pltpu.roll(row, r, 1)
        base = rows8[...]
        for blk in range(CH_QB // 8):
            o_ref[0, pl.ds(8 * blk, 8), :] = pltpu.roll(base, 8 * blk, 1)

    return pl.pallas_call(
        body, name=name, grid=(n_heads,),
        in_specs=[pl.BlockSpec((1, 1, CH_WIN), lambda h: (h, 0, 0))],
        out_specs=pl.BlockSpec((1, CH_QB, CH_WIN), lambda h: (h, 0, 0)),
        out_shape=_sds((n_heads, CH_QB, CH_WIN)),
        scratch_shapes=[pltpu.VMEM((8, CH_WIN), F32)],
        compiler_params=_params("arbitrary"),
    )(fvec)


def _bias_grad(dbias, *, name):
    n_heads = dbias.shape[0]
    first = CH_LOOK - REL_CLIP

    def body(d_ref, o_ref, acc8):
        acc = jnp.zeros((8, CH_WIN), F32)
        for blk in range(CH_QB // 8):
            acc = acc + pltpu.roll(d_ref[0, pl.ds(8 * blk, 8), :], (CH_WIN - 8 * blk) % CH_WIN, 1)
        acc8[...] = acc
        dvec = jnp.zeros((1, CH_WIN), F32)
        for r in range(8):
            dvec = dvec + pltpu.roll(acc8[pl.ds(r, 1), :], (CH_WIN - r) % CH_WIN, 1)
        lane = lax.broadcasted_iota(jnp.int32, (1, CH_WIN), 1)
        clipped = (lane <= first) | (lane >= first + REL_CLIP + CHUNK)
        total = jnp.sum(jnp.where(clipped, dvec, 0.0), axis=1, keepdims=True)
        o_ref[0] = jnp.where(lane == first, total, dvec)

    return pl.pallas_call(
        body, name=name, grid=(n_heads,),
        in_specs=[pl.BlockSpec((1, CH_QB, CH_WIN), lambda h: (h, 0, 0))],
        out_specs=pl.BlockSpec((1, 1, CH_WIN), lambda h: (h, 0, 0)),
        out_shape=_sds((n_heads, 1, CH_WIN)),
        scratch_shapes=[pltpu.VMEM((8, CH_WIN), F32)],
        compiler_params=_params("arbitrary"),
    )(dbias)


def _out_fwd(o, h1, g_sb, g_ch, g_post, wout, *, name):
    t = o.shape[0]
    tm = 512
    half = D_MODEL // 2

    def body(o_ref, h_ref, gsb_ref, gch_ref, gpost_ref, w_ref, h2_ref, mixed_ref, y_ref):
        ov = o_ref[...]
        mixed = jnp.concatenate([_rms(ov[:, :half], gsb_ref[...]),
                                 _rms(ov[:, half:], gch_ref[...])], axis=1)
        mixed_ref[...] = _bf(mixed)
        y = _dot(mixed, w_ref[...])
        y_ref[...] = y
        h2_ref[...] = h_ref[...] + _rms(y, gpost_ref[...])

    row = pl.BlockSpec((tm, D_MODEL), lambda i: (i, 0))
    gain = lambda n: pl.BlockSpec((1, n), lambda i: (0, 0))
    return pl.pallas_call(
        body, name=name, grid=(t // tm,),
        in_specs=[row, row, gain(half), gain(half), gain(D_MODEL),
                  pl.BlockSpec((D_MODEL, D_MODEL), lambda i: (0, 0))],
        out_specs=[row, row, row],
        out_shape=[_sds((t, D_MODEL)), _sds((t, D_MODEL), BF16), _sds((t, D_MODEL))],
        compiler_params=_params("arbitrary"),
    )(o, h1, g_sb, g_ch, g_post, wout)


def _out_bwd(dy, mixed, o, g_sb, g_ch, wout, *, name):
    t = o.shape[0]
    tm = 512
    ni = t // tm
    half = D_MODEL // 2

    def body(dy_ref, mixed_ref, o_ref, gsb_ref, gch_ref, w_ref,
             dw_ref, do_ref, dgsb_ref, dgch_ref, acc_ref):
        i = pl.program_id(0)

        @pl.when(i == 0)
        def _():
            acc_ref[...] = jnp.zeros_like(acc_ref)
            dgsb_ref[...] = jnp.zeros_like(dgsb_ref)
            dgch_ref[...] = jnp.zeros_like(dgch_ref)

        dyv = dy_ref[...]
        acc_ref[...] += _dot_tn(mixed_ref[...], dyv)
        dm = _dot_nt(dyv, w_ref[...])
        ov = o_ref[...]
        doa, dga = _rms_bwd(dm[:, :half], ov[:, :half], gsb_ref[...])
        dob, dgb = _rms_bwd(dm[:, half:], ov[:, half:], gch_ref[...])
        do_ref[...] = jnp.concatenate([doa, dob], axis=1)
        dgsb_ref[...] += dga
        dgch_ref[...] += dgb

        @pl.when(i == ni - 1)
        def _():
            dw_ref[...] = _bf(acc_ref[...])

    row = pl.BlockSpec((tm, D_MODEL), lambda i: (i, 0))
    gain = pl.BlockSpec((1, half), lambda i: (0, 0))
    sq = pl.BlockSpec((D_MODEL, D_MODEL), lambda i: (0, 0))
    return pl.pallas_call(
        body, name=name, grid=(ni,),
        in_specs=[row, row, row, gain, gain, sq],
        out_specs=[sq, row, gain, gain],
        out_shape=[_sds((D_MODEL, D_MODEL), BF16), _sds((t, D_MODEL)),
                   _sds((1, half)), _sds((1, half))],
        scratch_shapes=[pltpu.VMEM((D_MODEL, D_MODEL), F32)],
        compiler_params=_params("arbitrary"),
    )(dy, mixed, o, g_sb, g_ch, wout)


def _ple(p, h3, target, wp, wgate, g, *, name):
    t = h3.shape[0]
    tm = 512
    ni = t // tm

    def body(p_ref, h_ref, tgt_ref, wp_ref, wg_ref, g_ref,
             loss_ref, dres_ref, dwp_ref, dwg_ref, dg_ref, accp, accg):
        i = pl.program_id(0)

        @pl.when(i == 0)
        def _():
            loss_ref[...] = jnp.zeros_like(loss_ref)
            dg_ref[...] = jnp.zeros_like(dg_ref)
            accp[...] = jnp.zeros_like(accp)
            accg[...] = jnp.zeros_like(accg)

        pv, hv, gv = p_ref[...], h_ref[...], g_ref[...]
        pe = _dot(pv, wp_ref[...])
        sig = _sigmoid(_dot(hv, wg_ref[...]))
        e = pe * sig
        err = hv + _rms(e, gv) - tgt_ref[...]
        tok = jnp.mean(err * err, axis=-1, keepdims=True)
        loss_ref[...] += 0.5 * jnp.sum(tok, axis=0, keepdims=True)
        dh4 = err * (1.0 / D_MODEL)
        de, dg = _rms_bwd(dh4, e, gv)
        dg_ref[...] += dg
        dpe = de * sig
        dgt = de * pe * sig * (1.0 - sig)
        accp[...] += _dot_tn(pv, dpe)
        accg[...] += _dot_tn(hv, dgt)
        dres_ref[...] = dh4 + _dot_nt(dgt, wg_ref[...])

        @pl.when(i == ni - 1)
        def _():
            dwp_ref[...] = _bf(accp[...])
            dwg_ref[...] = _bf(accg[...])

    row = pl.BlockSpec((tm, D_MODEL), lambda i: (i, 0))
    const = lambda r, c: pl.BlockSpec((r, c), lambda i: (0, 0))
    return pl.pallas_call(
        body, name=name, grid=(ni,),
        in_specs=[pl.BlockSpec((tm, PLE_DIM), lambda i: (i, 0)), row, row,
                  const(PLE_DIM, D_MODEL), const(D_MODEL, D_MODEL), const(1, D_MODEL)],
        out_specs=[const(1, 128), row, const(PLE_DIM, D_MODEL), const(D_MODEL, D_MODEL),
                   const(1, D_MODEL)],
        out_shape=[_sds((1, 128)), _sds((t, D_MODEL)), _sds((PLE_DIM, D_MODEL), BF16),
                   _sds((D_MODEL, D_MODEL), BF16), _sds((1, D_MODEL))],
        scratch_shapes=[pltpu.VMEM((PLE_DIM, D_MODEL), F32), pltpu.VMEM((D_MODEL, D_MODEL), F32)],
        compiler_params=_params("arbitrary"),
    )(p, h3, target, wp, wgate, g)


def _rel_bias_to_fvec(rel_bias):
    rev = rel_bias[:, ::-1]
    n_heads = rel_bias.shape[0]
    first = CH_LOOK - REL_CLIP
    n_var = REL_CLIP + CHUNK
    clipped = rev[:, :1]
    fvec = jnp.concatenate([jnp.broadcast_to(clipped, (n_heads, first)), rev[:, :n_var],
                            jnp.broadcast_to(clipped, (n_heads, CH_WIN - first - n_var))], axis=1)
    return fvec.reshape(n_heads, 1, CH_WIN)


def _fvec_grad_to_rel_bias(dfvec):
    first = CH_LOOK - REL_CLIP
    n_var = REL_CLIP + CHUNK
    rev = jnp.pad(dfvec[:, 0, first:first + n_var], ((0, 0), (0, N_REL - n_var)))
    return rev[:, ::-1]


def _local_step(x, p, target, g, weights_for, grads_done, fvec, weights_early=None):
    w, tie = weights_for(0, x)
    w = dict(w)
    h1, n1, a1, b1, f1 = _ffn_fwd(x, g["ffn1_pre"] + tie, g["ffn1_post"],
                                  w["ffn1_gate"], w["ffn1_up"], w["ffn1_down"], name="ffn1_fwd")
    more, tie = weights_for(1, h1)
    w.update(more)
    qkv, u = _qkv_fwd(h1, g["mix_pre"] + tie, w["in"], name="qkv_fwd")
    bias = _bias_expand(fvec, name="bias_expand")
    o, ltot = _sb_fwd(qkv, name="sb_fwd")
    tie = weights_early(2, ltot) if weights_early else 0.0
    o = _ch_fwd(qkv, bias, o, name="ch_fwd")
    h2, mixed, y = _out_fwd(o, h1, g["out_sb"] + tie, g["out_ch"], g["mix_post"], w["out"],
                            name="out_fwd")
    w.update(weights_for(2, h2)[0])
    h3, n2, a2, b2, f2 = _ffn_fwd(h2, g["ffn2_pre"], g["ffn2_post"],
                                  w["ffn2_gate"], w["ffn2_up"], w["ffn2_down"], name="ffn2_fwd")
    loss, dh3, dwp, dwgate, dg_ple = _ple(p, h3, target, w["ple_proj"], w["ple_gate"],
                                          g["ple_post"], name="ple")
    tie = grads_done(0, {"ple_proj": dwp, "ple_gate": dwgate})

    df2, dg_ffn2_post = _junction(dh3, post=(f2, g["ffn2_post"] + tie, 0.5), name="junction3")
    dwg2, dwu2, dwd2, dn2 = _ffn_bwd(n2, df2, a2, b2, w["ffn2_gate"], w["ffn2_up"],
                                     w["ffn2_down"], name="ffn2_bwd")
    tie = grads_done(1, {"ffn2_gate": dwg2, "ffn2_up": dwu2, "ffn2_down": dwd2})
    dh2, dg_ffn2_pre, dy, dg_mix_post = _junction(
        dh3, pre=(dn2, h2, g["ffn2_pre"] + tie), post=(y, g["mix_post"], 1.0), name="junction2")
    dwout, do, dg_sb, dg_ch = _out_bwd(dy, mixed, o, g["out_sb"], g["out_ch"], w["out"],
                                       name="out_bwd")
    dq, dk, dv = _sb_bwd(qkv, ltot, do, name="sb_bwd")
    dq, dk, dv, dbias = _ch_bwd(qkv, bias, o, do, dq, dk, dv, name="ch_bwd")
    dfvec = _bias_grad(dbias, name="bias_grad")
    dwin, du = _qkv_bwd(dq, dk, dv, u, w["in"], name="qkv_bwd")
    tie = grads_done(2, {"out": dwout, "in": dwin})
    dh1, dg_mix_pre, df1, dg_ffn1_post = _junction(
        dh2, pre=(du, h1, g["mix_pre"] + tie), post=(f1, g["ffn1_post"], 0.5), name="junction1")
    dwg1, dwu1, dwd1, dn1 = _ffn_bwd(n1, df1, a1, b1, w["ffn1_gate"], w["ffn1_up"],
                                     w["ffn1_down"], name="ffn1_bwd")
    tie = grads_done(3, {"ffn1_gate": dwg1, "ffn1_up": dwu1, "ffn1_down": dwd1})
    dx, dg_ffn1_pre = _junction(dh1, pre=(dn1, x, g["ffn1_pre"] + tie), name="junction0")

    dg = {"ffn1_pre": dg_ffn1_pre, "ffn1_post": dg_ffn1_post, "mix_pre": dg_mix_pre,
          "mix_post": dg_mix_post, "out_sb": dg_sb, "out_ch": dg_ch,
          "ffn2_pre": dg_ffn2_pre, "ffn2_post": dg_ffn2_post, "ple_post": dg_ple}
    return loss, dx, dg, dfvec


_WEIGHTS = (
    ("ffn1_gate", "row", FF_SHARD, FF_SHARD_PAD, D_MODEL),
    ("ffn1_up", "row", FF_SHARD, FF_SHARD_PAD, D_MODEL),
    ("ffn1_down", "row", FF_SHARD, FF_SHARD_PAD, D_MODEL),
    ("in", "col", QKV_SHARD, QKV_SHARD, D_MODEL),
    ("out", "row", ROW_SHARD, ROW_SHARD, D_MODEL),
    ("ffn2_gate", "row", FF_SHARD, FF_SHARD_PAD, D_MODEL),
    ("ffn2_up", "row", FF_SHARD, FF_SHARD_PAD, D_MODEL),
    ("ffn2_down", "row", FF_SHARD, FF_SHARD_PAD, D_MODEL),
    ("ple_proj", "col", ROW_SHARD, ROW_SHARD, PLE_DIM),
    ("ple_gate", "row", ROW_SHARD, ROW_SHARD, D_MODEL),
)
_TRANSPOSED = ("ffn1_gate", "ffn1_up", "ffn2_gate", "ffn2_up")
_SPEC = {n: (kind, valid, pad, other) for n, kind, valid, pad, other in _WEIGHTS}
_GATHER_STAGES = (("ffn1_gate", "ffn1_up", "ffn1_down"), ("in", "out"),
                  ("ffn2_gate", "ffn2_up", "ffn2_down", "ple_proj", "ple_gate"))
_SCATTER_STAGES = (("ple_proj", "ple_gate"), ("ffn2_gate", "ffn2_up", "ffn2_down"),
                   ("out", "in"), ("ffn1_gate", "ffn1_up", "ffn1_down"))
HBM = pl.BlockSpec(memory_space=pltpu.HBM)
SEM = pl.BlockSpec(memory_space=pltpu.SEMAPHORE)
EFFECT = pltpu.SideEffectType.DATAFLOW_SIDE_EFFECTING


def _shard_shape(kind, size, other):
    return (other, size) if kind == "col" else (size, other)


def _window(ref, kind, start, size):
    return ref.at[:, pl.ds(start, size)] if kind == "col" else ref.at[pl.ds(start, size), :]


def _device_tuple(k):
    return (k // 4, (k // 2) % 2, k % 2)


def _my_index():
    return 4 * lax.axis_index("x") + 2 * lax.axis_index("y") + lax.axis_index("c")


def _pack_weights(shards):
    nw = len(_WEIGHTS)

    def body(*refs):
        ins, packed, full = refs[:nw], refs[nw:2 * nw], refs[2 * nw:3 * nw]
        sem = refs[3 * nw]
        me = _my_index()
        for (_, kind, valid, pad, _), src, dst in zip(_WEIGHTS, ins, packed):
            if pad != valid:
                dst[...] = jnp.zeros_like(dst)
            if kind == "col":
                dst[:, pl.ds(0, valid)] = _bf(src[...])
            else:
                dst[pl.ds(0, valid), :] = _bf(src[...])
        for k in range(N_DEV):
            @pl.when(me == k)
            def _():
                for w, (_, kind, _, pad, _) in enumerate(_WEIGHTS):
                    pltpu.make_async_copy(packed[w], _window(full[w], kind, k * pad, pad),
                                          sem.at[w]).start()
        for w, (_, kind, _, pad, _) in enumerate(_WEIGHTS):
            pltpu.make_async_copy(packed[w], _window(full[w], kind, 0, pad), sem.at[w]).wait()

    whole = lambda shape: pl.BlockSpec(shape, lambda i: (0, 0))
    packed_shapes = [_shard_shape(kind, pad, other) for _, kind, _, pad, other in _WEIGHTS]
    outs = pl.pallas_call(
        body, name="pack_weights", grid=(1,),
        in_specs=[whole(a.shape) for a in shards],
        out_specs=[whole(s) for s in packed_shapes] + [ANY] * nw,
        out_shape=[_sds(s, BF16) for s in packed_shapes]
        + [_sds(_shard_shape(kind, N_DEV * pad, other), BF16) for _, kind, _, pad, other in _WEIGHTS],
        scratch_shapes=[pltpu.SemaphoreType.DMA((nw,))],
        compiler_params=_params("arbitrary"),
    )(*shards)
    names = [n for n, *_ in _WEIGHTS]
    return dict(zip(names, outs[:nw])), dict(zip(names, outs[nw:]))


def _hbm(a):
    return pltpu.with_memory_space_constraint(a, pltpu.HBM)


def _split_start(name, n, body_copies, sources, lands, after):
    arrays = list(sources) + list(lands)
    ns, na = len(sources), len(arrays)

    def body(*refs):
        src, land = refs[:ns], refs[ns:na]
        send, recv = refs[na + 1], refs[na + 2]
        token = refs[-1]
        body_copies(src, land, send, recv)
        token[...] = jnp.zeros_like(token)

    out = pl.pallas_call(
        body, name=name,
        out_shape=(pltpu.SemaphoreType.DMA((n,)), pltpu.SemaphoreType.DMA((n,)),
                   *[pltpu.HBM(a.shape, a.dtype) for a in arrays], _sds((8, 128))),
        in_specs=[HBM] * na + [ANY], out_specs=(SEM, SEM, *[HBM] * na, VMEM),
        input_output_aliases={i: 2 + i for i in range(na)},
        compiler_params=pltpu.CompilerParams(has_side_effects=EFFECT),
    )(*[_hbm(a) for a in arrays], after)
    return out[0], out[1], out[2:2 + ns], out[2 + ns:2 + na], out[-1]


def _split_wait(name, n, seven_of, send, recv, sources, lands, after, keep_sources=False):
    arrays = list(sources) + list(lands)
    ns, na = len(sources), len(arrays)

    def body(*refs):
        land = refs[ns:na]
        send_ref, recv_ref = refs[na], refs[na + 1]
        myself = (lax.axis_index("x"), lax.axis_index("y"), lax.axis_index("c"))
        for w in range(n):
            seven = seven_of(w, land[w])
            copy = pltpu.make_async_remote_copy(
                src_ref=seven, dst_ref=seven, send_sem=send_ref.at[w], recv_sem=recv_ref.at[w],
                device_id=myself, device_id_type=MESH)
            copy.wait_send()
            copy.wait_recv()

    out = pl.pallas_call(
        body, name=name,
        out_shape=[pltpu.HBM(a.shape, a.dtype) for a in arrays],
        in_specs=[HBM] * na + [SEM, SEM, ANY], out_specs=[HBM] * na,
        input_output_aliases={i: i for i in range(na)},
        compiler_params=pltpu.CompilerParams(has_side_effects=EFFECT),
    )(*arrays, send, recv, after)
    return out if keep_sources else out[ns:]


_ALL_PEERS = (1, 2, 3, 4, 5, 6, 7)
_NEAR_PEERS = (1, 2, 4, 6)
_FAR_CHIPS = (2, 4, 6)


def _gather_start(stage, names, packed, full, after, peers=_ALL_PEERS):
    def copies(src, land, send, recv):
        me = _my_index()
        for k in range(N_DEV):
            @pl.when(me == k)
            def _():
                for w, name in enumerate(names):
                    kind, _, pad, _ = _SPEC[name]
                    dst = _window(land[w], kind, k * pad, pad)
                    for mask in peers:
                        pltpu.make_async_remote_copy(
                            src_ref=src[w], dst_ref=dst, send_sem=send.at[w],
                            recv_sem=recv.at[w], device_id=_device_tuple(k ^ mask),
                            device_id_type=MESH).start()

    return _split_start(f"gather_start{stage}", len(names), copies,
                        [packed[n] for n in names], [full[n] for n in names], after)


def _gather_wait(stage, names, started, after, count=N_DEV - 1):
    send, recv, src, land, _ = started

    def bytes_of(w, ref):
        kind, _, pad, _ = _SPEC[names[w]]
        return _window(ref, kind, 0, count * pad)

    return dict(zip(names, _split_wait(f"gather_wait{stage}", len(names), bytes_of,
                                       send, recv, src, land, after)))


def _relay_start(stage, names, full, after):
    def copies(_, land, send, recv):
        me = _my_index()
        for k in range(N_DEV):
            @pl.when(me == k)
            def _():
                for w, name in enumerate(names):
                    kind, _, pad, _ = _SPEC[name]
                    for mask in _FAR_CHIPS:
                        win = _window(land[w], kind, (k ^ mask) * pad, pad)
                        pltpu.make_async_remote_copy(
                            src_ref=win, dst_ref=win, send_sem=send.at[w], recv_sem=recv.at[w],
                            device_id=_device_tuple(k ^ 1), device_id_type=MESH).start()

    return _split_start(f"relay_start{stage}", len(names), copies, [],
                        [full[n] for n in names], after)


def _scatter_start(stage, names, grads, after):
    def copies(src, land, send, recv):
        me = _my_index()
        for k in range(N_DEV):
            @pl.when(me != k)
            def _():
                slot = lax.rem(me + (N_DEV - 1 - k), N_DEV)
                for w, name in enumerate(names):
                    kind, _, pad, _ = _SPEC[name]
                    pltpu.make_async_remote_copy(
                        src_ref=_window(src[w], kind, k * pad, pad), dst_ref=land[w].at[slot],
                        send_sem=send.at[w], recv_sem=recv.at[w],
                        device_id=_device_tuple(k), device_id_type=MESH).start()

    lands = [lax.empty((N_DEV - 1,) + _shard_shape(_SPEC[m][0], _SPEC[m][2], _SPEC[m][3]), BF16)
             for m in names]
    return _split_start(f"scatter_start{stage}", len(names), copies, grads, lands, after)


def _scatter_wait(stage, names, started, after):
    send, recv, src, land, _ = started
    n = len(names)
    out = _split_wait(f"scatter_wait{stage}", n, lambda w, ref: ref, send, recv, src, land, after,
                      keep_sources=True)
    return dict(zip(names, out[:n])), dict(zip(names, out[n:]))


N_CHIPS = N_DEV // 2


def _pair_start(stage, names, grads, after):
    def copies(src, land, send, recv):
        me = _my_index()
        for k in range(N_DEV):
            @pl.when(me == k)
            def _():
                for w, name in enumerate(names):
                    kind, _, pad, _ = _SPEC[name]
                    for chip in range(N_CHIPS):
                        j = 2 * chip + ((k ^ 1) & 1)
                        pltpu.make_async_remote_copy(
                            src_ref=_window(src[w], kind, j * pad, pad), dst_ref=land[w].at[chip],
                            send_sem=send.at[w], recv_sem=recv.at[w],
                            device_id=_device_tuple(k ^ 1), device_id_type=MESH).start()

    lands = [lax.empty((N_CHIPS,) + _shard_shape(_SPEC[m][0], _SPEC[m][2], _SPEC[m][3]), BF16)
             for m in names]
    return _split_start(f"pair_start{stage}", len(names), copies, grads, lands, after)


def _pair_sum(dw_full, pair, *, pad, name):
    other = dw_full.shape[1]

    def body(own_ref, pair_ref, out_ref):
        out_ref[0] = _bf(own_ref[...].astype(F32) + pair_ref[0].astype(F32))

    slot = pl.BlockSpec((1, pad, other), lambda q: (q, 0, 0))
    return pl.pallas_call(
        body, name=name, grid=(N_CHIPS,),
        in_specs=[pl.BlockSpec((pad, other), lambda q: (2 * q + lax.axis_index("c"), 0)), slot],
        out_specs=slot, out_shape=_sds((N_CHIPS, pad, other), BF16),
        compiler_params=_params("arbitrary"),
    )(dw_full, pair)


def _chip_start(stage, names, sums, after):
    def copies(src, land, send, recv):
        me = _my_index()
        my_chip = lax.shift_right_logical(me, 1)
        for k in range(N_DEV):
            @pl.when((me != k) & (((me ^ k) & 1) == 0))
            def _():
                slot = lax.rem(my_chip + (N_CHIPS - 1 - k // 2), N_CHIPS)
                for w in range(len(names)):
                    pltpu.make_async_remote_copy(
                        src_ref=src[w].at[k // 2], dst_ref=land[w].at[slot],
                        send_sem=send.at[w], recv_sem=recv.at[w],
                        device_id=_device_tuple(k), device_id_type=MESH).start()

    lands = [lax.empty((N_CHIPS - 1,) + a.shape[1:], BF16) for a in sums]
    return _split_start(f"chip_start{stage}", len(names), copies, sums, lands, after)


def _adamw_chip(w, m, v, land, sums, *, name):
    shape = w.shape

    def body(w_ref, m_ref, v_ref, land_ref, own_ref, *outs):
        rows = pl.ds(0, shape[0])
        grad = own_ref[0, rows, :].astype(F32)
        for s in range(N_CHIPS - 1):
            grad = grad + land_ref[s, rows, :].astype(F32)
        _adam_update(w_ref, m_ref, v_ref, grad, *outs)

    whole = lambda a: pl.BlockSpec(a.shape, lambda i: (0,) * a.ndim)
    own = pl.BlockSpec((1,) + sums.shape[1:],
                       lambda i: (2 * lax.axis_index("x") + lax.axis_index("y"), 0, 0))
    return pl.pallas_call(
        body, name=name, grid=(1,),
        in_specs=[whole(w), whole(m), whole(v), whole(land), own],
        out_specs=[whole(w)] * 4, out_shape=[_sds(shape)] * 4,
        compiler_params=_params("arbitrary"),
    )(w, m, v, land, sums)


def _allreduce_small(small, after):
    shape = small.shape

    def body(in_ref, _after, out_ref, gath, send, recv):
        me = _my_index()
        for k in range(N_DEV):
            @pl.when(me != k)
            def _():
                pltpu.make_async_remote_copy(
                    src_ref=in_ref, dst_ref=gath.at[me], send_sem=send, recv_sem=recv,
                    device_id=_device_tuple(k), device_id_type=MESH).start()

            @pl.when(me == k)
            def _():
                gath[k] = in_ref[...]
        seven = gath.at[pl.ds(0, N_DEV - 1)]
        pltpu.make_async_remote_copy(
            src_ref=seven, dst_ref=seven, send_sem=send, recv_sem=recv,
            device_id=_device_tuple(0), device_id_type=MESH).wait()
        total = gath[0]
        for s in range(1, N_DEV):
            total = total + gath[s]
        out_ref[...] = total

    return pl.pallas_call(
        body, name="allreduce_small",
        in_specs=[VMEM, ANY], out_specs=VMEM, out_shape=_sds(shape),
        scratch_shapes=[pltpu.VMEM((N_DEV,) + shape, F32),
                        pltpu.SemaphoreType.DMA, pltpu.SemaphoreType.DMA],
    )(small, after)


def _adam_update(w_ref, m_ref, v_ref, grad, grad_ref, delta_ref, nm_ref, nv_ref):
    new_m = ADAM_B1 * m_ref[...] + (1.0 - ADAM_B1) * grad
    new_v = ADAM_B2 * v_ref[...] + (1.0 - ADAM_B2) * (grad * grad)
    m_hat = new_m / (1.0 - ADAM_B1 ** ADAM_STEP)
    v_hat = new_v / (1.0 - ADAM_B2 ** ADAM_STEP)
    grad_ref[...] = grad
    delta_ref[...] = -ADAM_LR * (m_hat / (jnp.sqrt(v_hat) + ADAM_EPS) + ADAM_WD * w_ref[...])
    nm_ref[...] = new_m
    nv_ref[...] = new_v


def _adamw(w, m, v, g, *, name):
    def body(w_ref, m_ref, v_ref, g_ref, *outs):
        _adam_update(w_ref, m_ref, v_ref, g_ref[...], *outs)

    whole = pl.BlockSpec(w.shape, lambda i: (0,) * w.ndim)
    return pl.pallas_call(
        body, name=name, grid=(1,), in_specs=[whole] * 4, out_specs=[whole] * 4,
        out_shape=[_sds(w.shape)] * 4, compiler_params=_params("arbitrary"),
    )(w, m, v, g)


def _adamw_gains(small, params):
    n = len(params)

    def body(small_ref, *refs):
        ins, outs = refs[:3 * n], refs[3 * n:]
        for r in range(n):
            width = ins[3 * r].shape[1]
            if width == D_MODEL:
                grad = small_ref[pl.ds(r, 1), :]
            else:
                grad = small_ref[pl.ds(len(_GAINS), 1), pl.ds((r - len(_GAINS)) * width, width)]
            _adam_update(*ins[3 * r:3 * r + 3], grad, *outs[4 * r:4 * r + 4])

    whole = lambda a: pl.BlockSpec(a.shape, lambda i: (0, 0))
    flat = [a for group in params for a in group]
    return pl.pallas_call(
        body, name="adamw_gains", grid=(1,),
        in_specs=[whole(small)] + [whole(a) for a in flat],
        out_specs=[whole(w) for w, _, _ in params for _ in range(4)],
        out_shape=[_sds(w.shape) for w, _, _ in params for _ in range(4)],
        compiler_params=_params("arbitrary"),
    )(small, *flat)


def _adamw_shard(w, m, v, land, dw_full, *, kind, pad, name):
    shape = w.shape
    other = shape[0] if kind == "col" else shape[1]

    def body(w_ref, m_ref, v_ref, land_ref, own_ref, *outs):
        valid = ((slice(None), pl.ds(0, shape[1])) if kind == "col"
                 else (pl.ds(0, shape[0]), slice(None)))
        grad = own_ref[valid].astype(F32)
        for s in range(N_DEV - 1):
            grad = grad + land_ref[(s,) + valid].astype(F32)
        _adam_update(w_ref, m_ref, v_ref, grad, *outs)

    whole = lambda a: pl.BlockSpec(a.shape, lambda i: (0,) * a.ndim)
    own = pl.BlockSpec(_shard_shape(kind, pad, other),
                       (lambda i: (0, _my_index())) if kind == "col" else (lambda i: (_my_index(), 0)))
    return pl.pallas_call(
        body, name=name, grid=(1,),
        in_specs=[whole(w), whole(m), whole(v), whole(land), own],
        out_specs=[whole(w)] * 4, out_shape=[_sds(shape)] * 4,
        compiler_params=_params("arbitrary"),
    )(w, m, v, land, dw_full)


_GAINS = ("ffn1_pre", "ffn1_post", "mix_pre", "mix_post", "ffn2_pre", "ffn2_post", "ple_post")
_SMALL_ROWS = 16


def _stack_gains(get):
    return jnp.concatenate([get(n) for n in _GAINS]
                           + [jnp.concatenate([get("out_sb"), get("out_ch")], axis=1)], axis=0)


def kernel(x, p, g_ffn1_pre, g_ffn1_post, w_ffn1_gate, w_ffn1_up, w_ffn1_down, g_mix_pre, g_mix_post, w_in, g_out_sb, g_out_ch, rel_bias, w_out, g_ffn2_pre, g_ffn2_post, w_ffn2_gate, w_ffn2_up, w_ffn2_down, w_ple_proj, w_ple_gate, g_ple_post, loss_target, m_g_ffn1_pre, m_g_ffn1_post, m_w_ffn1_gate, m_w_ffn1_up, m_w_ffn1_down, m_g_mix_pre, m_g_mix_post, m_w_in, m_g_out_sb, m_g_out_ch, m_rel_bias, m_w_out, m_g_ffn2_pre, m_g_ffn2_post, m_w_ffn2_gate, m_w_ffn2_up, m_w_ffn2_down, m_w_ple_proj, m_w_ple_gate, m_g_ple_post, v_g_ffn1_pre, v_g_ffn1_post, v_w_ffn1_gate, v_w_ffn1_up, v_w_ffn1_down, v_g_mix_pre, v_g_mix_post, v_w_in, v_g_out_sb, v_g_out_ch, v_rel_bias, v_w_out, v_g_ffn2_pre, v_g_ffn2_post, v_w_ffn2_gate, v_w_ffn2_up, v_w_ffn2_down, v_w_ple_proj, v_w_ple_gate, v_g_ple_post):
    given = dict(locals())
    wnames = [n for n, *_ in _WEIGHTS]

    def shard(prefix, n):
        a = given[prefix + "w_" + n][0]
        return a.T if n in _TRANSPOSED else a

    packed, full = _pack_weights([shard("", n) for n in wnames])
    first = _GATHER_STAGES[0]
    anchor = x[0]
    two_level = (0, 2)
    gathers = {}

    def start_stage(stage, after):
        peers = _NEAR_PEERS if stage in two_level else _ALL_PEERS
        gathers[stage] = _gather_start(stage, _GATHER_STAGES[stage], packed, full, after,
                                       peers=peers)

    start_stage(0, anchor)

    relays = {}

    def first_level(stage, after):
        names = _GATHER_STAGES[stage]
        last_stage = stage + 1 == len(_GATHER_STAGES)
        count = len(_NEAR_PEERS) if stage in two_level else N_DEV - 1
        ws = _gather_wait(stage, names, gathers[stage], after, count=count)
        if not last_stage:
            start_stage(stage + 1, ws[names[0]])
        if stage in two_level:
            relays[stage] = _relay_start(stage, names, ws,
                                         anchor if last_stage else gathers[stage + 1][-1])
            return ws, relays[stage][-1]
        return ws, None if last_stage else gathers[stage + 1][-1]

    def weights_early(stage, after):
        return first_level(stage, after)[1][:1, :1]

    def weights_for(stage, after):
        names = _GATHER_STAGES[stage]
        ws, token = (None, None) if stage in relays else first_level(stage, after)
        if stage in relays:
            relay = relays[stage]
            ws = _gather_wait(f"{stage}r", names, relay, after, count=len(_FAR_CHIPS))
            token = None if stage + 1 == len(_GATHER_STAGES) else gathers[stage + 1][-1]
        return ws, jnp.zeros((1, 1), F32) if token is None else token[:1, :1]

    scatters = {}

    last = len(_SCATTER_STAGES) - 1

    def grads_done(stage, grads):
        names = _SCATTER_STAGES[stage]
        start = _pair_start if stage == last else _scatter_start
        scatters[stage] = start(stage, names, [grads[n] for n in names], anchor)
        return scatters[stage][-1][:1, :1]

    gains = {n: given["g_" + n] for n in _GAINS + ("out_sb", "out_ch")}
    fvec = _rel_bias_to_fvec(rel_bias[0])
    loss, dx, dg, dfvec = _local_step(x[0], p[0, 0], loss_target[0], gains,
                                      weights_for, grads_done, fvec, weights_early)

    results = {}

    def finish(stage, after):
        names = _SCATTER_STAGES[stage]
        dws, lands = _scatter_wait(stage, names, scatters[stage], after)
        for n in names:
            kind, _, pad, _ = _SPEC[n]
            out = _adamw_shard(shard("", n), shard("m_", n), shard("v_", n), lands[n], dws[n],
                               kind=kind, pad=pad, name="adamw_" + n)
            results["w_" + n] = [a.T for a in out] if n in _TRANSPOSED else out
        return results["w_" + names[-1]][0]

    names = _SCATTER_STAGES[last]
    whole = lambda w, ref: ref
    send, recv, src, land, _ = scatters[last]
    out = _split_wait(f"pair_wait{last}", len(names), whole, send, recv, src, land, dx,
                      keep_sources=True)
    sums = [_pair_sum(dwf, pair, pad=_SPEC[n][2], name="pair_sum_" + n)
            for n, dwf, pair in zip(names, out[:len(names)], out[len(names):])]
    send, recv, src, land, after = _chip_start(last, names, sums, anchor)
    for stage in range(last):
        after = finish(stage, after)
    out = _split_wait(f"chip_wait{last}", len(names), whole, send, recv, src, land, after,
                      keep_sources=True)
    for n, own, landed in zip(names, out[:len(names)], out[len(names):]):
        res = _adamw_chip(shard("", n), shard("m_", n), shard("v_", n), landed, own,
                          name="adamw_" + n)
        results["w_" + n] = [a.T for a in res] if n in _TRANSPOSED else res
        after = res[0]
    loss_col = jnp.pad(loss[:, :1], ((0, N_DEV - 1), (0, D_MODEL - CH_WIN - 1)))
    dfv = jnp.concatenate([dfvec[:, 0, :], loss_col], axis=1)
    small = _allreduce_small(jnp.concatenate([_stack_gains(lambda n: dg[n]), dfv], axis=0), after)
    gain_names = _GAINS + ("out_sb", "out_ch")
    gain_out = _adamw_gains(small, [(given["g_" + n], given["m_g_" + n], given["v_g_" + n])
                                    for n in gain_names])
    for r, n in enumerate(gain_names):
        results["g_" + n] = gain_out[4 * r:4 * r + 4]
    d_rel = _fvec_grad_to_rel_bias(small[N_DEV:, :CH_WIN].reshape(N_DEV, 1, CH_WIN))
    results["rel_bias"] = _adamw(rel_bias[0], m_rel_bias[0], v_rel_bias[0], d_rel,
                                 name="adamw_rel_bias")

    order = ("g_ffn1_pre", "g_ffn1_post", "w_ffn1_gate", "w_ffn1_up", "w_ffn1_down",
             "g_mix_pre", "g_mix_post", "w_in", "g_out_sb", "g_out_ch", "rel_bias", "w_out",
             "g_ffn2_pre", "g_ffn2_post", "w_ffn2_gate", "w_ffn2_up", "w_ffn2_down",
             "w_ple_proj", "w_ple_gate", "g_ple_post")

    def leaf(name, idx):
        a = results[name][idx]
        return a if name.startswith("g_") else a[None]

    total_loss = small[N_DEV, CH_WIN]
    return (total_loss, dx[None],
            *[leaf(n, 0) for n in order], *[leaf(n, 1) for n in order],
            *[leaf(n, 2) for n in order], *[leaf(n, 3) for n in order])
```

```python
import jax
import jax.numpy as jnp
from jax import lax
from jax.experimental import pallas as pl
from jax.experimental.pallas import tpu as pltpu

F32 = jnp.float32
BF16 = jnp.bfloat16

N_DEV = 8
D_MODEL = 1024
D_FF = 2816
FF_SHARD = D_FF // N_DEV
FF_SHARD_PAD = 384
D_FF_PAD = FF_SHARD_PAD * N_DEV
QKV_WIDTH = 3 * D_MODEL
QKV_SHARD = QKV_WIDTH // N_DEV
PLE_DIM = 256
ROW_SHARD = D_MODEL // N_DEV
HEAD_DIM = 64
PAIR = 2 * HEAD_DIM
N_PAIRS = 4
CHUNK = 64
LOOKBACK = 8
REL_CLIP = 128
N_REL = 2 * REL_CLIP + 1
CH_QB = 256
CH_LOOK = LOOKBACK * CHUNK
CH_WIN = CH_LOOK + CH_QB
SB_QB = 512
SB_KB = 256
SB_GROUP = 2
SB_LANES = tuple(slice(g * 128, (g + 1) * 128) for g in range(SB_GROUP))
EPS = 1e-6
NEG_INF = -1e30
ATT_SCALE = HEAD_DIM ** -0.5
ADAM_LR = 0.001
ADAM_B1 = 0.9
ADAM_B2 = 0.999
ADAM_EPS = 1e-08
ADAM_WD = 0.01
ADAM_STEP = 10
VMEM_LIMIT_BYTES = 48 * 1024 * 1024
MESH = pl.DeviceIdType.MESH

ANY = pl.BlockSpec(memory_space=pl.ANY)
VMEM = pl.BlockSpec(memory_space=pltpu.VMEM)


def _params(*sem):
    return pltpu.CompilerParams(dimension_semantics=sem or None,
                                vmem_limit_bytes=VMEM_LIMIT_BYTES)


def _sds(shape, dtype=F32):
    return jax.ShapeDtypeStruct(shape, dtype)


def _bf(x):
    return x.astype(BF16)


def _dot(a, b):
    return jnp.dot(_bf(a), _bf(b), preferred_element_type=F32)


def _dot_nt(a, b):
    return lax.dot_general(_bf(a), _bf(b), (((1,), (1,)), ((), ())),
                           preferred_element_type=F32)


def _dot_tn(a, b):
    return lax.dot_general(_bf(a), _bf(b), (((0,), (0,)), ((), ())),
                           preferred_element_type=F32)


def _sigmoid(x):
    return 1.0 / (1.0 + jnp.exp(-x))


def _softplus(x):
    return jnp.maximum(x, 0.0) + jnp.log(1.0 + jnp.exp(-jnp.abs(x)))


def _rstd(x):
    return lax.rsqrt(jnp.mean(x * x, axis=-1, keepdims=True) + EPS)


def _rms(x, g):
    return x * _rstd(x) * g


def _rms_bwd(dy, x, g):
    r = _rstd(x)
    w = dy * g
    dx = r * (w - x * (r * r) * jnp.mean(w * x, axis=-1, keepdims=True))
    dg = jnp.sum(dy * (x * r), axis=0, keepdims=True)
    return dx, dg


def _head_masks():
    lane = lax.broadcasted_iota(jnp.int32, (1, PAIR), 1)
    return lane < HEAD_DIM, lane >= HEAD_DIM


def _ffn_fwd(x, g_pre, g_post, wg, wu, wd, *, name):
    t = x.shape[0]
    tm, tj = 512, 1024
    ni, nj = t // tm, D_FF_PAD // tj

    def body(x_ref, gpre_ref, gpost_ref, wg_ref, wu_ref, wd_ref,
             h_ref, n_ref, a_ref, b_ref, f_ref, acc_ref):
        j = pl.program_id(1)

        @pl.when(j == 0)
        def _():
            n_ref[...] = _bf(_rms(x_ref[...], gpre_ref[...]))
            acc_ref[...] = jnp.zeros_like(acc_ref)

        n = n_ref[...]
        a = _dot_nt(n, wg_ref[...])
        b = _dot_nt(n, wu_ref[...])
        a_ref[...] = a
        b_ref[...] = b
        hmid = a * _sigmoid(a) * b
        acc_ref[...] += jnp.dot(_bf(hmid), wd_ref[...], preferred_element_type=F32)

        @pl.when(j == nj - 1)
        def _():
            f = acc_ref[...]
            f_ref[...] = f
            h_ref[...] = x_ref[...] + 0.5 * _rms(f, gpost_ref[...])

    row = pl.BlockSpec((tm, D_MODEL), lambda i, j: (i, 0))
    gain = pl.BlockSpec((1, D_MODEL), lambda i, j: (0, 0))
    col = pl.BlockSpec((tm, tj), lambda i, j: (i, j))
    wtile = pl.BlockSpec((tj, D_MODEL), lambda i, j: (j, 0))
    return pl.pallas_call(
        body, name=name, grid=(ni, nj),
        in_specs=[row, gain, gain, wtile, wtile, wtile],
        out_specs=[row, row, col, col, row],
        out_shape=[_sds((t, D_MODEL)), _sds((t, D_MODEL), BF16),
                   _sds((t, D_FF_PAD)), _sds((t, D_FF_PAD)), _sds((t, D_MODEL))],
        scratch_shapes=[pltpu.VMEM((tm, D_MODEL), F32)],
        compiler_params=_params("arbitrary", "arbitrary"),
    )(x, g_pre, g_post, wg, wu, wd)


def _ffn_bwd(n, df, a, b, wg, wu, wd, *, name):
    t = n.shape[0]
    tj, tm, ts = 256, t, 512
    nj, ni, ns = D_FF_PAD // tj, t // tm, tm // ts

    def body(n_hbm, df_hbm, a_ref, b_ref, wg_ref, wu_ref, wd_ref,
             dwg_ref, dwu_ref, dwd_ref, dn_hbm,
             n_v, df_v, dn_v, ag, au, ad, sem):
        j, i = pl.program_id(0), pl.program_id(1)

        @pl.when((j == 0) & (i == 0))
        def _():
            c1 = pltpu.make_async_copy(n_hbm, n_v, sem.at[0])
            c2 = pltpu.make_async_copy(df_hbm, df_v, sem.at[1])
            c1.start()
            c2.start()
            dn_v[...] = jnp.zeros_like(dn_v)
            c1.wait()
            c2.wait()

        @pl.when(i == 0)
        def _():
            ag[...] = jnp.zeros_like(ag)
            au[...] = jnp.zeros_like(au)
            ad[...] = jnp.zeros_like(ad)

        wgj, wuj, wdj = wg_ref[...], wu_ref[...], wd_ref[...]
        for s in range(ns):
            local = pl.ds(s * ts, ts)
            rows = pl.ds(pl.multiple_of(i * tm + s * ts, ts), ts)
            av, bv = a_ref[local, :], b_ref[local, :]
            sig = _sigmoid(av)
            silu = av * sig
            dfr = df_v[rows, :]
            nr = n_v[rows, :]
            dhmid = _dot_nt(dfr, wdj)
            da = dhmid * bv * (sig * (1.0 + av * (1.0 - sig)))
            db = dhmid * silu
            ad[...] += _dot_tn(silu * bv, dfr)
            ag[...] += _dot_tn(da, nr)
            au[...] += _dot_tn(db, nr)
            dn_v[rows, :] += _dot(da, wgj) + _dot(db, wuj)

        @pl.when(i == ni - 1)
        def _():
            dwg_ref[...] = _bf(ag[...])
            dwu_ref[...] = _bf(au[...])
            dwd_ref[...] = _bf(ad[...])

        @pl.when((j == nj - 1) & (i == ni - 1))
        def _():
            c = pltpu.make_async_copy(dn_v, dn_hbm, sem.at[0])
            c.start()
            c.wait()

    roww = pl.BlockSpec((tj, D_MODEL), lambda j, i: (j, 0))
    act = pl.BlockSpec((tm, tj), lambda j, i: (i, j))
    return pl.pallas_call(
        body, name=name, grid=(nj, ni),
        in_specs=[ANY, ANY, act, act, roww, roww, roww],
        out_specs=[roww, roww, roww, ANY],
        out_shape=[_sds((D_FF_PAD, D_MODEL), BF16)] * 3 + [_sds((t, D_MODEL))],
        scratch_shapes=[pltpu.VMEM((t, D_MODEL), BF16), pltpu.VMEM((t, D_MODEL), BF16),
                        pltpu.VMEM((t, D_MODEL), F32)]
        + [pltpu.VMEM((tj, D_MODEL), F32)] * 3 + [pltpu.SemaphoreType.DMA((2,))],
        compiler_params=_params("arbitrary", "arbitrary"),
    )(_hbm(n), _hbm(df), a, b, wg, wu, wd)


def _junction(dres, pre=None, post=None, *, name):
    t = dres.shape[0]
    tm = 512
    ni = t // tm
    n_in = 1 + (3 if pre else 0) + (2 if post else 0)
    coef = post[2] if post else None

    def body(*refs):
        ins, outs = list(refs[:n_in]), list(refs[n_in:])
        i = pl.program_id(0)
        dh = ins.pop(0)[...]
        if pre:
            dn_ref, x_ref, gpre_ref = ins.pop(0), ins.pop(0), ins.pop(0)
            dh_ref, dgpre_ref = outs.pop(0), outs.pop(0)
            dx, dg = _rms_bwd(dn_ref[...], x_ref[...], gpre_ref[...])
            dh = dh + dx
            dh_ref[...] = dh

            @pl.when(i == 0)
            def _():
                dgpre_ref[...] = jnp.zeros_like(dgpre_ref)
            dgpre_ref[...] += dg
        if post:
            f_ref, gpost_ref = ins.pop(0), ins.pop(0)
            df_ref, dgpost_ref = outs.pop(0), outs.pop(0)
            df, dg = _rms_bwd(coef * dh, f_ref[...], gpost_ref[...])
            df_ref[...] = _bf(df)

            @pl.when(i == 0)
            def _():
                dgpost_ref[...] = jnp.zeros_like(dgpost_ref)
            dgpost_ref[...] += dg

    row = pl.BlockSpec((tm, D_MODEL), lambda i: (i, 0))
    gain = pl.BlockSpec((1, D_MODEL), lambda i: (0, 0))
    args, in_specs, out_specs, out_shape = [dres], [row], [], []
    if pre:
        args += list(pre)
        in_specs += [row, row, gain]
        out_specs += [row, gain]
        out_shape += [_sds((t, D_MODEL)), _sds((1, D_MODEL))]
    if post:
        args += [post[0], post[1]]
        in_specs += [row, gain]
        out_specs += [row, gain]
        out_shape += [_sds((t, D_MODEL), BF16), _sds((1, D_MODEL))]
    return pl.pallas_call(
        body, name=name, grid=(ni,), in_specs=in_specs, out_specs=out_specs,
        out_shape=out_shape, compiler_params=_params("arbitrary"),
    )(*args)


def _qkv_fwd(h, g, win, *, name):
    t = h.shape[0]
    tm, tn = min(1024, t), 1024
    ni, nj = t // tm, QKV_WIDTH // tn

    def body(h_ref, g_ref, w_ref, qkv_ref, u_ref):
        @pl.when(pl.program_id(1) == 0)
        def _():
            u_ref[...] = _bf(_rms(h_ref[...], g_ref[...]))
        qkv_ref[...] = jnp.dot(u_ref[...], w_ref[...], preferred_element_type=F32)

    row = pl.BlockSpec((tm, D_MODEL), lambda i, j: (i, 0))
    return pl.pallas_call(
        body, name=name, grid=(ni, nj),
        in_specs=[row, pl.BlockSpec((1, D_MODEL), lambda i, j: (0, 0)),
                  pl.BlockSpec((D_MODEL, tn), lambda i, j: (0, j))],
        out_specs=[pl.BlockSpec((tm, tn), lambda i, j: (i, j)), row],
        out_shape=[_sds((t, QKV_WIDTH)), _sds((t, D_MODEL), BF16)],
        compiler_params=_params("arbitrary", "arbitrary"),
    )(h, g, win)


def _qkv_bwd(dq, dk, dv, u, win, *, name):
    t = u.shape[0]
    tn, ts = 512, 512
    nj, ns = QKV_WIDTH // tn, t // ts

    def body(dq_ref, dk_ref, dv_ref, u_ref, w_ref, dw_ref, du_hbm, du_v, acc_ref, sem):
        j = pl.program_id(0)

        @pl.when(j == 0)
        def _():
            du_v[...] = jnp.zeros_like(du_v)

        wj = w_ref[...]
        for role, d_ref in enumerate((dq_ref, dk_ref, dv_ref)):
            @pl.when(j % 3 == role)
            def _():
                acc_ref[...] = jnp.zeros_like(acc_ref)
                for s in range(ns):
                    rows = pl.ds(s * ts, ts)
                    dcol = d_ref[rows, :]
                    acc_ref[...] += _dot_tn(u_ref[rows, :], dcol)
                    du_v[rows, :] += _dot_nt(dcol, wj)
                dw_ref[...] = _bf(acc_ref[...])

        @pl.when(j == nj - 1)
        def _():
            c = pltpu.make_async_copy(du_v, du_hbm, sem)
            c.start()
            c.wait()

    colw = pl.BlockSpec((D_MODEL, tn), lambda j: (0, j))
    grp = pl.BlockSpec((t, tn), lambda j: (0, j // 3))
    return pl.pallas_call(
        body, name=name, grid=(nj,),
        in_specs=[grp, grp, grp, pl.BlockSpec((t, D_MODEL), lambda j: (0, 0)), colw],
        out_specs=[colw, ANY],
        out_shape=[_sds((D_MODEL, QKV_WIDTH), BF16), _sds((t, D_MODEL))],
        scratch_shapes=[pltpu.VMEM((t, D_MODEL), F32), pltpu.VMEM((D_MODEL, tn), F32),
                        pltpu.SemaphoreType.DMA],
        compiler_params=_params("arbitrary"),
    )(dq, dk, dv, _hbm(u), win)


def _sb_stack(x):
    lo, hi = _head_masks()
    return jnp.concatenate([jnp.where(lo, x, 0.0), jnp.where(hi, x, 0.0)], axis=0)


def _sb_unstack(x2, blk):
    return jnp.where(_head_masks()[0], x2[:blk], x2[blk:])


def _sb_rows_from(x2, blk, r0):
    return x2 if r0 == 0 else jnp.concatenate([x2[r0:blk], x2[blk + r0:]], axis=0)


def _sb_rows_merge(full2, sub2, blk, r0):
    if r0 == 0:
        return sub2
    rows = blk - r0
    return jnp.concatenate([full2[:r0], sub2[:rows], full2[blk:blk + r0], sub2[rows:]], axis=0)


def _sb_mask(qb, kb, offset):
    r = lax.broadcasted_iota(jnp.int32, (2 * qb, kb), 0) & (qb - 1)
    c = lax.broadcasted_iota(jnp.int32, (2 * qb, kb), 1) + offset
    return c < r


def _tri(n, keep):
    r = lax.broadcasted_iota(jnp.int32, (n, n), 0)
    c = lax.broadcasted_iota(jnp.int32, (n, n), 1)
    return jnp.where(keep(r, c), 1.0, 0.0).astype(BF16)


def _cumsum01(x, u):
    m = x.shape[0]
    hi = _bf(x)
    lo = _bf(x - hi.astype(F32))
    both = jnp.dot(jnp.concatenate([hi, lo], axis=0), u, preferred_element_type=F32)
    return both[:m] + both[m:]


def _sb_fwd(qkv, *, name):
    t = qkv.shape[0]
    blk, kb = min(SB_QB, t), SB_KB
    ni, per = t // blk, blk // kb

    def body(q_ref, k_ref, v_ref, o_ref, ltot_ref):
        i = pl.program_id(1)
        u_after = _tri(kb, lambda r, c: r > c)
        q2 = [_bf(_sb_stack(q_ref[:, lanes] * ATT_SCALE)) for lanes in SB_LANES]

        def tile(g, k0, mask, acc, c_l):
            kj = k_ref[pl.ds(k0, kb), SB_LANES[g]]
            vj = v_ref[pl.ds(k0, kb), SB_LANES[g]]
            z = _dot_nt(q2[g], kj)
            sp = _softplus(z)
            lf = -sp if mask is None else jnp.where(mask, -sp, 0.0)
            a = jnp.exp(z - sp + _cumsum01(lf, u_after) + c_l)
            if mask is not None:
                a = jnp.where(mask, a, 0.0)
            return acc + _dot(a, vj), c_l + jnp.sum(lf, axis=1, keepdims=True)

        def tiles(k0, mask, carry):
            return tuple(tile(g, k0, mask, *carry[g]) for g in range(SB_GROUP))

        carry = ((jnp.zeros((2 * blk, PAIR), F32), jnp.zeros((2 * blk, 1), F32)),) * SB_GROUP
        for d in reversed(range(per)):
            carry = tiles(pl.multiple_of(i * blk + d * kb, kb), _sb_mask(blk, kb, d * kb), carry)
        carry = lax.fori_loop(
            1, per * i + 1,
            lambda jj, c: tiles(pl.multiple_of((per * i - jj) * kb, kb), None, c), carry)
        for g, (acc, c_l) in enumerate(carry):
            o_ref[:, SB_LANES[g]] = _sb_unstack(acc, blk)
            ltot_ref[:, SB_LANES[g]] = _sb_unstack(jnp.broadcast_to(c_l, (2 * blk, PAIR)), blk)

    width = SB_GROUP * PAIR
    blkspec = pl.BlockSpec((blk, width), lambda p, i: (i, p))
    n_steps = N_PAIRS // SB_GROUP
    return pl.pallas_call(
        body, name=name, grid=(n_steps, ni),
        in_specs=[blkspec,
                  pl.BlockSpec((t, width), lambda p, i: (0, n_steps + p)),
                  pl.BlockSpec((t, width), lambda p, i: (0, 2 * n_steps + p))],
        out_specs=[blkspec, blkspec],
        out_shape=[_sds((t, D_MODEL)), _sds((t, D_MODEL // 2))],
        compiler_params=_params("arbitrary", "arbitrary"),
    )(qkv, qkv, qkv)


def _sb_bwd(qkv, ltot, do, *, name):
    t = qkv.shape[0]
    blk, kb = min(SB_QB, t), SB_KB
    ni, per = t // blk, blk // kb

    def body(q_ref, k_ref, v_ref, lt_ref, do_ref, dq_ref, dkout_ref, dvout_ref, dk_ref, dv_ref):
        i = pl.program_id(1)

        @pl.when(i == 0)
        def _():
            dk_ref[...] = jnp.zeros_like(dk_ref)
            dv_ref[...] = jnp.zeros_like(dv_ref)

        u_upto = _tri(kb, lambda r, c: r <= c)
        u_before = _tri(kb, lambda r, c: r < c)
        lane = lax.broadcasted_iota(jnp.int32, (1, PAIR), 1)
        q2 = [_bf(_sb_stack(q_ref[:, lanes] * ATT_SCALE)) for lanes in SB_LANES]
        do2 = [_bf(_sb_stack(do_ref[:, lanes])) for lanes in SB_LANES]
        total = [jnp.concatenate(
            [jnp.sum(jnp.where(lane == h * HEAD_DIM, lt_ref[:, lanes], 0.0), axis=1, keepdims=True)
             for h in range(2)], axis=0) for lanes in SB_LANES]

        def tile(g, ops, k0, mask, dq_acc, c_l, c_g):
            qg, dog, tot = ops
            krows = pl.ds(k0, kb)
            kj = k_ref[krows, SB_LANES[g]]
            vj = v_ref[krows, SB_LANES[g]]
            z = _dot_nt(qg, kj)
            sp = _softplus(z)
            sig = jnp.exp(z - sp)
            lf = -sp if mask is None else jnp.where(mask, -sp, 0.0)
            a = jnp.exp(z - sp + tot - (_cumsum01(lf, u_upto) + c_l))
            if mask is not None:
                a = jnp.where(mask, a, 0.0)
            gw = a * _dot_nt(dog, vj)
            g_before = jnp.dot(_bf(gw), u_before, preferred_element_type=F32) + c_g
            dz = gw * (1.0 - sig) - g_before * sig
            if mask is not None:
                dz = jnp.where(mask, dz, 0.0)
            dk_ref[krows, SB_LANES[g]] += _dot_tn(dz, qg)
            dv_ref[krows, SB_LANES[g]] += _dot_tn(a, dog)
            return (dq_acc + _dot(dz, kj), c_l + jnp.sum(lf, axis=1, keepdims=True),
                    c_g + jnp.sum(gw, axis=1, keepdims=True))

        def tiles(ops, k0, mask, carry):
            return tuple(tile(g, ops[g], k0, mask, *carry[g]) for g in range(SB_GROUP))

        ops = tuple(zip(q2, do2, total))
        zero = (jnp.zeros((2 * blk, PAIR), F32), jnp.zeros((2 * blk, 1), F32),
                jnp.zeros((2 * blk, 1), F32))
        carry = lax.fori_loop(
            0, per * i, lambda j, c: tiles(ops, pl.multiple_of(j * kb, kb), None, c),
            (zero,) * SB_GROUP)
        for d in range(per):
            r0 = d * kb
            sub = tiles(tuple(tuple(_sb_rows_from(a, blk, r0) for a in o) for o in ops),
                        pl.multiple_of(i * blk + r0, kb), _sb_mask(blk - r0, kb, 0),
                        tuple(tuple(_sb_rows_from(a, blk, r0) for a in c) for c in carry))
            carry = tuple(tuple(_sb_rows_merge(a, s, blk, r0) for a, s in zip(c, cs))
                          for c, cs in zip(carry, sub))
        for g, (dq_acc, _, _) in enumerate(carry):
            dq_ref[:, SB_LANES[g]] = _bf(_sb_unstack(dq_acc, blk) * ATT_SCALE)

        @pl.when(i == ni - 1)
        def _():
            dkout_ref[...] = _bf(dk_ref[...])
            dvout_ref[...] = _bf(dv_ref[...])

    width = SB_GROUP * PAIR
    n_steps = N_PAIRS // SB_GROUP
    blkspec = lambda off: pl.BlockSpec((blk, width), lambda p, i: (i, off + p))
    full = lambda off: pl.BlockSpec((t, width), lambda p, i: (0, off + p))
    return pl.pallas_call(
        body, name=name, grid=(n_steps, ni),
        in_specs=[blkspec(0), full(n_steps), full(2 * n_steps), blkspec(0), blkspec(0)],
        out_specs=[blkspec(0), full(0), full(0)],
        out_shape=[_sds((t, D_MODEL), BF16)] * 3,
        scratch_shapes=[pltpu.VMEM((t, width), F32), pltpu.VMEM((t, width), F32)],
        compiler_params=_params("arbitrary", "arbitrary"),
    )(qkv, qkv, qkv, _hbm(ltot), do)


def _ch_mask(i):
    r = lax.broadcasted_iota(jnp.int32, (CH_QB, CH_WIN), 0)
    c = lax.broadcasted_iota(jnp.int32, (CH_QB, CH_WIN), 1)
    qc = LOOKBACK + lax.shift_right_arithmetic(r, 6)
    kc = lax.shift_right_arithmetic(c, 6)
    first = i * (CH_QB // CHUNK) - LOOKBACK
    return (kc <= qc) & (kc >= qc - LOOKBACK) & (kc + first >= 0)


def _ch_probs(qm, kw, bias_h, mask):
    z = _dot_nt(qm, kw) * ATT_SCALE + bias_h
    z = jnp.where(mask, z, NEG_INF)
    e = jnp.exp(z - jnp.max(z, axis=1, keepdims=True))
    return e / jnp.sum(e, axis=1, keepdims=True)


def _ch_fill(pad_ref, src_ref, t):
    pad_ref[pl.ds(0, CH_LOOK), :] = jnp.zeros((CH_LOOK, PAIR), BF16)
    pad_ref[pl.ds(CH_LOOK, t), :] = _bf(src_ref[...])


def _ch_fwd(qkv, bias, o_in, *, name):
    t = qkv.shape[0]
    ni = t // CH_QB

    def body(q_ref, k_ref, v_ref, bias_ref, _alias, o_ref, kpad, vpad):
        i = pl.program_id(1)

        @pl.when(i == 0)
        def _():
            _ch_fill(kpad, k_ref, t)
            _ch_fill(vpad, v_ref, t)

        win = pl.ds(pl.multiple_of(i * CH_QB, CH_QB), CH_WIN)
        kw, vw = kpad[win, :], vpad[win, :]
        mask = _ch_mask(i)
        q = q_ref[...]
        outs = []
        for h, hm in enumerate(_head_masks()):
            p = _ch_probs(jnp.where(hm, q, 0.0), kw, bias_ref[h], mask)
            outs.append(_dot(p, vw))
        o_ref[...] = jnp.where(_head_masks()[0], outs[0], outs[1])

    full = lambda off: pl.BlockSpec((t, PAIR), lambda p, i: (0, off + p))
    return pl.pallas_call(
        body, name=name, grid=(N_PAIRS, ni),
        in_specs=[pl.BlockSpec((CH_QB, PAIR), lambda p, i: (i, 3 * N_PAIRS + p)),
                  full(4 * N_PAIRS), full(5 * N_PAIRS),
                  pl.BlockSpec((2, CH_QB, CH_WIN), lambda p, i: (p, 0, 0)), ANY],
        out_specs=pl.BlockSpec((CH_QB, PAIR), lambda p, i: (i, N_PAIRS + p)),
        out_shape=_sds((t, D_MODEL)),
        scratch_shapes=[pltpu.VMEM((t + CH_LOOK, PAIR), BF16)] * 2,
        input_output_aliases={4: 0},
        compiler_params=_params("arbitrary", "arbitrary"),
    )(qkv, qkv, qkv, bias, o_in)


def _ch_bwd(qkv, bias, o, do, dq_in, dk_in, dv_in, *, name):
    t = qkv.shape[0]
    ni = t // CH_QB

    def body(q_ref, k_ref, v_ref, bias_ref, o_ref, do_ref, _a0, _a1, _a2,
             dq_ref, dkout_ref, dvout_ref, dbias_ref, kpad, vpad, dkpad, dvpad):
        i = pl.program_id(1)

        @pl.when(i == 0)
        def _():
            _ch_fill(kpad, k_ref, t)
            _ch_fill(vpad, v_ref, t)
            dkpad[...] = jnp.zeros_like(dkpad)
            dvpad[...] = jnp.zeros_like(dvpad)
            dbias_ref[...] = jnp.zeros_like(dbias_ref)

        win = pl.ds(pl.multiple_of(i * CH_QB, CH_QB), CH_WIN)
        kw, vw = kpad[win, :], vpad[win, :]
        mask = _ch_mask(i)
        q, o_blk, do_blk = q_ref[...], o_ref[...], do_ref[...]
        dqs = []
        for h, hm in enumerate(_head_masks()):
            qm = _bf(jnp.where(hm, q, 0.0))
            dom = jnp.where(hm, do_blk, 0.0)
            delta = jnp.sum(dom * o_blk, axis=1, keepdims=True)
            dom = _bf(dom)
            p = _ch_probs(qm, kw, bias_ref[h], mask)
            ds = p * (_dot_nt(dom, vw) - delta)
            dbias_ref[h] += ds
            dsz = ds * ATT_SCALE
            dqs.append(_dot(dsz, kw))
            dkpad[win, :] += _dot_tn(dsz, qm)
            dvpad[win, :] += _dot_tn(p, dom)
        dq_ref[...] = _bf(jnp.where(_head_masks()[0], dqs[0], dqs[1]))

        @pl.when(i == ni - 1)
        def _():
            dkout_ref[...] = _bf(dkpad[pl.ds(CH_LOOK, t), :])
            dvout_ref[...] = _bf(dvpad[pl.ds(CH_LOOK, t), :])

    blkspec = lambda off: pl.BlockSpec((CH_QB, PAIR), lambda p, i: (i, off + p))
    full = lambda off: pl.BlockSpec((t, PAIR), lambda p, i: (0, off + p))
    bias_spec = pl.BlockSpec((2, CH_QB, CH_WIN), lambda p, i: (p, 0, 0))
    return pl.pallas_call(
        body, name=name, grid=(N_PAIRS, ni),
        in_specs=[blkspec(3 * N_PAIRS), full(4 * N_PAIRS), full(5 * N_PAIRS), bias_spec,
                  blkspec(N_PAIRS), blkspec(N_PAIRS), ANY, ANY, ANY],
        out_specs=[blkspec(N_PAIRS), full(N_PAIRS), full(N_PAIRS), bias_spec],
        out_shape=[_sds((t, D_MODEL), BF16)] * 3 + [_sds((2 * N_PAIRS, CH_QB, CH_WIN))],
        scratch_shapes=[pltpu.VMEM((t + CH_LOOK, PAIR), BF16)] * 2
        + [pltpu.VMEM((t + CH_LOOK, PAIR), F32)] * 2,
        input_output_aliases={6: 0, 7: 1, 8: 2},
        compiler_params=_params("arbitrary", "arbitrary"),
    )(qkv, qkv, qkv, bias, o, do, dq_in, dk_in, dv_in)


def _bias_expand(fvec, *, name):
    n_heads = fvec.shape[0]

    def body(f_ref, o_ref, rows8):
        row = f_ref[0]
        for r in range(8):
            rows8[pl.ds(r, 1), :] = pltpu.roll(row, r, 1)
        base = rows8[...]
        for blk in range(CH_QB // 8):
            o_ref[0, pl.ds(8 * blk, 8), :] = pltpu.roll(base, 8 * blk, 1)

    return pl.pallas_call(
        body, name=name, grid=(n_heads,),
        in_specs=[pl.BlockSpec((1, 1, CH_WIN), lambda h: (h, 0, 0))],
        out_specs=pl.BlockSpec((1, CH_QB, CH_WIN), lambda h: (h, 0, 0)),
        out_shape=_sds((n_heads, CH_QB, CH_WIN)),
        scratch_shapes=[pltpu.VMEM((8, CH_WIN), F32)],
        compiler_params=_params("arbitrary"),
    )(fvec)


def _bias_grad(dbias, *, name):
    n_heads = dbias.shape[0]
    first = CH_LOOK - REL_CLIP

    def body(d_ref, o_ref, acc8):
        acc = jnp.zeros((8, CH_WIN), F32)
        for blk in range(CH_QB // 8):
            acc = acc + pltpu.roll(d_ref[0, pl.ds(8 * blk, 8), :], (CH_WIN - 8 * blk) % CH_WIN, 1)
        acc8[...] = acc
        dvec = jnp.zeros((1, CH_WIN), F32)
        for r in range(8):
            dvec = dvec + pltpu.roll(acc8[pl.ds(r, 1), :], (CH_WIN - r) % CH_WIN, 1)
        lane = lax.broadcasted_iota(jnp.int32, (1, CH_WIN), 1)
        clipped = (lane <= first) | (lane >= first + REL_CLIP + CHUNK)
        total = jnp.sum(jnp.where(clipped, dvec, 0.0), axis=1, keepdims=True)
        o_ref[0] = jnp.where(lane == first, total, dvec)

    return pl.pallas_call(
        body, name=name, grid=(n_heads,),
        in_specs=[pl.BlockSpec((1, CH_QB, CH_WIN), lambda h: (h, 0, 0))],
        out_specs=pl.BlockSpec((1, 1, CH_WIN), lambda h: (h, 0, 0)),
        out_shape=_sds((n_heads, 1, CH_WIN)),
        scratch_shapes=[pltpu.VMEM((8, CH_WIN), F32)],
        compiler_params=_params("arbitrary"),
    )(dbias)


def _out_fwd(o, h1, g_sb, g_ch, g_post, wout, *, name):
    t = o.shape[0]
    tm = 512
    half = D_MODEL // 2

    def body(o_ref, h_ref, gsb_ref, gch_ref, gpost_ref, w_ref, h2_ref, mixed_ref, y_ref):
        ov = o_ref[...]
        mixed = jnp.concatenate([_rms(ov[:, :half], gsb_ref[...]),
                                 _rms(ov[:, half:], gch_ref[...])], axis=1)
        mixed_ref[...] = _bf(mixed)
        y = _dot(mixed, w_ref[...])
        y_ref[...] = y
        h2_ref[...] = h_ref[...] + _rms(y, gpost_ref[...])

    row = pl.BlockSpec((tm, D_MODEL), lambda i: (i, 0))
    gain = lambda n: pl.BlockSpec((1, n), lambda i: (0, 0))
    return pl.pallas_call(
        body, name=name, grid=(t // tm,),
        in_specs=[row, row, gain(half), gain(half), gain(D_MODEL),
                  pl.BlockSpec((D_MODEL, D_MODEL), lambda i: (0, 0))],
        out_specs=[row, row, row],
        out_shape=[_sds((t, D_MODEL)), _sds((t, D_MODEL), BF16), _sds((t, D_MODEL))],
        compiler_params=_params("arbitrary"),
    )(o, h1, g_sb, g_ch, g_post, wout)


def _out_bwd(dy, mixed, o, g_sb, g_ch, wout, *, name):
    t = o.shape[0]
    tm = 512
    ni = t // tm
    half = D_MODEL // 2

    def body(dy_ref, mixed_ref, o_ref, gsb_ref, gch_ref, w_ref,
             dw_ref, do_ref, dgsb_ref, dgch_ref, acc_ref):
        i = pl.program_id(0)

        @pl.when(i == 0)
        def _():
            acc_ref[...] = jnp.zeros_like(acc_ref)
            dgsb_ref[...] = jnp.zeros_like(dgsb_ref)
            dgch_ref[...] = jnp.zeros_like(dgch_ref)

        dyv = dy_ref[...]
        acc_ref[...] += _dot_tn(mixed_ref[...], dyv)
        dm = _dot_nt(dyv, w_ref[...])
        ov = o_ref[...]
        doa, dga = _rms_bwd(dm[:, :half], ov[:, :half], gsb_ref[...])
        dob, dgb = _rms_bwd(dm[:, half:], ov[:, half:], gch_ref[...])
        do_ref[...] = jnp.concatenate([doa, dob], axis=1)
        dgsb_ref[...] += dga
        dgch_ref[...] += dgb

        @pl.when(i == ni - 1)
        def _():
            dw_ref[...] = _bf(acc_ref[...])

    row = pl.BlockSpec((tm, D_MODEL), lambda i: (i, 0))
    gain = pl.BlockSpec((1, half), lambda i: (0, 0))
    sq = pl.BlockSpec((D_MODEL, D_MODEL), lambda i: (0, 0))
    return pl.pallas_call(
        body, name=name, grid=(ni,),
        in_specs=[row, row, row, gain, gain, sq],
        out_specs=[sq, row, gain, gain],
        out_shape=[_sds((D_MODEL, D_MODEL), BF16), _sds((t, D_MODEL)),
                   _sds((1, half)), _sds((1, half))],
        scratch_shapes=[pltpu.VMEM((D_MODEL, D_MODEL), F32)],
        compiler_params=_params("arbitrary"),
    )(_hbm(dy), mixed, o, g_sb, g_ch, wout)


def _ple(p, h3, target, wp, wgate, g, *, name):
    t = h3.shape[0]
    tm = 512
    ni = t // tm

    def body(p_ref, h_ref, tgt_ref, wp_ref, wg_ref, g_ref,
             loss_ref, dres_ref, dwp_ref, dwg_ref, dg_ref, accp, accg):
        i = pl.program_id(0)

        @pl.when(i == 0)
        def _():
            loss_ref[...] = jnp.zeros_like(loss_ref)
            dg_ref[...] = jnp.zeros_like(dg_ref)
            accp[...] = jnp.zeros_like(accp)
            accg[...] = jnp.zeros_like(accg)

        pv, hv, gv = p_ref[...], h_ref[...], g_ref[...]
        pe = _dot(pv, wp_ref[...])
        sig = _sigmoid(_dot(hv, wg_ref[...]))
        e = pe * sig
        err = hv + _rms(e, gv) - tgt_ref[...]
        tok = jnp.mean(err * err, axis=-1, keepdims=True)
        loss_ref[...] += 0.5 * jnp.sum(tok, axis=0, keepdims=True)
        dh4 = err * (1.0 / D_MODEL)
        de, dg = _rms_bwd(dh4, e, gv)
        dg_ref[...] += dg
        dpe = de * sig
        dgt = de * pe * sig * (1.0 - sig)
        accp[...] += _dot_tn(pv, dpe)
        accg[...] += _dot_tn(hv, dgt)
        dres_ref[...] = dh4 + _dot_nt(dgt, wg_ref[...])

        @pl.when(i == ni - 1)
        def _():
            dwp_ref[...] = _bf(accp[...])
            dwg_ref[...] = _bf(accg[...])

    row = pl.BlockSpec((tm, D_MODEL), lambda i: (i, 0))
    const = lambda r, c: pl.BlockSpec((r, c), lambda i: (0, 0))
    return pl.pallas_call(
        body, name=name, grid=(ni,),
        in_specs=[pl.BlockSpec((tm, PLE_DIM), lambda i: (i, 0)), row, row,
                  const(PLE_DIM, D_MODEL), const(D_MODEL, D_MODEL), const(1, D_MODEL)],
        out_specs=[const(1, 128), row, const(PLE_DIM, D_MODEL), const(D_MODEL, D_MODEL),
                   const(1, D_MODEL)],
        out_shape=[_sds((1, 128)), _sds((t, D_MODEL)), _sds((PLE_DIM, D_MODEL), BF16),
                   _sds((D_MODEL, D_MODEL), BF16), _sds((1, D_MODEL))],
        scratch_shapes=[pltpu.VMEM((PLE_DIM, D_MODEL), F32), pltpu.VMEM((D_MODEL, D_MODEL), F32)],
        compiler_params=_params("arbitrary"),
    )(p, h3, target, wp, wgate, g)


def _rel_bias_to_fvec(rel_bias):
    rev = rel_bias[:, ::-1]
    n_heads = rel_bias.shape[0]
    first = CH_LOOK - REL_CLIP
    n_var = REL_CLIP + CHUNK
    clipped = rev[:, :1]
    fvec = jnp.concatenate([jnp.broadcast_to(clipped, (n_heads, first)), rev[:, :n_var],
                            jnp.broadcast_to(clipped, (n_heads, CH_WIN - first - n_var))], axis=1)
    return fvec.reshape(n_heads, 1, CH_WIN)


def _fvec_grad_to_rel_bias(dfvec):
    first = CH_LOOK - REL_CLIP
    n_var = REL_CLIP + CHUNK
    rev = jnp.pad(dfvec[:, 0, first:first + n_var], ((0, 0), (0, N_REL - n_var)))
    return rev[:, ::-1]


def _local_step(x, p, target, g, weights_for, grads_done, fvec, weights_early=None):
    w, tie = weights_for(0, x)
    w = dict(w)
    h1, n1, a1, b1, f1 = _ffn_fwd(x, g["ffn1_pre"] + tie, g["ffn1_post"],
                                  w["ffn1_gate"], w["ffn1_up"], w["ffn1_down"], name="ffn1_fwd")
    more, tie = weights_for(1, h1)
    w.update(more)
    qkv, u = _qkv_fwd(h1, g["mix_pre"] + tie, w["in"], name="qkv_fwd")
    bias = _bias_expand(fvec, name="bias_expand")
    o, ltot = _sb_fwd(qkv, name="sb_fwd")
    tie = weights_early(2, ltot) if weights_early else 0.0
    o = _ch_fwd(qkv, bias, o, name="ch_fwd")
    h2, mixed, y = _out_fwd(o, h1, g["out_sb"] + tie, g["out_ch"], g["mix_post"], w["out"],
                            name="out_fwd")
    w.update(weights_for(2, h2)[0])
    h3, n2, a2, b2, f2 = _ffn_fwd(h2, g["ffn2_pre"], g["ffn2_post"],
                                  w["ffn2_gate"], w["ffn2_up"], w["ffn2_down"], name="ffn2_fwd")
    loss, dh3, dwp, dwgate, dg_ple = _ple(p, h3, target, w["ple_proj"], w["ple_gate"],
                                          g["ple_post"], name="ple")
    tie = grads_done(0, {"ple_proj": dwp, "ple_gate": dwgate})

    df2, dg_ffn2_post = _junction(dh3, post=(f2, g["ffn2_post"] + tie, 0.5), name="junction3")
    dwg2, dwu2, dwd2, dn2 = _ffn_bwd(n2, df2, a2, b2, w["ffn2_gate"], w["ffn2_up"],
                                     w["ffn2_down"], name="ffn2_bwd")
    tie = grads_done(1, {"ffn2_gate": dwg2, "ffn2_up": dwu2, "ffn2_down": dwd2})
    dh2, dg_ffn2_pre, dy, dg_mix_post = _junction(
        dh3, pre=(dn2, h2, g["ffn2_pre"] + tie), post=(y, g["mix_post"], 1.0), name="junction2")
    dwout, do, dg_sb, dg_ch = _out_bwd(dy, mixed, o, g["out_sb"], g["out_ch"], w["out"],
                                       name="out_bwd")
    dq, dk, dv = _sb_bwd(qkv, ltot, do, name="sb_bwd")
    dq, dk, dv, dbias = _ch_bwd(qkv, bias, o, do, dq, dk, dv, name="ch_bwd")
    dfvec = _bias_grad(dbias, name="bias_grad")
    dwin, du = _qkv_bwd(dq, dk, dv, u, w["in"], name="qkv_bwd")
    tie = grads_done(2, {"out": dwout, "in": dwin})
    dh1, dg_mix_pre, df1, dg_ffn1_post = _junction(
        dh2, pre=(du, h1, g["mix_pre"] + tie), post=(f1, g["ffn1_post"], 0.5), name="junction1")
    dwg1, dwu1, dwd1, dn1 = _ffn_bwd(n1, df1, a1, b1, w["ffn1_gate"], w["ffn1_up"],
                                     w["ffn1_down"], name="ffn1_bwd")
    tie = grads_done(3, {"ffn1_gate": dwg1, "ffn1_up": dwu1, "ffn1_down": dwd1})
    dx, dg_ffn1_pre = _junction(dh1, pre=(dn1, x, g["ffn1_pre"] + tie), name="junction0")

    dg = {"ffn1_pre": dg_ffn1_pre, "ffn1_post": dg_ffn1_post, "mix_pre": dg_mix_pre,
          "mix_post": dg_mix_post, "out_sb": dg_sb, "out_ch": dg_ch,
          "ffn2_pre": dg_ffn2_pre, "ffn2_post": dg_ffn2_post, "ple_post": dg_ple}
    return loss, dx, dg, dfvec


_WEIGHTS = (
    ("ffn1_gate", "row", FF_SHARD, FF_SHARD_PAD, D_MODEL),
    ("ffn1_up", "row", FF_SHARD, FF_SHARD_PAD, D_MODEL),
    ("ffn1_down", "row", FF_SHARD, FF_SHARD_PAD, D_MODEL),
    ("in", "col", QKV_SHARD, QKV_SHARD, D_MODEL),
    ("out", "row", ROW_SHARD, ROW_SHARD, D_MODEL),
    ("ffn2_gate", "row", FF_SHARD, FF_SHARD_PAD, D_MODEL),
    ("ffn2_up", "row", FF_SHARD, FF_SHARD_PAD, D_MODEL),
    ("ffn2_down", "row", FF_SHARD, FF_SHARD_PAD, D_MODEL),
    ("ple_proj", "col", ROW_SHARD, ROW_SHARD, PLE_DIM),
    ("ple_gate", "row", ROW_SHARD, ROW_SHARD, D_MODEL),
)
_TRANSPOSED = ("ffn1_gate", "ffn1_up", "ffn2_gate", "ffn2_up")
_SPEC = {n: (kind, valid, pad, other) for n, kind, valid, pad, other in _WEIGHTS}
_GATHER_STAGES = (("ffn1_gate", "ffn1_up", "ffn1_down"), ("in", "out"),
                  ("ffn2_gate", "ffn2_up", "ffn2_down", "ple_proj", "ple_gate"))
_SCATTER_STAGES = (("ple_proj", "ple_gate"), ("ffn2_gate", "ffn2_up", "ffn2_down"),
                   ("out", "in"), ("ffn1_gate", "ffn1_up", "ffn1_down"))
HBM = pl.BlockSpec(memory_space=pltpu.HBM)
SEM = pl.BlockSpec(memory_space=pltpu.SEMAPHORE)
EFFECT = pltpu.SideEffectType.DATAFLOW_SIDE_EFFECTING


def _shard_shape(kind, size, other):
    return (other, size) if kind == "col" else (size, other)


def _window(ref, kind, start, size):
    return ref.at[:, pl.ds(start, size)] if kind == "col" else ref.at[pl.ds(start, size), :]


def _device_tuple(k):
    return (k // 4, (k // 2) % 2, k % 2)


def _my_index():
    return 4 * lax.axis_index("x") + 2 * lax.axis_index("y") + lax.axis_index("c")


def _pack_weights(shards):
    nw = len(_WEIGHTS)

    def body(*refs):
        ins, packed, full = refs[:nw], refs[nw:2 * nw], refs[2 * nw:3 * nw]
        sem = refs[3 * nw]
        me = _my_index()
        for (_, kind, valid, pad, _), src, dst in zip(_WEIGHTS, ins, packed):
            if pad != valid:
                dst[...] = jnp.zeros_like(dst)
            if kind == "col":
                dst[:, pl.ds(0, valid)] = _bf(src[...])
            else:
                dst[pl.ds(0, valid), :] = _bf(src[...])
        for k in range(N_DEV):
            @pl.when(me == k)
            def _():
                for w, (_, kind, _, pad, _) in enumerate(_WEIGHTS):
                    pltpu.make_async_copy(packed[w], _window(full[w], kind, k * pad, pad),
                                          sem.at[w]).start()
        for w, (_, kind, _, pad, _) in enumerate(_WEIGHTS):
            pltpu.make_async_copy(packed[w], _window(full[w], kind, 0, pad), sem.at[w]).wait()

    whole = lambda shape: pl.BlockSpec(shape, lambda i: (0, 0))
    packed_shapes = [_shard_shape(kind, pad, other) for _, kind, _, pad, other in _WEIGHTS]
    outs = pl.pallas_call(
        body, name="pack_weights", grid=(1,),
        in_specs=[whole(a.shape) for a in shards],
        out_specs=[whole(s) for s in packed_shapes] + [ANY] * nw,
        out_shape=[_sds(s, BF16) for s in packed_shapes]
        + [_sds(_shard_shape(kind, N_DEV * pad, other), BF16) for _, kind, _, pad, other in _WEIGHTS],
        scratch_shapes=[pltpu.SemaphoreType.DMA((nw,))],
        compiler_params=_params("arbitrary"),
    )(*shards)
    names = [n for n, *_ in _WEIGHTS]
    return dict(zip(names, outs[:nw])), dict(zip(names, outs[nw:]))


def _hbm(a):
    return pltpu.with_memory_space_constraint(a, pltpu.HBM)


def _split_start(name, n, body_copies, sources, lands, after):
    arrays = list(sources) + list(lands)
    ns, na = len(sources), len(arrays)

    def body(*refs):
        src, land = refs[:ns], refs[ns:na]
        send, recv = refs[na + 1], refs[na + 2]
        token = refs[-1]
        body_copies(src, land, send, recv)
        token[...] = jnp.zeros_like(token)

    out = pl.pallas_call(
        body, name=name,
        out_shape=(pltpu.SemaphoreType.DMA((n,)), pltpu.SemaphoreType.DMA((n,)),
                   *[pltpu.HBM(a.shape, a.dtype) for a in arrays], _sds((8, 128))),
        in_specs=[HBM] * na + [ANY], out_specs=(SEM, SEM, *[HBM] * na, VMEM),
        input_output_aliases={i: 2 + i for i in range(na)},
        compiler_params=pltpu.CompilerParams(has_side_effects=EFFECT),
    )(*[_hbm(a) for a in arrays], after)
    return out[0], out[1], out[2:2 + ns], out[2 + ns:2 + na], out[-1]


def _split_wait(name, n, seven_of, send, recv, sources, lands, after, keep_sources=False):
    arrays = list(sources) + list(lands)
    ns, na = len(sources), len(arrays)

    def body(*refs):
        land = refs[ns:na]
        send_ref, recv_ref = refs[na], refs[na + 1]
        myself = (lax.axis_index("x"), lax.axis_index("y"), lax.axis_index("c"))
        for w in range(n):
            seven = seven_of(w, land[w])
            copy = pltpu.make_async_remote_copy(
                src_ref=seven, dst_ref=seven, send_sem=send_ref.at[w], recv_sem=recv_ref.at[w],
                device_id=myself, device_id_type=MESH)
            copy.wait_send()
            copy.wait_recv()

    out = pl.pallas_call(
        body, name=name,
        out_shape=[pltpu.HBM(a.shape, a.dtype) for a in arrays],
        in_specs=[HBM] * na + [SEM, SEM, ANY], out_specs=[HBM] * na,
        input_output_aliases={i: i for i in range(na)},
        compiler_params=pltpu.CompilerParams(has_side_effects=EFFECT),
    )(*arrays, send, recv, after)
    return out if keep_sources else out[ns:]


_ALL_PEERS = (1, 2, 3, 4, 5, 6, 7)
_NEAR_PEERS = (1, 2, 4, 6)
_FAR_CHIPS = (2, 4, 6)


def _gather_start(stage, names, packed, full, after, peers=_ALL_PEERS):
    def copies(src, land, send, recv):
        me = _my_index()
        for k in range(N_DEV):
            @pl.when(me == k)
            def _():
                for w, name in enumerate(names):
                    kind, _, pad, _ = _SPEC[name]
                    dst = _window(land[w], kind, k * pad, pad)
                    for mask in peers:
                        pltpu.make_async_remote_copy(
                            src_ref=src[w], dst_ref=dst, send_sem=send.at[w],
                            recv_sem=recv.at[w], device_id=_device_tuple(k ^ mask),
                            device_id_type=MESH).start()

    return _split_start(f"gather_start{stage}", len(names), copies,
                        [packed[n] for n in names], [full[n] for n in names], after)


def _gather_wait(stage, names, started, after, count=N_DEV - 1):
    send, recv, src, land, _ = started

    def bytes_of(w, ref):
        kind, _, pad, _ = _SPEC[names[w]]
        return _window(ref, kind, 0, count * pad)

    return dict(zip(names, _split_wait(f"gather_wait{stage}", len(names), bytes_of,
                                       send, recv, src, land, after)))


def _relay_start(stage, names, full, after):
    def copies(_, land, send, recv):
        me = _my_index()
        for k in range(N_DEV):
            @pl.when(me == k)
            def _():
                for w, name in enumerate(names):
                    kind, _, pad, _ = _SPEC[name]
                    for mask in _FAR_CHIPS:
                        win = _window(land[w], kind, (k ^ mask) * pad, pad)
                        pltpu.make_async_remote_copy(
                            src_ref=win, dst_ref=win, send_sem=send.at[w], recv_sem=recv.at[w],
                            device_id=_device_tuple(k ^ 1), device_id_type=MESH).start()

    return _split_start(f"relay_start{stage}", len(names), copies, [],
                        [full[n] for n in names], after)


def _scatter_start(stage, names, grads, after):
    def copies(src, land, send, recv):
        me = _my_index()
        for k in range(N_DEV):
            @pl.when(me != k)
            def _():
                slot = lax.rem(me + (N_DEV - 1 - k), N_DEV)
                for w, name in enumerate(names):
                    kind, _, pad, _ = _SPEC[name]
                    pltpu.make_async_remote_copy(
                        src_ref=_window(src[w], kind, k * pad, pad), dst_ref=land[w].at[slot],
                        send_sem=send.at[w], recv_sem=recv.at[w],
                        device_id=_device_tuple(k), device_id_type=MESH).start()

    lands = [lax.empty((N_DEV - 1,) + _shard_shape(_SPEC[m][0], _SPEC[m][2], _SPEC[m][3]), BF16)
             for m in names]
    return _split_start(f"scatter_start{stage}", len(names), copies, grads, lands, after)


def _scatter_wait(stage, names, started, after):
    send, recv, src, land, _ = started
    n = len(names)
    out = _split_wait(f"scatter_wait{stage}", n, lambda w, ref: ref, send, recv, src, land, after,
                      keep_sources=True)
    return dict(zip(names, out[:n])), dict(zip(names, out[n:]))


N_CHIPS = N_DEV // 2


def _pair_start(stage, names, grads, after):
    def copies(src, land, send, recv):
        me = _my_index()
        for k in range(N_DEV):
            @pl.when(me == k)
            def _():
                for w, name in enumerate(names):
                    kind, _, pad, _ = _SPEC[name]
                    for chip in range(N_CHIPS):
                        j = 2 * chip + ((k ^ 1) & 1)
                        pltpu.make_async_remote_copy(
                            src_ref=_window(src[w], kind, j * pad, pad), dst_ref=land[w].at[chip],
                            send_sem=send.at[w], recv_sem=recv.at[w],
                            device_id=_device_tuple(k ^ 1), device_id_type=MESH).start()

    lands = [lax.empty((N_CHIPS,) + _shard_shape(_SPEC[m][0], _SPEC[m][2], _SPEC[m][3]), BF16)
             for m in names]
    return _split_start(f"pair_start{stage}", len(names), copies, grads, lands, after)


def _pair_sum(dw_full, pair, *, pad, name):
    other = dw_full.shape[1]

    def body(own_ref, pair_ref, out_ref):
        out_ref[0] = _bf(own_ref[...].astype(F32) + pair_ref[0].astype(F32))

    slot = pl.BlockSpec((1, pad, other), lambda q: (q, 0, 0))
    return pl.pallas_call(
        body, name=name, grid=(N_CHIPS,),
        in_specs=[pl.BlockSpec((pad, other), lambda q: (2 * q + lax.axis_index("c"), 0)), slot],
        out_specs=slot, out_shape=_sds((N_CHIPS, pad, other), BF16),
        compiler_params=_params("arbitrary"),
    )(dw_full, pair)


def _chip_start(stage, names, sums, after):
    def copies(src, land, send, recv):
        me = _my_index()
        my_chip = lax.shift_right_logical(me, 1)
        for k in range(N_DEV):
            @pl.when((me != k) & (((me ^ k) & 1) == 0))
            def _():
                slot = lax.rem(my_chip + (N_CHIPS - 1 - k // 2), N_CHIPS)
                for w in range(len(names)):
                    pltpu.make_async_remote_copy(
                        src_ref=src[w].at[k // 2], dst_ref=land[w].at[slot],
                        send_sem=send.at[w], recv_sem=recv.at[w],
                        device_id=_device_tuple(k), device_id_type=MESH).start()

    lands = [lax.empty((N_CHIPS - 1,) + a.shape[1:], BF16) for a in sums]
    return _split_start(f"chip_start{stage}", len(names), copies, sums, lands, after)


def _adamw_chip(w, m, v, land, sums, *, name):
    shape = w.shape

    def body(w_ref, m_ref, v_ref, land_ref, own_ref, *outs):
        rows = pl.ds(0, shape[0])
        grad = own_ref[0, rows, :].astype(F32)
        for s in range(N_CHIPS - 1):
            grad = grad + land_ref[s, rows, :].astype(F32)
        _adam_update(w_ref, m_ref, v_ref, grad, *outs)

    whole = lambda a: pl.BlockSpec(a.shape, lambda i: (0,) * a.ndim)
    own = pl.BlockSpec((1,) + sums.shape[1:],
                       lambda i: (2 * lax.axis_index("x") + lax.axis_index("y"), 0, 0))
    return pl.pallas_call(
        body, name=name, grid=(1,),
        in_specs=[whole(w), whole(m), whole(v), whole(land), own],
        out_specs=[whole(w)] * 4, out_shape=[_sds(shape)] * 4,
        compiler_params=_params("arbitrary"),
    )(w, m, v, land, sums)


def _small_start(small, after):
    def copies(src, land, send, recv):
        me = _my_index()
        for k in range(N_DEV):
            @pl.when(me != k)
            def _():
                slot = lax.rem(me + (N_DEV - 1 - k), N_DEV)
                pltpu.make_async_remote_copy(
                    src_ref=src[0], dst_ref=land[0].at[slot], send_sem=send.at[0],
                    recv_sem=recv.at[0], device_id=_device_tuple(k), device_id_type=MESH).start()

    lands = [lax.empty((N_DEV - 1,) + small.shape, F32)]
    return _split_start("small_start", 1, copies, [small], lands, after)


def _sum_small(own, land):
    def body(own_ref, land_ref, out_ref):
        me = _my_index()
        total = None
        for s in range(N_DEV):
            slot = jnp.minimum(lax.rem(s + (N_DEV - 1) - me + N_DEV, N_DEV), N_DEV - 2)
            term = jnp.where(me == s, own_ref[...], land_ref[slot])
            total = term if total is None else total + term
        out_ref[...] = total

    return pl.pallas_call(
        body, name="sum_small", in_specs=[VMEM, VMEM], out_specs=VMEM,
        out_shape=_sds(own.shape),
    )(own, land)


def _adam_update(w_ref, m_ref, v_ref, grad, grad_ref, delta_ref, nm_ref, nv_ref):
    new_m = ADAM_B1 * m_ref[...] + (1.0 - ADAM_B1) * grad
    new_v = ADAM_B2 * v_ref[...] + (1.0 - ADAM_B2) * (grad * grad)
    m_hat = new_m / (1.0 - ADAM_B1 ** ADAM_STEP)
    v_hat = new_v / (1.0 - ADAM_B2 ** ADAM_STEP)
    grad_ref[...] = grad
    delta_ref[...] = -ADAM_LR * (m_hat / (jnp.sqrt(v_hat) + ADAM_EPS) + ADAM_WD * w_ref[...])
    nm_ref[...] = new_m
    nv_ref[...] = new_v


def _adamw(w, m, v, g, *, name):
    def body(w_ref, m_ref, v_ref, g_ref, *outs):
        _adam_update(w_ref, m_ref, v_ref, g_ref[...], *outs)

    whole = pl.BlockSpec(w.shape, lambda i: (0,) * w.ndim)
    return pl.pallas_call(
        body, name=name, grid=(1,), in_specs=[whole] * 4, out_specs=[whole] * 4,
        out_shape=[_sds(w.shape)] * 4, compiler_params=_params("arbitrary"),
    )(w, m, v, g)


def _adamw_gains(small, params):
    n = len(params)

    def body(small_ref, *refs):
        ins, outs = refs[:3 * n], refs[3 * n:]
        for r in range(n):
            width = ins[3 * r].shape[1]
            if width == D_MODEL:
                grad = small_ref[pl.ds(r, 1), :]
            else:
                grad = small_ref[pl.ds(len(_GAINS), 1), pl.ds((r - len(_GAINS)) * width, width)]
            _adam_update(*ins[3 * r:3 * r + 3], grad, *outs[4 * r:4 * r + 4])

    whole = lambda a: pl.BlockSpec(a.shape, lambda i: (0, 0))
    flat = [a for group in params for a in group]
    return pl.pallas_call(
        body, name="adamw_gains", grid=(1,),
        in_specs=[whole(small)] + [whole(a) for a in flat],
        out_specs=[whole(w) for w, _, _ in params for _ in range(4)],
        out_shape=[_sds(w.shape) for w, _, _ in params for _ in range(4)],
        compiler_params=_params("arbitrary"),
    )(small, *flat)


def _adamw_shard(w, m, v, land, dw_full, *, kind, pad, name):
    shape = w.shape
    other = shape[0] if kind == "col" else shape[1]

    def body(w_ref, m_ref, v_ref, land_ref, own_ref, *outs):
        valid = ((slice(None), pl.ds(0, shape[1])) if kind == "col"
                 else (pl.ds(0, shape[0]), slice(None)))
        grad = own_ref[valid].astype(F32)
        for s in range(N_DEV - 1):
            grad = grad + land_ref[(s,) + valid].astype(F32)
        _adam_update(w_ref, m_ref, v_ref, grad, *outs)

    whole = lambda a: pl.BlockSpec(a.shape, lambda i: (0,) * a.ndim)
    own = pl.BlockSpec(_shard_shape(kind, pad, other),
                       (lambda i: (0, _my_index())) if kind == "col" else (lambda i: (_my_index(), 0)))
    return pl.pallas_call(
        body, name=name, grid=(1,),
        in_specs=[whole(w), whole(m), whole(v), whole(land), own],
        out_specs=[whole(w)] * 4, out_shape=[_sds(shape)] * 4,
        compiler_params=_params("arbitrary"),
    )(w, m, v, land, dw_full)


_GAINS = ("ffn1_pre", "ffn1_post", "mix_pre", "mix_post", "ffn2_pre", "ffn2_post", "ple_post")
_SMALL_ROWS = 16


def _stack_gains(get):
    return jnp.concatenate([get(n) for n in _GAINS]
                           + [jnp.concatenate([get("out_sb"), get("out_ch")], axis=1)], axis=0)


def kernel(x, p, g_ffn1_pre, g_ffn1_post, w_ffn1_gate, w_ffn1_up, w_ffn1_down, g_mix_pre, g_mix_post, w_in, g_out_sb, g_out_ch, rel_bias, w_out, g_ffn2_pre, g_ffn2_post, w_ffn2_gate, w_ffn2_up, w_ffn2_down, w_ple_proj, w_ple_gate, g_ple_post, loss_target, m_g_ffn1_pre, m_g_ffn1_post, m_w_ffn1_gate, m_w_ffn1_up, m_w_ffn1_down, m_g_mix_pre, m_g_mix_post, m_w_in, m_g_out_sb, m_g_out_ch, m_rel_bias, m_w_out, m_g_ffn2_pre, m_g_ffn2_post, m_w_ffn2_gate, m_w_ffn2_up, m_w_ffn2_down, m_w_ple_proj, m_w_ple_gate, m_g_ple_post, v_g_ffn1_pre, v_g_ffn1_post, v_w_ffn1_gate, v_w_ffn1_up, v_w_ffn1_down, v_g_mix_pre, v_g_mix_post, v_w_in, v_g_out_sb, v_g_out_ch, v_rel_bias, v_w_out, v_g_ffn2_pre, v_g_ffn2_post, v_w_ffn2_gate, v_w_ffn2_up, v_w_ffn2_down, v_w_ple_proj, v_w_ple_gate, v_g_ple_post):
    given = dict(locals())
    wnames = [n for n, *_ in _WEIGHTS]

    def shard(prefix, n):
        a = given[prefix + "w_" + n][0]
        return a.T if n in _TRANSPOSED else a

    packed, full = _pack_weights([shard("", n) for n in wnames])
    first = _GATHER_STAGES[0]
    anchor = x[0]
    two_level = (0, 2)
    gathers = {}

    def start_stage(stage, after):
        peers = _NEAR_PEERS if stage in two_level else _ALL_PEERS
        gathers[stage] = _gather_start(stage, _GATHER_STAGES[stage], packed, full, after,
                                       peers=peers)

    start_stage(0, anchor)

    relays = {}

    def first_level(stage, after):
        names = _GATHER_STAGES[stage]
        last_stage = stage + 1 == len(_GATHER_STAGES)
        count = len(_NEAR_PEERS) if stage in two_level else N_DEV - 1
        ws = _gather_wait(stage, names, gathers[stage], after, count=count)
        if not last_stage:
            start_stage(stage + 1, ws[names[0]])
        if stage in two_level:
            relays[stage] = _relay_start(stage, names, ws,
                                         anchor if last_stage else gathers[stage + 1][-1])
            return ws, relays[stage][-1]
        return ws, None if last_stage else gathers[stage + 1][-1]

    def weights_early(stage, after):
        return first_level(stage, after)[1][:1, :1]

    def weights_for(stage, after):
        names = _GATHER_STAGES[stage]
        ws, token = (None, None) if stage in relays else first_level(stage, after)
        if stage in relays:
            relay = relays[stage]
            ws = _gather_wait(f"{stage}r", names, relay, after, count=len(_FAR_CHIPS))
            token = None if stage + 1 == len(_GATHER_STAGES) else gathers[stage + 1][-1]
        return ws, jnp.zeros((1, 1), F32) if token is None else token[:1, :1]

    scatters = {}

    last = len(_SCATTER_STAGES) - 1

    def grads_done(stage, grads):
        names = _SCATTER_STAGES[stage]
        start = _pair_start if stage == last else _scatter_start
        scatters[stage] = start(stage, names, [grads[n] for n in names], anchor)
        return scatters[stage][-1][:1, :1]

    gains = {n: given["g_" + n] for n in _GAINS + ("out_sb", "out_ch")}
    fvec = _rel_bias_to_fvec(rel_bias[0])
    loss, dx, dg, dfvec = _local_step(x[0], p[0, 0], loss_target[0], gains,
                                      weights_for, grads_done, fvec, weights_early)

    loss_col = jnp.pad(loss[:, :1], ((0, N_DEV - 1), (0, D_MODEL - CH_WIN - 1)))
    dfv = jnp.concatenate([dfvec[:, 0, :], loss_col], axis=1)
    small_sent = _small_start(jnp.concatenate([_stack_gains(lambda n: dg[n]), dfv], axis=0), dx)

    results = {}

    def finish(stage, after):
        names = _SCATTER_STAGES[stage]
        dws, lands = _scatter_wait(stage, names, scatters[stage], after)
        for n in names:
            kind, _, pad, _ = _SPEC[n]
            out = _adamw_shard(shard("", n), shard("m_", n), shard("v_", n), lands[n], dws[n],
                               kind=kind, pad=pad, name="adamw_" + n)
            results["w_" + n] = [a.T for a in out] if n in _TRANSPOSED else out
        return results["w_" + names[-1]][0]

    names = _SCATTER_STAGES[last]
    whole = lambda w, ref: ref
    send, recv, src, land, _ = scatters[last]
    out = _split_wait(f"pair_wait{last}", len(names), whole, send, recv, src, land, dx,
                      keep_sources=True)
    sums = [_pair_sum(dwf, pair, pad=_SPEC[n][2], name="pair_sum_" + n)
            for n, dwf, pair in zip(names, out[:len(names)], out[len(names):])]
    send, recv, src, land, after = _chip_start(last, names, sums, anchor)
    for stage in range(last):
        after = finish(stage, after)
    out = _split_wait(f"chip_wait{last}", len(names), whole, send, recv, src, land, after,
                      keep_sources=True)
    for n, own, landed in zip(names, out[:len(names)], out[len(names):]):
        res = _adamw_chip(shard("", n), shard("m_", n), shard("v_", n), landed, own,
                          name="adamw_" + n)
        results["w_" + n] = [a.T for a in res] if n in _TRANSPOSED else res
        after = res[0]
    send, recv, src, land, _ = small_sent
    own, landed = _split_wait("small_wait", 1, whole, send, recv, src, land, after,
                              keep_sources=True)
    small = _sum_small(own, landed)
    gain_names = _GAINS + ("out_sb", "out_ch")
    gain_out = _adamw_gains(small, [(given["g_" + n], given["m_g_" + n], given["v_g_" + n])
                                    for n in gain_names])
    for r, n in enumerate(gain_names):
        results["g_" + n] = gain_out[4 * r:4 * r + 4]
    d_rel = _fvec_grad_to_rel_bias(small[N_DEV:, :CH_WIN].reshape(N_DEV, 1, CH_WIN))
    results["rel_bias"] = _adamw(rel_bias[0], m_rel_bias[0], v_rel_bias[0], d_rel,
                                 name="adamw_rel_bias")

    order = ("g_ffn1_pre", "g_ffn1_post", "w_ffn1_gate", "w_ffn1_up", "w_ffn1_down",
             "g_mix_pre", "g_mix_post", "w_in", "g_out_sb", "g_out_ch", "rel_bias", "w_out",
             "g_ffn2_pre", "g_ffn2_post", "w_ffn2_gate", "w_ffn2_up", "w_ffn2_down",
             "w_ple_proj", "w_ple_gate", "g_ple_post")

    def leaf(name, idx):
        a = results[name][idx]
        return a if name.startswith("g_") else a[None]

    total_loss = small[N_DEV, CH_WIN]
    return (total_loss, dx[None],
            *[leaf(n, 0) for n in order], *[leaf(n, 1) for n in order],
            *[leaf(n, 2) for n in order], *[leaf(n, 3) for n in order])
```

```python
import jax
import jax.numpy as jnp
from jax import lax
from jax.experimental import pallas as pl
from jax.experimental.pallas import tpu as pltpu

F32 = jnp.float32
BF16 = jnp.bfloat16

N_DEV = 8
D_MODEL = 1024
D_FF = 2816
FF_SHARD = D_FF // N_DEV
FF_SHARD_PAD = 384
D_FF_PAD = FF_SHARD_PAD * N_DEV
QKV_WIDTH = 3 * D_MODEL
QKV_SHARD = QKV_WIDTH // N_DEV
PLE_DIM = 256
ROW_SHARD = D_MODEL // N_DEV
HEAD_DIM = 64
PAIR = 2 * HEAD_DIM
N_PAIRS = 4
CHUNK = 64
LOOKBACK = 8
REL_CLIP = 128
N_REL = 2 * REL_CLIP + 1
CH_QB = 256
CH_LOOK = LOOKBACK * CHUNK
CH_WIN = CH_LOOK + CH_QB
SB_QB = 512
SB_KB = 256
SB_GROUP = 2
SB_LANES = tuple(slice(g * 128, (g + 1) * 128) for g in range(SB_GROUP))
EPS = 1e-6
NEG_INF = -1e30
ATT_SCALE = HEAD_DIM ** -0.5
ADAM_LR = 0.001
ADAM_B1 = 0.9
ADAM_B2 = 0.999
ADAM_EPS = 1e-08
ADAM_WD = 0.01
ADAM_STEP = 10
VMEM_LIMIT_BYTES = 48 * 1024 * 1024
MESH = pl.DeviceIdType.MESH

ANY = pl.BlockSpec(memory_space=pl.ANY)
VMEM = pl.BlockSpec(memory_space=pltpu.VMEM)


def _params(*sem):
    return pltpu.CompilerParams(dimension_semantics=sem or None,
                                vmem_limit_bytes=VMEM_LIMIT_BYTES)


def _sds(shape, dtype=F32):
    return jax.ShapeDtypeStruct(shape, dtype)


def _bf(x):
    return x.astype(BF16)


def _dot(a, b):
    return jnp.dot(_bf(a), _bf(b), preferred_element_type=F32)


def _dot_nt(a, b):
    return lax.dot_general(_bf(a), _bf(b), (((1,), (1,)), ((), ())),
                           preferred_element_type=F32)


def _dot_tn(a, b):
    return lax.dot_general(_bf(a), _bf(b), (((0,), (0,)), ((), ())),
                           preferred_element_type=F32)


def _sigmoid(x):
    return 1.0 / (1.0 + jnp.exp(-x))


def _softplus(x):
    return jnp.maximum(x, 0.0) + jnp.log(1.0 + jnp.exp(-jnp.abs(x)))


def _rstd(x):
    return lax.rsqrt(jnp.mean(x * x, axis=-1, keepdims=True) + EPS)


def _rms(x, g):
    return x * _rstd(x) * g


def _rms_bwd(dy, x, g):
    r = _rstd(x)
    w = dy * g
    dx = r * (w - x * (r * r) * jnp.mean(w * x, axis=-1, keepdims=True))
    dg = jnp.sum(dy * (x * r), axis=0, keepdims=True)
    return dx, dg


def _head_masks():
    lane = lax.broadcasted_iota(jnp.int32, (1, PAIR), 1)
    return lane < HEAD_DIM, lane >= HEAD_DIM


def _ffn_fwd(x, g_pre, g_post, wg, wu, wd, *, name):
    t = x.shape[0]
    tm, tj = 512, 1024
    ni, nj = t // tm, D_FF_PAD // tj

    def body(x_ref, gpre_ref, gpost_ref, wg_ref, wu_ref, wd_ref,
             h_ref, n_ref, a_ref, b_ref, f_ref, acc_ref):
        j = pl.program_id(1)

        @pl.when(j == 0)
        def _():
            n_ref[...] = _bf(_rms(x_ref[...], gpre_ref[...]))
            acc_ref[...] = jnp.zeros_like(acc_ref)

        n = n_ref[...]
        a = _dot_nt(n, wg_ref[...])
        b = _dot_nt(n, wu_ref[...])
        a_ref[...] = a
        b_ref[...] = b
        hmid = a * _sigmoid(a) * b
        acc_ref[...] += jnp.dot(_bf(hmid), wd_ref[...], preferred_element_type=F32)

        @pl.when(j == nj - 1)
        def _():
            f = acc_ref[...]
            f_ref[...] = f
            h_ref[...] = x_ref[...] + 0.5 * _rms(f, gpost_ref[...])

    row = pl.BlockSpec((tm, D_MODEL), lambda i, j: (i, 0))
    gain = pl.BlockSpec((1, D_MODEL), lambda i, j: (0, 0))
    col = pl.BlockSpec((tm, tj), lambda i, j: (i, j))
    wtile = pl.BlockSpec((tj, D_MODEL), lambda i, j: (j, 0))
    return pl.pallas_call(
        body, name=name, grid=(ni, nj),
        in_specs=[row, gain, gain, wtile, wtile, wtile],
        out_specs=[row, row, col, col, row],
        out_shape=[_sds((t, D_MODEL)), _sds((t, D_MODEL), BF16),
                   _sds((t, D_FF_PAD)), _sds((t, D_FF_PAD)), _sds((t, D_MODEL))],
        scratch_shapes=[pltpu.VMEM((tm, D_MODEL), F32)],
        compiler_params=_params("arbitrary", "arbitrary"),
    )(x, g_pre, g_post, wg, wu, wd)


def _ffn_bwd(n, df, a, b, wg, wu, wd, *, name):
    t = n.shape[0]
    tj, tm, ts = 256, t, 512
    nj, ni, ns = D_FF_PAD // tj, t // tm, tm // ts

    def body(n_hbm, df_hbm, a_ref, b_ref, wg_ref, wu_ref, wd_ref,
             dwg_ref, dwu_ref, dwd_ref, dn_hbm,
             n_v, df_v, dn_v, ag, au, ad, sem):
        j, i = pl.program_id(0), pl.program_id(1)

        @pl.when((j == 0) & (i == 0))
        def _():
            c1 = pltpu.make_async_copy(n_hbm, n_v, sem.at[0])
            c2 = pltpu.make_async_copy(df_hbm, df_v, sem.at[1])
            c1.start()
            c2.start()
            dn_v[...] = jnp.zeros_like(dn_v)
            c1.wait()
            c2.wait()

        @pl.when(i == 0)
        def _():
            ag[...] = jnp.zeros_like(ag)
            au[...] = jnp.zeros_like(au)
            ad[...] = jnp.zeros_like(ad)

        wgj, wuj, wdj = wg_ref[...], wu_ref[...], wd_ref[...]
        for s in range(ns):
            local = pl.ds(s * ts, ts)
            rows = pl.ds(pl.multiple_of(i * tm + s * ts, ts), ts)
            av, bv = a_ref[local, :], b_ref[local, :]
            sig = _sigmoid(av)
            silu = av * sig
            dfr = df_v[rows, :]
            nr = n_v[rows, :]
            dhmid = _dot_nt(dfr, wdj)
            da = dhmid * bv * (sig * (1.0 + av * (1.0 - sig)))
            db = dhmid * silu
            ad[...] += _dot_tn(silu * bv, dfr)
            ag[...] += _dot_tn(da, nr)
            au[...] += _dot_tn(db, nr)
            dn_v[rows, :] += _dot(da, wgj) + _dot(db, wuj)

        @pl.when(i == ni - 1)
        def _():
            dwg_ref[...] = _bf(ag[...])
            dwu_ref[...] = _bf(au[...])
            dwd_ref[...] = _bf(ad[...])

        @pl.when((j == nj - 1) & (i == ni - 1))
        def _():
            c = pltpu.make_async_copy(dn_v, dn_hbm, sem.at[0])
            c.start()
            c.wait()

    roww = pl.BlockSpec((tj, D_MODEL), lambda j, i: (j, 0))
    act = pl.BlockSpec((tm, tj), lambda j, i: (i, j))
    return pl.pallas_call(
        body, name=name, grid=(nj, ni),
        in_specs=[ANY, ANY, act, act, roww, roww, roww],
        out_specs=[roww, roww, roww, ANY],
        out_shape=[_sds((D_FF_PAD, D_MODEL), BF16)] * 3 + [_sds((t, D_MODEL))],
        scratch_shapes=[pltpu.VMEM((t, D_MODEL), BF16), pltpu.VMEM((t, D_MODEL), BF16),
                        pltpu.VMEM((t, D_MODEL), F32)]
        + [pltpu.VMEM((tj, D_MODEL), F32)] * 3 + [pltpu.SemaphoreType.DMA((2,))],
        compiler_params=_params("arbitrary", "arbitrary"),
    )(n, df, a, b, wg, wu, wd)


def _junction(dres, pre=None, post=None, *, name):
    t = dres.shape[0]
    tm = 512
    ni = t // tm
    n_in = 1 + (3 if pre else 0) + (2 if post else 0)
    coef = post[2] if post else None

    def body(*refs):
        ins, outs = list(refs[:n_in]), list(refs[n_in:])
        i = pl.program_id(0)
        dh = ins.pop(0)[...]
        if pre:
            dn_ref, x_ref, gpre_ref = ins.pop(0), ins.pop(0), ins.pop(0)
            dh_ref, dgpre_ref = outs.pop(0), outs.pop(0)
            dx, dg = _rms_bwd(dn_ref[...], x_ref[...], gpre_ref[...])
            dh = dh + dx
            dh_ref[...] = dh

            @pl.when(i == 0)
            def _():
                dgpre_ref[...] = jnp.zeros_like(dgpre_ref)
            dgpre_ref[...] += dg
        if post:
            f_ref, gpost_ref = ins.pop(0), ins.pop(0)
            df_ref, dgpost_ref = outs.pop(0), outs.pop(0)
            df, dg = _rms_bwd(coef * dh, f_ref[...], gpost_ref[...])
            df_ref[...] = _bf(df)

            @pl.when(i == 0)
            def _():
                dgpost_ref[...] = jnp.zeros_like(dgpost_ref)
            dgpost_ref[...] += dg

    row = pl.BlockSpec((tm, D_MODEL), lambda i: (i, 0))
    gain = pl.BlockSpec((1, D_MODEL), lambda i: (0, 0))
    args, in_specs, out_specs, out_shape = [dres], [row], [], []
    if pre:
        args += list(pre)
        in_specs += [row, row, gain]
        out_specs += [row, gain]
        out_shape += [_sds((t, D_MODEL)), _sds((1, D_MODEL))]
    if post:
        args += [post[0], post[1]]
        in_specs += [row, gain]
        out_specs += [row, gain]
        out_shape += [_sds((t, D_MODEL), BF16), _sds((1, D_MODEL))]
    return pl.pallas_call(
        body, name=name, grid=(ni,), in_specs=in_specs, out_specs=out_specs,
        out_shape=out_shape, compiler_params=_params("arbitrary"),
    )(*args)


def _qkv_fwd(h, g, win, *, name):
    t = h.shape[0]
    tm, tn = min(1024, t), 1024
    ni, nj = t // tm, QKV_WIDTH // tn

    def body(h_ref, g_ref, w_ref, qkv_ref, u_ref):
        @pl.when(pl.program_id(1) == 0)
        def _():
            u_ref[...] = _bf(_rms(h_ref[...], g_ref[...]))
        qkv_ref[...] = jnp.dot(u_ref[...], w_ref[...], preferred_element_type=F32)

    row = pl.BlockSpec((tm, D_MODEL), lambda i, j: (i, 0))
    return pl.pallas_call(
        body, name=name, grid=(ni, nj),
        in_specs=[row, pl.BlockSpec((1, D_MODEL), lambda i, j: (0, 0)),
                  pl.BlockSpec((D_MODEL, tn), lambda i, j: (0, j))],
        out_specs=[pl.BlockSpec((tm, tn), lambda i, j: (i, j)), row],
        out_shape=[_sds((t, QKV_WIDTH)), _sds((t, D_MODEL), BF16)],
        compiler_params=_params("arbitrary", "arbitrary"),
    )(h, g, win)


def _qkv_bwd(dq, dk, dv, u, win, *, name):
    t = u.shape[0]
    tn, ts = 512, 512
    nj, ns = QKV_WIDTH // tn, t // ts

    def body(dq_ref, dk_ref, dv_ref, u_ref, w_ref, dw_ref, du_hbm, du_v, acc_ref, sem):
        j = pl.program_id(0)

        @pl.when(j == 0)
        def _():
            du_v[...] = jnp.zeros_like(du_v)

        wj = w_ref[...]
        for role, d_ref in enumerate((dq_ref, dk_ref, dv_ref)):
            @pl.when(j % 3 == role)
            def _():
                acc_ref[...] = jnp.zeros_like(acc_ref)
                for s in range(ns):
                    rows = pl.ds(s * ts, ts)
                    dcol = d_ref[rows, :]
                    acc_ref[...] += _dot_tn(u_ref[rows, :], dcol)
                    du_v[rows, :] += _dot_nt(dcol, wj)
                dw_ref[...] = _bf(acc_ref[...])

        @pl.when(j == nj - 1)
        def _():
            c = pltpu.make_async_copy(du_v, du_hbm, sem)
            c.start()
            c.wait()

    colw = pl.BlockSpec((D_MODEL, tn), lambda j: (0, j))
    grp = pl.BlockSpec((t, tn), lambda j: (0, j // 3))
    return pl.pallas_call(
        body, name=name, grid=(nj,),
        in_specs=[grp, grp, grp, pl.BlockSpec((t, D_MODEL), lambda j: (0, 0)), colw],
        out_specs=[colw, ANY],
        out_shape=[_sds((D_MODEL, QKV_WIDTH), BF16), _sds((t, D_MODEL))],
        scratch_shapes=[pltpu.VMEM((t, D_MODEL), F32), pltpu.VMEM((D_MODEL, tn), F32),
                        pltpu.SemaphoreType.DMA],
        compiler_params=_params("arbitrary"),
    )(dq, dk, dv, u, win)


def _sb_stack(x):
    lo, hi = _head_masks()
    return jnp.concatenate([jnp.where(lo, x, 0.0), jnp.where(hi, x, 0.0)], axis=0)


def _sb_unstack(x2, blk):
    return jnp.where(_head_masks()[0], x2[:blk], x2[blk:])


def _sb_rows_from(x2, blk, r0):
    return x2 if r0 == 0 else jnp.concatenate([x2[r0:blk], x2[blk + r0:]], axis=0)


def _sb_rows_merge(full2, sub2, blk, r0):
    if r0 == 0:
        return sub2
    rows = blk - r0
    return jnp.concatenate([full2[:r0], sub2[:rows], full2[blk:blk + r0], sub2[rows:]], axis=0)


def _sb_mask(qb, kb, offset):
    r = lax.broadcasted_iota(jnp.int32, (2 * qb, kb), 0) & (qb - 1)
    c = lax.broadcasted_iota(jnp.int32, (2 * qb, kb), 1) + offset
    return c < r


def _tri(n, keep):
    r = lax.broadcasted_iota(jnp.int32, (n, n), 0)
    c = lax.broadcasted_iota(jnp.int32, (n, n), 1)
    return jnp.where(keep(r, c), 1.0, 0.0).astype(BF16)


def _cumsum01(x, u):
    m = x.shape[0]
    hi = _bf(x)
    lo = _bf(x - hi.astype(F32))
    both = jnp.dot(jnp.concatenate([hi, lo], axis=0), u, preferred_element_type=F32)
    return both[:m] + both[m:]


def _sb_fwd(qkv, *, name):
    t = qkv.shape[0]
    blk, kb = min(SB_QB, t), SB_KB
    ni, per = t // blk, blk // kb

    def body(q_ref, k_ref, v_ref, o_ref, ltot_ref):
        i = pl.program_id(1)
        u_after = _tri(kb, lambda r, c: r > c)
        q2 = [_bf(_sb_stack(q_ref[:, lanes] * ATT_SCALE)) for lanes in SB_LANES]

        def tile(g, k0, mask, acc, c_l):
            kj = k_ref[pl.ds(k0, kb), SB_LANES[g]]
            vj = v_ref[pl.ds(k0, kb), SB_LANES[g]]
            z = _dot_nt(q2[g], kj)
            sp = _softplus(z)
            lf = -sp if mask is None else jnp.where(mask, -sp, 0.0)
            a = jnp.exp(z - sp + _cumsum01(lf, u_after) + c_l)
            if mask is not None:
                a = jnp.where(mask, a, 0.0)
            return acc + _dot(a, vj), c_l + jnp.sum(lf, axis=1, keepdims=True)

        def tiles(k0, mask, carry):
            return tuple(tile(g, k0, mask, *carry[g]) for g in range(SB_GROUP))

        carry = ((jnp.zeros((2 * blk, PAIR), F32), jnp.zeros((2 * blk, 1), F32)),) * SB_GROUP
        for d in reversed(range(per)):
            carry = tiles(pl.multiple_of(i * blk + d * kb, kb), _sb_mask(blk, kb, d * kb), carry)
        carry = lax.fori_loop(
            1, per * i + 1,
            lambda jj, c: tiles(pl.multiple_of((per * i - jj) * kb, kb), None, c), carry)
        for g, (acc, c_l) in enumerate(carry):
            o_ref[:, SB_LANES[g]] = _sb_unstack(acc, blk)
            ltot_ref[:, SB_LANES[g]] = _sb_unstack(jnp.broadcast_to(c_l, (2 * blk, PAIR)), blk)

    width = SB_GROUP * PAIR
    blkspec = pl.BlockSpec((blk, width), lambda p, i: (i, p))
    n_steps = N_PAIRS // SB_GROUP
    return pl.pallas_call(
        body, name=name, grid=(n_steps, ni),
        in_specs=[blkspec,
                  pl.BlockSpec((t, width), lambda p, i: (0, n_steps + p)),
                  pl.BlockSpec((t, width), lambda p, i: (0, 2 * n_steps + p))],
        out_specs=[blkspec, blkspec],
        out_shape=[_sds((t, D_MODEL)), _sds((t, D_MODEL // 2))],
        compiler_params=_params("arbitrary", "arbitrary"),
    )(qkv, qkv, qkv)


def _sb_bwd(qkv, ltot, do, *, name):
    t = qkv.shape[0]
    blk, kb = min(SB_QB, t), SB_KB
    ni, per = t // blk, blk // kb

    def body(q_ref, k_ref, v_ref, lt_ref, do_ref, dq_ref, dkout_ref, dvout_ref, dk_ref, dv_ref):
        i = pl.program_id(1)

        @pl.when(i == 0)
        def _():
            dk_ref[...] = jnp.zeros_like(dk_ref)
            dv_ref[...] = jnp.zeros_like(dv_ref)

        u_upto = _tri(kb, lambda r, c: r <= c)
        u_before = _tri(kb, lambda r, c: r < c)
        lane = lax.broadcasted_iota(jnp.int32, (1, PAIR), 1)
        q2 = [_bf(_sb_stack(q_ref[:, lanes] * ATT_SCALE)) for lanes in SB_LANES]
        do2 = [_bf(_sb_stack(do_ref[:, lanes])) for lanes in SB_LANES]
        total = [jnp.concatenate(
            [jnp.sum(jnp.where(lane == h * HEAD_DIM, lt_ref[:, lanes], 0.0), axis=1, keepdims=True)
             for h in range(2)], axis=0) for lanes in SB_LANES]

        def tile(g, ops, k0, mask, dq_acc, c_l, c_g):
            qg, dog, tot = ops
            krows = pl.ds(k0, kb)
            kj = k_ref[krows, SB_LANES[g]]
            vj = v_ref[krows, SB_LANES[g]]
            z = _dot_nt(qg, kj)
            sp = _softplus(z)
            sig = jnp.exp(z - sp)
            lf = -sp if mask is None else jnp.where(mask, -sp, 0.0)
            a = jnp.exp(z - sp + tot - (_cumsum01(lf, u_upto) + c_l))
            if mask is not None:
                a = jnp.where(mask, a, 0.0)
            gw = a * _dot_nt(dog, vj)
            g_before = jnp.dot(_bf(gw), u_before, preferred_element_type=F32) + c_g
            dz = gw * (1.0 - sig) - g_before * sig
            if mask is not None:
                dz = jnp.where(mask, dz, 0.0)
            dk_ref[krows, SB_LANES[g]] += _dot_tn(dz, qg)
            dv_ref[krows, SB_LANES[g]] += _dot_tn(a, dog)
            return (dq_acc + _dot(dz, kj), c_l + jnp.sum(lf, axis=1, keepdims=True),
                    c_g + jnp.sum(gw, axis=1, keepdims=True))

        def tiles(ops, k0, mask, carry):
            return tuple(tile(g, ops[g], k0, mask, *carry[g]) for g in range(SB_GROUP))

        ops = tuple(zip(q2, do2, total))
        zero = (jnp.zeros((2 * blk, PAIR), F32), jnp.zeros((2 * blk, 1), F32),
                jnp.zeros((2 * blk, 1), F32))
        carry = lax.fori_loop(
            0, per * i, lambda j, c: tiles(ops, pl.multiple_of(j * kb, kb), None, c),
            (zero,) * SB_GROUP)
        for d in range(per):
            r0 = d * kb
            sub = tiles(tuple(tuple(_sb_rows_from(a, blk, r0) for a in o) for o in ops),
                        pl.multiple_of(i * blk + r0, kb), _sb_mask(blk - r0, kb, 0),
                        tuple(tuple(_sb_rows_from(a, blk, r0) for a in c) for c in carry))
            carry = tuple(tuple(_sb_rows_merge(a, s, blk, r0) for a, s in zip(c, cs))
                          for c, cs in zip(carry, sub))
        for g, (dq_acc, _, _) in enumerate(carry):
            dq_ref[:, SB_LANES[g]] = _bf(_sb_unstack(dq_acc, blk) * ATT_SCALE)

        @pl.when(i == ni - 1)
        def _():
            dkout_ref[...] = _bf(dk_ref[...])
            dvout_ref[...] = _bf(dv_ref[...])

    width = SB_GROUP * PAIR
    n_steps = N_PAIRS // SB_GROUP
    blkspec = lambda off: pl.BlockSpec((blk, width), lambda p, i: (i, off + p))
    full = lambda off: pl.BlockSpec((t, width), lambda p, i: (0, off + p))
    return pl.pallas_call(
        body, name=name, grid=(n_steps, ni),
        in_specs=[blkspec(0), full(n_steps), full(2 * n_steps), blkspec(0), blkspec(0)],
        out_specs=[blkspec(0), full(0), full(0)],
        out_shape=[_sds((t, D_MODEL), BF16)] * 3,
        scratch_shapes=[pltpu.VMEM((t, width), F32), pltpu.VMEM((t, width), F32)],
        compiler_params=_params("arbitrary", "arbitrary"),
    )(qkv, qkv, qkv, ltot, do)


def _ch_mask(i):
    r = lax.broadcasted_iota(jnp.int32, (CH_QB, CH_WIN), 0)
    c = lax.broadcasted_iota(jnp.int32, (CH_QB, CH_WIN), 1)
    qc = LOOKBACK + lax.shift_right_arithmetic(r, 6)
    kc = lax.shift_right_arithmetic(c, 6)
    first = i * (CH_QB // CHUNK) - LOOKBACK
    return (kc <= qc) & (kc >= qc - LOOKBACK) & (kc + first >= 0)


def _ch_probs(qm, kw, bias_h, mask):
    z = _dot_nt(qm, kw) * ATT_SCALE + bias_h
    z = jnp.where(mask, z, NEG_INF)
    e = jnp.exp(z - jnp.max(z, axis=1, keepdims=True))
    return e / jnp.sum(e, axis=1, keepdims=True)


def _ch_fill(pad_ref, src_ref, t):
    pad_ref[pl.ds(0, CH_LOOK), :] = jnp.zeros((CH_LOOK, PAIR), BF16)
    pad_ref[pl.ds(CH_LOOK, t), :] = _bf(src_ref[...])


def _ch_fwd(qkv, bias, o_in, *, name):
    t = qkv.shape[0]
    ni = t // CH_QB

    def body(q_ref, k_ref, v_ref, bias_ref, _alias, o_ref, kpad, vpad):
        i = pl.program_id(1)

        @pl.when(i == 0)
        def _():
            _ch_fill(kpad, k_ref, t)
            _ch_fill(vpad, v_ref, t)

        win = pl.ds(pl.multiple_of(i * CH_QB, CH_QB), CH_WIN)
        kw, vw = kpad[win, :], vpad[win, :]
        mask = _ch_mask(i)
        q = q_ref[...]
        outs = []
        for h, hm in enumerate(_head_masks()):
            p = _ch_probs(jnp.where(hm, q, 0.0), kw, bias_ref[h], mask)
            outs.append(_dot(p, vw))
        o_ref[...] = jnp.where(_head_masks()[0], outs[0], outs[1])

    full = lambda off: pl.BlockSpec((t, PAIR), lambda p, i: (0, off + p))
    return pl.pallas_call(
        body, name=name, grid=(N_PAIRS, ni),
        in_specs=[pl.BlockSpec((CH_QB, PAIR), lambda p, i: (i, 3 * N_PAIRS + p)),
                  full(4 * N_PAIRS), full(5 * N_PAIRS),
                  pl.BlockSpec((2, CH_QB, CH_WIN), lambda p, i: (p, 0, 0)), ANY],
        out_specs=pl.BlockSpec((CH_QB, PAIR), lambda p, i: (i, N_PAIRS + p)),
        out_shape=_sds((t, D_MODEL)),
        scratch_shapes=[pltpu.VMEM((t + CH_LOOK, PAIR), BF16)] * 2,
        input_output_aliases={4: 0},
        compiler_params=_params("arbitrary", "arbitrary"),
    )(qkv, qkv, qkv, bias, o_in)


def _ch_bwd(qkv, bias, o, do, dq_in, dk_in, dv_in, *, name):
    t = qkv.shape[0]
    ni = t // CH_QB

    def body(q_ref, k_ref, v_ref, bias_ref, o_ref, do_ref, _a0, _a1, _a2,
             dq_ref, dkout_ref, dvout_ref, dbias_ref, kpad, vpad, dkpad, dvpad):
        i = pl.program_id(1)

        @pl.when(i == 0)
        def _():
            _ch_fill(kpad, k_ref, t)
            _ch_fill(vpad, v_ref, t)
            dkpad[...] = jnp.zeros_like(dkpad)
            dvpad[...] = jnp.zeros_like(dvpad)
            dbias_ref[...] = jnp.zeros_like(dbias_ref)

        win = pl.ds(pl.multiple_of(i * CH_QB, CH_QB), CH_WIN)
        kw, vw = kpad[win, :], vpad[win, :]
        mask = _ch_mask(i)
        q, o_blk, do_blk = q_ref[...], o_ref[...], do_ref[...]
        dqs = []
        for h, hm in enumerate(_head_masks()):
            qm = _bf(jnp.where(hm, q, 0.0))
            dom = jnp.where(hm, do_blk, 0.0)
            delta = jnp.sum(dom * o_blk, axis=1, keepdims=True)
            dom = _bf(dom)
            p = _ch_probs(qm, kw, bias_ref[h], mask)
            ds = p * (_dot_nt(dom, vw) - delta)
            dbias_ref[h] += ds
            dsz = ds * ATT_SCALE
            dqs.append(_dot(dsz, kw))
            dkpad[win, :] += _dot_tn(dsz, qm)
            dvpad[win, :] += _dot_tn(p, dom)
        dq_ref[...] = _bf(jnp.where(_head_masks()[0], dqs[0], dqs[1]))

        @pl.when(i == ni - 1)
        def _():
            dkout_ref[...] = _bf(dkpad[pl.ds(CH_LOOK, t), :])
            dvout_ref[...] = _bf(dvpad[pl.ds(CH_LOOK, t), :])

    blkspec = lambda off: pl.BlockSpec((CH_QB, PAIR), lambda p, i: (i, off + p))
    full = lambda off: pl.BlockSpec((t, PAIR), lambda p, i: (0, off + p))
    bias_spec = pl.BlockSpec((2, CH_QB, CH_WIN), lambda p, i: (p, 0, 0))
    return pl.pallas_call(
        body, name=name, grid=(N_PAIRS, ni),
        in_specs=[blkspec(3 * N_PAIRS), full(4 * N_PAIRS), full(5 * N_PAIRS), bias_spec,
                  blkspec(N_PAIRS), blkspec(N_PAIRS), ANY, ANY, ANY],
        out_specs=[blkspec(N_PAIRS), full(N_PAIRS), full(N_PAIRS), bias_spec],
        out_shape=[_sds((t, D_MODEL), BF16)] * 3 + [_sds((2 * N_PAIRS, CH_QB, CH_WIN))],
        scratch_shapes=[pltpu.VMEM((t + CH_LOOK, PAIR), BF16)] * 2
        + [pltpu.VMEM((t + CH_LOOK, PAIR), F32)] * 2,
        input_output_aliases={6: 0, 7: 1, 8: 2},
        compiler_params=_params("arbitrary", "arbitrary"),
    )(qkv, qkv, qkv, bias, o, do, dq_in, dk_in, dv_in)


def _bias_expand(fvec, *, name):
    n_heads = fvec.shape[0]

    def body(f_ref, o_ref, rows8):
        row = f_ref[0]
        for r in range(8):
            rows8[pl.ds(r, 1), :] = pltpu.roll(row, r, 1)
        base = rows8[...]
        for blk in range(CH_QB // 8):
            o_ref[0, pl.ds(8 * blk, 8), :] = pltpu.roll(base, 8 * blk, 1)

    return pl.pallas_call(
        body, name=name, grid=(n_heads,),
        in_specs=[pl.BlockSpec((1, 1, CH_WIN), lambda h: (h, 0, 0))],
        out_specs=pl.BlockSpec((1, CH_QB, CH_WIN), lambda h: (h, 0, 0)),
        out_shape=_sds((n_heads, CH_QB, CH_WIN)),
        scratch_shapes=[pltpu.VMEM((8, CH_WIN), F32)],
        compiler_params=_params("arbitrary"),
    )(fvec)


def _bias_grad(dbias, after, *, name):
    n_heads = dbias.shape[0]
    first = CH_LOOK - REL_CLIP

    def body(d_ref, _after, o_ref, acc8):
        acc = jnp.zeros((8, CH_WIN), F32)
        for blk in range(CH_QB // 8):
            acc = acc + pltpu.roll(d_ref[0, pl.ds(8 * blk, 8), :], (CH_WIN - 8 * blk) % CH_WIN, 1)
        acc8[...] = acc
        dvec = jnp.zeros((1, CH_WIN), F32)
        for r in range(8):
            dvec = dvec + pltpu.roll(acc8[pl.ds(r, 1), :], (CH_WIN - r) % CH_WIN, 1)
        lane = lax.broadcasted_iota(jnp.int32, (1, CH_WIN), 1)
        clipped = (lane <= first) | (lane >= first + REL_CLIP + CHUNK)
        total = jnp.sum(jnp.where(clipped, dvec, 0.0), axis=1, keepdims=True)
        o_ref[0] = jnp.where(lane == first, total, dvec)

    return pl.pallas_call(
        body, name=name, grid=(n_heads,),
        in_specs=[pl.BlockSpec((1, CH_QB, CH_WIN), lambda h: (h, 0, 0)), ANY],
        out_specs=pl.BlockSpec((1, 1, CH_WIN), lambda h: (h, 0, 0)),
        out_shape=_sds((n_heads, 1, CH_WIN)),
        scratch_shapes=[pltpu.VMEM((8, CH_WIN), F32)],
        compiler_params=_params("arbitrary"),
    )(dbias, after)


def _out_fwd(o, h1, g_sb, g_ch, g_post, wout, *, name):
    t = o.shape[0]
    tm = 512
    half = D_MODEL // 2

    def body(o_ref, h_ref, gsb_ref, gch_ref, gpost_ref, w_ref, h2_ref, mixed_ref, y_ref):
        ov = o_ref[...]
        mixed = jnp.concatenate([_rms(ov[:, :half], gsb_ref[...]),
                                 _rms(ov[:, half:], gch_ref[...])], axis=1)
        mixed_ref[...] = _bf(mixed)
        y = _dot(mixed, w_ref[...])
        y_ref[...] = y
        h2_ref[...] = h_ref[...] + _rms(y, gpost_ref[...])

    row = pl.BlockSpec((tm, D_MODEL), lambda i: (i, 0))
    gain = lambda n: pl.BlockSpec((1, n), lambda i: (0, 0))
    return pl.pallas_call(
        body, name=name, grid=(t // tm,),
        in_specs=[row, row, gain(half), gain(half), gain(D_MODEL),
                  pl.BlockSpec((D_MODEL, D_MODEL), lambda i: (0, 0))],
        out_specs=[row, row, row],
        out_shape=[_sds((t, D_MODEL)), _sds((t, D_MODEL), BF16), _sds((t, D_MODEL))],
        compiler_params=_params("arbitrary"),
    )(o, h1, g_sb, g_ch, g_post, wout)


def _out_bwd(dy, mixed, o, g_sb, g_ch, wout, *, name):
    t = o.shape[0]
    tm = 512
    ni = t // tm
    half = D_MODEL // 2

    def body(dy_ref, mixed_ref, o_ref, gsb_ref, gch_ref, w_ref,
             dw_ref, do_ref, dgsb_ref, dgch_ref, acc_ref):
        i = pl.program_id(0)

        @pl.when(i == 0)
        def _():
            acc_ref[...] = jnp.zeros_like(acc_ref)
            dgsb_ref[...] = jnp.zeros_like(dgsb_ref)
            dgch_ref[...] = jnp.zeros_like(dgch_ref)

        dyv = dy_ref[...]
        acc_ref[...] += _dot_tn(mixed_ref[...], dyv)
        dm = _dot_nt(dyv, w_ref[...])
        ov = o_ref[...]
        doa, dga = _rms_bwd(dm[:, :half], ov[:, :half], gsb_ref[...])
        dob, dgb = _rms_bwd(dm[:, half:], ov[:, half:], gch_ref[...])
        do_ref[...] = jnp.concatenate([doa, dob], axis=1)
        dgsb_ref[...] += dga
        dgch_ref[...] += dgb

        @pl.when(i == ni - 1)
        def _():
            dw_ref[...] = _bf(acc_ref[...])

    row = pl.BlockSpec((tm, D_MODEL), lambda i: (i, 0))
    gain = pl.BlockSpec((1, half), lambda i: (0, 0))
    sq = pl.BlockSpec((D_MODEL, D_MODEL), lambda i: (0, 0))
    return pl.pallas_call(
        body, name=name, grid=(ni,),
        in_specs=[row, row, row, gain, gain, sq],
        out_specs=[sq, row, gain, gain],
        out_shape=[_sds((D_MODEL, D_MODEL), BF16), _sds((t, D_MODEL)),
                   _sds((1, half)), _sds((1, half))],
        scratch_shapes=[pltpu.VMEM((D_MODEL, D_MODEL), F32)],
        compiler_params=_params("arbitrary"),
    )(dy, mixed, o, g_sb, g_ch, wout)


def _ple(p, h3, target, wp, wgate, g, f_post, g_post, *, name):
    t = h3.shape[0]
    tm = 512
    ni = t // tm

    def body(p_ref, h_ref, tgt_ref, wp_ref, wg_ref, g_ref, f_ref, gf_ref,
             loss_ref, dres_ref, dwp_ref, dwg_ref, dg_ref, df_ref, dgf_ref, accp, accg):
        i = pl.program_id(0)

        @pl.when(i == 0)
        def _():
            loss_ref[...] = jnp.zeros_like(loss_ref)
            dg_ref[...] = jnp.zeros_like(dg_ref)
            dgf_ref[...] = jnp.zeros_like(dgf_ref)
            accp[...] = jnp.zeros_like(accp)
            accg[...] = jnp.zeros_like(accg)

        pv, hv, gv = p_ref[...], h_ref[...], g_ref[...]
        pe = _dot(pv, wp_ref[...])
        sig = _sigmoid(_dot(hv, wg_ref[...]))
        e = pe * sig
        err = hv + _rms(e, gv) - tgt_ref[...]
        tok = jnp.mean(err * err, axis=-1, keepdims=True)
        loss_ref[...] += 0.5 * jnp.sum(tok, axis=0, keepdims=True)
        dh4 = err * (1.0 / D_MODEL)
        de, dg = _rms_bwd(dh4, e, gv)
        dg_ref[...] += dg
        dpe = de * sig
        dgt = de * pe * sig * (1.0 - sig)
        accp[...] += _dot_tn(pv, dpe)
        accg[...] += _dot_tn(hv, dgt)
        dres = dh4 + _dot_nt(dgt, wg_ref[...])
        dres_ref[...] = dres
        df, dgf = _rms_bwd(0.5 * dres, f_ref[...], gf_ref[...])
        df_ref[...] = _bf(df)
        dgf_ref[...] += dgf

        @pl.when(i == ni - 1)
        def _():
            dwp_ref[...] = _bf(accp[...])
            dwg_ref[...] = _bf(accg[...])

    row = pl.BlockSpec((tm, D_MODEL), lambda i: (i, 0))
    const = lambda r, c: pl.BlockSpec((r, c), lambda i: (0, 0))
    return pl.pallas_call(
        body, name=name, grid=(ni,),
        in_specs=[pl.BlockSpec((tm, PLE_DIM), lambda i: (i, 0)), row, row,
                  const(PLE_DIM, D_MODEL), const(D_MODEL, D_MODEL), const(1, D_MODEL),
                  row, const(1, D_MODEL)],
        out_specs=[const(1, 128), row, const(PLE_DIM, D_MODEL), const(D_MODEL, D_MODEL),
                   const(1, D_MODEL), row, const(1, D_MODEL)],
        out_shape=[_sds((1, 128)), _sds((t, D_MODEL)), _sds((PLE_DIM, D_MODEL), BF16),
                   _sds((D_MODEL, D_MODEL), BF16), _sds((1, D_MODEL)),
                   _sds((t, D_MODEL), BF16), _sds((1, D_MODEL))],
        scratch_shapes=[pltpu.VMEM((PLE_DIM, D_MODEL), F32), pltpu.VMEM((D_MODEL, D_MODEL), F32)],
        compiler_params=_params("arbitrary"),
    )(p, h3, target, wp, wgate, g, f_post, g_post)


def _rel_bias_to_fvec(rel_bias):
    rev = rel_bias[:, ::-1]
    n_heads = rel_bias.shape[0]
    first = CH_LOOK - REL_CLIP
    n_var = REL_CLIP + CHUNK
    clipped = rev[:, :1]
    fvec = jnp.concatenate([jnp.broadcast_to(clipped, (n_heads, first)), rev[:, :n_var],
                            jnp.broadcast_to(clipped, (n_heads, CH_WIN - first - n_var))], axis=1)
    return fvec.reshape(n_heads, 1, CH_WIN)


def _fvec_grad_to_rel_bias(dfvec):
    first = CH_LOOK - REL_CLIP
    n_var = REL_CLIP + CHUNK
    rev = jnp.pad(dfvec[:, 0, first:first + n_var], ((0, 0), (0, N_REL - n_var)))
    return rev[:, ::-1]


def _local_step(x, p, target, g, weights_for, grads_done, fvec, weights_early=None):
    bias = _bias_expand(fvec, name="bias_expand")
    w, tie = weights_for(0, bias)
    w = dict(w)
    h1, n1, a1, b1, f1 = _ffn_fwd(x, g["ffn1_pre"] + tie, g["ffn1_post"],
                                  w["ffn1_gate"], w["ffn1_up"], w["ffn1_down"], name="ffn1_fwd")
    more, tie = weights_for(1, h1)
    w.update(more)
    qkv, u = _qkv_fwd(h1, g["mix_pre"] + tie, w["in"], name="qkv_fwd")
    o, ltot = _sb_fwd(qkv, name="sb_fwd")
    tie = weights_early(2, ltot) if weights_early else 0.0
    o = _ch_fwd(qkv, bias, o, name="ch_fwd")
    h2, mixed, y = _out_fwd(o, h1, g["out_sb"] + tie, g["out_ch"], g["mix_post"], w["out"],
                            name="out_fwd")
    w.update(weights_for(2, h2)[0])
    h3, n2, a2, b2, f2 = _ffn_fwd(h2, g["ffn2_pre"], g["ffn2_post"],
                                  w["ffn2_gate"], w["ffn2_up"], w["ffn2_down"], name="ffn2_fwd")
    loss, dh3, dwp, dwgate, dg_ple, df2, dg_ffn2_post = _ple(
        p, h3, target, w["ple_proj"], w["ple_gate"], g["ple_post"], f2, g["ffn2_post"], name="ple")
    grads_done(0, {"ple_proj": dwp, "ple_gate": dwgate})
    dwg2, dwu2, dwd2, dn2 = _ffn_bwd(n2, df2, a2, b2, w["ffn2_gate"], w["ffn2_up"],
                                     w["ffn2_down"], name="ffn2_bwd")
    tie = grads_done(1, {"ffn2_gate": dwg2, "ffn2_up": dwu2, "ffn2_down": dwd2})
    dh2, dg_ffn2_pre, dy, dg_mix_post = _junction(
        dh3, pre=(dn2, h2, g["ffn2_pre"] + tie), post=(y, g["mix_post"], 1.0), name="junction2")
    dwout, do, dg_sb, dg_ch = _out_bwd(dy, mixed, o, g["out_sb"], g["out_ch"], w["out"],
                                       name="out_bwd")
    dq, dk, dv = _sb_bwd(qkv, ltot, do, name="sb_bwd")
    dq, dk, dv, dbias = _ch_bwd(qkv, bias, o, do, dq, dk, dv, name="ch_bwd")
    dwin, du = _qkv_bwd(dq, dk, dv, u, w["in"], name="qkv_bwd")
    tie = grads_done(2, {"out": dwout, "in": dwin})
    dh1, dg_mix_pre, df1, dg_ffn1_post = _junction(
        dh2, pre=(du, h1, g["mix_pre"] + tie), post=(f1, g["ffn1_post"], 0.5), name="junction1")
    dwg1, dwu1, dwd1, dn1 = _ffn_bwd(n1, df1, a1, b1, w["ffn1_gate"], w["ffn1_up"],
                                     w["ffn1_down"], name="ffn1_bwd")
    tie = grads_done(3, {"ffn1_gate": dwg1, "ffn1_up": dwu1, "ffn1_down": dwd1})
    dx, dg_ffn1_pre = _junction(dh1, pre=(dn1, x, g["ffn1_pre"] + tie), name="junction0")

    dg = {"ffn1_pre": dg_ffn1_pre, "ffn1_post": dg_ffn1_post, "mix_pre": dg_mix_pre,
          "mix_post": dg_mix_post, "out_sb": dg_sb, "out_ch": dg_ch,
          "ffn2_pre": dg_ffn2_pre, "ffn2_post": dg_ffn2_post, "ple_post": dg_ple}
    return loss, dx, dg, dbias


_WEIGHTS = (
    ("ffn1_gate", "row", FF_SHARD, FF_SHARD_PAD, D_MODEL),
    ("ffn1_up", "row", FF_SHARD, FF_SHARD_PAD, D_MODEL),
    ("ffn1_down", "row", FF_SHARD, FF_SHARD_PAD, D_MODEL),
    ("in", "col", QKV_SHARD, QKV_SHARD, D_MODEL),
    ("out", "row", ROW_SHARD, ROW_SHARD, D_MODEL),
    ("ffn2_gate", "row", FF_SHARD, FF_SHARD_PAD, D_MODEL),
    ("ffn2_up", "row", FF_SHARD, FF_SHARD_PAD, D_MODEL),
    ("ffn2_down", "row", FF_SHARD, FF_SHARD_PAD, D_MODEL),
    ("ple_proj", "col", ROW_SHARD, ROW_SHARD, PLE_DIM),
    ("ple_gate", "row", ROW_SHARD, ROW_SHARD, D_MODEL),
)
_TRANSPOSED = ("ffn1_gate", "ffn1_up", "ffn2_gate", "ffn2_up")
_SPEC = {n: (kind, valid, pad, other) for n, kind, valid, pad, other in _WEIGHTS}
_GATHER_STAGES = (("ffn1_gate", "ffn1_up", "ffn1_down"), ("in", "out"),
                  ("ffn2_gate", "ffn2_up", "ffn2_down", "ple_proj", "ple_gate"))
_SCATTER_STAGES = (("ple_proj", "ple_gate"), ("ffn2_gate", "ffn2_up", "ffn2_down"),
                   ("out", "in"), ("ffn1_gate", "ffn1_up", "ffn1_down"))
HBM = pl.BlockSpec(memory_space=pltpu.HBM)
SEM = pl.BlockSpec(memory_space=pltpu.SEMAPHORE)
EFFECT = pltpu.SideEffectType.DATAFLOW_SIDE_EFFECTING


def _shard_shape(kind, size, other):
    return (other, size) if kind == "col" else (size, other)


def _window(ref, kind, start, size):
    return ref.at[:, pl.ds(start, size)] if kind == "col" else ref.at[pl.ds(start, size), :]


def _device_tuple(k):
    return (k // 4, (k // 2) % 2, k % 2)


def _my_index():
    return 4 * lax.axis_index("x") + 2 * lax.axis_index("y") + lax.axis_index("c")


def _pack_weights(shards):
    nw = len(_WEIGHTS)

    def body(*refs):
        ins, packed, full = refs[:nw], refs[nw:2 * nw], refs[2 * nw:3 * nw]
        sem = refs[3 * nw]
        me = _my_index()
        for (_, kind, valid, pad, _), src, dst in zip(_WEIGHTS, ins, packed):
            if pad != valid:
                dst[...] = jnp.zeros_like(dst)
            if kind == "col":
                dst[:, pl.ds(0, valid)] = _bf(src[...])
            else:
                dst[pl.ds(0, valid), :] = _bf(src[...])
        for k in range(N_DEV):
            @pl.when(me == k)
            def _():
                for w, (_, kind, _, pad, _) in enumerate(_WEIGHTS):
                    pltpu.make_async_copy(packed[w], _window(full[w], kind, k * pad, pad),
                                          sem.at[w]).start()
        for w, (_, kind, _, pad, _) in enumerate(_WEIGHTS):
            pltpu.make_async_copy(packed[w], _window(full[w], kind, 0, pad), sem.at[w]).wait()

    whole = lambda shape: pl.BlockSpec(shape, lambda i: (0, 0))
    packed_shapes = [_shard_shape(kind, pad, other) for _, kind, _, pad, other in _WEIGHTS]
    outs = pl.pallas_call(
        body, name="pack_weights", grid=(1,),
        in_specs=[whole(a.shape) for a in shards],
        out_specs=[whole(s) for s in packed_shapes] + [ANY] * nw,
        out_shape=[_sds(s, BF16) for s in packed_shapes]
        + [_sds(_shard_shape(kind, N_DEV * pad, other), BF16) for _, kind, _, pad, other in _WEIGHTS],
        scratch_shapes=[pltpu.SemaphoreType.DMA((nw,))],
        compiler_params=_params("arbitrary"),
    )(*shards)
    names = [n for n, *_ in _WEIGHTS]
    return dict(zip(names, outs[:nw])), dict(zip(names, outs[nw:]))


def _hbm(a):
    return pltpu.with_memory_space_constraint(a, pltpu.HBM)


def _split_start(name, n, body_copies, sources, lands, after):
    arrays = list(sources) + list(lands)
    ns, na = len(sources), len(arrays)

    def body(*refs):
        src, land = refs[:ns], refs[ns:na]
        send, recv = refs[na + 1], refs[na + 2]
        token = refs[-1]
        body_copies(src, land, send, recv)
        token[...] = jnp.zeros_like(token)

    out = pl.pallas_call(
        body, name=name,
        out_shape=(pltpu.SemaphoreType.DMA((n,)), pltpu.SemaphoreType.DMA((n,)),
                   *[pltpu.HBM(a.shape, a.dtype) for a in arrays], _sds((8, 128))),
        in_specs=[HBM] * na + [ANY], out_specs=(SEM, SEM, *[HBM] * na, VMEM),
        input_output_aliases={i: 2 + i for i in range(na)},
        compiler_params=pltpu.CompilerParams(has_side_effects=EFFECT),
    )(*[_hbm(a) for a in arrays], after)
    return out[0], out[1], out[2:2 + ns], out[2 + ns:2 + na], out[-1]


def _split_wait(name, n, seven_of, send, recv, sources, lands, after, keep_sources=False):
    arrays = list(sources) + list(lands)
    ns, na = len(sources), len(arrays)

    def body(*refs):
        land = refs[ns:na]
        send_ref, recv_ref = refs[na], refs[na + 1]
        myself = (lax.axis_index("x"), lax.axis_index("y"), lax.axis_index("c"))
        for w in range(n):
            seven = seven_of(w, land[w])
            copy = pltpu.make_async_remote_copy(
                src_ref=seven, dst_ref=seven, send_sem=send_ref.at[w], recv_sem=recv_ref.at[w],
                device_id=myself, device_id_type=MESH)
            copy.wait_send()
            copy.wait_recv()

    out = pl.pallas_call(
        body, name=name,
        out_shape=[pltpu.HBM(a.shape, a.dtype) for a in arrays],
        in_specs=[HBM] * na + [SEM, SEM, ANY], out_specs=[HBM] * na,
        input_output_aliases={i: i for i in range(na)},
        compiler_params=pltpu.CompilerParams(has_side_effects=EFFECT),
    )(*arrays, send, recv, after)
    return out if keep_sources else out[ns:]


_ALL_PEERS = (1, 2, 3, 4, 5, 6, 7)
_NEAR_PEERS = (1, 2, 4, 6)
_FAR_CHIPS = (2, 4, 6)


def _gather_start(stage, names, packed, full, after, peers=_ALL_PEERS):
    def copies(src, land, send, recv):
        me = _my_index()
        for k in range(N_DEV):
            @pl.when(me == k)
            def _():
                for w, name in enumerate(names):
                    kind, _, pad, _ = _SPEC[name]
                    dst = _window(land[w], kind, k * pad, pad)
                    for mask in peers:
                        pltpu.make_async_remote_copy(
                            src_ref=src[w], dst_ref=dst, send_sem=send.at[w],
                            recv_sem=recv.at[w], device_id=_device_tuple(k ^ mask),
                            device_id_type=MESH).start()

    return _split_start(f"gather_start{stage}", len(names), copies,
                        [packed[n] for n in names], [full[n] for n in names], after)


def _gather_wait(stage, names, started, after, count=N_DEV - 1):
    send, recv, src, land, _ = started

    def bytes_of(w, ref):
        kind, _, pad, _ = _SPEC[names[w]]
        return _window(ref, kind, 0, count * pad)

    return dict(zip(names, _split_wait(f"gather_wait{stage}", len(names), bytes_of,
                                       send, recv, src, land, after)))


def _relay_start(stage, names, full, after):
    def copies(_, land, send, recv):
        me = _my_index()
        for k in range(N_DEV):
            @pl.when(me == k)
            def _():
                for w, name in enumerate(names):
                    kind, _, pad, _ = _SPEC[name]
                    for mask in _FAR_CHIPS:
                        win = _window(land[w], kind, (k ^ mask) * pad, pad)
                        pltpu.make_async_remote_copy(
                            src_ref=win, dst_ref=win, send_sem=send.at[w], recv_sem=recv.at[w],
                            device_id=_device_tuple(k ^ 1), device_id_type=MESH).start()

    return _split_start(f"relay_start{stage}", len(names), copies, [],
                        [full[n] for n in names], after)


def _scatter_start(stage, names, grads, after):
    def copies(src, land, send, recv):
        me = _my_index()
        for k in range(N_DEV):
            @pl.when(me != k)
            def _():
                slot = lax.rem(me + (N_DEV - 1 - k), N_DEV)
                for w, name in enumerate(names):
                    kind, _, pad, _ = _SPEC[name]
                    pltpu.make_async_remote_copy(
                        src_ref=_window(src[w], kind, k * pad, pad), dst_ref=land[w].at[slot],
                        send_sem=send.at[w], recv_sem=recv.at[w],
                        device_id=_device_tuple(k), device_id_type=MESH).start()

    lands = [lax.empty((N_DEV - 1,) + _shard_shape(_SPEC[m][0], _SPEC[m][2], _SPEC[m][3]), BF16)
             for m in names]
    return _split_start(f"scatter_start{stage}", len(names), copies, grads, lands, after)


def _scatter_wait(stage, names, started, after):
    send, recv, src, land, _ = started
    n = len(names)
    out = _split_wait(f"scatter_wait{stage}", n, lambda w, ref: ref, send, recv, src, land, after,
                      keep_sources=True)
    return dict(zip(names, out[:n])), dict(zip(names, out[n:]))


N_CHIPS = N_DEV // 2


def _pair_start(stage, names, grads, after):
    def copies(src, land, send, recv):
        me = _my_index()
        for k in range(N_DEV):
            @pl.when(me == k)
            def _():
                for w, name in enumerate(names):
                    kind, _, pad, _ = _SPEC[name]
                    for chip in range(N_CHIPS):
                        j = 2 * chip + ((k ^ 1) & 1)
                        pltpu.make_async_remote_copy(
                            src_ref=_window(src[w], kind, j * pad, pad), dst_ref=land[w].at[chip],
                            send_sem=send.at[w], recv_sem=recv.at[w],
                            device_id=_device_tuple(k ^ 1), device_id_type=MESH).start()

    lands = [lax.empty((N_CHIPS,) + _shard_shape(_SPEC[m][0], _SPEC[m][2], _SPEC[m][3]), BF16)
             for m in names]
    return _split_start(f"pair_start{stage}", len(names), copies, grads, lands, after)


def _pair_sum(dw_full, pair, *, pad, name):
    other = dw_full.shape[1]

    def body(own_ref, pair_ref, out_ref):
        out_ref[0] = _bf(own_ref[...].astype(F32) + pair_ref[0].astype(F32))

    slot = pl.BlockSpec((1, pad, other), lambda q: (q, 0, 0))
    return pl.pallas_call(
        body, name=name, grid=(N_CHIPS,),
        in_specs=[pl.BlockSpec((pad, other), lambda q: (2 * q + lax.axis_index("c"), 0)), slot],
        out_specs=slot, out_shape=_sds((N_CHIPS, pad, other), BF16),
        compiler_params=_params("arbitrary"),
    )(dw_full, pair)


def _chip_start(stage, names, sums, after):
    def copies(src, land, send, recv):
        me = _my_index()
        my_chip = lax.shift_right_logical(me, 1)
        for k in range(N_DEV):
            @pl.when((me != k) & (((me ^ k) & 1) == 0))
            def _():
                slot = lax.rem(my_chip + (N_CHIPS - 1 - k // 2), N_CHIPS)
                for w in range(len(names)):
                    pltpu.make_async_remote_copy(
                        src_ref=src[w].at[k // 2], dst_ref=land[w].at[slot],
                        send_sem=send.at[w], recv_sem=recv.at[w],
                        device_id=_device_tuple(k), device_id_type=MESH).start()

    lands = [lax.empty((N_CHIPS - 1,) + a.shape[1:], BF16) for a in sums]
    return _split_start(f"chip_start{stage}", len(names), copies, sums, lands, after)


def _adamw_chip(w, m, v, land, sums, *, name):
    shape = w.shape

    def body(w_ref, m_ref, v_ref, land_ref, own_ref, *outs):
        rows = pl.ds(0, shape[0])
        grad = own_ref[0, rows, :].astype(F32)
        for s in range(N_CHIPS - 1):
            grad = grad + land_ref[s, rows, :].astype(F32)
        _adam_update(w_ref, m_ref, v_ref, grad, *outs)

    whole = lambda a: pl.BlockSpec(a.shape, lambda i: (0,) * a.ndim)
    own = pl.BlockSpec((1,) + sums.shape[1:],
                       lambda i: (2 * lax.axis_index("x") + lax.axis_index("y"), 0, 0))
    return pl.pallas_call(
        body, name=name, grid=(1,),
        in_specs=[whole(w), whole(m), whole(v), whole(land), own],
        out_specs=[whole(w)] * 4, out_shape=[_sds(shape)] * 4,
        compiler_params=_params("arbitrary"),
    )(w, m, v, land, sums)


def _allreduce_small(small, after):
    shape = small.shape

    def body(in_ref, _after, out_ref, gath, send, recv):
        me = _my_index()
        for k in range(N_DEV):
            @pl.when(me != k)
            def _():
                pltpu.make_async_remote_copy(
                    src_ref=in_ref, dst_ref=gath.at[me], send_sem=send, recv_sem=recv,
                    device_id=_device_tuple(k), device_id_type=MESH).start()

            @pl.when(me == k)
            def _():
                gath[k] = in_ref[...]
        seven = gath.at[pl.ds(0, N_DEV - 1)]
        pltpu.make_async_remote_copy(
            src_ref=seven, dst_ref=seven, send_sem=send, recv_sem=recv,
            device_id=_device_tuple(0), device_id_type=MESH).wait()
        total = gath[0]
        for s in range(1, N_DEV):
            total = total + gath[s]
        out_ref[...] = total

    return pl.pallas_call(
        body, name="allreduce_small",
        in_specs=[VMEM, ANY], out_specs=VMEM, out_shape=_sds(shape),
        scratch_shapes=[pltpu.VMEM((N_DEV,) + shape, F32),
                        pltpu.SemaphoreType.DMA, pltpu.SemaphoreType.DMA],
    )(small, after)


def _adam_update(w_ref, m_ref, v_ref, grad, grad_ref, delta_ref, nm_ref, nv_ref):
    new_m = ADAM_B1 * m_ref[...] + (1.0 - ADAM_B1) * grad
    new_v = ADAM_B2 * v_ref[...] + (1.0 - ADAM_B2) * (grad * grad)
    m_hat = new_m / (1.0 - ADAM_B1 ** ADAM_STEP)
    v_hat = new_v / (1.0 - ADAM_B2 ** ADAM_STEP)
    grad_ref[...] = grad
    delta_ref[...] = -ADAM_LR * (m_hat / (jnp.sqrt(v_hat) + ADAM_EPS) + ADAM_WD * w_ref[...])
    nm_ref[...] = new_m
    nv_ref[...] = new_v


def _adamw(w, m, v, g, *, name):
    def body(w_ref, m_ref, v_ref, g_ref, *outs):
        _adam_update(w_ref, m_ref, v_ref, g_ref[...], *outs)

    whole = pl.BlockSpec(w.shape, lambda i: (0,) * w.ndim)
    return pl.pallas_call(
        body, name=name, grid=(1,), in_specs=[whole] * 4, out_specs=[whole] * 4,
        out_shape=[_sds(w.shape)] * 4, compiler_params=_params("arbitrary"),
    )(w, m, v, g)


def _adamw_gains(small, params):
    n = len(params)

    def body(small_ref, *refs):
        ins, outs = refs[:3 * n], refs[3 * n:]
        for r in range(n):
            width = ins[3 * r].shape[1]
            if width == D_MODEL:
                grad = small_ref[pl.ds(r, 1), :]
            else:
                grad = small_ref[pl.ds(len(_GAINS), 1), pl.ds((r - len(_GAINS)) * width, width)]
            _adam_update(*ins[3 * r:3 * r + 3], grad, *outs[4 * r:4 * r + 4])

    whole = lambda a: pl.BlockSpec(a.shape, lambda i: (0, 0))
    flat = [a for group in params for a in group]
    return pl.pallas_call(
        body, name="adamw_gains", grid=(1,),
        in_specs=[whole(small)] + [whole(a) for a in flat],
        out_specs=[whole(w) for w, _, _ in params for _ in range(4)],
        out_shape=[_sds(w.shape) for w, _, _ in params for _ in range(4)],
        compiler_params=_params("arbitrary"),
    )(small, *flat)


def _adamw_shard(w, m, v, land, dw_full, *, kind, pad, name):
    shape = w.shape
    other = shape[0] if kind == "col" else shape[1]

    def body(w_ref, m_ref, v_ref, land_ref, own_ref, *outs):
        valid = ((slice(None), pl.ds(0, shape[1])) if kind == "col"
                 else (pl.ds(0, shape[0]), slice(None)))
        grad = own_ref[valid].astype(F32)
        for s in range(N_DEV - 1):
            grad = grad + land_ref[(s,) + valid].astype(F32)
        _adam_update(w_ref, m_ref, v_ref, grad, *outs)

    whole = lambda a: pl.BlockSpec(a.shape, lambda i: (0,) * a.ndim)
    own = pl.BlockSpec(_shard_shape(kind, pad, other),
                       (lambda i: (0, _my_index())) if kind == "col" else (lambda i: (_my_index(), 0)))
    return pl.pallas_call(
        body, name=name, grid=(1,),
        in_specs=[whole(w), whole(m), whole(v), whole(land), own],
        out_specs=[whole(w)] * 4, out_shape=[_sds(shape)] * 4,
        compiler_params=_params("arbitrary"),
    )(w, m, v, land, dw_full)


_GAINS = ("ffn1_pre", "ffn1_post", "mix_pre", "mix_post", "ffn2_pre", "ffn2_post", "ple_post")
_SMALL_ROWS = 16


def _stack_gains(get):
    return jnp.concatenate([get(n) for n in _GAINS]
                           + [jnp.concatenate([get("out_sb"), get("out_ch")], axis=1)], axis=0)


def kernel(x, p, g_ffn1_pre, g_ffn1_post, w_ffn1_gate, w_ffn1_up, w_ffn1_down, g_mix_pre, g_mix_post, w_in, g_out_sb, g_out_ch, rel_bias, w_out, g_ffn2_pre, g_ffn2_post, w_ffn2_gate, w_ffn2_up, w_ffn2_down, w_ple_proj, w_ple_gate, g_ple_post, loss_target, m_g_ffn1_pre, m_g_ffn1_post, m_w_ffn1_gate, m_w_ffn1_up, m_w_ffn1_down, m_g_mix_pre, m_g_mix_post, m_w_in, m_g_out_sb, m_g_out_ch, m_rel_bias, m_w_out, m_g_ffn2_pre, m_g_ffn2_post, m_w_ffn2_gate, m_w_ffn2_up, m_w_ffn2_down, m_w_ple_proj, m_w_ple_gate, m_g_ple_post, v_g_ffn1_pre, v_g_ffn1_post, v_w_ffn1_gate, v_w_ffn1_up, v_w_ffn1_down, v_g_mix_pre, v_g_mix_post, v_w_in, v_g_out_sb, v_g_out_ch, v_rel_bias, v_w_out, v_g_ffn2_pre, v_g_ffn2_post, v_w_ffn2_gate, v_w_ffn2_up, v_w_ffn2_down, v_w_ple_proj, v_w_ple_gate, v_g_ple_post):
    given = dict(locals())
    wnames = [n for n, *_ in _WEIGHTS]

    def shard(prefix, n):
        a = given[prefix + "w_" + n][0]
        return a.T if n in _TRANSPOSED else a

    packed, full = _pack_weights([shard("", n) for n in wnames])
    first = _GATHER_STAGES[0]
    anchor = x[0]
    two_level = (0, 2)
    gathers = {}

    def start_stage(stage, after):
        peers = _NEAR_PEERS if stage in two_level else _ALL_PEERS
        gathers[stage] = _gather_start(stage, _GATHER_STAGES[stage], packed, full, after,
                                       peers=peers)

    start_stage(0, anchor)

    relays = {}

    def first_level(stage, after):
        names = _GATHER_STAGES[stage]
        last_stage = stage + 1 == len(_GATHER_STAGES)
        count = len(_NEAR_PEERS) if stage in two_level else N_DEV - 1
        ws = _gather_wait(stage, names, gathers[stage], after, count=count)
        if not last_stage:
            start_stage(stage + 1, ws[names[0]])
        if stage in two_level:
            relays[stage] = _relay_start(stage, names, ws,
                                         anchor if last_stage else gathers[stage + 1][-1])
            return ws, relays[stage][-1]
        return ws, None if last_stage else gathers[stage + 1][-1]

    def weights_early(stage, after):
        return first_level(stage, after)[1][:1, :1]

    def weights_for(stage, after):
        names = _GATHER_STAGES[stage]
        ws, token = (None, None) if stage in relays else first_level(stage, after)
        if stage in relays:
            relay = relays[stage]
            ws = _gather_wait(f"{stage}r", names, relay, after, count=len(_FAR_CHIPS))
            token = None if stage + 1 == len(_GATHER_STAGES) else gathers[stage + 1][-1]
        return ws, jnp.zeros((1, 1), F32) if token is None else token[:1, :1]

    scatters = {}

    last = len(_SCATTER_STAGES) - 1

    def grads_done(stage, grads):
        names = _SCATTER_STAGES[stage]
        start = _pair_start if stage == last else _scatter_start
        scatters[stage] = start(stage, names, [grads[n] for n in names], anchor)
        return scatters[stage][-1][:1, :1]

    gains = {n: given["g_" + n] for n in _GAINS + ("out_sb", "out_ch")}
    fvec = _rel_bias_to_fvec(rel_bias[0])
    loss, dx, dg, dbias = _local_step(x[0], p[0, 0], loss_target[0], gains,
                                      weights_for, grads_done, fvec, weights_early)

    results = {}

    def finish(stage, after):
        names = _SCATTER_STAGES[stage]
        dws, lands = _scatter_wait(stage, names, scatters[stage], after)
        for n in names:
            kind, _, pad, _ = _SPEC[n]
            out = _adamw_shard(shard("", n), shard("m_", n), shard("v_", n), lands[n], dws[n],
                               kind=kind, pad=pad, name="adamw_" + n)
            results["w_" + n] = [a.T for a in out] if n in _TRANSPOSED else out
        return results["w_" + names[-1]][0]

    names = _SCATTER_STAGES[last]
    whole = lambda w, ref: ref
    send, recv, src, land, _ = scatters[last]
    out = _split_wait(f"pair_wait{last}", len(names), whole, send, recv, src, land, dx,
                      keep_sources=True)
    sums = [_pair_sum(dwf, pair, pad=_SPEC[n][2], name="pair_sum_" + n)
            for n, dwf, pair in zip(names, out[:len(names)], out[len(names):])]
    send, recv, src, land, after = _chip_start(last, names, sums, anchor)
    for stage in range(last):
        after = finish(stage, after)
    dfvec = _bias_grad(dbias, after, name="bias_grad")
    out = _split_wait(f"chip_wait{last}", len(names), whole, send, recv, src, land, dfvec,
                      keep_sources=True)
    for n, own, landed in zip(names, out[:len(names)], out[len(names):]):
        res = _adamw_chip(shard("", n), shard("m_", n), shard("v_", n), landed, own,
                          name="adamw_" + n)
        results["w_" + n] = [a.T for a in res] if n in _TRANSPOSED else res
        after = res[0]
    loss_col = jnp.pad(loss[:, :1], ((0, N_DEV - 1), (0, D_MODEL - CH_WIN - 1)))
    dfv = jnp.concatenate([dfvec[:, 0, :], loss_col], axis=1)
    small = _allreduce_small(jnp.concatenate([_stack_gains(lambda n: dg[n]), dfv], axis=0), after)
    gain_names = _GAINS + ("out_sb", "out_ch")
    gain_out = _adamw_gains(small, [(given["g_" + n], given["m_g_" + n], given["v_g_" + n])
                                    for n in gain_names])
    for r, n in enumerate(gain_names):
        results["g_" + n] = gain_out[4 * r:4 * r + 4]
    d_rel = _fvec_grad_to_rel_bias(small[N_DEV:, :CH_WIN].reshape(N_DEV, 1, CH_WIN))
    results["rel_bias"] = _adamw(rel_bias[0], m_rel_bias[0], v_rel_bias[0], d_rel,
                                 name="adamw_rel_bias")

    order = ("g_ffn1_pre", "g_ffn1_post", "w_ffn1_gate", "w_ffn1_up", "w_ffn1_down",
             "g_mix_pre", "g_mix_post", "w_in", "g_out_sb", "g_out_ch", "rel_bias", "w_out",
             "g_ffn2_pre", "g_ffn2_post", "w_ffn2_gate", "w_ffn2_up", "w_ffn2_down",
             "w_ple_proj", "w_ple_gate", "g_ple_post")

    def leaf(name, idx):
        a = results[name][idx]
        return a if name.startswith("g_") else a[None]

    total_loss = small[N_DEV, CH_WIN]
    return (total_loss, dx[None],
            *[leaf(n, 0) for n in order], *[leaf(n, 1) for n in order],
            *[leaf(n, 2) for n in order], *[leaf(n, 3) for n in order])
```

```python
import jax
import jax.numpy as jnp
from jax import lax
from jax.experimental import pallas as pl
from jax.experimental.pallas import tpu as pltpu

F32 = jnp.float32
BF16 = jnp.bfloat16

N_DEV = 8
D_MODEL = 1024
D_FF = 2816
FF_SHARD = D_FF // N_DEV
FF_SHARD_PAD = 384
D_FF_PAD = FF_SHARD_PAD * N_DEV
QKV_WIDTH = 3 * D_MODEL
QKV_SHARD = QKV_WIDTH // N_DEV
PLE_DIM = 256
ROW_SHARD = D_MODEL // N_DEV
HEAD_DIM = 64
PAIR = 2 * HEAD_DIM
N_PAIRS = 4
CHUNK = 64
LOOKBACK = 8
REL_CLIP = 128
N_REL = 2 * REL_CLIP + 1
CH_QB = 256
CH_LOOK = LOOKBACK * CHUNK
CH_WIN = CH_LOOK + CH_QB
SB_QB = 512
SB_KB = 256
SB_GROUP = 2
SB_LANES = tuple(slice(g * 128, (g + 1) * 128) for g in range(SB_GROUP))
EPS = 1e-6
NEG_INF = -1e30
ATT_SCALE = HEAD_DIM ** -0.5
ADAM_LR = 0.001
ADAM_B1 = 0.9
ADAM_B2 = 0.999
ADAM_EPS = 1e-08
ADAM_WD = 0.01
ADAM_STEP = 10
VMEM_LIMIT_BYTES = 48 * 1024 * 1024
MESH = pl.DeviceIdType.MESH

ANY = pl.BlockSpec(memory_space=pl.ANY)
VMEM = pl.BlockSpec(memory_space=pltpu.VMEM)


def _params(*sem):
    return pltpu.CompilerParams(dimension_semantics=sem or None,
                                vmem_limit_bytes=VMEM_LIMIT_BYTES)


def _sds(shape, dtype=F32):
    return jax.ShapeDtypeStruct(shape, dtype)


def _bf(x):
    return x.astype(BF16)


def _dot(a, b):
    return jnp.dot(_bf(a), _bf(b), preferred_element_type=F32)


def _dot_nt(a, b):
    return lax.dot_general(_bf(a), _bf(b), (((1,), (1,)), ((), ())),
                           preferred_element_type=F32)


def _dot_tn(a, b):
    return lax.dot_general(_bf(a), _bf(b), (((0,), (0,)), ((), ())),
                           preferred_element_type=F32)


def _sigmoid(x):
    return 1.0 / (1.0 + jnp.exp(-x))


def _softplus(x):
    return jnp.maximum(x, 0.0) + jnp.log(1.0 + jnp.exp(-jnp.abs(x)))


def _rstd(x):
    return lax.rsqrt(jnp.mean(x * x, axis=-1, keepdims=True) + EPS)


def _rms(x, g):
    return x * _rstd(x) * g


def _rms_bwd(dy, x, g):
    r = _rstd(x)
    w = dy * g
    dx = r * (w - x * (r * r) * jnp.mean(w * x, axis=-1, keepdims=True))
    dg = jnp.sum(dy * (x * r), axis=0, keepdims=True)
    return dx, dg


def _head_masks():
    lane = lax.broadcasted_iota(jnp.int32, (1, PAIR), 1)
    return lane < HEAD_DIM, lane >= HEAD_DIM


def _ffn_fwd(x, g_pre, g_post, wg, wu, wd, *, name):
    t = x.shape[0]
    tm, tj = 512, 1024
    ni, nj = t // tm, D_FF_PAD // tj

    def body(x_ref, gpre_ref, gpost_ref, wg_ref, wu_ref, wd_ref,
             h_ref, n_ref, a_ref, b_ref, f_ref, acc_ref):
        j = pl.program_id(1)

        @pl.when(j == 0)
        def _():
            n_ref[...] = _bf(_rms(x_ref[...], gpre_ref[...]))
            acc_ref[...] = jnp.zeros_like(acc_ref)

        n = n_ref[...]
        a = _dot_nt(n, wg_ref[...])
        b = _dot_nt(n, wu_ref[...])
        a_ref[...] = a
        b_ref[...] = b
        hmid = a * _sigmoid(a) * b
        acc_ref[...] += jnp.dot(_bf(hmid), wd_ref[...], preferred_element_type=F32)

        @pl.when(j == nj - 1)
        def _():
            f = acc_ref[...]
            f_ref[...] = f
            h_ref[...] = x_ref[...] + 0.5 * _rms(f, gpost_ref[...])

    row = pl.BlockSpec((tm, D_MODEL), lambda i, j: (i, 0))
    gain = pl.BlockSpec((1, D_MODEL), lambda i, j: (0, 0))
    col = pl.BlockSpec((tm, tj), lambda i, j: (i, j))
    wtile = pl.BlockSpec((tj, D_MODEL), lambda i, j: (j, 0))
    return pl.pallas_call(
        body, name=name, grid=(ni, nj),
        in_specs=[row, gain, gain, wtile, wtile, wtile],
        out_specs=[row, row, col, col, row],
        out_shape=[_sds((t, D_MODEL)), _sds((t, D_MODEL), BF16),
                   _sds((t, D_FF_PAD)), _sds((t, D_FF_PAD)), _sds((t, D_MODEL))],
        scratch_shapes=[pltpu.VMEM((tm, D_MODEL), F32)],
        compiler_params=_params("arbitrary", "arbitrary"),
    )(x, g_pre, g_post, wg, wu, wd)


def _ffn_bwd(n, df, a, b, wg, wu, wd, *, name):
    t = n.shape[0]
    tj, tm, ts = 256, t, 512
    nj, ni, ns = D_FF_PAD // tj, t // tm, tm // ts

    def body(n_hbm, df_hbm, a_ref, b_ref, wg_ref, wu_ref, wd_ref,
             dwg_ref, dwu_ref, dwd_ref, dn_hbm,
             n_v, df_v, dn_v, ag, au, ad, sem):
        j, i = pl.program_id(0), pl.program_id(1)

        @pl.when((j == 0) & (i == 0))
        def _():
            c1 = pltpu.make_async_copy(n_hbm, n_v, sem.at[0])
            c2 = pltpu.make_async_copy(df_hbm, df_v, sem.at[1])
            c1.start()
            c2.start()
            dn_v[...] = jnp.zeros_like(dn_v)
            c1.wait()
            c2.wait()

        @pl.when(i == 0)
        def _():
            ag[...] = jnp.zeros_like(ag)
            au[...] = jnp.zeros_like(au)
            ad[...] = jnp.zeros_like(ad)

        wgj, wuj, wdj = wg_ref[...], wu_ref[...], wd_ref[...]
        for s in range(ns):
            local = pl.ds(s * ts, ts)
            rows = pl.ds(pl.multiple_of(i * tm + s * ts, ts), ts)
            av, bv = a_ref[local, :], b_ref[local, :]
            sig = _sigmoid(av)
            silu = av * sig
            dfr = df_v[rows, :]
            nr = n_v[rows, :]
            dhmid = _dot_nt(dfr, wdj)
            da = dhmid * bv * (sig * (1.0 + av * (1.0 - sig)))
            db = dhmid * silu
            ad[...] += _dot_tn(silu * bv, dfr)
            ag[...] += _dot_tn(da, nr)
            au[...] += _dot_tn(db, nr)
            dn_v[rows, :] += _dot(da, wgj) + _dot(db, wuj)

        @pl.when(i == ni - 1)
        def _():
            dwg_ref[...] = _bf(ag[...])
            dwu_ref[...] = _bf(au[...])
            dwd_ref[...] = _bf(ad[...])

        @pl.when((j == nj - 1) & (i == ni - 1))
        def _():
            c = pltpu.make_async_copy(dn_v, dn_hbm, sem.at[0])
            c.start()
            c.wait()

    roww = pl.BlockSpec((tj, D_MODEL), lambda j, i: (j, 0))
    act = pl.BlockSpec((tm, tj), lambda j, i: (i, j))
    return pl.pallas_call(
        body, name=name, grid=(nj, ni),
        in_specs=[ANY, ANY, act, act, roww, roww, roww],
        out_specs=[roww, roww, roww, ANY],
        out_shape=[_sds((D_FF_PAD, D_MODEL), BF16)] * 3 + [_sds((t, D_MODEL))],
        scratch_shapes=[pltpu.VMEM((t, D_MODEL), BF16), pltpu.VMEM((t, D_MODEL), BF16),
                        pltpu.VMEM((t, D_MODEL), F32)]
        + [pltpu.VMEM((tj, D_MODEL), F32)] * 3 + [pltpu.SemaphoreType.DMA((2,))],
        compiler_params=_params("arbitrary", "arbitrary"),
    )(n, df, a, b, wg, wu, wd)


def _junction(dres, pre=None, post=None, *, name):
    t = dres.shape[0]
    tm = 512
    ni = t // tm
    n_in = 1 + (3 if pre else 0) + (2 if post else 0)
    coef = post[2] if post else None

    def body(*refs):
        ins, outs = list(refs[:n_in]), list(refs[n_in:])
        i = pl.program_id(0)
        dh = ins.pop(0)[...]
        if pre:
            dn_ref, x_ref, gpre_ref = ins.pop(0), ins.pop(0), ins.pop(0)
            dh_ref, dgpre_ref = outs.pop(0), outs.pop(0)
            dx, dg = _rms_bwd(dn_ref[...], x_ref[...], gpre_ref[...])
            dh = dh + dx
            dh_ref[...] = dh

            @pl.when(i == 0)
            def _():
                dgpre_ref[...] = jnp.zeros_like(dgpre_ref)
            dgpre_ref[...] += dg
        if post:
            f_ref, gpost_ref = ins.pop(0), ins.pop(0)
            df_ref, dgpost_ref = outs.pop(0), outs.pop(0)
            df, dg = _rms_bwd(coef * dh, f_ref[...], gpost_ref[...])
            df_ref[...] = _bf(df)

            @pl.when(i == 0)
            def _():
                dgpost_ref[...] = jnp.zeros_like(dgpost_ref)
            dgpost_ref[...] += dg

    row = pl.BlockSpec((tm, D_MODEL), lambda i: (i, 0))
    gain = pl.BlockSpec((1, D_MODEL), lambda i: (0, 0))
    args, in_specs, out_specs, out_shape = [dres], [row], [], []
    if pre:
        args += list(pre)
        in_specs += [row, row, gain]
        out_specs += [row, gain]
        out_shape += [_sds((t, D_MODEL)), _sds((1, D_MODEL))]
    if post:
        args += [post[0], post[1]]
        in_specs += [row, gain]
        out_specs += [row, gain]
        out_shape += [_sds((t, D_MODEL), BF16), _sds((1, D_MODEL))]
    return pl.pallas_call(
        body, name=name, grid=(ni,), in_specs=in_specs, out_specs=out_specs,
        out_shape=out_shape, compiler_params=_params("arbitrary"),
    )(*args)


def _qkv_fwd(h, g, win, *, name):
    t = h.shape[0]
    tm, tn = min(1024, t), 1024
    ni, nj = t // tm, QKV_WIDTH // tn

    def body(h_ref, g_ref, w_ref, qkv_ref, u_ref):
        @pl.when(pl.program_id(1) == 0)
        def _():
            u_ref[...] = _bf(_rms(h_ref[...], g_ref[...]))
        qkv_ref[...] = jnp.dot(u_ref[...], w_ref[...], preferred_element_type=F32)

    row = pl.BlockSpec((tm, D_MODEL), lambda i, j: (i, 0))
    return pl.pallas_call(
        body, name=name, grid=(ni, nj),
        in_specs=[row, pl.BlockSpec((1, D_MODEL), lambda i, j: (0, 0)),
                  pl.BlockSpec((D_MODEL, tn), lambda i, j: (0, j))],
        out_specs=[pl.BlockSpec((tm, tn), lambda i, j: (i, j)), row],
        out_shape=[_sds((t, QKV_WIDTH)), _sds((t, D_MODEL), BF16)],
        compiler_params=_params("arbitrary", "arbitrary"),
    )(h, g, win)


def _qkv_bwd(dq, dk, dv, u, win, *, name):
    t = u.shape[0]
    tn, ts = 512, 512
    nj, ns = QKV_WIDTH // tn, t // ts

    def body(dq_ref, dk_ref, dv_ref, u_ref, w_ref, dw_ref, du_hbm, du_v, acc_ref, sem):
        j = pl.program_id(0)

        @pl.when(j == 0)
        def _():
            du_v[...] = jnp.zeros_like(du_v)

        wj = w_ref[...]
        for role, d_ref in enumerate((dq_ref, dk_ref, dv_ref)):
            @pl.when(j % 3 == role)
            def _():
                acc_ref[...] = jnp.zeros_like(acc_ref)
                for s in range(ns):
                    rows = pl.ds(s * ts, ts)
                    dcol = d_ref[rows, :]
                    acc_ref[...] += _dot_tn(u_ref[rows, :], dcol)
                    du_v[rows, :] += _dot_nt(dcol, wj)
                dw_ref[...] = _bf(acc_ref[...])

        @pl.when(j == nj - 1)
        def _():
            c = pltpu.make_async_copy(du_v, du_hbm, sem)
            c.start()
            c.wait()

    colw = pl.BlockSpec((D_MODEL, tn), lambda j: (0, j))
    grp = pl.BlockSpec((t, tn), lambda j: (0, j // 3))
    return pl.pallas_call(
        body, name=name, grid=(nj,),
        in_specs=[grp, grp, grp, pl.BlockSpec((t, D_MODEL), lambda j: (0, 0)), colw],
        out_specs=[colw, ANY],
        out_shape=[_sds((D_MODEL, QKV_WIDTH), BF16), _sds((t, D_MODEL))],
        scratch_shapes=[pltpu.VMEM((t, D_MODEL), F32), pltpu.VMEM((D_MODEL, tn), F32),
                        pltpu.SemaphoreType.DMA],
        compiler_params=_params("arbitrary"),
    )(dq, dk, dv, u, win)


def _sb_stack(x):
    lo, hi = _head_masks()
    return jnp.concatenate([jnp.where(lo, x, 0.0), jnp.where(hi, x, 0.0)], axis=0)


def _sb_unstack(x2, blk):
    return jnp.where(_head_masks()[0], x2[:blk], x2[blk:])


def _sb_rows_from(x2, blk, r0):
    return x2 if r0 == 0 else jnp.concatenate([x2[r0:blk], x2[blk + r0:]], axis=0)


def _sb_rows_merge(full2, sub2, blk, r0):
    if r0 == 0:
        return sub2
    rows = blk - r0
    return jnp.concatenate([full2[:r0], sub2[:rows], full2[blk:blk + r0], sub2[rows:]], axis=0)


def _sb_mask(qb, kb, offset):
    r = lax.broadcasted_iota(jnp.int32, (2 * qb, kb), 0) & (qb - 1)
    c = lax.broadcasted_iota(jnp.int32, (2 * qb, kb), 1) + offset
    return c < r


def _tri(n, keep):
    r = lax.broadcasted_iota(jnp.int32, (n, n), 0)
    c = lax.broadcasted_iota(jnp.int32, (n, n), 1)
    return jnp.where(keep(r, c), 1.0, 0.0).astype(BF16)


def _cumsum01(x, u):
    m = x.shape[0]
    hi = _bf(x)
    lo = _bf(x - hi.astype(F32))
    both = jnp.dot(jnp.concatenate([hi, lo], axis=0), u, preferred_element_type=F32)
    return both[:m] + both[m:]


def _sb_fwd(qkv, *, name):
    t = qkv.shape[0]
    blk, kb = min(SB_QB, t), SB_KB
    ni, per = t // blk, blk // kb

    def body(q_ref, k_ref, v_ref, o_ref, ltot_ref):
        i = pl.program_id(1)
        u_after = _tri(kb, lambda r, c: r > c)
        q2 = [_bf(_sb_stack(q_ref[:, lanes] * ATT_SCALE)) for lanes in SB_LANES]

        def tile(g, k0, mask, acc, c_l):
            kj = k_ref[pl.ds(k0, kb), SB_LANES[g]]
            vj = v_ref[pl.ds(k0, kb), SB_LANES[g]]
            z = _dot_nt(q2[g], kj)
            sp = _softplus(z)
            lf = -sp if mask is None else jnp.where(mask, -sp, 0.0)
            a = jnp.exp(z - sp + _cumsum01(lf, u_after) + c_l)
            if mask is not None:
                a = jnp.where(mask, a, 0.0)
            return acc + _dot(a, vj), c_l + jnp.sum(lf, axis=1, keepdims=True)

        def tiles(k0, mask, carry):
            return tuple(tile(g, k0, mask, *carry[g]) for g in range(SB_GROUP))

        carry = ((jnp.zeros((2 * blk, PAIR), F32), jnp.zeros((2 * blk, 1), F32)),) * SB_GROUP
        for d in reversed(range(per)):
            carry = tiles(pl.multiple_of(i * blk + d * kb, kb), _sb_mask(blk, kb, d * kb), carry)
        carry = lax.fori_loop(
            1, per * i + 1,
            lambda jj, c: tiles(pl.multiple_of((per * i - jj) * kb, kb), None, c), carry)
        for g, (acc, c_l) in enumerate(carry):
            o_ref[:, SB_LANES[g]] = _sb_unstack(acc, blk)
            ltot_ref[:, SB_LANES[g]] = _sb_unstack(jnp.broadcast_to(c_l, (2 * blk, PAIR)), blk)

    width = SB_GROUP * PAIR
    blkspec = pl.BlockSpec((blk, width), lambda p, i: (i, p))
    n_steps = N_PAIRS // SB_GROUP
    return pl.pallas_call(
        body, name=name, grid=(n_steps, ni),
        in_specs=[blkspec,
                  pl.BlockSpec((t, width), lambda p, i: (0, n_steps + p)),
                  pl.BlockSpec((t, width), lambda p, i: (0, 2 * n_steps + p))],
        out_specs=[blkspec, blkspec],
        out_shape=[_sds((t, D_MODEL)), _sds((t, D_MODEL // 2))],
        compiler_params=_params("arbitrary", "arbitrary"),
    )(qkv, qkv, qkv)


def _sb_bwd(qkv, ltot, do, *, name):
    t = qkv.shape[0]
    blk, kb = min(SB_QB, t), SB_KB
    ni, per = t // blk, blk // kb

    def body(q_ref, k_ref, v_ref, lt_ref, do_ref, dq_ref, dkout_ref, dvout_ref, dk_ref, dv_ref):
        i = pl.program_id(1)

        @pl.when(i == 0)
        def _():
            dk_ref[...] = jnp.zeros_like(dk_ref)
            dv_ref[...] = jnp.zeros_like(dv_ref)

        u_upto = _tri(kb, lambda r, c: r <= c)
        u_before = _tri(kb, lambda r, c: r < c)
        lane = lax.broadcasted_iota(jnp.int32, (1, PAIR), 1)
        q2 = [_bf(_sb_stack(q_ref[:, lanes] * ATT_SCALE)) for lanes in SB_LANES]
        do2 = [_bf(_sb_stack(do_ref[:, lanes])) for lanes in SB_LANES]
        total = [jnp.concatenate(
            [jnp.sum(jnp.where(lane == h * HEAD_DIM, lt_ref[:, lanes], 0.0), axis=1, keepdims=True)
             for h in range(2)], axis=0) for lanes in SB_LANES]

        def tile(g, ops, k0, mask, dq_acc, c_l, c_g):
            qg, dog, tot = ops
            krows = pl.ds(k0, kb)
            kj = k_ref[krows, SB_LANES[g]]
            vj = v_ref[krows, SB_LANES[g]]
            z = _dot_nt(qg, kj)
            sp = _softplus(z)
            sig = jnp.exp(z - sp)
            lf = -sp if mask is None else jnp.where(mask, -sp, 0.0)
            a = jnp.exp(z - sp + tot - (_cumsum01(lf, u_upto) + c_l))
            if mask is not None:
                a = jnp.where(mask, a, 0.0)
            gw = a * _dot_nt(dog, vj)
            g_before = jnp.dot(_bf(gw), u_before, preferred_element_type=F32) + c_g
            dz = gw * (1.0 - sig) - g_before * sig
            if mask is not None:
                dz = jnp.where(mask, dz, 0.0)
            dk_ref[krows, SB_LANES[g]] += _dot_tn(dz, qg)
            dv_ref[krows, SB_LANES[g]] += _dot_tn(a, dog)
            return (dq_acc + _dot(dz, kj), c_l + jnp.sum(lf, axis=1, keepdims=True),
                    c_g + jnp.sum(gw, axis=1, keepdims=True))

        def tiles(ops, k0, mask, carry):
            return tuple(tile(g, ops[g], k0, mask, *carry[g]) for g in range(SB_GROUP))

        ops = tuple(zip(q2, do2, total))
        zero = (jnp.zeros((2 * blk, PAIR), F32), jnp.zeros((2 * blk, 1), F32),
                jnp.zeros((2 * blk, 1), F32))
        carry = lax.fori_loop(
            0, per * i, lambda j, c: tiles(ops, pl.multiple_of(j * kb, kb), None, c),
            (zero,) * SB_GROUP)
        for d in range(per):
            r0 = d * kb
            sub = tiles(tuple(tuple(_sb_rows_from(a, blk, r0) for a in o) for o in ops),
                        pl.multiple_of(i * blk + r0, kb), _sb_mask(blk - r0, kb, 0),
                        tuple(tuple(_sb_rows_from(a, blk, r0) for a in c) for c in carry))
            carry = tuple(tuple(_sb_rows_merge(a, s, blk, r0) for a, s in zip(c, cs))
                          for c, cs in zip(carry, sub))
        for g, (dq_acc, _, _) in enumerate(carry):
            dq_ref[:, SB_LANES[g]] = _bf(_sb_unstack(dq_acc, blk) * ATT_SCALE)

        @pl.when(i == ni - 1)
        def _():
            dkout_ref[...] = _bf(dk_ref[...])
            dvout_ref[...] = _bf(dv_ref[...])

    width = SB_GROUP * PAIR
    n_steps = N_PAIRS // SB_GROUP
    blkspec = lambda off: pl.BlockSpec((blk, width), lambda p, i: (i, off + p))
    full = lambda off: pl.BlockSpec((t, width), lambda p, i: (0, off + p))
    return pl.pallas_call(
        body, name=name, grid=(n_steps, ni),
        in_specs=[blkspec(0), full(n_steps), full(2 * n_steps), blkspec(0), blkspec(0)],
        out_specs=[blkspec(0), full(0), full(0)],
        out_shape=[_sds((t, D_MODEL), BF16)] * 3,
        scratch_shapes=[pltpu.VMEM((t, width), F32), pltpu.VMEM((t, width), F32)],
        compiler_params=_params("arbitrary", "arbitrary"),
    )(qkv, qkv, qkv, ltot, do)


def _ch_mask(i):
    r = lax.broadcasted_iota(jnp.int32, (CH_QB, CH_WIN), 0)
    c = lax.broadcasted_iota(jnp.int32, (CH_QB, CH_WIN), 1)
    qc = LOOKBACK + lax.shift_right_arithmetic(r, 6)
    kc = lax.shift_right_arithmetic(c, 6)
    first = i * (CH_QB // CHUNK) - LOOKBACK
    return (kc <= qc) & (kc >= qc - LOOKBACK) & (kc + first >= 0)


def _ch_probs(qm, kw, bias_h, mask):
    z = _dot_nt(qm, kw) * ATT_SCALE + bias_h
    z = jnp.where(mask, z, NEG_INF)
    e = jnp.exp(z - jnp.max(z, axis=1, keepdims=True))
    return e / jnp.sum(e, axis=1, keepdims=True)


def _ch_fill(pad_ref, src_ref, t):
    pad_ref[pl.ds(0, CH_LOOK), :] = jnp.zeros((CH_LOOK, PAIR), BF16)
    pad_ref[pl.ds(CH_LOOK, t), :] = _bf(src_ref[...])


def _ch_fwd(qkv, bias, o_in, *, name):
    t = qkv.shape[0]
    ni = t // CH_QB

    def body(q_ref, k_ref, v_ref, bias_ref, _alias, o_ref, kpad, vpad):
        i = pl.program_id(1)

        @pl.when(i == 0)
        def _():
            _ch_fill(kpad, k_ref, t)
            _ch_fill(vpad, v_ref, t)

        win = pl.ds(pl.multiple_of(i * CH_QB, CH_QB), CH_WIN)
        kw, vw = kpad[win, :], vpad[win, :]
        mask = _ch_mask(i)
        q = q_ref[...]
        outs = []
        for h, hm in enumerate(_head_masks()):
            p = _ch_probs(jnp.where(hm, q, 0.0), kw, bias_ref[h], mask)
            outs.append(_dot(p, vw))
        o_ref[...] = jnp.where(_head_masks()[0], outs[0], outs[1])

    full = lambda off: pl.BlockSpec((t, PAIR), lambda p, i: (0, off + p))
    return pl.pallas_call(
        body, name=name, grid=(N_PAIRS, ni),
        in_specs=[pl.BlockSpec((CH_QB, PAIR), lambda p, i: (i, 3 * N_PAIRS + p)),
                  full(4 * N_PAIRS), full(5 * N_PAIRS),
                  pl.BlockSpec((2, CH_QB, CH_WIN), lambda p, i: (p, 0, 0)), ANY],
        out_specs=pl.BlockSpec((CH_QB, PAIR), lambda p, i: (i, N_PAIRS + p)),
        out_shape=_sds((t, D_MODEL)),
        scratch_shapes=[pltpu.VMEM((t + CH_LOOK, PAIR), BF16)] * 2,
        input_output_aliases={4: 0},
        compiler_params=_params("arbitrary", "arbitrary"),
    )(qkv, qkv, qkv, bias, o_in)


def _ch_bwd(qkv, bias, o, do, dq_in, dk_in, dv_in, *, name):
    t = qkv.shape[0]
    ni = t // CH_QB

    def body(q_ref, k_ref, v_ref, bias_ref, o_ref, do_ref, _a0, _a1, _a2,
             dq_ref, dkout_ref, dvout_ref, dbias_ref, kpad, vpad, dkpad, dvpad):
        i = pl.program_id(1)

        @pl.when(i == 0)
        def _():
            _ch_fill(kpad, k_ref, t)
            _ch_fill(vpad, v_ref, t)
            dkpad[...] = jnp.zeros_like(dkpad)
            dvpad[...] = jnp.zeros_like(dvpad)
            dbias_ref[...] = jnp.zeros_like(dbias_ref)

        win = pl.ds(pl.multiple_of(i * CH_QB, CH_QB), CH_WIN)
        kw, vw = kpad[win, :], vpad[win, :]
        mask = _ch_mask(i)
        q, o_blk, do_blk = q_ref[...], o_ref[...], do_ref[...]
        dqs = []
        for h, hm in enumerate(_head_masks()):
            qm = _bf(jnp.where(hm, q, 0.0))
            dom = jnp.where(hm, do_blk, 0.0)
            delta = jnp.sum(dom * o_blk, axis=1, keepdims=True)
            dom = _bf(dom)
            p = _ch_probs(qm, kw, bias_ref[h], mask)
            ds = p * (_dot_nt(dom, vw) - delta)
            dbias_ref[h] += ds
            dsz = ds * ATT_SCALE
            dqs.append(_dot(dsz, kw))
            dkpad[win, :] += _dot_tn(dsz, qm)
            dvpad[win, :] += _dot_tn(p, dom)
        dq_ref[...] = _bf(jnp.where(_head_masks()[0], dqs[0], dqs[1]))

        @pl.when(i == ni - 1)
        def _():
            dkout_ref[...] = _bf(dkpad[pl.ds(CH_LOOK, t), :])
            dvout_ref[...] = _bf(dvpad[pl.ds(CH_LOOK, t), :])

    blkspec = lambda off: pl.BlockSpec((CH_QB, PAIR), lambda p, i: (i, off + p))
    full = lambda off: pl.BlockSpec((t, PAIR), lambda p, i: (0, off + p))
    bias_spec = pl.BlockSpec((2, CH_QB, CH_WIN), lambda p, i: (p, 0, 0))
    return pl.pallas_call(
        body, name=name, grid=(N_PAIRS, ni),
        in_specs=[blkspec(3 * N_PAIRS), full(4 * N_PAIRS), full(5 * N_PAIRS), bias_spec,
                  blkspec(N_PAIRS), blkspec(N_PAIRS), ANY, ANY, ANY],
        out_specs=[blkspec(N_PAIRS), full(N_PAIRS), full(N_PAIRS), bias_spec],
        out_shape=[_sds((t, D_MODEL), BF16)] * 3 + [_sds((2 * N_PAIRS, CH_QB, CH_WIN))],
        scratch_shapes=[pltpu.VMEM((t + CH_LOOK, PAIR), BF16)] * 2
        + [pltpu.VMEM((t + CH_LOOK, PAIR), F32)] * 2,
        input_output_aliases={6: 0, 7: 1, 8: 2},
        compiler_params=_params("arbitrary", "arbitrary"),
    )(qkv, qkv, qkv, bias, o, do, dq_in, dk_in, dv_in)


def _bias_expand(fvec, *, name):
    n_heads = fvec.shape[0]

    def body(f_ref, o_ref, rows8):
        row = f_ref[0]
        for r in range(8):
            rows8[pl.ds(r, 1), :] = pltpu.roll(row, r, 1)
        base = rows8[...]
        for blk in range(CH_QB // 8):
            o_ref[0, pl.ds(8 * blk, 8), :] = pltpu.roll(base, 8 * blk, 1)

    return pl.pallas_call(
        body, name=name, grid=(n_heads,),
        in_specs=[pl.BlockSpec((1, 1, CH_WIN), lambda h: (h, 0, 0))],
        out_specs=pl.BlockSpec((1, CH_QB, CH_WIN), lambda h: (h, 0, 0)),
        out_shape=_sds((n_heads, CH_QB, CH_WIN)),
        scratch_shapes=[pltpu.VMEM((8, CH_WIN), F32)],
        compiler_params=_params("arbitrary"),
    )(fvec)


def _bias_grad(dbias, after, *, name):
    n_heads = dbias.shape[0]
    first = CH_LOOK - REL_CLIP

    def body(d_ref, _after, o_ref, acc8):
        acc = jnp.zeros((8, CH_WIN), F32)
        for blk in range(CH_QB // 8):
            acc = acc + pltpu.roll(d_ref[0, pl.ds(8 * blk, 8), :], (CH_WIN - 8 * blk) % CH_WIN, 1)
        acc8[...] = acc
        dvec = jnp.zeros((1, CH_WIN), F32)
        for r in range(8):
            dvec = dvec + pltpu.roll(acc8[pl.ds(r, 1), :], (CH_WIN - r) % CH_WIN, 1)
        lane = lax.broadcasted_iota(jnp.int32, (1, CH_WIN), 1)
        clipped = (lane <= first) | (lane >= first + REL_CLIP + CHUNK)
        total = jnp.sum(jnp.where(clipped, dvec, 0.0), axis=1, keepdims=True)
        o_ref[0] = jnp.where(lane == first, total, dvec)

    return pl.pallas_call(
        body, name=name, grid=(n_heads,),
        in_specs=[pl.BlockSpec((1, CH_QB, CH_WIN), lambda h: (h, 0, 0)), ANY],
        out_specs=pl.BlockSpec((1, 1, CH_WIN), lambda h: (h, 0, 0)),
        out_shape=_sds((n_heads, 1, CH_WIN)),
        scratch_shapes=[pltpu.VMEM((8, CH_WIN), F32)],
        compiler_params=_params("arbitrary"),
    )(dbias, after)


def _out_fwd(o, h1, g_sb, g_ch, g_post, wout, *, name):
    t = o.shape[0]
    tm = 512
    half = D_MODEL // 2

    def body(o_ref, h_ref, gsb_ref, gch_ref, gpost_ref, w_ref, h2_ref, mixed_ref, y_ref):
        ov = o_ref[...]
        mixed = jnp.concatenate([_rms(ov[:, :half], gsb_ref[...]),
                                 _rms(ov[:, half:], gch_ref[...])], axis=1)
        mixed_ref[...] = _bf(mixed)
        y = _dot(mixed, w_ref[...])
        y_ref[...] = y
        h2_ref[...] = h_ref[...] + _rms(y, gpost_ref[...])

    row = pl.BlockSpec((tm, D_MODEL), lambda i: (i, 0))
    gain = lambda n: pl.BlockSpec((1, n), lambda i: (0, 0))
    return pl.pallas_call(
        body, name=name, grid=(t // tm,),
        in_specs=[row, row, gain(half), gain(half), gain(D_MODEL),
                  pl.BlockSpec((D_MODEL, D_MODEL), lambda i: (0, 0))],
        out_specs=[row, row, row],
        out_shape=[_sds((t, D_MODEL)), _sds((t, D_MODEL), BF16), _sds((t, D_MODEL))],
        compiler_params=_params("arbitrary"),
    )(o, h1, g_sb, g_ch, g_post, wout)


def _out_bwd(dy, mixed, o, g_sb, g_ch, wout, *, name):
    t = o.shape[0]
    tm = 512
    ni = t // tm
    half = D_MODEL // 2

    def body(dy_ref, mixed_ref, o_ref, gsb_ref, gch_ref, w_ref,
             dw_ref, do_ref, dgsb_ref, dgch_ref, acc_ref):
        i = pl.program_id(0)

        @pl.when(i == 0)
        def _():
            acc_ref[...] = jnp.zeros_like(acc_ref)
            dgsb_ref[...] = jnp.zeros_like(dgsb_ref)
            dgch_ref[...] = jnp.zeros_like(dgch_ref)

        dyv = dy_ref[...]
        acc_ref[...] += _dot_tn(mixed_ref[...], dyv)
        dm = _dot_nt(dyv, w_ref[...])
        ov = o_ref[...]
        doa, dga = _rms_bwd(dm[:, :half], ov[:, :half], gsb_ref[...])
        dob, dgb = _rms_bwd(dm[:, half:], ov[:, half:], gch_ref[...])
        do_ref[...] = jnp.concatenate([doa, dob], axis=1)
        dgsb_ref[...] += dga
        dgch_ref[...] += dgb

        @pl.when(i == ni - 1)
        def _():
            dw_ref[...] = _bf(acc_ref[...])

    row = pl.BlockSpec((tm, D_MODEL), lambda i: (i, 0))
    gain = pl.BlockSpec((1, half), lambda i: (0, 0))
    sq = pl.BlockSpec((D_MODEL, D_MODEL), lambda i: (0, 0))
    return pl.pallas_call(
        body, name=name, grid=(ni,),
        in_specs=[row, row, row, gain, gain, sq],
        out_specs=[sq, row, gain, gain],
        out_shape=[_sds((D_MODEL, D_MODEL), BF16), _sds((t, D_MODEL)),
                   _sds((1, half)), _sds((1, half))],
        scratch_shapes=[pltpu.VMEM((D_MODEL, D_MODEL), F32)],
        compiler_params=_params("arbitrary"),
    )(dy, mixed, o, g_sb, g_ch, wout)


def _ple(p, h3, target, wp, wgate, g, f_post, g_post, *, name):
    t = h3.shape[0]
    tm = 512
    ni = t // tm

    def body(p_ref, h_ref, tgt_ref, wp_ref, wg_ref, g_ref, f_ref, gf_ref,
             loss_ref, dres_ref, dwp_ref, dwg_ref, dg_ref, df_ref, dgf_ref, accp, accg):
        i = pl.program_id(0)

        @pl.when(i == 0)
        def _():
            loss_ref[...] = jnp.zeros_like(loss_ref)
            dg_ref[...] = jnp.zeros_like(dg_ref)
            dgf_ref[...] = jnp.zeros_like(dgf_ref)
            accp[...] = jnp.zeros_like(accp)
            accg[...] = jnp.zeros_like(accg)

        pv, hv, gv = p_ref[...], h_ref[...], g_ref[...]
        pe = _dot(pv, wp_ref[...])
        sig = _sigmoid(_dot(hv, wg_ref[...]))
        e = pe * sig
        err = hv + _rms(e, gv) - tgt_ref[...]
        tok = jnp.mean(err * err, axis=-1, keepdims=True)
        loss_ref[...] += 0.5 * jnp.sum(tok, axis=0, keepdims=True)
        dh4 = err * (1.0 / D_MODEL)
        de, dg = _rms_bwd(dh4, e, gv)
        dg_ref[...] += dg
        dpe = de * sig
        dgt = de * pe * sig * (1.0 - sig)
        accp[...] += _dot_tn(pv, dpe)
        accg[...] += _dot_tn(hv, dgt)
        dres = dh4 + _dot_nt(dgt, wg_ref[...])
        dres_ref[...] = dres
        df, dgf = _rms_bwd(0.5 * dres, f_ref[...], gf_ref[...])
        df_ref[...] = _bf(df)
        dgf_ref[...] += dgf

        @pl.when(i == ni - 1)
        def _():
            dwp_ref[...] = _bf(accp[...])
            dwg_ref[...] = _bf(accg[...])

    row = pl.BlockSpec((tm, D_MODEL), lambda i: (i, 0))
    const = lambda r, c: pl.BlockSpec((r, c), lambda i: (0, 0))
    return pl.pallas_call(
        body, name=name, grid=(ni,),
        in_specs=[pl.BlockSpec((tm, PLE_DIM), lambda i: (i, 0)), row, row,
                  const(PLE_DIM, D_MODEL), const(D_MODEL, D_MODEL), const(1, D_MODEL),
                  row, const(1, D_MODEL)],
        out_specs=[const(1, 128), row, const(PLE_DIM, D_MODEL), const(D_MODEL, D_MODEL),
                   const(1, D_MODEL), row, const(1, D_MODEL)],
        out_shape=[_sds((1, 128)), _sds((t, D_MODEL)), _sds((PLE_DIM, D_MODEL), BF16),
                   _sds((D_MODEL, D_MODEL), BF16), _sds((1, D_MODEL)),
                   _sds((t, D_MODEL), BF16), _sds((1, D_MODEL))],
        scratch_shapes=[pltpu.VMEM((PLE_DIM, D_MODEL), F32), pltpu.VMEM((D_MODEL, D_MODEL), F32)],
        compiler_params=_params("arbitrary"),
    )(p, h3, target, wp, wgate, g, f_post, g_post)


def _rel_bias_to_fvec(rel_bias):
    rev = rel_bias[:, ::-1]
    n_heads = rel_bias.shape[0]
    first = CH_LOOK - REL_CLIP
    n_var = REL_CLIP + CHUNK
    clipped = rev[:, :1]
    fvec = jnp.concatenate([jnp.broadcast_to(clipped, (n_heads, first)), rev[:, :n_var],
                            jnp.broadcast_to(clipped, (n_heads, CH_WIN - first - n_var))], axis=1)
    return fvec.reshape(n_heads, 1, CH_WIN)


def _fvec_grad_to_rel_bias(dfvec):
    first = CH_LOOK - REL_CLIP
    n_var = REL_CLIP + CHUNK
    rev = jnp.pad(dfvec[:, 0, first:first + n_var], ((0, 0), (0, N_REL - n_var)))
    return rev[:, ::-1]


def _local_step(x, p, target, g, weights_for, grads_done, fvec, weights_early=None):
    bias = _bias_expand(fvec, name="bias_expand")
    w, tie = weights_for(0, bias)
    w = dict(w)
    h1, n1, a1, b1, f1 = _ffn_fwd(x, g["ffn1_pre"] + tie, g["ffn1_post"],
                                  w["ffn1_gate"], w["ffn1_up"], w["ffn1_down"], name="ffn1_fwd")
    more, tie = weights_for(1, h1)
    w.update(more)
    qkv, u = _qkv_fwd(h1, g["mix_pre"] + tie, w["in"], name="qkv_fwd")
    o, ltot = _sb_fwd(qkv, name="sb_fwd")
    tie = weights_early(2, ltot) if weights_early else 0.0
    o = _ch_fwd(qkv, bias, o, name="ch_fwd")
    h2, mixed, y = _out_fwd(o, h1, g["out_sb"] + tie, g["out_ch"], g["mix_post"], w["out"],
                            name="out_fwd")
    w.update(weights_for(2, h2)[0])
    h3, n2, a2, b2, f2 = _ffn_fwd(h2, g["ffn2_pre"], g["ffn2_post"],
                                  w["ffn2_gate"], w["ffn2_up"], w["ffn2_down"], name="ffn2_fwd")
    loss, dh3, dwp, dwgate, dg_ple, df2, dg_ffn2_post = _ple(
        p, h3, target, w["ple_proj"], w["ple_gate"], g["ple_post"], f2, g["ffn2_post"], name="ple")
    tie = grads_done(0, {"ple_proj": dwp, "ple_gate": dwgate})
    dwg2, dwu2, dwd2, dn2 = _ffn_bwd(n2, df2, a2, b2, w["ffn2_gate"], w["ffn2_up"],
                                     w["ffn2_down"], name="ffn2_bwd")
    tie = tie + grads_done(1, {"ffn2_gate": dwg2, "ffn2_up": dwu2, "ffn2_down": dwd2})
    dh2, dg_ffn2_pre, dy, dg_mix_post = _junction(
        dh3, pre=(dn2, h2, g["ffn2_pre"] + tie), post=(y, g["mix_post"], 1.0), name="junction2")
    dwout, do, dg_sb, dg_ch = _out_bwd(dy, mixed, o, g["out_sb"], g["out_ch"], w["out"],
                                       name="out_bwd")
    dq, dk, dv = _sb_bwd(qkv, ltot, do, name="sb_bwd")
    dq, dk, dv, dbias = _ch_bwd(qkv, bias, o, do, dq, dk, dv, name="ch_bwd")
    dwin, du = _qkv_bwd(dq, dk, dv, u, w["in"], name="qkv_bwd")
    tie = grads_done(2, {"out": dwout, "in": dwin})
    dh1, dg_mix_pre, df1, dg_ffn1_post = _junction(
        dh2, pre=(du, h1, g["mix_pre"] + tie), post=(f1, g["ffn1_post"], 0.5), name="junction1")
    dwg1, dwu1, dwd1, dn1 = _ffn_bwd(n1, df1, a1, b1, w["ffn1_gate"], w["ffn1_up"],
                                     w["ffn1_down"], name="ffn1_bwd")
    tie = grads_done(3, {"ffn1_gate": dwg1, "ffn1_up": dwu1, "ffn1_down": dwd1})
    dx, dg_ffn1_pre = _junction(dh1, pre=(dn1, x, g["ffn1_pre"] + tie), name="junction0")

    dg = {"ffn1_pre": dg_ffn1_pre, "ffn1_post": dg_ffn1_post, "mix_pre": dg_mix_pre,
          "mix_post": dg_mix_post, "out_sb": dg_sb, "out_ch": dg_ch,
          "ffn2_pre": dg_ffn2_pre, "ffn2_post": dg_ffn2_post, "ple_post": dg_ple}
    return loss, dx, dg, dbias


_WEIGHTS = (
    ("ffn1_gate", "row", FF_SHARD, FF_SHARD_PAD, D_MODEL),
    ("ffn1_up", "row", FF_SHARD, FF_SHARD_PAD, D_MODEL),
    ("ffn1_down", "row", FF_SHARD, FF_SHARD_PAD, D_MODEL),
    ("in", "col", QKV_SHARD, QKV_SHARD, D_MODEL),
    ("out", "row", ROW_SHARD, ROW_SHARD, D_MODEL),
    ("ffn2_gate", "row", FF_SHARD, FF_SHARD_PAD, D_MODEL),
    ("ffn2_up", "row", FF_SHARD, FF_SHARD_PAD, D_MODEL),
    ("ffn2_down", "row", FF_SHARD, FF_SHARD_PAD, D_MODEL),
    ("ple_proj", "col", ROW_SHARD, ROW_SHARD, PLE_DIM),
    ("ple_gate", "row", ROW_SHARD, ROW_SHARD, D_MODEL),
)
_TRANSPOSED = ("ffn1_gate", "ffn1_up", "ffn2_gate", "ffn2_up")
_SPEC = {n: (kind, valid, pad, other) for n, kind, valid, pad, other in _WEIGHTS}
_GATHER_STAGES = (("ffn1_gate", "ffn1_up", "ffn1_down"), ("in", "out"),
                  ("ffn2_gate", "ffn2_up", "ffn2_down", "ple_proj", "ple_gate"))
_SCATTER_STAGES = (("ple_proj", "ple_gate"), ("ffn2_gate", "ffn2_up", "ffn2_down"),
                   ("out", "in"), ("ffn1_gate", "ffn1_up", "ffn1_down"))
HBM = pl.BlockSpec(memory_space=pltpu.HBM)
SEM = pl.BlockSpec(memory_space=pltpu.SEMAPHORE)
EFFECT = pltpu.SideEffectType.DATAFLOW_SIDE_EFFECTING


def _shard_shape(kind, size, other):
    return (other, size) if kind == "col" else (size, other)


def _window(ref, kind, start, size):
    return ref.at[:, pl.ds(start, size)] if kind == "col" else ref.at[pl.ds(start, size), :]


def _device_tuple(k):
    return (k // 4, (k // 2) % 2, k % 2)


def _my_index():
    return 4 * lax.axis_index("x") + 2 * lax.axis_index("y") + lax.axis_index("c")


def _pack_weights(shards):
    nw = len(_WEIGHTS)

    def body(*refs):
        ins, packed, full = refs[:nw], refs[nw:2 * nw], refs[2 * nw:3 * nw]
        sem = refs[3 * nw]
        me = _my_index()
        for (_, kind, valid, pad, _), src, dst in zip(_WEIGHTS, ins, packed):
            if pad != valid:
                dst[...] = jnp.zeros_like(dst)
            if kind == "col":
                dst[:, pl.ds(0, valid)] = _bf(src[...])
            else:
                dst[pl.ds(0, valid), :] = _bf(src[...])
        for k in range(N_DEV):
            @pl.when(me == k)
            def _():
                for w, (_, kind, _, pad, _) in enumerate(_WEIGHTS):
                    pltpu.make_async_copy(packed[w], _window(full[w], kind, k * pad, pad),
                                          sem.at[w]).start()
        for w, (_, kind, _, pad, _) in enumerate(_WEIGHTS):
            pltpu.make_async_copy(packed[w], _window(full[w], kind, 0, pad), sem.at[w]).wait()

    whole = lambda shape: pl.BlockSpec(shape, lambda i: (0, 0))
    packed_shapes = [_shard_shape(kind, pad, other) for _, kind, _, pad, other in _WEIGHTS]
    outs = pl.pallas_call(
        body, name="pack_weights", grid=(1,),
        in_specs=[whole(a.shape) for a in shards],
        out_specs=[whole(s) for s in packed_shapes] + [ANY] * nw,
        out_shape=[_sds(s, BF16) for s in packed_shapes]
        + [_sds(_shard_shape(kind, N_DEV * pad, other), BF16) for _, kind, _, pad, other in _WEIGHTS],
        scratch_shapes=[pltpu.SemaphoreType.DMA((nw,))],
        compiler_params=_params("arbitrary"),
    )(*shards)
    names = [n for n, *_ in _WEIGHTS]
    return dict(zip(names, outs[:nw])), dict(zip(names, outs[nw:]))


def _hbm(a):
    return pltpu.with_memory_space_constraint(a, pltpu.HBM)


def _split_start(name, n, body_copies, sources, lands, after):
    arrays = list(sources) + list(lands)
    ns, na = len(sources), len(arrays)

    def body(*refs):
        src, land = refs[:ns], refs[ns:na]
        send, recv = refs[na + 1], refs[na + 2]
        token = refs[-1]
        body_copies(src, land, send, recv)
        token[...] = jnp.zeros_like(token)

    out = pl.pallas_call(
        body, name=name,
        out_shape=(pltpu.SemaphoreType.DMA((n,)), pltpu.SemaphoreType.DMA((n,)),
                   *[pltpu.HBM(a.shape, a.dtype) for a in arrays], _sds((8, 128))),
        in_specs=[HBM] * na + [ANY], out_specs=(SEM, SEM, *[HBM] * na, VMEM),
        input_output_aliases={i: 2 + i for i in range(na)},
        compiler_params=pltpu.CompilerParams(has_side_effects=EFFECT),
    )(*[_hbm(a) for a in arrays], after)
    return out[0], out[1], out[2:2 + ns], out[2 + ns:2 + na], out[-1]


def _split_wait(name, n, seven_of, send, recv, sources, lands, after, keep_sources=False):
    arrays = list(sources) + list(lands)
    ns, na = len(sources), len(arrays)

    def body(*refs):
        land = refs[ns:na]
        send_ref, recv_ref = refs[na], refs[na + 1]
        myself = (lax.axis_index("x"), lax.axis_index("y"), lax.axis_index("c"))
        for w in range(n):
            seven = seven_of(w, land[w])
            copy = pltpu.make_async_remote_copy(
                src_ref=seven, dst_ref=seven, send_sem=send_ref.at[w], recv_sem=recv_ref.at[w],
                device_id=myself, device_id_type=MESH)
            copy.wait_send()
            copy.wait_recv()

    out = pl.pallas_call(
        body, name=name,
        out_shape=[pltpu.HBM(a.shape, a.dtype) for a in arrays],
        in_specs=[HBM] * na + [SEM, SEM, ANY], out_specs=[HBM] * na,
        input_output_aliases={i: i for i in range(na)},
        compiler_params=pltpu.CompilerParams(has_side_effects=EFFECT),
    )(*arrays, send, recv, after)
    return out if keep_sources else out[ns:]


_ALL_PEERS = (1, 2, 3, 4, 5, 6, 7)
_NEAR_PEERS = (1, 2, 4, 6)
_FAR_CHIPS = (2, 4, 6)


def _gather_start(stage, names, packed, full, after, peers=_ALL_PEERS):
    def copies(src, land, send, recv):
        me = _my_index()
        for k in range(N_DEV):
            @pl.when(me == k)
            def _():
                for w, name in enumerate(names):
                    kind, _, pad, _ = _SPEC[name]
                    dst = _window(land[w], kind, k * pad, pad)
                    for mask in peers:
                        pltpu.make_async_remote_copy(
                            src_ref=src[w], dst_ref=dst, send_sem=send.at[w],
                            recv_sem=recv.at[w], device_id=_device_tuple(k ^ mask),
                            device_id_type=MESH).start()

    return _split_start(f"gather_start{stage}", len(names), copies,
                        [packed[n] for n in names], [full[n] for n in names], after)


def _gather_wait(stage, names, started, after, count=N_DEV - 1):
    send, recv, src, land, _ = started

    def bytes_of(w, ref):
        kind, _, pad, _ = _SPEC[names[w]]
        return _window(ref, kind, 0, count * pad)

    return dict(zip(names, _split_wait(f"gather_wait{stage}", len(names), bytes_of,
                                       send, recv, src, land, after)))


def _relay_start(stage, names, full, after):
    def copies(_, land, send, recv):
        me = _my_index()
        for k in range(N_DEV):
            @pl.when(me == k)
            def _():
                for w, name in enumerate(names):
                    kind, _, pad, _ = _SPEC[name]
                    for mask in _FAR_CHIPS:
                        win = _window(land[w], kind, (k ^ mask) * pad, pad)
                        pltpu.make_async_remote_copy(
                            src_ref=win, dst_ref=win, send_sem=send.at[w], recv_sem=recv.at[w],
                            device_id=_device_tuple(k ^ 1), device_id_type=MESH).start()

    return _split_start(f"relay_start{stage}", len(names), copies, [],
                        [full[n] for n in names], after)


def _scatter_start(stage, names, grads, after):
    def copies(src, land, send, recv):
        me = _my_index()
        for k in range(N_DEV):
            @pl.when(me != k)
            def _():
                slot = lax.rem(me + (N_DEV - 1 - k), N_DEV)
                for w, name in enumerate(names):
                    kind, _, pad, _ = _SPEC[name]
                    pltpu.make_async_remote_copy(
                        src_ref=_window(src[w], kind, k * pad, pad), dst_ref=land[w].at[slot],
                        send_sem=send.at[w], recv_sem=recv.at[w],
                        device_id=_device_tuple(k), device_id_type=MESH).start()

    lands = [lax.empty((N_DEV - 1,) + _shard_shape(_SPEC[m][0], _SPEC[m][2], _SPEC[m][3]), BF16)
             for m in names]
    return _split_start(f"scatter_start{stage}", len(names), copies, grads, lands, after)


def _scatter_wait(stage, names, started, after):
    send, recv, src, land, _ = started
    n = len(names)
    out = _split_wait(f"scatter_wait{stage}", n, lambda w, ref: ref, send, recv, src, land, after,
                      keep_sources=True)
    return dict(zip(names, out[:n])), dict(zip(names, out[n:]))


N_CHIPS = N_DEV // 2


def _pair_start(stage, names, grads, after):
    def copies(src, land, send, recv):
        me = _my_index()
        for k in range(N_DEV):
            @pl.when(me == k)
            def _():
                for w, name in enumerate(names):
                    kind, _, pad, _ = _SPEC[name]
                    for chip in range(N_CHIPS):
                        j = 2 * chip + ((k ^ 1) & 1)
                        pltpu.make_async_remote_copy(
                            src_ref=_window(src[w], kind, j * pad, pad), dst_ref=land[w].at[chip],
                            send_sem=send.at[w], recv_sem=recv.at[w],
                            device_id=_device_tuple(k ^ 1), device_id_type=MESH).start()

    lands = [lax.empty((N_CHIPS,) + _shard_shape(_SPEC[m][0], _SPEC[m][2], _SPEC[m][3]), BF16)
             for m in names]
    return _split_start(f"pair_start{stage}", len(names), copies, grads, lands, after)


def _pair_sum(dw_full, pair, *, pad, name):
    other = dw_full.shape[1]

    def body(own_ref, pair_ref, out_ref):
        out_ref[0] = _bf(own_ref[...].astype(F32) + pair_ref[0].astype(F32))

    slot = pl.BlockSpec((1, pad, other), lambda q: (q, 0, 0))
    return pl.pallas_call(
        body, name=name, grid=(N_CHIPS,),
        in_specs=[pl.BlockSpec((pad, other), lambda q: (2 * q + lax.axis_index("c"), 0)), slot],
        out_specs=slot, out_shape=_sds((N_CHIPS, pad, other), BF16),
        compiler_params=_params("arbitrary"),
    )(dw_full, pair)


def _chip_start(stage, names, sums, after):
    def copies(src, land, send, recv):
        me = _my_index()
        my_chip = lax.shift_right_logical(me, 1)
        for k in range(N_DEV):
            @pl.when((me != k) & (((me ^ k) & 1) == 0))
            def _():
                slot = lax.rem(my_chip + (N_CHIPS - 1 - k // 2), N_CHIPS)
                for w in range(len(names)):
                    pltpu.make_async_remote_copy(
                        src_ref=src[w].at[k // 2], dst_ref=land[w].at[slot],
                        send_sem=send.at[w], recv_sem=recv.at[w],
                        device_id=_device_tuple(k), device_id_type=MESH).start()

    lands = [lax.empty((N_CHIPS - 1,) + a.shape[1:], BF16) for a in sums]
    return _split_start(f"chip_start{stage}", len(names), copies, sums, lands, after)


def _adamw_chip(w, m, v, land, sums, *, name):
    shape = w.shape

    def body(w_ref, m_ref, v_ref, land_ref, own_ref, *outs):
        rows = pl.ds(0, shape[0])
        grad = own_ref[0, rows, :].astype(F32)
        for s in range(N_CHIPS - 1):
            grad = grad + land_ref[s, rows, :].astype(F32)
        _adam_update(w_ref, m_ref, v_ref, grad, *outs)

    whole = lambda a: pl.BlockSpec(a.shape, lambda i: (0,) * a.ndim)
    own = pl.BlockSpec((1,) + sums.shape[1:],
                       lambda i: (2 * lax.axis_index("x") + lax.axis_index("y"), 0, 0))
    return pl.pallas_call(
        body, name=name, grid=(1,),
        in_specs=[whole(w), whole(m), whole(v), whole(land), own],
        out_specs=[whole(w)] * 4, out_shape=[_sds(shape)] * 4,
        compiler_params=_params("arbitrary"),
    )(w, m, v, land, sums)


def _allreduce_small(small, after):
    shape = small.shape

    def body(in_ref, _after, out_ref, gath, send, recv):
        me = _my_index()
        for k in range(N_DEV):
            @pl.when(me != k)
            def _():
                pltpu.make_async_remote_copy(
                    src_ref=in_ref, dst_ref=gath.at[me], send_sem=send, recv_sem=recv,
                    device_id=_device_tuple(k), device_id_type=MESH).start()

            @pl.when(me == k)
            def _():
                gath[k] = in_ref[...]
        seven = gath.at[pl.ds(0, N_DEV - 1)]
        pltpu.make_async_remote_copy(
            src_ref=seven, dst_ref=seven, send_sem=send, recv_sem=recv,
            device_id=_device_tuple(0), device_id_type=MESH).wait()
        total = gath[0]
        for s in range(1, N_DEV):
            total = total + gath[s]
        out_ref[...] = total

    return pl.pallas_call(
        body, name="allreduce_small",
        in_specs=[VMEM, ANY], out_specs=VMEM, out_shape=_sds(shape),
        scratch_shapes=[pltpu.VMEM((N_DEV,) + shape, F32),
                        pltpu.SemaphoreType.DMA, pltpu.SemaphoreType.DMA],
    )(small, after)


def _adam_update(w_ref, m_ref, v_ref, grad, grad_ref, delta_ref, nm_ref, nv_ref):
    new_m = ADAM_B1 * m_ref[...] + (1.0 - ADAM_B1) * grad
    new_v = ADAM_B2 * v_ref[...] + (1.0 - ADAM_B2) * (grad * grad)
    m_hat = new_m / (1.0 - ADAM_B1 ** ADAM_STEP)
    v_hat = new_v / (1.0 - ADAM_B2 ** ADAM_STEP)
    grad_ref[...] = grad
    delta_ref[...] = -ADAM_LR * (m_hat / (jnp.sqrt(v_hat) + ADAM_EPS) + ADAM_WD * w_ref[...])
    nm_ref[...] = new_m
    nv_ref[...] = new_v


def _adamw(w, m, v, g, *, name):
    def body(w_ref, m_ref, v_ref, g_ref, *outs):
        _adam_update(w_ref, m_ref, v_ref, g_ref[...], *outs)

    whole = pl.BlockSpec(w.shape, lambda i: (0,) * w.ndim)
    return pl.pallas_call(
        body, name=name, grid=(1,), in_specs=[whole] * 4, out_specs=[whole] * 4,
        out_shape=[_sds(w.shape)] * 4, compiler_params=_params("arbitrary"),
    )(w, m, v, g)


def _adamw_gains(small, params):
    n = len(params)

    def body(small_ref, *refs):
        ins, outs = refs[:3 * n], refs[3 * n:]
        for r in range(n):
            width = ins[3 * r].shape[1]
            if width == D_MODEL:
                grad = small_ref[pl.ds(r, 1), :]
            else:
                grad = small_ref[pl.ds(len(_GAINS), 1), pl.ds((r - len(_GAINS)) * width, width)]
            _adam_update(*ins[3 * r:3 * r + 3], grad, *outs[4 * r:4 * r + 4])

    whole = lambda a: pl.BlockSpec(a.shape, lambda i: (0, 0))
    flat = [a for group in params for a in group]
    return pl.pallas_call(
        body, name="adamw_gains", grid=(1,),
        in_specs=[whole(small)] + [whole(a) for a in flat],
        out_specs=[whole(w) for w, _, _ in params for _ in range(4)],
        out_shape=[_sds(w.shape) for w, _, _ in params for _ in range(4)],
        compiler_params=_params("arbitrary"),
    )(small, *flat)


def _adamw_shard(w, m, v, land, dw_full, *, kind, pad, name):
    shape = w.shape
    other = shape[0] if kind == "col" else shape[1]

    def body(w_ref, m_ref, v_ref, land_ref, own_ref, *outs):
        valid = ((slice(None), pl.ds(0, shape[1])) if kind == "col"
                 else (pl.ds(0, shape[0]), slice(None)))
        grad = own_ref[valid].astype(F32)
        for s in range(N_DEV - 1):
            grad = grad + land_ref[(s,) + valid].astype(F32)
        _adam_update(w_ref, m_ref, v_ref, grad, *outs)

    whole = lambda a: pl.BlockSpec(a.shape, lambda i: (0,) * a.ndim)
    own = pl.BlockSpec(_shard_shape(kind, pad, other),
                       (lambda i: (0, _my_index())) if kind == "col" else (lambda i: (_my_index(), 0)))
    return pl.pallas_call(
        body, name=name, grid=(1,),
        in_specs=[whole(w), whole(m), whole(v), whole(land), own],
        out_specs=[whole(w)] * 4, out_shape=[_sds(shape)] * 4,
        compiler_params=_params("arbitrary"),
    )(w, m, v, land, dw_full)


_GAINS = ("ffn1_pre", "ffn1_post", "mix_pre", "mix_post", "ffn2_pre", "ffn2_post", "ple_post")
_SMALL_ROWS = 16


def _stack_gains(get):
    return jnp.concatenate([get(n) for n in _GAINS]
                           + [jnp.concatenate([get("out_sb"), get("out_ch")], axis=1)], axis=0)


def kernel(x, p, g_ffn1_pre, g_ffn1_post, w_ffn1_gate, w_ffn1_up, w_ffn1_down, g_mix_pre, g_mix_post, w_in, g_out_sb, g_out_ch, rel_bias, w_out, g_ffn2_pre, g_ffn2_post, w_ffn2_gate, w_ffn2_up, w_ffn2_down, w_ple_proj, w_ple_gate, g_ple_post, loss_target, m_g_ffn1_pre, m_g_ffn1_post, m_w_ffn1_gate, m_w_ffn1_up, m_w_ffn1_down, m_g_mix_pre, m_g_mix_post, m_w_in, m_g_out_sb, m_g_out_ch, m_rel_bias, m_w_out, m_g_ffn2_pre, m_g_ffn2_post, m_w_ffn2_gate, m_w_ffn2_up, m_w_ffn2_down, m_w_ple_proj, m_w_ple_gate, m_g_ple_post, v_g_ffn1_pre, v_g_ffn1_post, v_w_ffn1_gate, v_w_ffn1_up, v_w_ffn1_down, v_g_mix_pre, v_g_mix_post, v_w_in, v_g_out_sb, v_g_out_ch, v_rel_bias, v_w_out, v_g_ffn2_pre, v_g_ffn2_post, v_w_ffn2_gate, v_w_ffn2_up, v_w_ffn2_down, v_w_ple_proj, v_w_ple_gate, v_g_ple_post):
    given = dict(locals())
    wnames = [n for n, *_ in _WEIGHTS]

    def shard(prefix, n):
        a = given[prefix + "w_" + n][0]
        return a.T if n in _TRANSPOSED else a

    packed, full = _pack_weights([shard("", n) for n in wnames])
    first = _GATHER_STAGES[0]
    anchor = x[0]
    two_level = (0, 2)
    gathers = {}

    def start_stage(stage, after):
        peers = _NEAR_PEERS if stage in two_level else _ALL_PEERS
        gathers[stage] = _gather_start(stage, _GATHER_STAGES[stage], packed, full, after,
                                       peers=peers)

    start_stage(0, anchor)

    relays = {}

    def first_level(stage, after):
        names = _GATHER_STAGES[stage]
        last_stage = stage + 1 == len(_GATHER_STAGES)
        count = len(_NEAR_PEERS) if stage in two_level else N_DEV - 1
        ws = _gather_wait(stage, names, gathers[stage], after, count=count)
        if not last_stage:
            start_stage(stage + 1, ws[names[0]])
        if stage in two_level:
            relays[stage] = _relay_start(stage, names, ws,
                                         anchor if last_stage else gathers[stage + 1][-1])
            return ws, relays[stage][-1]
        return ws, None if last_stage else gathers[stage + 1][-1]

    def weights_early(stage, after):
        return first_level(stage, after)[1][:1, :1]

    def weights_for(stage, after):
        names = _GATHER_STAGES[stage]
        ws, token = (None, None) if stage in relays else first_level(stage, after)
        if stage in relays:
            relay = relays[stage]
            ws = _gather_wait(f"{stage}r", names, relay, after, count=len(_FAR_CHIPS))
            token = None if stage + 1 == len(_GATHER_STAGES) else gathers[stage + 1][-1]
        return ws, jnp.zeros((1, 1), F32) if token is None else token[:1, :1]

    scatters = {}

    last = len(_SCATTER_STAGES) - 1

    def grads_done(stage, grads):
        names = _SCATTER_STAGES[stage]
        start = _pair_start if stage == last else _scatter_start
        scatters[stage] = start(stage, names, [grads[n] for n in names], anchor)
        return scatters[stage][-1][:1, :1]

    gains = {n: given["g_" + n] for n in _GAINS + ("out_sb", "out_ch")}
    fvec = _rel_bias_to_fvec(rel_bias[0])
    loss, dx, dg, dbias = _local_step(x[0], p[0, 0], loss_target[0], gains,
                                      weights_for, grads_done, fvec, weights_early)

    results = {}

    def finish(stage, after):
        names = _SCATTER_STAGES[stage]
        dws, lands = _scatter_wait(stage, names, scatters[stage], after)
        for n in names:
            kind, _, pad, _ = _SPEC[n]
            out = _adamw_shard(shard("", n), shard("m_", n), shard("v_", n), lands[n], dws[n],
                               kind=kind, pad=pad, name="adamw_" + n)
            results["w_" + n] = [a.T for a in out] if n in _TRANSPOSED else out
        return results["w_" + names[-1]][0]

    names = _SCATTER_STAGES[last]
    whole = lambda w, ref: ref
    send, recv, src, land, _ = scatters[last]
    out = _split_wait(f"pair_wait{last}", len(names), whole, send, recv, src, land, dx,
                      keep_sources=True)
    sums = [_pair_sum(dwf, pair, pad=_SPEC[n][2], name="pair_sum_" + n)
            for n, dwf, pair in zip(names, out[:len(names)], out[len(names):])]
    send, recv, src, land, after = _chip_start(last, names, sums, anchor)
    for stage in range(last):
        after = finish(stage, after)
    dfvec = _bias_grad(dbias, after, name="bias_grad")
    out = _split_wait(f"chip_wait{last}", len(names), whole, send, recv, src, land, dfvec,
                      keep_sources=True)
    for n, own, landed in zip(names, out[:len(names)], out[len(names):]):
        res = _adamw_chip(shard("", n), shard("m_", n), shard("v_", n), landed, own,
                          name="adamw_" + n)
        results["w_" + n] = [a.T for a in res] if n in _TRANSPOSED else res
        after = res[0]
    loss_col = jnp.pad(loss[:, :1], ((0, N_DEV - 1), (0, D_MODEL - CH_WIN - 1)))
    dfv = jnp.concatenate([dfvec[:, 0, :], loss_col], axis=1)
    small = _allreduce_small(jnp.concatenate([_stack_gains(lambda n: dg[n]), dfv], axis=0), after)
    gain_names = _GAINS + ("out_sb", "out_ch")
    gain_out = _adamw_gains(small, [(given["g_" + n], given["m_g_" + n], given["v_g_" + n])
                                    for n in gain_names])
    for r, n in enumerate(gain_names):
        results["g_" + n] = gain_out[4 * r:4 * r + 4]
    d_rel = _fvec_grad_to_rel_bias(small[N_DEV:, :CH_WIN].reshape(N_DEV, 1, CH_WIN))
    results["rel_bias"] = _adamw(rel_bias[0], m_rel_bias[0], v_rel_bias[0], d_rel,
                                 name="adamw_rel_bias")

    order = ("g_ffn1_pre", "g_ffn1_post", "w_ffn1_gate", "w_ffn1_up", "w_ffn1_down",
             "g_mix_pre", "g_mix_post", "w_in", "g_out_sb", "g_out_ch", "rel_bias", "w_out",
             "g_ffn2_pre", "g_ffn2_post", "w_ffn2_gate", "w_ffn2_up", "w_ffn2_down",
             "w_ple_proj", "w_ple_gate", "g_ple_post")

    def leaf(name, idx):
        a = results[name][idx]
        return a if name.startswith("g_") else a[None]

    total_loss = small[N_DEV, CH_WIN]
    return (total_loss, dx[None],
            *[leaf(n, 0) for n in order], *[leaf(n, 1) for n in order],
            *[leaf(n, 2) for n in order], *[leaf(n, 3) for n in order])
```

```python
import jax
import jax.numpy as jnp
from jax import lax
from jax.experimental import pallas as pl
from jax.experimental.pallas import tpu as pltpu

F32 = jnp.float32
BF16 = jnp.bfloat16

N_DEV = 8
D_MODEL = 1024
D_FF = 2816
FF_SHARD = D_FF // N_DEV
FF_SHARD_PAD = 384
D_FF_PAD = FF_SHARD_PAD * N_DEV
QKV_WIDTH = 3 * D_MODEL
QKV_SHARD = QKV_WIDTH // N_DEV
PLE_DIM = 256
ROW_SHARD = D_MODEL // N_DEV
HEAD_DIM = 64
PAIR = 2 * HEAD_DIM
N_PAIRS = 4
CHUNK = 64
LOOKBACK = 8
REL_CLIP = 128
N_REL = 2 * REL_CLIP + 1
CH_QB = 256
CH_LOOK = LOOKBACK * CHUNK
CH_WIN = CH_LOOK + CH_QB
SB_QB = 512
SB_KB = 256
SB_GROUP = 2
SB_LANES = tuple(slice(g * 128, (g + 1) * 128) for g in range(SB_GROUP))
EPS = 1e-6
NEG_INF = -1e30
ATT_SCALE = HEAD_DIM ** -0.5
ADAM_LR = 0.001
ADAM_B1 = 0.9
ADAM_B2 = 0.999
ADAM_EPS = 1e-08
ADAM_WD = 0.01
ADAM_STEP = 10
VMEM_LIMIT_BYTES = 48 * 1024 * 1024
MESH = pl.DeviceIdType.MESH

ANY = pl.BlockSpec(memory_space=pl.ANY)
VMEM = pl.BlockSpec(memory_space=pltpu.VMEM)


def _params(*sem):
    return pltpu.CompilerParams(dimension_semantics=sem or None,
                                vmem_limit_bytes=VMEM_LIMIT_BYTES)


def _sds(shape, dtype=F32):
    return jax.ShapeDtypeStruct(shape, dtype)


def _bf(x):
    return x.astype(BF16)


def _dot(a, b):
    return jnp.dot(_bf(a), _bf(b), preferred_element_type=F32)


def _dot_nt(a, b):
    return lax.dot_general(_bf(a), _bf(b), (((1,), (1,)), ((), ())),
                           preferred_element_type=F32)


def _dot_tn(a, b):
    return lax.dot_general(_bf(a), _bf(b), (((0,), (0,)), ((), ())),
                           preferred_element_type=F32)


def _sigmoid(x):
    return 1.0 / (1.0 + jnp.exp(-x))


def _softplus(x):
    return jnp.maximum(x, 0.0) + jnp.log(1.0 + jnp.exp(-jnp.abs(x)))


def _rstd(x):
    return lax.rsqrt(jnp.mean(x * x, axis=-1, keepdims=True) + EPS)


def _rms(x, g):
    return x * _rstd(x) * g


def _rms_bwd(dy, x, g):
    r = _rstd(x)
    w = dy * g
    dx = r * (w - x * (r * r) * jnp.mean(w * x, axis=-1, keepdims=True))
    dg = jnp.sum(dy * (x * r), axis=0, keepdims=True)
    return dx, dg


def _head_masks():
    lane = lax.broadcasted_iota(jnp.int32, (1, PAIR), 1)
    return lane < HEAD_DIM, lane >= HEAD_DIM


def _ffn_fwd(x, g_pre, g_post, wg, wu, wd, *, name):
    t = x.shape[0]
    tm, tj = 512, 1024
    ni, nj = t // tm, D_FF_PAD // tj

    def body(x_ref, gpre_ref, gpost_ref, wg_ref, wu_ref, wd_ref,
             h_ref, n_ref, a_ref, b_ref, f_ref, acc_ref):
        j = pl.program_id(1)

        @pl.when(j == 0)
        def _():
            n_ref[...] = _bf(_rms(x_ref[...], gpre_ref[...]))
            acc_ref[...] = jnp.zeros_like(acc_ref)

        n = n_ref[...]
        a = _dot_nt(n, wg_ref[...])
        b = _dot_nt(n, wu_ref[...])
        a_ref[...] = a
        b_ref[...] = b
        hmid = a * _sigmoid(a) * b
        acc_ref[...] += jnp.dot(_bf(hmid), wd_ref[...], preferred_element_type=F32)

        @pl.when(j == nj - 1)
        def _():
            f = acc_ref[...]
            f_ref[...] = f
            h_ref[...] = x_ref[...] + 0.5 * _rms(f, gpost_ref[...])

    row = pl.BlockSpec((tm, D_MODEL), lambda i, j: (i, 0))
    gain = pl.BlockSpec((1, D_MODEL), lambda i, j: (0, 0))
    col = pl.BlockSpec((tm, tj), lambda i, j: (i, j))
    wtile = pl.BlockSpec((tj, D_MODEL), lambda i, j: (j, 0))
    return pl.pallas_call(
        body, name=name, grid=(ni, nj),
        in_specs=[row, gain, gain, wtile, wtile, wtile],
        out_specs=[row, row, col, col, row],
        out_shape=[_sds((t, D_MODEL)), _sds((t, D_MODEL), BF16),
                   _sds((t, D_FF_PAD)), _sds((t, D_FF_PAD)), _sds((t, D_MODEL))],
        scratch_shapes=[pltpu.VMEM((tm, D_MODEL), F32)],
        compiler_params=_params("arbitrary", "arbitrary"),
    )(x, g_pre, g_post, wg, wu, wd)


def _ffn_bwd(n, df, a, b, wg, wu, wd, *, name):
    t = n.shape[0]
    tj, tm, ts = 256, t, 512
    nj, ni, ns = D_FF_PAD // tj, t // tm, tm // ts

    def body(n_hbm, df_hbm, a_ref, b_ref, wg_ref, wu_ref, wd_ref,
             dwg_ref, dwu_ref, dwd_ref, dn_hbm,
             n_v, df_v, dn_v, ag, au, ad, sem):
        j, i = pl.program_id(0), pl.program_id(1)

        @pl.when((j == 0) & (i == 0))
        def _():
            c1 = pltpu.make_async_copy(n_hbm, n_v, sem.at[0])
            c2 = pltpu.make_async_copy(df_hbm, df_v, sem.at[1])
            c1.start()
            c2.start()
            dn_v[...] = jnp.zeros_like(dn_v)
            c1.wait()
            c2.wait()

        @pl.when(i == 0)
        def _():
            ag[...] = jnp.zeros_like(ag)
            au[...] = jnp.zeros_like(au)
            ad[...] = jnp.zeros_like(ad)

        wgj, wuj, wdj = wg_ref[...], wu_ref[...], wd_ref[...]
        for s in range(ns):
            local = pl.ds(s * ts, ts)
            rows = pl.ds(pl.multiple_of(i * tm + s * ts, ts), ts)
            av, bv = a_ref[local, :], b_ref[local, :]
            sig = _sigmoid(av)
            silu = av * sig
            dfr = df_v[rows, :]
            nr = n_v[rows, :]
            dhmid = _dot_nt(dfr, wdj)
            da = dhmid * bv * (sig * (1.0 + av * (1.0 - sig)))
            db = dhmid * silu
            ad[...] += _dot_tn(silu * bv, dfr)
            ag[...] += _dot_tn(da, nr)
            au[...] += _dot_tn(db, nr)
            dn_v[rows, :] += _dot(da, wgj) + _dot(db, wuj)

        @pl.when(i == ni - 1)
        def _():
            dwg_ref[...] = _bf(ag[...])
            dwu_ref[...] = _bf(au[...])
            dwd_ref[...] = _bf(ad[...])

        @pl.when((j == nj - 1) & (i == ni - 1))
        def _():
            c = pltpu.make_async_copy(dn_v, dn_hbm, sem.at[0])
            c.start()
            c.wait()

    roww = pl.BlockSpec((tj, D_MODEL), lambda j, i: (j, 0))
    act = pl.BlockSpec((tm, tj), lambda j, i: (i, j))
    return pl.pallas_call(
        body, name=name, grid=(nj, ni),
        in_specs=[ANY, ANY, act, act, roww, roww, roww],
        out_specs=[roww, roww, roww, ANY],
        out_shape=[_sds((D_FF_PAD, D_MODEL), BF16)] * 3 + [_sds((t, D_MODEL))],
        scratch_shapes=[pltpu.VMEM((t, D_MODEL), BF16), pltpu.VMEM((t, D_MODEL), BF16),
                        pltpu.VMEM((t, D_MODEL), F32)]
        + [pltpu.VMEM((tj, D_MODEL), F32)] * 3 + [pltpu.SemaphoreType.DMA((2,))],
        compiler_params=_params("arbitrary", "arbitrary"),
    )(n, df, a, b, wg, wu, wd)


def _junction(dres, pre=None, post=None, *, name):
    t = dres.shape[0]
    tm = 512
    ni = t // tm
    n_in = 1 + (3 if pre else 0) + (2 if post else 0)
    coef = post[2] if post else None

    def body(*refs):
        ins, outs = list(refs[:n_in]), list(refs[n_in:])
        i = pl.program_id(0)
        dh = ins.pop(0)[...]
        if pre:
            dn_ref, x_ref, gpre_ref = ins.pop(0), ins.pop(0), ins.pop(0)
            dh_ref, dgpre_ref = outs.pop(0), outs.pop(0)
            dx, dg = _rms_bwd(dn_ref[...], x_ref[...], gpre_ref[...])
            dh = dh + dx
            dh_ref[...] = dh

            @pl.when(i == 0)
            def _():
                dgpre_ref[...] = jnp.zeros_like(dgpre_ref)
            dgpre_ref[...] += dg
        if post:
            f_ref, gpost_ref = ins.pop(0), ins.pop(0)
            df_ref, dgpost_ref = outs.pop(0), outs.pop(0)
            df, dg = _rms_bwd(coef * dh, f_ref[...], gpost_ref[...])
            df_ref[...] = _bf(df)

            @pl.when(i == 0)
            def _():
                dgpost_ref[...] = jnp.zeros_like(dgpost_ref)
            dgpost_ref[...] += dg

    row = pl.BlockSpec((tm, D_MODEL), lambda i: (i, 0))
    gain = pl.BlockSpec((1, D_MODEL), lambda i: (0, 0))
    args, in_specs, out_specs, out_shape = [dres], [row], [], []
    if pre:
        args += list(pre)
        in_specs += [row, row, gain]
        out_specs += [row, gain]
        out_shape += [_sds((t, D_MODEL)), _sds((1, D_MODEL))]
    if post:
        args += [post[0], post[1]]
        in_specs += [row, gain]
        out_specs += [row, gain]
        out_shape += [_sds((t, D_MODEL), BF16), _sds((1, D_MODEL))]
    return pl.pallas_call(
        body, name=name, grid=(ni,), in_specs=in_specs, out_specs=out_specs,
        out_shape=out_shape, compiler_params=_params("arbitrary"),
    )(*args)


def _qkv_fwd(h, g, win, *, name):
    t = h.shape[0]
    tm, tn = min(1024, t), 1024
    ni, nj = t // tm, QKV_WIDTH // tn

    def body(h_ref, g_ref, w_ref, qkv_ref, u_ref):
        @pl.when(pl.program_id(1) == 0)
        def _():
            u_ref[...] = _bf(_rms(h_ref[...], g_ref[...]))
        qkv_ref[...] = jnp.dot(u_ref[...], w_ref[...], preferred_element_type=F32)

    row = pl.BlockSpec((tm, D_MODEL), lambda i, j: (i, 0))
    return pl.pallas_call(
        body, name=name, grid=(ni, nj),
        in_specs=[row, pl.BlockSpec((1, D_MODEL), lambda i, j: (0, 0)),
                  pl.BlockSpec((D_MODEL, tn), lambda i, j: (0, j))],
        out_specs=[pl.BlockSpec((tm, tn), lambda i, j: (i, j)), row],
        out_shape=[_sds((t, QKV_WIDTH)), _sds((t, D_MODEL), BF16)],
        compiler_params=_params("arbitrary", "arbitrary"),
    )(h, g, win)


def _qkv_bwd(dq, dk, dv, u, win, *, name):
    t = u.shape[0]
    tn, ts = 512, 512
    nj, ns = QKV_WIDTH // tn, t // ts

    def body(dq_ref, dk_ref, dv_ref, u_ref, w_ref, dw_ref, du_hbm, du_v, acc_ref, sem):
        j = pl.program_id(0)

        @pl.when(j == 0)
        def _():
            du_v[...] = jnp.zeros_like(du_v)

        wj = w_ref[...]
        for role, d_ref in enumerate((dq_ref, dk_ref, dv_ref)):
            @pl.when(j % 3 == role)
            def _():
                acc_ref[...] = jnp.zeros_like(acc_ref)
                for s in range(ns):
                    rows = pl.ds(s * ts, ts)
                    dcol = d_ref[rows, :]
                    acc_ref[...] += _dot_tn(u_ref[rows, :], dcol)
                    du_v[rows, :] += _dot_nt(dcol, wj)
                dw_ref[...] = _bf(acc_ref[...])

        @pl.when(j == nj - 1)
        def _():
            c = pltpu.make_async_copy(du_v, du_hbm, sem)
            c.start()
            c.wait()

    colw = pl.BlockSpec((D_MODEL, tn), lambda j: (0, j))
    grp = pl.BlockSpec((t, tn), lambda j: (0, j // 3))
    return pl.pallas_call(
        body, name=name, grid=(nj,),
        in_specs=[grp, grp, grp, pl.BlockSpec((t, D_MODEL), lambda j: (0, 0)), colw],
        out_specs=[colw, ANY],
        out_shape=[_sds((D_MODEL, QKV_WIDTH), BF16), _sds((t, D_MODEL))],
        scratch_shapes=[pltpu.VMEM((t, D_MODEL), F32), pltpu.VMEM((D_MODEL, tn), F32),
                        pltpu.SemaphoreType.DMA],
        compiler_params=_params("arbitrary"),
    )(dq, dk, dv, u, win)


def _sb_stack(x):
    lo, hi = _head_masks()
    return jnp.concatenate([jnp.where(lo, x, 0.0), jnp.where(hi, x, 0.0)], axis=0)


def _sb_unstack(x2, blk):
    return jnp.where(_head_masks()[0], x2[:blk], x2[blk:])


def _sb_rows_from(x2, blk, r0):
    return x2 if r0 == 0 else jnp.concatenate([x2[r0:blk], x2[blk + r0:]], axis=0)


def _sb_rows_merge(full2, sub2, blk, r0):
    if r0 == 0:
        return sub2
    rows = blk - r0
    return jnp.concatenate([full2[:r0], sub2[:rows], full2[blk:blk + r0], sub2[rows:]], axis=0)


def _sb_mask(qb, kb, offset):
    r = lax.broadcasted_iota(jnp.int32, (2 * qb, kb), 0) & (qb - 1)
    c = lax.broadcasted_iota(jnp.int32, (2 * qb, kb), 1) + offset
    return c < r


def _tri(n, keep):
    r = lax.broadcasted_iota(jnp.int32, (n, n), 0)
    c = lax.broadcasted_iota(jnp.int32, (n, n), 1)
    return jnp.where(keep(r, c), 1.0, 0.0).astype(BF16)


def _cumsum01(x, u):
    m = x.shape[0]
    hi = _bf(x)
    lo = _bf(x - hi.astype(F32))
    both = jnp.dot(jnp.concatenate([hi, lo], axis=0), u, preferred_element_type=F32)
    return both[:m] + both[m:]


def _sb_fwd(qkv, *, name):
    t = qkv.shape[0]
    blk, kb = min(SB_QB, t), SB_KB
    ni, per = t // blk, blk // kb

    def body(q_ref, k_ref, v_ref, o_ref, ltot_ref):
        i = pl.program_id(1)
        u_after = _tri(kb, lambda r, c: r > c)
        q2 = [_bf(_sb_stack(q_ref[:, lanes] * ATT_SCALE)) for lanes in SB_LANES]

        def tile(g, k0, mask, acc, c_l):
            kj = k_ref[pl.ds(k0, kb), SB_LANES[g]]
            vj = v_ref[pl.ds(k0, kb), SB_LANES[g]]
            z = _dot_nt(q2[g], kj)
            sp = _softplus(z)
            lf = -sp if mask is None else jnp.where(mask, -sp, 0.0)
            a = jnp.exp(z - sp + _cumsum01(lf, u_after) + c_l)
            if mask is not None:
                a = jnp.where(mask, a, 0.0)
            return acc + _dot(a, vj), c_l + jnp.sum(lf, axis=1, keepdims=True)

        def tiles(k0, mask, carry):
            return tuple(tile(g, k0, mask, *carry[g]) for g in range(SB_GROUP))

        carry = ((jnp.zeros((2 * blk, PAIR), F32), jnp.zeros((2 * blk, 1), F32)),) * SB_GROUP
        for d in reversed(range(per)):
            carry = tiles(pl.multiple_of(i * blk + d * kb, kb), _sb_mask(blk, kb, d * kb), carry)
        carry = lax.fori_loop(
            1, per * i + 1,
            lambda jj, c: tiles(pl.multiple_of((per * i - jj) * kb, kb), None, c), carry)
        for g, (acc, c_l) in enumerate(carry):
            o_ref[:, SB_LANES[g]] = _sb_unstack(acc, blk)
            ltot_ref[:, SB_LANES[g]] = _sb_unstack(jnp.broadcast_to(c_l, (2 * blk, PAIR)), blk)

    width = SB_GROUP * PAIR
    blkspec = pl.BlockSpec((blk, width), lambda p, i: (i, p))
    n_steps = N_PAIRS // SB_GROUP
    return pl.pallas_call(
        body, name=name, grid=(n_steps, ni),
        in_specs=[blkspec,
                  pl.BlockSpec((t, width), lambda p, i: (0, n_steps + p)),
                  pl.BlockSpec((t, width), lambda p, i: (0, 2 * n_steps + p))],
        out_specs=[blkspec, blkspec],
        out_shape=[_sds((t, D_MODEL)), _sds((t, D_MODEL // 2))],
        compiler_params=_params("arbitrary", "arbitrary"),
    )(qkv, qkv, qkv)


def _sb_bwd(qkv, ltot, do, *, name):
    t = qkv.shape[0]
    blk, kb = min(SB_QB, t), SB_KB
    ni, per = t // blk, blk // kb

    def body(q_ref, k_ref, v_ref, lt_ref, do_ref, dq_ref, dkout_ref, dvout_ref, dk_ref, dv_ref):
        i = pl.program_id(1)

        @pl.when(i == 0)
        def _():
            dk_ref[...] = jnp.zeros_like(dk_ref)
            dv_ref[...] = jnp.zeros_like(dv_ref)

        u_upto = _tri(kb, lambda r, c: r <= c)
        u_before = _tri(kb, lambda r, c: r < c)
        lane = lax.broadcasted_iota(jnp.int32, (1, PAIR), 1)
        q2 = [_bf(_sb_stack(q_ref[:, lanes] * ATT_SCALE)) for lanes in SB_LANES]
        do2 = [_bf(_sb_stack(do_ref[:, lanes])) for lanes in SB_LANES]
        total = [jnp.concatenate(
            [jnp.sum(jnp.where(lane == h * HEAD_DIM, lt_ref[:, lanes], 0.0), axis=1, keepdims=True)
             for h in range(2)], axis=0) for lanes in SB_LANES]

        def tile(g, ops, k0, mask, dq_acc, c_l, c_g):
            qg, dog, tot = ops
            krows = pl.ds(k0, kb)
            kj = k_ref[krows, SB_LANES[g]]
            vj = v_ref[krows, SB_LANES[g]]
            z = _dot_nt(qg, kj)
            sp = _softplus(z)
            sig = jnp.exp(z - sp)
            lf = -sp if mask is None else jnp.where(mask, -sp, 0.0)
            a = jnp.exp(z - sp + tot - (_cumsum01(lf, u_upto) + c_l))
            if mask is not None:
                a = jnp.where(mask, a, 0.0)
            gw = a * _dot_nt(dog, vj)
            g_before = jnp.dot(_bf(gw), u_before, preferred_element_type=F32) + c_g
            dz = gw * (1.0 - sig) - g_before * sig
            if mask is not None:
                dz = jnp.where(mask, dz, 0.0)
            dk_ref[krows, SB_LANES[g]] += _dot_tn(dz, qg)
            dv_ref[krows, SB_LANES[g]] += _dot_tn(a, dog)
            return (dq_acc + _dot(dz, kj), c_l + jnp.sum(lf, axis=1, keepdims=True),
                    c_g + jnp.sum(gw, axis=1, keepdims=True))

        def tiles(ops, k0, mask, carry):
            return tuple(tile(g, ops[g], k0, mask, *carry[g]) for g in range(SB_GROUP))

        ops = tuple(zip(q2, do2, total))
        zero = (jnp.zeros((2 * blk, PAIR), F32), jnp.zeros((2 * blk, 1), F32),
                jnp.zeros((2 * blk, 1), F32))
        carry = lax.fori_loop(
            0, per * i, lambda j, c: tiles(ops, pl.multiple_of(j * kb, kb), None, c),
            (zero,) * SB_GROUP)
        for d in range(per):
            r0 = d * kb
            sub = tiles(tuple(tuple(_sb_rows_from(a, blk, r0) for a in o) for o in ops),
                        pl.multiple_of(i * blk + r0, kb), _sb_mask(blk - r0, kb, 0),
                        tuple(tuple(_sb_rows_from(a, blk, r0) for a in c) for c in carry))
            carry = tuple(tuple(_sb_rows_merge(a, s, blk, r0) for a, s in zip(c, cs))
                          for c, cs in zip(carry, sub))
        for g, (dq_acc, _, _) in enumerate(carry):
            dq_ref[:, SB_LANES[g]] = _bf(_sb_unstack(dq_acc, blk) * ATT_SCALE)

        @pl.when(i == ni - 1)
        def _():
            dkout_ref[...] = _bf(dk_ref[...])
            dvout_ref[...] = _bf(dv_ref[...])

    width = SB_GROUP * PAIR
    n_steps = N_PAIRS // SB_GROUP
    blkspec = lambda off: pl.BlockSpec((blk, width), lambda p, i: (i, off + p))
    full = lambda off: pl.BlockSpec((t, width), lambda p, i: (0, off + p))
    return pl.pallas_call(
        body, name=name, grid=(n_steps, ni),
        in_specs=[blkspec(0), full(n_steps), full(2 * n_steps), blkspec(0), blkspec(0)],
        out_specs=[blkspec(0), full(0), full(0)],
        out_shape=[_sds((t, D_MODEL), BF16)] * 3,
        scratch_shapes=[pltpu.VMEM((t, width), F32), pltpu.VMEM((t, width), F32)],
        compiler_params=_params("arbitrary", "arbitrary"),
    )(qkv, qkv, qkv, ltot, do)


def _ch_mask(i):
    r = lax.broadcasted_iota(jnp.int32, (CH_QB, CH_WIN), 0)
    c = lax.broadcasted_iota(jnp.int32, (CH_QB, CH_WIN), 1)
    qc = LOOKBACK + lax.shift_right_arithmetic(r, 6)
    kc = lax.shift_right_arithmetic(c, 6)
    first = i * (CH_QB // CHUNK) - LOOKBACK
    return (kc <= qc) & (kc >= qc - LOOKBACK) & (kc + first >= 0)


def _ch_probs(qm, kw, bias_h, mask):
    z = _dot_nt(qm, kw) * ATT_SCALE + bias_h
    z = jnp.where(mask, z, NEG_INF)
    e = jnp.exp(z - jnp.max(z, axis=1, keepdims=True))
    return e / jnp.sum(e, axis=1, keepdims=True)


def _ch_fill(pad_ref, src_ref, t):
    pad_ref[pl.ds(0, CH_LOOK), :] = jnp.zeros((CH_LOOK, PAIR), BF16)
    pad_ref[pl.ds(CH_LOOK, t), :] = _bf(src_ref[...])


def _ch_fwd(qkv, bias, o_in, *, name):
    t = qkv.shape[0]
    ni = t // CH_QB

    def body(q_ref, k_ref, v_ref, bias_ref, _alias, o_ref, kpad, vpad):
        i = pl.program_id(1)

        @pl.when(i == 0)
        def _():
            _ch_fill(kpad, k_ref, t)
            _ch_fill(vpad, v_ref, t)

        win = pl.ds(pl.multiple_of(i * CH_QB, CH_QB), CH_WIN)
        kw, vw = kpad[win, :], vpad[win, :]
        mask = _ch_mask(i)
        q = q_ref[...]
        outs = []
        for h, hm in enumerate(_head_masks()):
            p = _ch_probs(jnp.where(hm, q, 0.0), kw, bias_ref[h], mask)
            outs.append(_dot(p, vw))
        o_ref[...] = jnp.where(_head_masks()[0], outs[0], outs[1])

    full = lambda off: pl.BlockSpec((t, PAIR), lambda p, i: (0, off + p))
    return pl.pallas_call(
        body, name=name, grid=(N_PAIRS, ni),
        in_specs=[pl.BlockSpec((CH_QB, PAIR), lambda p, i: (i, 3 * N_PAIRS + p)),
                  full(4 * N_PAIRS), full(5 * N_PAIRS),
                  pl.BlockSpec((2, CH_QB, CH_WIN), lambda p, i: (p, 0, 0)), ANY],
        out_specs=pl.BlockSpec((CH_QB, PAIR), lambda p, i: (i, N_PAIRS + p)),
        out_shape=_sds((t, D_MODEL)),
        scratch_shapes=[pltpu.VMEM((t + CH_LOOK, PAIR), BF16)] * 2,
        input_output_aliases={4: 0},
        compiler_params=_params("arbitrary", "arbitrary"),
    )(qkv, qkv, qkv, bias, o_in)


def _ch_bwd(qkv, bias, o, do, dq_in, dk_in, dv_in, *, name):
    t = qkv.shape[0]
    ni = t // CH_QB

    def body(q_ref, k_ref, v_ref, bias_ref, o_ref, do_ref, _a0, _a1, _a2,
             dq_ref, dkout_ref, dvout_ref, dbias_ref, kpad, vpad, dkpad, dvpad):
        i = pl.program_id(1)

        @pl.when(i == 0)
        def _():
            _ch_fill(kpad, k_ref, t)
            _ch_fill(vpad, v_ref, t)
            dkpad[...] = jnp.zeros_like(dkpad)
            dvpad[...] = jnp.zeros_like(dvpad)
            dbias_ref[...] = jnp.zeros_like(dbias_ref)

        win = pl.ds(pl.multiple_of(i * CH_QB, CH_QB), CH_WIN)
        kw, vw = kpad[win, :], vpad[win, :]
        mask = _ch_mask(i)
        q, o_blk, do_blk = q_ref[...], o_ref[...], do_ref[...]
        dqs = []
        for h, hm in enumerate(_head_masks()):
            qm = _bf(jnp.where(hm, q, 0.0))
            dom = jnp.where(hm, do_blk, 0.0)
            delta = jnp.sum(dom * o_blk, axis=1, keepdims=True)
            dom = _bf(dom)
            p = _ch_probs(qm, kw, bias_ref[h], mask)
            ds = p * (_dot_nt(dom, vw) - delta)
            dbias_ref[h] += ds
            dsz = ds * ATT_SCALE
            dqs.append(_dot(dsz, kw))
            dkpad[win, :] += _dot_tn(dsz, qm)
            dvpad[win, :] += _dot_tn(p, dom)
        dq_ref[...] = _bf(jnp.where(_head_masks()[0], dqs[0], dqs[1]))

        @pl.when(i == ni - 1)
        def _():
            dkout_ref[...] = _bf(dkpad[pl.ds(CH_LOOK, t), :])
            dvout_ref[...] = _bf(dvpad[pl.ds(CH_LOOK, t), :])

    blkspec = lambda off: pl.BlockSpec((CH_QB, PAIR), lambda p, i: (i, off + p))
    full = lambda off: pl.BlockSpec((t, PAIR), lambda p, i: (0, off + p))
    bias_spec = pl.BlockSpec((2, CH_QB, CH_WIN), lambda p, i: (p, 0, 0))
    return pl.pallas_call(
        body, name=name, grid=(N_PAIRS, ni),
        in_specs=[blkspec(3 * N_PAIRS), full(4 * N_PAIRS), full(5 * N_PAIRS), bias_spec,
                  blkspec(N_PAIRS), blkspec(N_PAIRS), ANY, ANY, ANY],
        out_specs=[blkspec(N_PAIRS), full(N_PAIRS), full(N_PAIRS), bias_spec],
        out_shape=[_sds((t, D_MODEL), BF16)] * 3 + [_sds((2 * N_PAIRS, CH_QB, CH_WIN))],
        scratch_shapes=[pltpu.VMEM((t + CH_LOOK, PAIR), BF16)] * 2
        + [pltpu.VMEM((t + CH_LOOK, PAIR), F32)] * 2,
        input_output_aliases={6: 0, 7: 1, 8: 2},
        compiler_params=_params("arbitrary", "arbitrary"),
    )(qkv, qkv, qkv, bias, o, do, dq_in, dk_in, dv_in)


def _bias_expand(fvec, *, name):
    n_heads = fvec.shape[0]

    def body(f_ref, o_ref, rows8):
        row = f_ref[0]
        for r in range(8):
            rows8[pl.ds(r, 1), :] = pltpu.roll(row, r, 1)
        base = rows8[...]
        for blk in range(CH_QB // 8):
            o_ref[0, pl.ds(8 * blk, 8), :] = pltpu.roll(base, 8 * blk, 1)

    return pl.pallas_call(
        body, name=name, grid=(n_heads,),
        in_specs=[pl.BlockSpec((1, 1, CH_WIN), lambda h: (h, 0, 0))],
        out_specs=pl.BlockSpec((1, CH_QB, CH_WIN), lambda h: (h, 0, 0)),
        out_shape=_sds((n_heads, CH_QB, CH_WIN)),
        scratch_shapes=[pltpu.VMEM((8, CH_WIN), F32)],
        compiler_params=_params("arbitrary"),
    )(fvec)


def _bias_grad(dbias, after, *, name):
    n_heads = dbias.shape[0]
    first = CH_LOOK - REL_CLIP

    def body(d_ref, _after, o_ref, acc8):
        acc = jnp.zeros((8, CH_WIN), F32)
        for blk in range(CH_QB // 8):
            acc = acc + pltpu.roll(d_ref[0, pl.ds(8 * blk, 8), :], (CH_WIN - 8 * blk) % CH_WIN, 1)
        acc8[...] = acc
        dvec = jnp.zeros((1, CH_WIN), F32)
        for r in range(8):
            dvec = dvec + pltpu.roll(acc8[pl.ds(r, 1), :], (CH_WIN - r) % CH_WIN, 1)
        lane = lax.broadcasted_iota(jnp.int32, (1, CH_WIN), 1)
        clipped = (lane <= first) | (lane >= first + REL_CLIP + CHUNK)
        total = jnp.sum(jnp.where(clipped, dvec, 0.0), axis=1, keepdims=True)
        o_ref[0] = jnp.where(lane == first, total, dvec)

    return pl.pallas_call(
        body, name=name, grid=(n_heads,),
        in_specs=[pl.BlockSpec((1, CH_QB, CH_WIN), lambda h: (h, 0, 0)), ANY],
        out_specs=pl.BlockSpec((1, 1, CH_WIN), lambda h: (h, 0, 0)),
        out_shape=_sds((n_heads, 1, CH_WIN)),
        scratch_shapes=[pltpu.VMEM((8, CH_WIN), F32)],
        compiler_params=_params("arbitrary"),
    )(dbias, after)


def _out_fwd(o, h1, g_sb, g_ch, g_post, wout, *, name):
    t = o.shape[0]
    tm = 512
    half = D_MODEL // 2

    def body(o_ref, h_ref, gsb_ref, gch_ref, gpost_ref, w_ref, h2_ref, mixed_ref, y_ref):
        ov = o_ref[...]
        mixed = jnp.concatenate([_rms(ov[:, :half], gsb_ref[...]),
                                 _rms(ov[:, half:], gch_ref[...])], axis=1)
        mixed_ref[...] = _bf(mixed)
        y = _dot(mixed, w_ref[...])
        y_ref[...] = y
        h2_ref[...] = h_ref[...] + _rms(y, gpost_ref[...])

    row = pl.BlockSpec((tm, D_MODEL), lambda i: (i, 0))
    gain = lambda n: pl.BlockSpec((1, n), lambda i: (0, 0))
    return pl.pallas_call(
        body, name=name, grid=(t // tm,),
        in_specs=[row, row, gain(half), gain(half), gain(D_MODEL),
                  pl.BlockSpec((D_MODEL, D_MODEL), lambda i: (0, 0))],
        out_specs=[row, row, row],
        out_shape=[_sds((t, D_MODEL)), _sds((t, D_MODEL), BF16), _sds((t, D_MODEL))],
        compiler_params=_params("arbitrary"),
    )(o, h1, g_sb, g_ch, g_post, wout)


def _out_bwd(dy, mixed, o, g_sb, g_ch, wout, *, name):
    t = o.shape[0]
    tm = 512
    ni = t // tm
    half = D_MODEL // 2

    def body(dy_ref, mixed_ref, o_ref, gsb_ref, gch_ref, w_ref,
             dw_ref, do_ref, dgsb_ref, dgch_ref, acc_ref):
        i = pl.program_id(0)

        @pl.when(i == 0)
        def _():
            acc_ref[...] = jnp.zeros_like(acc_ref)
            dgsb_ref[...] = jnp.zeros_like(dgsb_ref)
            dgch_ref[...] = jnp.zeros_like(dgch_ref)

        dyv = dy_ref[...]
        acc_ref[...] += _dot_tn(mixed_ref[...], dyv)
        dm = _dot_nt(dyv, w_ref[...])
        ov = o_ref[...]
        doa, dga = _rms_bwd(dm[:, :half], ov[:, :half], gsb_ref[...])
        dob, dgb = _rms_bwd(dm[:, half:], ov[:, half:], gch_ref[...])
        do_ref[...] = jnp.concatenate([doa, dob], axis=1)
        dgsb_ref[...] += dga
        dgch_ref[...] += dgb

        @pl.when(i == ni - 1)
        def _():
            dw_ref[...] = _bf(acc_ref[...])

    row = pl.BlockSpec((tm, D_MODEL), lambda i: (i, 0))
    gain = pl.BlockSpec((1, half), lambda i: (0, 0))
    sq = pl.BlockSpec((D_MODEL, D_MODEL), lambda i: (0, 0))
    return pl.pallas_call(
        body, name=name, grid=(ni,),
        in_specs=[row, row, row, gain, gain, sq],
        out_specs=[sq, row, gain, gain],
        out_shape=[_sds((D_MODEL, D_MODEL), BF16), _sds((t, D_MODEL)),
                   _sds((1, half)), _sds((1, half))],
        scratch_shapes=[pltpu.VMEM((D_MODEL, D_MODEL), F32)],
        compiler_params=_params("arbitrary"),
    )(dy, mixed, o, g_sb, g_ch, wout)


def _ple(p, h3, target, wp, wgate, g, f_post, g_post, *, name):
    t = h3.shape[0]
    tm = 512
    ni = t // tm

    def body(p_ref, h_ref, tgt_ref, wp_ref, wg_ref, g_ref, f_ref, gf_ref,
             loss_ref, dres_ref, dwp_ref, dwg_ref, dg_ref, df_ref, dgf_ref, accp, accg):
        i = pl.program_id(0)

        @pl.when(i == 0)
        def _():
            loss_ref[...] = jnp.zeros_like(loss_ref)
            dg_ref[...] = jnp.zeros_like(dg_ref)
            dgf_ref[...] = jnp.zeros_like(dgf_ref)
            accp[...] = jnp.zeros_like(accp)
            accg[...] = jnp.zeros_like(accg)

        pv, hv, gv = p_ref[...], h_ref[...], g_ref[...]
        pe = _dot(pv, wp_ref[...])
        sig = _sigmoid(_dot(hv, wg_ref[...]))
        e = pe * sig
        err = hv + _rms(e, gv) - tgt_ref[...]
        tok = jnp.mean(err * err, axis=-1, keepdims=True)
        loss_ref[...] += 0.5 * jnp.sum(tok, axis=0, keepdims=True)
        dh4 = err * (1.0 / D_MODEL)
        de, dg = _rms_bwd(dh4, e, gv)
        dg_ref[...] += dg
        dpe = de * sig
        dgt = de * pe * sig * (1.0 - sig)
        accp[...] += _dot_tn(pv, dpe)
        accg[...] += _dot_tn(hv, dgt)
        dres = dh4 + _dot_nt(dgt, wg_ref[...])
        dres_ref[...] = dres
        df, dgf = _rms_bwd(0.5 * dres, f_ref[...], gf_ref[...])
        df_ref[...] = _bf(df)
        dgf_ref[...] += dgf

        @pl.when(i == ni - 1)
        def _():
            dwp_ref[...] = _bf(accp[...])
            dwg_ref[...] = _bf(accg[...])

    row = pl.BlockSpec((tm, D_MODEL), lambda i: (i, 0))
    const = lambda r, c: pl.BlockSpec((r, c), lambda i: (0, 0))
    return pl.pallas_call(
        body, name=name, grid=(ni,),
        in_specs=[pl.BlockSpec((tm, PLE_DIM), lambda i: (i, 0)), row, row,
                  const(PLE_DIM, D_MODEL), const(D_MODEL, D_MODEL), const(1, D_MODEL),
                  row, const(1, D_MODEL)],
        out_specs=[const(1, 128), row, const(PLE_DIM, D_MODEL), const(D_MODEL, D_MODEL),
                   const(1, D_MODEL), row, const(1, D_MODEL)],
        out_shape=[_sds((1, 128)), _sds((t, D_MODEL)), _sds((PLE_DIM, D_MODEL), BF16),
                   _sds((D_MODEL, D_MODEL), BF16), _sds((1, D_MODEL)),
                   _sds((t, D_MODEL), BF16), _sds((1, D_MODEL))],
        scratch_shapes=[pltpu.VMEM((PLE_DIM, D_MODEL), F32), pltpu.VMEM((D_MODEL, D_MODEL), F32)],
        compiler_params=_params("arbitrary"),
    )(p, h3, target, wp, wgate, g, f_post, g_post)


def _rel_bias_to_fvec(rel_bias):
    rev = rel_bias[:, ::-1]
    n_heads = rel_bias.shape[0]
    first = CH_LOOK - REL_CLIP
    n_var = REL_CLIP + CHUNK
    clipped = rev[:, :1]
    fvec = jnp.concatenate([jnp.broadcast_to(clipped, (n_heads, first)), rev[:, :n_var],
                            jnp.broadcast_to(clipped, (n_heads, CH_WIN - first - n_var))], axis=1)
    return fvec.reshape(n_heads, 1, CH_WIN)


def _fvec_grad_to_rel_bias(dfvec):
    first = CH_LOOK - REL_CLIP
    n_var = REL_CLIP + CHUNK
    rev = jnp.pad(dfvec[:, 0, first:first + n_var], ((0, 0), (0, N_REL - n_var)))
    return rev[:, ::-1]


def _local_step(x, p, target, g, weights_for, grads_done, fvec, weights_early=None):
    bias = _bias_expand(fvec, name="bias_expand")
    w, tie = weights_for(0, bias)
    w = dict(w)
    h1, n1, a1, b1, f1 = _ffn_fwd(x, g["ffn1_pre"] + tie, g["ffn1_post"],
                                  w["ffn1_gate"], w["ffn1_up"], w["ffn1_down"], name="ffn1_fwd")
    more, tie = weights_for(1, h1)
    w.update(more)
    qkv, u = _qkv_fwd(h1, g["mix_pre"] + tie, w["in"], name="qkv_fwd")
    o, ltot = _sb_fwd(qkv, name="sb_fwd")
    tie = weights_early(2, ltot) if weights_early else 0.0
    o = _ch_fwd(qkv, bias, o, name="ch_fwd")
    h2, mixed, y = _out_fwd(o, h1, g["out_sb"] + tie, g["out_ch"], g["mix_post"], w["out"],
                            name="out_fwd")
    w.update(weights_for(2, h2)[0])
    h3, n2, a2, b2, f2 = _ffn_fwd(h2, g["ffn2_pre"], g["ffn2_post"],
                                  w["ffn2_gate"], w["ffn2_up"], w["ffn2_down"], name="ffn2_fwd")
    loss, dh3, dwp, dwgate, dg_ple, df2, dg_ffn2_post = _ple(
        p, h3, target, w["ple_proj"], w["ple_gate"], g["ple_post"], f2, g["ffn2_post"], name="ple")
    tie = grads_done(0, {"ple_proj": dwp, "ple_gate": dwgate})
    dwg2, dwu2, dwd2, dn2 = _ffn_bwd(n2, df2, a2, b2, w["ffn2_gate"], w["ffn2_up"],
                                     w["ffn2_down"], name="ffn2_bwd")
    tie = tie + grads_done(1, {"ffn2_gate": dwg2, "ffn2_up": dwu2, "ffn2_down": dwd2})
    dh2, dg_ffn2_pre, dy, dg_mix_post = _junction(
        dh3, pre=(dn2, h2, g["ffn2_pre"] + tie), post=(y, g["mix_post"], 1.0), name="junction2")
    dwout, do, dg_sb, dg_ch = _out_bwd(dy, mixed, o, g["out_sb"], g["out_ch"], w["out"],
                                       name="out_bwd")
    dq, dk, dv = _sb_bwd(qkv, ltot, do, name="sb_bwd")
    dq, dk, dv, dbias = _ch_bwd(qkv, bias, o, do, dq, dk, dv, name="ch_bwd")
    dwin, du = _qkv_bwd(dq, dk, dv, u, w["in"], name="qkv_bwd")
    tie = grads_done(2, {"out": dwout, "in": dwin})
    dh1, dg_mix_pre, df1, dg_ffn1_post = _junction(
        dh2, pre=(du, h1, g["mix_pre"] + tie), post=(f1, g["ffn1_post"], 0.5), name="junction1")
    dwg1, dwu1, dwd1, dn1 = _ffn_bwd(n1, df1, a1, b1, w["ffn1_gate"], w["ffn1_up"],
                                     w["ffn1_down"], name="ffn1_bwd")
    tie = grads_done(3, {"ffn1_gate": dwg1, "ffn1_up": dwu1, "ffn1_down": dwd1})
    dx, dg_ffn1_pre = _junction(dh1, pre=(dn1, x, g["ffn1_pre"] + tie), name="junction0")

    dg = {"ffn1_pre": dg_ffn1_pre, "ffn1_post": dg_ffn1_post, "mix_pre": dg_mix_pre,
          "mix_post": dg_mix_post, "out_sb": dg_sb, "out_ch": dg_ch,
          "ffn2_pre": dg_ffn2_pre, "ffn2_post": dg_ffn2_post, "ple_post": dg_ple}
    return loss, dx, dg, dbias


_WEIGHTS = (
    ("ffn1_gate", "row", FF_SHARD, FF_SHARD_PAD, D_MODEL),
    ("ffn1_up", "row", FF_SHARD, FF_SHARD_PAD, D_MODEL),
    ("ffn1_down", "row", FF_SHARD, FF_SHARD_PAD, D_MODEL),
    ("in", "col", QKV_SHARD, QKV_SHARD, D_MODEL),
    ("out", "row", ROW_SHARD, ROW_SHARD, D_MODEL),
    ("ffn2_gate", "row", FF_SHARD, FF_SHARD_PAD, D_MODEL),
    ("ffn2_up", "row", FF_SHARD, FF_SHARD_PAD, D_MODEL),
    ("ffn2_down", "row", FF_SHARD, FF_SHARD_PAD, D_MODEL),
    ("ple_proj", "col", ROW_SHARD, ROW_SHARD, PLE_DIM),
    ("ple_gate", "row", ROW_SHARD, ROW_SHARD, D_MODEL),
)
_TRANSPOSED = ("ffn1_gate", "ffn1_up", "ffn2_gate", "ffn2_up")
_SPEC = {n: (kind, valid, pad, other) for n, kind, valid, pad, other in _WEIGHTS}
_GATHER_STAGES = (("ffn1_gate", "ffn1_up", "ffn1_down"), ("in", "out"),
                  ("ffn2_gate", "ffn2_up", "ffn2_down", "ple_proj", "ple_gate"))
_SCATTER_STAGES = (("ple_proj", "ple_gate"), ("ffn2_gate", "ffn2_up", "ffn2_down"),
                   ("out", "in"), ("ffn1_gate", "ffn1_up", "ffn1_down"))
HBM = pl.BlockSpec(memory_space=pltpu.HBM)
SEM = pl.BlockSpec(memory_space=pltpu.SEMAPHORE)
EFFECT = pltpu.SideEffectType.DATAFLOW_SIDE_EFFECTING


def _shard_shape(kind, size, other):
    return (other, size) if kind == "col" else (size, other)


def _window(ref, kind, start, size):
    return ref.at[:, pl.ds(start, size)] if kind == "col" else ref.at[pl.ds(start, size), :]


def _device_tuple(k):
    return (k // 4, (k // 2) % 2, k % 2)


def _my_index():
    return 4 * lax.axis_index("x") + 2 * lax.axis_index("y") + lax.axis_index("c")


def _pack_weights(shards):
    nw = len(_WEIGHTS)

    def body(*refs):
        ins, packed, full = refs[:nw], refs[nw:2 * nw], refs[2 * nw:3 * nw]
        sem = refs[3 * nw]
        me = _my_index()
        for (_, kind, valid, pad, _), src, dst in zip(_WEIGHTS, ins, packed):
            if pad != valid:
                dst[...] = jnp.zeros_like(dst)
            if kind == "col":
                dst[:, pl.ds(0, valid)] = _bf(src[...])
            else:
                dst[pl.ds(0, valid), :] = _bf(src[...])
        for k in range(N_DEV):
            @pl.when(me == k)
            def _():
                for w, (_, kind, _, pad, _) in enumerate(_WEIGHTS):
                    pltpu.make_async_copy(packed[w], _window(full[w], kind, k * pad, pad),
                                          sem.at[w]).start()
        for w, (_, kind, _, pad, _) in enumerate(_WEIGHTS):
            pltpu.make_async_copy(packed[w], _window(full[w], kind, 0, pad), sem.at[w]).wait()

    whole = lambda shape: pl.BlockSpec(shape, lambda i: (0, 0))
    packed_shapes = [_shard_shape(kind, pad, other) for _, kind, _, pad, other in _WEIGHTS]
    outs = pl.pallas_call(
        body, name="pack_weights", grid=(1,),
        in_specs=[whole(a.shape) for a in shards],
        out_specs=[whole(s) for s in packed_shapes] + [ANY] * nw,
        out_shape=[_sds(s, BF16) for s in packed_shapes]
        + [_sds(_shard_shape(kind, N_DEV * pad, other), BF16) for _, kind, _, pad, other in _WEIGHTS],
        scratch_shapes=[pltpu.SemaphoreType.DMA((nw,))],
        compiler_params=_params("arbitrary"),
    )(*shards)
    names = [n for n, *_ in _WEIGHTS]
    return dict(zip(names, outs[:nw])), dict(zip(names, outs[nw:]))


def _hbm(a):
    return pltpu.with_memory_space_constraint(a, pltpu.HBM)


def _split_start(name, n, body_copies, sources, lands, after):
    arrays = list(sources) + list(lands)
    ns, na = len(sources), len(arrays)

    def body(*refs):
        src, land = refs[:ns], refs[ns:na]
        send, recv = refs[na + 1], refs[na + 2]
        token = refs[-1]
        body_copies(src, land, send, recv)
        token[...] = jnp.zeros_like(token)

    out = pl.pallas_call(
        body, name=name,
        out_shape=(pltpu.SemaphoreType.DMA((n,)), pltpu.SemaphoreType.DMA((n,)),
                   *[pltpu.HBM(a.shape, a.dtype) for a in arrays], _sds((8, 128))),
        in_specs=[HBM] * na + [ANY], out_specs=(SEM, SEM, *[HBM] * na, VMEM),
        input_output_aliases={i: 2 + i for i in range(na)},
        compiler_params=pltpu.CompilerParams(has_side_effects=EFFECT),
    )(*[_hbm(a) for a in arrays], after)
    return out[0], out[1], out[2:2 + ns], out[2 + ns:2 + na], out[-1]


def _split_wait(name, n, seven_of, send, recv, sources, lands, after, keep_sources=False):
    arrays = list(sources) + list(lands)
    ns, na = len(sources), len(arrays)

    def body(*refs):
        land = refs[ns:na]
        send_ref, recv_ref = refs[na], refs[na + 1]
        myself = (lax.axis_index("x"), lax.axis_index("y"), lax.axis_index("c"))
        for w in range(n):
            seven = seven_of(w, land[w])
            copy = pltpu.make_async_remote_copy(
                src_ref=seven, dst_ref=seven, send_sem=send_ref.at[w], recv_sem=recv_ref.at[w],
                device_id=myself, device_id_type=MESH)
            copy.wait_send()
            copy.wait_recv()

    out = pl.pallas_call(
        body, name=name,
        out_shape=[pltpu.HBM(a.shape, a.dtype) for a in arrays],
        in_specs=[HBM] * na + [SEM, SEM, ANY], out_specs=[HBM] * na,
        input_output_aliases={i: i for i in range(na)},
        compiler_params=pltpu.CompilerParams(has_side_effects=EFFECT),
    )(*arrays, send, recv, after)
    return out if keep_sources else out[ns:]


_ALL_PEERS = (1, 2, 3, 4, 5, 6, 7)
_NEAR_PEERS = (1, 2, 4, 6)
_FAR_CHIPS = (2, 4, 6)


def _gather_start(stage, names, packed, full, after, peers=_ALL_PEERS):
    def copies(src, land, send, recv):
        me = _my_index()
        for k in range(N_DEV):
            @pl.when(me == k)
            def _():
                for w, name in enumerate(names):
                    kind, _, pad, _ = _SPEC[name]
                    dst = _window(land[w], kind, k * pad, pad)
                    for mask in peers:
                        pltpu.make_async_remote_copy(
                            src_ref=src[w], dst_ref=dst, send_sem=send.at[w],
                            recv_sem=recv.at[w], device_id=_device_tuple(k ^ mask),
                            device_id_type=MESH).start()

    return _split_start(f"gather_start{stage}", len(names), copies,
                        [packed[n] for n in names], [full[n] for n in names], after)


def _gather_wait(stage, names, started, after, count=N_DEV - 1):
    send, recv, src, land, _ = started

    def bytes_of(w, ref):
        kind, _, pad, _ = _SPEC[names[w]]
        return _window(ref, kind, 0, count * pad)

    return dict(zip(names, _split_wait(f"gather_wait{stage}", len(names), bytes_of,
                                       send, recv, src, land, after)))


def _relay_start(stage, names, full, after):
    def copies(_, land, send, recv):
        me = _my_index()
        for k in range(N_DEV):
            @pl.when(me == k)
            def _():
                for w, name in enumerate(names):
                    kind, _, pad, _ = _SPEC[name]
                    for mask in _FAR_CHIPS:
                        win = _window(land[w], kind, (k ^ mask) * pad, pad)
                        pltpu.make_async_remote_copy(
                            src_ref=win, dst_ref=win, send_sem=send.at[w], recv_sem=recv.at[w],
                            device_id=_device_tuple(k ^ 1), device_id_type=MESH).start()

    return _split_start(f"relay_start{stage}", len(names), copies, [],
                        [full[n] for n in names], after)


def _scatter_start(stage, names, grads, after):
    def copies(src, land, send, recv):
        me = _my_index()
        for k in range(N_DEV):
            @pl.when(me != k)
            def _():
                slot = lax.rem(me + (N_DEV - 1 - k), N_DEV)
                for w, name in enumerate(names):
                    kind, _, pad, _ = _SPEC[name]
                    pltpu.make_async_remote_copy(
                        src_ref=_window(src[w], kind, k * pad, pad), dst_ref=land[w].at[slot],
                        send_sem=send.at[w], recv_sem=recv.at[w],
                        device_id=_device_tuple(k), device_id_type=MESH).start()

    lands = [lax.empty((N_DEV - 1,) + _shard_shape(_SPEC[m][0], _SPEC[m][2], _SPEC[m][3]), BF16)
             for m in names]
    return _split_start(f"scatter_start{stage}", len(names), copies, grads, lands, after)


def _scatter_wait(stage, names, started, after):
    send, recv, src, land, _ = started
    n = len(names)
    out = _split_wait(f"scatter_wait{stage}", n, lambda w, ref: ref, send, recv, src, land, after,
                      keep_sources=True)
    return dict(zip(names, out[:n])), dict(zip(names, out[n:]))


N_CHIPS = N_DEV // 2


def _pair_start(stage, names, grads, after):
    def copies(src, land, send, recv):
        me = _my_index()
        for k in range(N_DEV):
            @pl.when(me == k)
            def _():
                for w, name in enumerate(names):
                    kind, _, pad, _ = _SPEC[name]
                    for chip in range(N_CHIPS):
                        j = 2 * chip + ((k ^ 1) & 1)
                        pltpu.make_async_remote_copy(
                            src_ref=_window(src[w], kind, j * pad, pad), dst_ref=land[w].at[chip],
                            send_sem=send.at[w], recv_sem=recv.at[w],
                            device_id=_device_tuple(k ^ 1), device_id_type=MESH).start()

    lands = [lax.empty((N_CHIPS,) + _shard_shape(_SPEC[m][0], _SPEC[m][2], _SPEC[m][3]), BF16)
             for m in names]
    return _split_start(f"pair_start{stage}", len(names), copies, grads, lands, after)


def _pair_sum(dw_full, pair, *, pad, name):
    other = dw_full.shape[1]

    def body(own_ref, pair_ref, out_ref):
        out_ref[0] = _bf(own_ref[...].astype(F32) + pair_ref[0].astype(F32))

    slot = pl.BlockSpec((1, pad, other), lambda q: (q, 0, 0))
    return pl.pallas_call(
        body, name=name, grid=(N_CHIPS,),
        in_specs=[pl.BlockSpec((pad, other), lambda q: (2 * q + lax.axis_index("c"), 0)), slot],
        out_specs=slot, out_shape=_sds((N_CHIPS, pad, other), BF16),
        compiler_params=_params("arbitrary"),
    )(dw_full, pair)


def _chip_start(stage, names, sums, after):
    def copies(src, land, send, recv):
        me = _my_index()
        my_chip = lax.shift_right_logical(me, 1)
        for k in range(N_DEV):
            @pl.when((me != k) & (((me ^ k) & 1) == 0))
            def _():
                slot = lax.rem(my_chip + (N_CHIPS - 1 - k // 2), N_CHIPS)
                for w in range(len(names)):
                    pltpu.make_async_remote_copy(
                        src_ref=src[w].at[k // 2], dst_ref=land[w].at[slot],
                        send_sem=send.at[w], recv_sem=recv.at[w],
                        device_id=_device_tuple(k), device_id_type=MESH).start()

    lands = [lax.empty((N_CHIPS - 1,) + a.shape[1:], BF16) for a in sums]
    return _split_start(f"chip_start{stage}", len(names), copies, sums, lands, after)


def _adamw_chip(w, m, v, land, sums, *, name):
    shape = w.shape

    def body(w_ref, m_ref, v_ref, land_ref, own_ref, *outs):
        rows = pl.ds(0, shape[0])
        grad = own_ref[0, rows, :].astype(F32)
        for s in range(N_CHIPS - 1):
            grad = grad + land_ref[s, rows, :].astype(F32)
        _adam_update(w_ref, m_ref, v_ref, grad, *outs)

    whole = lambda a: pl.BlockSpec(a.shape, lambda i: (0,) * a.ndim)
    own = pl.BlockSpec((1,) + sums.shape[1:],
                       lambda i: (2 * lax.axis_index("x") + lax.axis_index("y"), 0, 0))
    return pl.pallas_call(
        body, name=name, grid=(1,),
        in_specs=[whole(w), whole(m), whole(v), whole(land), own],
        out_specs=[whole(w)] * 4, out_shape=[_sds(shape)] * 4,
        compiler_params=_params("arbitrary"),
    )(w, m, v, land, sums)


def _allreduce_small(small, after):
    shape = small.shape

    def body(in_ref, _after, out_ref, gath, send, recv):
        me = _my_index()
        for k in range(N_DEV):
            @pl.when(me != k)
            def _():
                pltpu.make_async_remote_copy(
                    src_ref=in_ref, dst_ref=gath.at[me], send_sem=send, recv_sem=recv,
                    device_id=_device_tuple(k), device_id_type=MESH).start()

            @pl.when(me == k)
            def _():
                gath[k] = in_ref[...]
        seven = gath.at[pl.ds(0, N_DEV - 1)]
        pltpu.make_async_remote_copy(
            src_ref=seven, dst_ref=seven, send_sem=send, recv_sem=recv,
            device_id=_device_tuple(0), device_id_type=MESH).wait()
        total = gath[0]
        for s in range(1, N_DEV):
            total = total + gath[s]
        out_ref[...] = total

    return pl.pallas_call(
        body, name="allreduce_small",
        in_specs=[VMEM, ANY], out_specs=VMEM, out_shape=_sds(shape),
        scratch_shapes=[pltpu.VMEM((N_DEV,) + shape, F32),
                        pltpu.SemaphoreType.DMA, pltpu.SemaphoreType.DMA],
    )(small, after)


def _adam_update(w_ref, m_ref, v_ref, grad, grad_ref, delta_ref, nm_ref, nv_ref):
    new_m = ADAM_B1 * m_ref[...] + (1.0 - ADAM_B1) * grad
    new_v = ADAM_B2 * v_ref[...] + (1.0 - ADAM_B2) * (grad * grad)
    m_hat = new_m / (1.0 - ADAM_B1 ** ADAM_STEP)
    v_hat = new_v / (1.0 - ADAM_B2 ** ADAM_STEP)
    grad_ref[...] = grad
    delta_ref[...] = -ADAM_LR * (m_hat / (jnp.sqrt(v_hat) + ADAM_EPS) + ADAM_WD * w_ref[...])
    nm_ref[...] = new_m
    nv_ref[...] = new_v


def _adamw(w, m, v, g, *, name):
    def body(w_ref, m_ref, v_ref, g_ref, *outs):
        _adam_update(w_ref, m_ref, v_ref, g_ref[...], *outs)

    whole = pl.BlockSpec(w.shape, lambda i: (0,) * w.ndim)
    return pl.pallas_call(
        body, name=name, grid=(1,), in_specs=[whole] * 4, out_specs=[whole] * 4,
        out_shape=[_sds(w.shape)] * 4, compiler_params=_params("arbitrary"),
    )(w, m, v, g)


def _adamw_gains(small, params):
    n = len(params)

    def body(small_ref, *refs):
        ins, outs = refs[:3 * n], refs[3 * n:]
        for r in range(n):
            width = ins[3 * r].shape[1]
            if width == D_MODEL:
                grad = small_ref[pl.ds(r, 1), :]
            else:
                grad = small_ref[pl.ds(len(_GAINS), 1), pl.ds((r - len(_GAINS)) * width, width)]
            _adam_update(*ins[3 * r:3 * r + 3], grad, *outs[4 * r:4 * r + 4])

    whole = lambda a: pl.BlockSpec(a.shape, lambda i: (0, 0))
    flat = [a for group in params for a in group]
    return pl.pallas_call(
        body, name="adamw_gains", grid=(1,),
        in_specs=[whole(small)] + [whole(a) for a in flat],
        out_specs=[whole(w) for w, _, _ in params for _ in range(4)],
        out_shape=[_sds(w.shape) for w, _, _ in params for _ in range(4)],
        compiler_params=_params("arbitrary"),
    )(small, *flat)


def _adamw_shard(w, m, v, land, dw_full, *, kind, pad, name):
    shape = w.shape
    other = shape[0] if kind == "col" else shape[1]

    def body(w_ref, m_ref, v_ref, land_ref, own_ref, *outs):
        valid = ((slice(None), pl.ds(0, shape[1])) if kind == "col"
                 else (pl.ds(0, shape[0]), slice(None)))
        grad = own_ref[valid].astype(F32)
        for s in range(N_DEV - 1):
            grad = grad + land_ref[(s,) + valid].astype(F32)
        _adam_update(w_ref, m_ref, v_ref, grad, *outs)

    whole = lambda a: pl.BlockSpec(a.shape, lambda i: (0,) * a.ndim)
    own = pl.BlockSpec(_shard_shape(kind, pad, other),
                       (lambda i: (0, _my_index())) if kind == "col" else (lambda i: (_my_index(), 0)))
    return pl.pallas_call(
        body, name=name, grid=(1,),
        in_specs=[whole(w), whole(m), whole(v), whole(land), own],
        out_specs=[whole(w)] * 4, out_shape=[_sds(shape)] * 4,
        compiler_params=_params("arbitrary"),
    )(w, m, v, land, dw_full)


_GAINS = ("ffn1_pre", "ffn1_post", "mix_pre", "mix_post", "ffn2_pre", "ffn2_post", "ple_post")
_SMALL_ROWS = 16


def _stack_gains(get):
    return jnp.concatenate([get(n) for n in _GAINS]
                           + [jnp.concatenate([get("out_sb"), get("out_ch")], axis=1)], axis=0)


def kernel(x, p, g_ffn1_pre, g_ffn1_post, w_ffn1_gate, w_ffn1_up, w_ffn1_down, g_mix_pre, g_mix_post, w_in, g_out_sb, g_out_ch, rel_bias, w_out, g_ffn2_pre, g_ffn2_post, w_ffn2_gate, w_ffn2_up, w_ffn2_down, w_ple_proj, w_ple_gate, g_ple_post, loss_target, m_g_ffn1_pre, m_g_ffn1_post, m_w_ffn1_gate, m_w_ffn1_up, m_w_ffn1_down, m_g_mix_pre, m_g_mix_post, m_w_in, m_g_out_sb, m_g_out_ch, m_rel_bias, m_w_out, m_g_ffn2_pre, m_g_ffn2_post, m_w_ffn2_gate, m_w_ffn2_up, m_w_ffn2_down, m_w_ple_proj, m_w_ple_gate, m_g_ple_post, v_g_ffn1_pre, v_g_ffn1_post, v_w_ffn1_gate, v_w_ffn1_up, v_w_ffn1_down, v_g_mix_pre, v_g_mix_post, v_w_in, v_g_out_sb, v_g_out_ch, v_rel_bias, v_w_out, v_g_ffn2_pre, v_g_ffn2_post, v_w_ffn2_gate, v_w_ffn2_up, v_w_ffn2_down, v_w_ple_proj, v_w_ple_gate, v_g_ple_post):
    given = dict(locals())
    wnames = [n for n, *_ in _WEIGHTS]

    def shard(prefix, n):
        a = given[prefix + "w_" + n][0]
        return a.T if n in _TRANSPOSED else a

    packed, full = _pack_weights([shard("", n) for n in wnames])
    first = _GATHER_STAGES[0]
    anchor = x[0]
    two_level = (0, 2)
    gathers = {}

    def start_stage(stage, after):
        peers = _NEAR_PEERS if stage in two_level else _ALL_PEERS
        gathers[stage] = _gather_start(stage, _GATHER_STAGES[stage], packed, full, after,
                                       peers=peers)

    start_stage(0, anchor)

    relays = {}

    def first_level(stage, after):
        names = _GATHER_STAGES[stage]
        last_stage = stage + 1 == len(_GATHER_STAGES)
        count = len(_NEAR_PEERS) if stage in two_level else N_DEV - 1
        ws = _gather_wait(stage, names, gathers[stage], after, count=count)
        if not last_stage:
            start_stage(stage + 1, ws[names[0]])
        if stage in two_level:
            relays[stage] = _relay_start(stage, names, ws,
                                         anchor if last_stage else gathers[stage + 1][-1])
            return ws, relays[stage][-1]
        return ws, None if last_stage else gathers[stage + 1][-1]

    def weights_early(stage, after):
        return first_level(stage, after)[1][:1, :1]

    def weights_for(stage, after):
        names = _GATHER_STAGES[stage]
        ws, token = (None, None) if stage in relays else first_level(stage, after)
        if stage in relays:
            relay = relays[stage]
            ws = _gather_wait(f"{stage}r", names, relay, after, count=len(_FAR_CHIPS))
            token = None if stage + 1 == len(_GATHER_STAGES) else gathers[stage + 1][-1]
        return ws, jnp.zeros((1, 1), F32) if token is None else token[:1, :1]

    scatters = {}

    last = len(_SCATTER_STAGES) - 1

    def grads_done(stage, grads):
        names = _SCATTER_STAGES[stage]
        start = _pair_start if stage == last else _scatter_start
        scatters[stage] = start(stage, names, [grads[n] for n in names], anchor)
        return scatters[stage][-1][:1, :1]

    gains = {n: given["g_" + n] for n in _GAINS + ("out_sb", "out_ch")}
    fvec = _rel_bias_to_fvec(rel_bias[0])
    loss, dx, dg, dbias = _local_step(x[0], p[0, 0], loss_target[0], gains,
                                      weights_for, grads_done, fvec, weights_early)

    results = {}

    def finish(stage, after):
        names = _SCATTER_STAGES[stage]
        dws, lands = _scatter_wait(stage, names, scatters[stage], after)
        for n in names:
            kind, _, pad, _ = _SPEC[n]
            out = _adamw_shard(shard("", n), shard("m_", n), shard("v_", n), lands[n], dws[n],
                               kind=kind, pad=pad, name="adamw_" + n)
            results["w_" + n] = [a.T for a in out] if n in _TRANSPOSED else out
        return results["w_" + names[-1]][0]

    names = _SCATTER_STAGES[last]
    whole = lambda w, ref: ref
    send, recv, src, land, _ = scatters[last]
    out = _split_wait(f"pair_wait{last}", len(names), whole, send, recv, src, land, dx,
                      keep_sources=True)
    sums = [_pair_sum(dwf, pair, pad=_SPEC[n][2], name="pair_sum_" + n)
            for n, dwf, pair in zip(names, out[:len(names)], out[len(names):])]
    send, recv, src, land, after = _chip_start(last, names, sums, anchor)
    for stage in range(last):
        after = finish(stage, after)
    dfvec = _bias_grad(dbias, after, name="bias_grad")
    loss_col = jnp.pad(loss[:, :1], ((0, N_DEV - 1), (0, D_MODEL - CH_WIN - 1)))
    dfv = jnp.concatenate([dfvec[:, 0, :], loss_col], axis=1)
    small = _allreduce_small(jnp.concatenate([_stack_gains(lambda n: dg[n]), dfv], axis=0), after)
    gain_names = _GAINS + ("out_sb", "out_ch")
    gain_out = _adamw_gains(small, [(given["g_" + n], given["m_g_" + n], given["v_g_" + n])
                                    for n in gain_names])
    for r, n in enumerate(gain_names):
        results["g_" + n] = gain_out[4 * r:4 * r + 4]
    d_rel = _fvec_grad_to_rel_bias(small[N_DEV:, :CH_WIN].reshape(N_DEV, 1, CH_WIN))
    results["rel_bias"] = _adamw(rel_bias[0], m_rel_bias[0], v_rel_bias[0], d_rel,
                                 name="adamw_rel_bias")
    out = _split_wait(f"chip_wait{last}", len(names), whole, send, recv, src, land,
                      results["rel_bias"][0], keep_sources=True)
    for n, own, landed in zip(names, out[:len(names)], out[len(names):]):
        res = _adamw_chip(shard("", n), shard("m_", n), shard("v_", n), landed, own,
                          name="adamw_" + n)
        results["w_" + n] = [a.T for a in res] if n in _TRANSPOSED else res

    order = ("g_ffn1_pre", "g_ffn1_post", "w_ffn1_gate", "w_ffn1_up", "w_ffn1_down",
             "g_mix_pre", "g_mix_post", "w_in", "g_out_sb", "g_out_ch", "rel_bias", "w_out",
             "g_ffn2_pre", "g_ffn2_post", "w_ffn2_gate", "w_ffn2_up", "w_ffn2_down",
             "w_ple_proj", "w_ple_gate", "g_ple_post")

    def leaf(name, idx):
        a = results[name][idx]
        return a if name.startswith("g_") else a[None]

    total_loss = small[N_DEV, CH_WIN]
    return (total_loss, dx[None],
            *[leaf(n, 0) for n in order], *[leaf(n, 1) for n in order],
            *[leaf(n, 2) for n in order], *[leaf(n, 3) for n in order])
```

```python
import jax
import jax.numpy as jnp
from jax import lax
from jax.experimental import pallas as pl
from jax.experimental.pallas import tpu as pltpu

F32 = jnp.float32
BF16 = jnp.bfloat16

N_DEV = 8
D_MODEL = 1024
D_FF = 2816
FF_SHARD = D_FF // N_DEV
FF_SHARD_PAD = 384
D_FF_PAD = FF_SHARD_PAD * N_DEV
QKV_WIDTH = 3 * D_MODEL
QKV_SHARD = QKV_WIDTH // N_DEV
PLE_DIM = 256
ROW_SHARD = D_MODEL // N_DEV
HEAD_DIM = 64
PAIR = 2 * HEAD_DIM
N_PAIRS = 4
CHUNK = 64
LOOKBACK = 8
REL_CLIP = 128
N_REL = 2 * REL_CLIP + 1
CH_QB = 256
CH_LOOK = LOOKBACK * CHUNK
CH_WIN = CH_LOOK + CH_QB
SB_QB = 512
SB_KB = 256
SB_GROUP = 2
SB_LANES = tuple(slice(g * 128, (g + 1) * 128) for g in range(SB_GROUP))
EPS = 1e-6
NEG_INF = -1e30
ATT_SCALE = HEAD_DIM ** -0.5
ADAM_LR = 0.001
ADAM_B1 = 0.9
ADAM_B2 = 0.999
ADAM_EPS = 1e-08
ADAM_WD = 0.01
ADAM_STEP = 10
VMEM_LIMIT_BYTES = 48 * 1024 * 1024
MESH = pl.DeviceIdType.MESH

ANY = pl.BlockSpec(memory_space=pl.ANY)
VMEM = pl.BlockSpec(memory_space=pltpu.VMEM)


def _params(*sem):
    return pltpu.CompilerParams(dimension_semantics=sem or None,
                                vmem_limit_bytes=VMEM_LIMIT_BYTES)


def _sds(shape, dtype=F32):
    return jax.ShapeDtypeStruct(shape, dtype)


def _bf(x):
    return x.astype(BF16)


def _dot(a, b):
    return jnp.dot(_bf(a), _bf(b), preferred_element_type=F32)


def _dot_nt(a, b):
    return lax.dot_general(_bf(a), _bf(b), (((1,), (1,)), ((), ())),
                           preferred_element_type=F32)


def _dot_tn(a, b):
    return lax.dot_general(_bf(a), _bf(b), (((0,), (0,)), ((), ())),
                           preferred_element_type=F32)


def _sigmoid(x):
    return 1.0 / (1.0 + jnp.exp(-x))


def _softplus(x):
    return jnp.maximum(x, 0.0) + jnp.log(1.0 + jnp.exp(-jnp.abs(x)))


def _rstd(x):
    return lax.rsqrt(jnp.mean(x * x, axis=-1, keepdims=True) + EPS)


def _rms(x, g):
    return x * _rstd(x) * g


def _rms_bwd(dy, x, g):
    r = _rstd(x)
    w = dy * g
    dx = r * (w - x * (r * r) * jnp.mean(w * x, axis=-1, keepdims=True))
    dg = jnp.sum(dy * (x * r), axis=0, keepdims=True)
    return dx, dg


def _head_masks():
    lane = lax.broadcasted_iota(jnp.int32, (1, PAIR), 1)
    return lane < HEAD_DIM, lane >= HEAD_DIM


def _ffn_fwd(x, g_pre, g_post, wg, wu, wd, *, name):
    t = x.shape[0]
    tm, tj = 512, 1024
    ni, nj = t // tm, D_FF_PAD // tj

    def body(x_ref, gpre_ref, gpost_ref, wg_ref, wu_ref, wd_ref,
             h_ref, n_ref, a_ref, b_ref, f_ref, acc_ref):
        j = pl.program_id(1)

        @pl.when(j == 0)
        def _():
            n_ref[...] = _bf(_rms(x_ref[...], gpre_ref[...]))
            acc_ref[...] = jnp.zeros_like(acc_ref)

        n = n_ref[...]
        a = _dot_nt(n, wg_ref[...])
        b = _dot_nt(n, wu_ref[...])
        a_ref[...] = a
        b_ref[...] = b
        hmid = a * _sigmoid(a) * b
        acc_ref[...] += jnp.dot(_bf(hmid), wd_ref[...], preferred_element_type=F32)

        @pl.when(j == nj - 1)
        def _():
            f = acc_ref[...]
            f_ref[...] = f
            h_ref[...] = x_ref[...] + 0.5 * _rms(f, gpost_ref[...])

    row = pl.BlockSpec((tm, D_MODEL), lambda i, j: (i, 0))
    gain = pl.BlockSpec((1, D_MODEL), lambda i, j: (0, 0))
    col = pl.BlockSpec((tm, tj), lambda i, j: (i, j))
    wtile = pl.BlockSpec((tj, D_MODEL), lambda i, j: (j, 0))
    return pl.pallas_call(
        body, name=name, grid=(ni, nj),
        in_specs=[row, gain, gain, wtile, wtile, wtile],
        out_specs=[row, row, col, col, row],
        out_shape=[_sds((t, D_MODEL)), _sds((t, D_MODEL), BF16),
                   _sds((t, D_FF_PAD)), _sds((t, D_FF_PAD)), _sds((t, D_MODEL))],
        scratch_shapes=[pltpu.VMEM((tm, D_MODEL), F32)],
        compiler_params=_params("arbitrary", "arbitrary"),
    )(x, g_pre, g_post, wg, wu, wd)


def _ffn_bwd(n, df, a, b, wg, wu, wd, *, name):
    t = n.shape[0]
    tj, tm, ts = 256, t, 512
    nj, ni, ns = D_FF_PAD // tj, t // tm, tm // ts

    def body(n_hbm, df_hbm, a_ref, b_ref, wg_ref, wu_ref, wd_ref,
             dwg_ref, dwu_ref, dwd_ref, dn_hbm,
             n_v, df_v, dn_v, ag, au, ad, sem):
        j, i = pl.program_id(0), pl.program_id(1)

        @pl.when((j == 0) & (i == 0))
        def _():
            c1 = pltpu.make_async_copy(n_hbm, n_v, sem.at[0])
            c2 = pltpu.make_async_copy(df_hbm, df_v, sem.at[1])
            c1.start()
            c2.start()
            dn_v[...] = jnp.zeros_like(dn_v)
            c1.wait()
            c2.wait()

        @pl.when(i == 0)
        def _():
            ag[...] = jnp.zeros_like(ag)
            au[...] = jnp.zeros_like(au)
            ad[...] = jnp.zeros_like(ad)

        wgj, wuj, wdj = wg_ref[...], wu_ref[...], wd_ref[...]
        for s in range(ns):
            local = pl.ds(s * ts, ts)
            rows = pl.ds(pl.multiple_of(i * tm + s * ts, ts), ts)
            av, bv = a_ref[local, :], b_ref[local, :]
            sig = _sigmoid(av)
            silu = av * sig
            dfr = df_v[rows, :]
            nr = n_v[rows, :]
            dhmid = _dot_nt(dfr, wdj)
            da = dhmid * bv * (sig * (1.0 + av * (1.0 - sig)))
            db = dhmid * silu
            ad[...] += _dot_tn(silu * bv, dfr)
            ag[...] += _dot_tn(da, nr)
            au[...] += _dot_tn(db, nr)
            dn_v[rows, :] += _dot(da, wgj) + _dot(db, wuj)

        @pl.when(i == ni - 1)
        def _():
            dwg_ref[...] = _bf(ag[...])
            dwu_ref[...] = _bf(au[...])
            dwd_ref[...] = _bf(ad[...])

        @pl.when((j == nj - 1) & (i == ni - 1))
        def _():
            c = pltpu.make_async_copy(dn_v, dn_hbm, sem.at[0])
            c.start()
            c.wait()

    roww = pl.BlockSpec((tj, D_MODEL), lambda j, i: (j, 0))
    act = pl.BlockSpec((tm, tj), lambda j, i: (i, j))
    return pl.pallas_call(
        body, name=name, grid=(nj, ni),
        in_specs=[ANY, ANY, act, act, roww, roww, roww],
        out_specs=[roww, roww, roww, ANY],
        out_shape=[_sds((D_FF_PAD, D_MODEL), BF16)] * 3 + [_sds((t, D_MODEL))],
        scratch_shapes=[pltpu.VMEM((t, D_MODEL), BF16), pltpu.VMEM((t, D_MODEL), BF16),
                        pltpu.VMEM((t, D_MODEL), F32)]
        + [pltpu.VMEM((tj, D_MODEL), F32)] * 3 + [pltpu.SemaphoreType.DMA((2,))],
        compiler_params=_params("arbitrary", "arbitrary"),
    )(n, df, a, b, wg, wu, wd)


def _junction(dres, pre=None, post=None, *, name):
    t = dres.shape[0]
    tm = 512
    ni = t // tm
    n_in = 1 + (3 if pre else 0) + (2 if post else 0)
    coef = post[2] if post else None

    def body(*refs):
        ins, outs = list(refs[:n_in]), list(refs[n_in:])
        i = pl.program_id(0)
        dh = ins.pop(0)[...]
        if pre:
            dn_ref, x_ref, gpre_ref = ins.pop(0), ins.pop(0), ins.pop(0)
            dh_ref, dgpre_ref = outs.pop(0), outs.pop(0)
            dx, dg = _rms_bwd(dn_ref[...], x_ref[...], gpre_ref[...])
            dh = dh + dx
            dh_ref[...] = dh

            @pl.when(i == 0)
            def _():
                dgpre_ref[...] = jnp.zeros_like(dgpre_ref)
            dgpre_ref[...] += dg
        if post:
            f_ref, gpost_ref = ins.pop(0), ins.pop(0)
            df_ref, dgpost_ref = outs.pop(0), outs.pop(0)
            df, dg = _rms_bwd(coef * dh, f_ref[...], gpost_ref[...])
            df_ref[...] = _bf(df)

            @pl.when(i == 0)
            def _():
                dgpost_ref[...] = jnp.zeros_like(dgpost_ref)
            dgpost_ref[...] += dg

    row = pl.BlockSpec((tm, D_MODEL), lambda i: (i, 0))
    gain = pl.BlockSpec((1, D_MODEL), lambda i: (0, 0))
    args, in_specs, out_specs, out_shape = [dres], [row], [], []
    if pre:
        args += list(pre)
        in_specs += [row, row, gain]
        out_specs += [row, gain]
        out_shape += [_sds((t, D_MODEL)), _sds((1, D_MODEL))]
    if post:
        args += [post[0], post[1]]
        in_specs += [row, gain]
        out_specs += [row, gain]
        out_shape += [_sds((t, D_MODEL), BF16), _sds((1, D_MODEL))]
    return pl.pallas_call(
        body, name=name, grid=(ni,), in_specs=in_specs, out_specs=out_specs,
        out_shape=out_shape, compiler_params=_params("arbitrary"),
    )(*args)


def _qkv_fwd(h, g, win, *, name):
    t = h.shape[0]
    tm, tn = min(1024, t), 1024
    ni, nj = t // tm, QKV_WIDTH // tn

    def body(h_ref, g_ref, w_ref, qkv_ref, u_ref):
        @pl.when(pl.program_id(1) == 0)
        def _():
            u_ref[...] = _bf(_rms(h_ref[...], g_ref[...]))
        qkv_ref[...] = jnp.dot(u_ref[...], w_ref[...], preferred_element_type=F32)

    row = pl.BlockSpec((tm, D_MODEL), lambda i, j: (i, 0))
    return pl.pallas_call(
        body, name=name, grid=(ni, nj),
        in_specs=[row, pl.BlockSpec((1, D_MODEL), lambda i, j: (0, 0)),
                  pl.BlockSpec((D_MODEL, tn), lambda i, j: (0, j))],
        out_specs=[pl.BlockSpec((tm, tn), lambda i, j: (i, j)), row],
        out_shape=[_sds((t, QKV_WIDTH)), _sds((t, D_MODEL), BF16)],
        compiler_params=_params("arbitrary", "arbitrary"),
    )(h, g, win)


def _qkv_bwd(dq, dk, dv, u, win, *, name):
    t = u.shape[0]
    tn, ts = 512, 512
    nj, ns = QKV_WIDTH // tn, t // ts

    def body(dq_ref, dk_ref, dv_ref, u_ref, w_ref, dw_ref, du_hbm, du_v, acc_ref, sem):
        j = pl.program_id(0)

        @pl.when(j == 0)
        def _():
            du_v[...] = jnp.zeros_like(du_v)

        wj = w_ref[...]
        for role, d_ref in enumerate((dq_ref, dk_ref, dv_ref)):
            @pl.when(j % 3 == role)
            def _():
                acc_ref[...] = jnp.zeros_like(acc_ref)
                for s in range(ns):
                    rows = pl.ds(s * ts, ts)
                    dcol = d_ref[rows, :]
                    acc_ref[...] += _dot_tn(u_ref[rows, :], dcol)
                    du_v[rows, :] += _dot_nt(dcol, wj)
                dw_ref[...] = _bf(acc_ref[...])

        @pl.when(j == nj - 1)
        def _():
            c = pltpu.make_async_copy(du_v, du_hbm, sem)
            c.start()
            c.wait()

    colw = pl.BlockSpec((D_MODEL, tn), lambda j: (0, j))
    grp = pl.BlockSpec((t, tn), lambda j: (0, j // 3))
    return pl.pallas_call(
        body, name=name, grid=(nj,),
        in_specs=[grp, grp, grp, pl.BlockSpec((t, D_MODEL), lambda j: (0, 0)), colw],
        out_specs=[colw, ANY],
        out_shape=[_sds((D_MODEL, QKV_WIDTH), BF16), _sds((t, D_MODEL))],
        scratch_shapes=[pltpu.VMEM((t, D_MODEL), F32), pltpu.VMEM((D_MODEL, tn), F32),
                        pltpu.SemaphoreType.DMA],
        compiler_params=_params("arbitrary"),
    )(dq, dk, dv, u, win)


def _sb_stack(x):
    lo, hi = _head_masks()
    return jnp.concatenate([jnp.where(lo, x, 0.0), jnp.where(hi, x, 0.0)], axis=0)


def _sb_unstack(x2, blk):
    return jnp.where(_head_masks()[0], x2[:blk], x2[blk:])


def _sb_rows_from(x2, blk, r0):
    return x2 if r0 == 0 else jnp.concatenate([x2[r0:blk], x2[blk + r0:]], axis=0)


def _sb_rows_merge(full2, sub2, blk, r0):
    if r0 == 0:
        return sub2
    rows = blk - r0
    return jnp.concatenate([full2[:r0], sub2[:rows], full2[blk:blk + r0], sub2[rows:]], axis=0)


def _sb_mask(qb, kb, offset):
    r = lax.broadcasted_iota(jnp.int32, (2 * qb, kb), 0) & (qb - 1)
    c = lax.broadcasted_iota(jnp.int32, (2 * qb, kb), 1) + offset
    return c < r


def _tri(n, keep):
    r = lax.broadcasted_iota(jnp.int32, (n, n), 0)
    c = lax.broadcasted_iota(jnp.int32, (n, n), 1)
    return jnp.where(keep(r, c), 1.0, 0.0).astype(BF16)


def _cumsum01(x, u):
    m = x.shape[0]
    hi = _bf(x)
    lo = _bf(x - hi.astype(F32))
    both = jnp.dot(jnp.concatenate([hi, lo], axis=0), u, preferred_element_type=F32)
    return both[:m] + both[m:]


def _sb_fwd(qkv, *, name):
    t = qkv.shape[0]
    blk, kb = min(SB_QB, t), SB_KB
    ni, per = t // blk, blk // kb

    def body(q_ref, k_ref, v_ref, o_ref, ltot_ref):
        i = pl.program_id(1)
        u_after = _tri(kb, lambda r, c: r > c)
        q2 = [_bf(_sb_stack(q_ref[:, lanes] * ATT_SCALE)) for lanes in SB_LANES]

        def tile(g, k0, mask, acc, c_l):
            kj = k_ref[pl.ds(k0, kb), SB_LANES[g]]
            vj = v_ref[pl.ds(k0, kb), SB_LANES[g]]
            z = _dot_nt(q2[g], kj)
            sp = _softplus(z)
            lf = -sp if mask is None else jnp.where(mask, -sp, 0.0)
            a = jnp.exp(z - sp + _cumsum01(lf, u_after) + c_l)
            if mask is not None:
                a = jnp.where(mask, a, 0.0)
            return acc + _dot(a, vj), c_l + jnp.sum(lf, axis=1, keepdims=True)

        def tiles(k0, mask, carry):
            return tuple(tile(g, k0, mask, *carry[g]) for g in range(SB_GROUP))

        carry = ((jnp.zeros((2 * blk, PAIR), F32), jnp.zeros((2 * blk, 1), F32)),) * SB_GROUP
        for d in reversed(range(per)):
            carry = tiles(pl.multiple_of(i * blk + d * kb, kb), _sb_mask(blk, kb, d * kb), carry)
        carry = lax.fori_loop(
            1, per * i + 1,
            lambda jj, c: tiles(pl.multiple_of((per * i - jj) * kb, kb), None, c), carry)
        for g, (acc, c_l) in enumerate(carry):
            o_ref[:, SB_LANES[g]] = _sb_unstack(acc, blk)
            ltot_ref[:, SB_LANES[g]] = _sb_unstack(jnp.broadcast_to(c_l, (2 * blk, PAIR)), blk)

    width = SB_GROUP * PAIR
    blkspec = pl.BlockSpec((blk, width), lambda p, i: (i, p))
    n_steps = N_PAIRS // SB_GROUP
    return pl.pallas_call(
        body, name=name, grid=(n_steps, ni),
        in_specs=[blkspec,
                  pl.BlockSpec((t, width), lambda p, i: (0, n_steps + p)),
                  pl.BlockSpec((t, width), lambda p, i: (0, 2 * n_steps + p))],
        out_specs=[blkspec, blkspec],
        out_shape=[_sds((t, D_MODEL)), _sds((t, D_MODEL // 2))],
        compiler_params=_params("arbitrary", "arbitrary"),
    )(qkv, qkv, qkv)


def _sb_bwd(qkv, ltot, do, *, name):
    t = qkv.shape[0]
    blk, kb = min(SB_QB, t), SB_KB
    ni, per = t // blk, blk // kb

    def body(q_ref, k_ref, v_ref, lt_ref, do_ref, dq_ref, dkout_ref, dvout_ref, dk_ref, dv_ref):
        i = pl.program_id(1)

        @pl.when(i == 0)
        def _():
            dk_ref[...] = jnp.zeros_like(dk_ref)
            dv_ref[...] = jnp.zeros_like(dv_ref)

        u_upto = _tri(kb, lambda r, c: r <= c)
        u_before = _tri(kb, lambda r, c: r < c)
        lane = lax.broadcasted_iota(jnp.int32, (1, PAIR), 1)
        q2 = [_bf(_sb_stack(q_ref[:, lanes] * ATT_SCALE)) for lanes in SB_LANES]
        do2 = [_bf(_sb_stack(do_ref[:, lanes])) for lanes in SB_LANES]
        total = [jnp.concatenate(
            [jnp.sum(jnp.where(lane == h * HEAD_DIM, lt_ref[:, lanes], 0.0), axis=1, keepdims=True)
             for h in range(2)], axis=0) for lanes in SB_LANES]

        def tile(g, ops, k0, mask, dq_acc, c_l, c_g):
            qg, dog, tot = ops
            krows = pl.ds(k0, kb)
            kj = k_ref[krows, SB_LANES[g]]
            vj = v_ref[krows, SB_LANES[g]]
            z = _dot_nt(qg, kj)
            sp = _softplus(z)
            sig = jnp.exp(z - sp)
            lf = -sp if mask is None else jnp.where(mask, -sp, 0.0)
            a = jnp.exp(z - sp + tot - (_cumsum01(lf, u_upto) + c_l))
            if mask is not None:
                a = jnp.where(mask, a, 0.0)
            gw = a * _dot_nt(dog, vj)
            g_before = jnp.dot(_bf(gw), u_before, preferred_element_type=F32) + c_g
            dz = gw * (1.0 - sig) - g_before * sig
            if mask is not None:
                dz = jnp.where(mask, dz, 0.0)
            dk_ref[krows, SB_LANES[g]] += _dot_tn(dz, qg)
            dv_ref[krows, SB_LANES[g]] += _dot_tn(a, dog)
            return (dq_acc + _dot(dz, kj), c_l + jnp.sum(lf, axis=1, keepdims=True),
                    c_g + jnp.sum(gw, axis=1, keepdims=True))

        def tiles(ops, k0, mask, carry):
            return tuple(tile(g, ops[g], k0, mask, *carry[g]) for g in range(SB_GROUP))

        ops = tuple(zip(q2, do2, total))
        zero = (jnp.zeros((2 * blk, PAIR), F32), jnp.zeros((2 * blk, 1), F32),
                jnp.zeros((2 * blk, 1), F32))
        carry = lax.fori_loop(
            0, per * i, lambda j, c: tiles(ops, pl.multiple_of(j * kb, kb), None, c),
            (zero,) * SB_GROUP)
        for d in range(per):
            r0 = d * kb
            sub = tiles(tuple(tuple(_sb_rows_from(a, blk, r0) for a in o) for o in ops),
                        pl.multiple_of(i * blk + r0, kb), _sb_mask(blk - r0, kb, 0),
                        tuple(tuple(_sb_rows_from(a, blk, r0) for a in c) for c in carry))
            carry = tuple(tuple(_sb_rows_merge(a, s, blk, r0) for a, s in zip(c, cs))
                          for c, cs in zip(carry, sub))
        for g, (dq_acc, _, _) in enumerate(carry):
            dq_ref[:, SB_LANES[g]] = _bf(_sb_unstack(dq_acc, blk) * ATT_SCALE)

        @pl.when(i == ni - 1)
        def _():
            dkout_ref[...] = _bf(dk_ref[...])
            dvout_ref[...] = _bf(dv_ref[...])

    width = SB_GROUP * PAIR
    n_steps = N_PAIRS // SB_GROUP
    blkspec = lambda off: pl.BlockSpec((blk, width), lambda p, i: (i, off + p))
    full = lambda off: pl.BlockSpec((t, width), lambda p, i: (0, off + p))
    return pl.pallas_call(
        body, name=name, grid=(n_steps, ni),
        in_specs=[blkspec(0), full(n_steps), full(2 * n_steps), blkspec(0), blkspec(0)],
        out_specs=[blkspec(0), full(0), full(0)],
        out_shape=[_sds((t, D_MODEL), BF16)] * 3,
        scratch_shapes=[pltpu.VMEM((t, width), F32), pltpu.VMEM((t, width), F32)],
        compiler_params=_params("arbitrary", "arbitrary"),
    )(qkv, qkv, qkv, ltot, do)


def _ch_mask(i):
    r = lax.broadcasted_iota(jnp.int32, (CH_QB, CH_WIN), 0)
    c = lax.broadcasted_iota(jnp.int32, (CH_QB, CH_WIN), 1)
    qc = LOOKBACK + lax.shift_right_arithmetic(r, 6)
    kc = lax.shift_right_arithmetic(c, 6)
    first = i * (CH_QB // CHUNK) - LOOKBACK
    return (kc <= qc) & (kc >= qc - LOOKBACK) & (kc + first >= 0)


def _ch_probs(qm, kw, bias_h, mask):
    z = _dot_nt(qm, kw) * ATT_SCALE + bias_h
    z = jnp.where(mask, z, NEG_INF)
    e = jnp.exp(z - jnp.max(z, axis=1, keepdims=True))
    return e / jnp.sum(e, axis=1, keepdims=True)


def _ch_fill(pad_ref, src_ref, t):
    pad_ref[pl.ds(0, CH_LOOK), :] = jnp.zeros((CH_LOOK, PAIR), BF16)
    pad_ref[pl.ds(CH_LOOK, t), :] = _bf(src_ref[...])


def _ch_fwd(qkv, bias, o_in, *, name):
    t = qkv.shape[0]
    ni = t // CH_QB

    def body(q_ref, k_ref, v_ref, bias_ref, _alias, o_ref, kpad, vpad):
        i = pl.program_id(1)

        @pl.when(i == 0)
        def _():
            _ch_fill(kpad, k_ref, t)
            _ch_fill(vpad, v_ref, t)

        win = pl.ds(pl.multiple_of(i * CH_QB, CH_QB), CH_WIN)
        kw, vw = kpad[win, :], vpad[win, :]
        mask = _ch_mask(i)
        q = q_ref[...]
        outs = []
        for h, hm in enumerate(_head_masks()):
            p = _ch_probs(jnp.where(hm, q, 0.0), kw, bias_ref[h], mask)
            outs.append(_dot(p, vw))
        o_ref[...] = jnp.where(_head_masks()[0], outs[0], outs[1])

    full = lambda off: pl.BlockSpec((t, PAIR), lambda p, i: (0, off + p))
    return pl.pallas_call(
        body, name=name, grid=(N_PAIRS, ni),
        in_specs=[pl.BlockSpec((CH_QB, PAIR), lambda p, i: (i, 3 * N_PAIRS + p)),
                  full(4 * N_PAIRS), full(5 * N_PAIRS),
                  pl.BlockSpec((2, CH_QB, CH_WIN), lambda p, i: (p, 0, 0)), ANY],
        out_specs=pl.BlockSpec((CH_QB, PAIR), lambda p, i: (i, N_PAIRS + p)),
        out_shape=_sds((t, D_MODEL)),
        scratch_shapes=[pltpu.VMEM((t + CH_LOOK, PAIR), BF16)] * 2,
        input_output_aliases={4: 0},
        compiler_params=_params("arbitrary", "arbitrary"),
    )(qkv, qkv, qkv, bias, o_in)


def _ch_bwd(qkv, bias, o, do, dq_in, dk_in, dv_in, *, name):
    t = qkv.shape[0]
    ni = t // CH_QB

    def body(q_ref, k_ref, v_ref, bias_ref, o_ref, do_ref, _a0, _a1, _a2,
             dq_ref, dkout_ref, dvout_ref, dbias_ref, kpad, vpad, dkpad, dvpad):
        i = pl.program_id(1)

        @pl.when(i == 0)
        def _():
            _ch_fill(kpad, k_ref, t)
            _ch_fill(vpad, v_ref, t)
            dkpad[...] = jnp.zeros_like(dkpad)
            dvpad[...] = jnp.zeros_like(dvpad)
            dbias_ref[...] = jnp.zeros_like(dbias_ref)

        win = pl.ds(pl.multiple_of(i * CH_QB, CH_QB), CH_WIN)
        kw, vw = kpad[win, :], vpad[win, :]
        mask = _ch_mask(i)
        q, o_blk, do_blk = q_ref[...], o_ref[...], do_ref[...]
        dqs = []
        for h, hm in enumerate(_head_masks()):
            qm = _bf(jnp.where(hm, q, 0.0))
            dom = jnp.where(hm, do_blk, 0.0)
            delta = jnp.sum(dom * o_blk, axis=1, keepdims=True)
            dom = _bf(dom)
            p = _ch_probs(qm, kw, bias_ref[h], mask)
            ds = p * (_dot_nt(dom, vw) - delta)
            dbias_ref[h] += ds
            dsz = ds * ATT_SCALE
            dqs.append(_dot(dsz, kw))
            dkpad[win, :] += _dot_tn(dsz, qm)
            dvpad[win, :] += _dot_tn(p, dom)
        dq_ref[...] = _bf(jnp.where(_head_masks()[0], dqs[0], dqs[1]))

        @pl.when(i == ni - 1)
        def _():
            dkout_ref[...] = _bf(dkpad[pl.ds(CH_LOOK, t), :])
            dvout_ref[...] = _bf(dvpad[pl.ds(CH_LOOK, t), :])

    blkspec = lambda off: pl.BlockSpec((CH_QB, PAIR), lambda p, i: (i, off + p))
    full = lambda off: pl.BlockSpec((t, PAIR), lambda p, i: (0, off + p))
    bias_spec = pl.BlockSpec((2, CH_QB, CH_WIN), lambda p, i: (p, 0, 0))
    return pl.pallas_call(
        body, name=name, grid=(N_PAIRS, ni),
        in_specs=[blkspec(3 * N_PAIRS), full(4 * N_PAIRS), full(5 * N_PAIRS), bias_spec,
                  blkspec(N_PAIRS), blkspec(N_PAIRS), ANY, ANY, ANY],
        out_specs=[blkspec(N_PAIRS), full(N_PAIRS), full(N_PAIRS), bias_spec],
        out_shape=[_sds((t, D_MODEL), BF16)] * 3 + [_sds((2 * N_PAIRS, CH_QB, CH_WIN))],
        scratch_shapes=[pltpu.VMEM((t + CH_LOOK, PAIR), BF16)] * 2
        + [pltpu.VMEM((t + CH_LOOK, PAIR), F32)] * 2,
        input_output_aliases={6: 0, 7: 1, 8: 2},
        compiler_params=_params("arbitrary", "arbitrary"),
    )(qkv, qkv, qkv, bias, o, do, dq_in, dk_in, dv_in)


def _bias_expand(fvec, *, name):
    n_heads = fvec.shape[0]

    def body(f_ref, o_ref, rows8):
        row = f_ref[0]
        for r in range(8):
            rows8[pl.ds(r, 1), :] = pltpu.roll(row, r, 1)
        base = rows8[...]
        for blk in range(CH_QB // 8):
            o_ref[0, pl.ds(8 * blk, 8), :] = pltpu.roll(base, 8 * blk, 1)

    return pl.pallas_call(
        body, name=name, grid=(n_heads,),
        in_specs=[pl.BlockSpec((1, 1, CH_WIN), lambda h: (h, 0, 0))],
        out_specs=pl.BlockSpec((1, CH_QB, CH_WIN), lambda h: (h, 0, 0)),
        out_shape=_sds((n_heads, CH_QB, CH_WIN)),
        scratch_shapes=[pltpu.VMEM((8, CH_WIN), F32)],
        compiler_params=_params("arbitrary"),
    )(fvec)


def _bias_grad(dbias, after, *, name):
    n_heads = dbias.shape[0]
    first = CH_LOOK - REL_CLIP

    def body(d_ref, _after, o_ref, acc8):
        acc = jnp.zeros((8, CH_WIN), F32)
        for blk in range(CH_QB // 8):
            acc = acc + pltpu.roll(d_ref[0, pl.ds(8 * blk, 8), :], (CH_WIN - 8 * blk) % CH_WIN, 1)
        acc8[...] = acc
        dvec = jnp.zeros((1, CH_WIN), F32)
        for r in range(8):
            dvec = dvec + pltpu.roll(acc8[pl.ds(r, 1), :], (CH_WIN - r) % CH_WIN, 1)
        lane = lax.broadcasted_iota(jnp.int32, (1, CH_WIN), 1)
        clipped = (lane <= first) | (lane >= first + REL_CLIP + CHUNK)
        total = jnp.sum(jnp.where(clipped, dvec, 0.0), axis=1, keepdims=True)
        o_ref[0] = jnp.where(lane == first, total, dvec)

    return pl.pallas_call(
        body, name=name, grid=(n_heads,),
        in_specs=[pl.BlockSpec((1, CH_QB, CH_WIN), lambda h: (h, 0, 0)), ANY],
        out_specs=pl.BlockSpec((1, 1, CH_WIN), lambda h: (h, 0, 0)),
        out_shape=_sds((n_heads, 1, CH_WIN)),
        scratch_shapes=[pltpu.VMEM((8, CH_WIN), F32)],
        compiler_params=_params("arbitrary"),
    )(dbias, after)


def _out_fwd(o, h1, g_sb, g_ch, g_post, wout, *, name):
    t = o.shape[0]
    tm = 512
    half = D_MODEL // 2

    def body(o_ref, h_ref, gsb_ref, gch_ref, gpost_ref, w_ref, h2_ref, mixed_ref, y_ref):
        ov = o_ref[...]
        mixed = jnp.concatenate([_rms(ov[:, :half], gsb_ref[...]),
                                 _rms(ov[:, half:], gch_ref[...])], axis=1)
        mixed_ref[...] = _bf(mixed)
        y = _dot(mixed, w_ref[...])
        y_ref[...] = y
        h2_ref[...] = h_ref[...] + _rms(y, gpost_ref[...])

    row = pl.BlockSpec((tm, D_MODEL), lambda i: (i, 0))
    gain = lambda n: pl.BlockSpec((1, n), lambda i: (0, 0))
    return pl.pallas_call(
        body, name=name, grid=(t // tm,),
        in_specs=[row, row, gain(half), gain(half), gain(D_MODEL),
                  pl.BlockSpec((D_MODEL, D_MODEL), lambda i: (0, 0))],
        out_specs=[row, row, row],
        out_shape=[_sds((t, D_MODEL)), _sds((t, D_MODEL), BF16), _sds((t, D_MODEL))],
        compiler_params=_params("arbitrary"),
    )(o, h1, g_sb, g_ch, g_post, wout)


def _out_bwd(dy, mixed, o, g_sb, g_ch, wout, *, name):
    t = o.shape[0]
    tm = 512
    ni = t // tm
    half = D_MODEL // 2

    def body(dy_ref, mixed_ref, o_ref, gsb_ref, gch_ref, w_ref,
             dw_ref, do_ref, dgsb_ref, dgch_ref, acc_ref):
        i = pl.program_id(0)

        @pl.when(i == 0)
        def _():
            acc_ref[...] = jnp.zeros_like(acc_ref)
            dgsb_ref[...] = jnp.zeros_like(dgsb_ref)
            dgch_ref[...] = jnp.zeros_like(dgch_ref)

        dyv = dy_ref[...]
        acc_ref[...] += _dot_tn(mixed_ref[...], dyv)
        dm = _dot_nt(dyv, w_ref[...])
        ov = o_ref[...]
        doa, dga = _rms_bwd(dm[:, :half], ov[:, :half], gsb_ref[...])
        dob, dgb = _rms_bwd(dm[:, half:], ov[:, half:], gch_ref[...])
        do_ref[...] = jnp.concatenate([doa, dob], axis=1)
        dgsb_ref[...] += dga
        dgch_ref[...] += dgb

        @pl.when(i == ni - 1)
        def _():
            dw_ref[...] = _bf(acc_ref[...])

    row = pl.BlockSpec((tm, D_MODEL), lambda i: (i, 0))
    gain = pl.BlockSpec((1, half), lambda i: (0, 0))
    sq = pl.BlockSpec((D_MODEL, D_MODEL), lambda i: (0, 0))
    return pl.pallas_call(
        body, name=name, grid=(ni,),
        in_specs=[row, row, row, gain, gain, sq],
        out_specs=[sq, row, gain, gain],
        out_shape=[_sds((D_MODEL, D_MODEL), BF16), _sds((t, D_MODEL)),
                   _sds((1, half)), _sds((1, half))],
        scratch_shapes=[pltpu.VMEM((D_MODEL, D_MODEL), F32)],
        compiler_params=_params("arbitrary"),
    )(dy, mixed, o, g_sb, g_ch, wout)


def _ple(p, h3, target, wp, wgate, g, f_post, g_post, *, name):
    t = h3.shape[0]
    tm = 512
    ni = t // tm

    def body(p_ref, h_ref, tgt_ref, wp_ref, wg_ref, g_ref, f_ref, gf_ref,
             loss_ref, dres_ref, dwp_ref, dwg_ref, dg_ref, df_ref, dgf_ref, accp, accg):
        i = pl.program_id(0)

        @pl.when(i == 0)
        def _():
            loss_ref[...] = jnp.zeros_like(loss_ref)
            dg_ref[...] = jnp.zeros_like(dg_ref)
            dgf_ref[...] = jnp.zeros_like(dgf_ref)
            accp[...] = jnp.zeros_like(accp)
            accg[...] = jnp.zeros_like(accg)

        pv, hv, gv = p_ref[...], h_ref[...], g_ref[...]
        pe = _dot(pv, wp_ref[...])
        sig = _sigmoid(_dot(hv, wg_ref[...]))
        e = pe * sig
        err = hv + _rms(e, gv) - tgt_ref[...]
        tok = jnp.mean(err * err, axis=-1, keepdims=True)
        loss_ref[...] += 0.5 * jnp.sum(tok, axis=0, keepdims=True)
        dh4 = err * (1.0 / D_MODEL)
        de, dg = _rms_bwd(dh4, e, gv)
        dg_ref[...] += dg
        dpe = de * sig
        dgt = de * pe * sig * (1.0 - sig)
        accp[...] += _dot_tn(pv, dpe)
        accg[...] += _dot_tn(hv, dgt)
        dres = dh4 + _dot_nt(dgt, wg_ref[...])
        dres_ref[...] = dres
        df, dgf = _rms_bwd(0.5 * dres, f_ref[...], gf_ref[...])
        df_ref[...] = _bf(df)
        dgf_ref[...] += dgf

        @pl.when(i == ni - 1)
        def _():
            dwp_ref[...] = _bf(accp[...])
            dwg_ref[...] = _bf(accg[...])

    row = pl.BlockSpec((tm, D_MODEL), lambda i: (i, 0))
    const = lambda r, c: pl.BlockSpec((r, c), lambda i: (0, 0))
    return pl.pallas_call(
        body, name=name, grid=(ni,),
        in_specs=[pl.BlockSpec((tm, PLE_DIM), lambda i: (i, 0)), row, row,
                  const(PLE_DIM, D_MODEL), const(D_MODEL, D_MODEL), const(1, D_MODEL),
                  row, const(1, D_MODEL)],
        out_specs=[const(1, 128), row, const(PLE_DIM, D_MODEL), const(D_MODEL, D_MODEL),
                   const(1, D_MODEL), row, const(1, D_MODEL)],
        out_shape=[_sds((1, 128)), _sds((t, D_MODEL)), _sds((PLE_DIM, D_MODEL), BF16),
                   _sds((D_MODEL, D_MODEL), BF16), _sds((1, D_MODEL)),
                   _sds((t, D_MODEL), BF16), _sds((1, D_MODEL))],
        scratch_shapes=[pltpu.VMEM((PLE_DIM, D_MODEL), F32), pltpu.VMEM((D_MODEL, D_MODEL), F32)],
        compiler_params=_params("arbitrary"),
    )(p, h3, target, wp, wgate, g, f_post, g_post)


def _rel_bias_to_fvec(rel_bias):
    rev = rel_bias[:, ::-1]
    n_heads = rel_bias.shape[0]
    first = CH_LOOK - REL_CLIP
    n_var = REL_CLIP + CHUNK
    clipped = rev[:, :1]
    fvec = jnp.concatenate([jnp.broadcast_to(clipped, (n_heads, first)), rev[:, :n_var],
                            jnp.broadcast_to(clipped, (n_heads, CH_WIN - first - n_var))], axis=1)
    return fvec.reshape(n_heads, 1, CH_WIN)


def _fvec_grad_to_rel_bias(dfvec):
    first = CH_LOOK - REL_CLIP
    n_var = REL_CLIP + CHUNK
    rev = jnp.pad(dfvec[:, 0, first:first + n_var], ((0, 0), (0, N_REL - n_var)))
    return rev[:, ::-1]


def _local_step(x, p, target, g, weights_for, grads_done, fvec, weights_early=None):
    bias = _bias_expand(fvec, name="bias_expand")
    w, tie = weights_for(0, bias)
    w = dict(w)
    h1, n1, a1, b1, f1 = _ffn_fwd(x, g["ffn1_pre"] + tie, g["ffn1_post"],
                                  w["ffn1_gate"], w["ffn1_up"], w["ffn1_down"], name="ffn1_fwd")
    more, tie = weights_for(1, h1)
    w.update(more)
    qkv, u = _qkv_fwd(h1, g["mix_pre"] + tie, w["in"], name="qkv_fwd")
    o, ltot = _sb_fwd(qkv, name="sb_fwd")
    tie = weights_early(2, ltot) if weights_early else 0.0
    o = _ch_fwd(qkv, bias, o, name="ch_fwd")
    w.update(weights_for(2, o)[0])
    h2, mixed, y = _out_fwd(o, h1, g["out_sb"] + tie, g["out_ch"], g["mix_post"], w["out"],
                            name="out_fwd")
    h3, n2, a2, b2, f2 = _ffn_fwd(h2, g["ffn2_pre"], g["ffn2_post"],
                                  w["ffn2_gate"], w["ffn2_up"], w["ffn2_down"], name="ffn2_fwd")
    loss, dh3, dwp, dwgate, dg_ple, df2, dg_ffn2_post = _ple(
        p, h3, target, w["ple_proj"], w["ple_gate"], g["ple_post"], f2, g["ffn2_post"], name="ple")
    tie = grads_done(0, {"ple_proj": dwp, "ple_gate": dwgate})
    dwg2, dwu2, dwd2, dn2 = _ffn_bwd(n2, df2, a2, b2, w["ffn2_gate"], w["ffn2_up"],
                                     w["ffn2_down"], name="ffn2_bwd")
    tie = tie + grads_done(1, {"ffn2_gate": dwg2, "ffn2_up": dwu2, "ffn2_down": dwd2})
    dh2, dg_ffn2_pre, dy, dg_mix_post = _junction(
        dh3, pre=(dn2, h2, g["ffn2_pre"] + tie), post=(y, g["mix_post"], 1.0), name="junction2")
    dwout, do, dg_sb, dg_ch = _out_bwd(dy, mixed, o, g["out_sb"], g["out_ch"], w["out"],
                                       name="out_bwd")
    dq, dk, dv = _sb_bwd(qkv, ltot, do, name="sb_bwd")
    dq, dk, dv, dbias = _ch_bwd(qkv, bias, o, do, dq, dk, dv, name="ch_bwd")
    dwin, du = _qkv_bwd(dq, dk, dv, u, w["in"], name="qkv_bwd")
    tie = grads_done(2, {"out": dwout, "in": dwin})
    dh1, dg_mix_pre, df1, dg_ffn1_post = _junction(
        dh2, pre=(du, h1, g["mix_pre"] + tie), post=(f1, g["ffn1_post"], 0.5), name="junction1")
    dwg1, dwu1, dwd1, dn1 = _ffn_bwd(n1, df1, a1, b1, w["ffn1_gate"], w["ffn1_up"],
                                     w["ffn1_down"], name="ffn1_bwd")
    tie = grads_done(3, {"ffn1_gate": dwg1, "ffn1_up": dwu1, "ffn1_down": dwd1})
    dx, dg_ffn1_pre = _junction(dh1, pre=(dn1, x, g["ffn1_pre"] + tie), name="junction0")

    dg = {"ffn1_pre": dg_ffn1_pre, "ffn1_post": dg_ffn1_post, "mix_pre": dg_mix_pre,
          "mix_post": dg_mix_post, "out_sb": dg_sb, "out_ch": dg_ch,
          "ffn2_pre": dg_ffn2_pre, "ffn2_post": dg_ffn2_post, "ple_post": dg_ple}
    return loss, dx, dg, dbias


_WEIGHTS = (
    ("ffn1_gate", "row", FF_SHARD, FF_SHARD_PAD, D_MODEL),
    ("ffn1_up", "row", FF_SHARD, FF_SHARD_PAD, D_MODEL),
    ("ffn1_down", "row", FF_SHARD, FF_SHARD_PAD, D_MODEL),
    ("in", "col", QKV_SHARD, QKV_SHARD, D_MODEL),
    ("out", "row", ROW_SHARD, ROW_SHARD, D_MODEL),
    ("ffn2_gate", "row", FF_SHARD, FF_SHARD_PAD, D_MODEL),
    ("ffn2_up", "row", FF_SHARD, FF_SHARD_PAD, D_MODEL),
    ("ffn2_down", "row", FF_SHARD, FF_SHARD_PAD, D_MODEL),
    ("ple_proj", "col", ROW_SHARD, ROW_SHARD, PLE_DIM),
    ("ple_gate", "row", ROW_SHARD, ROW_SHARD, D_MODEL),
)
_TRANSPOSED = ("ffn1_gate", "ffn1_up", "ffn2_gate", "ffn2_up")
_SPEC = {n: (kind, valid, pad, other) for n, kind, valid, pad, other in _WEIGHTS}
_GATHER_STAGES = (("ffn1_gate", "ffn1_up", "ffn1_down"), ("in",),
                  ("out", "ffn2_gate", "ffn2_up", "ffn2_down", "ple_proj", "ple_gate"))
_SCATTER_STAGES = (("ple_proj", "ple_gate"), ("ffn2_gate", "ffn2_up", "ffn2_down"),
                   ("out", "in"), ("ffn1_gate", "ffn1_up", "ffn1_down"))
HBM = pl.BlockSpec(memory_space=pltpu.HBM)
SEM = pl.BlockSpec(memory_space=pltpu.SEMAPHORE)
EFFECT = pltpu.SideEffectType.DATAFLOW_SIDE_EFFECTING


def _shard_shape(kind, size, other):
    return (other, size) if kind == "col" else (size, other)


def _window(ref, kind, start, size):
    return ref.at[:, pl.ds(start, size)] if kind == "col" else ref.at[pl.ds(start, size), :]


def _device_tuple(k):
    return (k // 4, (k // 2) % 2, k % 2)


def _my_index():
    return 4 * lax.axis_index("x") + 2 * lax.axis_index("y") + lax.axis_index("c")


def _pack_weights(shards):
    nw = len(_WEIGHTS)

    def body(*refs):
        ins, packed, full = refs[:nw], refs[nw:2 * nw], refs[2 * nw:3 * nw]
        sem = refs[3 * nw]
        me = _my_index()
        for (_, kind, valid, pad, _), src, dst in zip(_WEIGHTS, ins, packed):
            if pad != valid:
                dst[...] = jnp.zeros_like(dst)
            if kind == "col":
                dst[:, pl.ds(0, valid)] = _bf(src[...])
            else:
                dst[pl.ds(0, valid), :] = _bf(src[...])
        for k in range(N_DEV):
            @pl.when(me == k)
            def _():
                for w, (_, kind, _, pad, _) in enumerate(_WEIGHTS):
                    pltpu.make_async_copy(packed[w], _window(full[w], kind, k * pad, pad),
                                          sem.at[w]).start()
        for w, (_, kind, _, pad, _) in enumerate(_WEIGHTS):
            pltpu.make_async_copy(packed[w], _window(full[w], kind, 0, pad), sem.at[w]).wait()

    whole = lambda shape: pl.BlockSpec(shape, lambda i: (0, 0))
    packed_shapes = [_shard_shape(kind, pad, other) for _, kind, _, pad, other in _WEIGHTS]
    outs = pl.pallas_call(
        body, name="pack_weights", grid=(1,),
        in_specs=[whole(a.shape) for a in shards],
        out_specs=[whole(s) for s in packed_shapes] + [ANY] * nw,
        out_shape=[_sds(s, BF16) for s in packed_shapes]
        + [_sds(_shard_shape(kind, N_DEV * pad, other), BF16) for _, kind, _, pad, other in _WEIGHTS],
        scratch_shapes=[pltpu.SemaphoreType.DMA((nw,))],
        compiler_params=_params("arbitrary"),
    )(*shards)
    names = [n for n, *_ in _WEIGHTS]
    return dict(zip(names, outs[:nw])), dict(zip(names, outs[nw:]))


def _hbm(a):
    return pltpu.with_memory_space_constraint(a, pltpu.HBM)


def _split_start(name, n, body_copies, sources, lands, after):
    arrays = list(sources) + list(lands)
    ns, na = len(sources), len(arrays)

    def body(*refs):
        src, land = refs[:ns], refs[ns:na]
        send, recv = refs[na + 1], refs[na + 2]
        token = refs[-1]
        body_copies(src, land, send, recv)
        token[...] = jnp.zeros_like(token)

    out = pl.pallas_call(
        body, name=name,
        out_shape=(pltpu.SemaphoreType.DMA((n,)), pltpu.SemaphoreType.DMA((n,)),
                   *[pltpu.HBM(a.shape, a.dtype) for a in arrays], _sds((8, 128))),
        in_specs=[HBM] * na + [ANY], out_specs=(SEM, SEM, *[HBM] * na, VMEM),
        input_output_aliases={i: 2 + i for i in range(na)},
        compiler_params=pltpu.CompilerParams(has_side_effects=EFFECT),
    )(*[_hbm(a) for a in arrays], after)
    return out[0], out[1], out[2:2 + ns], out[2 + ns:2 + na], out[-1]


def _split_wait(name, n, seven_of, send, recv, sources, lands, after, keep_sources=False):
    arrays = list(sources) + list(lands)
    ns, na = len(sources), len(arrays)

    def body(*refs):
        land = refs[ns:na]
        send_ref, recv_ref = refs[na], refs[na + 1]
        myself = (lax.axis_index("x"), lax.axis_index("y"), lax.axis_index("c"))
        for w in range(n):
            seven = seven_of(w, land[w])
            copy = pltpu.make_async_remote_copy(
                src_ref=seven, dst_ref=seven, send_sem=send_ref.at[w], recv_sem=recv_ref.at[w],
                device_id=myself, device_id_type=MESH)
            copy.wait_send()
            copy.wait_recv()

    out = pl.pallas_call(
        body, name=name,
        out_shape=[pltpu.HBM(a.shape, a.dtype) for a in arrays],
        in_specs=[HBM] * na + [SEM, SEM, ANY], out_specs=[HBM] * na,
        input_output_aliases={i: i for i in range(na)},
        compiler_params=pltpu.CompilerParams(has_side_effects=EFFECT),
    )(*arrays, send, recv, after)
    return out if keep_sources else out[ns:]


_ALL_PEERS = (1, 2, 3, 4, 5, 6, 7)
_NEAR_PEERS = (1, 2, 4, 6)
_FAR_CHIPS = (2, 4, 6)


def _gather_start(stage, names, packed, full, after, peers=_ALL_PEERS):
    def copies(src, land, send, recv):
        me = _my_index()
        for k in range(N_DEV):
            @pl.when(me == k)
            def _():
                for w, name in enumerate(names):
                    kind, _, pad, _ = _SPEC[name]
                    dst = _window(land[w], kind, k * pad, pad)
                    for mask in peers:
                        pltpu.make_async_remote_copy(
                            src_ref=src[w], dst_ref=dst, send_sem=send.at[w],
                            recv_sem=recv.at[w], device_id=_device_tuple(k ^ mask),
                            device_id_type=MESH).start()

    return _split_start(f"gather_start{stage}", len(names), copies,
                        [packed[n] for n in names], [full[n] for n in names], after)


def _gather_wait(stage, names, started, after, count=N_DEV - 1):
    send, recv, src, land, _ = started

    def bytes_of(w, ref):
        kind, _, pad, _ = _SPEC[names[w]]
        return _window(ref, kind, 0, count * pad)

    return dict(zip(names, _split_wait(f"gather_wait{stage}", len(names), bytes_of,
                                       send, recv, src, land, after)))


def _relay_start(stage, names, full, after):
    def copies(_, land, send, recv):
        me = _my_index()
        for k in range(N_DEV):
            @pl.when(me == k)
            def _():
                for w, name in enumerate(names):
                    kind, _, pad, _ = _SPEC[name]
                    for mask in _FAR_CHIPS:
                        win = _window(land[w], kind, (k ^ mask) * pad, pad)
                        pltpu.make_async_remote_copy(
                            src_ref=win, dst_ref=win, send_sem=send.at[w], recv_sem=recv.at[w],
                            device_id=_device_tuple(k ^ 1), device_id_type=MESH).start()

    return _split_start(f"relay_start{stage}", len(names), copies, [],
                        [full[n] for n in names], after)


def _scatter_start(stage, names, grads, after):
    def copies(src, land, send, recv):
        me = _my_index()
        for k in range(N_DEV):
            @pl.when(me != k)
            def _():
                slot = lax.rem(me + (N_DEV - 1 - k), N_DEV)
                for w, name in enumerate(names):
                    kind, _, pad, _ = _SPEC[name]
                    pltpu.make_async_remote_copy(
                        src_ref=_window(src[w], kind, k * pad, pad), dst_ref=land[w].at[slot],
                        send_sem=send.at[w], recv_sem=recv.at[w],
                        device_id=_device_tuple(k), device_id_type=MESH).start()

    lands = [lax.empty((N_DEV - 1,) + _shard_shape(_SPEC[m][0], _SPEC[m][2], _SPEC[m][3]), BF16)
             for m in names]
    return _split_start(f"scatter_start{stage}", len(names), copies, grads, lands, after)


def _scatter_wait(stage, names, started, after):
    send, recv, src, land, _ = started
    n = len(names)
    out = _split_wait(f"scatter_wait{stage}", n, lambda w, ref: ref, send, recv, src, land, after,
                      keep_sources=True)
    return dict(zip(names, out[:n])), dict(zip(names, out[n:]))


N_CHIPS = N_DEV // 2


def _pair_start(stage, names, grads, after):
    def copies(src, land, send, recv):
        me = _my_index()
        for k in range(N_DEV):
            @pl.when(me == k)
            def _():
                for w, name in enumerate(names):
                    kind, _, pad, _ = _SPEC[name]
                    for chip in range(N_CHIPS):
                        j = 2 * chip + ((k ^ 1) & 1)
                        pltpu.make_async_remote_copy(
                            src_ref=_window(src[w], kind, j * pad, pad), dst_ref=land[w].at[chip],
                            send_sem=send.at[w], recv_sem=recv.at[w],
                            device_id=_device_tuple(k ^ 1), device_id_type=MESH).start()

    lands = [lax.empty((N_CHIPS,) + _shard_shape(_SPEC[m][0], _SPEC[m][2], _SPEC[m][3]), BF16)
             for m in names]
    return _split_start(f"pair_start{stage}", len(names), copies, grads, lands, after)


def _pair_sum(dw_full, pair, *, pad, name):
    other = dw_full.shape[1]

    def body(own_ref, pair_ref, out_ref):
        out_ref[0] = _bf(own_ref[...].astype(F32) + pair_ref[0].astype(F32))

    slot = pl.BlockSpec((1, pad, other), lambda q: (q, 0, 0))
    return pl.pallas_call(
        body, name=name, grid=(N_CHIPS,),
        in_specs=[pl.BlockSpec((pad, other), lambda q: (2 * q + lax.axis_index("c"), 0)), slot],
        out_specs=slot, out_shape=_sds((N_CHIPS, pad, other), BF16),
        compiler_params=_params("arbitrary"),
    )(dw_full, pair)


def _chip_start(stage, names, sums, after):
    def copies(src, land, send, recv):
        me = _my_index()
        my_chip = lax.shift_right_logical(me, 1)
        for k in range(N_DEV):
            @pl.when((me != k) & (((me ^ k) & 1) == 0))
            def _():
                slot = lax.rem(my_chip + (N_CHIPS - 1 - k // 2), N_CHIPS)
                for w in range(len(names)):
                    pltpu.make_async_remote_copy(
                        src_ref=src[w].at[k // 2], dst_ref=land[w].at[slot],
                        send_sem=send.at[w], recv_sem=recv.at[w],
                        device_id=_device_tuple(k), device_id_type=MESH).start()

    lands = [lax.empty((N_CHIPS - 1,) + a.shape[1:], BF16) for a in sums]
    return _split_start(f"chip_start{stage}", len(names), copies, sums, lands, after)


def _adamw_chip(w, m, v, land, sums, *, name):
    shape = w.shape

    def body(w_ref, m_ref, v_ref, land_ref, own_ref, *outs):
        rows = pl.ds(0, shape[0])
        grad = own_ref[0, rows, :].astype(F32)
        for s in range(N_CHIPS - 1):
            grad = grad + land_ref[s, rows, :].astype(F32)
        _adam_update(w_ref, m_ref, v_ref, grad, *outs)

    whole = lambda a: pl.BlockSpec(a.shape, lambda i: (0,) * a.ndim)
    own = pl.BlockSpec((1,) + sums.shape[1:],
                       lambda i: (2 * lax.axis_index("x") + lax.axis_index("y"), 0, 0))
    return pl.pallas_call(
        body, name=name, grid=(1,),
        in_specs=[whole(w), whole(m), whole(v), whole(land), own],
        out_specs=[whole(w)] * 4, out_shape=[_sds(shape)] * 4,
        compiler_params=_params("arbitrary"),
    )(w, m, v, land, sums)


def _allreduce_small(small, after):
    shape = small.shape

    def body(in_ref, _after, out_ref, gath, send, recv):
        me = _my_index()
        for k in range(N_DEV):
            @pl.when(me != k)
            def _():
                pltpu.make_async_remote_copy(
                    src_ref=in_ref, dst_ref=gath.at[me], send_sem=send, recv_sem=recv,
                    device_id=_device_tuple(k), device_id_type=MESH).start()

            @pl.when(me == k)
            def _():
                gath[k] = in_ref[...]
        seven = gath.at[pl.ds(0, N_DEV - 1)]
        pltpu.make_async_remote_copy(
            src_ref=seven, dst_ref=seven, send_sem=send, recv_sem=recv,
            device_id=_device_tuple(0), device_id_type=MESH).wait()
        total = gath[0]
        for s in range(1, N_DEV):
            total = total + gath[s]
        out_ref[...] = total

    return pl.pallas_call(
        body, name="allreduce_small",
        in_specs=[VMEM, ANY], out_specs=VMEM, out_shape=_sds(shape),
        scratch_shapes=[pltpu.VMEM((N_DEV,) + shape, F32),
                        pltpu.SemaphoreType.DMA, pltpu.SemaphoreType.DMA],
    )(small, after)


def _adam_update(w_ref, m_ref, v_ref, grad, grad_ref, delta_ref, nm_ref, nv_ref):
    new_m = ADAM_B1 * m_ref[...] + (1.0 - ADAM_B1) * grad
    new_v = ADAM_B2 * v_ref[...] + (1.0 - ADAM_B2) * (grad * grad)
    m_hat = new_m / (1.0 - ADAM_B1 ** ADAM_STEP)
    v_hat = new_v / (1.0 - ADAM_B2 ** ADAM_STEP)
    grad_ref[...] = grad
    delta_ref[...] = -ADAM_LR * (m_hat / (jnp.sqrt(v_hat) + ADAM_EPS) + ADAM_WD * w_ref[...])
    nm_ref[...] = new_m
    nv_ref[...] = new_v


def _adamw(w, m, v, g, *, name):
    def body(w_ref, m_ref, v_ref, g_ref, *outs):
        _adam_update(w_ref, m_ref, v_ref, g_ref[...], *outs)

    whole = pl.BlockSpec(w.shape, lambda i: (0,) * w.ndim)
    return pl.pallas_call(
        body, name=name, grid=(1,), in_specs=[whole] * 4, out_specs=[whole] * 4,
        out_shape=[_sds(w.shape)] * 4, compiler_params=_params("arbitrary"),
    )(w, m, v, g)


def _adamw_gains(small, params):
    n = len(params)

    def body(small_ref, *refs):
        ins, outs = refs[:3 * n], refs[3 * n:]
        for r in range(n):
            width = ins[3 * r].shape[1]
            if width == D_MODEL:
                grad = small_ref[pl.ds(r, 1), :]
            else:
                grad = small_ref[pl.ds(len(_GAINS), 1), pl.ds((r - len(_GAINS)) * width, width)]
            _adam_update(*ins[3 * r:3 * r + 3], grad, *outs[4 * r:4 * r + 4])

    whole = lambda a: pl.BlockSpec(a.shape, lambda i: (0, 0))
    flat = [a for group in params for a in group]
    return pl.pallas_call(
        body, name="adamw_gains", grid=(1,),
        in_specs=[whole(small)] + [whole(a) for a in flat],
        out_specs=[whole(w) for w, _, _ in params for _ in range(4)],
        out_shape=[_sds(w.shape) for w, _, _ in params for _ in range(4)],
        compiler_params=_params("arbitrary"),
    )(small, *flat)


def _adamw_shard(w, m, v, land, dw_full, *, kind, pad, name):
    shape = w.shape
    other = shape[0] if kind == "col" else shape[1]

    def body(w_ref, m_ref, v_ref, land_ref, own_ref, *outs):
        valid = ((slice(None), pl.ds(0, shape[1])) if kind == "col"
                 else (pl.ds(0, shape[0]), slice(None)))
        grad = own_ref[valid].astype(F32)
        for s in range(N_DEV - 1):
            grad = grad + land_ref[(s,) + valid].astype(F32)
        _adam_update(w_ref, m_ref, v_ref, grad, *outs)

    whole = lambda a: pl.BlockSpec(a.shape, lambda i: (0,) * a.ndim)
    own = pl.BlockSpec(_shard_shape(kind, pad, other),
                       (lambda i: (0, _my_index())) if kind == "col" else (lambda i: (_my_index(), 0)))
    return pl.pallas_call(
        body, name=name, grid=(1,),
        in_specs=[whole(w), whole(m), whole(v), whole(land), own],
        out_specs=[whole(w)] * 4, out_shape=[_sds(shape)] * 4,
        compiler_params=_params("arbitrary"),
    )(w, m, v, land, dw_full)


_GAINS = ("ffn1_pre", "ffn1_post", "mix_pre", "mix_post", "ffn2_pre", "ffn2_post", "ple_post")
_SMALL_ROWS = 16


def _stack_gains(get):
    return jnp.concatenate([get(n) for n in _GAINS]
                           + [jnp.concatenate([get("out_sb"), get("out_ch")], axis=1)], axis=0)


def kernel(x, p, g_ffn1_pre, g_ffn1_post, w_ffn1_gate, w_ffn1_up, w_ffn1_down, g_mix_pre, g_mix_post, w_in, g_out_sb, g_out_ch, rel_bias, w_out, g_ffn2_pre, g_ffn2_post, w_ffn2_gate, w_ffn2_up, w_ffn2_down, w_ple_proj, w_ple_gate, g_ple_post, loss_target, m_g_ffn1_pre, m_g_ffn1_post, m_w_ffn1_gate, m_w_ffn1_up, m_w_ffn1_down, m_g_mix_pre, m_g_mix_post, m_w_in, m_g_out_sb, m_g_out_ch, m_rel_bias, m_w_out, m_g_ffn2_pre, m_g_ffn2_post, m_w_ffn2_gate, m_w_ffn2_up, m_w_ffn2_down, m_w_ple_proj, m_w_ple_gate, m_g_ple_post, v_g_ffn1_pre, v_g_ffn1_post, v_w_ffn1_gate, v_w_ffn1_up, v_w_ffn1_down, v_g_mix_pre, v_g_mix_post, v_w_in, v_g_out_sb, v_g_out_ch, v_rel_bias, v_w_out, v_g_ffn2_pre, v_g_ffn2_post, v_w_ffn2_gate, v_w_ffn2_up, v_w_ffn2_down, v_w_ple_proj, v_w_ple_gate, v_g_ple_post):
    given = dict(locals())
    wnames = [n for n, *_ in _WEIGHTS]

    def shard(prefix, n):
        a = given[prefix + "w_" + n][0]
        return a.T if n in _TRANSPOSED else a

    packed, full = _pack_weights([shard("", n) for n in wnames])
    first = _GATHER_STAGES[0]
    anchor = x[0]
    two_level = (0, 2)
    gathers = {}

    def start_stage(stage, after):
        peers = _NEAR_PEERS if stage in two_level else _ALL_PEERS
        gathers[stage] = _gather_start(stage, _GATHER_STAGES[stage], packed, full, after,
                                       peers=peers)

    start_stage(0, anchor)

    relays = {}

    def first_level(stage, after):
        names = _GATHER_STAGES[stage]
        last_stage = stage + 1 == len(_GATHER_STAGES)
        count = len(_NEAR_PEERS) if stage in two_level else N_DEV - 1
        ws = _gather_wait(stage, names, gathers[stage], after, count=count)
        if not last_stage:
            start_stage(stage + 1, ws[names[0]])
        if stage in two_level:
            relays[stage] = _relay_start(stage, names, ws,
                                         anchor if last_stage else gathers[stage + 1][-1])
            return ws, relays[stage][-1]
        return ws, None if last_stage else gathers[stage + 1][-1]

    def weights_early(stage, after):
        return first_level(stage, after)[1][:1, :1]

    def weights_for(stage, after):
        names = _GATHER_STAGES[stage]
        ws, token = (None, None) if stage in relays else first_level(stage, after)
        if stage in relays:
            relay = relays[stage]
            ws = _gather_wait(f"{stage}r", names, relay, after, count=len(_FAR_CHIPS))
            token = None if stage + 1 == len(_GATHER_STAGES) else gathers[stage + 1][-1]
        return ws, jnp.zeros((1, 1), F32) if token is None else token[:1, :1]

    scatters = {}

    last = len(_SCATTER_STAGES) - 1

    def grads_done(stage, grads):
        names = _SCATTER_STAGES[stage]
        start = _pair_start if stage == last else _scatter_start
        scatters[stage] = start(stage, names, [grads[n] for n in names], anchor)
        return scatters[stage][-1][:1, :1]

    gains = {n: given["g_" + n] for n in _GAINS + ("out_sb", "out_ch")}
    fvec = _rel_bias_to_fvec(rel_bias[0])
    loss, dx, dg, dbias = _local_step(x[0], p[0, 0], loss_target[0], gains,
                                      weights_for, grads_done, fvec, weights_early)

    results = {}

    def finish(stage, after):
        names = _SCATTER_STAGES[stage]
        dws, lands = _scatter_wait(stage, names, scatters[stage], after)
        for n in names:
            kind, _, pad, _ = _SPEC[n]
            out = _adamw_shard(shard("", n), shard("m_", n), shard("v_", n), lands[n], dws[n],
                               kind=kind, pad=pad, name="adamw_" + n)
            results["w_" + n] = [a.T for a in out] if n in _TRANSPOSED else out
        return results["w_" + names[-1]][0]

    names = _SCATTER_STAGES[last]
    whole = lambda w, ref: ref
    send, recv, src, land, _ = scatters[last]
    out = _split_wait(f"pair_wait{last}", len(names), whole, send, recv, src, land, dx,
                      keep_sources=True)
    sums = [_pair_sum(dwf, pair, pad=_SPEC[n][2], name="pair_sum_" + n)
            for n, dwf, pair in zip(names, out[:len(names)], out[len(names):])]
    send, recv, src, land, after = _chip_start(last, names, sums, anchor)
    for stage in range(last):
        after = finish(stage, after)
    dfvec = _bias_grad(dbias, after, name="bias_grad")
    loss_col = jnp.pad(loss[:, :1], ((0, N_DEV - 1), (0, D_MODEL - CH_WIN - 1)))
    dfv = jnp.concatenate([dfvec[:, 0, :], loss_col], axis=1)
    small = _allreduce_small(jnp.concatenate([_stack_gains(lambda n: dg[n]), dfv], axis=0), after)
    gain_names = _GAINS + ("out_sb", "out_ch")
    gain_out = _adamw_gains(small, [(given["g_" + n], given["m_g_" + n], given["v_g_" + n])
                                    for n in gain_names])
    for r, n in enumerate(gain_names):
        results["g_" + n] = gain_out[4 * r:4 * r + 4]
    d_rel = _fvec_grad_to_rel_bias(small[N_DEV:, :CH_WIN].reshape(N_DEV, 1, CH_WIN))
    results["rel_bias"] = _adamw(rel_bias[0], m_rel_bias[0], v_rel_bias[0], d_rel,
                                 name="adamw_rel_bias")
    out = _split_wait(f"chip_wait{last}", len(names), whole, send, recv, src, land,
                      results["rel_bias"][0], keep_sources=True)
    for n, own, landed in zip(names, out[:len(names)], out[len(names):]):
        res = _adamw_chip(shard("", n), shard("m_", n), shard("v_", n), landed, own,
                          name="adamw_" + n)
        results["w_" + n] = [a.T for a in res] if n in _TRANSPOSED else res

    order = ("g_ffn1_pre", "g_ffn1_post", "w_ffn1_gate", "w_ffn1_up", "w_ffn1_down",
             "g_mix_pre", "g_mix_post", "w_in", "g_out_sb", "g_out_ch", "rel_bias", "w_out",
             "g_ffn2_pre", "g_ffn2_post", "w_ffn2_gate", "w_ffn2_up", "w_ffn2_down",
             "w_ple_proj", "w_ple_gate", "g_ple_post")

    def leaf(name, idx):
        a = results[name][idx]
        return a if name.startswith("g_") else a[None]

    total_loss = small[N_DEV, CH_WIN]
    return (total_loss, dx[None],
            *[leaf(n, 0) for n in order], *[leaf(n, 1) for n in order],
            *[leaf(n, 2) for n in order], *[leaf(n, 3) for n in order])
```

```python
import jax
import jax.numpy as jnp
from jax import lax
from jax.experimental import pallas as pl
from jax.experimental.pallas import tpu as pltpu

F32 = jnp.float32
BF16 = jnp.bfloat16

N_DEV = 8
D_MODEL = 1024
D_FF = 2816
FF_SHARD = D_FF // N_DEV
FF_SHARD_PAD = 384
D_FF_PAD = FF_SHARD_PAD * N_DEV
QKV_WIDTH = 3 * D_MODEL
QKV_SHARD = QKV_WIDTH // N_DEV
PLE_DIM = 256
ROW_SHARD = D_MODEL // N_DEV
HEAD_DIM = 64
PAIR = 2 * HEAD_DIM
N_PAIRS = 4
CHUNK = 64
LOOKBACK = 8
REL_CLIP = 128
N_REL = 2 * REL_CLIP + 1
CH_QB = 256
CH_LOOK = LOOKBACK * CHUNK
CH_WIN = CH_LOOK + CH_QB
SB_QB = 512
SB_KB = 256
SB_GROUP = 2
SB_LANES = tuple(slice(g * 128, (g + 1) * 128) for g in range(SB_GROUP))
EPS = 1e-6
NEG_INF = -1e30
ATT_SCALE = HEAD_DIM ** -0.5
ADAM_LR = 0.001
ADAM_B1 = 0.9
ADAM_B2 = 0.999
ADAM_EPS = 1e-08
ADAM_WD = 0.01
ADAM_STEP = 10
VMEM_LIMIT_BYTES = 48 * 1024 * 1024
MESH = pl.DeviceIdType.MESH

ANY = pl.BlockSpec(memory_space=pl.ANY)
VMEM = pl.BlockSpec(memory_space=pltpu.VMEM)


def _params(*sem):
    return pltpu.CompilerParams(dimension_semantics=sem or None,
                                vmem_limit_bytes=VMEM_LIMIT_BYTES)


def _sds(shape, dtype=F32):
    return jax.ShapeDtypeStruct(shape, dtype)


def _bf(x):
    return x.astype(BF16)


def _dot(a, b):
    return jnp.dot(_bf(a), _bf(b), preferred_element_type=F32)


def _dot_nt(a, b):
    return lax.dot_general(_bf(a), _bf(b), (((1,), (1,)), ((), ())),
                           preferred_element_type=F32)


def _dot_tn(a, b):
    return lax.dot_general(_bf(a), _bf(b), (((0,), (0,)), ((), ())),
                           preferred_element_type=F32)


def _sigmoid(x):
    return 1.0 / (1.0 + jnp.exp(-x))


def _softplus(x):
    return jnp.maximum(x, 0.0) + jnp.log(1.0 + jnp.exp(-jnp.abs(x)))


def _rstd(x):
    return lax.rsqrt(jnp.mean(x * x, axis=-1, keepdims=True) + EPS)


def _rms(x, g):
    return x * _rstd(x) * g


def _rms_bwd(dy, x, g):
    r = _rstd(x)
    w = dy * g
    dx = r * (w - x * (r * r) * jnp.mean(w * x, axis=-1, keepdims=True))
    dg = jnp.sum(dy * (x * r), axis=0, keepdims=True)
    return dx, dg


def _head_masks():
    lane = lax.broadcasted_iota(jnp.int32, (1, PAIR), 1)
    return lane < HEAD_DIM, lane >= HEAD_DIM


def _ffn_fwd(x, g_pre, g_post, wg, wu, wd, *, name):
    t = x.shape[0]
    tm, tj = 512, 1024
    ni, nj = t // tm, D_FF_PAD // tj

    def body(x_ref, gpre_ref, gpost_ref, wg_ref, wu_ref, wd_ref,
             h_ref, n_ref, a_ref, b_ref, f_ref, acc_ref):
        j = pl.program_id(1)

        @pl.when(j == 0)
        def _():
            n_ref[...] = _bf(_rms(x_ref[...], gpre_ref[...]))
            acc_ref[...] = jnp.zeros_like(acc_ref)

        n = n_ref[...]
        a = _dot_nt(n, wg_ref[...])
        b = _dot_nt(n, wu_ref[...])
        a_ref[...] = a
        b_ref[...] = b
        hmid = a * _sigmoid(a) * b
        acc_ref[...] += jnp.dot(_bf(hmid), wd_ref[...], preferred_element_type=F32)

        @pl.when(j == nj - 1)
        def _():
            f = acc_ref[...]
            f_ref[...] = f
            h_ref[...] = x_ref[...] + 0.5 * _rms(f, gpost_ref[...])

    row = pl.BlockSpec((tm, D_MODEL), lambda i, j: (i, 0))
    gain = pl.BlockSpec((1, D_MODEL), lambda i, j: (0, 0))
    col = pl.BlockSpec((tm, tj), lambda i, j: (i, j))
    wtile = pl.BlockSpec((tj, D_MODEL), lambda i, j: (j, 0))
    return pl.pallas_call(
        body, name=name, grid=(ni, nj),
        in_specs=[row, gain, gain, wtile, wtile, wtile],
        out_specs=[row, row, col, col, row],
        out_shape=[_sds((t, D_MODEL)), _sds((t, D_MODEL), BF16),
                   _sds((t, D_FF_PAD)), _sds((t, D_FF_PAD)), _sds((t, D_MODEL))],
        scratch_shapes=[pltpu.VMEM((tm, D_MODEL), F32)],
        compiler_params=_params("arbitrary", "arbitrary"),
    )(x, g_pre, g_post, wg, wu, wd)


def _ffn_bwd(n, df, a, b, wg, wu, wd, *, name):
    t = n.shape[0]
    tj, tm, ts = 256, t, 512
    nj, ni, ns = D_FF_PAD // tj, t // tm, tm // ts

    def body(n_hbm, df_hbm, a_ref, b_ref, wg_ref, wu_ref, wd_ref,
             dwg_ref, dwu_ref, dwd_ref, dn_hbm,
             n_v, df_v, dn_v, ag, au, ad, sem):
        j, i = pl.program_id(0), pl.program_id(1)

        @pl.when((j == 0) & (i == 0))
        def _():
            c1 = pltpu.make_async_copy(n_hbm, n_v, sem.at[0])
            c2 = pltpu.make_async_copy(df_hbm, df_v, sem.at[1])
            c1.start()
            c2.start()
            dn_v[...] = jnp.zeros_like(dn_v)
            c1.wait()
            c2.wait()

        @pl.when(i == 0)
        def _():
            ag[...] = jnp.zeros_like(ag)
            au[...] = jnp.zeros_like(au)
            ad[...] = jnp.zeros_like(ad)

        wgj, wuj, wdj = wg_ref[...], wu_ref[...], wd_ref[...]
        for s in range(ns):
            local = pl.ds(s * ts, ts)
            rows = pl.ds(pl.multiple_of(i * tm + s * ts, ts), ts)
            av, bv = a_ref[local, :], b_ref[local, :]
            sig = _sigmoid(av)
            silu = av * sig
            dfr = df_v[rows, :]
            nr = n_v[rows, :]
            dhmid = _dot_nt(dfr, wdj)
            da = dhmid * bv * (sig * (1.0 + av * (1.0 - sig)))
            db = dhmid * silu
            ad[...] += _dot_tn(silu * bv, dfr)
            ag[...] += _dot_tn(da, nr)
            au[...] += _dot_tn(db, nr)
            dn_v[rows, :] += _dot(da, wgj) + _dot(db, wuj)

        @pl.when(i == ni - 1)
        def _():
            dwg_ref[...] = _bf(ag[...])
            dwu_ref[...] = _bf(au[...])
            dwd_ref[...] = _bf(ad[...])

        @pl.when((j == nj - 1) & (i == ni - 1))
        def _():
            c = pltpu.make_async_copy(dn_v, dn_hbm, sem.at[0])
            c.start()
            c.wait()

    roww = pl.BlockSpec((tj, D_MODEL), lambda j, i: (j, 0))
    act = pl.BlockSpec((tm, tj), lambda j, i: (i, j))
    return pl.pallas_call(
        body, name=name, grid=(nj, ni),
        in_specs=[ANY, ANY, act, act, roww, roww, roww],
        out_specs=[roww, roww, roww, ANY],
        out_shape=[_sds((D_FF_PAD, D_MODEL), BF16)] * 3 + [_sds((t, D_MODEL))],
        scratch_shapes=[pltpu.VMEM((t, D_MODEL), BF16), pltpu.VMEM((t, D_MODEL), BF16),
                        pltpu.VMEM((t, D_MODEL), F32)]
        + [pltpu.VMEM((tj, D_MODEL), F32)] * 3 + [pltpu.SemaphoreType.DMA((2,))],
        compiler_params=_params("arbitrary", "arbitrary"),
    )(n, df, a, b, wg, wu, wd)


def _junction(dres, pre=None, post=None, *, name):
    t = dres.shape[0]
    tm = 512
    ni = t // tm
    n_in = 1 + (3 if pre else 0) + (2 if post else 0)
    coef = post[2] if post else None

    def body(*refs):
        ins, outs = list(refs[:n_in]), list(refs[n_in:])
        i = pl.program_id(0)
        dh = ins.pop(0)[...]
        if pre:
            dn_ref, x_ref, gpre_ref = ins.pop(0), ins.pop(0), ins.pop(0)
            dh_ref, dgpre_ref = outs.pop(0), outs.pop(0)
            dx, dg = _rms_bwd(dn_ref[...], x_ref[...], gpre_ref[...])
            dh = dh + dx
            dh_ref[...] = dh

            @pl.when(i == 0)
            def _():
                dgpre_ref[...] = jnp.zeros_like(dgpre_ref)
            dgpre_ref[...] += dg
        if post:
            f_ref, gpost_ref = ins.pop(0), ins.pop(0)
            df_ref, dgpost_ref = outs.pop(0), outs.pop(0)
            df, dg = _rms_bwd(coef * dh, f_ref[...], gpost_ref[...])
            df_ref[...] = _bf(df)

            @pl.when(i == 0)
            def _():
                dgpost_ref[...] = jnp.zeros_like(dgpost_ref)
            dgpost_ref[...] += dg

    row = pl.BlockSpec((tm, D_MODEL), lambda i: (i, 0))
    gain = pl.BlockSpec((1, D_MODEL), lambda i: (0, 0))
    args, in_specs, out_specs, out_shape = [dres], [row], [], []
    if pre:
        args += list(pre)
        in_specs += [row, row, gain]
        out_specs += [row, gain]
        out_shape += [_sds((t, D_MODEL)), _sds((1, D_MODEL))]
    if post:
        args += [post[0], post[1]]
        in_specs += [row, gain]
        out_specs += [row, gain]
        out_shape += [_sds((t, D_MODEL), BF16), _sds((1, D_MODEL))]
    return pl.pallas_call(
        body, name=name, grid=(ni,), in_specs=in_specs, out_specs=out_specs,
        out_shape=out_shape, compiler_params=_params("arbitrary"),
    )(*args)


def _qkv_fwd(h, g, win, *, name):
    t = h.shape[0]
    tm, tn = min(1024, t), 1024
    ni, nj = t // tm, QKV_WIDTH // tn

    def body(h_ref, g_ref, w_ref, qkv_ref, u_ref):
        @pl.when(pl.program_id(1) == 0)
        def _():
            u_ref[...] = _bf(_rms(h_ref[...], g_ref[...]))
        qkv_ref[...] = jnp.dot(u_ref[...], w_ref[...], preferred_element_type=F32)

    row = pl.BlockSpec((tm, D_MODEL), lambda i, j: (i, 0))
    return pl.pallas_call(
        body, name=name, grid=(ni, nj),
        in_specs=[row, pl.BlockSpec((1, D_MODEL), lambda i, j: (0, 0)),
                  pl.BlockSpec((D_MODEL, tn), lambda i, j: (0, j))],
        out_specs=[pl.BlockSpec((tm, tn), lambda i, j: (i, j)), row],
        out_shape=[_sds((t, QKV_WIDTH)), _sds((t, D_MODEL), BF16)],
        compiler_params=_params("arbitrary", "arbitrary"),
    )(h, g, win)


def _qkv_bwd(dq, dk, dv, u, win, *, name):
    t = u.shape[0]
    tn, ts = 512, 512
    nj, ns = QKV_WIDTH // tn, t // ts

    def body(dq_ref, dk_ref, dv_ref, u_ref, w_ref, dw_ref, du_hbm, du_v, acc_ref, sem):
        j = pl.program_id(0)

        @pl.when(j == 0)
        def _():
            du_v[...] = jnp.zeros_like(du_v)

        wj = w_ref[...]
        for role, d_ref in enumerate((dq_ref, dk_ref, dv_ref)):
            @pl.when(j % 3 == role)
            def _():
                acc_ref[...] = jnp.zeros_like(acc_ref)
                for s in range(ns):
                    rows = pl.ds(s * ts, ts)
                    dcol = d_ref[rows, :]
                    acc_ref[...] += _dot_tn(u_ref[rows, :], dcol)
                    du_v[rows, :] += _dot_nt(dcol, wj)
                dw_ref[...] = _bf(acc_ref[...])

        @pl.when(j == nj - 1)
        def _():
            c = pltpu.make_async_copy(du_v, du_hbm, sem)
            c.start()
            c.wait()

    colw = pl.BlockSpec((D_MODEL, tn), lambda j: (0, j))
    grp = pl.BlockSpec((t, tn), lambda j: (0, j // 3))
    return pl.pallas_call(
        body, name=name, grid=(nj,),
        in_specs=[grp, grp, grp, pl.BlockSpec((t, D_MODEL), lambda j: (0, 0)), colw],
        out_specs=[colw, ANY],
        out_shape=[_sds((D_MODEL, QKV_WIDTH), BF16), _sds((t, D_MODEL))],
        scratch_shapes=[pltpu.VMEM((t, D_MODEL), F32), pltpu.VMEM((D_MODEL, tn), F32),
                        pltpu.SemaphoreType.DMA],
        compiler_params=_params("arbitrary"),
    )(dq, dk, dv, u, win)


def _sb_stack(x):
    lo, hi = _head_masks()
    return jnp.concatenate([jnp.where(lo, x, 0.0), jnp.where(hi, x, 0.0)], axis=0)


def _sb_unstack(x2, blk):
    return jnp.where(_head_masks()[0], x2[:blk], x2[blk:])


def _sb_rows_from(x2, blk, r0):
    return x2 if r0 == 0 else jnp.concatenate([x2[r0:blk], x2[blk + r0:]], axis=0)


def _sb_rows_merge(full2, sub2, blk, r0):
    if r0 == 0:
        return sub2
    rows = blk - r0
    return jnp.concatenate([full2[:r0], sub2[:rows], full2[blk:blk + r0], sub2[rows:]], axis=0)


def _sb_mask(qb, kb, offset):
    r = lax.broadcasted_iota(jnp.int32, (2 * qb, kb), 0) & (qb - 1)
    c = lax.broadcasted_iota(jnp.int32, (2 * qb, kb), 1) + offset
    return c < r


def _tri(n, keep):
    r = lax.broadcasted_iota(jnp.int32, (n, n), 0)
    c = lax.broadcasted_iota(jnp.int32, (n, n), 1)
    return jnp.where(keep(r, c), 1.0, 0.0).astype(BF16)


def _cumsum01(x, u):
    m = x.shape[0]
    hi = _bf(x)
    lo = _bf(x - hi.astype(F32))
    both = jnp.dot(jnp.concatenate([hi, lo], axis=0), u, preferred_element_type=F32)
    return both[:m] + both[m:]


def _sb_fwd(qkv, *, name):
    t = qkv.shape[0]
    blk, kb = min(SB_QB, t), SB_KB
    ni, per = t // blk, blk // kb

    def body(q_ref, k_ref, v_ref, o_ref, ltot_ref):
        i = pl.program_id(1)
        u_after = _tri(kb, lambda r, c: r > c)
        q2 = [_bf(_sb_stack(q_ref[:, lanes] * ATT_SCALE)) for lanes in SB_LANES]

        def tile(g, k0, mask, acc, c_l):
            kj = k_ref[pl.ds(k0, kb), SB_LANES[g]]
            vj = v_ref[pl.ds(k0, kb), SB_LANES[g]]
            z = _dot_nt(q2[g], kj)
            sp = _softplus(z)
            lf = -sp if mask is None else jnp.where(mask, -sp, 0.0)
            a = jnp.exp(z - sp + _cumsum01(lf, u_after) + c_l)
            if mask is not None:
                a = jnp.where(mask, a, 0.0)
            return acc + _dot(a, vj), c_l + jnp.sum(lf, axis=1, keepdims=True)

        def tiles(k0, mask, carry):
            return tuple(tile(g, k0, mask, *carry[g]) for g in range(SB_GROUP))

        carry = ((jnp.zeros((2 * blk, PAIR), F32), jnp.zeros((2 * blk, 1), F32)),) * SB_GROUP
        for d in reversed(range(per)):
            carry = tiles(pl.multiple_of(i * blk + d * kb, kb), _sb_mask(blk, kb, d * kb), carry)
        carry = lax.fori_loop(
            1, per * i + 1,
            lambda jj, c: tiles(pl.multiple_of((per * i - jj) * kb, kb), None, c), carry)
        for g, (acc, c_l) in enumerate(carry):
            o_ref[:, SB_LANES[g]] = _sb_unstack(acc, blk)
            ltot_ref[:, SB_LANES[g]] = _sb_unstack(jnp.broadcast_to(c_l, (2 * blk, PAIR)), blk)

    width = SB_GROUP * PAIR
    blkspec = pl.BlockSpec((blk, width), lambda p, i: (i, p))
    n_steps = N_PAIRS // SB_GROUP
    return pl.pallas_call(
        body, name=name, grid=(n_steps, ni),
        in_specs=[blkspec,
                  pl.BlockSpec((t, width), lambda p, i: (0, n_steps + p)),
                  pl.BlockSpec((t, width), lambda p, i: (0, 2 * n_steps + p))],
        out_specs=[blkspec, blkspec],
        out_shape=[_sds((t, D_MODEL)), _sds((t, D_MODEL // 2))],
        compiler_params=_params("arbitrary", "arbitrary"),
    )(qkv, qkv, qkv)


def _sb_bwd(qkv, ltot, do, *, name):
    t = qkv.shape[0]
    blk, kb = min(SB_QB, t), SB_KB
    ni, per = t // blk, blk // kb

    def body(q_ref, k_ref, v_ref, lt_ref, do_ref, dq_ref, dkout_ref, dvout_ref, dk_ref, dv_ref):
        i = pl.program_id(1)

        @pl.when(i == 0)
        def _():
            dk_ref[...] = jnp.zeros_like(dk_ref)
            dv_ref[...] = jnp.zeros_like(dv_ref)

        u_upto = _tri(kb, lambda r, c: r <= c)
        u_before = _tri(kb, lambda r, c: r < c)
        lane = lax.broadcasted_iota(jnp.int32, (1, PAIR), 1)
        q2 = [_bf(_sb_stack(q_ref[:, lanes] * ATT_SCALE)) for lanes in SB_LANES]
        do2 = [_bf(_sb_stack(do_ref[:, lanes])) for lanes in SB_LANES]
        total = [jnp.concatenate(
            [jnp.sum(jnp.where(lane == h * HEAD_DIM, lt_ref[:, lanes], 0.0), axis=1, keepdims=True)
             for h in range(2)], axis=0) for lanes in SB_LANES]

        def tile(g, ops, k0, mask, dq_acc, c_l, c_g):
            qg, dog, tot = ops
            krows = pl.ds(k0, kb)
            kj = k_ref[krows, SB_LANES[g]]
            vj = v_ref[krows, SB_LANES[g]]
            z = _dot_nt(qg, kj)
            sp = _softplus(z)
            sig = jnp.exp(z - sp)
            lf = -sp if mask is None else jnp.where(mask, -sp, 0.0)
            a = jnp.exp(z - sp + tot - (_cumsum01(lf, u_upto) + c_l))
            if mask is not None:
                a = jnp.where(mask, a, 0.0)
            gw = a * _dot_nt(dog, vj)
            g_before = jnp.dot(_bf(gw), u_before, preferred_element_type=F32) + c_g
            dz = gw * (1.0 - sig) - g_before * sig
            if mask is not None:
                dz = jnp.where(mask, dz, 0.0)
            dk_ref[krows, SB_LANES[g]] += _dot_tn(dz, qg)
            dv_ref[krows, SB_LANES[g]] += _dot_tn(a, dog)
            return (dq_acc + _dot(dz, kj), c_l + jnp.sum(lf, axis=1, keepdims=True),
                    c_g + jnp.sum(gw, axis=1, keepdims=True))

        def tiles(ops, k0, mask, carry):
            return tuple(tile(g, ops[g], k0, mask, *carry[g]) for g in range(SB_GROUP))

        ops = tuple(zip(q2, do2, total))
        zero = (jnp.zeros((2 * blk, PAIR), F32), jnp.zeros((2 * blk, 1), F32),
                jnp.zeros((2 * blk, 1), F32))
        carry = lax.fori_loop(
            0, per * i, lambda j, c: tiles(ops, pl.multiple_of(j * kb, kb), None, c),
            (zero,) * SB_GROUP)
        for d in range(per):
            r0 = d * kb
            sub = tiles(tuple(tuple(_sb_rows_from(a, blk, r0) for a in o) for o in ops),
                        pl.multiple_of(i * blk + r0, kb), _sb_mask(blk - r0, kb, 0),
                        tuple(tuple(_sb_rows_from(a, blk, r0) for a in c) for c in carry))
            carry = tuple(tuple(_sb_rows_merge(a, s, blk, r0) for a, s in zip(c, cs))
                          for c, cs in zip(carry, sub))
        for g, (dq_acc, _, _) in enumerate(carry):
            dq_ref[:, SB_LANES[g]] = _bf(_sb_unstack(dq_acc, blk) * ATT_SCALE)

        @pl.when(i == ni - 1)
        def _():
            dkout_ref[...] = _bf(dk_ref[...])
            dvout_ref[...] = _bf(dv_ref[...])

    width = SB_GROUP * PAIR
    n_steps = N_PAIRS // SB_GROUP
    blkspec = lambda off: pl.BlockSpec((blk, width), lambda p, i: (i, off + p))
    full = lambda off: pl.BlockSpec((t, width), lambda p, i: (0, off + p))
    return pl.pallas_call(
        body, name=name, grid=(n_steps, ni),
        in_specs=[blkspec(0), full(n_steps), full(2 * n_steps), blkspec(0), blkspec(0)],
        out_specs=[blkspec(0), full(0), full(0)],
        out_shape=[_sds((t, D_MODEL), BF16)] * 3,
        scratch_shapes=[pltpu.VMEM((t, width), F32), pltpu.VMEM((t, width), F32)],
        compiler_params=_params("arbitrary", "arbitrary"),
    )(qkv, qkv, qkv, ltot, do)


def _ch_mask(i):
    r = lax.broadcasted_iota(jnp.int32, (CH_QB, CH_WIN), 0)
    c = lax.broadcasted_iota(jnp.int32, (CH_QB, CH_WIN), 1)
    qc = LOOKBACK + lax.shift_right_arithmetic(r, 6)
    kc = lax.shift_right_arithmetic(c, 6)
    first = i * (CH_QB // CHUNK) - LOOKBACK
    return (kc <= qc) & (kc >= qc - LOOKBACK) & (kc + first >= 0)


def _ch_probs(qm, kw, bias_h, mask):
    z = _dot_nt(qm, kw) * ATT_SCALE + bias_h
    z = jnp.where(mask, z, NEG_INF)
    e = jnp.exp(z - jnp.max(z, axis=1, keepdims=True))
    return e / jnp.sum(e, axis=1, keepdims=True)


def _ch_fill(pad_ref, src_ref, t):
    pad_ref[pl.ds(0, CH_LOOK), :] = jnp.zeros((CH_LOOK, PAIR), BF16)
    pad_ref[pl.ds(CH_LOOK, t), :] = _bf(src_ref[...])


def _ch_fwd(qkv, bias, o_in, *, name):
    t = qkv.shape[0]
    ni = t // CH_QB

    def body(q_ref, k_ref, v_ref, bias_ref, _alias, o_ref, kpad, vpad):
        i = pl.program_id(1)

        @pl.when(i == 0)
        def _():
            _ch_fill(kpad, k_ref, t)
            _ch_fill(vpad, v_ref, t)

        win = pl.ds(pl.multiple_of(i * CH_QB, CH_QB), CH_WIN)
        kw, vw = kpad[win, :], vpad[win, :]
        mask = _ch_mask(i)
        q = q_ref[...]
        outs = []
        for h, hm in enumerate(_head_masks()):
            p = _ch_probs(jnp.where(hm, q, 0.0), kw, bias_ref[h], mask)
            outs.append(_dot(p, vw))
        o_ref[...] = jnp.where(_head_masks()[0], outs[0], outs[1])

    full = lambda off: pl.BlockSpec((t, PAIR), lambda p, i: (0, off + p))
    return pl.pallas_call(
        body, name=name, grid=(N_PAIRS, ni),
        in_specs=[pl.BlockSpec((CH_QB, PAIR), lambda p, i: (i, 3 * N_PAIRS + p)),
                  full(4 * N_PAIRS), full(5 * N_PAIRS),
                  pl.BlockSpec((2, CH_QB, CH_WIN), lambda p, i: (p, 0, 0)), ANY],
        out_specs=pl.BlockSpec((CH_QB, PAIR), lambda p, i: (i, N_PAIRS + p)),
        out_shape=_sds((t, D_MODEL)),
        scratch_shapes=[pltpu.VMEM((t + CH_LOOK, PAIR), BF16)] * 2,
        input_output_aliases={4: 0},
        compiler_params=_params("arbitrary", "arbitrary"),
    )(qkv, qkv, qkv, bias, o_in)


def _ch_bwd(qkv, bias, o, do, dq_in, dk_in, dv_in, *, name):
    t = qkv.shape[0]
    ni = t // CH_QB

    def body(q_ref, k_ref, v_ref, bias_ref, o_ref, do_ref, _a0, _a1, _a2,
             dq_ref, dkout_ref, dvout_ref, dbias_ref, kpad, vpad, dkpad, dvpad):
        i = pl.program_id(1)

        @pl.when(i == 0)
        def _():
            _ch_fill(kpad, k_ref, t)
            _ch_fill(vpad, v_ref, t)
            dkpad[...] = jnp.zeros_like(dkpad)
            dvpad[...] = jnp.zeros_like(dvpad)
            dbias_ref[...] = jnp.zeros_like(dbias_ref)

        win = pl.ds(pl.multiple_of(i * CH_QB, CH_QB), CH_WIN)
        kw, vw = kpad[win, :], vpad[win, :]
        mask = _ch_mask(i)
        q, o_blk, do_blk = q_ref[...], o_ref[...], do_ref[...]
        dqs = []
        for h, hm in enumerate(_head_masks()):
            qm = _bf(jnp.where(hm, q, 0.0))
            dom = jnp.where(hm, do_blk, 0.0)
            delta = jnp.sum(dom * o_blk, axis=1, keepdims=True)
            dom = _bf(dom)
            p = _ch_probs(qm, kw, bias_ref[h], mask)
            ds = p * (_dot_nt(dom, vw) - delta)
            dbias_ref[h] += ds
            dsz = ds * ATT_SCALE
            dqs.append(_dot(dsz, kw))
            dkpad[win, :] += _dot_tn(dsz, qm)
            dvpad[win, :] += _dot_tn(p, dom)
        dq_ref[...] = _bf(jnp.where(_head_masks()[0], dqs[0], dqs[1]))

        @pl.when(i == ni - 1)
        def _():
            dkout_ref[...] = _bf(dkpad[pl.ds(CH_LOOK, t), :])
            dvout_ref[...] = _bf(dvpad[pl.ds(CH_LOOK, t), :])

    blkspec = lambda off: pl.BlockSpec((CH_QB, PAIR), lambda p, i: (i, off + p))
    full = lambda off: pl.BlockSpec((t, PAIR), lambda p, i: (0, off + p))
    bias_spec = pl.BlockSpec((2, CH_QB, CH_WIN), lambda p, i: (p, 0, 0))
    return pl.pallas_call(
        body, name=name, grid=(N_PAIRS, ni),
        in_specs=[blkspec(3 * N_PAIRS), full(4 * N_PAIRS), full(5 * N_PAIRS), bias_spec,
                  blkspec(N_PAIRS), blkspec(N_PAIRS), ANY, ANY, ANY],
        out_specs=[blkspec(N_PAIRS), full(N_PAIRS), full(N_PAIRS), bias_spec],
        out_shape=[_sds((t, D_MODEL), BF16)] * 3 + [_sds((2 * N_PAIRS, CH_QB, CH_WIN))],
        scratch_shapes=[pltpu.VMEM((t + CH_LOOK, PAIR), BF16)] * 2
        + [pltpu.VMEM((t + CH_LOOK, PAIR), F32)] * 2,
        input_output_aliases={6: 0, 7: 1, 8: 2},
        compiler_params=_params("arbitrary", "arbitrary"),
    )(qkv, qkv, qkv, bias, o, do, dq_in, dk_in, dv_in)


def _bias_expand(fvec, *, name):
    n_heads = fvec.shape[0]

    def body(f_ref, o_ref, rows8):
        row = f_ref[0]
        for r in range(8):
            rows8[pl.ds(r, 1), :] = pltpu.roll(row, r, 1)
        base = rows8[...]
        for blk in range(CH_QB // 8):
            o_ref[0, pl.ds(8 * blk, 8), :] = pltpu.roll(base, 8 * blk, 1)

    return pl.pallas_call(
        body, name=name, grid=(n_heads,),
        in_specs=[pl.BlockSpec((1, 1, CH_WIN), lambda h: (h, 0, 0))],
        out_specs=pl.BlockSpec((1, CH_QB, CH_WIN), lambda h: (h, 0, 0)),
        out_shape=_sds((n_heads, CH_QB, CH_WIN)),
        scratch_shapes=[pltpu.VMEM((8, CH_WIN), F32)],
        compiler_params=_params("arbitrary"),
    )(fvec)


def _bias_grad(dbias, after, *, name):
    n_heads = dbias.shape[0]
    first = CH_LOOK - REL_CLIP

    def body(d_ref, _after, o_ref, acc8):
        acc = jnp.zeros((8, CH_WIN), F32)
        for blk in range(CH_QB // 8):
            acc = acc + pltpu.roll(d_ref[0, pl.ds(8 * blk, 8), :], (CH_WIN - 8 * blk) % CH_WIN, 1)
        acc8[...] = acc
        dvec = jnp.zeros((1, CH_WIN), F32)
        for r in range(8):
            dvec = dvec + pltpu.roll(acc8[pl.ds(r, 1), :], (CH_WIN - r) % CH_WIN, 1)
        lane = lax.broadcasted_iota(jnp.int32, (1, CH_WIN), 1)
        clipped = (lane <= first) | (lane >= first + REL_CLIP + CHUNK)
        total = jnp.sum(jnp.where(clipped, dvec, 0.0), axis=1, keepdims=True)
        o_ref[0] = jnp.where(lane == first, total, dvec)

    return pl.pallas_call(
        body, name=name, grid=(n_heads,),
        in_specs=[pl.BlockSpec((1, CH_QB, CH_WIN), lambda h: (h, 0, 0)), ANY],
        out_specs=pl.BlockSpec((1, 1, CH_WIN), lambda h: (h, 0, 0)),
        out_shape=_sds((n_heads, 1, CH_WIN)),
        scratch_shapes=[pltpu.VMEM((8, CH_WIN), F32)],
        compiler_params=_params("arbitrary"),
    )(dbias, after)


def _out_fwd(o, h1, g_sb, g_ch, g_post, wout, *, name):
    t = o.shape[0]
    tm = 512
    half = D_MODEL // 2

    def body(o_ref, h_ref, gsb_ref, gch_ref, gpost_ref, w_ref, h2_ref, mixed_ref, y_ref):
        ov = o_ref[...]
        mixed = jnp.concatenate([_rms(ov[:, :half], gsb_ref[...]),
                                 _rms(ov[:, half:], gch_ref[...])], axis=1)
        mixed_ref[...] = _bf(mixed)
        y = _dot(mixed, w_ref[...])
        y_ref[...] = y
        h2_ref[...] = h_ref[...] + _rms(y, gpost_ref[...])

    row = pl.BlockSpec((tm, D_MODEL), lambda i: (i, 0))
    gain = lambda n: pl.BlockSpec((1, n), lambda i: (0, 0))
    return pl.pallas_call(
        body, name=name, grid=(t // tm,),
        in_specs=[row, row, gain(half), gain(half), gain(D_MODEL),
                  pl.BlockSpec((D_MODEL, D_MODEL), lambda i: (0, 0))],
        out_specs=[row, row, row],
        out_shape=[_sds((t, D_MODEL)), _sds((t, D_MODEL), BF16), _sds((t, D_MODEL))],
        compiler_params=_params("arbitrary"),
    )(o, h1, g_sb, g_ch, g_post, wout)


def _out_bwd(dy, mixed, o, g_sb, g_ch, wout, *, name):
    t = o.shape[0]
    tm = 512
    ni = t // tm
    half = D_MODEL // 2

    def body(dy_ref, mixed_ref, o_ref, gsb_ref, gch_ref, w_ref,
             dw_ref, do_ref, dgsb_ref, dgch_ref, acc_ref):
        i = pl.program_id(0)

        @pl.when(i == 0)
        def _():
            acc_ref[...] = jnp.zeros_like(acc_ref)
            dgsb_ref[...] = jnp.zeros_like(dgsb_ref)
            dgch_ref[...] = jnp.zeros_like(dgch_ref)

        dyv = dy_ref[...]
        acc_ref[...] += _dot_tn(mixed_ref[...], dyv)
        dm = _dot_nt(dyv, w_ref[...])
        ov = o_ref[...]
        doa, dga = _rms_bwd(dm[:, :half], ov[:, :half], gsb_ref[...])
        dob, dgb = _rms_bwd(dm[:, half:], ov[:, half:], gch_ref[...])
        do_ref[...] = jnp.concatenate([doa, dob], axis=1)
        dgsb_ref[...] += dga
        dgch_ref[...] += dgb

        @pl.when(i == ni - 1)
        def _():
            dw_ref[...] = _bf(acc_ref[...])

    row = pl.BlockSpec((tm, D_MODEL), lambda i: (i, 0))
    gain = pl.BlockSpec((1, half), lambda i: (0, 0))
    sq = pl.BlockSpec((D_MODEL, D_MODEL), lambda i: (0, 0))
    return pl.pallas_call(
        body, name=name, grid=(ni,),
        in_specs=[row, row, row, gain, gain, sq],
        out_specs=[sq, row, gain, gain],
        out_shape=[_sds((D_MODEL, D_MODEL), BF16), _sds((t, D_MODEL)),
                   _sds((1, half)), _sds((1, half))],
        scratch_shapes=[pltpu.VMEM((D_MODEL, D_MODEL), F32)],
        compiler_params=_params("arbitrary"),
    )(dy, mixed, o, g_sb, g_ch, wout)


def _ple(p, h3, target, wp, wgate, g, f_post, g_post, *, name):
    t = h3.shape[0]
    tm = 512
    ni = t // tm

    def body(p_ref, h_ref, tgt_ref, wp_ref, wg_ref, g_ref, f_ref, gf_ref,
             loss_ref, dres_ref, dwp_ref, dwg_ref, dg_ref, df_ref, dgf_ref, accp, accg):
        i = pl.program_id(0)

        @pl.when(i == 0)
        def _():
            loss_ref[...] = jnp.zeros_like(loss_ref)
            dg_ref[...] = jnp.zeros_like(dg_ref)
            dgf_ref[...] = jnp.zeros_like(dgf_ref)
            accp[...] = jnp.zeros_like(accp)
            accg[...] = jnp.zeros_like(accg)

        pv, hv, gv = p_ref[...], h_ref[...], g_ref[...]
        pe = _dot(pv, wp_ref[...])
        sig = _sigmoid(_dot(hv, wg_ref[...]))
        e = pe * sig
        err = hv + _rms(e, gv) - tgt_ref[...]
        tok = jnp.mean(err * err, axis=-1, keepdims=True)
        loss_ref[...] += 0.5 * jnp.sum(tok, axis=0, keepdims=True)
        dh4 = err * (1.0 / D_MODEL)
        de, dg = _rms_bwd(dh4, e, gv)
        dg_ref[...] += dg
        dpe = de * sig
        dgt = de * pe * sig * (1.0 - sig)
        accp[...] += _dot_tn(pv, dpe)
        accg[...] += _dot_tn(hv, dgt)
        dres = dh4 + _dot_nt(dgt, wg_ref[...])
        dres_ref[...] = dres
        df, dgf = _rms_bwd(0.5 * dres, f_ref[...], gf_ref[...])
        df_ref[...] = _bf(df)
        dgf_ref[...] += dgf

        @pl.when(i == ni - 1)
        def _():
            dwp_ref[...] = _bf(accp[...])
            dwg_ref[...] = _bf(accg[...])

    row = pl.BlockSpec((tm, D_MODEL), lambda i: (i, 0))
    const = lambda r, c: pl.BlockSpec((r, c), lambda i: (0, 0))
    return pl.pallas_call(
        body, name=name, grid=(ni,),
        in_specs=[pl.BlockSpec((tm, PLE_DIM), lambda i: (i, 0)), row, row,
                  const(PLE_DIM, D_MODEL), const(D_MODEL, D_MODEL), const(1, D_MODEL),
                  row, const(1, D_MODEL)],
        out_specs=[const(1, 128), row, const(PLE_DIM, D_MODEL), const(D_MODEL, D_MODEL),
                   const(1, D_MODEL), row, const(1, D_MODEL)],
        out_shape=[_sds((1, 128)), _sds((t, D_MODEL)), _sds((PLE_DIM, D_MODEL), BF16),
                   _sds((D_MODEL, D_MODEL), BF16), _sds((1, D_MODEL)),
                   _sds((t, D_MODEL), BF16), _sds((1, D_MODEL))],
        scratch_shapes=[pltpu.VMEM((PLE_DIM, D_MODEL), F32), pltpu.VMEM((D_MODEL, D_MODEL), F32)],
        compiler_params=_params("arbitrary"),
    )(p, h3, target, wp, wgate, g, f_post, g_post)


def _rel_bias_to_fvec(rel_bias):
    rev = rel_bias[:, ::-1]
    n_heads = rel_bias.shape[0]
    first = CH_LOOK - REL_CLIP
    n_var = REL_CLIP + CHUNK
    clipped = rev[:, :1]
    fvec = jnp.concatenate([jnp.broadcast_to(clipped, (n_heads, first)), rev[:, :n_var],
                            jnp.broadcast_to(clipped, (n_heads, CH_WIN - first - n_var))], axis=1)
    return fvec.reshape(n_heads, 1, CH_WIN)


def _fvec_grad_to_rel_bias(dfvec):
    first = CH_LOOK - REL_CLIP
    n_var = REL_CLIP + CHUNK
    rev = jnp.pad(dfvec[:, 0, first:first + n_var], ((0, 0), (0, N_REL - n_var)))
    return rev[:, ::-1]


def _local_step(x, p, target, g, weights_for, grads_done, fvec, weights_early=None):
    bias = _bias_expand(fvec, name="bias_expand")
    w, tie = weights_for(0, bias)
    w = dict(w)
    h1, n1, a1, b1, f1 = _ffn_fwd(x, g["ffn1_pre"] + tie, g["ffn1_post"],
                                  w["ffn1_gate"], w["ffn1_up"], w["ffn1_down"], name="ffn1_fwd")
    more, tie = weights_for(1, h1)
    w.update(more)
    qkv, u = _qkv_fwd(h1, g["mix_pre"] + tie, w["in"], name="qkv_fwd")
    o, ltot = _sb_fwd(qkv, name="sb_fwd")
    tie = weights_early(2, ltot) if weights_early else 0.0
    o = _ch_fwd(qkv, bias, o, name="ch_fwd")
    w.update(weights_for(2, o)[0])
    h2, mixed, y = _out_fwd(o, h1, g["out_sb"] + tie, g["out_ch"], g["mix_post"], w["out"],
                            name="out_fwd")
    h3, n2, a2, b2, f2 = _ffn_fwd(h2, g["ffn2_pre"], g["ffn2_post"],
                                  w["ffn2_gate"], w["ffn2_up"], w["ffn2_down"], name="ffn2_fwd")
    loss, dh3, dwp, dwgate, dg_ple, df2, dg_ffn2_post = _ple(
        p, h3, target, w["ple_proj"], w["ple_gate"], g["ple_post"], f2, g["ffn2_post"], name="ple")
    tie = grads_done(0, {"ple_proj": dwp, "ple_gate": dwgate})
    dwg2, dwu2, dwd2, dn2 = _ffn_bwd(n2, df2, a2, b2, w["ffn2_gate"], w["ffn2_up"],
                                     w["ffn2_down"], name="ffn2_bwd")
    tie = tie + grads_done(1, {"ffn2_gate": dwg2, "ffn2_up": dwu2, "ffn2_down": dwd2})
    dh2, dg_ffn2_pre, dy, dg_mix_post = _junction(
        dh3, pre=(dn2, h2, g["ffn2_pre"] + tie), post=(y, g["mix_post"], 1.0), name="junction2")
    dwout, do, dg_sb, dg_ch = _out_bwd(dy, mixed, o, g["out_sb"], g["out_ch"], w["out"],
                                       name="out_bwd")
    dq, dk, dv = _sb_bwd(qkv, ltot, do, name="sb_bwd")
    dq, dk, dv, dbias = _ch_bwd(qkv, bias, o, do, dq, dk, dv, name="ch_bwd")
    dwin, du = _qkv_bwd(dq, dk, dv, u, w["in"], name="qkv_bwd")
    tie = grads_done(2, {"out": dwout, "in": dwin})
    dh1, dg_mix_pre, df1, dg_ffn1_post = _junction(
        dh2, pre=(du, h1, g["mix_pre"] + tie), post=(f1, g["ffn1_post"], 0.5), name="junction1")
    dwg1, dwu1, dwd1, dn1 = _ffn_bwd(n1, df1, a1, b1, w["ffn1_gate"], w["ffn1_up"],
                                     w["ffn1_down"], name="ffn1_bwd")
    tie = grads_done(3, {"ffn1_gate": dwg1, "ffn1_up": dwu1, "ffn1_down": dwd1})
    dx, dg_ffn1_pre = _junction(dh1, pre=(dn1, x, g["ffn1_pre"] + tie), name="junction0")

    dg = {"ffn1_pre": dg_ffn1_pre, "ffn1_post": dg_ffn1_post, "mix_pre": dg_mix_pre,
          "mix_post": dg_mix_post, "out_sb": dg_sb, "out_ch": dg_ch,
          "ffn2_pre": dg_ffn2_pre, "ffn2_post": dg_ffn2_post, "ple_post": dg_ple}
    return loss, dx, dg, dbias


_WEIGHTS = (
    ("ffn1_gate", "row", FF_SHARD, FF_SHARD_PAD, D_MODEL),
    ("ffn1_up", "row", FF_SHARD, FF_SHARD_PAD, D_MODEL),
    ("ffn1_down", "row", FF_SHARD, FF_SHARD_PAD, D_MODEL),
    ("in", "col", QKV_SHARD, QKV_SHARD, D_MODEL),
    ("out", "row", ROW_SHARD, ROW_SHARD, D_MODEL),
    ("ffn2_gate", "row", FF_SHARD, FF_SHARD_PAD, D_MODEL),
    ("ffn2_up", "row", FF_SHARD, FF_SHARD_PAD, D_MODEL),
    ("ffn2_down", "row", FF_SHARD, FF_SHARD_PAD, D_MODEL),
    ("ple_proj", "col", ROW_SHARD, ROW_SHARD, PLE_DIM),
    ("ple_gate", "row", ROW_SHARD, ROW_SHARD, D_MODEL),
)
_TRANSPOSED = ("ffn1_gate", "ffn1_up", "ffn2_gate", "ffn2_up")
_SPEC = {n: (kind, valid, pad, other) for n, kind, valid, pad, other in _WEIGHTS}
_GATHER_STAGES = (("ffn1_gate", "ffn1_up", "ffn1_down"), ("in",),
                  ("out", "ffn2_gate", "ffn2_up", "ffn2_down", "ple_proj", "ple_gate"))
_SCATTER_STAGES = (("ple_proj", "ple_gate"), ("ffn2_gate", "ffn2_up", "ffn2_down"),
                   ("out", "in"), ("ffn1_gate", "ffn1_up", "ffn1_down"))
HBM = pl.BlockSpec(memory_space=pltpu.HBM)
SEM = pl.BlockSpec(memory_space=pltpu.SEMAPHORE)
EFFECT = pltpu.SideEffectType.DATAFLOW_SIDE_EFFECTING


def _shard_shape(kind, size, other):
    return (other, size) if kind == "col" else (size, other)


def _window(ref, kind, start, size):
    return ref.at[:, pl.ds(start, size)] if kind == "col" else ref.at[pl.ds(start, size), :]


def _device_tuple(k):
    return (k // 4, (k // 2) % 2, k % 2)


def _my_index():
    return 4 * lax.axis_index("x") + 2 * lax.axis_index("y") + lax.axis_index("c")


def _pack_weights(names, shards, after, *, name):
    nw = len(names)
    specs = [_SPEC[n] for n in names]

    def body(*refs):
        ins, packed, full = refs[:nw], refs[nw + 1:2 * nw + 1], refs[2 * nw + 1:3 * nw + 1]
        sem = refs[3 * nw + 1]
        me = _my_index()
        for (kind, valid, pad, _), src, dst in zip(specs, ins, packed):
            if pad != valid:
                dst[...] = jnp.zeros_like(dst)
            if kind == "col":
                dst[:, pl.ds(0, valid)] = _bf(src[...])
            else:
                dst[pl.ds(0, valid), :] = _bf(src[...])
        for k in range(N_DEV):
            @pl.when(me == k)
            def _():
                for w, (kind, _, pad, _) in enumerate(specs):
                    pltpu.make_async_copy(packed[w], _window(full[w], kind, k * pad, pad),
                                          sem.at[w]).start()
        for w, (kind, _, pad, _) in enumerate(specs):
            pltpu.make_async_copy(packed[w], _window(full[w], kind, 0, pad), sem.at[w]).wait()

    whole = lambda shape: pl.BlockSpec(shape, lambda i: (0, 0))
    packed_shapes = [_shard_shape(kind, pad, other) for kind, _, pad, other in specs]
    outs = pl.pallas_call(
        body, name=name, grid=(1,),
        in_specs=[whole(a.shape) for a in shards] + [ANY],
        out_specs=[whole(s) for s in packed_shapes] + [ANY] * nw,
        out_shape=[_sds(s, BF16) for s in packed_shapes]
        + [_sds(_shard_shape(kind, N_DEV * pad, other), BF16) for kind, _, pad, other in specs],
        scratch_shapes=[pltpu.SemaphoreType.DMA((nw,))],
        compiler_params=_params("arbitrary"),
    )(*shards, after)
    return dict(zip(names, outs[:nw])), dict(zip(names, outs[nw:]))


def _hbm(a):
    return pltpu.with_memory_space_constraint(a, pltpu.HBM)


def _split_start(name, n, body_copies, sources, lands, after):
    arrays = list(sources) + list(lands)
    ns, na = len(sources), len(arrays)

    def body(*refs):
        src, land = refs[:ns], refs[ns:na]
        send, recv = refs[na + 1], refs[na + 2]
        token = refs[-1]
        body_copies(src, land, send, recv)
        token[...] = jnp.zeros_like(token)

    out = pl.pallas_call(
        body, name=name,
        out_shape=(pltpu.SemaphoreType.DMA((n,)), pltpu.SemaphoreType.DMA((n,)),
                   *[pltpu.HBM(a.shape, a.dtype) for a in arrays], _sds((8, 128))),
        in_specs=[HBM] * na + [ANY], out_specs=(SEM, SEM, *[HBM] * na, VMEM),
        input_output_aliases={i: 2 + i for i in range(na)},
        compiler_params=pltpu.CompilerParams(has_side_effects=EFFECT),
    )(*[_hbm(a) for a in arrays], after)
    return out[0], out[1], out[2:2 + ns], out[2 + ns:2 + na], out[-1]


def _split_wait(name, n, seven_of, send, recv, sources, lands, after, keep_sources=False):
    arrays = list(sources) + list(lands)
    ns, na = len(sources), len(arrays)

    def body(*refs):
        land = refs[ns:na]
        send_ref, recv_ref = refs[na], refs[na + 1]
        myself = (lax.axis_index("x"), lax.axis_index("y"), lax.axis_index("c"))
        for w in range(n):
            seven = seven_of(w, land[w])
            copy = pltpu.make_async_remote_copy(
                src_ref=seven, dst_ref=seven, send_sem=send_ref.at[w], recv_sem=recv_ref.at[w],
                device_id=myself, device_id_type=MESH)
            copy.wait_send()
            copy.wait_recv()

    afters = tuple(after) if isinstance(after, (tuple, list)) else (after,)
    out = pl.pallas_call(
        body, name=name,
        out_shape=[pltpu.HBM(a.shape, a.dtype) for a in arrays],
        in_specs=[HBM] * na + [SEM, SEM] + [ANY] * len(afters), out_specs=[HBM] * na,
        input_output_aliases={i: i for i in range(na)},
        compiler_params=pltpu.CompilerParams(has_side_effects=EFFECT),
    )(*arrays, send, recv, *afters)
    return out if keep_sources else out[ns:]


_ALL_PEERS = (1, 2, 3, 4, 5, 6, 7)
_NEAR_PEERS = (1, 2, 4, 6)
_FAR_CHIPS = (2, 4, 6)


def _gather_start(stage, names, packed, full, after, peers=_ALL_PEERS):
    def copies(src, land, send, recv):
        me = _my_index()
        for k in range(N_DEV):
            @pl.when(me == k)
            def _():
                for w, name in enumerate(names):
                    kind, _, pad, _ = _SPEC[name]
                    dst = _window(land[w], kind, k * pad, pad)
                    for mask in peers:
                        pltpu.make_async_remote_copy(
                            src_ref=src[w], dst_ref=dst, send_sem=send.at[w],
                            recv_sem=recv.at[w], device_id=_device_tuple(k ^ mask),
                            device_id_type=MESH).start()

    return _split_start(f"gather_start{stage}", len(names), copies,
                        [packed[n] for n in names], [full[n] for n in names], after)


def _gather_wait(stage, names, started, after, count=N_DEV - 1):
    send, recv, src, land, _ = started

    def bytes_of(w, ref):
        kind, _, pad, _ = _SPEC[names[w]]
        return _window(ref, kind, 0, count * pad)

    return dict(zip(names, _split_wait(f"gather_wait{stage}", len(names), bytes_of,
                                       send, recv, src, land, after)))


def _relay_start(stage, names, full, after):
    def copies(_, land, send, recv):
        me = _my_index()
        for k in range(N_DEV):
            @pl.when(me == k)
            def _():
                for w, name in enumerate(names):
                    kind, _, pad, _ = _SPEC[name]
                    for mask in _FAR_CHIPS:
                        win = _window(land[w], kind, (k ^ mask) * pad, pad)
                        pltpu.make_async_remote_copy(
                            src_ref=win, dst_ref=win, send_sem=send.at[w], recv_sem=recv.at[w],
                            device_id=_device_tuple(k ^ 1), device_id_type=MESH).start()

    return _split_start(f"relay_start{stage}", len(names), copies, [],
                        [full[n] for n in names], after)


def _scatter_start(stage, names, grads, after):
    def copies(src, land, send, recv):
        me = _my_index()
        for k in range(N_DEV):
            @pl.when(me != k)
            def _():
                slot = lax.rem(me + (N_DEV - 1 - k), N_DEV)
                for w, name in enumerate(names):
                    kind, _, pad, _ = _SPEC[name]
                    pltpu.make_async_remote_copy(
                        src_ref=_window(src[w], kind, k * pad, pad), dst_ref=land[w].at[slot],
                        send_sem=send.at[w], recv_sem=recv.at[w],
                        device_id=_device_tuple(k), device_id_type=MESH).start()

    lands = [lax.empty((N_DEV - 1,) + _shard_shape(_SPEC[m][0], _SPEC[m][2], _SPEC[m][3]), BF16)
             for m in names]
    return _split_start(f"scatter_start{stage}", len(names), copies, grads, lands, after)


def _scatter_wait(stage, names, started, after):
    send, recv, src, land, _ = started
    n = len(names)
    out = _split_wait(f"scatter_wait{stage}", n, lambda w, ref: ref, send, recv, src, land, after,
                      keep_sources=True)
    return dict(zip(names, out[:n])), dict(zip(names, out[n:]))


N_CHIPS = N_DEV // 2


def _pair_start(stage, names, grads, after):
    def copies(src, land, send, recv):
        me = _my_index()
        for k in range(N_DEV):
            @pl.when(me == k)
            def _():
                for w, name in enumerate(names):
                    kind, _, pad, _ = _SPEC[name]
                    for chip in range(N_CHIPS):
                        j = 2 * chip + ((k ^ 1) & 1)
                        pltpu.make_async_remote_copy(
                            src_ref=_window(src[w], kind, j * pad, pad), dst_ref=land[w].at[chip],
                            send_sem=send.at[w], recv_sem=recv.at[w],
                            device_id=_device_tuple(k ^ 1), device_id_type=MESH).start()

    lands = [lax.empty((N_CHIPS,) + _shard_shape(_SPEC[m][0], _SPEC[m][2], _SPEC[m][3]), BF16)
             for m in names]
    return _split_start(f"pair_start{stage}", len(names), copies, grads, lands, after)


def _pair_sum(dw_full, pair, *, pad, name):
    other = dw_full.shape[1]

    def body(own_ref, pair_ref, out_ref):
        out_ref[0] = _bf(own_ref[...].astype(F32) + pair_ref[0].astype(F32))

    slot = pl.BlockSpec((1, pad, other), lambda q: (q, 0, 0))
    return pl.pallas_call(
        body, name=name, grid=(N_CHIPS,),
        in_specs=[pl.BlockSpec((pad, other), lambda q: (2 * q + lax.axis_index("c"), 0)), slot],
        out_specs=slot, out_shape=_sds((N_CHIPS, pad, other), BF16),
        compiler_params=_params("arbitrary"),
    )(dw_full, pair)


def _chip_start(stage, names, sums, after):
    def copies(src, land, send, recv):
        me = _my_index()
        my_chip = lax.shift_right_logical(me, 1)
        for k in range(N_DEV):
            @pl.when((me != k) & (((me ^ k) & 1) == 0))
            def _():
                slot = lax.rem(my_chip + (N_CHIPS - 1 - k // 2), N_CHIPS)
                for w in range(len(names)):
                    pltpu.make_async_remote_copy(
                        src_ref=src[w].at[k // 2], dst_ref=land[w].at[slot],
                        send_sem=send.at[w], recv_sem=recv.at[w],
                        device_id=_device_tuple(k), device_id_type=MESH).start()

    lands = [lax.empty((N_CHIPS - 1,) + a.shape[1:], BF16) for a in sums]
    return _split_start(f"chip_start{stage}", len(names), copies, sums, lands, after)


def _adamw_chip(w, m, v, land, sums, *, name):
    shape = w.shape

    def body(w_ref, m_ref, v_ref, land_ref, own_ref, *outs):
        rows = pl.ds(0, shape[0])
        grad = own_ref[0, rows, :].astype(F32)
        for s in range(N_CHIPS - 1):
            grad = grad + land_ref[s, rows, :].astype(F32)
        _adam_update(w_ref, m_ref, v_ref, grad, *outs)

    whole = lambda a: pl.BlockSpec(a.shape, lambda i: (0,) * a.ndim)
    own = pl.BlockSpec((1,) + sums.shape[1:],
                       lambda i: (2 * lax.axis_index("x") + lax.axis_index("y"), 0, 0))
    return pl.pallas_call(
        body, name=name, grid=(1,),
        in_specs=[whole(w), whole(m), whole(v), whole(land), own],
        out_specs=[whole(w)] * 4, out_shape=[_sds(shape)] * 4,
        compiler_params=_params("arbitrary"),
    )(w, m, v, land, sums)


def _allreduce_small(small, after):
    shape = small.shape

    def body(in_ref, _after, out_ref, gath, send, recv):
        me = _my_index()
        for k in range(N_DEV):
            @pl.when(me != k)
            def _():
                pltpu.make_async_remote_copy(
                    src_ref=in_ref, dst_ref=gath.at[me], send_sem=send, recv_sem=recv,
                    device_id=_device_tuple(k), device_id_type=MESH).start()

            @pl.when(me == k)
            def _():
                gath[k] = in_ref[...]
        seven = gath.at[pl.ds(0, N_DEV - 1)]
        pltpu.make_async_remote_copy(
            src_ref=seven, dst_ref=seven, send_sem=send, recv_sem=recv,
            device_id=_device_tuple(0), device_id_type=MESH).wait()
        total = gath[0]
        for s in range(1, N_DEV):
            total = total + gath[s]
        out_ref[...] = total

    return pl.pallas_call(
        body, name="allreduce_small",
        in_specs=[VMEM, ANY], out_specs=VMEM, out_shape=_sds(shape),
        scratch_shapes=[pltpu.VMEM((N_DEV,) + shape, F32),
                        pltpu.SemaphoreType.DMA, pltpu.SemaphoreType.DMA],
    )(small, after)


def _adam_update(w_ref, m_ref, v_ref, grad, grad_ref, delta_ref, nm_ref, nv_ref):
    new_m = ADAM_B1 * m_ref[...] + (1.0 - ADAM_B1) * grad
    new_v = ADAM_B2 * v_ref[...] + (1.0 - ADAM_B2) * (grad * grad)
    m_hat = new_m / (1.0 - ADAM_B1 ** ADAM_STEP)
    v_hat = new_v / (1.0 - ADAM_B2 ** ADAM_STEP)
    grad_ref[...] = grad
    delta_ref[...] = -ADAM_LR * (m_hat / (jnp.sqrt(v_hat) + ADAM_EPS) + ADAM_WD * w_ref[...])
    nm_ref[...] = new_m
    nv_ref[...] = new_v


def _adamw(w, m, v, g, *, name):
    def body(w_ref, m_ref, v_ref, g_ref, *outs):
        _adam_update(w_ref, m_ref, v_ref, g_ref[...], *outs)

    whole = pl.BlockSpec(w.shape, lambda i: (0,) * w.ndim)
    return pl.pallas_call(
        body, name=name, grid=(1,), in_specs=[whole] * 4, out_specs=[whole] * 4,
        out_shape=[_sds(w.shape)] * 4, compiler_params=_params("arbitrary"),
    )(w, m, v, g)


def _adamw_gains(small, params):
    n = len(params)

    def body(small_ref, *refs):
        ins, outs = refs[:3 * n], refs[3 * n:]
        for r in range(n):
            width = ins[3 * r].shape[1]
            if width == D_MODEL:
                grad = small_ref[pl.ds(r, 1), :]
            else:
                grad = small_ref[pl.ds(len(_GAINS), 1), pl.ds((r - len(_GAINS)) * width, width)]
            _adam_update(*ins[3 * r:3 * r + 3], grad, *outs[4 * r:4 * r + 4])

    whole = lambda a: pl.BlockSpec(a.shape, lambda i: (0, 0))
    flat = [a for group in params for a in group]
    return pl.pallas_call(
        body, name="adamw_gains", grid=(1,),
        in_specs=[whole(small)] + [whole(a) for a in flat],
        out_specs=[whole(w) for w, _, _ in params for _ in range(4)],
        out_shape=[_sds(w.shape) for w, _, _ in params for _ in range(4)],
        compiler_params=_params("arbitrary"),
    )(small, *flat)


def _adamw_shard(w, m, v, land, dw_full, *, kind, pad, name):
    shape = w.shape
    other = shape[0] if kind == "col" else shape[1]

    def body(w_ref, m_ref, v_ref, land_ref, own_ref, *outs):
        valid = ((slice(None), pl.ds(0, shape[1])) if kind == "col"
                 else (pl.ds(0, shape[0]), slice(None)))
        grad = own_ref[valid].astype(F32)
        for s in range(N_DEV - 1):
            grad = grad + land_ref[(s,) + valid].astype(F32)
        _adam_update(w_ref, m_ref, v_ref, grad, *outs)

    whole = lambda a: pl.BlockSpec(a.shape, lambda i: (0,) * a.ndim)
    own = pl.BlockSpec(_shard_shape(kind, pad, other),
                       (lambda i: (0, _my_index())) if kind == "col" else (lambda i: (_my_index(), 0)))
    return pl.pallas_call(
        body, name=name, grid=(1,),
        in_specs=[whole(w), whole(m), whole(v), whole(land), own],
        out_specs=[whole(w)] * 4, out_shape=[_sds(shape)] * 4,
        compiler_params=_params("arbitrary"),
    )(w, m, v, land, dw_full)


_GAINS = ("ffn1_pre", "ffn1_post", "mix_pre", "mix_post", "ffn2_pre", "ffn2_post", "ple_post")
_SMALL_ROWS = 16


def _stack_gains(get):
    return jnp.concatenate([get(n) for n in _GAINS]
                           + [jnp.concatenate([get("out_sb"), get("out_ch")], axis=1)], axis=0)


def kernel(x, p, g_ffn1_pre, g_ffn1_post, w_ffn1_gate, w_ffn1_up, w_ffn1_down, g_mix_pre, g_mix_post, w_in, g_out_sb, g_out_ch, rel_bias, w_out, g_ffn2_pre, g_ffn2_post, w_ffn2_gate, w_ffn2_up, w_ffn2_down, w_ple_proj, w_ple_gate, g_ple_post, loss_target, m_g_ffn1_pre, m_g_ffn1_post, m_w_ffn1_gate, m_w_ffn1_up, m_w_ffn1_down, m_g_mix_pre, m_g_mix_post, m_w_in, m_g_out_sb, m_g_out_ch, m_rel_bias, m_w_out, m_g_ffn2_pre, m_g_ffn2_post, m_w_ffn2_gate, m_w_ffn2_up, m_w_ffn2_down, m_w_ple_proj, m_w_ple_gate, m_g_ple_post, v_g_ffn1_pre, v_g_ffn1_post, v_w_ffn1_gate, v_w_ffn1_up, v_w_ffn1_down, v_g_mix_pre, v_g_mix_post, v_w_in, v_g_out_sb, v_g_out_ch, v_rel_bias, v_w_out, v_g_ffn2_pre, v_g_ffn2_post, v_w_ffn2_gate, v_w_ffn2_up, v_w_ffn2_down, v_w_ple_proj, v_w_ple_gate, v_g_ple_post):
    given = dict(locals())
    wnames = [n for n, *_ in _WEIGHTS]

    def shard(prefix, n):
        a = given[prefix + "w_" + n][0]
        return a.T if n in _TRANSPOSED else a

    anchor = x[0]
    first = _GATHER_STAGES[0]
    packed, full = _pack_weights(first, [shard("", n) for n in first], anchor, name="pack_first")
    two_level = (0, 2)
    gathers = {}

    def start_stage(stage, after):
        peers = _NEAR_PEERS if stage in two_level else _ALL_PEERS
        gathers[stage] = _gather_start(stage, _GATHER_STAGES[stage], packed, full, after,
                                       peers=peers)

    start_stage(0, anchor)
    rest = [n for n in wnames if n not in first]
    packed_rest, full_rest = _pack_weights(rest, [shard("", n) for n in rest], gathers[0][-1],
                                           name="pack_rest")
    packed.update(packed_rest)
    full.update(full_rest)

    relays = {}

    def first_level(stage, after):
        names = _GATHER_STAGES[stage]
        last_stage = stage + 1 == len(_GATHER_STAGES)
        count = len(_NEAR_PEERS) if stage in two_level else N_DEV - 1
        if stage == 0:
            after = (after, packed_rest[rest[0]])
        ws = _gather_wait(stage, names, gathers[stage], after, count=count)
        if not last_stage:
            start_stage(stage + 1, ws[names[0]])
        if stage in two_level:
            relays[stage] = _relay_start(stage, names, ws,
                                         anchor if last_stage else gathers[stage + 1][-1])
            return ws, relays[stage][-1]
        return ws, None if last_stage else gathers[stage + 1][-1]

    def weights_early(stage, after):
        return first_level(stage, after)[1][:1, :1]

    def weights_for(stage, after):
        names = _GATHER_STAGES[stage]
        ws, token = (None, None) if stage in relays else first_level(stage, after)
        if stage in relays:
            relay = relays[stage]
            ws = _gather_wait(f"{stage}r", names, relay, after, count=len(_FAR_CHIPS))
            token = None if stage + 1 == len(_GATHER_STAGES) else gathers[stage + 1][-1]
        return ws, jnp.zeros((1, 1), F32) if token is None else token[:1, :1]

    scatters = {}

    last = len(_SCATTER_STAGES) - 1

    def grads_done(stage, grads):
        names = _SCATTER_STAGES[stage]
        start = _pair_start if stage == last else _scatter_start
        scatters[stage] = start(stage, names, [grads[n] for n in names], anchor)
        return scatters[stage][-1][:1, :1]

    gains = {n: given["g_" + n] for n in _GAINS + ("out_sb", "out_ch")}
    fvec = _rel_bias_to_fvec(rel_bias[0])
    loss, dx, dg, dbias = _local_step(x[0], p[0, 0], loss_target[0], gains,
                                      weights_for, grads_done, fvec, weights_early)

    results = {}

    def finish(stage, after):
        names = _SCATTER_STAGES[stage]
        dws, lands = _scatter_wait(stage, names, scatters[stage], after)
        for n in names:
            kind, _, pad, _ = _SPEC[n]
            out = _adamw_shard(shard("", n), shard("m_", n), shard("v_", n), lands[n], dws[n],
                               kind=kind, pad=pad, name="adamw_" + n)
            results["w_" + n] = [a.T for a in out] if n in _TRANSPOSED else out
        return results["w_" + names[-1]][0]

    names = _SCATTER_STAGES[last]
    whole = lambda w, ref: ref
    send, recv, src, land, _ = scatters[last]
    out = _split_wait(f"pair_wait{last}", len(names), whole, send, recv, src, land, dx,
                      keep_sources=True)
    sums = [_pair_sum(dwf, pair, pad=_SPEC[n][2], name="pair_sum_" + n)
            for n, dwf, pair in zip(names, out[:len(names)], out[len(names):])]
    send, recv, src, land, after = _chip_start(last, names, sums, anchor)
    for stage in range(last):
        after = finish(stage, after)
    dfvec = _bias_grad(dbias, after, name="bias_grad")
    loss_col = jnp.pad(loss[:, :1], ((0, N_DEV - 1), (0, D_MODEL - CH_WIN - 1)))
    dfv = jnp.concatenate([dfvec[:, 0, :], loss_col], axis=1)
    small = _allreduce_small(jnp.concatenate([_stack_gains(lambda n: dg[n]), dfv], axis=0), after)
    gain_names = _GAINS + ("out_sb", "out_ch")
    gain_out = _adamw_gains(small, [(given["g_" + n], given["m_g_" + n], given["v_g_" + n])
                                    for n in gain_names])
    for r, n in enumerate(gain_names):
        results["g_" + n] = gain_out[4 * r:4 * r + 4]
    d_rel = _fvec_grad_to_rel_bias(small[N_DEV:, :CH_WIN].reshape(N_DEV, 1, CH_WIN))
    results["rel_bias"] = _adamw(rel_bias[0], m_rel_bias[0], v_rel_bias[0], d_rel,
                                 name="adamw_rel_bias")
    out = _split_wait(f"chip_wait{last}", len(names), whole, send, recv, src, land,
                      results["rel_bias"][0], keep_sources=True)
    for n, own, landed in zip(names, out[:len(names)], out[len(names):]):
        res = _adamw_chip(shard("", n), shard("m_", n), shard("v_", n), landed, own,
                          name="adamw_" + n)
        results["w_" + n] = [a.T for a in res] if n in _TRANSPOSED else res

    order = ("g_ffn1_pre", "g_ffn1_post", "w_ffn1_gate", "w_ffn1_up", "w_ffn1_down",
             "g_mix_pre", "g_mix_post", "w_in", "g_out_sb", "g_out_ch", "rel_bias", "w_out",
             "g_ffn2_pre", "g_ffn2_post", "w_ffn2_gate", "w_ffn2_up", "w_ffn2_down",
             "w_ple_proj", "w_ple_gate", "g_ple_post")

    def leaf(name, idx):
        a = results[name][idx]
        return a if name.startswith("g_") else a[None]

    total_loss = small[N_DEV, CH_WIN]
    return (total_loss, dx[None],
            *[leaf(n, 0) for n in order], *[leaf(n, 1) for n in order],
            *[leaf(n, 2) for n in order], *[leaf(n, 3) for n in order])
```

```python
import jax
import jax.numpy as jnp
from jax import lax
from jax.experimental import pallas as pl
from jax.experimental.pallas import tpu as pltpu

F32 = jnp.float32
BF16 = jnp.bfloat16

N_DEV = 8
D_MODEL = 1024
D_FF = 2816
FF_SHARD = D_FF // N_DEV
FF_SHARD_PAD = 384
D_FF_PAD = FF_SHARD_PAD * N_DEV
QKV_WIDTH = 3 * D_MODEL
QKV_SHARD = QKV_WIDTH // N_DEV
PLE_DIM = 256
ROW_SHARD = D_MODEL // N_DEV
HEAD_DIM = 64
PAIR = 2 * HEAD_DIM
N_PAIRS = 4
CHUNK = 64
LOOKBACK = 8
REL_CLIP = 128
N_REL = 2 * REL_CLIP + 1
CH_QB = 256
CH_LOOK = LOOKBACK * CHUNK
CH_WIN = CH_LOOK + CH_QB
SB_QB = 512
SB_KB = 256
SB_GROUP = 2
SB_LANES = tuple(slice(g * 128, (g + 1) * 128) for g in range(SB_GROUP))
EPS = 1e-6
NEG_INF = -1e30
ATT_SCALE = HEAD_DIM ** -0.5
ADAM_LR = 0.001
ADAM_B1 = 0.9
ADAM_B2 = 0.999
ADAM_EPS = 1e-08
ADAM_WD = 0.01
ADAM_STEP = 10
VMEM_LIMIT_BYTES = 48 * 1024 * 1024
MESH = pl.DeviceIdType.MESH

ANY = pl.BlockSpec(memory_space=pl.ANY)
VMEM = pl.BlockSpec(memory_space=pltpu.VMEM)


def _params(*sem):
    return pltpu.CompilerParams(dimension_semantics=sem or None,
                                vmem_limit_bytes=VMEM_LIMIT_BYTES)


def _sds(shape, dtype=F32):
    return jax.ShapeDtypeStruct(shape, dtype)


def _bf(x):
    return x.astype(BF16)


def _dot(a, b):
    return jnp.dot(_bf(a), _bf(b), preferred_element_type=F32)


def _dot_nt(a, b):
    return lax.dot_general(_bf(a), _bf(b), (((1,), (1,)), ((), ())),
                           preferred_element_type=F32)


def _dot_tn(a, b):
    return lax.dot_general(_bf(a), _bf(b), (((0,), (0,)), ((), ())),
                           preferred_element_type=F32)


def _sigmoid(x):
    return 1.0 / (1.0 + jnp.exp(-x))


def _softplus(x):
    return jnp.maximum(x, 0.0) + jnp.log(1.0 + jnp.exp(-jnp.abs(x)))


def _rstd(x):
    return lax.rsqrt(jnp.mean(x * x, axis=-1, keepdims=True) + EPS)


def _rms(x, g):
    return x * _rstd(x) * g


def _rms_bwd(dy, x, g):
    r = _rstd(x)
    w = dy * g
    dx = r * (w - x * (r * r) * jnp.mean(w * x, axis=-1, keepdims=True))
    dg = jnp.sum(dy * (x * r), axis=0, keepdims=True)
    return dx, dg


def _head_masks():
    lane = lax.broadcasted_iota(jnp.int32, (1, PAIR), 1)
    return lane < HEAD_DIM, lane >= HEAD_DIM


def _ffn_fwd(x, g_pre, g_post, wg, wu, wd, *, name):
    t = x.shape[0]
    tm, tj = 512, 1024
    ni, nj = t // tm, D_FF_PAD // tj

    def body(x_ref, gpre_ref, gpost_ref, wg_ref, wu_ref, wd_ref,
             h_ref, n_ref, a_ref, b_ref, f_ref, acc_ref):
        j = pl.program_id(1)

        @pl.when(j == 0)
        def _():
            n_ref[...] = _bf(_rms(x_ref[...], gpre_ref[...]))
            acc_ref[...] = jnp.zeros_like(acc_ref)

        n = n_ref[...]
        a = _dot_nt(n, wg_ref[...])
        b = _dot_nt(n, wu_ref[...])
        a_ref[...] = a
        b_ref[...] = b
        hmid = a * _sigmoid(a) * b
        acc_ref[...] += jnp.dot(_bf(hmid), wd_ref[...], preferred_element_type=F32)

        @pl.when(j == nj - 1)
        def _():
            f = acc_ref[...]
            f_ref[...] = f
            h_ref[...] = x_ref[...] + 0.5 * _rms(f, gpost_ref[...])

    row = pl.BlockSpec((tm, D_MODEL), lambda i, j: (i, 0))
    gain = pl.BlockSpec((1, D_MODEL), lambda i, j: (0, 0))
    col = pl.BlockSpec((tm, tj), lambda i, j: (i, j))
    wtile = pl.BlockSpec((tj, D_MODEL), lambda i, j: (j, 0))
    return pl.pallas_call(
        body, name=name, grid=(ni, nj),
        in_specs=[row, gain, gain, wtile, wtile, wtile],
        out_specs=[row, row, col, col, row],
        out_shape=[_sds((t, D_MODEL)), _sds((t, D_MODEL), BF16),
                   _sds((t, D_FF_PAD)), _sds((t, D_FF_PAD)), _sds((t, D_MODEL))],
        scratch_shapes=[pltpu.VMEM((tm, D_MODEL), F32)],
        compiler_params=_params("arbitrary", "arbitrary"),
    )(x, g_pre, g_post, wg, wu, wd)


def _ffn_bwd(n, df, a, b, wg, wu, wd, *, name):
    t = n.shape[0]
    tj, tm, ts = 256, t, 512
    nj, ni, ns = D_FF_PAD // tj, t // tm, tm // ts

    def body(n_hbm, df_hbm, a_ref, b_ref, wg_ref, wu_ref, wd_ref,
             dwg_ref, dwu_ref, dwd_ref, dn_hbm,
             n_v, df_v, dn_v, ag, au, ad, sem):
        j, i = pl.program_id(0), pl.program_id(1)

        @pl.when((j == 0) & (i == 0))
        def _():
            c1 = pltpu.make_async_copy(n_hbm, n_v, sem.at[0])
            c2 = pltpu.make_async_copy(df_hbm, df_v, sem.at[1])
            c1.start()
            c2.start()
            dn_v[...] = jnp.zeros_like(dn_v)
            c1.wait()
            c2.wait()

        @pl.when(i == 0)
        def _():
            ag[...] = jnp.zeros_like(ag)
            au[...] = jnp.zeros_like(au)
            ad[...] = jnp.zeros_like(ad)

        wgj, wuj, wdj = wg_ref[...], wu_ref[...], wd_ref[...]
        for s in range(ns):
            local = pl.ds(s * ts, ts)
            rows = pl.ds(pl.multiple_of(i * tm + s * ts, ts), ts)
            av, bv = a_ref[local, :], b_ref[local, :]
            sig = _sigmoid(av)
            silu = av * sig
            dfr = df_v[rows, :]
            nr = n_v[rows, :]
            dhmid = _dot_nt(dfr, wdj)
            da = dhmid * bv * (sig * (1.0 + av * (1.0 - sig)))
            db = dhmid * silu
            ad[...] += _dot_tn(silu * bv, dfr)
            ag[...] += _dot_tn(da, nr)
            au[...] += _dot_tn(db, nr)
            dn_v[rows, :] += _dot(da, wgj) + _dot(db, wuj)

        @pl.when(i == ni - 1)
        def _():
            dwg_ref[...] = _bf(ag[...])
            dwu_ref[...] = _bf(au[...])
            dwd_ref[...] = _bf(ad[...])

        @pl.when((j == nj - 1) & (i == ni - 1))
        def _():
            c = pltpu.make_async_copy(dn_v, dn_hbm, sem.at[0])
            c.start()
            c.wait()

    roww = pl.BlockSpec((tj, D_MODEL), lambda j, i: (j, 0))
    act = pl.BlockSpec((tm, tj), lambda j, i: (i, j))
    return pl.pallas_call(
        body, name=name, grid=(nj, ni),
        in_specs=[ANY, ANY, act, act, roww, roww, roww],
        out_specs=[roww, roww, roww, ANY],
        out_shape=[_sds((D_FF_PAD, D_MODEL), BF16)] * 3 + [_sds((t, D_MODEL))],
        scratch_shapes=[pltpu.VMEM((t, D_MODEL), BF16), pltpu.VMEM((t, D_MODEL), BF16),
                        pltpu.VMEM((t, D_MODEL), F32)]
        + [pltpu.VMEM((tj, D_MODEL), F32)] * 3 + [pltpu.SemaphoreType.DMA((2,))],
        compiler_params=_params("arbitrary", "arbitrary"),
    )(n, df, a, b, wg, wu, wd)


def _junction(dres, pre=None, post=None, *, name):
    t = dres.shape[0]
    tm = 512
    ni = t // tm
    n_in = 1 + (3 if pre else 0) + (2 if post else 0)
    coef = post[2] if post else None

    def body(*refs):
        ins, outs = list(refs[:n_in]), list(refs[n_in:])
        i = pl.program_id(0)
        dh = ins.pop(0)[...]
        if pre:
            dn_ref, x_ref, gpre_ref = ins.pop(0), ins.pop(0), ins.pop(0)
            dh_ref, dgpre_ref = outs.pop(0), outs.pop(0)
            dx, dg = _rms_bwd(dn_ref[...], x_ref[...], gpre_ref[...])
            dh = dh + dx
            dh_ref[...] = dh

            @pl.when(i == 0)
            def _():
                dgpre_ref[...] = jnp.zeros_like(dgpre_ref)
            dgpre_ref[...] += dg
        if post:
            f_ref, gpost_ref = ins.pop(0), ins.pop(0)
            df_ref, dgpost_ref = outs.pop(0), outs.pop(0)
            df, dg = _rms_bwd(coef * dh, f_ref[...], gpost_ref[...])
            df_ref[...] = _bf(df)

            @pl.when(i == 0)
            def _():
                dgpost_ref[...] = jnp.zeros_like(dgpost_ref)
            dgpost_ref[...] += dg

    row = pl.BlockSpec((tm, D_MODEL), lambda i: (i, 0))
    gain = pl.BlockSpec((1, D_MODEL), lambda i: (0, 0))
    args, in_specs, out_specs, out_shape = [dres], [row], [], []
    if pre:
        args += list(pre)
        in_specs += [row, row, gain]
        out_specs += [row, gain]
        out_shape += [_sds((t, D_MODEL)), _sds((1, D_MODEL))]
    if post:
        args += [post[0], post[1]]
        in_specs += [row, gain]
        out_specs += [row, gain]
        out_shape += [_sds((t, D_MODEL), BF16), _sds((1, D_MODEL))]
    return pl.pallas_call(
        body, name=name, grid=(ni,), in_specs=in_specs, out_specs=out_specs,
        out_shape=out_shape, compiler_params=_params("arbitrary"),
    )(*args)


def _qkv_fwd(h, g, win, *, name):
    t = h.shape[0]
    tm, tn = min(1024, t), 1024
    ni, nj = t // tm, QKV_WIDTH // tn

    def body(h_ref, g_ref, w_ref, qkv_ref, u_ref):
        @pl.when(pl.program_id(1) == 0)
        def _():
            u_ref[...] = _bf(_rms(h_ref[...], g_ref[...]))
        qkv_ref[...] = jnp.dot(u_ref[...], w_ref[...], preferred_element_type=F32)

    row = pl.BlockSpec((tm, D_MODEL), lambda i, j: (i, 0))
    return pl.pallas_call(
        body, name=name, grid=(ni, nj),
        in_specs=[row, pl.BlockSpec((1, D_MODEL), lambda i, j: (0, 0)),
                  pl.BlockSpec((D_MODEL, tn), lambda i, j: (0, j))],
        out_specs=[pl.BlockSpec((tm, tn), lambda i, j: (i, j)), row],
        out_shape=[_sds((t, QKV_WIDTH)), _sds((t, D_MODEL), BF16)],
        compiler_params=_params("arbitrary", "arbitrary"),
    )(h, g, win)


def _qkv_bwd(dq, dk, dv, u, win, *, name):
    t = u.shape[0]
    tn, ts = 512, 512
    nj, ns = QKV_WIDTH // tn, t // ts

    def body(dq_ref, dk_ref, dv_ref, u_ref, w_ref, dw_ref, du_hbm, du_v, acc_ref, sem):
        j = pl.program_id(0)

        @pl.when(j == 0)
        def _():
            du_v[...] = jnp.zeros_like(du_v)

        wj = w_ref[...]
        for role, d_ref in enumerate((dq_ref, dk_ref, dv_ref)):
            @pl.when(j % 3 == role)
            def _():
                acc_ref[...] = jnp.zeros_like(acc_ref)
                for s in range(ns):
                    rows = pl.ds(s * ts, ts)
                    dcol = d_ref[rows, :]
                    acc_ref[...] += _dot_tn(u_ref[rows, :], dcol)
                    du_v[rows, :] += _dot_nt(dcol, wj)
                dw_ref[...] = _bf(acc_ref[...])

        @pl.when(j == nj - 1)
        def _():
            c = pltpu.make_async_copy(du_v, du_hbm, sem)
            c.start()
            c.wait()

    colw = pl.BlockSpec((D_MODEL, tn), lambda j: (0, j))
    grp = pl.BlockSpec((t, tn), lambda j: (0, j // 3))
    return pl.pallas_call(
        body, name=name, grid=(nj,),
        in_specs=[grp, grp, grp, pl.BlockSpec((t, D_MODEL), lambda j: (0, 0)), colw],
        out_specs=[colw, ANY],
        out_shape=[_sds((D_MODEL, QKV_WIDTH), BF16), _sds((t, D_MODEL))],
        scratch_shapes=[pltpu.VMEM((t, D_MODEL), F32), pltpu.VMEM((D_MODEL, tn), F32),
                        pltpu.SemaphoreType.DMA],
        compiler_params=_params("arbitrary"),
    )(dq, dk, dv, u, win)


def _sb_stack(x):
    lo, hi = _head_masks()
    return jnp.concatenate([jnp.where(lo, x, 0.0), jnp.where(hi, x, 0.0)], axis=0)


def _sb_unstack(x2, blk):
    return jnp.where(_head_masks()[0], x2[:blk], x2[blk:])


def _sb_rows_from(x2, blk, r0):
    return x2 if r0 == 0 else jnp.concatenate([x2[r0:blk], x2[blk + r0:]], axis=0)


def _sb_rows_merge(full2, sub2, blk, r0):
    if r0 == 0:
        return sub2
    rows = blk - r0
    return jnp.concatenate([full2[:r0], sub2[:rows], full2[blk:blk + r0], sub2[rows:]], axis=0)


def _sb_mask(qb, kb, offset):
    r = lax.broadcasted_iota(jnp.int32, (2 * qb, kb), 0) & (qb - 1)
    c = lax.broadcasted_iota(jnp.int32, (2 * qb, kb), 1) + offset
    return c < r


def _tri(n, keep):
    r = lax.broadcasted_iota(jnp.int32, (n, n), 0)
    c = lax.broadcasted_iota(jnp.int32, (n, n), 1)
    return jnp.where(keep(r, c), 1.0, 0.0).astype(BF16)


def _cumsum01(x, u):
    m = x.shape[0]
    hi = _bf(x)
    lo = _bf(x - hi.astype(F32))
    both = jnp.dot(jnp.concatenate([hi, lo], axis=0), u, preferred_element_type=F32)
    return both[:m] + both[m:]


def _sb_fwd(qkv, *, name):
    t = qkv.shape[0]
    blk, kb = min(SB_QB, t), SB_KB
    ni, per = t // blk, blk // kb

    def body(q_ref, k_ref, v_ref, o_ref, ltot_ref):
        i = pl.program_id(1)
        u_after = _tri(kb, lambda r, c: r > c)
        q2 = [_bf(_sb_stack(q_ref[:, lanes] * ATT_SCALE)) for lanes in SB_LANES]

        def tile(g, k0, mask, acc, c_l):
            kj = k_ref[pl.ds(k0, kb), SB_LANES[g]]
            vj = v_ref[pl.ds(k0, kb), SB_LANES[g]]
            z = _dot_nt(q2[g], kj)
            sp = _softplus(z)
            lf = -sp if mask is None else jnp.where(mask, -sp, 0.0)
            a = jnp.exp(z - sp + _cumsum01(lf, u_after) + c_l)
            if mask is not None:
                a = jnp.where(mask, a, 0.0)
            return acc + _dot(a, vj), c_l + jnp.sum(lf, axis=1, keepdims=True)

        def tiles(k0, mask, carry):
            return tuple(tile(g, k0, mask, *carry[g]) for g in range(SB_GROUP))

        carry = ((jnp.zeros((2 * blk, PAIR), F32), jnp.zeros((2 * blk, 1), F32)),) * SB_GROUP
        for d in reversed(range(per)):
            carry = tiles(pl.multiple_of(i * blk + d * kb, kb), _sb_mask(blk, kb, d * kb), carry)
        carry = lax.fori_loop(
            1, per * i + 1,
            lambda jj, c: tiles(pl.multiple_of((per * i - jj) * kb, kb), None, c), carry)
        for g, (acc, c_l) in enumerate(carry):
            o_ref[:, SB_LANES[g]] = _sb_unstack(acc, blk)
            ltot_ref[:, SB_LANES[g]] = _sb_unstack(jnp.broadcast_to(c_l, (2 * blk, PAIR)), blk)

    width = SB_GROUP * PAIR
    blkspec = pl.BlockSpec((blk, width), lambda p, i: (i, p))
    n_steps = N_PAIRS // SB_GROUP
    return pl.pallas_call(
        body, name=name, grid=(n_steps, ni),
        in_specs=[blkspec,
                  pl.BlockSpec((t, width), lambda p, i: (0, n_steps + p)),
                  pl.BlockSpec((t, width), lambda p, i: (0, 2 * n_steps + p))],
        out_specs=[blkspec, blkspec],
        out_shape=[_sds((t, D_MODEL)), _sds((t, D_MODEL // 2))],
        compiler_params=_params("arbitrary", "arbitrary"),
    )(qkv, qkv, qkv)


def _sb_bwd(qkv, ltot, do, *, name):
    t = qkv.shape[0]
    blk, kb = min(SB_QB, t), SB_KB
    ni, per = t // blk, blk // kb

    def body(q_ref, k_ref, v_ref, lt_ref, do_ref, dq_ref, dkout_ref, dvout_ref, dk_ref, dv_ref):
        i = pl.program_id(1)

        @pl.when(i == 0)
        def _():
            dk_ref[...] = jnp.zeros_like(dk_ref)
            dv_ref[...] = jnp.zeros_like(dv_ref)

        u_upto = _tri(kb, lambda r, c: r <= c)
        u_before = _tri(kb, lambda r, c: r < c)
        lane = lax.broadcasted_iota(jnp.int32, (1, PAIR), 1)
        q2 = [_bf(_sb_stack(q_ref[:, lanes] * ATT_SCALE)) for lanes in SB_LANES]
        do2 = [_bf(_sb_stack(do_ref[:, lanes])) for lanes in SB_LANES]
        total = [jnp.concatenate(
            [jnp.sum(jnp.where(lane == h * HEAD_DIM, lt_ref[:, lanes], 0.0), axis=1, keepdims=True)
             for h in range(2)], axis=0) for lanes in SB_LANES]

        def tile(g, ops, k0, mask, dq_acc, c_l, c_g):
            qg, dog, tot = ops
            krows = pl.ds(k0, kb)
            kj = k_ref[krows, SB_LANES[g]]
            vj = v_ref[krows, SB_LANES[g]]
            z = _dot_nt(qg, kj)
            sp = _softplus(z)
            sig = jnp.exp(z - sp)
            lf = -sp if mask is None else jnp.where(mask, -sp, 0.0)
            a = jnp.exp(z - sp + tot - (_cumsum01(lf, u_upto) + c_l))
            if mask is not None:
                a = jnp.where(mask, a, 0.0)
            gw = a * _dot_nt(dog, vj)
            g_before = jnp.dot(_bf(gw), u_before, preferred_element_type=F32) + c_g
            dz = gw * (1.0 - sig) - g_before * sig
            if mask is not None:
                dz = jnp.where(mask, dz, 0.0)
            dk_ref[krows, SB_LANES[g]] += _dot_tn(dz, qg)
            dv_ref[krows, SB_LANES[g]] += _dot_tn(a, dog)
            return (dq_acc + _dot(dz, kj), c_l + jnp.sum(lf, axis=1, keepdims=True),
                    c_g + jnp.sum(gw, axis=1, keepdims=True))

        def tiles(ops, k0, mask, carry):
            return tuple(tile(g, ops[g], k0, mask, *carry[g]) for g in range(SB_GROUP))

        ops = tuple(zip(q2, do2, total))
        zero = (jnp.zeros((2 * blk, PAIR), F32), jnp.zeros((2 * blk, 1), F32),
                jnp.zeros((2 * blk, 1), F32))
        carry = lax.fori_loop(
            0, per * i, lambda j, c: tiles(ops, pl.multiple_of(j * kb, kb), None, c),
            (zero,) * SB_GROUP)
        for d in range(per):
            r0 = d * kb
            sub = tiles(tuple(tuple(_sb_rows_from(a, blk, r0) for a in o) for o in ops),
                        pl.multiple_of(i * blk + r0, kb), _sb_mask(blk - r0, kb, 0),
                        tuple(tuple(_sb_rows_from(a, blk, r0) for a in c) for c in carry))
            carry = tuple(tuple(_sb_rows_merge(a, s, blk, r0) for a, s in zip(c, cs))
                          for c, cs in zip(carry, sub))
        for g, (dq_acc, _, _) in enumerate(carry):
            dq_ref[:, SB_LANES[g]] = _bf(_sb_unstack(dq_acc, blk) * ATT_SCALE)

        @pl.when(i == ni - 1)
        def _():
            dkout_ref[...] = _bf(dk_ref[...])
            dvout_ref[...] = _bf(dv_ref[...])

    width = SB_GROUP * PAIR
    n_steps = N_PAIRS // SB_GROUP
    blkspec = lambda off: pl.BlockSpec((blk, width), lambda p, i: (i, off + p))
    full = lambda off: pl.BlockSpec((t, width), lambda p, i: (0, off + p))
    return pl.pallas_call(
        body, name=name, grid=(n_steps, ni),
        in_specs=[blkspec(0), full(n_steps), full(2 * n_steps), blkspec(0), blkspec(0)],
        out_specs=[blkspec(0), full(0), full(0)],
        out_shape=[_sds((t, D_MODEL), BF16)] * 3,
        scratch_shapes=[pltpu.VMEM((t, width), F32), pltpu.VMEM((t, width), F32)],
        compiler_params=_params("arbitrary", "arbitrary"),
    )(qkv, qkv, qkv, ltot, do)


def _ch_mask(i):
    c = lax.broadcasted_iota(jnp.int32, (1, CH_WIN), 1)
    return c >= CH_LOOK - i * CH_QB


def _ch_band(row0, rows):
    r = row0 + lax.broadcasted_iota(jnp.int32, (rows, CH_WIN), 0)
    c = lax.broadcasted_iota(jnp.int32, (rows, CH_WIN), 1)
    qc = LOOKBACK + lax.shift_right_arithmetic(r, 6)
    kc = lax.shift_right_arithmetic(c, 6)
    return (kc <= qc) & (kc >= qc - LOOKBACK)


def _ch_probs(qm, kw, bias_h, mask):
    z = _dot_nt(qm, kw) * ATT_SCALE + bias_h
    z = jnp.where(mask, z, NEG_INF)
    e = jnp.exp(z - jnp.max(z, axis=1, keepdims=True))
    return e * (1.0 / jnp.sum(e, axis=1, keepdims=True))


def _ch_fill(pad_ref, src_ref, t):
    pad_ref[pl.ds(0, CH_LOOK), :] = jnp.zeros((CH_LOOK, PAIR), BF16)
    pad_ref[pl.ds(CH_LOOK, t), :] = _bf(src_ref[...])


def _ch_fwd(qkv, bias, o_in, *, name):
    t = qkv.shape[0]
    ni = t // CH_QB

    def body(q_ref, k_ref, v_ref, bias_ref, _alias, o_ref, kpad, vpad):
        i = pl.program_id(1)

        @pl.when(i == 0)
        def _():
            _ch_fill(kpad, k_ref, t)
            _ch_fill(vpad, v_ref, t)

        win = pl.ds(pl.multiple_of(i * CH_QB, CH_QB), CH_WIN)
        kw, vw = kpad[win, :], vpad[win, :]
        mask = _ch_mask(i)
        q = q_ref[...]
        outs = []
        for h, hm in enumerate(_head_masks()):
            p = _ch_probs(jnp.where(hm, q, 0.0), kw, bias_ref[h], mask)
            outs.append(_dot(p, vw))
        o_ref[...] = jnp.where(_head_masks()[0], outs[0], outs[1])

    full = lambda off: pl.BlockSpec((t, PAIR), lambda p, i: (0, off + p))
    return pl.pallas_call(
        body, name=name, grid=(N_PAIRS, ni),
        in_specs=[pl.BlockSpec((CH_QB, PAIR), lambda p, i: (i, 3 * N_PAIRS + p)),
                  full(4 * N_PAIRS), full(5 * N_PAIRS),
                  pl.BlockSpec((2, CH_QB, CH_WIN), lambda p, i: (p, 0, 0)), ANY],
        out_specs=pl.BlockSpec((CH_QB, PAIR), lambda p, i: (i, N_PAIRS + p)),
        out_shape=_sds((t, D_MODEL)),
        scratch_shapes=[pltpu.VMEM((t + CH_LOOK, PAIR), BF16)] * 2,
        input_output_aliases={4: 0},
        compiler_params=_params("arbitrary", "arbitrary"),
    )(qkv, qkv, qkv, bias, o_in)


def _ch_bwd(qkv, bias, o, do, dq_in, dk_in, dv_in, *, name):
    t = qkv.shape[0]
    ni = t // CH_QB

    def body(q_ref, k_ref, v_ref, bias_ref, o_ref, do_ref, _a0, _a1, _a2,
             dq_ref, dkout_ref, dvout_ref, dbias_ref, kpad, vpad, dkpad, dvpad):
        i = pl.program_id(1)

        @pl.when(i == 0)
        def _():
            _ch_fill(kpad, k_ref, t)
            _ch_fill(vpad, v_ref, t)
            dkpad[...] = jnp.zeros_like(dkpad)
            dvpad[...] = jnp.zeros_like(dvpad)
            dbias_ref[...] = jnp.zeros_like(dbias_ref)

        win = pl.ds(pl.multiple_of(i * CH_QB, CH_QB), CH_WIN)
        kw, vw = kpad[win, :], vpad[win, :]
        mask = _ch_mask(i)
        q, o_blk, do_blk = q_ref[...], o_ref[...], do_ref[...]
        dqs = []
        for h, hm in enumerate(_head_masks()):
            qm = _bf(jnp.where(hm, q, 0.0))
            dom = jnp.where(hm, do_blk, 0.0)
            delta = jnp.sum(dom * o_blk, axis=1, keepdims=True)
            dom = _bf(dom)
            p = _ch_probs(qm, kw, bias_ref[h], mask)
            ds = p * (_dot_nt(dom, vw) - delta)
            dbias_ref[h] += ds
            dsz = ds * ATT_SCALE
            dqs.append(_dot(dsz, kw))
            dkpad[win, :] += _dot_tn(dsz, qm)
            dvpad[win, :] += _dot_tn(p, dom)
        dq_ref[...] = _bf(jnp.where(_head_masks()[0], dqs[0], dqs[1]))

        @pl.when(i == ni - 1)
        def _():
            dkout_ref[...] = _bf(dkpad[pl.ds(CH_LOOK, t), :])
            dvout_ref[...] = _bf(dvpad[pl.ds(CH_LOOK, t), :])

    blkspec = lambda off: pl.BlockSpec((CH_QB, PAIR), lambda p, i: (i, off + p))
    full = lambda off: pl.BlockSpec((t, PAIR), lambda p, i: (0, off + p))
    bias_spec = pl.BlockSpec((2, CH_QB, CH_WIN), lambda p, i: (p, 0, 0))
    return pl.pallas_call(
        body, name=name, grid=(N_PAIRS, ni),
        in_specs=[blkspec(3 * N_PAIRS), full(4 * N_PAIRS), full(5 * N_PAIRS), bias_spec,
                  blkspec(N_PAIRS), blkspec(N_PAIRS), ANY, ANY, ANY],
        out_specs=[blkspec(N_PAIRS), full(N_PAIRS), full(N_PAIRS), bias_spec],
        out_shape=[_sds((t, D_MODEL), BF16)] * 3 + [_sds((2 * N_PAIRS, CH_QB, CH_WIN))],
        scratch_shapes=[pltpu.VMEM((t + CH_LOOK, PAIR), BF16)] * 2
        + [pltpu.VMEM((t + CH_LOOK, PAIR), F32)] * 2,
        input_output_aliases={6: 0, 7: 1, 8: 2},
        compiler_params=_params("arbitrary", "arbitrary"),
    )(qkv, qkv, qkv, bias, o, do, dq_in, dk_in, dv_in)


def _bias_expand(fvec, *, name):
    n_heads = fvec.shape[0]

    def body(f_ref, o_ref, rows8):
        row = f_ref[0]
        for r in range(8):
            rows8[pl.ds(r, 1), :] = pltpu.roll(row, r, 1)
        base = rows8[...]
        for blk in range(CH_QB // 8):
            o_ref[0, pl.ds(8 * blk, 8), :] = jnp.where(
                _ch_band(8 * blk, 8), pltpu.roll(base, 8 * blk, 1), NEG_INF)

    return pl.pallas_call(
        body, name=name, grid=(n_heads,),
        in_specs=[pl.BlockSpec((1, 1, CH_WIN), lambda h: (h, 0, 0))],
        out_specs=pl.BlockSpec((1, CH_QB, CH_WIN), lambda h: (h, 0, 0)),
        out_shape=_sds((n_heads, CH_QB, CH_WIN)),
        scratch_shapes=[pltpu.VMEM((8, CH_WIN), F32)],
        compiler_params=_params("arbitrary"),
    )(fvec)


def _bias_grad(dbias, after, *, name):
    n_heads = dbias.shape[0]
    first = CH_LOOK - REL_CLIP

    def body(d_ref, _after, o_ref, acc8):
        acc = jnp.zeros((8, CH_WIN), F32)
        for blk in range(CH_QB // 8):
            acc = acc + pltpu.roll(d_ref[0, pl.ds(8 * blk, 8), :], (CH_WIN - 8 * blk) % CH_WIN, 1)
        acc8[...] = acc
        dvec = jnp.zeros((1, CH_WIN), F32)
        for r in range(8):
            dvec = dvec + pltpu.roll(acc8[pl.ds(r, 1), :], (CH_WIN - r) % CH_WIN, 1)
        lane = lax.broadcasted_iota(jnp.int32, (1, CH_WIN), 1)
        clipped = (lane <= first) | (lane >= first + REL_CLIP + CHUNK)
        total = jnp.sum(jnp.where(clipped, dvec, 0.0), axis=1, keepdims=True)
        o_ref[0] = jnp.where(lane == first, total, dvec)

    return pl.pallas_call(
        body, name=name, grid=(n_heads,),
        in_specs=[pl.BlockSpec((1, CH_QB, CH_WIN), lambda h: (h, 0, 0)), ANY],
        out_specs=pl.BlockSpec((1, 1, CH_WIN), lambda h: (h, 0, 0)),
        out_shape=_sds((n_heads, 1, CH_WIN)),
        scratch_shapes=[pltpu.VMEM((8, CH_WIN), F32)],
        compiler_params=_params("arbitrary"),
    )(dbias, after)


def _out_fwd(o, h1, g_sb, g_ch, g_post, wout, *, name):
    t = o.shape[0]
    tm = 512
    half = D_MODEL // 2

    def body(o_ref, h_ref, gsb_ref, gch_ref, gpost_ref, w_ref, h2_ref, mixed_ref, y_ref):
        ov = o_ref[...]
        mixed = jnp.concatenate([_rms(ov[:, :half], gsb_ref[...]),
                                 _rms(ov[:, half:], gch_ref[...])], axis=1)
        mixed_ref[...] = _bf(mixed)
        y = _dot(mixed, w_ref[...])
        y_ref[...] = y
        h2_ref[...] = h_ref[...] + _rms(y, gpost_ref[...])

    row = pl.BlockSpec((tm, D_MODEL), lambda i: (i, 0))
    gain = lambda n: pl.BlockSpec((1, n), lambda i: (0, 0))
    return pl.pallas_call(
        body, name=name, grid=(t // tm,),
        in_specs=[row, row, gain(half), gain(half), gain(D_MODEL),
                  pl.BlockSpec((D_MODEL, D_MODEL), lambda i: (0, 0))],
        out_specs=[row, row, row],
        out_shape=[_sds((t, D_MODEL)), _sds((t, D_MODEL), BF16), _sds((t, D_MODEL))],
        compiler_params=_params("arbitrary"),
    )(o, h1, g_sb, g_ch, g_post, wout)


def _out_bwd(dy, mixed, o, g_sb, g_ch, wout, *, name):
    t = o.shape[0]
    tm = 512
    ni = t // tm
    half = D_MODEL // 2

    def body(dy_ref, mixed_ref, o_ref, gsb_ref, gch_ref, w_ref,
             dw_ref, do_ref, dgsb_ref, dgch_ref, acc_ref):
        i = pl.program_id(0)

        @pl.when(i == 0)
        def _():
            acc_ref[...] = jnp.zeros_like(acc_ref)
            dgsb_ref[...] = jnp.zeros_like(dgsb_ref)
            dgch_ref[...] = jnp.zeros_like(dgch_ref)

        dyv = dy_ref[...]
        acc_ref[...] += _dot_tn(mixed_ref[...], dyv)
        dm = _dot_nt(dyv, w_ref[...])
        ov = o_ref[...]
        doa, dga = _rms_bwd(dm[:, :half], ov[:, :half], gsb_ref[...])
        dob, dgb = _rms_bwd(dm[:, half:], ov[:, half:], gch_ref[...])
        do_ref[...] = jnp.concatenate([doa, dob], axis=1)
        dgsb_ref[...] += dga
        dgch_ref[...] += dgb

        @pl.when(i == ni - 1)
        def _():
            dw_ref[...] = _bf(acc_ref[...])

    row = pl.BlockSpec((tm, D_MODEL), lambda i: (i, 0))
    gain = pl.BlockSpec((1, half), lambda i: (0, 0))
    sq = pl.BlockSpec((D_MODEL, D_MODEL), lambda i: (0, 0))
    return pl.pallas_call(
        body, name=name, grid=(ni,),
        in_specs=[row, row, row, gain, gain, sq],
        out_specs=[sq, row, gain, gain],
        out_shape=[_sds((D_MODEL, D_MODEL), BF16), _sds((t, D_MODEL)),
                   _sds((1, half)), _sds((1, half))],
        scratch_shapes=[pltpu.VMEM((D_MODEL, D_MODEL), F32)],
        compiler_params=_params("arbitrary"),
    )(dy, mixed, o, g_sb, g_ch, wout)


def _ple(p, h3, target, wp, wgate, g, f_post, g_post, *, name):
    t = h3.shape[0]
    tm = 512
    ni = t // tm

    def body(p_ref, h_ref, tgt_ref, wp_ref, wg_ref, g_ref, f_ref, gf_ref,
             loss_ref, dres_ref, dwp_ref, dwg_ref, dg_ref, df_ref, dgf_ref, accp, accg):
        i = pl.program_id(0)

        @pl.when(i == 0)
        def _():
            loss_ref[...] = jnp.zeros_like(loss_ref)
            dg_ref[...] = jnp.zeros_like(dg_ref)
            dgf_ref[...] = jnp.zeros_like(dgf_ref)
            accp[...] = jnp.zeros_like(accp)
            accg[...] = jnp.zeros_like(accg)

        pv, hv, gv = p_ref[...], h_ref[...], g_ref[...]
        pe = _dot(pv, wp_ref[...])
        sig = _sigmoid(_dot(hv, wg_ref[...]))
        e = pe * sig
        err = hv + _rms(e, gv) - tgt_ref[...]
        tok = jnp.mean(err * err, axis=-1, keepdims=True)
        loss_ref[...] += 0.5 * jnp.sum(tok, axis=0, keepdims=True)
        dh4 = err * (1.0 / D_MODEL)
        de, dg = _rms_bwd(dh4, e, gv)
        dg_ref[...] += dg
        dpe = de * sig
        dgt = de * pe * sig * (1.0 - sig)
        accp[...] += _dot_tn(pv, dpe)
        accg[...] += _dot_tn(hv, dgt)
        dres = dh4 + _dot_nt(dgt, wg_ref[...])
        dres_ref[...] = dres
        df, dgf = _rms_bwd(0.5 * dres, f_ref[...], gf_ref[...])
        df_ref[...] = _bf(df)
        dgf_ref[...] += dgf

        @pl.when(i == ni - 1)
        def _():
            dwp_ref[...] = _bf(accp[...])
            dwg_ref[...] = _bf(accg[...])

    row = pl.BlockSpec((tm, D_MODEL), lambda i: (i, 0))
    const = lambda r, c: pl.BlockSpec((r, c), lambda i: (0, 0))
    return pl.pallas_call(
        body, name=name, grid=(ni,),
        in_specs=[pl.BlockSpec((tm, PLE_DIM), lambda i: (i, 0)), row, row,
                  const(PLE_DIM, D_MODEL), const(D_MODEL, D_MODEL), const(1, D_MODEL),
                  row, const(1, D_MODEL)],
        out_specs=[const(1, 128), row, const(PLE_DIM, D_MODEL), const(D_MODEL, D_MODEL),
                   const(1, D_MODEL), row, const(1, D_MODEL)],
        out_shape=[_sds((1, 128)), _sds((t, D_MODEL)), _sds((PLE_DIM, D_MODEL), BF16),
                   _sds((D_MODEL, D_MODEL), BF16), _sds((1, D_MODEL)),
                   _sds((t, D_MODEL), BF16), _sds((1, D_MODEL))],
        scratch_shapes=[pltpu.VMEM((PLE_DIM, D_MODEL), F32), pltpu.VMEM((D_MODEL, D_MODEL), F32)],
        compiler_params=_params("arbitrary"),
    )(p, h3, target, wp, wgate, g, f_post, g_post)


def _rel_bias_to_fvec(rel_bias):
    rev = rel_bias[:, ::-1]
    n_heads = rel_bias.shape[0]
    first = CH_LOOK - REL_CLIP
    n_var = REL_CLIP + CHUNK
    clipped = rev[:, :1]
    fvec = jnp.concatenate([jnp.broadcast_to(clipped, (n_heads, first)), rev[:, :n_var],
                            jnp.broadcast_to(clipped, (n_heads, CH_WIN - first - n_var))], axis=1)
    return fvec.reshape(n_heads, 1, CH_WIN)


def _fvec_grad_to_rel_bias(dfvec):
    first = CH_LOOK - REL_CLIP
    n_var = REL_CLIP + CHUNK
    rev = jnp.pad(dfvec[:, 0, first:first + n_var], ((0, 0), (0, N_REL - n_var)))
    return rev[:, ::-1]


def _local_step(x, p, target, g, weights_for, grads_done, fvec, weights_early=None):
    bias = _bias_expand(fvec, name="bias_expand")
    w, tie = weights_for(0, bias)
    w = dict(w)
    h1, n1, a1, b1, f1 = _ffn_fwd(x, g["ffn1_pre"] + tie, g["ffn1_post"],
                                  w["ffn1_gate"], w["ffn1_up"], w["ffn1_down"], name="ffn1_fwd")
    more, tie = weights_for(1, h1)
    w.update(more)
    qkv, u = _qkv_fwd(h1, g["mix_pre"] + tie, w["in"], name="qkv_fwd")
    o, ltot = _sb_fwd(qkv, name="sb_fwd")
    tie = weights_early(2, ltot) if weights_early else 0.0
    o = _ch_fwd(qkv, bias, o, name="ch_fwd")
    w.update(weights_for(2, o)[0])
    h2, mixed, y = _out_fwd(o, h1, g["out_sb"] + tie, g["out_ch"], g["mix_post"], w["out"],
                            name="out_fwd")
    h3, n2, a2, b2, f2 = _ffn_fwd(h2, g["ffn2_pre"], g["ffn2_post"],
                                  w["ffn2_gate"], w["ffn2_up"], w["ffn2_down"], name="ffn2_fwd")
    loss, dh3, dwp, dwgate, dg_ple, df2, dg_ffn2_post = _ple(
        p, h3, target, w["ple_proj"], w["ple_gate"], g["ple_post"], f2, g["ffn2_post"], name="ple")
    tie = grads_done(0, {"ple_proj": dwp, "ple_gate": dwgate})
    dwg2, dwu2, dwd2, dn2 = _ffn_bwd(n2, df2, a2, b2, w["ffn2_gate"], w["ffn2_up"],
                                     w["ffn2_down"], name="ffn2_bwd")
    tie = tie + grads_done(1, {"ffn2_gate": dwg2, "ffn2_up": dwu2, "ffn2_down": dwd2})
    dh2, dg_ffn2_pre, dy, dg_mix_post = _junction(
        dh3, pre=(dn2, h2, g["ffn2_pre"] + tie), post=(y, g["mix_post"], 1.0), name="junction2")
    dwout, do, dg_sb, dg_ch = _out_bwd(dy, mixed, o, g["out_sb"], g["out_ch"], w["out"],
                                       name="out_bwd")
    dq, dk, dv = _sb_bwd(qkv, ltot, do, name="sb_bwd")
    dq, dk, dv, dbias = _ch_bwd(qkv, bias, o, do, dq, dk, dv, name="ch_bwd")
    dwin, du = _qkv_bwd(dq, dk, dv, u, w["in"], name="qkv_bwd")
    tie = grads_done(2, {"out": dwout, "in": dwin})
    dh1, dg_mix_pre, df1, dg_ffn1_post = _junction(
        dh2, pre=(du, h1, g["mix_pre"] + tie), post=(f1, g["ffn1_post"], 0.5), name="junction1")
    dwg1, dwu1, dwd1, dn1 = _ffn_bwd(n1, df1, a1, b1, w["ffn1_gate"], w["ffn1_up"],
                                     w["ffn1_down"], name="ffn1_bwd")
    tie = grads_done(3, {"ffn1_gate": dwg1, "ffn1_up": dwu1, "ffn1_down": dwd1})
    dx, dg_ffn1_pre = _junction(dh1, pre=(dn1, x, g["ffn1_pre"] + tie), name="junction0")

    dg = {"ffn1_pre": dg_ffn1_pre, "ffn1_post": dg_ffn1_post, "mix_pre": dg_mix_pre,
          "mix_post": dg_mix_post, "out_sb": dg_sb, "out_ch": dg_ch,
          "ffn2_pre": dg_ffn2_pre, "ffn2_post": dg_ffn2_post, "ple_post": dg_ple}
    return loss, dx, dg, dbias


_WEIGHTS = (
    ("ffn1_gate", "row", FF_SHARD, FF_SHARD_PAD, D_MODEL),
    ("ffn1_up", "row", FF_SHARD, FF_SHARD_PAD, D_MODEL),
    ("ffn1_down", "row", FF_SHARD, FF_SHARD_PAD, D_MODEL),
    ("in", "col", QKV_SHARD, QKV_SHARD, D_MODEL),
    ("out", "row", ROW_SHARD, ROW_SHARD, D_MODEL),
    ("ffn2_gate", "row", FF_SHARD, FF_SHARD_PAD, D_MODEL),
    ("ffn2_up", "row", FF_SHARD, FF_SHARD_PAD, D_MODEL),
    ("ffn2_down", "row", FF_SHARD, FF_SHARD_PAD, D_MODEL),
    ("ple_proj", "col", ROW_SHARD, ROW_SHARD, PLE_DIM),
    ("ple_gate", "row", ROW_SHARD, ROW_SHARD, D_MODEL),
)
_TRANSPOSED = ("ffn1_gate", "ffn1_up", "ffn2_gate", "ffn2_up")
_SPEC = {n: (kind, valid, pad, other) for n, kind, valid, pad, other in _WEIGHTS}
_GATHER_STAGES = (("ffn1_gate", "ffn1_up", "ffn1_down"), ("in",),
                  ("out", "ffn2_gate", "ffn2_up", "ffn2_down", "ple_proj", "ple_gate"))
_SCATTER_STAGES = (("ple_proj", "ple_gate"), ("ffn2_gate", "ffn2_up", "ffn2_down"),
                   ("out", "in"), ("ffn1_gate", "ffn1_up", "ffn1_down"))
HBM = pl.BlockSpec(memory_space=pltpu.HBM)
SEM = pl.BlockSpec(memory_space=pltpu.SEMAPHORE)
EFFECT = pltpu.SideEffectType.DATAFLOW_SIDE_EFFECTING


def _shard_shape(kind, size, other):
    return (other, size) if kind == "col" else (size, other)


def _window(ref, kind, start, size):
    return ref.at[:, pl.ds(start, size)] if kind == "col" else ref.at[pl.ds(start, size), :]


def _device_tuple(k):
    return (k // 4, (k // 2) % 2, k % 2)


def _my_index():
    return 4 * lax.axis_index("x") + 2 * lax.axis_index("y") + lax.axis_index("c")


def _pack_weights(names, shards, after, *, name):
    nw = len(names)
    specs = [_SPEC[n] for n in names]

    def body(*refs):
        ins, packed, full = refs[:nw], refs[nw + 1:2 * nw + 1], refs[2 * nw + 1:3 * nw + 1]
        sem = refs[3 * nw + 1]
        me = _my_index()
        for (kind, valid, pad, _), src, dst in zip(specs, ins, packed):
            if pad != valid:
                dst[...] = jnp.zeros_like(dst)
            if kind == "col":
                dst[:, pl.ds(0, valid)] = _bf(src[...])
            else:
                dst[pl.ds(0, valid), :] = _bf(src[...])
        for k in range(N_DEV):
            @pl.when(me == k)
            def _():
                for w, (kind, _, pad, _) in enumerate(specs):
                    pltpu.make_async_copy(packed[w], _window(full[w], kind, k * pad, pad),
                                          sem.at[w]).start()
        for w, (kind, _, pad, _) in enumerate(specs):
            pltpu.make_async_copy(packed[w], _window(full[w], kind, 0, pad), sem.at[w]).wait()

    whole = lambda shape: pl.BlockSpec(shape, lambda i: (0, 0))
    packed_shapes = [_shard_shape(kind, pad, other) for kind, _, pad, other in specs]
    outs = pl.pallas_call(
        body, name=name, grid=(1,),
        in_specs=[whole(a.shape) for a in shards] + [ANY],
        out_specs=[whole(s) for s in packed_shapes] + [ANY] * nw,
        out_shape=[_sds(s, BF16) for s in packed_shapes]
        + [_sds(_shard_shape(kind, N_DEV * pad, other), BF16) for kind, _, pad, other in specs],
        scratch_shapes=[pltpu.SemaphoreType.DMA((nw,))],
        compiler_params=_params("arbitrary"),
    )(*shards, after)
    return dict(zip(names, outs[:nw])), dict(zip(names, outs[nw:]))


def _hbm(a):
    return pltpu.with_memory_space_constraint(a, pltpu.HBM)


def _split_start(name, n, body_copies, sources, lands, after):
    arrays = list(sources) + list(lands)
    ns, na = len(sources), len(arrays)

    def body(*refs):
        src, land = refs[:ns], refs[ns:na]
        send, recv = refs[na + 1], refs[na + 2]
        token = refs[-1]
        body_copies(src, land, send, recv)
        token[...] = jnp.zeros_like(token)

    out = pl.pallas_call(
        body, name=name,
        out_shape=(pltpu.SemaphoreType.DMA((n,)), pltpu.SemaphoreType.DMA((n,)),
                   *[pltpu.HBM(a.shape, a.dtype) for a in arrays], _sds((8, 128))),
        in_specs=[HBM] * na + [ANY], out_specs=(SEM, SEM, *[HBM] * na, VMEM),
        input_output_aliases={i: 2 + i for i in range(na)},
        compiler_params=pltpu.CompilerParams(has_side_effects=EFFECT),
    )(*[_hbm(a) for a in arrays], after)
    return out[0], out[1], out[2:2 + ns], out[2 + ns:2 + na], out[-1]


def _split_wait(name, n, seven_of, send, recv, sources, lands, after, keep_sources=False):
    arrays = list(sources) + list(lands)
    ns, na = len(sources), len(arrays)

    def body(*refs):
        land = refs[ns:na]
        send_ref, recv_ref = refs[na], refs[na + 1]
        myself = (lax.axis_index("x"), lax.axis_index("y"), lax.axis_index("c"))
        for w in range(n):
            seven = seven_of(w, land[w])
            copy = pltpu.make_async_remote_copy(
                src_ref=seven, dst_ref=seven, send_sem=send_ref.at[w], recv_sem=recv_ref.at[w],
                device_id=myself, device_id_type=MESH)
            copy.wait_send()
            copy.wait_recv()

    afters = tuple(after) if isinstance(after, (tuple, list)) else (after,)
    out = pl.pallas_call(
        body, name=name,
        out_shape=[pltpu.HBM(a.shape, a.dtype) for a in arrays],
        in_specs=[HBM] * na + [SEM, SEM] + [ANY] * len(afters), out_specs=[HBM] * na,
        input_output_aliases={i: i for i in range(na)},
        compiler_params=pltpu.CompilerParams(has_side_effects=EFFECT),
    )(*arrays, send, recv, *afters)
    return out if keep_sources else out[ns:]


_ALL_PEERS = (1, 2, 3, 4, 5, 6, 7)
_NEAR_PEERS = (1, 2, 4, 6)
_FAR_CHIPS = (2, 4, 6)


def _gather_start(stage, names, packed, full, after, peers=_ALL_PEERS):
    def copies(src, land, send, recv):
        me = _my_index()
        for k in range(N_DEV):
            @pl.when(me == k)
            def _():
                for w, name in enumerate(names):
                    kind, _, pad, _ = _SPEC[name]
                    dst = _window(land[w], kind, k * pad, pad)
                    for mask in peers:
                        pltpu.make_async_remote_copy(
                            src_ref=src[w], dst_ref=dst, send_sem=send.at[w],
                            recv_sem=recv.at[w], device_id=_device_tuple(k ^ mask),
                            device_id_type=MESH).start()

    return _split_start(f"gather_start{stage}", len(names), copies,
                        [packed[n] for n in names], [full[n] for n in names], after)


def _gather_wait(stage, names, started, after, count=N_DEV - 1):
    send, recv, src, land, _ = started

    def bytes_of(w, ref):
        kind, _, pad, _ = _SPEC[names[w]]
        return _window(ref, kind, 0, count * pad)

    return dict(zip(names, _split_wait(f"gather_wait{stage}", len(names), bytes_of,
                                       send, recv, src, land, after)))


def _relay_start(stage, names, full, after):
    def copies(_, land, send, recv):
        me = _my_index()
        for k in range(N_DEV):
            @pl.when(me == k)
            def _():
                for w, name in enumerate(names):
                    kind, _, pad, _ = _SPEC[name]
                    for mask in _FAR_CHIPS:
                        win = _window(land[w], kind, (k ^ mask) * pad, pad)
                        pltpu.make_async_remote_copy(
                            src_ref=win, dst_ref=win, send_sem=send.at[w], recv_sem=recv.at[w],
                            device_id=_device_tuple(k ^ 1), device_id_type=MESH).start()

    return _split_start(f"relay_start{stage}", len(names), copies, [],
                        [full[n] for n in names], after)


def _scatter_start(stage, names, grads, after):
    def copies(src, land, send, recv):
        me = _my_index()
        for k in range(N_DEV):
            @pl.when(me != k)
            def _():
                slot = lax.rem(me + (N_DEV - 1 - k), N_DEV)
                for w, name in enumerate(names):
                    kind, _, pad, _ = _SPEC[name]
                    pltpu.make_async_remote_copy(
                        src_ref=_window(src[w], kind, k * pad, pad), dst_ref=land[w].at[slot],
                        send_sem=send.at[w], recv_sem=recv.at[w],
                        device_id=_device_tuple(k), device_id_type=MESH).start()

    lands = [lax.empty((N_DEV - 1,) + _shard_shape(_SPEC[m][0], _SPEC[m][2], _SPEC[m][3]), BF16)
             for m in names]
    return _split_start(f"scatter_start{stage}", len(names), copies, grads, lands, after)


def _scatter_wait(stage, names, started, after):
    send, recv, src, land, _ = started
    n = len(names)
    out = _split_wait(f"scatter_wait{stage}", n, lambda w, ref: ref, send, recv, src, land, after,
                      keep_sources=True)
    return dict(zip(names, out[:n])), dict(zip(names, out[n:]))


N_CHIPS = N_DEV // 2


def _pair_start(stage, names, grads, after):
    def copies(src, land, send, recv):
        me = _my_index()
        for k in range(N_DEV):
            @pl.when(me == k)
            def _():
                for w, name in enumerate(names):
                    kind, _, pad, _ = _SPEC[name]
                    for chip in range(N_CHIPS):
                        j = 2 * chip + ((k ^ 1) & 1)
                        pltpu.make_async_remote_copy(
                            src_ref=_window(src[w], kind, j * pad, pad), dst_ref=land[w].at[chip],
                            send_sem=send.at[w], recv_sem=recv.at[w],
                            device_id=_device_tuple(k ^ 1), device_id_type=MESH).start()

    lands = [lax.empty((N_CHIPS,) + _shard_shape(_SPEC[m][0], _SPEC[m][2], _SPEC[m][3]), BF16)
             for m in names]
    return _split_start(f"pair_start{stage}", len(names), copies, grads, lands, after)


def _pair_sum(dw_full, pair, *, pad, name):
    other = dw_full.shape[1]

    def body(own_ref, pair_ref, out_ref):
        out_ref[0] = _bf(own_ref[...].astype(F32) + pair_ref[0].astype(F32))

    slot = pl.BlockSpec((1, pad, other), lambda q: (q, 0, 0))
    return pl.pallas_call(
        body, name=name, grid=(N_CHIPS,),
        in_specs=[pl.BlockSpec((pad, other), lambda q: (2 * q + lax.axis_index("c"), 0)), slot],
        out_specs=slot, out_shape=_sds((N_CHIPS, pad, other), BF16),
        compiler_params=_params("arbitrary"),
    )(dw_full, pair)


def _chip_start(stage, names, sums, after):
    def copies(src, land, send, recv):
        me = _my_index()
        my_chip = lax.shift_right_logical(me, 1)
        for k in range(N_DEV):
            @pl.when((me != k) & (((me ^ k) & 1) == 0))
            def _():
                slot = lax.rem(my_chip + (N_CHIPS - 1 - k // 2), N_CHIPS)
                for w in range(len(names)):
                    pltpu.make_async_remote_copy(
                        src_ref=src[w].at[k // 2], dst_ref=land[w].at[slot],
                        send_sem=send.at[w], recv_sem=recv.at[w],
                        device_id=_device_tuple(k), device_id_type=MESH).start()

    lands = [lax.empty((N_CHIPS - 1,) + a.shape[1:], BF16) for a in sums]
    return _split_start(f"chip_start{stage}", len(names), copies, sums, lands, after)


def _adamw_chip(w, m, v, land, sums, *, name):
    shape = w.shape

    def body(w_ref, m_ref, v_ref, land_ref, own_ref, *outs):
        rows = pl.ds(0, shape[0])
        grad = own_ref[0, rows, :].astype(F32)
        for s in range(N_CHIPS - 1):
            grad = grad + land_ref[s, rows, :].astype(F32)
        _adam_update(w_ref, m_ref, v_ref, grad, *outs)

    whole = lambda a: pl.BlockSpec(a.shape, lambda i: (0,) * a.ndim)
    own = pl.BlockSpec((1,) + sums.shape[1:],
                       lambda i: (2 * lax.axis_index("x") + lax.axis_index("y"), 0, 0))
    return pl.pallas_call(
        body, name=name, grid=(1,),
        in_specs=[whole(w), whole(m), whole(v), whole(land), own],
        out_specs=[whole(w)] * 4, out_shape=[_sds(shape)] * 4,
        compiler_params=_params("arbitrary"),
    )(w, m, v, land, sums)


def _allreduce_small(small, after):
    shape = small.shape

    def body(in_ref, _after, out_ref, gath, send, recv):
        me = _my_index()
        for k in range(N_DEV):
            @pl.when(me != k)
            def _():
                pltpu.make_async_remote_copy(
                    src_ref=in_ref, dst_ref=gath.at[me], send_sem=send, recv_sem=recv,
                    device_id=_device_tuple(k), device_id_type=MESH).start()

            @pl.when(me == k)
            def _():
                gath[k] = in_ref[...]
        seven = gath.at[pl.ds(0, N_DEV - 1)]
        pltpu.make_async_remote_copy(
            src_ref=seven, dst_ref=seven, send_sem=send, recv_sem=recv,
            device_id=_device_tuple(0), device_id_type=MESH).wait()
        total = gath[0]
        for s in range(1, N_DEV):
            total = total + gath[s]
        out_ref[...] = total

    return pl.pallas_call(
        body, name="allreduce_small",
        in_specs=[VMEM, ANY], out_specs=VMEM, out_shape=_sds(shape),
        scratch_shapes=[pltpu.VMEM((N_DEV,) + shape, F32),
                        pltpu.SemaphoreType.DMA, pltpu.SemaphoreType.DMA],
    )(small, after)


def _adam_update(w_ref, m_ref, v_ref, grad, grad_ref, delta_ref, nm_ref, nv_ref):
    new_m = ADAM_B1 * m_ref[...] + (1.0 - ADAM_B1) * grad
    new_v = ADAM_B2 * v_ref[...] + (1.0 - ADAM_B2) * (grad * grad)
    m_hat = new_m / (1.0 - ADAM_B1 ** ADAM_STEP)
    v_hat = new_v / (1.0 - ADAM_B2 ** ADAM_STEP)
    grad_ref[...] = grad
    delta_ref[...] = -ADAM_LR * (m_hat / (jnp.sqrt(v_hat) + ADAM_EPS) + ADAM_WD * w_ref[...])
    nm_ref[...] = new_m
    nv_ref[...] = new_v


def _adamw(w, m, v, g, *, name):
    def body(w_ref, m_ref, v_ref, g_ref, *outs):
        _adam_update(w_ref, m_ref, v_ref, g_ref[...], *outs)

    whole = pl.BlockSpec(w.shape, lambda i: (0,) * w.ndim)
    return pl.pallas_call(
        body, name=name, grid=(1,), in_specs=[whole] * 4, out_specs=[whole] * 4,
        out_shape=[_sds(w.shape)] * 4, compiler_params=_params("arbitrary"),
    )(w, m, v, g)


def _adamw_gains(small, params):
    n = len(params)

    def body(small_ref, *refs):
        ins, outs = refs[:3 * n], refs[3 * n:]
        for r in range(n):
            width = ins[3 * r].shape[1]
            if width == D_MODEL:
                grad = small_ref[pl.ds(r, 1), :]
            else:
                grad = small_ref[pl.ds(len(_GAINS), 1), pl.ds((r - len(_GAINS)) * width, width)]
            _adam_update(*ins[3 * r:3 * r + 3], grad, *outs[4 * r:4 * r + 4])

    whole = lambda a: pl.BlockSpec(a.shape, lambda i: (0, 0))
    flat = [a for group in params for a in group]
    return pl.pallas_call(
        body, name="adamw_gains", grid=(1,),
        in_specs=[whole(small)] + [whole(a) for a in flat],
        out_specs=[whole(w) for w, _, _ in params for _ in range(4)],
        out_shape=[_sds(w.shape) for w, _, _ in params for _ in range(4)],
        compiler_params=_params("arbitrary"),
    )(small, *flat)


def _adamw_shard(w, m, v, land, dw_full, *, kind, pad, name):
    shape = w.shape
    other = shape[0] if kind == "col" else shape[1]

    def body(w_ref, m_ref, v_ref, land_ref, own_ref, *outs):
        valid = ((slice(None), pl.ds(0, shape[1])) if kind == "col"
                 else (pl.ds(0, shape[0]), slice(None)))
        grad = own_ref[valid].astype(F32)
        for s in range(N_DEV - 1):
            grad = grad + land_ref[(s,) + valid].astype(F32)
        _adam_update(w_ref, m_ref, v_ref, grad, *outs)

    whole = lambda a: pl.BlockSpec(a.shape, lambda i: (0,) * a.ndim)
    own = pl.BlockSpec(_shard_shape(kind, pad, other),
                       (lambda i: (0, _my_index())) if kind == "col" else (lambda i: (_my_index(), 0)))
    return pl.pallas_call(
        body, name=name, grid=(1,),
        in_specs=[whole(w), whole(m), whole(v), whole(land), own],
        out_specs=[whole(w)] * 4, out_shape=[_sds(shape)] * 4,
        compiler_params=_params("arbitrary"),
    )(w, m, v, land, dw_full)


_GAINS = ("ffn1_pre", "ffn1_post", "mix_pre", "mix_post", "ffn2_pre", "ffn2_post", "ple_post")
_SMALL_ROWS = 16


def _stack_gains(get):
    return jnp.concatenate([get(n) for n in _GAINS]
                           + [jnp.concatenate([get("out_sb"), get("out_ch")], axis=1)], axis=0)


def kernel(x, p, g_ffn1_pre, g_ffn1_post, w_ffn1_gate, w_ffn1_up, w_ffn1_down, g_mix_pre, g_mix_post, w_in, g_out_sb, g_out_ch, rel_bias, w_out, g_ffn2_pre, g_ffn2_post, w_ffn2_gate, w_ffn2_up, w_ffn2_down, w_ple_proj, w_ple_gate, g_ple_post, loss_target, m_g_ffn1_pre, m_g_ffn1_post, m_w_ffn1_gate, m_w_ffn1_up, m_w_ffn1_down, m_g_mix_pre, m_g_mix_post, m_w_in, m_g_out_sb, m_g_out_ch, m_rel_bias, m_w_out, m_g_ffn2_pre, m_g_ffn2_post, m_w_ffn2_gate, m_w_ffn2_up, m_w_ffn2_down, m_w_ple_proj, m_w_ple_gate, m_g_ple_post, v_g_ffn1_pre, v_g_ffn1_post, v_w_ffn1_gate, v_w_ffn1_up, v_w_ffn1_down, v_g_mix_pre, v_g_mix_post, v_w_in, v_g_out_sb, v_g_out_ch, v_rel_bias, v_w_out, v_g_ffn2_pre, v_g_ffn2_post, v_w_ffn2_gate, v_w_ffn2_up, v_w_ffn2_down, v_w_ple_proj, v_w_ple_gate, v_g_ple_post):
    given = dict(locals())
    wnames = [n for n, *_ in _WEIGHTS]

    def shard(prefix, n):
        a = given[prefix + "w_" + n][0]
        return a.T if n in _TRANSPOSED else a

    anchor = x[0]
    first = _GATHER_STAGES[0]
    packed, full = _pack_weights(first, [shard("", n) for n in first], anchor, name="pack_first")
    two_level = (0, 2)
    gathers = {}

    def start_stage(stage, after):
        peers = _NEAR_PEERS if stage in two_level else _ALL_PEERS
        gathers[stage] = _gather_start(stage, _GATHER_STAGES[stage], packed, full, after,
                                       peers=peers)

    start_stage(0, anchor)
    rest = [n for n in wnames if n not in first]
    packed_rest, full_rest = _pack_weights(rest, [shard("", n) for n in rest], gathers[0][-1],
                                           name="pack_rest")
    packed.update(packed_rest)
    full.update(full_rest)

    relays = {}

    def first_level(stage, after):
        names = _GATHER_STAGES[stage]
        last_stage = stage + 1 == len(_GATHER_STAGES)
        count = len(_NEAR_PEERS) if stage in two_level else N_DEV - 1
        if stage == 0:
            after = (after, packed_rest[rest[0]])
        ws = _gather_wait(stage, names, gathers[stage], after, count=count)
        if not last_stage:
            start_stage(stage + 1, ws[names[0]])
        if stage in two_level:
            relays[stage] = _relay_start(stage, names, ws,
                                         anchor if last_stage else gathers[stage + 1][-1])
            return ws, relays[stage][-1]
        return ws, None if last_stage else gathers[stage + 1][-1]

    def weights_early(stage, after):
        return first_level(stage, after)[1][:1, :1]

    def weights_for(stage, after):
        names = _GATHER_STAGES[stage]
        ws, token = (None, None) if stage in relays else first_level(stage, after)
        if stage in relays:
            relay = relays[stage]
            ws = _gather_wait(f"{stage}r", names, relay, after, count=len(_FAR_CHIPS))
            token = None if stage + 1 == len(_GATHER_STAGES) else gathers[stage + 1][-1]
        return ws, jnp.zeros((1, 1), F32) if token is None else token[:1, :1]

    scatters = {}

    last = len(_SCATTER_STAGES) - 1

    def grads_done(stage, grads):
        names = _SCATTER_STAGES[stage]
        start = _pair_start if stage == last else _scatter_start
        scatters[stage] = start(stage, names, [grads[n] for n in names], anchor)
        return scatters[stage][-1][:1, :1]

    gains = {n: given["g_" + n] for n in _GAINS + ("out_sb", "out_ch")}
    fvec = _rel_bias_to_fvec(rel_bias[0])
    loss, dx, dg, dbias = _local_step(x[0], p[0, 0], loss_target[0], gains,
                                      weights_for, grads_done, fvec, weights_early)

    results = {}

    def finish(stage, after):
        names = _SCATTER_STAGES[stage]
        dws, lands = _scatter_wait(stage, names, scatters[stage], after)
        for n in names:
            kind, _, pad, _ = _SPEC[n]
            out = _adamw_shard(shard("", n), shard("m_", n), shard("v_", n), lands[n], dws[n],
                               kind=kind, pad=pad, name="adamw_" + n)
            results["w_" + n] = [a.T for a in out] if n in _TRANSPOSED else out
        return results["w_" + names[-1]][0]

    names = _SCATTER_STAGES[last]
    whole = lambda w, ref: ref
    send, recv, src, land, _ = scatters[last]
    out = _split_wait(f"pair_wait{last}", len(names), whole, send, recv, src, land, dx,
                      keep_sources=True)
    sums = [_pair_sum(dwf, pair, pad=_SPEC[n][2], name="pair_sum_" + n)
            for n, dwf, pair in zip(names, out[:len(names)], out[len(names):])]
    send, recv, src, land, after = _chip_start(last, names, sums, anchor)
    for stage in range(last):
        after = finish(stage, after)
    dfvec = _bias_grad(dbias, after, name="bias_grad")
    loss_col = jnp.pad(loss[:, :1], ((0, N_DEV - 1), (0, D_MODEL - CH_WIN - 1)))
    dfv = jnp.concatenate([dfvec[:, 0, :], loss_col], axis=1)
    small = _allreduce_small(jnp.concatenate([_stack_gains(lambda n: dg[n]), dfv], axis=0), after)
    gain_names = _GAINS + ("out_sb", "out_ch")
    gain_out = _adamw_gains(small, [(given["g_" + n], given["m_g_" + n], given["v_g_" + n])
                                    for n in gain_names])
    for r, n in enumerate(gain_names):
        results["g_" + n] = gain_out[4 * r:4 * r + 4]
    d_rel = _fvec_grad_to_rel_bias(small[N_DEV:, :CH_WIN].reshape(N_DEV, 1, CH_WIN))
    results["rel_bias"] = _adamw(rel_bias[0], m_rel_bias[0], v_rel_bias[0], d_rel,
                                 name="adamw_rel_bias")
    out = _split_wait(f"chip_wait{last}", len(names), whole, send, recv, src, land,
                      results["rel_bias"][0], keep_sources=True)
    for n, own, landed in zip(names, out[:len(names)], out[len(names):]):
        res = _adamw_chip(shard("", n), shard("m_", n), shard("v_", n), landed, own,
                          name="adamw_" + n)
        results["w_" + n] = [a.T for a in res] if n in _TRANSPOSED else res

    order = ("g_ffn1_pre", "g_ffn1_post", "w_ffn1_gate", "w_ffn1_up", "w_ffn1_down",
             "g_mix_pre", "g_mix_post", "w_in", "g_out_sb", "g_out_ch", "rel_bias", "w_out",
             "g_ffn2_pre", "g_ffn2_post", "w_ffn2_gate", "w_ffn2_up", "w_ffn2_down",
             "w_ple_proj", "w_ple_gate", "g_ple_post")

    def leaf(name, idx):
        a = results[name][idx]
        return a if name.startswith("g_") else a[None]

    total_loss = small[N_DEV, CH_WIN]
    return (total_loss, dx[None],
            *[leaf(n, 0) for n in order], *[leaf(n, 1) for n in order],
            *[leaf(n, 2) for n in order], *[leaf(n, 3) for n in order])
```

```python
import jax
import jax.numpy as jnp
from jax import lax
from jax.experimental import pallas as pl
from jax.experimental.pallas import tpu as pltpu

F32 = jnp.float32
BF16 = jnp.bfloat16

N_DEV = 8
D_MODEL = 1024
D_FF = 2816
FF_SHARD = D_FF // N_DEV
FF_SHARD_PAD = 384
D_FF_PAD = FF_SHARD_PAD * N_DEV
QKV_WIDTH = 3 * D_MODEL
QKV_SHARD = QKV_WIDTH // N_DEV
PLE_DIM = 256
ROW_SHARD = D_MODEL // N_DEV
HEAD_DIM = 64
PAIR = 2 * HEAD_DIM
N_PAIRS = 4
CHUNK = 64
LOOKBACK = 8
REL_CLIP = 128
N_REL = 2 * REL_CLIP + 1
CH_QB = 256
CH_LOOK = LOOKBACK * CHUNK
CH_WIN = CH_LOOK + CH_QB
SB_QB = 512
SB_KB = 256
SB_GROUP = 2
SB_LANES = tuple(slice(g * 128, (g + 1) * 128) for g in range(SB_GROUP))
EPS = 1e-6
NEG_INF = -1e30
ATT_SCALE = HEAD_DIM ** -0.5
ADAM_LR = 0.001
ADAM_B1 = 0.9
ADAM_B2 = 0.999
ADAM_EPS = 1e-08
ADAM_WD = 0.01
ADAM_STEP = 10
VMEM_LIMIT_BYTES = 48 * 1024 * 1024
MESH = pl.DeviceIdType.MESH

ANY = pl.BlockSpec(memory_space=pl.ANY)
VMEM = pl.BlockSpec(memory_space=pltpu.VMEM)


def _params(*sem):
    return pltpu.CompilerParams(dimension_semantics=sem or None,
                                vmem_limit_bytes=VMEM_LIMIT_BYTES)


def _sds(shape, dtype=F32):
    return jax.ShapeDtypeStruct(shape, dtype)


def _bf(x):
    return x.astype(BF16)


def _dot(a, b):
    return jnp.dot(_bf(a), _bf(b), preferred_element_type=F32)


def _dot_nt(a, b):
    return lax.dot_general(_bf(a), _bf(b), (((1,), (1,)), ((), ())),
                           preferred_element_type=F32)


def _dot_tn(a, b):
    return lax.dot_general(_bf(a), _bf(b), (((0,), (0,)), ((), ())),
                           preferred_element_type=F32)


def _sigmoid(x):
    return 1.0 / (1.0 + jnp.exp(-x))


def _softplus(x):
    return jnp.maximum(x, 0.0) + jnp.log(1.0 + jnp.exp(-jnp.abs(x)))


def _rstd(x):
    return lax.rsqrt(jnp.mean(x * x, axis=-1, keepdims=True) + EPS)


def _rms(x, g):
    return x * _rstd(x) * g


def _rms_bwd(dy, x, g):
    r = _rstd(x)
    w = dy * g
    dx = r * (w - x * (r * r) * jnp.mean(w * x, axis=-1, keepdims=True))
    dg = jnp.sum(dy * (x * r), axis=0, keepdims=True)
    return dx, dg


def _head_masks():
    lane = lax.broadcasted_iota(jnp.int32, (1, PAIR), 1)
    return lane < HEAD_DIM, lane >= HEAD_DIM


def _ffn_fwd(x, g_pre, g_post, wg, wu, wd, *, name):
    t = x.shape[0]
    tm, tj = 512, 1024
    ni, nj = t // tm, D_FF_PAD // tj

    def body(x_ref, gpre_ref, gpost_ref, wg_ref, wu_ref, wd_ref,
             h_ref, n_ref, a_ref, b_ref, f_ref, acc_ref):
        j = pl.program_id(1)

        @pl.when(j == 0)
        def _():
            n_ref[...] = _bf(_rms(x_ref[...], gpre_ref[...]))
            acc_ref[...] = jnp.zeros_like(acc_ref)

        n = n_ref[...]
        a = _dot_nt(n, wg_ref[...])
        b = _dot_nt(n, wu_ref[...])
        a_ref[...] = a
        b_ref[...] = b
        hmid = a * _sigmoid(a) * b
        acc_ref[...] += jnp.dot(_bf(hmid), wd_ref[...], preferred_element_type=F32)

        @pl.when(j == nj - 1)
        def _():
            f = acc_ref[...]
            f_ref[...] = f
            h_ref[...] = x_ref[...] + 0.5 * _rms(f, gpost_ref[...])

    row = pl.BlockSpec((tm, D_MODEL), lambda i, j: (i, 0))
    gain = pl.BlockSpec((1, D_MODEL), lambda i, j: (0, 0))
    col = pl.BlockSpec((tm, tj), lambda i, j: (i, j))
    wtile = pl.BlockSpec((tj, D_MODEL), lambda i, j: (j, 0))
    return pl.pallas_call(
        body, name=name, grid=(ni, nj),
        in_specs=[row, gain, gain, wtile, wtile, wtile],
        out_specs=[row, row, col, col, row],
        out_shape=[_sds((t, D_MODEL)), _sds((t, D_MODEL), BF16),
                   _sds((t, D_FF_PAD)), _sds((t, D_FF_PAD)), _sds((t, D_MODEL))],
        scratch_shapes=[pltpu.VMEM((tm, D_MODEL), F32)],
        compiler_params=_params("arbitrary", "arbitrary"),
    )(x, g_pre, g_post, wg, wu, wd)


def _ffn_bwd(n, df, a, b, wg, wu, wd, *, name):
    t = n.shape[0]
    tj, tm, ts = 256, t, 512
    nj, ni, ns = D_FF_PAD // tj, t // tm, tm // ts

    def body(n_hbm, df_hbm, a_ref, b_ref, wg_ref, wu_ref, wd_ref,
             dwg_ref, dwu_ref, dwd_ref, dn_hbm,
             n_v, df_v, dn_v, ag, au, ad, sem):
        j, i = pl.program_id(0), pl.program_id(1)

        @pl.when((j == 0) & (i == 0))
        def _():
            c1 = pltpu.make_async_copy(n_hbm, n_v, sem.at[0])
            c2 = pltpu.make_async_copy(df_hbm, df_v, sem.at[1])
            c1.start()
            c2.start()
            dn_v[...] = jnp.zeros_like(dn_v)
            c1.wait()
            c2.wait()

        @pl.when(i == 0)
        def _():
            ag[...] = jnp.zeros_like(ag)
            au[...] = jnp.zeros_like(au)
            ad[...] = jnp.zeros_like(ad)

        wgj, wuj, wdj = wg_ref[...], wu_ref[...], wd_ref[...]
        for s in range(ns):
            local = pl.ds(s * ts, ts)
            rows = pl.ds(pl.multiple_of(i * tm + s * ts, ts), ts)
            av, bv = a_ref[local, :], b_ref[local, :]
            sig = _sigmoid(av)
            silu = av * sig
            dfr = df_v[rows, :]
            nr = n_v[rows, :]
            dhmid = _dot_nt(dfr, wdj)
            da = dhmid * bv * (sig * (1.0 + av * (1.0 - sig)))
            db = dhmid * silu
            ad[...] += _dot_tn(silu * bv, dfr)
            ag[...] += _dot_tn(da, nr)
            au[...] += _dot_tn(db, nr)
            dn_v[rows, :] += _dot(da, wgj) + _dot(db, wuj)

        @pl.when(i == ni - 1)
        def _():
            dwg_ref[...] = _bf(ag[...])
            dwu_ref[...] = _bf(au[...])
            dwd_ref[...] = _bf(ad[...])

        @pl.when((j == nj - 1) & (i == ni - 1))
        def _():
            c = pltpu.make_async_copy(dn_v, dn_hbm, sem.at[0])
            c.start()
            c.wait()

    roww = pl.BlockSpec((tj, D_MODEL), lambda j, i: (j, 0))
    act = pl.BlockSpec((tm, tj), lambda j, i: (i, j))
    return pl.pallas_call(
        body, name=name, grid=(nj, ni),
        in_specs=[ANY, ANY, act, act, roww, roww, roww],
        out_specs=[roww, roww, roww, ANY],
        out_shape=[_sds((D_FF_PAD, D_MODEL), BF16)] * 3 + [_sds((t, D_MODEL))],
        scratch_shapes=[pltpu.VMEM((t, D_MODEL), BF16), pltpu.VMEM((t, D_MODEL), BF16),
                        pltpu.VMEM((t, D_MODEL), F32)]
        + [pltpu.VMEM((tj, D_MODEL), F32)] * 3 + [pltpu.SemaphoreType.DMA((2,))],
        compiler_params=_params("arbitrary", "arbitrary"),
    )(n, df, a, b, wg, wu, wd)


def _junction(dres, pre=None, post=None, *, name):
    t = dres.shape[0]
    tm = 512
    ni = t // tm
    n_in = 1 + (3 if pre else 0) + (2 if post else 0)
    coef = post[2] if post else None

    def body(*refs):
        ins, outs = list(refs[:n_in]), list(refs[n_in:])
        i = pl.program_id(0)
        dh = ins.pop(0)[...]
        if pre:
            dn_ref, x_ref, gpre_ref = ins.pop(0), ins.pop(0), ins.pop(0)
            dh_ref, dgpre_ref = outs.pop(0), outs.pop(0)
            dx, dg = _rms_bwd(dn_ref[...], x_ref[...], gpre_ref[...])
            dh = dh + dx
            dh_ref[...] = dh

            @pl.when(i == 0)
            def _():
                dgpre_ref[...] = jnp.zeros_like(dgpre_ref)
            dgpre_ref[...] += dg
        if post:
            f_ref, gpost_ref = ins.pop(0), ins.pop(0)
            df_ref, dgpost_ref = outs.pop(0), outs.pop(0)
            df, dg = _rms_bwd(coef * dh, f_ref[...], gpost_ref[...])
            df_ref[...] = _bf(df)

            @pl.when(i == 0)
            def _():
                dgpost_ref[...] = jnp.zeros_like(dgpost_ref)
            dgpost_ref[...] += dg

    row = pl.BlockSpec((tm, D_MODEL), lambda i: (i, 0))
    gain = pl.BlockSpec((1, D_MODEL), lambda i: (0, 0))
    args, in_specs, out_specs, out_shape = [dres], [row], [], []
    if pre:
        args += list(pre)
        in_specs += [row, row, gain]
        out_specs += [row, gain]
        out_shape += [_sds((t, D_MODEL)), _sds((1, D_MODEL))]
    if post:
        args += [post[0], post[1]]
        in_specs += [row, gain]
        out_specs += [row, gain]
        out_shape += [_sds((t, D_MODEL), BF16), _sds((1, D_MODEL))]
    return pl.pallas_call(
        body, name=name, grid=(ni,), in_specs=in_specs, out_specs=out_specs,
        out_shape=out_shape, compiler_params=_params("arbitrary"),
    )(*args)


def _qkv_fwd(h, g, win, *, name):
    t = h.shape[0]
    tm, tn = min(1024, t), 1024
    ni, nj = t // tm, QKV_WIDTH // tn

    def body(h_ref, g_ref, w_ref, qkv_ref, u_ref):
        @pl.when(pl.program_id(1) == 0)
        def _():
            u_ref[...] = _bf(_rms(h_ref[...], g_ref[...]))
        qkv_ref[...] = jnp.dot(u_ref[...], w_ref[...], preferred_element_type=F32)

    row = pl.BlockSpec((tm, D_MODEL), lambda i, j: (i, 0))
    return pl.pallas_call(
        body, name=name, grid=(ni, nj),
        in_specs=[row, pl.BlockSpec((1, D_MODEL), lambda i, j: (0, 0)),
                  pl.BlockSpec((D_MODEL, tn), lambda i, j: (0, j))],
        out_specs=[pl.BlockSpec((tm, tn), lambda i, j: (i, j)), row],
        out_shape=[_sds((t, QKV_WIDTH)), _sds((t, D_MODEL), BF16)],
        compiler_params=_params("arbitrary", "arbitrary"),
    )(h, g, win)


def _qkv_bwd(dq, dk, dv, u, win, *, name):
    t = u.shape[0]
    tn, ts = 512, 512
    nj, ns = QKV_WIDTH // tn, t // ts

    def body(dq_ref, dk_ref, dv_ref, u_ref, w_ref, dw_ref, du_hbm, du_v, acc_ref, sem):
        j = pl.program_id(0)

        @pl.when(j == 0)
        def _():
            du_v[...] = jnp.zeros_like(du_v)

        wj = w_ref[...]
        for role, d_ref in enumerate((dq_ref, dk_ref, dv_ref)):
            @pl.when(j % 3 == role)
            def _():
                acc_ref[...] = jnp.zeros_like(acc_ref)
                for s in range(ns):
                    rows = pl.ds(s * ts, ts)
                    dcol = d_ref[rows, :]
                    acc_ref[...] += _dot_tn(u_ref[rows, :], dcol)
                    du_v[rows, :] += _dot_nt(dcol, wj)
                dw_ref[...] = _bf(acc_ref[...])

        @pl.when(j == nj - 1)
        def _():
            c = pltpu.make_async_copy(du_v, du_hbm, sem)
            c.start()
            c.wait()

    colw = pl.BlockSpec((D_MODEL, tn), lambda j: (0, j))
    grp = pl.BlockSpec((t, tn), lambda j: (0, j // 3))
    return pl.pallas_call(
        body, name=name, grid=(nj,),
        in_specs=[grp, grp, grp, pl.BlockSpec((t, D_MODEL), lambda j: (0, 0)), colw],
        out_specs=[colw, ANY],
        out_shape=[_sds((D_MODEL, QKV_WIDTH), BF16), _sds((t, D_MODEL))],
        scratch_shapes=[pltpu.VMEM((t, D_MODEL), F32), pltpu.VMEM((D_MODEL, tn), F32),
                        pltpu.SemaphoreType.DMA],
        compiler_params=_params("arbitrary"),
    )(dq, dk, dv, u, win)


def _sb_stack(x):
    lo, hi = _head_masks()
    return jnp.concatenate([jnp.where(lo, x, 0.0), jnp.where(hi, x, 0.0)], axis=0)


def _sb_unstack(x2, blk):
    return jnp.where(_head_masks()[0], x2[:blk], x2[blk:])


def _sb_rows_from(x2, blk, r0):
    return x2 if r0 == 0 else jnp.concatenate([x2[r0:blk], x2[blk + r0:]], axis=0)


def _sb_rows_merge(full2, sub2, blk, r0):
    if r0 == 0:
        return sub2
    rows = blk - r0
    return jnp.concatenate([full2[:r0], sub2[:rows], full2[blk:blk + r0], sub2[rows:]], axis=0)


def _sb_mask(qb, kb, offset):
    r = lax.broadcasted_iota(jnp.int32, (2 * qb, kb), 0) & (qb - 1)
    c = lax.broadcasted_iota(jnp.int32, (2 * qb, kb), 1) + offset
    return c < r


def _tri(n, keep):
    r = lax.broadcasted_iota(jnp.int32, (n, n), 0)
    c = lax.broadcasted_iota(jnp.int32, (n, n), 1)
    return jnp.where(keep(r, c), 1.0, 0.0).astype(BF16)


def _cumsum01(x, u):
    m = x.shape[0]
    hi = _bf(x)
    lo = _bf(x - hi.astype(F32))
    both = jnp.dot(jnp.concatenate([hi, lo], axis=0), u, preferred_element_type=F32)
    return both[:m] + both[m:]


def _sb_fwd(qkv, *, name):
    t = qkv.shape[0]
    blk, kb = min(SB_QB, t), SB_KB
    ni, per = t // blk, blk // kb

    def body(q_ref, k_ref, v_ref, o_ref, ltot_ref):
        i = pl.program_id(1)
        u_after = _tri(kb, lambda r, c: r > c)
        q2 = [_bf(_sb_stack(q_ref[:, lanes] * ATT_SCALE)) for lanes in SB_LANES]

        def tile(g, k0, mask, acc, c_l):
            kj = k_ref[pl.ds(k0, kb), SB_LANES[g]]
            vj = v_ref[pl.ds(k0, kb), SB_LANES[g]]
            z = _dot_nt(q2[g], kj)
            sp = _softplus(z)
            lf = -sp if mask is None else jnp.where(mask, -sp, 0.0)
            a = jnp.exp(z - sp + _cumsum01(lf, u_after) + c_l)
            if mask is not None:
                a = jnp.where(mask, a, 0.0)
            return acc + _dot(a, vj), c_l + jnp.sum(lf, axis=1, keepdims=True)

        def tiles(k0, mask, carry):
            return tuple(tile(g, k0, mask, *carry[g]) for g in range(SB_GROUP))

        carry = ((jnp.zeros((2 * blk, PAIR), F32), jnp.zeros((2 * blk, 1), F32)),) * SB_GROUP
        for d in reversed(range(per)):
            carry = tiles(pl.multiple_of(i * blk + d * kb, kb), _sb_mask(blk, kb, d * kb), carry)
        carry = lax.fori_loop(
            1, per * i + 1,
            lambda jj, c: tiles(pl.multiple_of((per * i - jj) * kb, kb), None, c), carry)
        for g, (acc, c_l) in enumerate(carry):
            o_ref[:, SB_LANES[g]] = _sb_unstack(acc, blk)
            ltot_ref[:, SB_LANES[g]] = _sb_unstack(jnp.broadcast_to(c_l, (2 * blk, PAIR)), blk)

    width = SB_GROUP * PAIR
    blkspec = pl.BlockSpec((blk, width), lambda p, i: (i, p))
    n_steps = N_PAIRS // SB_GROUP
    return pl.pallas_call(
        body, name=name, grid=(n_steps, ni),
        in_specs=[blkspec,
                  pl.BlockSpec((t, width), lambda p, i: (0, n_steps + p)),
                  pl.BlockSpec((t, width), lambda p, i: (0, 2 * n_steps + p))],
        out_specs=[blkspec, blkspec],
        out_shape=[_sds((t, D_MODEL)), _sds((t, D_MODEL // 2))],
        compiler_params=_params("arbitrary", "arbitrary"),
    )(qkv, qkv, qkv)


def _sb_bwd(qkv, ltot, do, *, name):
    t = qkv.shape[0]
    blk, kb = min(SB_QB, t), SB_KB
    ni, per = t // blk, blk // kb

    def body(q_ref, k_ref, v_ref, lt_ref, do_ref, dq_ref, dkout_ref, dvout_ref, dk_ref, dv_ref):
        i = pl.program_id(1)

        @pl.when(i == 0)
        def _():
            dk_ref[...] = jnp.zeros_like(dk_ref)
            dv_ref[...] = jnp.zeros_like(dv_ref)

        u_upto = _tri(kb, lambda r, c: r <= c)
        u_before = _tri(kb, lambda r, c: r < c)
        lane = lax.broadcasted_iota(jnp.int32, (1, PAIR), 1)
        q2 = [_bf(_sb_stack(q_ref[:, lanes] * ATT_SCALE)) for lanes in SB_LANES]
        do2 = [_bf(_sb_stack(do_ref[:, lanes])) for lanes in SB_LANES]
        total = [jnp.concatenate(
            [jnp.sum(jnp.where(lane == h * HEAD_DIM, lt_ref[:, lanes], 0.0), axis=1, keepdims=True)
             for h in range(2)], axis=0) for lanes in SB_LANES]

        def tile(g, ops, k0, mask, dq_acc, c_l, c_g):
            qg, dog, tot = ops
            krows = pl.ds(k0, kb)
            kj = k_ref[krows, SB_LANES[g]]
            vj = v_ref[krows, SB_LANES[g]]
            z = _dot_nt(qg, kj)
            sp = _softplus(z)
            sig = jnp.exp(z - sp)
            lf = -sp if mask is None else jnp.where(mask, -sp, 0.0)
            a = jnp.exp(z - sp + tot - (_cumsum01(lf, u_upto) + c_l))
            if mask is not None:
                a = jnp.where(mask, a, 0.0)
            gw = a * _dot_nt(dog, vj)
            g_before = jnp.dot(_bf(gw), u_before, preferred_element_type=F32) + c_g
            dz = gw * (1.0 - sig) - g_before * sig
            if mask is not None:
                dz = jnp.where(mask, dz, 0.0)
            dk_ref[krows, SB_LANES[g]] += _dot_tn(dz, qg)
            dv_ref[krows, SB_LANES[g]] += _dot_tn(a, dog)
            return (dq_acc + _dot(dz, kj), c_l + jnp.sum(lf, axis=1, keepdims=True),
                    c_g + jnp.sum(gw, axis=1, keepdims=True))

        def tiles(ops, k0, mask, carry):
            return tuple(tile(g, ops[g], k0, mask, *carry[g]) for g in range(SB_GROUP))

        ops = tuple(zip(q2, do2, total))
        zero = (jnp.zeros((2 * blk, PAIR), F32), jnp.zeros((2 * blk, 1), F32),
                jnp.zeros((2 * blk, 1), F32))
        carry = lax.fori_loop(
            0, per * i, lambda j, c: tiles(ops, pl.multiple_of(j * kb, kb), None, c),
            (zero,) * SB_GROUP)
        for d in range(per):
            r0 = d * kb
            sub = tiles(tuple(tuple(_sb_rows_from(a, blk, r0) for a in o) for o in ops),
                        pl.multiple_of(i * blk + r0, kb), _sb_mask(blk - r0, kb, 0),
                        tuple(tuple(_sb_rows_from(a, blk, r0) for a in c) for c in carry))
            carry = tuple(tuple(_sb_rows_merge(a, s, blk, r0) for a, s in zip(c, cs))
                          for c, cs in zip(carry, sub))
        for g, (dq_acc, _, _) in enumerate(carry):
            dq_ref[:, SB_LANES[g]] = _bf(_sb_unstack(dq_acc, blk) * ATT_SCALE)

        @pl.when(i == ni - 1)
        def _():
            dkout_ref[...] = _bf(dk_ref[...])
            dvout_ref[...] = _bf(dv_ref[...])

    width = SB_GROUP * PAIR
    n_steps = N_PAIRS // SB_GROUP
    blkspec = lambda off: pl.BlockSpec((blk, width), lambda p, i: (i, off + p))
    full = lambda off: pl.BlockSpec((t, width), lambda p, i: (0, off + p))
    return pl.pallas_call(
        body, name=name, grid=(n_steps, ni),
        in_specs=[blkspec(0), full(n_steps), full(2 * n_steps), blkspec(0), blkspec(0)],
        out_specs=[blkspec(0), full(0), full(0)],
        out_shape=[_sds((t, D_MODEL), BF16)] * 3,
        scratch_shapes=[pltpu.VMEM((t, width), F32), pltpu.VMEM((t, width), F32)],
        compiler_params=_params("arbitrary", "arbitrary"),
    )(qkv, qkv, qkv, ltot, do)


def _ch_mask(i):
    c = lax.broadcasted_iota(jnp.int32, (1, CH_WIN), 1)
    return c >= CH_LOOK - i * CH_QB


def _ch_band(row0, rows):
    r = row0 + lax.broadcasted_iota(jnp.int32, (rows, CH_WIN), 0)
    c = lax.broadcasted_iota(jnp.int32, (rows, CH_WIN), 1)
    qc = LOOKBACK + lax.shift_right_arithmetic(r, 6)
    kc = lax.shift_right_arithmetic(c, 6)
    return (kc <= qc) & (kc >= qc - LOOKBACK)


def _ch_probs(qm, kw, bias_h, mask):
    z = _dot_nt(qm, kw) + bias_h
    z = jnp.where(mask, z, NEG_INF)
    e = jnp.exp(z - jnp.max(z, axis=1, keepdims=True))
    return e * (1.0 / jnp.sum(e, axis=1, keepdims=True))


def _ch_fill(pad_ref, src_ref, t):
    pad_ref[pl.ds(0, CH_LOOK), :] = jnp.zeros((CH_LOOK, PAIR), BF16)
    pad_ref[pl.ds(CH_LOOK, t), :] = _bf(src_ref[...])


def _ch_fwd(qkv, bias, o_in, *, name):
    t = qkv.shape[0]
    ni = t // CH_QB

    def body(q_ref, k_ref, v_ref, bias_ref, _alias, o_ref, kpad, vpad):
        i = pl.program_id(1)

        @pl.when(i == 0)
        def _():
            _ch_fill(kpad, k_ref, t)
            _ch_fill(vpad, v_ref, t)

        win = pl.ds(pl.multiple_of(i * CH_QB, CH_QB), CH_WIN)
        kw, vw = kpad[win, :], vpad[win, :]
        mask = _ch_mask(i)
        q = q_ref[...] * ATT_SCALE
        outs = []
        for h, hm in enumerate(_head_masks()):
            p = _ch_probs(jnp.where(hm, q, 0.0), kw, bias_ref[h], mask)
            outs.append(_dot(p, vw))
        o_ref[...] = jnp.where(_head_masks()[0], outs[0], outs[1])

    full = lambda off: pl.BlockSpec((t, PAIR), lambda p, i: (0, off + p))
    return pl.pallas_call(
        body, name=name, grid=(N_PAIRS, ni),
        in_specs=[pl.BlockSpec((CH_QB, PAIR), lambda p, i: (i, 3 * N_PAIRS + p)),
                  full(4 * N_PAIRS), full(5 * N_PAIRS),
                  pl.BlockSpec((2, CH_QB, CH_WIN), lambda p, i: (p, 0, 0)), ANY],
        out_specs=pl.BlockSpec((CH_QB, PAIR), lambda p, i: (i, N_PAIRS + p)),
        out_shape=_sds((t, D_MODEL)),
        scratch_shapes=[pltpu.VMEM((t + CH_LOOK, PAIR), BF16)] * 2,
        input_output_aliases={4: 0},
        compiler_params=_params("arbitrary", "arbitrary"),
    )(qkv, qkv, qkv, bias, o_in)


def _ch_bwd(qkv, bias, o, do, dq_in, dk_in, dv_in, *, name):
    t = qkv.shape[0]
    ni = t // CH_QB

    def body(q_ref, k_ref, v_ref, bias_ref, o_ref, do_ref, _a0, _a1, _a2,
             dq_ref, dkout_ref, dvout_ref, dbias_ref, kpad, vpad, dkpad, dvpad):
        i = pl.program_id(1)

        @pl.when(i == 0)
        def _():
            _ch_fill(kpad, k_ref, t)
            _ch_fill(vpad, v_ref, t)
            dkpad[...] = jnp.zeros_like(dkpad)
            dvpad[...] = jnp.zeros_like(dvpad)
            dbias_ref[...] = jnp.zeros_like(dbias_ref)

        win = pl.ds(pl.multiple_of(i * CH_QB, CH_QB), CH_WIN)
        kw, vw = kpad[win, :], vpad[win, :]
        mask = _ch_mask(i)
        q, o_blk, do_blk = q_ref[...] * ATT_SCALE, o_ref[...], do_ref[...]
        dqs = []
        for h, hm in enumerate(_head_masks()):
            qm = _bf(jnp.where(hm, q, 0.0))
            dom = jnp.where(hm, do_blk, 0.0)
            delta = jnp.sum(dom * o_blk, axis=1, keepdims=True)
            dom = _bf(dom)
            p = _ch_probs(qm, kw, bias_ref[h], mask)
            ds = p * (_dot_nt(dom, vw) - delta)
            dbias_ref[h] += ds
            dqs.append(_dot(ds, kw))
            dkpad[win, :] += _dot_tn(ds, qm)
            dvpad[win, :] += _dot_tn(p, dom)
        dq_ref[...] = _bf(jnp.where(_head_masks()[0], dqs[0], dqs[1]) * ATT_SCALE)

        @pl.when(i == ni - 1)
        def _():
            dkout_ref[...] = _bf(dkpad[pl.ds(CH_LOOK, t), :])
            dvout_ref[...] = _bf(dvpad[pl.ds(CH_LOOK, t), :])

    blkspec = lambda off: pl.BlockSpec((CH_QB, PAIR), lambda p, i: (i, off + p))
    full = lambda off: pl.BlockSpec((t, PAIR), lambda p, i: (0, off + p))
    bias_spec = pl.BlockSpec((2, CH_QB, CH_WIN), lambda p, i: (p, 0, 0))
    return pl.pallas_call(
        body, name=name, grid=(N_PAIRS, ni),
        in_specs=[blkspec(3 * N_PAIRS), full(4 * N_PAIRS), full(5 * N_PAIRS), bias_spec,
                  blkspec(N_PAIRS), blkspec(N_PAIRS), ANY, ANY, ANY],
        out_specs=[blkspec(N_PAIRS), full(N_PAIRS), full(N_PAIRS), bias_spec],
        out_shape=[_sds((t, D_MODEL), BF16)] * 3 + [_sds((2 * N_PAIRS, CH_QB, CH_WIN))],
        scratch_shapes=[pltpu.VMEM((t + CH_LOOK, PAIR), BF16)] * 2
        + [pltpu.VMEM((t + CH_LOOK, PAIR), F32)] * 2,
        input_output_aliases={6: 0, 7: 1, 8: 2},
        compiler_params=_params("arbitrary", "arbitrary"),
    )(qkv, qkv, qkv, bias, o, do, dq_in, dk_in, dv_in)


def _bias_expand(fvec, *, name):
    n_heads = fvec.shape[0]

    def body(f_ref, o_ref, rows8):
        row = f_ref[0]
        for r in range(8):
            rows8[pl.ds(r, 1), :] = pltpu.roll(row, r, 1)
        base = rows8[...]
        for blk in range(CH_QB // 8):
            o_ref[0, pl.ds(8 * blk, 8), :] = jnp.where(
                _ch_band(8 * blk, 8), pltpu.roll(base, 8 * blk, 1), NEG_INF)

    return pl.pallas_call(
        body, name=name, grid=(n_heads,),
        in_specs=[pl.BlockSpec((1, 1, CH_WIN), lambda h: (h, 0, 0))],
        out_specs=pl.BlockSpec((1, CH_QB, CH_WIN), lambda h: (h, 0, 0)),
        out_shape=_sds((n_heads, CH_QB, CH_WIN)),
        scratch_shapes=[pltpu.VMEM((8, CH_WIN), F32)],
        compiler_params=_params("arbitrary"),
    )(fvec)


def _bias_grad(dbias, after, *, name):
    n_heads = dbias.shape[0]
    first = CH_LOOK - REL_CLIP

    def body(d_ref, _after, o_ref, acc8):
        acc = jnp.zeros((8, CH_WIN), F32)
        for blk in range(CH_QB // 8):
            acc = acc + pltpu.roll(d_ref[0, pl.ds(8 * blk, 8), :], (CH_WIN - 8 * blk) % CH_WIN, 1)
        acc8[...] = acc
        dvec = jnp.zeros((1, CH_WIN), F32)
        for r in range(8):
            dvec = dvec + pltpu.roll(acc8[pl.ds(r, 1), :], (CH_WIN - r) % CH_WIN, 1)
        lane = lax.broadcasted_iota(jnp.int32, (1, CH_WIN), 1)
        clipped = (lane <= first) | (lane >= first + REL_CLIP + CHUNK)
        total = jnp.sum(jnp.where(clipped, dvec, 0.0), axis=1, keepdims=True)
        o_ref[0] = jnp.where(lane == first, total, dvec)

    return pl.pallas_call(
        body, name=name, grid=(n_heads,),
        in_specs=[pl.BlockSpec((1, CH_QB, CH_WIN), lambda h: (h, 0, 0)), ANY],
        out_specs=pl.BlockSpec((1, 1, CH_WIN), lambda h: (h, 0, 0)),
        out_shape=_sds((n_heads, 1, CH_WIN)),
        scratch_shapes=[pltpu.VMEM((8, CH_WIN), F32)],
        compiler_params=_params("arbitrary"),
    )(dbias, after)


def _out_fwd(o, h1, g_sb, g_ch, g_post, wout, *, name):
    t = o.shape[0]
    tm = 512
    half = D_MODEL // 2

    def body(o_ref, h_ref, gsb_ref, gch_ref, gpost_ref, w_ref, h2_ref, mixed_ref, y_ref):
        ov = o_ref[...]
        mixed = jnp.concatenate([_rms(ov[:, :half], gsb_ref[...]),
                                 _rms(ov[:, half:], gch_ref[...])], axis=1)
        mixed_ref[...] = _bf(mixed)
        y = _dot(mixed, w_ref[...])
        y_ref[...] = y
        h2_ref[...] = h_ref[...] + _rms(y, gpost_ref[...])

    row = pl.BlockSpec((tm, D_MODEL), lambda i: (i, 0))
    gain = lambda n: pl.BlockSpec((1, n), lambda i: (0, 0))
    return pl.pallas_call(
        body, name=name, grid=(t // tm,),
        in_specs=[row, row, gain(half), gain(half), gain(D_MODEL),
                  pl.BlockSpec((D_MODEL, D_MODEL), lambda i: (0, 0))],
        out_specs=[row, row, row],
        out_shape=[_sds((t, D_MODEL)), _sds((t, D_MODEL), BF16), _sds((t, D_MODEL))],
        compiler_params=_params("arbitrary"),
    )(o, h1, g_sb, g_ch, g_post, wout)


def _out_bwd(dy, mixed, o, g_sb, g_ch, wout, *, name):
    t = o.shape[0]
    tm = 512
    ni = t // tm
    half = D_MODEL // 2

    def body(dy_ref, mixed_ref, o_ref, gsb_ref, gch_ref, w_ref,
             dw_ref, do_ref, dgsb_ref, dgch_ref, acc_ref):
        i = pl.program_id(0)

        @pl.when(i == 0)
        def _():
            acc_ref[...] = jnp.zeros_like(acc_ref)
            dgsb_ref[...] = jnp.zeros_like(dgsb_ref)
            dgch_ref[...] = jnp.zeros_like(dgch_ref)

        dyv = dy_ref[...]
        acc_ref[...] += _dot_tn(mixed_ref[...], dyv)
        dm = _dot_nt(dyv, w_ref[...])
        ov = o_ref[...]
        doa, dga = _rms_bwd(dm[:, :half], ov[:, :half], gsb_ref[...])
        dob, dgb = _rms_bwd(dm[:, half:], ov[:, half:], gch_ref[...])
        do_ref[...] = jnp.concatenate([doa, dob], axis=1)
        dgsb_ref[...] += dga
        dgch_ref[...] += dgb

        @pl.when(i == ni - 1)
        def _():
            dw_ref[...] = _bf(acc_ref[...])

    row = pl.BlockSpec((tm, D_MODEL), lambda i: (i, 0))
    gain = pl.BlockSpec((1, half), lambda i: (0, 0))
    sq = pl.BlockSpec((D_MODEL, D_MODEL), lambda i: (0, 0))
    return pl.pallas_call(
        body, name=name, grid=(ni,),
        in_specs=[row, row, row, gain, gain, sq],
        out_specs=[sq, row, gain, gain],
        out_shape=[_sds((D_MODEL, D_MODEL), BF16), _sds((t, D_MODEL)),
                   _sds((1, half)), _sds((1, half))],
        scratch_shapes=[pltpu.VMEM((D_MODEL, D_MODEL), F32)],
        compiler_params=_params("arbitrary"),
    )(dy, mixed, o, g_sb, g_ch, wout)


def _ple(p, h3, target, wp, wgate, g, f_post, g_post, *, name):
    t = h3.shape[0]
    tm = 512
    ni = t // tm

    def body(p_ref, h_ref, tgt_ref, wp_ref, wg_ref, g_ref, f_ref, gf_ref,
             loss_ref, dres_ref, dwp_ref, dwg_ref, dg_ref, df_ref, dgf_ref, accp, accg):
        i = pl.program_id(0)

        @pl.when(i == 0)
        def _():
            loss_ref[...] = jnp.zeros_like(loss_ref)
            dg_ref[...] = jnp.zeros_like(dg_ref)
            dgf_ref[...] = jnp.zeros_like(dgf_ref)
            accp[...] = jnp.zeros_like(accp)
            accg[...] = jnp.zeros_like(accg)

        pv, hv, gv = p_ref[...], h_ref[...], g_ref[...]
        pe = _dot(pv, wp_ref[...])
        sig = _sigmoid(_dot(hv, wg_ref[...]))
        e = pe * sig
        err = hv + _rms(e, gv) - tgt_ref[...]
        tok = jnp.mean(err * err, axis=-1, keepdims=True)
        loss_ref[...] += 0.5 * jnp.sum(tok, axis=0, keepdims=True)
        dh4 = err * (1.0 / D_MODEL)
        de, dg = _rms_bwd(dh4, e, gv)
        dg_ref[...] += dg
        dpe = de * sig
        dgt = de * pe * sig * (1.0 - sig)
        accp[...] += _dot_tn(pv, dpe)
        accg[...] += _dot_tn(hv, dgt)
        dres = dh4 + _dot_nt(dgt, wg_ref[...])
        dres_ref[...] = dres
        df, dgf = _rms_bwd(0.5 * dres, f_ref[...], gf_ref[...])
        df_ref[...] = _bf(df)
        dgf_ref[...] += dgf

        @pl.when(i == ni - 1)
        def _():
            dwp_ref[...] = _bf(accp[...])
            dwg_ref[...] = _bf(accg[...])

    row = pl.BlockSpec((tm, D_MODEL), lambda i: (i, 0))
    const = lambda r, c: pl.BlockSpec((r, c), lambda i: (0, 0))
    return pl.pallas_call(
        body, name=name, grid=(ni,),
        in_specs=[pl.BlockSpec((tm, PLE_DIM), lambda i: (i, 0)), row, row,
                  const(PLE_DIM, D_MODEL), const(D_MODEL, D_MODEL), const(1, D_MODEL),
                  row, const(1, D_MODEL)],
        out_specs=[const(1, 128), row, const(PLE_DIM, D_MODEL), const(D_MODEL, D_MODEL),
                   const(1, D_MODEL), row, const(1, D_MODEL)],
        out_shape=[_sds((1, 128)), _sds((t, D_MODEL)), _sds((PLE_DIM, D_MODEL), BF16),
                   _sds((D_MODEL, D_MODEL), BF16), _sds((1, D_MODEL)),
                   _sds((t, D_MODEL), BF16), _sds((1, D_MODEL))],
        scratch_shapes=[pltpu.VMEM((PLE_DIM, D_MODEL), F32), pltpu.VMEM((D_MODEL, D_MODEL), F32)],
        compiler_params=_params("arbitrary"),
    )(p, h3, target, wp, wgate, g, f_post, g_post)


def _rel_bias_to_fvec(rel_bias):
    rev = rel_bias[:, ::-1]
    n_heads = rel_bias.shape[0]
    first = CH_LOOK - REL_CLIP
    n_var = REL_CLIP + CHUNK
    clipped = rev[:, :1]
    fvec = jnp.concatenate([jnp.broadcast_to(clipped, (n_heads, first)), rev[:, :n_var],
                            jnp.broadcast_to(clipped, (n_heads, CH_WIN - first - n_var))], axis=1)
    return fvec.reshape(n_heads, 1, CH_WIN)


def _fvec_grad_to_rel_bias(dfvec):
    first = CH_LOOK - REL_CLIP
    n_var = REL_CLIP + CHUNK
    rev = jnp.pad(dfvec[:, 0, first:first + n_var], ((0, 0), (0, N_REL - n_var)))
    return rev[:, ::-1]


def _local_step(x, p, target, g, weights_for, grads_done, fvec, weights_early=None):
    bias = _bias_expand(fvec, name="bias_expand")
    w, tie = weights_for(0, bias)
    w = dict(w)
    h1, n1, a1, b1, f1 = _ffn_fwd(x, g["ffn1_pre"] + tie, g["ffn1_post"],
                                  w["ffn1_gate"], w["ffn1_up"], w["ffn1_down"], name="ffn1_fwd")
    more, tie = weights_for(1, h1)
    w.update(more)
    qkv, u = _qkv_fwd(h1, g["mix_pre"] + tie, w["in"], name="qkv_fwd")
    o, ltot = _sb_fwd(qkv, name="sb_fwd")
    tie = weights_early(2, ltot) if weights_early else 0.0
    o = _ch_fwd(qkv, bias, o, name="ch_fwd")
    w.update(weights_for(2, o)[0])
    h2, mixed, y = _out_fwd(o, h1, g["out_sb"] + tie, g["out_ch"], g["mix_post"], w["out"],
                            name="out_fwd")
    h3, n2, a2, b2, f2 = _ffn_fwd(h2, g["ffn2_pre"], g["ffn2_post"],
                                  w["ffn2_gate"], w["ffn2_up"], w["ffn2_down"], name="ffn2_fwd")
    loss, dh3, dwp, dwgate, dg_ple, df2, dg_ffn2_post = _ple(
        p, h3, target, w["ple_proj"], w["ple_gate"], g["ple_post"], f2, g["ffn2_post"], name="ple")
    tie = grads_done(0, {"ple_proj": dwp, "ple_gate": dwgate})
    dwg2, dwu2, dwd2, dn2 = _ffn_bwd(n2, df2, a2, b2, w["ffn2_gate"], w["ffn2_up"],
                                     w["ffn2_down"], name="ffn2_bwd")
    tie = tie + grads_done(1, {"ffn2_gate": dwg2, "ffn2_up": dwu2, "ffn2_down": dwd2})
    dh2, dg_ffn2_pre, dy, dg_mix_post = _junction(
        dh3, pre=(dn2, h2, g["ffn2_pre"] + tie), post=(y, g["mix_post"], 1.0), name="junction2")
    dwout, do, dg_sb, dg_ch = _out_bwd(dy, mixed, o, g["out_sb"], g["out_ch"], w["out"],
                                       name="out_bwd")
    dq, dk, dv = _sb_bwd(qkv, ltot, do, name="sb_bwd")
    dq, dk, dv, dbias = _ch_bwd(qkv, bias, o, do, dq, dk, dv, name="ch_bwd")
    dwin, du = _qkv_bwd(dq, dk, dv, u, w["in"], name="qkv_bwd")
    tie = grads_done(2, {"out": dwout, "in": dwin})
    dh1, dg_mix_pre, df1, dg_ffn1_post = _junction(
        dh2, pre=(du, h1, g["mix_pre"] + tie), post=(f1, g["ffn1_post"], 0.5), name="junction1")
    dwg1, dwu1, dwd1, dn1 = _ffn_bwd(n1, df1, a1, b1, w["ffn1_gate"], w["ffn1_up"],
                                     w["ffn1_down"], name="ffn1_bwd")
    tie = grads_done(3, {"ffn1_gate": dwg1, "ffn1_up": dwu1, "ffn1_down": dwd1})
    dx, dg_ffn1_pre = _junction(dh1, pre=(dn1, x, g["ffn1_pre"] + tie), name="junction0")

    dg = {"ffn1_pre": dg_ffn1_pre, "ffn1_post": dg_ffn1_post, "mix_pre": dg_mix_pre,
          "mix_post": dg_mix_post, "out_sb": dg_sb, "out_ch": dg_ch,
          "ffn2_pre": dg_ffn2_pre, "ffn2_post": dg_ffn2_post, "ple_post": dg_ple}
    return loss, dx, dg, dbias


_WEIGHTS = (
    ("ffn1_gate", "row", FF_SHARD, FF_SHARD_PAD, D_MODEL),
    ("ffn1_up", "row", FF_SHARD, FF_SHARD_PAD, D_MODEL),
    ("ffn1_down", "row", FF_SHARD, FF_SHARD_PAD, D_MODEL),
    ("in", "col", QKV_SHARD, QKV_SHARD, D_MODEL),
    ("out", "row", ROW_SHARD, ROW_SHARD, D_MODEL),
    ("ffn2_gate", "row", FF_SHARD, FF_SHARD_PAD, D_MODEL),
    ("ffn2_up", "row", FF_SHARD, FF_SHARD_PAD, D_MODEL),
    ("ffn2_down", "row", FF_SHARD, FF_SHARD_PAD, D_MODEL),
    ("ple_proj", "col", ROW_SHARD, ROW_SHARD, PLE_DIM),
    ("ple_gate", "row", ROW_SHARD, ROW_SHARD, D_MODEL),
)
_TRANSPOSED = ("ffn1_gate", "ffn1_up", "ffn2_gate", "ffn2_up")
_SPEC = {n: (kind, valid, pad, other) for n, kind, valid, pad, other in _WEIGHTS}
_GATHER_STAGES = (("ffn1_gate", "ffn1_up", "ffn1_down"), ("in",),
                  ("out", "ffn2_gate", "ffn2_up", "ffn2_down", "ple_proj", "ple_gate"))
_SCATTER_STAGES = (("ple_proj", "ple_gate"), ("ffn2_gate", "ffn2_up", "ffn2_down"),
                   ("out", "in"), ("ffn1_gate", "ffn1_up", "ffn1_down"))
HBM = pl.BlockSpec(memory_space=pltpu.HBM)
SEM = pl.BlockSpec(memory_space=pltpu.SEMAPHORE)
EFFECT = pltpu.SideEffectType.DATAFLOW_SIDE_EFFECTING


def _shard_shape(kind, size, other):
    return (other, size) if kind == "col" else (size, other)


def _window(ref, kind, start, size):
    return ref.at[:, pl.ds(start, size)] if kind == "col" else ref.at[pl.ds(start, size), :]


def _device_tuple(k):
    return (k // 4, (k // 2) % 2, k % 2)


def _my_index():
    return 4 * lax.axis_index("x") + 2 * lax.axis_index("y") + lax.axis_index("c")


def _pack_weights(names, shards, after, *, name):
    nw = len(names)
    specs = [_SPEC[n] for n in names]

    def body(*refs):
        ins, packed, full = refs[:nw], refs[nw + 1:2 * nw + 1], refs[2 * nw + 1:3 * nw + 1]
        sem = refs[3 * nw + 1]
        me = _my_index()
        for (kind, valid, pad, _), src, dst in zip(specs, ins, packed):
            if pad != valid:
                dst[...] = jnp.zeros_like(dst)
            if kind == "col":
                dst[:, pl.ds(0, valid)] = _bf(src[...])
            else:
                dst[pl.ds(0, valid), :] = _bf(src[...])
        for k in range(N_DEV):
            @pl.when(me == k)
            def _():
                for w, (kind, _, pad, _) in enumerate(specs):
                    pltpu.make_async_copy(packed[w], _window(full[w], kind, k * pad, pad),
                                          sem.at[w]).start()
        for w, (kind, _, pad, _) in enumerate(specs):
            pltpu.make_async_copy(packed[w], _window(full[w], kind, 0, pad), sem.at[w]).wait()

    whole = lambda shape: pl.BlockSpec(shape, lambda i: (0, 0))
    packed_shapes = [_shard_shape(kind, pad, other) for kind, _, pad, other in specs]
    outs = pl.pallas_call(
        body, name=name, grid=(1,),
        in_specs=[whole(a.shape) for a in shards] + [ANY],
        out_specs=[whole(s) for s in packed_shapes] + [ANY] * nw,
        out_shape=[_sds(s, BF16) for s in packed_shapes]
        + [_sds(_shard_shape(kind, N_DEV * pad, other), BF16) for kind, _, pad, other in specs],
        scratch_shapes=[pltpu.SemaphoreType.DMA((nw,))],
        compiler_params=_params("arbitrary"),
    )(*shards, after)
    return dict(zip(names, outs[:nw])), dict(zip(names, outs[nw:]))


def _hbm(a):
    return pltpu.with_memory_space_constraint(a, pltpu.HBM)


def _split_start(name, n, body_copies, sources, lands, after):
    arrays = list(sources) + list(lands)
    ns, na = len(sources), len(arrays)

    def body(*refs):
        src, land = refs[:ns], refs[ns:na]
        send, recv = refs[na + 1], refs[na + 2]
        token = refs[-1]
        body_copies(src, land, send, recv)
        token[...] = jnp.zeros_like(token)

    out = pl.pallas_call(
        body, name=name,
        out_shape=(pltpu.SemaphoreType.DMA((n,)), pltpu.SemaphoreType.DMA((n,)),
                   *[pltpu.HBM(a.shape, a.dtype) for a in arrays], _sds((8, 128))),
        in_specs=[HBM] * na + [ANY], out_specs=(SEM, SEM, *[HBM] * na, VMEM),
        input_output_aliases={i: 2 + i for i in range(na)},
        compiler_params=pltpu.CompilerParams(has_side_effects=EFFECT),
    )(*[_hbm(a) for a in arrays], after)
    return out[0], out[1], out[2:2 + ns], out[2 + ns:2 + na], out[-1]


def _split_wait(name, n, seven_of, send, recv, sources, lands, after, keep_sources=False):
    arrays = list(sources) + list(lands)
    ns, na = len(sources), len(arrays)

    def body(*refs):
        land = refs[ns:na]
        send_ref, recv_ref = refs[na], refs[na + 1]
        myself = (lax.axis_index("x"), lax.axis_index("y"), lax.axis_index("c"))
        for w in range(n):
            seven = seven_of(w, land[w])
            copy = pltpu.make_async_remote_copy(
                src_ref=seven, dst_ref=seven, send_sem=send_ref.at[w], recv_sem=recv_ref.at[w],
                device_id=myself, device_id_type=MESH)
            copy.wait_send()
            copy.wait_recv()

    afters = tuple(after) if isinstance(after, (tuple, list)) else (after,)
    out = pl.pallas_call(
        body, name=name,
        out_shape=[pltpu.HBM(a.shape, a.dtype) for a in arrays],
        in_specs=[HBM] * na + [SEM, SEM] + [ANY] * len(afters), out_specs=[HBM] * na,
        input_output_aliases={i: i for i in range(na)},
        compiler_params=pltpu.CompilerParams(has_side_effects=EFFECT),
    )(*arrays, send, recv, *afters)
    return out if keep_sources else out[ns:]


_ALL_PEERS = (1, 2, 3, 4, 5, 6, 7)
_NEAR_PEERS = (1, 2, 4, 6)
_FAR_CHIPS = (2, 4, 6)


def _gather_start(stage, names, packed, full, after, peers=_ALL_PEERS):
    def copies(src, land, send, recv):
        me = _my_index()
        for k in range(N_DEV):
            @pl.when(me == k)
            def _():
                for w, name in enumerate(names):
                    kind, _, pad, _ = _SPEC[name]
                    dst = _window(land[w], kind, k * pad, pad)
                    for mask in peers:
                        pltpu.make_async_remote_copy(
                            src_ref=src[w], dst_ref=dst, send_sem=send.at[w],
                            recv_sem=recv.at[w], device_id=_device_tuple(k ^ mask),
                            device_id_type=MESH).start()

    return _split_start(f"gather_start{stage}", len(names), copies,
                        [packed[n] for n in names], [full[n] for n in names], after)


def _gather_wait(stage, names, started, after, count=N_DEV - 1):
    send, recv, src, land, _ = started

    def bytes_of(w, ref):
        kind, _, pad, _ = _SPEC[names[w]]
        return _window(ref, kind, 0, count * pad)

    return dict(zip(names, _split_wait(f"gather_wait{stage}", len(names), bytes_of,
                                       send, recv, src, land, after)))


def _relay_start(stage, names, full, after):
    def copies(_, land, send, recv):
        me = _my_index()
        for k in range(N_DEV):
            @pl.when(me == k)
            def _():
                for w, name in enumerate(names):
                    kind, _, pad, _ = _SPEC[name]
                    for mask in _FAR_CHIPS:
                        win = _window(land[w], kind, (k ^ mask) * pad, pad)
                        pltpu.make_async_remote_copy(
                            src_ref=win, dst_ref=win, send_sem=send.at[w], recv_sem=recv.at[w],
                            device_id=_device_tuple(k ^ 1), device_id_type=MESH).start()

    return _split_start(f"relay_start{stage}", len(names), copies, [],
                        [full[n] for n in names], after)


def _scatter_start(stage, names, grads, after):
    def copies(src, land, send, recv):
        me = _my_index()
        for k in range(N_DEV):
            @pl.when(me != k)
            def _():
                slot = lax.rem(me + (N_DEV - 1 - k), N_DEV)
                for w, name in enumerate(names):
                    kind, _, pad, _ = _SPEC[name]
                    pltpu.make_async_remote_copy(
                        src_ref=_window(src[w], kind, k * pad, pad), dst_ref=land[w].at[slot],
                        send_sem=send.at[w], recv_sem=recv.at[w],
                        device_id=_device_tuple(k), device_id_type=MESH).start()

    lands = [lax.empty((N_DEV - 1,) + _shard_shape(_SPEC[m][0], _SPEC[m][2], _SPEC[m][3]), BF16)
             for m in names]
    return _split_start(f"scatter_start{stage}", len(names), copies, grads, lands, after)


def _scatter_wait(stage, names, started, after):
    send, recv, src, land, _ = started
    n = len(names)
    out = _split_wait(f"scatter_wait{stage}", n, lambda w, ref: ref, send, recv, src, land, after,
                      keep_sources=True)
    return dict(zip(names, out[:n])), dict(zip(names, out[n:]))


N_CHIPS = N_DEV // 2


def _pair_start(stage, names, grads, after):
    def copies(src, land, send, recv):
        me = _my_index()
        for k in range(N_DEV):
            @pl.when(me == k)
            def _():
                for w, name in enumerate(names):
                    kind, _, pad, _ = _SPEC[name]
                    for chip in range(N_CHIPS):
                        j = 2 * chip + ((k ^ 1) & 1)
                        pltpu.make_async_remote_copy(
                            src_ref=_window(src[w], kind, j * pad, pad), dst_ref=land[w].at[chip],
                            send_sem=send.at[w], recv_sem=recv.at[w],
                            device_id=_device_tuple(k ^ 1), device_id_type=MESH).start()

    lands = [lax.empty((N_CHIPS,) + _shard_shape(_SPEC[m][0], _SPEC[m][2], _SPEC[m][3]), BF16)
             for m in names]
    return _split_start(f"pair_start{stage}", len(names), copies, grads, lands, after)


def _pair_sum(dw_full, pair, *, pad, name):
    other = dw_full.shape[1]

    def body(own_ref, pair_ref, out_ref):
        out_ref[0] = _bf(own_ref[...].astype(F32) + pair_ref[0].astype(F32))

    slot = pl.BlockSpec((1, pad, other), lambda q: (q, 0, 0))
    return pl.pallas_call(
        body, name=name, grid=(N_CHIPS,),
        in_specs=[pl.BlockSpec((pad, other), lambda q: (2 * q + lax.axis_index("c"), 0)), slot],
        out_specs=slot, out_shape=_sds((N_CHIPS, pad, other), BF16),
        compiler_params=_params("arbitrary"),
    )(dw_full, pair)


def _chip_start(stage, names, sums, after):
    def copies(src, land, send, recv):
        me = _my_index()
        my_chip = lax.shift_right_logical(me, 1)
        for k in range(N_DEV):
            @pl.when((me != k) & (((me ^ k) & 1) == 0))
            def _():
                slot = lax.rem(my_chip + (N_CHIPS - 1 - k // 2), N_CHIPS)
                for w in range(len(names)):
                    pltpu.make_async_remote_copy(
                        src_ref=src[w].at[k // 2], dst_ref=land[w].at[slot],
                        send_sem=send.at[w], recv_sem=recv.at[w],
                        device_id=_device_tuple(k), device_id_type=MESH).start()

    lands = [lax.empty((N_CHIPS - 1,) + a.shape[1:], BF16) for a in sums]
    return _split_start(f"chip_start{stage}", len(names), copies, sums, lands, after)


def _adamw_chip(w, m, v, land, sums, *, name):
    shape = w.shape

    def body(w_ref, m_ref, v_ref, land_ref, own_ref, *outs):
        rows = pl.ds(0, shape[0])
        grad = own_ref[0, rows, :].astype(F32)
        for s in range(N_CHIPS - 1):
            grad = grad + land_ref[s, rows, :].astype(F32)
        _adam_update(w_ref, m_ref, v_ref, grad, *outs)

    whole = lambda a: pl.BlockSpec(a.shape, lambda i: (0,) * a.ndim)
    own = pl.BlockSpec((1,) + sums.shape[1:],
                       lambda i: (2 * lax.axis_index("x") + lax.axis_index("y"), 0, 0))
    return pl.pallas_call(
        body, name=name, grid=(1,),
        in_specs=[whole(w), whole(m), whole(v), whole(land), own],
        out_specs=[whole(w)] * 4, out_shape=[_sds(shape)] * 4,
        compiler_params=_params("arbitrary"),
    )(w, m, v, land, sums)


def _allreduce_small(small, after):
    shape = small.shape

    def body(in_ref, _after, out_ref, gath, send, recv):
        me = _my_index()
        for k in range(N_DEV):
            @pl.when(me != k)
            def _():
                pltpu.make_async_remote_copy(
                    src_ref=in_ref, dst_ref=gath.at[me], send_sem=send, recv_sem=recv,
                    device_id=_device_tuple(k), device_id_type=MESH).start()

            @pl.when(me == k)
            def _():
                gath[k] = in_ref[...]
        seven = gath.at[pl.ds(0, N_DEV - 1)]
        pltpu.make_async_remote_copy(
            src_ref=seven, dst_ref=seven, send_sem=send, recv_sem=recv,
            device_id=_device_tuple(0), device_id_type=MESH).wait()
        total = gath[0]
        for s in range(1, N_DEV):
            total = total + gath[s]
        out_ref[...] = total

    return pl.pallas_call(
        body, name="allreduce_small",
        in_specs=[VMEM, ANY], out_specs=VMEM, out_shape=_sds(shape),
        scratch_shapes=[pltpu.VMEM((N_DEV,) + shape, F32),
                        pltpu.SemaphoreType.DMA, pltpu.SemaphoreType.DMA],
    )(small, after)


def _adam_update(w_ref, m_ref, v_ref, grad, grad_ref, delta_ref, nm_ref, nv_ref):
    new_m = ADAM_B1 * m_ref[...] + (1.0 - ADAM_B1) * grad
    new_v = ADAM_B2 * v_ref[...] + (1.0 - ADAM_B2) * (grad * grad)
    m_hat = new_m / (1.0 - ADAM_B1 ** ADAM_STEP)
    v_hat = new_v / (1.0 - ADAM_B2 ** ADAM_STEP)
    grad_ref[...] = grad
    delta_ref[...] = -ADAM_LR * (m_hat / (jnp.sqrt(v_hat) + ADAM_EPS) + ADAM_WD * w_ref[...])
    nm_ref[...] = new_m
    nv_ref[...] = new_v


def _adamw(w, m, v, g, *, name):
    def body(w_ref, m_ref, v_ref, g_ref, *outs):
        _adam_update(w_ref, m_ref, v_ref, g_ref[...], *outs)

    whole = pl.BlockSpec(w.shape, lambda i: (0,) * w.ndim)
    return pl.pallas_call(
        body, name=name, grid=(1,), in_specs=[whole] * 4, out_specs=[whole] * 4,
        out_shape=[_sds(w.shape)] * 4, compiler_params=_params("arbitrary"),
    )(w, m, v, g)


def _adamw_gains(small, params):
    n = len(params)

    def body(small_ref, *refs):
        ins, outs = refs[:3 * n], refs[3 * n:]
        for r in range(n):
            width = ins[3 * r].shape[1]
            if width == D_MODEL:
                grad = small_ref[pl.ds(r, 1), :]
            else:
                grad = small_ref[pl.ds(len(_GAINS), 1), pl.ds((r - len(_GAINS)) * width, width)]
            _adam_update(*ins[3 * r:3 * r + 3], grad, *outs[4 * r:4 * r + 4])

    whole = lambda a: pl.BlockSpec(a.shape, lambda i: (0, 0))
    flat = [a for group in params for a in group]
    return pl.pallas_call(
        body, name="adamw_gains", grid=(1,),
        in_specs=[whole(small)] + [whole(a) for a in flat],
        out_specs=[whole(w) for w, _, _ in params for _ in range(4)],
        out_shape=[_sds(w.shape) for w, _, _ in params for _ in range(4)],
        compiler_params=_params("arbitrary"),
    )(small, *flat)


def _adamw_shard(w, m, v, land, dw_full, *, kind, pad, name):
    shape = w.shape
    other = shape[0] if kind == "col" else shape[1]

    def body(w_ref, m_ref, v_ref, land_ref, own_ref, *outs):
        valid = ((slice(None), pl.ds(0, shape[1])) if kind == "col"
                 else (pl.ds(0, shape[0]), slice(None)))
        grad = own_ref[valid].astype(F32)
        for s in range(N_DEV - 1):
            grad = grad + land_ref[(s,) + valid].astype(F32)
        _adam_update(w_ref, m_ref, v_ref, grad, *outs)

    whole = lambda a: pl.BlockSpec(a.shape, lambda i: (0,) * a.ndim)
    own = pl.BlockSpec(_shard_shape(kind, pad, other),
                       (lambda i: (0, _my_index())) if kind == "col" else (lambda i: (_my_index(), 0)))
    return pl.pallas_call(
        body, name=name, grid=(1,),
        in_specs=[whole(w), whole(m), whole(v), whole(land), own],
        out_specs=[whole(w)] * 4, out_shape=[_sds(shape)] * 4,
        compiler_params=_params("arbitrary"),
    )(w, m, v, land, dw_full)


_GAINS = ("ffn1_pre", "ffn1_post", "mix_pre", "mix_post", "ffn2_pre", "ffn2_post", "ple_post")
_SMALL_ROWS = 16


def _stack_gains(get):
    return jnp.concatenate([get(n) for n in _GAINS]
                           + [jnp.concatenate([get("out_sb"), get("out_ch")], axis=1)], axis=0)


def kernel(x, p, g_ffn1_pre, g_ffn1_post, w_ffn1_gate, w_ffn1_up, w_ffn1_down, g_mix_pre, g_mix_post, w_in, g_out_sb, g_out_ch, rel_bias, w_out, g_ffn2_pre, g_ffn2_post, w_ffn2_gate, w_ffn2_up, w_ffn2_down, w_ple_proj, w_ple_gate, g_ple_post, loss_target, m_g_ffn1_pre, m_g_ffn1_post, m_w_ffn1_gate, m_w_ffn1_up, m_w_ffn1_down, m_g_mix_pre, m_g_mix_post, m_w_in, m_g_out_sb, m_g_out_ch, m_rel_bias, m_w_out, m_g_ffn2_pre, m_g_ffn2_post, m_w_ffn2_gate, m_w_ffn2_up, m_w_ffn2_down, m_w_ple_proj, m_w_ple_gate, m_g_ple_post, v_g_ffn1_pre, v_g_ffn1_post, v_w_ffn1_gate, v_w_ffn1_up, v_w_ffn1_down, v_g_mix_pre, v_g_mix_post, v_w_in, v_g_out_sb, v_g_out_ch, v_rel_bias, v_w_out, v_g_ffn2_pre, v_g_ffn2_post, v_w_ffn2_gate, v_w_ffn2_up, v_w_ffn2_down, v_w_ple_proj, v_w_ple_gate, v_g_ple_post):
    given = dict(locals())
    wnames = [n for n, *_ in _WEIGHTS]

    def shard(prefix, n):
        a = given[prefix + "w_" + n][0]
        return a.T if n in _TRANSPOSED else a

    anchor = x[0]
    first = _GATHER_STAGES[0]
    packed, full = _pack_weights(first, [shard("", n) for n in first], anchor, name="pack_first")
    two_level = (0, 2)
    gathers = {}

    def start_stage(stage, after):
        peers = _NEAR_PEERS if stage in two_level else _ALL_PEERS
        gathers[stage] = _gather_start(stage, _GATHER_STAGES[stage], packed, full, after,
                                       peers=peers)

    start_stage(0, anchor)
    rest = [n for n in wnames if n not in first]
    packed_rest, full_rest = _pack_weights(rest, [shard("", n) for n in rest], gathers[0][-1],
                                           name="pack_rest")
    packed.update(packed_rest)
    full.update(full_rest)

    relays = {}

    def first_level(stage, after):
        names = _GATHER_STAGES[stage]
        last_stage = stage + 1 == len(_GATHER_STAGES)
        count = len(_NEAR_PEERS) if stage in two_level else N_DEV - 1
        if stage == 0:
            after = (after, packed_rest[rest[0]])
        ws = _gather_wait(stage, names, gathers[stage], after, count=count)
        if not last_stage:
            start_stage(stage + 1, ws[names[0]])
        if stage in two_level:
            relays[stage] = _relay_start(stage, names, ws,
                                         anchor if last_stage else gathers[stage + 1][-1])
            return ws, relays[stage][-1]
        return ws, None if last_stage else gathers[stage + 1][-1]

    def weights_early(stage, after):
        return first_level(stage, after)[1][:1, :1]

    def weights_for(stage, after):
        names = _GATHER_STAGES[stage]
        ws, token = (None, None) if stage in relays else first_level(stage, after)
        if stage in relays:
            relay = relays[stage]
            ws = _gather_wait(f"{stage}r", names, relay, after, count=len(_FAR_CHIPS))
            token = None if stage + 1 == len(_GATHER_STAGES) else gathers[stage + 1][-1]
        return ws, jnp.zeros((1, 1), F32) if token is None else token[:1, :1]

    scatters = {}

    last = len(_SCATTER_STAGES) - 1

    def grads_done(stage, grads):
        names = _SCATTER_STAGES[stage]
        start = _pair_start if stage == last else _scatter_start
        scatters[stage] = start(stage, names, [grads[n] for n in names], anchor)
        return scatters[stage][-1][:1, :1]

    gains = {n: given["g_" + n] for n in _GAINS + ("out_sb", "out_ch")}
    fvec = _rel_bias_to_fvec(rel_bias[0])
    loss, dx, dg, dbias = _local_step(x[0], p[0, 0], loss_target[0], gains,
                                      weights_for, grads_done, fvec, weights_early)

    results = {}

    def finish(stage, after):
        names = _SCATTER_STAGES[stage]
        dws, lands = _scatter_wait(stage, names, scatters[stage], after)
        for n in names:
            kind, _, pad, _ = _SPEC[n]
            out = _adamw_shard(shard("", n), shard("m_", n), shard("v_", n), lands[n], dws[n],
                               kind=kind, pad=pad, name="adamw_" + n)
            results["w_" + n] = [a.T for a in out] if n in _TRANSPOSED else out
        return results["w_" + names[-1]][0]

    names = _SCATTER_STAGES[last]
    whole = lambda w, ref: ref
    send, recv, src, land, _ = scatters[last]
    out = _split_wait(f"pair_wait{last}", len(names), whole, send, recv, src, land, dx,
                      keep_sources=True)
    sums = [_pair_sum(dwf, pair, pad=_SPEC[n][2], name="pair_sum_" + n)
            for n, dwf, pair in zip(names, out[:len(names)], out[len(names):])]
    send, recv, src, land, after = _chip_start(last, names, sums, anchor)
    for stage in range(last):
        after = finish(stage, after)
    dfvec = _bias_grad(dbias, after, name="bias_grad")
    loss_col = jnp.pad(loss[:, :1], ((0, N_DEV - 1), (0, D_MODEL - CH_WIN - 1)))
    dfv = jnp.concatenate([dfvec[:, 0, :], loss_col], axis=1)
    small = _allreduce_small(jnp.concatenate([_stack_gains(lambda n: dg[n]), dfv], axis=0), after)
    gain_names = _GAINS + ("out_sb", "out_ch")
    gain_out = _adamw_gains(small, [(given["g_" + n], given["m_g_" + n], given["v_g_" + n])
                                    for n in gain_names])
    for r, n in enumerate(gain_names):
        results["g_" + n] = gain_out[4 * r:4 * r + 4]
    d_rel = _fvec_grad_to_rel_bias(small[N_DEV:, :CH_WIN].reshape(N_DEV, 1, CH_WIN))
    results["rel_bias"] = _adamw(rel_bias[0], m_rel_bias[0], v_rel_bias[0], d_rel,
                                 name="adamw_rel_bias")
    out = _split_wait(f"chip_wait{last}", len(names), whole, send, recv, src, land,
                      results["rel_bias"][0], keep_sources=True)
    for n, own, landed in zip(names, out[:len(names)], out[len(names):]):
        res = _adamw_chip(shard("", n), shard("m_", n), shard("v_", n), landed, own,
                          name="adamw_" + n)
        results["w_" + n] = [a.T for a in res] if n in _TRANSPOSED else res

    order = ("g_ffn1_pre", "g_ffn1_post", "w_ffn1_gate", "w_ffn1_up", "w_ffn1_down",
             "g_mix_pre", "g_mix_post", "w_in", "g_out_sb", "g_out_ch", "rel_bias", "w_out",
             "g_ffn2_pre", "g_ffn2_post", "w_ffn2_gate", "w_ffn2_up", "w_ffn2_down",
             "w_ple_proj", "w_ple_gate", "g_ple_post")

    def leaf(name, idx):
        a = results[name][idx]
        return a if name.startswith("g_") else a[None]

    total_loss = small[N_DEV, CH_WIN]
    return (total_loss, dx[None],
            *[leaf(n, 0) for n in order], *[leaf(n, 1) for n in order],
            *[leaf(n, 2) for n in order], *[leaf(n, 3) for n in order])
```

```python
import jax
import jax.numpy as jnp
from jax import lax
from jax.experimental import pallas as pl
from jax.experimental.pallas import tpu as pltpu

F32 = jnp.float32
BF16 = jnp.bfloat16

N_DEV = 8
D_MODEL = 1024
D_FF = 2816
FF_SHARD = D_FF // N_DEV
FF_SHARD_PAD = 384
D_FF_PAD = FF_SHARD_PAD * N_DEV
QKV_WIDTH = 3 * D_MODEL
QKV_SHARD = QKV_WIDTH // N_DEV
PLE_DIM = 256
ROW_SHARD = D_MODEL // N_DEV
HEAD_DIM = 64
PAIR = 2 * HEAD_DIM
N_PAIRS = 4
CHUNK = 64
LOOKBACK = 8
REL_CLIP = 128
N_REL = 2 * REL_CLIP + 1
CH_QB = 256
CH_LOOK = LOOKBACK * CHUNK
CH_WIN = CH_LOOK + CH_QB
SB_QB = 512
SB_KB = 256
SB_GROUP = 2
SB_LANES = tuple(slice(g * 128, (g + 1) * 128) for g in range(SB_GROUP))
EPS = 1e-6
NEG_INF = -1e30
ATT_SCALE = HEAD_DIM ** -0.5
ADAM_LR = 0.001
ADAM_B1 = 0.9
ADAM_B2 = 0.999
ADAM_EPS = 1e-08
ADAM_WD = 0.01
ADAM_STEP = 10
VMEM_LIMIT_BYTES = 48 * 1024 * 1024
MESH = pl.DeviceIdType.MESH

ANY = pl.BlockSpec(memory_space=pl.ANY)
VMEM = pl.BlockSpec(memory_space=pltpu.VMEM)


def _params(*sem):
    return pltpu.CompilerParams(dimension_semantics=sem or None,
                                vmem_limit_bytes=VMEM_LIMIT_BYTES)


def _sds(shape, dtype=F32):
    return jax.ShapeDtypeStruct(shape, dtype)


def _bf(x):
    return x.astype(BF16)


def _dot(a, b):
    return jnp.dot(_bf(a), _bf(b), preferred_element_type=F32)


def _dot_nt(a, b):
    return lax.dot_general(_bf(a), _bf(b), (((1,), (1,)), ((), ())),
                           preferred_element_type=F32)


def _dot_tn(a, b):
    return lax.dot_general(_bf(a), _bf(b), (((0,), (0,)), ((), ())),
                           preferred_element_type=F32)


def _sigmoid(x):
    return 1.0 / (1.0 + jnp.exp(-x))


def _softplus(x):
    return jnp.maximum(x, 0.0) + jnp.log(1.0 + jnp.exp(-jnp.abs(x)))


def _rstd(x):
    return lax.rsqrt(jnp.mean(x * x, axis=-1, keepdims=True) + EPS)


def _rms(x, g):
    return x * _rstd(x) * g


def _rms_bwd(dy, x, g):
    r = _rstd(x)
    w = dy * g
    dx = r * (w - x * (r * r) * jnp.mean(w * x, axis=-1, keepdims=True))
    dg = jnp.sum(dy * (x * r), axis=0, keepdims=True)
    return dx, dg


def _head_masks():
    lane = lax.broadcasted_iota(jnp.int32, (1, PAIR), 1)
    return lane < HEAD_DIM, lane >= HEAD_DIM


def _ffn_fwd(x, g_pre, g_post, wg, wu, wd, *, name):
    t = x.shape[0]
    tm, tj = 512, 1024
    ni, nj = t // tm, D_FF_PAD // tj

    def body(x_ref, gpre_ref, gpost_ref, wg_ref, wu_ref, wd_ref,
             h_ref, n_ref, a_ref, b_ref, f_ref, acc_ref):
        j = pl.program_id(1)

        @pl.when(j == 0)
        def _():
            n_ref[...] = _bf(_rms(x_ref[...], gpre_ref[...]))
            acc_ref[...] = jnp.zeros_like(acc_ref)

        n = n_ref[...]
        a = _dot_nt(n, wg_ref[...])
        b = _dot_nt(n, wu_ref[...])
        a_ref[...] = a
        b_ref[...] = b
        hmid = a * _sigmoid(a) * b
        acc_ref[...] += jnp.dot(_bf(hmid), wd_ref[...], preferred_element_type=F32)

        @pl.when(j == nj - 1)
        def _():
            f = acc_ref[...]
            f_ref[...] = f
            h_ref[...] = x_ref[...] + 0.5 * _rms(f, gpost_ref[...])

    row = pl.BlockSpec((tm, D_MODEL), lambda i, j: (i, 0))
    gain = pl.BlockSpec((1, D_MODEL), lambda i, j: (0, 0))
    col = pl.BlockSpec((tm, tj), lambda i, j: (i, j))
    wtile = pl.BlockSpec((tj, D_MODEL), lambda i, j: (j, 0))
    return pl.pallas_call(
        body, name=name, grid=(ni, nj),
        in_specs=[row, gain, gain, wtile, wtile, wtile],
        out_specs=[row, row, col, col, row],
        out_shape=[_sds((t, D_MODEL)), _sds((t, D_MODEL), BF16),
                   _sds((t, D_FF_PAD)), _sds((t, D_FF_PAD)), _sds((t, D_MODEL))],
        scratch_shapes=[pltpu.VMEM((tm, D_MODEL), F32)],
        compiler_params=_params("arbitrary", "arbitrary"),
    )(x, g_pre, g_post, wg, wu, wd)


def _ffn_bwd(n, df, a, b, wg, wu, wd, *, name):
    t = n.shape[0]
    tj, tm, ts = 256, t, 512
    nj, ni, ns = D_FF_PAD // tj, t // tm, tm // ts

    def body(n_hbm, df_hbm, a_ref, b_ref, wg_ref, wu_ref, wd_ref,
             dwg_ref, dwu_ref, dwd_ref, dn_hbm,
             n_v, df_v, dn_v, ag, au, ad, sem):
        j, i = pl.program_id(0), pl.program_id(1)

        @pl.when((j == 0) & (i == 0))
        def _():
            c1 = pltpu.make_async_copy(n_hbm, n_v, sem.at[0])
            c2 = pltpu.make_async_copy(df_hbm, df_v, sem.at[1])
            c1.start()
            c2.start()
            dn_v[...] = jnp.zeros_like(dn_v)
            c1.wait()
            c2.wait()

        @pl.when(i == 0)
        def _():
            ag[...] = jnp.zeros_like(ag)
            au[...] = jnp.zeros_like(au)
            ad[...] = jnp.zeros_like(ad)

        wgj, wuj, wdj = wg_ref[...], wu_ref[...], wd_ref[...]
        for s in range(ns):
            local = pl.ds(s * ts, ts)
            rows = pl.ds(pl.multiple_of(i * tm + s * ts, ts), ts)
            av, bv = a_ref[local, :], b_ref[local, :]
            sig = _sigmoid(av)
            silu = av * sig
            dfr = df_v[rows, :]
            nr = n_v[rows, :]
            dhmid = _dot_nt(dfr, wdj)
            da = dhmid * bv * (sig * (1.0 + av * (1.0 - sig)))
            db = dhmid * silu
            ad[...] += _dot_tn(silu * bv, dfr)
            ag[...] += _dot_tn(da, nr)
            au[...] += _dot_tn(db, nr)
            dn_v[rows, :] += _dot(da, wgj) + _dot(db, wuj)

        @pl.when(i == ni - 1)
        def _():
            dwg_ref[...] = _bf(ag[...])
            dwu_ref[...] = _bf(au[...])
            dwd_ref[...] = _bf(ad[...])

        @pl.when((j == nj - 1) & (i == ni - 1))
        def _():
            c = pltpu.make_async_copy(dn_v, dn_hbm, sem.at[0])
            c.start()
            c.wait()

    roww = pl.BlockSpec((tj, D_MODEL), lambda j, i: (j, 0))
    act = pl.BlockSpec((tm, tj), lambda j, i: (i, j))
    return pl.pallas_call(
        body, name=name, grid=(nj, ni),
        in_specs=[ANY, ANY, act, act, roww, roww, roww],
        out_specs=[roww, roww, roww, ANY],
        out_shape=[_sds((D_FF_PAD, D_MODEL), BF16)] * 3 + [_sds((t, D_MODEL))],
        scratch_shapes=[pltpu.VMEM((t, D_MODEL), BF16), pltpu.VMEM((t, D_MODEL), BF16),
                        pltpu.VMEM((t, D_MODEL), F32)]
        + [pltpu.VMEM((tj, D_MODEL), F32)] * 3 + [pltpu.SemaphoreType.DMA((2,))],
        compiler_params=_params("arbitrary", "arbitrary"),
    )(n, df, a, b, wg, wu, wd)


def _junction(dres, pre=None, post=None, *, name):
    t = dres.shape[0]
    tm = 512
    ni = t // tm
    n_in = 1 + (3 if pre else 0) + (2 if post else 0)
    coef = post[2] if post else None

    def body(*refs):
        ins, outs = list(refs[:n_in]), list(refs[n_in:])
        i = pl.program_id(0)
        dh = ins.pop(0)[...]
        if pre:
            dn_ref, x_ref, gpre_ref = ins.pop(0), ins.pop(0), ins.pop(0)
            dh_ref, dgpre_ref = outs.pop(0), outs.pop(0)
            dx, dg = _rms_bwd(dn_ref[...], x_ref[...], gpre_ref[...])
            dh = dh + dx
            dh_ref[...] = dh

            @pl.when(i == 0)
            def _():
                dgpre_ref[...] = jnp.zeros_like(dgpre_ref)
            dgpre_ref[...] += dg
        if post:
            f_ref, gpost_ref = ins.pop(0), ins.pop(0)
            df_ref, dgpost_ref = outs.pop(0), outs.pop(0)
            df, dg = _rms_bwd(coef * dh, f_ref[...], gpost_ref[...])
            df_ref[...] = _bf(df)

            @pl.when(i == 0)
            def _():
                dgpost_ref[...] = jnp.zeros_like(dgpost_ref)
            dgpost_ref[...] += dg

    row = pl.BlockSpec((tm, D_MODEL), lambda i: (i, 0))
    gain = pl.BlockSpec((1, D_MODEL), lambda i: (0, 0))
    args, in_specs, out_specs, out_shape = [dres], [row], [], []
    if pre:
        args += list(pre)
        in_specs += [row, row, gain]
        out_specs += [row, gain]
        out_shape += [_sds((t, D_MODEL)), _sds((1, D_MODEL))]
    if post:
        args += [post[0], post[1]]
        in_specs += [row, gain]
        out_specs += [row, gain]
        out_shape += [_sds((t, D_MODEL), BF16), _sds((1, D_MODEL))]
    return pl.pallas_call(
        body, name=name, grid=(ni,), in_specs=in_specs, out_specs=out_specs,
        out_shape=out_shape, compiler_params=_params("arbitrary"),
    )(*args)


def _qkv_fwd(h, g, win, *, name):
    t = h.shape[0]
    tm, tn = min(1024, t), 1024
    ni, nj = t // tm, QKV_WIDTH // tn

    def body(h_ref, g_ref, w_ref, qkv_ref, u_ref):
        @pl.when(pl.program_id(1) == 0)
        def _():
            u_ref[...] = _bf(_rms(h_ref[...], g_ref[...]))
        qkv_ref[...] = jnp.dot(u_ref[...], w_ref[...], preferred_element_type=F32)

    row = pl.BlockSpec((tm, D_MODEL), lambda i, j: (i, 0))
    return pl.pallas_call(
        body, name=name, grid=(ni, nj),
        in_specs=[row, pl.BlockSpec((1, D_MODEL), lambda i, j: (0, 0)),
                  pl.BlockSpec((D_MODEL, tn), lambda i, j: (0, j))],
        out_specs=[pl.BlockSpec((tm, tn), lambda i, j: (i, j)), row],
        out_shape=[_sds((t, QKV_WIDTH)), _sds((t, D_MODEL), BF16)],
        compiler_params=_params("arbitrary", "arbitrary"),
    )(h, g, win)


def _qkv_bwd(dq, dk, dv, u, win, *, name):
    t = u.shape[0]
    tn, ts = 512, 512
    nj, ns = QKV_WIDTH // tn, t // ts

    def body(dq_ref, dk_ref, dv_ref, u_ref, w_ref, dw_ref, du_hbm, du_v, acc_ref, sem):
        j = pl.program_id(0)

        @pl.when(j == 0)
        def _():
            du_v[...] = jnp.zeros_like(du_v)

        wj = w_ref[...]
        for role, d_ref in enumerate((dq_ref, dk_ref, dv_ref)):
            @pl.when(j % 3 == role)
            def _():
                acc_ref[...] = jnp.zeros_like(acc_ref)
                for s in range(ns):
                    rows = pl.ds(s * ts, ts)
                    dcol = d_ref[rows, :]
                    acc_ref[...] += _dot_tn(u_ref[rows, :], dcol)
                    du_v[rows, :] += _dot_nt(dcol, wj)
                dw_ref[...] = _bf(acc_ref[...])

        @pl.when(j == nj - 1)
        def _():
            c = pltpu.make_async_copy(du_v, du_hbm, sem)
            c.start()
            c.wait()

    colw = pl.BlockSpec((D_MODEL, tn), lambda j: (0, j))
    grp = pl.BlockSpec((t, tn), lambda j: (0, j // 3))
    return pl.pallas_call(
        body, name=name, grid=(nj,),
        in_specs=[grp, grp, grp, pl.BlockSpec((t, D_MODEL), lambda j: (0, 0)), colw],
        out_specs=[colw, ANY],
        out_shape=[_sds((D_MODEL, QKV_WIDTH), BF16), _sds((t, D_MODEL))],
        scratch_shapes=[pltpu.VMEM((t, D_MODEL), F32), pltpu.VMEM((D_MODEL, tn), F32),
                        pltpu.SemaphoreType.DMA],
        compiler_params=_params("arbitrary"),
    )(dq, dk, dv, u, win)


def _sb_stack(x):
    lo, hi = _head_masks()
    return jnp.concatenate([jnp.where(lo, x, 0.0), jnp.where(hi, x, 0.0)], axis=0)


def _sb_unstack(x2, blk):
    return jnp.where(_head_masks()[0], x2[:blk], x2[blk:])


def _sb_rows_from(x2, blk, r0):
    return x2 if r0 == 0 else jnp.concatenate([x2[r0:blk], x2[blk + r0:]], axis=0)


def _sb_rows_merge(full2, sub2, blk, r0):
    if r0 == 0:
        return sub2
    rows = blk - r0
    return jnp.concatenate([full2[:r0], sub2[:rows], full2[blk:blk + r0], sub2[rows:]], axis=0)


def _sb_mask(qb, kb, offset):
    r = lax.broadcasted_iota(jnp.int32, (2 * qb, kb), 0) & (qb - 1)
    c = lax.broadcasted_iota(jnp.int32, (2 * qb, kb), 1) + offset
    return c < r


def _tri(n, keep):
    r = lax.broadcasted_iota(jnp.int32, (n, n), 0)
    c = lax.broadcasted_iota(jnp.int32, (n, n), 1)
    return jnp.where(keep(r, c), 1.0, 0.0).astype(BF16)


def _cumsum01(x, u):
    m = x.shape[0]
    hi = _bf(x)
    lo = _bf(x - hi.astype(F32))
    both = jnp.dot(jnp.concatenate([hi, lo], axis=0), u, preferred_element_type=F32)
    return both[:m] + both[m:]


def _sb_fwd(qkv, *, name):
    t = qkv.shape[0]
    blk, kb = min(SB_QB, t), SB_KB
    ni, per = t // blk, blk // kb

    def body(q_ref, k_ref, v_ref, o_ref, ltot_ref):
        i = pl.program_id(1)
        u_after = _tri(kb, lambda r, c: r > c)
        q2 = [_bf(_sb_stack(q_ref[:, lanes] * ATT_SCALE)) for lanes in SB_LANES]

        def tile(g, k0, mask, acc, c_l):
            kj = k_ref[pl.ds(k0, kb), SB_LANES[g]]
            vj = v_ref[pl.ds(k0, kb), SB_LANES[g]]
            z = _dot_nt(q2[g], kj)
            sp = _softplus(z)
            lf = -sp if mask is None else jnp.where(mask, -sp, 0.0)
            a = jnp.exp(z - sp + _cumsum01(lf, u_after) + c_l)
            if mask is not None:
                a = jnp.where(mask, a, 0.0)
            return acc + _dot(a, vj), c_l + jnp.sum(lf, axis=1, keepdims=True)

        def tiles(k0, mask, carry):
            return tuple(tile(g, k0, mask, *carry[g]) for g in range(SB_GROUP))

        carry = ((jnp.zeros((2 * blk, PAIR), F32), jnp.zeros((2 * blk, 1), F32)),) * SB_GROUP
        for d in reversed(range(per)):
            carry = tiles(pl.multiple_of(i * blk + d * kb, kb), _sb_mask(blk, kb, d * kb), carry)
        carry = lax.fori_loop(
            1, per * i + 1,
            lambda jj, c: tiles(pl.multiple_of((per * i - jj) * kb, kb), None, c), carry)
        for g, (acc, c_l) in enumerate(carry):
            o_ref[:, SB_LANES[g]] = _sb_unstack(acc, blk)
            ltot_ref[:, SB_LANES[g]] = _sb_unstack(jnp.broadcast_to(c_l, (2 * blk, PAIR)), blk)

    width = SB_GROUP * PAIR
    blkspec = pl.BlockSpec((blk, width), lambda p, i: (i, p))
    n_steps = N_PAIRS // SB_GROUP
    return pl.pallas_call(
        body, name=name, grid=(n_steps, ni),
        in_specs=[blkspec,
                  pl.BlockSpec((t, width), lambda p, i: (0, n_steps + p)),
                  pl.BlockSpec((t, width), lambda p, i: (0, 2 * n_steps + p))],
        out_specs=[blkspec, blkspec],
        out_shape=[_sds((t, D_MODEL)), _sds((t, D_MODEL // 2))],
        compiler_params=_params("arbitrary", "arbitrary"),
    )(qkv, qkv, qkv)


def _sb_bwd(qkv, ltot, do, *, name):
    t = qkv.shape[0]
    blk, kb = min(SB_QB, t), SB_KB
    ni, per = t // blk, blk // kb

    def body(q_ref, k_ref, v_ref, lt_ref, do_ref, dq_ref, dkout_ref, dvout_ref, dk_ref, dv_ref):
        i = pl.program_id(1)

        @pl.when(i == 0)
        def _():
            dk_ref[...] = jnp.zeros_like(dk_ref)
            dv_ref[...] = jnp.zeros_like(dv_ref)

        u_upto = _tri(kb, lambda r, c: r <= c)
        u_before = _tri(kb, lambda r, c: r < c)
        lane = lax.broadcasted_iota(jnp.int32, (1, PAIR), 1)
        q2 = [_bf(_sb_stack(q_ref[:, lanes] * ATT_SCALE)) for lanes in SB_LANES]
        do2 = [_bf(_sb_stack(do_ref[:, lanes])) for lanes in SB_LANES]
        total = [jnp.concatenate(
            [jnp.sum(jnp.where(lane == h * HEAD_DIM, lt_ref[:, lanes], 0.0), axis=1, keepdims=True)
             for h in range(2)], axis=0) for lanes in SB_LANES]

        def tile(g, ops, k0, mask, dq_acc, c_l, c_g):
            qg, dog, tot = ops
            krows = pl.ds(k0, kb)
            kj = k_ref[krows, SB_LANES[g]]
            vj = v_ref[krows, SB_LANES[g]]
            z = _dot_nt(qg, kj)
            sp = _softplus(z)
            sig = jnp.exp(z - sp)
            lf = -sp if mask is None else jnp.where(mask, -sp, 0.0)
            a = jnp.exp(z - sp + tot - (_cumsum01(lf, u_upto) + c_l))
            if mask is not None:
                a = jnp.where(mask, a, 0.0)
            gw = a * _dot_nt(dog, vj)
            g_before = jnp.dot(_bf(gw), u_before, preferred_element_type=F32) + c_g
            dz = gw * (1.0 - sig) - g_before * sig
            if mask is not None:
                dz = jnp.where(mask, dz, 0.0)
            dk_ref[krows, SB_LANES[g]] += _dot_tn(dz, qg)
            dv_ref[krows, SB_LANES[g]] += _dot_tn(a, dog)
            return (dq_acc + _dot(dz, kj), c_l + jnp.sum(lf, axis=1, keepdims=True),
                    c_g + jnp.sum(gw, axis=1, keepdims=True))

        def tiles(ops, k0, mask, carry):
            return tuple(tile(g, ops[g], k0, mask, *carry[g]) for g in range(SB_GROUP))

        ops = tuple(zip(q2, do2, total))
        zero = (jnp.zeros((2 * blk, PAIR), F32), jnp.zeros((2 * blk, 1), F32),
                jnp.zeros((2 * blk, 1), F32))
        carry = lax.fori_loop(
            0, per * i, lambda j, c: tiles(ops, pl.multiple_of(j * kb, kb), None, c),
            (zero,) * SB_GROUP)
        for d in range(per):
            r0 = d * kb
            sub = tiles(tuple(tuple(_sb_rows_from(a, blk, r0) for a in o) for o in ops),
                        pl.multiple_of(i * blk + r0, kb), _sb_mask(blk - r0, kb, 0),
                        tuple(tuple(_sb_rows_from(a, blk, r0) for a in c) for c in carry))
            carry = tuple(tuple(_sb_rows_merge(a, s, blk, r0) for a, s in zip(c, cs))
                          for c, cs in zip(carry, sub))
        for g, (dq_acc, _, _) in enumerate(carry):
            dq_ref[:, SB_LANES[g]] = _bf(_sb_unstack(dq_acc, blk) * ATT_SCALE)

        @pl.when(i == ni - 1)
        def _():
            dkout_ref[...] = _bf(dk_ref[...])
            dvout_ref[...] = _bf(dv_ref[...])

    width = SB_GROUP * PAIR
    n_steps = N_PAIRS // SB_GROUP
    blkspec = lambda off: pl.BlockSpec((blk, width), lambda p, i: (i, off + p))
    full = lambda off: pl.BlockSpec((t, width), lambda p, i: (0, off + p))
    return pl.pallas_call(
        body, name=name, grid=(n_steps, ni),
        in_specs=[blkspec(0), full(n_steps), full(2 * n_steps), blkspec(0), blkspec(0)],
        out_specs=[blkspec(0), full(0), full(0)],
        out_shape=[_sds((t, D_MODEL), BF16)] * 3,
        scratch_shapes=[pltpu.VMEM((t, width), F32), pltpu.VMEM((t, width), F32)],
        compiler_params=_params("arbitrary", "arbitrary"),
    )(qkv, qkv, qkv, ltot, do)


def _ch_mask(i):
    c = lax.broadcasted_iota(jnp.int32, (1, CH_WIN), 1)
    return c >= CH_LOOK - i * CH_QB


def _ch_band(row0, rows):
    r = row0 + lax.broadcasted_iota(jnp.int32, (rows, CH_WIN), 0)
    c = lax.broadcasted_iota(jnp.int32, (rows, CH_WIN), 1)
    qc = LOOKBACK + lax.shift_right_arithmetic(r, 6)
    kc = lax.shift_right_arithmetic(c, 6)
    return (kc <= qc) & (kc >= qc - LOOKBACK)


def _ch_probs(qm, kw, bias_h, mask):
    z = _dot_nt(qm, kw) * ATT_SCALE + bias_h
    z = jnp.where(mask, z, NEG_INF)
    e = jnp.exp(z - jnp.max(z, axis=1, keepdims=True))
    return e * (1.0 / jnp.sum(e, axis=1, keepdims=True))


def _ch_fill(pad_ref, src_ref, t):
    pad_ref[pl.ds(0, CH_LOOK), :] = jnp.zeros((CH_LOOK, PAIR), BF16)
    pad_ref[pl.ds(CH_LOOK, t), :] = _bf(src_ref[...])


def _ch_fwd(qkv, bias, o_in, *, name):
    t = qkv.shape[0]
    ni = t // CH_QB

    def body(q_ref, k_ref, v_ref, bias_ref, _alias, o_ref, kpad, vpad):
        i = pl.program_id(1)

        @pl.when(i == 0)
        def _():
            _ch_fill(kpad, k_ref, t)
            _ch_fill(vpad, v_ref, t)

        win = pl.ds(pl.multiple_of(i * CH_QB, CH_QB), CH_WIN)
        kw, vw = kpad[win, :], vpad[win, :]
        mask = _ch_mask(i)
        q = q_ref[...]
        outs = []
        for h, hm in enumerate(_head_masks()):
            p = _ch_probs(jnp.where(hm, q, 0.0), kw, bias_ref[h], mask)
            outs.append(_dot(p, vw))
        o_ref[...] = jnp.where(_head_masks()[0], outs[0], outs[1])

    full = lambda off: pl.BlockSpec((t, PAIR), lambda p, i: (0, off + p))
    return pl.pallas_call(
        body, name=name, grid=(N_PAIRS, ni),
        in_specs=[pl.BlockSpec((CH_QB, PAIR), lambda p, i: (i, 3 * N_PAIRS + p)),
                  full(4 * N_PAIRS), full(5 * N_PAIRS),
                  pl.BlockSpec((2, CH_QB, CH_WIN), lambda p, i: (p, 0, 0)), ANY],
        out_specs=pl.BlockSpec((CH_QB, PAIR), lambda p, i: (i, N_PAIRS + p)),
        out_shape=_sds((t, D_MODEL)),
        scratch_shapes=[pltpu.VMEM((t + CH_LOOK, PAIR), BF16)] * 2,
        input_output_aliases={4: 0},
        compiler_params=_params("arbitrary", "arbitrary"),
    )(qkv, qkv, qkv, bias, o_in)


def _ch_bwd(qkv, bias, o, do, dq_in, dk_in, dv_in, *, name):
    t = qkv.shape[0]
    ni = t // CH_QB

    def body(q_ref, k_ref, v_ref, bias_ref, o_ref, do_ref, _a0, _a1, _a2,
             dq_ref, dkout_ref, dvout_ref, dbias_ref, kpad, vpad, dkpad, dvpad):
        i = pl.program_id(1)

        @pl.when(i == 0)
        def _():
            _ch_fill(kpad, k_ref, t)
            _ch_fill(vpad, v_ref, t)
            dkpad[...] = jnp.zeros_like(dkpad)
            dvpad[...] = jnp.zeros_like(dvpad)
            dbias_ref[...] = jnp.zeros_like(dbias_ref)

        win = pl.ds(pl.multiple_of(i * CH_QB, CH_QB), CH_WIN)
        kw, vw = kpad[win, :], vpad[win, :]
        mask = _ch_mask(i)
        q, o_blk, do_blk = q_ref[...], o_ref[...], do_ref[...]
        dqs = []
        for h, hm in enumerate(_head_masks()):
            qm = _bf(jnp.where(hm, q, 0.0))
            dom = jnp.where(hm, do_blk, 0.0)
            delta = jnp.sum(dom * o_blk, axis=1, keepdims=True)
            dom = _bf(dom)
            p = _ch_probs(qm, kw, bias_ref[h], mask)
            ds = p * (_dot_nt(dom, vw) - delta)
            dbias_ref[h] += ds
            dsz = ds * ATT_SCALE
            dqs.append(_dot(dsz, kw))
            dkpad[win, :] += _dot_tn(dsz, qm)
            dvpad[win, :] += _dot_tn(p, dom)
        dq_ref[...] = _bf(jnp.where(_head_masks()[0], dqs[0], dqs[1]))

        @pl.when(i == ni - 1)
        def _():
            dkout_ref[...] = _bf(dkpad[pl.ds(CH_LOOK, t), :])
            dvout_ref[...] = _bf(dvpad[pl.ds(CH_LOOK, t), :])

    blkspec = lambda off: pl.BlockSpec((CH_QB, PAIR), lambda p, i: (i, off + p))
    full = lambda off: pl.BlockSpec((t, PAIR), lambda p, i: (0, off + p))
    bias_spec = pl.BlockSpec((2, CH_QB, CH_WIN), lambda p, i: (p, 0, 0))
    return pl.pallas_call(
        body, name=name, grid=(N_PAIRS, ni),
        in_specs=[blkspec(3 * N_PAIRS), full(4 * N_PAIRS), full(5 * N_PAIRS), bias_spec,
                  blkspec(N_PAIRS), blkspec(N_PAIRS), ANY, ANY, ANY],
        out_specs=[blkspec(N_PAIRS), full(N_PAIRS), full(N_PAIRS), bias_spec],
        out_shape=[_sds((t, D_MODEL), BF16)] * 3 + [_sds((2 * N_PAIRS, CH_QB, CH_WIN))],
        scratch_shapes=[pltpu.VMEM((t + CH_LOOK, PAIR), BF16)] * 2
        + [pltpu.VMEM((t + CH_LOOK, PAIR), F32)] * 2,
        input_output_aliases={6: 0, 7: 1, 8: 2},
        compiler_params=_params("arbitrary", "arbitrary"),
    )(qkv, qkv, qkv, bias, o, do, dq_in, dk_in, dv_in)


def _bias_expand(fvec, *, name):
    n_heads = fvec.shape[0]

    def body(f_ref, o_ref, rows8):
        row = f_ref[0]
        for r in range(8):
            rows8[pl.ds(r, 1), :] = pltpu.roll(row, r, 1)
        base = rows8[...]
        for blk in range(CH_QB // 8):
            o_ref[0, pl.ds(8 * blk, 8), :] = jnp.where(
                _ch_band(8 * blk, 8), pltpu.roll(base, 8 * blk, 1), NEG_INF)

    return pl.pallas_call(
        body, name=name, grid=(n_heads,),
        in_specs=[pl.BlockSpec((1, 1, CH_WIN), lambda h: (h, 0, 0))],
        out_specs=pl.BlockSpec((1, CH_QB, CH_WIN), lambda h: (h, 0, 0)),
        out_shape=_sds((n_heads, CH_QB, CH_WIN)),
        scratch_shapes=[pltpu.VMEM((8, CH_WIN), F32)],
        compiler_params=_params("arbitrary"),
    )(fvec)


def _bias_grad(dbias, after, *, name):
    n_heads = dbias.shape[0]
    first = CH_LOOK - REL_CLIP

    def body(d_ref, _after, o_ref, acc8):
        acc = jnp.zeros((8, CH_WIN), F32)
        for blk in range(CH_QB // 8):
            acc = acc + pltpu.roll(d_ref[0, pl.ds(8 * blk, 8), :], (CH_WIN - 8 * blk) % CH_WIN, 1)
        acc8[...] = acc
        dvec = jnp.zeros((1, CH_WIN), F32)
        for r in range(8):
            dvec = dvec + pltpu.roll(acc8[pl.ds(r, 1), :], (CH_WIN - r) % CH_WIN, 1)
        lane = lax.broadcasted_iota(jnp.int32, (1, CH_WIN), 1)
        clipped = (lane <= first) | (lane >= first + REL_CLIP + CHUNK)
        total = jnp.sum(jnp.where(clipped, dvec, 0.0), axis=1, keepdims=True)
        o_ref[0] = jnp.where(lane == first, total, dvec)

    return pl.pallas_call(
        body, name=name, grid=(n_heads,),
        in_specs=[pl.BlockSpec((1, CH_QB, CH_WIN), lambda h: (h, 0, 0)), ANY],
        out_specs=pl.BlockSpec((1, 1, CH_WIN), lambda h: (h, 0, 0)),
        out_shape=_sds((n_heads, 1, CH_WIN)),
        scratch_shapes=[pltpu.VMEM((8, CH_WIN), F32)],
        compiler_params=_params("arbitrary"),
    )(dbias, after)


def _out_fwd(o, h1, g_sb, g_ch, g_post, wout, *, name):
    t = o.shape[0]
    tm = 512
    half = D_MODEL // 2

    def body(o_ref, h_ref, gsb_ref, gch_ref, gpost_ref, w_ref, h2_ref, mixed_ref, y_ref):
        ov = o_ref[...]
        mixed = jnp.concatenate([_rms(ov[:, :half], gsb_ref[...]),
                                 _rms(ov[:, half:], gch_ref[...])], axis=1)
        mixed_ref[...] = _bf(mixed)
        y = _dot(mixed, w_ref[...])
        y_ref[...] = y
        h2_ref[...] = h_ref[...] + _rms(y, gpost_ref[...])

    row = pl.BlockSpec((tm, D_MODEL), lambda i: (i, 0))
    gain = lambda n: pl.BlockSpec((1, n), lambda i: (0, 0))
    return pl.pallas_call(
        body, name=name, grid=(t // tm,),
        in_specs=[row, row, gain(half), gain(half), gain(D_MODEL),
                  pl.BlockSpec((D_MODEL, D_MODEL), lambda i: (0, 0))],
        out_specs=[row, row, row],
        out_shape=[_sds((t, D_MODEL)), _sds((t, D_MODEL), BF16), _sds((t, D_MODEL))],
        compiler_params=_params("arbitrary"),
    )(o, h1, g_sb, g_ch, g_post, wout)


def _out_bwd(dy, mixed, o, g_sb, g_ch, wout, *, name):
    t = o.shape[0]
    tm = 512
    ni = t // tm
    half = D_MODEL // 2

    def body(dy_ref, mixed_ref, o_ref, gsb_ref, gch_ref, w_ref,
             dw_ref, do_ref, dgsb_ref, dgch_ref, acc_ref):
        i = pl.program_id(0)

        @pl.when(i == 0)
        def _():
            acc_ref[...] = jnp.zeros_like(acc_ref)
            dgsb_ref[...] = jnp.zeros_like(dgsb_ref)
            dgch_ref[...] = jnp.zeros_like(dgch_ref)

        dyv = dy_ref[...]
        acc_ref[...] += _dot_tn(mixed_ref[...], dyv)
        dm = _dot_nt(dyv, w_ref[...])
        ov = o_ref[...]
        doa, dga = _rms_bwd(dm[:, :half], ov[:, :half], gsb_ref[...])
        dob, dgb = _rms_bwd(dm[:, half:], ov[:, half:], gch_ref[...])
        do_ref[...] = jnp.concatenate([doa, dob], axis=1)
        dgsb_ref[...] += dga
        dgch_ref[...] += dgb

        @pl.when(i == ni - 1)
        def _():
            dw_ref[...] = _bf(acc_ref[...])

    row = pl.BlockSpec((tm, D_MODEL), lambda i: (i, 0))
    gain = pl.BlockSpec((1, half), lambda i: (0, 0))
    sq = pl.BlockSpec((D_MODEL, D_MODEL), lambda i: (0, 0))
    return pl.pallas_call(
        body, name=name, grid=(ni,),
        in_specs=[row, row, row, gain, gain, sq],
        out_specs=[sq, row, gain, gain],
        out_shape=[_sds((D_MODEL, D_MODEL), BF16), _sds((t, D_MODEL)),
                   _sds((1, half)), _sds((1, half))],
        scratch_shapes=[pltpu.VMEM((D_MODEL, D_MODEL), F32)],
        compiler_params=_params("arbitrary"),
    )(dy, mixed, o, g_sb, g_ch, wout)


def _ple(p, h3, target, wp, wgate, g, f_post, g_post, *, name):
    t = h3.shape[0]
    tm = 512
    ni = t // tm

    def body(p_ref, h_ref, tgt_ref, wp_ref, wg_ref, g_ref, f_ref, gf_ref,
             loss_ref, dres_ref, dwp_ref, dwg_ref, dg_ref, df_ref, dgf_ref, accp, accg):
        i = pl.program_id(0)

        @pl.when(i == 0)
        def _():
            loss_ref[...] = jnp.zeros_like(loss_ref)
            dg_ref[...] = jnp.zeros_like(dg_ref)
            dgf_ref[...] = jnp.zeros_like(dgf_ref)
            accp[...] = jnp.zeros_like(accp)
            accg[...] = jnp.zeros_like(accg)

        pv, hv, gv = p_ref[...], h_ref[...], g_ref[...]
        pe = _dot(pv, wp_ref[...])
        sig = _sigmoid(_dot(hv, wg_ref[...]))
        e = pe * sig
        err = hv + _rms(e, gv) - tgt_ref[...]
        tok = jnp.mean(err * err, axis=-1, keepdims=True)
        loss_ref[...] += 0.5 * jnp.sum(tok, axis=0, keepdims=True)
        dh4 = err * (1.0 / D_MODEL)
        de, dg = _rms_bwd(dh4, e, gv)
        dg_ref[...] += dg
        dpe = de * sig
        dgt = de * pe * sig * (1.0 - sig)
        accp[...] += _dot_tn(pv, dpe)
        accg[...] += _dot_tn(hv, dgt)
        dres = dh4 + _dot_nt(dgt, wg_ref[...])
        dres_ref[...] = dres
        df, dgf = _rms_bwd(0.5 * dres, f_ref[...], gf_ref[...])
        df_ref[...] = _bf(df)
        dgf_ref[...] += dgf

        @pl.when(i == ni - 1)
        def _():
            dwp_ref[...] = _bf(accp[...])
            dwg_ref[...] = _bf(accg[...])

    row = pl.BlockSpec((tm, D_MODEL), lambda i: (i, 0))
    const = lambda r, c: pl.BlockSpec((r, c), lambda i: (0, 0))
    return pl.pallas_call(
        body, name=name, grid=(ni,),
        in_specs=[pl.BlockSpec((tm, PLE_DIM), lambda i: (i, 0)), row, row,
                  const(PLE_DIM, D_MODEL), const(D_MODEL, D_MODEL), const(1, D_MODEL),
                  row, const(1, D_MODEL)],
        out_specs=[const(1, 128), row, const(PLE_DIM, D_MODEL), const(D_MODEL, D_MODEL),
                   const(1, D_MODEL), row, const(1, D_MODEL)],
        out_shape=[_sds((1, 128)), _sds((t, D_MODEL)), _sds((PLE_DIM, D_MODEL), BF16),
                   _sds((D_MODEL, D_MODEL), BF16), _sds((1, D_MODEL)),
                   _sds((t, D_MODEL), BF16), _sds((1, D_MODEL))],
        scratch_shapes=[pltpu.VMEM((PLE_DIM, D_MODEL), F32), pltpu.VMEM((D_MODEL, D_MODEL), F32)],
        compiler_params=_params("arbitrary"),
    )(p, h3, target, wp, wgate, g, f_post, g_post)


def _rel_bias_to_fvec(rel_bias):
    rev = rel_bias[:, ::-1]
    n_heads = rel_bias.shape[0]
    first = CH_LOOK - REL_CLIP
    n_var = REL_CLIP + CHUNK
    clipped = rev[:, :1]
    fvec = jnp.concatenate([jnp.broadcast_to(clipped, (n_heads, first)), rev[:, :n_var],
                            jnp.broadcast_to(clipped, (n_heads, CH_WIN - first - n_var))], axis=1)
    return fvec.reshape(n_heads, 1, CH_WIN)


def _fvec_grad_to_rel_bias(dfvec):
    first = CH_LOOK - REL_CLIP
    n_var = REL_CLIP + CHUNK
    rev = jnp.pad(dfvec[:, 0, first:first + n_var], ((0, 0), (0, N_REL - n_var)))
    return rev[:, ::-1]


def _local_step(x, p, target, g, weights_for, grads_done, fvec, weights_early=None):
    bias = _bias_expand(fvec, name="bias_expand")
    w, tie = weights_for(0, bias)
    w = dict(w)
    h1, n1, a1, b1, f1 = _ffn_fwd(x, g["ffn1_pre"] + tie, g["ffn1_post"],
                                  w["ffn1_gate"], w["ffn1_up"], w["ffn1_down"], name="ffn1_fwd")
    more, tie = weights_for(1, h1)
    w.update(more)
    qkv, u = _qkv_fwd(h1, g["mix_pre"] + tie, w["in"], name="qkv_fwd")
    o, ltot = _sb_fwd(qkv, name="sb_fwd")
    tie = weights_early(2, ltot) if weights_early else 0.0
    o = _ch_fwd(qkv, bias, o, name="ch_fwd")
    w.update(weights_for(2, o)[0])
    h2, mixed, y = _out_fwd(o, h1, g["out_sb"] + tie, g["out_ch"], g["mix_post"], w["out"],
                            name="out_fwd")
    h3, n2, a2, b2, f2 = _ffn_fwd(h2, g["ffn2_pre"], g["ffn2_post"],
                                  w["ffn2_gate"], w["ffn2_up"], w["ffn2_down"], name="ffn2_fwd")
    loss, dh3, dwp, dwgate, dg_ple, df2, dg_ffn2_post = _ple(
        p, h3, target, w["ple_proj"], w["ple_gate"], g["ple_post"], f2, g["ffn2_post"], name="ple")
    tie = grads_done(0, {"ple_proj": dwp, "ple_gate": dwgate})
    dwg2, dwu2, dwd2, dn2 = _ffn_bwd(n2, df2, a2, b2, w["ffn2_gate"], w["ffn2_up"],
                                     w["ffn2_down"], name="ffn2_bwd")
    tie = tie + grads_done(1, {"ffn2_gate": dwg2, "ffn2_up": dwu2, "ffn2_down": dwd2})
    dh2, dg_ffn2_pre, dy, dg_mix_post = _junction(
        dh3, pre=(dn2, h2, g["ffn2_pre"] + tie), post=(y, g["mix_post"], 1.0), name="junction2")
    dwout, do, dg_sb, dg_ch = _out_bwd(dy, mixed, o, g["out_sb"], g["out_ch"], w["out"],
                                       name="out_bwd")
    dq, dk, dv = _sb_bwd(qkv, ltot, do, name="sb_bwd")
    dq, dk, dv, dbias = _ch_bwd(qkv, bias, o, do, dq, dk, dv, name="ch_bwd")
    dwin, du = _qkv_bwd(dq, dk, dv, u, w["in"], name="qkv_bwd")
    tie = grads_done(2, {"out": dwout, "in": dwin})
    dh1, dg_mix_pre, df1, dg_ffn1_post = _junction(
        dh2, pre=(du, h1, g["mix_pre"] + tie), post=(f1, g["ffn1_post"], 0.5), name="junction1")
    dwg1, dwu1, dwd1, dn1 = _ffn_bwd(n1, df1, a1, b1, w["ffn1_gate"], w["ffn1_up"],
                                     w["ffn1_down"], name="ffn1_bwd")
    tie = grads_done(3, {"ffn1_gate": dwg1, "ffn1_up": dwu1, "ffn1_down": dwd1})
    dx, dg_ffn1_pre = _junction(dh1, pre=(dn1, x, g["ffn1_pre"] + tie), name="junction0")

    dg = {"ffn1_pre": dg_ffn1_pre, "ffn1_post": dg_ffn1_post, "mix_pre": dg_mix_pre,
          "mix_post": dg_mix_post, "out_sb": dg_sb, "out_ch": dg_ch,
          "ffn2_pre": dg_ffn2_pre, "ffn2_post": dg_ffn2_post, "ple_post": dg_ple}
    return loss, dx, dg, dbias


_WEIGHTS = (
    ("ffn1_gate", "row", FF_SHARD, FF_SHARD_PAD, D_MODEL),
    ("ffn1_up", "row", FF_SHARD, FF_SHARD_PAD, D_MODEL),
    ("ffn1_down", "row", FF_SHARD, FF_SHARD_PAD, D_MODEL),
    ("in", "col", QKV_SHARD, QKV_SHARD, D_MODEL),
    ("out", "row", ROW_SHARD, ROW_SHARD, D_MODEL),
    ("ffn2_gate", "row", FF_SHARD, FF_SHARD_PAD, D_MODEL),
    ("ffn2_up", "row", FF_SHARD, FF_SHARD_PAD, D_MODEL),
    ("ffn2_down", "row", FF_SHARD, FF_SHARD_PAD, D_MODEL),
    ("ple_proj", "col", ROW_SHARD, ROW_SHARD, PLE_DIM),
    ("ple_gate", "row", ROW_SHARD, ROW_SHARD, D_MODEL),
)
_TRANSPOSED = ("ffn1_gate", "ffn1_up", "ffn2_gate", "ffn2_up")
_SPEC = {n: (kind, valid, pad, other) for n, kind, valid, pad, other in _WEIGHTS}
_GATHER_STAGES = (("ffn1_gate", "ffn1_up", "ffn1_down"), ("in",),
                  ("out", "ffn2_gate", "ffn2_up", "ffn2_down", "ple_proj", "ple_gate"))
_SCATTER_STAGES = (("ple_proj", "ple_gate"), ("ffn2_gate", "ffn2_up", "ffn2_down"),
                   ("out", "in"), ("ffn1_gate", "ffn1_up", "ffn1_down"))
HBM = pl.BlockSpec(memory_space=pltpu.HBM)
SEM = pl.BlockSpec(memory_space=pltpu.SEMAPHORE)
EFFECT = pltpu.SideEffectType.DATAFLOW_SIDE_EFFECTING


def _shard_shape(kind, size, other):
    return (other, size) if kind == "col" else (size, other)


def _window(ref, kind, start, size):
    return ref.at[:, pl.ds(start, size)] if kind == "col" else ref.at[pl.ds(start, size), :]


def _device_tuple(k):
    return (k // 4, (k // 2) % 2, k % 2)


def _my_index():
    return 4 * lax.axis_index("x") + 2 * lax.axis_index("y") + lax.axis_index("c")


def _pack_weights(names, shards, after, *, name):
    nw = len(names)
    specs = [_SPEC[n] for n in names]

    def body(*refs):
        ins, packed, full = refs[:nw], refs[nw + 1:2 * nw + 1], refs[2 * nw + 1:3 * nw + 1]
        sem = refs[3 * nw + 1]
        me = _my_index()
        for (kind, valid, pad, _), src, dst in zip(specs, ins, packed):
            if pad != valid:
                dst[...] = jnp.zeros_like(dst)
            if kind == "col":
                dst[:, pl.ds(0, valid)] = _bf(src[...])
            else:
                dst[pl.ds(0, valid), :] = _bf(src[...])
        for k in range(N_DEV):
            @pl.when(me == k)
            def _():
                for w, (kind, _, pad, _) in enumerate(specs):
                    pltpu.make_async_copy(packed[w], _window(full[w], kind, k * pad, pad),
                                          sem.at[w]).start()
        for w, (kind, _, pad, _) in enumerate(specs):
            pltpu.make_async_copy(packed[w], _window(full[w], kind, 0, pad), sem.at[w]).wait()

    whole = lambda shape: pl.BlockSpec(shape, lambda i: (0, 0))
    packed_shapes = [_shard_shape(kind, pad, other) for kind, _, pad, other in specs]
    outs = pl.pallas_call(
        body, name=name, grid=(1,),
        in_specs=[whole(a.shape) for a in shards] + [ANY],
        out_specs=[whole(s) for s in packed_shapes] + [ANY] * nw,
        out_shape=[_sds(s, BF16) for s in packed_shapes]
        + [_sds(_shard_shape(kind, N_DEV * pad, other), BF16) for kind, _, pad, other in specs],
        scratch_shapes=[pltpu.SemaphoreType.DMA((nw,))],
        compiler_params=_params("arbitrary"),
    )(*shards, after)
    return dict(zip(names, outs[:nw])), dict(zip(names, outs[nw:]))


def _hbm(a):
    return pltpu.with_memory_space_constraint(a, pltpu.HBM)


def _split_start(name, n, body_copies, sources, lands, after):
    arrays = list(sources) + list(lands)
    ns, na = len(sources), len(arrays)

    def body(*refs):
        src, land = refs[:ns], refs[ns:na]
        send, recv = refs[na + 1], refs[na + 2]
        token = refs[-1]
        body_copies(src, land, send, recv)
        token[...] = jnp.zeros_like(token)

    out = pl.pallas_call(
        body, name=name,
        out_shape=(pltpu.SemaphoreType.DMA((n,)), pltpu.SemaphoreType.DMA((n,)),
                   *[pltpu.HBM(a.shape, a.dtype) for a in arrays], _sds((8, 128))),
        in_specs=[HBM] * na + [ANY], out_specs=(SEM, SEM, *[HBM] * na, VMEM),
        input_output_aliases={i: 2 + i for i in range(na)},
        compiler_params=pltpu.CompilerParams(has_side_effects=EFFECT),
    )(*[_hbm(a) for a in arrays], after)
    return out[0], out[1], out[2:2 + ns], out[2 + ns:2 + na], out[-1]


def _split_wait(name, n, seven_of, send, recv, sources, lands, after, keep_sources=False):
    arrays = list(sources) + list(lands)
    ns, na = len(sources), len(arrays)

    def body(*refs):
        land = refs[ns:na]
        send_ref, recv_ref = refs[na], refs[na + 1]
        myself = (lax.axis_index("x"), lax.axis_index("y"), lax.axis_index("c"))
        for w in range(n):
            seven = seven_of(w, land[w])
            copy = pltpu.make_async_remote_copy(
                src_ref=seven, dst_ref=seven, send_sem=send_ref.at[w], recv_sem=recv_ref.at[w],
                device_id=myself, device_id_type=MESH)
            copy.wait_send()
            copy.wait_recv()

    afters = tuple(after) if isinstance(after, (tuple, list)) else (after,)
    out = pl.pallas_call(
        body, name=name,
        out_shape=[pltpu.HBM(a.shape, a.dtype) for a in arrays],
        in_specs=[HBM] * na + [SEM, SEM] + [ANY] * len(afters), out_specs=[HBM] * na,
        input_output_aliases={i: i for i in range(na)},
        compiler_params=pltpu.CompilerParams(has_side_effects=EFFECT),
    )(*arrays, send, recv, *afters)
    return out if keep_sources else out[ns:]


_ALL_PEERS = (1, 2, 3, 4, 5, 6, 7)
_NEAR_PEERS = (1, 2, 4, 6)
_FAR_CHIPS = (2, 4, 6)


def _gather_start(stage, names, packed, full, after, peers=_ALL_PEERS):
    def copies(src, land, send, recv):
        me = _my_index()
        for k in range(N_DEV):
            @pl.when(me == k)
            def _():
                for w, name in enumerate(names):
                    kind, _, pad, _ = _SPEC[name]
                    dst = _window(land[w], kind, k * pad, pad)
                    for mask in peers:
                        pltpu.make_async_remote_copy(
                            src_ref=src[w], dst_ref=dst, send_sem=send.at[w],
                            recv_sem=recv.at[w], device_id=_device_tuple(k ^ mask),
                            device_id_type=MESH).start()

    return _split_start(f"gather_start{stage}", len(names), copies,
                        [packed[n] for n in names], [full[n] for n in names], after)


def _gather_wait(stage, names, started, after, count=N_DEV - 1):
    send, recv, src, land, _ = started

    def bytes_of(w, ref):
        kind, _, pad, _ = _SPEC[names[w]]
        return _window(ref, kind, 0, count * pad)

    return dict(zip(names, _split_wait(f"gather_wait{stage}", len(names), bytes_of,
                                       send, recv, src, land, after)))


def _relay_start(stage, names, full, after):
    def copies(_, land, send, recv):
        me = _my_index()
        for k in range(N_DEV):
            @pl.when(me == k)
            def _():
                for w, name in enumerate(names):
                    kind, _, pad, _ = _SPEC[name]
                    for mask in _FAR_CHIPS:
                        win = _window(land[w], kind, (k ^ mask) * pad, pad)
                        pltpu.make_async_remote_copy(
                            src_ref=win, dst_ref=win, send_sem=send.at[w], recv_sem=recv.at[w],
                            device_id=_device_tuple(k ^ 1), device_id_type=MESH).start()

    return _split_start(f"relay_start{stage}", len(names), copies, [],
                        [full[n] for n in names], after)


def _scatter_start(stage, names, grads, after):
    def copies(src, land, send, recv):
        me = _my_index()
        for k in range(N_DEV):
            @pl.when(me != k)
            def _():
                slot = lax.rem(me + (N_DEV - 1 - k), N_DEV)
                for w, name in enumerate(names):
                    kind, _, pad, _ = _SPEC[name]
                    pltpu.make_async_remote_copy(
                        src_ref=_window(src[w], kind, k * pad, pad), dst_ref=land[w].at[slot],
                        send_sem=send.at[w], recv_sem=recv.at[w],
                        device_id=_device_tuple(k), device_id_type=MESH).start()

    lands = [lax.empty((N_DEV - 1,) + _shard_shape(_SPEC[m][0], _SPEC[m][2], _SPEC[m][3]), BF16)
             for m in names]
    return _split_start(f"scatter_start{stage}", len(names), copies, grads, lands, after)


def _scatter_wait(stage, names, started, after):
    send, recv, src, land, _ = started
    n = len(names)
    out = _split_wait(f"scatter_wait{stage}", n, lambda w, ref: ref, send, recv, src, land, after,
                      keep_sources=True)
    return dict(zip(names, out[:n])), dict(zip(names, out[n:]))


N_CHIPS = N_DEV // 2


def _pair_start(stage, names, grads, after):
    def copies(src, land, send, recv):
        me = _my_index()
        for k in range(N_DEV):
            @pl.when(me == k)
            def _():
                for w, name in enumerate(names):
                    kind, _, pad, _ = _SPEC[name]
                    for chip in range(N_CHIPS):
                        j = 2 * chip + ((k ^ 1) & 1)
                        pltpu.make_async_remote_copy(
                            src_ref=_window(src[w], kind, j * pad, pad), dst_ref=land[w].at[chip],
                            send_sem=send.at[w], recv_sem=recv.at[w],
                            device_id=_device_tuple(k ^ 1), device_id_type=MESH).start()

    lands = [lax.empty((N_CHIPS,) + _shard_shape(_SPEC[m][0], _SPEC[m][2], _SPEC[m][3]), BF16)
             for m in names]
    return _split_start(f"pair_start{stage}", len(names), copies, grads, lands, after)


def _pair_sum(dw_fulls, pairs, *, pad, name):
    n = len(dw_fulls)
    other = dw_fulls[0].shape[1]

    def body(*refs):
        for own_ref, pair_ref, out_ref in zip(refs[:n], refs[n:2 * n], refs[2 * n:]):
            out_ref[0] = _bf(own_ref[...].astype(F32) + pair_ref[0].astype(F32))

    slot = pl.BlockSpec((1, pad, other), lambda q: (q, 0, 0))
    own = pl.BlockSpec((pad, other), lambda q: (2 * q + lax.axis_index("c"), 0))
    return pl.pallas_call(
        body, name=name, grid=(N_CHIPS,),
        in_specs=[own] * n + [slot] * n, out_specs=[slot] * n,
        out_shape=[_sds((N_CHIPS, pad, other), BF16)] * n,
        compiler_params=_params("arbitrary"),
    )(*dw_fulls, *pairs)


def _chip_start(stage, names, sums, after):
    def copies(src, land, send, recv):
        me = _my_index()
        my_chip = lax.shift_right_logical(me, 1)
        for k in range(N_DEV):
            @pl.when((me != k) & (((me ^ k) & 1) == 0))
            def _():
                slot = lax.rem(my_chip + (N_CHIPS - 1 - k // 2), N_CHIPS)
                for w in range(len(names)):
                    pltpu.make_async_remote_copy(
                        src_ref=src[w].at[k // 2], dst_ref=land[w].at[slot],
                        send_sem=send.at[w], recv_sem=recv.at[w],
                        device_id=_device_tuple(k), device_id_type=MESH).start()

    lands = [lax.empty((N_CHIPS - 1,) + a.shape[1:], BF16) for a in sums]
    return _split_start(f"chip_start{stage}", len(names), copies, sums, lands, after)


def _adamw_chip(w, m, v, land, sums, *, name):
    shape = w.shape

    def body(w_ref, m_ref, v_ref, land_ref, own_ref, *outs):
        rows = pl.ds(0, shape[0])
        grad = own_ref[0, rows, :].astype(F32)
        for s in range(N_CHIPS - 1):
            grad = grad + land_ref[s, rows, :].astype(F32)
        _adam_update(w_ref, m_ref, v_ref, grad, *outs)

    whole = lambda a: pl.BlockSpec(a.shape, lambda i: (0,) * a.ndim)
    own = pl.BlockSpec((1,) + sums.shape[1:],
                       lambda i: (2 * lax.axis_index("x") + lax.axis_index("y"), 0, 0))
    return pl.pallas_call(
        body, name=name, grid=(1,),
        in_specs=[whole(w), whole(m), whole(v), whole(land), own],
        out_specs=[whole(w)] * 4, out_shape=[_sds(shape)] * 4,
        compiler_params=_params("arbitrary"),
    )(w, m, v, land, sums)


def _allreduce_small(small, after):
    shape = small.shape

    def body(in_ref, _after, out_ref, gath, send, recv):
        me = _my_index()
        for k in range(N_DEV):
            @pl.when(me != k)
            def _():
                pltpu.make_async_remote_copy(
                    src_ref=in_ref, dst_ref=gath.at[me], send_sem=send, recv_sem=recv,
                    device_id=_device_tuple(k), device_id_type=MESH).start()

            @pl.when(me == k)
            def _():
                gath[k] = in_ref[...]
        seven = gath.at[pl.ds(0, N_DEV - 1)]
        pltpu.make_async_remote_copy(
            src_ref=seven, dst_ref=seven, send_sem=send, recv_sem=recv,
            device_id=_device_tuple(0), device_id_type=MESH).wait()
        total = gath[0]
        for s in range(1, N_DEV):
            total = total + gath[s]
        out_ref[...] = total

    return pl.pallas_call(
        body, name="allreduce_small",
        in_specs=[VMEM, ANY], out_specs=VMEM, out_shape=_sds(shape),
        scratch_shapes=[pltpu.VMEM((N_DEV,) + shape, F32),
                        pltpu.SemaphoreType.DMA, pltpu.SemaphoreType.DMA],
    )(small, after)


def _adam_update(w_ref, m_ref, v_ref, grad, grad_ref, delta_ref, nm_ref, nv_ref):
    new_m = ADAM_B1 * m_ref[...] + (1.0 - ADAM_B1) * grad
    new_v = ADAM_B2 * v_ref[...] + (1.0 - ADAM_B2) * (grad * grad)
    m_hat = new_m / (1.0 - ADAM_B1 ** ADAM_STEP)
    v_hat = new_v / (1.0 - ADAM_B2 ** ADAM_STEP)
    grad_ref[...] = grad
    delta_ref[...] = -ADAM_LR * (m_hat / (jnp.sqrt(v_hat) + ADAM_EPS) + ADAM_WD * w_ref[...])
    nm_ref[...] = new_m
    nv_ref[...] = new_v


def _adamw(w, m, v, g, *, name):
    def body(w_ref, m_ref, v_ref, g_ref, *outs):
        _adam_update(w_ref, m_ref, v_ref, g_ref[...], *outs)

    whole = pl.BlockSpec(w.shape, lambda i: (0,) * w.ndim)
    return pl.pallas_call(
        body, name=name, grid=(1,), in_specs=[whole] * 4, out_specs=[whole] * 4,
        out_shape=[_sds(w.shape)] * 4, compiler_params=_params("arbitrary"),
    )(w, m, v, g)


def _adamw_gains(small, params):
    n = len(params)

    def body(small_ref, *refs):
        ins, outs = refs[:3 * n], refs[3 * n:]
        for r in range(n):
            width = ins[3 * r].shape[1]
            if width == D_MODEL:
                grad = small_ref[pl.ds(r, 1), :]
            else:
                grad = small_ref[pl.ds(len(_GAINS), 1), pl.ds((r - len(_GAINS)) * width, width)]
            _adam_update(*ins[3 * r:3 * r + 3], grad, *outs[4 * r:4 * r + 4])

    whole = lambda a: pl.BlockSpec(a.shape, lambda i: (0, 0))
    flat = [a for group in params for a in group]
    return pl.pallas_call(
        body, name="adamw_gains", grid=(1,),
        in_specs=[whole(small)] + [whole(a) for a in flat],
        out_specs=[whole(w) for w, _, _ in params for _ in range(4)],
        out_shape=[_sds(w.shape) for w, _, _ in params for _ in range(4)],
        compiler_params=_params("arbitrary"),
    )(small, *flat)


def _adamw_shard(w, m, v, land, dw_full, *, kind, pad, name):
    shape = w.shape
    other = shape[0] if kind == "col" else shape[1]

    def body(w_ref, m_ref, v_ref, land_ref, own_ref, *outs):
        valid = ((slice(None), pl.ds(0, shape[1])) if kind == "col"
                 else (pl.ds(0, shape[0]), slice(None)))
        grad = own_ref[valid].astype(F32)
        for s in range(N_DEV - 1):
            grad = grad + land_ref[(s,) + valid].astype(F32)
        _adam_update(w_ref, m_ref, v_ref, grad, *outs)

    whole = lambda a: pl.BlockSpec(a.shape, lambda i: (0,) * a.ndim)
    own = pl.BlockSpec(_shard_shape(kind, pad, other),
                       (lambda i: (0, _my_index())) if kind == "col" else (lambda i: (_my_index(), 0)))
    return pl.pallas_call(
        body, name=name, grid=(1,),
        in_specs=[whole(w), whole(m), whole(v), whole(land), own],
        out_specs=[whole(w)] * 4, out_shape=[_sds(shape)] * 4,
        compiler_params=_params("arbitrary"),
    )(w, m, v, land, dw_full)


_GAINS = ("ffn1_pre", "ffn1_post", "mix_pre", "mix_post", "ffn2_pre", "ffn2_post", "ple_post")
_SMALL_ROWS = 16


def _stack_gains(get):
    return jnp.concatenate([get(n) for n in _GAINS]
                           + [jnp.concatenate([get("out_sb"), get("out_ch")], axis=1)], axis=0)


def kernel(x, p, g_ffn1_pre, g_ffn1_post, w_ffn1_gate, w_ffn1_up, w_ffn1_down, g_mix_pre, g_mix_post, w_in, g_out_sb, g_out_ch, rel_bias, w_out, g_ffn2_pre, g_ffn2_post, w_ffn2_gate, w_ffn2_up, w_ffn2_down, w_ple_proj, w_ple_gate, g_ple_post, loss_target, m_g_ffn1_pre, m_g_ffn1_post, m_w_ffn1_gate, m_w_ffn1_up, m_w_ffn1_down, m_g_mix_pre, m_g_mix_post, m_w_in, m_g_out_sb, m_g_out_ch, m_rel_bias, m_w_out, m_g_ffn2_pre, m_g_ffn2_post, m_w_ffn2_gate, m_w_ffn2_up, m_w_ffn2_down, m_w_ple_proj, m_w_ple_gate, m_g_ple_post, v_g_ffn1_pre, v_g_ffn1_post, v_w_ffn1_gate, v_w_ffn1_up, v_w_ffn1_down, v_g_mix_pre, v_g_mix_post, v_w_in, v_g_out_sb, v_g_out_ch, v_rel_bias, v_w_out, v_g_ffn2_pre, v_g_ffn2_post, v_w_ffn2_gate, v_w_ffn2_up, v_w_ffn2_down, v_w_ple_proj, v_w_ple_gate, v_g_ple_post):
    given = dict(locals())
    wnames = [n for n, *_ in _WEIGHTS]

    def shard(prefix, n):
        a = given[prefix + "w_" + n][0]
        return a.T if n in _TRANSPOSED else a

    anchor = x[0]
    first = _GATHER_STAGES[0]
    packed, full = _pack_weights(first, [shard("", n) for n in first], anchor, name="pack_first")
    two_level = (0, 2)
    gathers = {}

    def start_stage(stage, after):
        peers = _NEAR_PEERS if stage in two_level else _ALL_PEERS
        gathers[stage] = _gather_start(stage, _GATHER_STAGES[stage], packed, full, after,
                                       peers=peers)

    start_stage(0, anchor)
    rest = [n for n in wnames if n not in first]
    packed_rest, full_rest = _pack_weights(rest, [shard("", n) for n in rest], gathers[0][-1],
                                           name="pack_rest")
    packed.update(packed_rest)
    full.update(full_rest)

    relays = {}

    def first_level(stage, after):
        names = _GATHER_STAGES[stage]
        last_stage = stage + 1 == len(_GATHER_STAGES)
        count = len(_NEAR_PEERS) if stage in two_level else N_DEV - 1
        if stage == 0:
            after = (after, packed_rest[rest[0]])
        ws = _gather_wait(stage, names, gathers[stage], after, count=count)
        if not last_stage:
            start_stage(stage + 1, ws[names[0]])
        if stage in two_level:
            relays[stage] = _relay_start(stage, names, ws,
                                         anchor if last_stage else gathers[stage + 1][-1])
            return ws, relays[stage][-1]
        return ws, None if last_stage else gathers[stage + 1][-1]

    def weights_early(stage, after):
        return first_level(stage, after)[1][:1, :1]

    def weights_for(stage, after):
        names = _GATHER_STAGES[stage]
        ws, token = (None, None) if stage in relays else first_level(stage, after)
        if stage in relays:
            relay = relays[stage]
            ws = _gather_wait(f"{stage}r", names, relay, after, count=len(_FAR_CHIPS))
            token = None if stage + 1 == len(_GATHER_STAGES) else gathers[stage + 1][-1]
        return ws, jnp.zeros((1, 1), F32) if token is None else token[:1, :1]

    scatters = {}

    last = len(_SCATTER_STAGES) - 1

    def grads_done(stage, grads):
        names = _SCATTER_STAGES[stage]
        start = _pair_start if stage == last else _scatter_start
        scatters[stage] = start(stage, names, [grads[n] for n in names], anchor)
        return scatters[stage][-1][:1, :1]

    gains = {n: given["g_" + n] for n in _GAINS + ("out_sb", "out_ch")}
    fvec = _rel_bias_to_fvec(rel_bias[0])
    loss, dx, dg, dbias = _local_step(x[0], p[0, 0], loss_target[0], gains,
                                      weights_for, grads_done, fvec, weights_early)

    results = {}

    def finish(stage, after):
        names = _SCATTER_STAGES[stage]
        dws, lands = _scatter_wait(stage, names, scatters[stage], after)
        for n in names:
            kind, _, pad, _ = _SPEC[n]
            out = _adamw_shard(shard("", n), shard("m_", n), shard("v_", n), lands[n], dws[n],
                               kind=kind, pad=pad, name="adamw_" + n)
            results["w_" + n] = [a.T for a in out] if n in _TRANSPOSED else out
        return results["w_" + names[-1]][0]

    names = _SCATTER_STAGES[last]
    whole = lambda w, ref: ref
    send, recv, src, land, _ = scatters[last]
    out = _split_wait(f"pair_wait{last}", len(names), whole, send, recv, src, land, dx,
                      keep_sources=True)
    sums = _pair_sum(out[:len(names)], out[len(names):], pad=_SPEC[names[0]][2], name="pair_sum")
    send, recv, src, land, after = _chip_start(last, names, sums, anchor)
    for stage in range(last):
        after = finish(stage, after)
    dfvec = _bias_grad(dbias, after, name="bias_grad")
    loss_col = jnp.pad(loss[:, :1], ((0, N_DEV - 1), (0, D_MODEL - CH_WIN - 1)))
    dfv = jnp.concatenate([dfvec[:, 0, :], loss_col], axis=1)
    small = _allreduce_small(jnp.concatenate([_stack_gains(lambda n: dg[n]), dfv], axis=0), after)
    gain_names = _GAINS + ("out_sb", "out_ch")
    gain_out = _adamw_gains(small, [(given["g_" + n], given["m_g_" + n], given["v_g_" + n])
                                    for n in gain_names])
    for r, n in enumerate(gain_names):
        results["g_" + n] = gain_out[4 * r:4 * r + 4]
    d_rel = _fvec_grad_to_rel_bias(small[N_DEV:, :CH_WIN].reshape(N_DEV, 1, CH_WIN))
    results["rel_bias"] = _adamw(rel_bias[0], m_rel_bias[0], v_rel_bias[0], d_rel,
                                 name="adamw_rel_bias")
    out = _split_wait(f"chip_wait{last}", len(names), whole, send, recv, src, land,
                      results["rel_bias"][0], keep_sources=True)
    for n, own, landed in zip(names, out[:len(names)], out[len(names):]):
        res = _adamw_chip(shard("", n), shard("m_", n), shard("v_", n), landed, own,
                          name="adamw_" + n)
        results["w_" + n] = [a.T for a in res] if n in _TRANSPOSED else res

    order = ("g_ffn1_pre", "g_ffn1_post", "w_ffn1_gate", "w_ffn1_up", "w_ffn1_down",
             "g_mix_pre", "g_mix_post", "w_in", "g_out_sb", "g_out_ch", "rel_bias", "w_out",
             "g_ffn2_pre", "g_ffn2_post", "w_ffn2_gate", "w_ffn2_up", "w_ffn2_down",
             "w_ple_proj", "w_ple_gate", "g_ple_post")

    def leaf(name, idx):
        a = results[name][idx]
        return a if name.startswith("g_") else a[None]

    total_loss = small[N_DEV, CH_WIN]
    return (total_loss, dx[None],
            *[leaf(n, 0) for n in order], *[leaf(n, 1) for n in order],
            *[leaf(n, 2) for n in order], *[leaf(n, 3) for n in order])
```

```python
import jax
import jax.numpy as jnp
from jax import lax
from jax.experimental import pallas as pl
from jax.experimental.pallas import tpu as pltpu

F32 = jnp.float32
BF16 = jnp.bfloat16

N_DEV = 8
D_MODEL = 1024
D_FF = 2816
FF_SHARD = D_FF // N_DEV
FF_SHARD_PAD = 384
D_FF_PAD = FF_SHARD_PAD * N_DEV
QKV_WIDTH = 3 * D_MODEL
QKV_SHARD = QKV_WIDTH // N_DEV
PLE_DIM = 256
ROW_SHARD = D_MODEL // N_DEV
HEAD_DIM = 64
PAIR = 2 * HEAD_DIM
N_PAIRS = 4
CHUNK = 64
LOOKBACK = 8
REL_CLIP = 128
N_REL = 2 * REL_CLIP + 1
CH_QB = 256
CH_LOOK = LOOKBACK * CHUNK
CH_WIN = CH_LOOK + CH_QB
SB_QB = 512
SB_KB = 256
SB_GROUP = 2
SB_LANES = tuple(slice(g * 128, (g + 1) * 128) for g in range(SB_GROUP))
EPS = 1e-6
NEG_INF = -1e30
ATT_SCALE = HEAD_DIM ** -0.5
ADAM_LR = 0.001
ADAM_B1 = 0.9
ADAM_B2 = 0.999
ADAM_EPS = 1e-08
ADAM_WD = 0.01
ADAM_STEP = 10
VMEM_LIMIT_BYTES = 48 * 1024 * 1024
MESH = pl.DeviceIdType.MESH

ANY = pl.BlockSpec(memory_space=pl.ANY)
VMEM = pl.BlockSpec(memory_space=pltpu.VMEM)


def _params(*sem):
    return pltpu.CompilerParams(dimension_semantics=sem or None,
                                vmem_limit_bytes=VMEM_LIMIT_BYTES)


def _sds(shape, dtype=F32):
    return jax.ShapeDtypeStruct(shape, dtype)


def _bf(x):
    return x.astype(BF16)


def _dot(a, b):
    return jnp.dot(_bf(a), _bf(b), preferred_element_type=F32)


def _dot_nt(a, b):
    return lax.dot_general(_bf(a), _bf(b), (((1,), (1,)), ((), ())),
                           preferred_element_type=F32)


def _dot_tn(a, b):
    return lax.dot_general(_bf(a), _bf(b), (((0,), (0,)), ((), ())),
                           preferred_element_type=F32)


def _sigmoid(x):
    return 1.0 / (1.0 + jnp.exp(-x))


def _softplus(x):
    return jnp.maximum(x, 0.0) + jnp.log(1.0 + jnp.exp(-jnp.abs(x)))


def _rstd(x):
    return lax.rsqrt(jnp.mean(x * x, axis=-1, keepdims=True) + EPS)


def _rms(x, g):
    return x * _rstd(x) * g


def _rms_bwd(dy, x, g):
    r = _rstd(x)
    w = dy * g
    dx = r * (w - x * (r * r) * jnp.mean(w * x, axis=-1, keepdims=True))
    dg = jnp.sum(dy * (x * r), axis=0, keepdims=True)
    return dx, dg


def _head_masks():
    lane = lax.broadcasted_iota(jnp.int32, (1, PAIR), 1)
    return lane < HEAD_DIM, lane >= HEAD_DIM


def _ffn_fwd(x, g_pre, g_post, wg, wu, wd, *, name):
    t = x.shape[0]
    tm, tj = 512, 1024
    ni, nj = t // tm, D_FF_PAD // tj

    def body(x_ref, gpre_ref, gpost_ref, wg_ref, wu_ref, wd_ref,
             h_ref, n_ref, a_ref, b_ref, f_ref, acc_ref):
        j = pl.program_id(1)

        @pl.when(j == 0)
        def _():
            n_ref[...] = _bf(_rms(x_ref[...], gpre_ref[...]))
            acc_ref[...] = jnp.zeros_like(acc_ref)

        n = n_ref[...]
        a = _dot_nt(n, wg_ref[...])
        b = _dot_nt(n, wu_ref[...])
        a_ref[...] = a
        b_ref[...] = b
        hmid = a * _sigmoid(a) * b
        acc_ref[...] += jnp.dot(_bf(hmid), wd_ref[...], preferred_element_type=F32)

        @pl.when(j == nj - 1)
        def _():
            f = acc_ref[...]
            f_ref[...] = f
            h_ref[...] = x_ref[...] + 0.5 * _rms(f, gpost_ref[...])

    row = pl.BlockSpec((tm, D_MODEL), lambda i, j: (i, 0))
    gain = pl.BlockSpec((1, D_MODEL), lambda i, j: (0, 0))
    col = pl.BlockSpec((tm, tj), lambda i, j: (i, j))
    wtile = pl.BlockSpec((tj, D_MODEL), lambda i, j: (j, 0))
    return pl.pallas_call(
        body, name=name, grid=(ni, nj),
        in_specs=[row, gain, gain, wtile, wtile, wtile],
        out_specs=[row, row, col, col, row],
        out_shape=[_sds((t, D_MODEL)), _sds((t, D_MODEL), BF16),
                   _sds((t, D_FF_PAD)), _sds((t, D_FF_PAD)), _sds((t, D_MODEL))],
        scratch_shapes=[pltpu.VMEM((tm, D_MODEL), F32)],
        compiler_params=_params("arbitrary", "arbitrary"),
    )(x, g_pre, g_post, wg, wu, wd)


def _ffn_bwd(n, df, a, b, wg, wu, wd, *, name):
    t = n.shape[0]
    tj, tm, ts = 256, t, 512
    nj, ni, ns = D_FF_PAD // tj, t // tm, tm // ts

    def body(n_hbm, df_hbm, a_ref, b_ref, wg_ref, wu_ref, wd_ref,
             dwg_ref, dwu_ref, dwd_ref, dn_hbm,
             n_v, df_v, dn_v, ag, au, ad, sem):
        j, i = pl.program_id(0), pl.program_id(1)

        @pl.when((j == 0) & (i == 0))
        def _():
            c1 = pltpu.make_async_copy(n_hbm, n_v, sem.at[0])
            c2 = pltpu.make_async_copy(df_hbm, df_v, sem.at[1])
            c1.start()
            c2.start()
            dn_v[...] = jnp.zeros_like(dn_v)
            c1.wait()
            c2.wait()

        @pl.when(i == 0)
        def _():
            ag[...] = jnp.zeros_like(ag)
            au[...] = jnp.zeros_like(au)
            ad[...] = jnp.zeros_like(ad)

        for s in range(ns):
            local = pl.ds(s * ts, ts)
            rows = pl.ds(pl.multiple_of(i * tm + s * ts, ts), ts)
            av, bv = a_ref[local, :], b_ref[local, :]
            sig = _sigmoid(av)
            silu = av * sig
            dfr = df_v[rows, :]
            nr = n_v[rows, :]
            dhmid = _dot_nt(dfr, wd_ref[...])
            da = dhmid * bv * (sig * (1.0 + av * (1.0 - sig)))
            db = dhmid * silu
            ad[...] += _dot_tn(silu * bv, dfr)
            ag[...] += _dot_tn(da, nr)
            au[...] += _dot_tn(db, nr)
            for cols in (pl.ds(0, D_MODEL // 2), pl.ds(D_MODEL // 2, D_MODEL // 2)):
                dn_v[rows, cols] += _dot(da, wg_ref[:, cols]) + _dot(db, wu_ref[:, cols])

        @pl.when(i == ni - 1)
        def _():
            dwg_ref[...] = _bf(ag[...])
            dwu_ref[...] = _bf(au[...])
            dwd_ref[...] = _bf(ad[...])

        @pl.when((j == nj - 1) & (i == ni - 1))
        def _():
            c = pltpu.make_async_copy(dn_v, dn_hbm, sem.at[0])
            c.start()
            c.wait()

    roww = pl.BlockSpec((tj, D_MODEL), lambda j, i: (j, 0))
    act = pl.BlockSpec((tm, tj), lambda j, i: (i, j))
    return pl.pallas_call(
        body, name=name, grid=(nj, ni),
        in_specs=[ANY, ANY, act, act, roww, roww, roww],
        out_specs=[roww, roww, roww, ANY],
        out_shape=[_sds((D_FF_PAD, D_MODEL), BF16)] * 3 + [_sds((t, D_MODEL))],
        scratch_shapes=[pltpu.VMEM((t, D_MODEL), BF16), pltpu.VMEM((t, D_MODEL), BF16),
                        pltpu.VMEM((t, D_MODEL), F32)]
        + [pltpu.VMEM((tj, D_MODEL), F32)] * 3 + [pltpu.SemaphoreType.DMA((2,))],
        compiler_params=_params("arbitrary", "arbitrary"),
    )(n, df, a, b, wg, wu, wd)


def _junction(dres, pre=None, post=None, *, name):
    t = dres.shape[0]
    tm = 512
    ni = t // tm
    n_in = 1 + (3 if pre else 0) + (2 if post else 0)
    coef = post[2] if post else None

    def body(*refs):
        ins, outs = list(refs[:n_in]), list(refs[n_in:])
        i = pl.program_id(0)
        dh = ins.pop(0)[...]
        if pre:
            dn_ref, x_ref, gpre_ref = ins.pop(0), ins.pop(0), ins.pop(0)
            dh_ref, dgpre_ref = outs.pop(0), outs.pop(0)
            dx, dg = _rms_bwd(dn_ref[...], x_ref[...], gpre_ref[...])
            dh = dh + dx
            dh_ref[...] = dh

            @pl.when(i == 0)
            def _():
                dgpre_ref[...] = jnp.zeros_like(dgpre_ref)
            dgpre_ref[...] += dg
        if post:
            f_ref, gpost_ref = ins.pop(0), ins.pop(0)
            df_ref, dgpost_ref = outs.pop(0), outs.pop(0)
            df, dg = _rms_bwd(coef * dh, f_ref[...], gpost_ref[...])
            df_ref[...] = _bf(df)

            @pl.when(i == 0)
            def _():
                dgpost_ref[...] = jnp.zeros_like(dgpost_ref)
            dgpost_ref[...] += dg

    row = pl.BlockSpec((tm, D_MODEL), lambda i: (i, 0))
    gain = pl.BlockSpec((1, D_MODEL), lambda i: (0, 0))
    args, in_specs, out_specs, out_shape = [dres], [row], [], []
    if pre:
        args += list(pre)
        in_specs += [row, row, gain]
        out_specs += [row, gain]
        out_shape += [_sds((t, D_MODEL)), _sds((1, D_MODEL))]
    if post:
        args += [post[0], post[1]]
        in_specs += [row, gain]
        out_specs += [row, gain]
        out_shape += [_sds((t, D_MODEL), BF16), _sds((1, D_MODEL))]
    return pl.pallas_call(
        body, name=name, grid=(ni,), in_specs=in_specs, out_specs=out_specs,
        out_shape=out_shape, compiler_params=_params("arbitrary"),
    )(*args)


def _qkv_fwd(h, g, win, *, name):
    t = h.shape[0]
    tm, tn = min(1024, t), 1024
    ni, nj = t // tm, QKV_WIDTH // tn

    def body(h_ref, g_ref, w_ref, qkv_ref, u_ref):
        @pl.when(pl.program_id(1) == 0)
        def _():
            u_ref[...] = _bf(_rms(h_ref[...], g_ref[...]))
        qkv_ref[...] = jnp.dot(u_ref[...], w_ref[...], preferred_element_type=F32)

    row = pl.BlockSpec((tm, D_MODEL), lambda i, j: (i, 0))
    return pl.pallas_call(
        body, name=name, grid=(ni, nj),
        in_specs=[row, pl.BlockSpec((1, D_MODEL), lambda i, j: (0, 0)),
                  pl.BlockSpec((D_MODEL, tn), lambda i, j: (0, j))],
        out_specs=[pl.BlockSpec((tm, tn), lambda i, j: (i, j)), row],
        out_shape=[_sds((t, QKV_WIDTH)), _sds((t, D_MODEL), BF16)],
        compiler_params=_params("arbitrary", "arbitrary"),
    )(h, g, win)


def _qkv_bwd(dq, dk, dv, u, win, *, name):
    t = u.shape[0]
    tn, ts = 512, 512
    nj, ns = QKV_WIDTH // tn, t // ts

    def body(dq_ref, dk_ref, dv_ref, u_ref, w_ref, dw_ref, du_hbm, du_v, acc_ref, sem):
        j = pl.program_id(0)

        @pl.when(j == 0)
        def _():
            du_v[...] = jnp.zeros_like(du_v)

        for role, d_ref in enumerate((dq_ref, dk_ref, dv_ref)):
            @pl.when(j % 3 == role)
            def _():
                acc_ref[...] = jnp.zeros_like(acc_ref)
                for s in range(ns):
                    rows = pl.ds(s * ts, ts)
                    dcol = d_ref[rows, :]
                    acc_ref[...] += _dot_tn(u_ref[rows, :], dcol)
                    du_v[rows, :] += _dot_nt(dcol, w_ref[...])
                dw_ref[...] = _bf(acc_ref[...])

        @pl.when(j == nj - 1)
        def _():
            c = pltpu.make_async_copy(du_v, du_hbm, sem)
            c.start()
            c.wait()

    colw = pl.BlockSpec((D_MODEL, tn), lambda j: (0, j))
    grp = pl.BlockSpec((t, tn), lambda j: (0, j // 3))
    return pl.pallas_call(
        body, name=name, grid=(nj,),
        in_specs=[grp, grp, grp, pl.BlockSpec((t, D_MODEL), lambda j: (0, 0)), colw],
        out_specs=[colw, ANY],
        out_shape=[_sds((D_MODEL, QKV_WIDTH), BF16), _sds((t, D_MODEL))],
        scratch_shapes=[pltpu.VMEM((t, D_MODEL), F32), pltpu.VMEM((D_MODEL, tn), F32),
                        pltpu.SemaphoreType.DMA],
        compiler_params=_params("arbitrary"),
    )(dq, dk, dv, u, win)


def _sb_stack(x):
    lo, hi = _head_masks()
    return jnp.concatenate([jnp.where(lo, x, 0.0), jnp.where(hi, x, 0.0)], axis=0)


def _sb_unstack(x2, blk):
    return jnp.where(_head_masks()[0], x2[:blk], x2[blk:])


def _sb_rows_from(x2, blk, r0):
    return x2 if r0 == 0 else jnp.concatenate([x2[r0:blk], x2[blk + r0:]], axis=0)


def _sb_rows_merge(full2, sub2, blk, r0):
    if r0 == 0:
        return sub2
    rows = blk - r0
    return jnp.concatenate([full2[:r0], sub2[:rows], full2[blk:blk + r0], sub2[rows:]], axis=0)


def _sb_mask(qb, kb, offset):
    r = lax.broadcasted_iota(jnp.int32, (2 * qb, kb), 0) & (qb - 1)
    c = lax.broadcasted_iota(jnp.int32, (2 * qb, kb), 1) + offset
    return c < r


def _tri(n, keep):
    r = lax.broadcasted_iota(jnp.int32, (n, n), 0)
    c = lax.broadcasted_iota(jnp.int32, (n, n), 1)
    return jnp.where(keep(r, c), 1.0, 0.0).astype(BF16)


def _cumsum01(x, u):
    m = x.shape[0]
    hi = _bf(x)
    lo = _bf(x - hi.astype(F32))
    both = jnp.dot(jnp.concatenate([hi, lo], axis=0), u, preferred_element_type=F32)
    return both[:m] + both[m:]


def _sb_fwd(qkv, *, name):
    t = qkv.shape[0]
    blk, kb = min(SB_QB, t), SB_KB
    ni, per = t // blk, blk // kb

    def body(q_ref, k_ref, v_ref, o_ref, ltot_ref):
        i = pl.program_id(1)
        u_after = _tri(kb, lambda r, c: r > c)
        q2 = [_bf(_sb_stack(q_ref[:, lanes] * ATT_SCALE)) for lanes in SB_LANES]

        def tile(g, k0, mask, acc, c_l):
            kj = k_ref[pl.ds(k0, kb), SB_LANES[g]]
            vj = v_ref[pl.ds(k0, kb), SB_LANES[g]]
            z = _dot_nt(q2[g], kj)
            sp = _softplus(z)
            lf = -sp if mask is None else jnp.where(mask, -sp, 0.0)
            a = jnp.exp(z - sp + _cumsum01(lf, u_after) + c_l)
            if mask is not None:
                a = jnp.where(mask, a, 0.0)
            return acc + _dot(a, vj), c_l + jnp.sum(lf, axis=1, keepdims=True)

        def tiles(k0, mask, carry):
            return tuple(tile(g, k0, mask, *carry[g]) for g in range(SB_GROUP))

        carry = ((jnp.zeros((2 * blk, PAIR), F32), jnp.zeros((2 * blk, 1), F32)),) * SB_GROUP
        for d in reversed(range(per)):
            carry = tiles(pl.multiple_of(i * blk + d * kb, kb), _sb_mask(blk, kb, d * kb), carry)
        carry = lax.fori_loop(
            1, per * i + 1,
            lambda jj, c: tiles(pl.multiple_of((per * i - jj) * kb, kb), None, c), carry)
        for g, (acc, c_l) in enumerate(carry):
            o_ref[:, SB_LANES[g]] = _sb_unstack(acc, blk)
            ltot_ref[:, SB_LANES[g]] = _sb_unstack(jnp.broadcast_to(c_l, (2 * blk, PAIR)), blk)

    width = SB_GROUP * PAIR
    blkspec = pl.BlockSpec((blk, width), lambda p, i: (i, p))
    n_steps = N_PAIRS // SB_GROUP
    return pl.pallas_call(
        body, name=name, grid=(n_steps, ni),
        in_specs=[blkspec,
                  pl.BlockSpec((t, width), lambda p, i: (0, n_steps + p)),
                  pl.BlockSpec((t, width), lambda p, i: (0, 2 * n_steps + p))],
        out_specs=[blkspec, blkspec],
        out_shape=[_sds((t, D_MODEL)), _sds((t, D_MODEL // 2))],
        compiler_params=_params("arbitrary", "arbitrary"),
    )(qkv, qkv, qkv)


def _sb_bwd(qkv, ltot, do, *, name):
    t = qkv.shape[0]
    blk, kb = min(SB_QB, t), SB_KB
    ni, per = t // blk, blk // kb

    def body(q_ref, k_ref, v_ref, lt_ref, do_ref, dq_ref, dkout_ref, dvout_ref, dk_ref, dv_ref):
        i = pl.program_id(1)

        @pl.when(i == 0)
        def _():
            dk_ref[...] = jnp.zeros_like(dk_ref)
            dv_ref[...] = jnp.zeros_like(dv_ref)

        u_upto = _tri(kb, lambda r, c: r <= c)
        u_before = _tri(kb, lambda r, c: r < c)
        lane = lax.broadcasted_iota(jnp.int32, (1, PAIR), 1)
        q2 = [_bf(_sb_stack(q_ref[:, lanes] * ATT_SCALE)) for lanes in SB_LANES]
        do2 = [_bf(_sb_stack(do_ref[:, lanes])) for lanes in SB_LANES]
        total = [jnp.concatenate(
            [jnp.sum(jnp.where(lane == h * HEAD_DIM, lt_ref[:, lanes], 0.0), axis=1, keepdims=True)
             for h in range(2)], axis=0) for lanes in SB_LANES]

        def tile(g, ops, k0, mask, dq_acc, c_l, c_g):
            qg, dog, tot = ops
            krows = pl.ds(k0, kb)
            kj = k_ref[krows, SB_LANES[g]]
            vj = v_ref[krows, SB_LANES[g]]
            z = _dot_nt(qg, kj)
            sp = _softplus(z)
            sig = jnp.exp(z - sp)
            lf = -sp if mask is None else jnp.where(mask, -sp, 0.0)
            a = jnp.exp(z - sp + tot - (_cumsum01(lf, u_upto) + c_l))
            if mask is not None:
                a = jnp.where(mask, a, 0.0)
            gw = a * _dot_nt(dog, vj)
            g_before = jnp.dot(_bf(gw), u_before, preferred_element_type=F32) + c_g
            dz = gw * (1.0 - sig) - g_before * sig
            if mask is not None:
                dz = jnp.where(mask, dz, 0.0)
            dk_ref[krows, SB_LANES[g]] += _dot_tn(dz, qg)
            dv_ref[krows, SB_LANES[g]] += _dot_tn(a, dog)
            return (dq_acc + _dot(dz, kj), c_l + jnp.sum(lf, axis=1, keepdims=True),
                    c_g + jnp.sum(gw, axis=1, keepdims=True))

        def tiles(ops, k0, mask, carry):
            return tuple(tile(g, ops[g], k0, mask, *carry[g]) for g in range(SB_GROUP))

        ops = tuple(zip(q2, do2, total))
        zero = (jnp.zeros((2 * blk, PAIR), F32), jnp.zeros((2 * blk, 1), F32),
                jnp.zeros((2 * blk, 1), F32))
        carry = lax.fori_loop(
            0, per * i, lambda j, c: tiles(ops, pl.multiple_of(j * kb, kb), None, c),
            (zero,) * SB_GROUP)
        for d in range(per):
            r0 = d * kb
            sub = tiles(tuple(tuple(_sb_rows_from(a, blk, r0) for a in o) for o in ops),
                        pl.multiple_of(i * blk + r0, kb), _sb_mask(blk - r0, kb, 0),
                        tuple(tuple(_sb_rows_from(a, blk, r0) for a in c) for c in carry))
            carry = tuple(tuple(_sb_rows_merge(a, s, blk, r0) for a, s in zip(c, cs))
                          for c, cs in zip(carry, sub))
        for g, (dq_acc, _, _) in enumerate(carry):
            dq_ref[:, SB_LANES[g]] = _bf(_sb_unstack(dq_acc, blk) * ATT_SCALE)

        @pl.when(i == ni - 1)
        def _():
            dkout_ref[...] = _bf(dk_ref[...])
            dvout_ref[...] = _bf(dv_ref[...])

    width = SB_GROUP * PAIR
    n_steps = N_PAIRS // SB_GROUP
    blkspec = lambda off: pl.BlockSpec((blk, width), lambda p, i: (i, off + p))
    full = lambda off: pl.BlockSpec((t, width), lambda p, i: (0, off + p))
    return pl.pallas_call(
        body, name=name, grid=(n_steps, ni),
        in_specs=[blkspec(0), full(n_steps), full(2 * n_steps), blkspec(0), blkspec(0)],
        out_specs=[blkspec(0), full(0), full(0)],
        out_shape=[_sds((t, D_MODEL), BF16)] * 3,
        scratch_shapes=[pltpu.VMEM((t, width), F32), pltpu.VMEM((t, width), F32)],
        compiler_params=_params("arbitrary", "arbitrary"),
    )(qkv, qkv, qkv, ltot, do)


def _ch_mask(i):
    c = lax.broadcasted_iota(jnp.int32, (1, CH_WIN), 1)
    return c >= CH_LOOK - i * CH_QB


def _ch_band(row0, rows):
    r = row0 + lax.broadcasted_iota(jnp.int32, (rows, CH_WIN), 0)
    c = lax.broadcasted_iota(jnp.int32, (rows, CH_WIN), 1)
    qc = LOOKBACK + lax.shift_right_arithmetic(r, 6)
    kc = lax.shift_right_arithmetic(c, 6)
    return (kc <= qc) & (kc >= qc - LOOKBACK)


def _ch_probs(qm, kw, bias_h, mask):
    z = _dot_nt(qm, kw) * ATT_SCALE + bias_h
    z = jnp.where(mask, z, NEG_INF)
    e = jnp.exp(z - jnp.max(z, axis=1, keepdims=True))
    return e * (1.0 / jnp.sum(e, axis=1, keepdims=True))


def _ch_fill(pad_ref, src_ref, t):
    pad_ref[pl.ds(0, CH_LOOK), :] = jnp.zeros((CH_LOOK, PAIR), BF16)
    pad_ref[pl.ds(CH_LOOK, t), :] = _bf(src_ref[...])


def _ch_fwd(qkv, bias, o_in, *, name):
    t = qkv.shape[0]
    ni = t // CH_QB

    def body(q_ref, k_ref, v_ref, bias_ref, _alias, o_ref, kpad, vpad):
        i = pl.program_id(1)

        @pl.when(i == 0)
        def _():
            _ch_fill(kpad, k_ref, t)
            _ch_fill(vpad, v_ref, t)

        win = pl.ds(pl.multiple_of(i * CH_QB, CH_QB), CH_WIN)
        kw, vw = kpad[win, :], vpad[win, :]
        mask = _ch_mask(i)
        q = q_ref[...]
        outs = []
        for h, hm in enumerate(_head_masks()):
            p = _ch_probs(jnp.where(hm, q, 0.0), kw, bias_ref[h], mask)
            outs.append(_dot(p, vw))
        o_ref[...] = jnp.where(_head_masks()[0], outs[0], outs[1])

    full = lambda off: pl.BlockSpec((t, PAIR), lambda p, i: (0, off + p))
    return pl.pallas_call(
        body, name=name, grid=(N_PAIRS, ni),
        in_specs=[pl.BlockSpec((CH_QB, PAIR), lambda p, i: (i, 3 * N_PAIRS + p)),
                  full(4 * N_PAIRS), full(5 * N_PAIRS),
                  pl.BlockSpec((2, CH_QB, CH_WIN), lambda p, i: (p, 0, 0)), ANY],
        out_specs=pl.BlockSpec((CH_QB, PAIR), lambda p, i: (i, N_PAIRS + p)),
        out_shape=_sds((t, D_MODEL)),
        scratch_shapes=[pltpu.VMEM((t + CH_LOOK, PAIR), BF16)] * 2,
        input_output_aliases={4: 0},
        compiler_params=_params("arbitrary", "arbitrary"),
    )(qkv, qkv, qkv, bias, o_in)


def _ch_bwd(qkv, bias, o, do, dq_in, dk_in, dv_in, *, name):
    t = qkv.shape[0]
    ni = t // CH_QB

    def body(q_ref, k_ref, v_ref, bias_ref, o_ref, do_ref, _a0, _a1, _a2,
             dq_ref, dkout_ref, dvout_ref, dbias_ref, kpad, vpad, dkpad, dvpad):
        i = pl.program_id(1)

        @pl.when(i == 0)
        def _():
            _ch_fill(kpad, k_ref, t)
            _ch_fill(vpad, v_ref, t)
            dkpad[...] = jnp.zeros_like(dkpad)
            dvpad[...] = jnp.zeros_like(dvpad)
            dbias_ref[...] = jnp.zeros_like(dbias_ref)

        win = pl.ds(pl.multiple_of(i * CH_QB, CH_QB), CH_WIN)
        kw, vw = kpad[win, :], vpad[win, :]
        mask = _ch_mask(i)
        q, o_blk, do_blk = q_ref[...], o_ref[...], do_ref[...]
        dqs = []
        for h, hm in enumerate(_head_masks()):
            qm = _bf(jnp.where(hm, q, 0.0))
            dom = jnp.where(hm, do_blk, 0.0)
            delta = jnp.sum(dom * o_blk, axis=1, keepdims=True)
            dom = _bf(dom)
            p = _ch_probs(qm, kw, bias_ref[h], mask)
            ds = p * (_dot_nt(dom, vw) - delta)
            dbias_ref[h] += ds
            dsz = ds * ATT_SCALE
            dqs.append(_dot(dsz, kw))
            dkpad[win, :] += _dot_tn(dsz, qm)
            dvpad[win, :] += _dot_tn(p, dom)
        dq_ref[...] = _bf(jnp.where(_head_masks()[0], dqs[0], dqs[1]))

        @pl.when(i == ni - 1)
        def _():
            dkout_ref[...] = _bf(dkpad[pl.ds(CH_LOOK, t), :])
            dvout_ref[...] = _bf(dvpad[pl.ds(CH_LOOK, t), :])

    blkspec = lambda off: pl.BlockSpec((CH_QB, PAIR), lambda p, i: (i, off + p))
    full = lambda off: pl.BlockSpec((t, PAIR), lambda p, i: (0, off + p))
    bias_spec = pl.BlockSpec((2, CH_QB, CH_WIN), lambda p, i: (p, 0, 0))
    return pl.pallas_call(
        body, name=name, grid=(N_PAIRS, ni),
        in_specs=[blkspec(3 * N_PAIRS), full(4 * N_PAIRS), full(5 * N_PAIRS), bias_spec,
                  blkspec(N_PAIRS), blkspec(N_PAIRS), ANY, ANY, ANY],
        out_specs=[blkspec(N_PAIRS), full(N_PAIRS), full(N_PAIRS), bias_spec],
        out_shape=[_sds((t, D_MODEL), BF16)] * 3 + [_sds((2 * N_PAIRS, CH_QB, CH_WIN))],
        scratch_shapes=[pltpu.VMEM((t + CH_LOOK, PAIR), BF16)] * 2
        + [pltpu.VMEM((t + CH_LOOK, PAIR), F32)] * 2,
        input_output_aliases={6: 0, 7: 1, 8: 2},
        compiler_params=_params("arbitrary", "arbitrary"),
    )(qkv, qkv, qkv, bias, o, do, dq_in, dk_in, dv_in)


def _bias_expand(fvec, *, name):
    n_heads = fvec.shape[0]

    def body(f_ref, o_ref, rows8):
        row = f_ref[0]
        for r in range(8):
            rows8[pl.ds(r, 1), :] = pltpu.roll(row, r, 1)
        base = rows8[...]
        for blk in range(CH_QB // 8):
            o_ref[0, pl.ds(8 * blk, 8), :] = jnp.where(
                _ch_band(8 * blk, 8), pltpu.roll(base, 8 * blk, 1), NEG_INF)

    return pl.pallas_call(
        body, name=name, grid=(n_heads,),
        in_specs=[pl.BlockSpec((1, 1, CH_WIN), lambda h: (h, 0, 0))],
        out_specs=pl.BlockSpec((1, CH_QB, CH_WIN), lambda h: (h, 0, 0)),
        out_shape=_sds((n_heads, CH_QB, CH_WIN)),
        scratch_shapes=[pltpu.VMEM((8, CH_WIN), F32)],
        compiler_params=_params("arbitrary"),
    )(fvec)


def _bias_grad(dbias, after, *, name):
    n_heads = dbias.shape[0]
    first = CH_LOOK - REL_CLIP

    def body(d_ref, _after, o_ref, acc8):
        acc = jnp.zeros((8, CH_WIN), F32)
        for blk in range(CH_QB // 8):
            acc = acc + pltpu.roll(d_ref[0, pl.ds(8 * blk, 8), :], (CH_WIN - 8 * blk) % CH_WIN, 1)
        acc8[...] = acc
        dvec = jnp.zeros((1, CH_WIN), F32)
        for r in range(8):
            dvec = dvec + pltpu.roll(acc8[pl.ds(r, 1), :], (CH_WIN - r) % CH_WIN, 1)
        lane = lax.broadcasted_iota(jnp.int32, (1, CH_WIN), 1)
        clipped = (lane <= first) | (lane >= first + REL_CLIP + CHUNK)
        total = jnp.sum(jnp.where(clipped, dvec, 0.0), axis=1, keepdims=True)
        o_ref[0] = jnp.where(lane == first, total, dvec)

    return pl.pallas_call(
        body, name=name, grid=(n_heads,),
        in_specs=[pl.BlockSpec((1, CH_QB, CH_WIN), lambda h: (h, 0, 0)), ANY],
        out_specs=pl.BlockSpec((1, 1, CH_WIN), lambda h: (h, 0, 0)),
        out_shape=_sds((n_heads, 1, CH_WIN)),
        scratch_shapes=[pltpu.VMEM((8, CH_WIN), F32)],
        compiler_params=_params("arbitrary"),
    )(dbias, after)


def _out_fwd(o, h1, g_sb, g_ch, g_post, wout, *, name):
    t = o.shape[0]
    tm = 512
    half = D_MODEL // 2

    def body(o_ref, h_ref, gsb_ref, gch_ref, gpost_ref, w_ref, h2_ref, mixed_ref, y_ref):
        ov = o_ref[...]
        mixed = jnp.concatenate([_rms(ov[:, :half], gsb_ref[...]),
                                 _rms(ov[:, half:], gch_ref[...])], axis=1)
        mixed_ref[...] = _bf(mixed)
        y = _dot(mixed, w_ref[...])
        y_ref[...] = y
        h2_ref[...] = h_ref[...] + _rms(y, gpost_ref[...])

    row = pl.BlockSpec((tm, D_MODEL), lambda i: (i, 0))
    gain = lambda n: pl.BlockSpec((1, n), lambda i: (0, 0))
    return pl.pallas_call(
        body, name=name, grid=(t // tm,),
        in_specs=[row, row, gain(half), gain(half), gain(D_MODEL),
                  pl.BlockSpec((D_MODEL, D_MODEL), lambda i: (0, 0))],
        out_specs=[row, row, row],
        out_shape=[_sds((t, D_MODEL)), _sds((t, D_MODEL), BF16), _sds((t, D_MODEL))],
        compiler_params=_params("arbitrary"),
    )(o, h1, g_sb, g_ch, g_post, wout)


def _out_bwd(dy, mixed, o, g_sb, g_ch, wout, *, name):
    t = o.shape[0]
    tm = 512
    ni = t // tm
    half = D_MODEL // 2

    def body(dy_ref, mixed_ref, o_ref, gsb_ref, gch_ref, w_ref,
             dw_ref, do_ref, dgsb_ref, dgch_ref, acc_ref):
        i = pl.program_id(0)

        @pl.when(i == 0)
        def _():
            acc_ref[...] = jnp.zeros_like(acc_ref)
            dgsb_ref[...] = jnp.zeros_like(dgsb_ref)
            dgch_ref[...] = jnp.zeros_like(dgch_ref)

        dyv = dy_ref[...]
        acc_ref[...] += _dot_tn(mixed_ref[...], dyv)
        dm = _dot_nt(dyv, w_ref[...])
        ov = o_ref[...]
        doa, dga = _rms_bwd(dm[:, :half], ov[:, :half], gsb_ref[...])
        dob, dgb = _rms_bwd(dm[:, half:], ov[:, half:], gch_ref[...])
        do_ref[...] = jnp.concatenate([doa, dob], axis=1)
        dgsb_ref[...] += dga
        dgch_ref[...] += dgb

        @pl.when(i == ni - 1)
        def _():
            dw_ref[...] = _bf(acc_ref[...])

    row = pl.BlockSpec((tm, D_MODEL), lambda i: (i, 0))
    gain = pl.BlockSpec((1, half), lambda i: (0, 0))
    sq = pl.BlockSpec((D_MODEL, D_MODEL), lambda i: (0, 0))
    return pl.pallas_call(
        body, name=name, grid=(ni,),
        in_specs=[row, row, row, gain, gain, sq],
        out_specs=[sq, row, gain, gain],
        out_shape=[_sds((D_MODEL, D_MODEL), BF16), _sds((t, D_MODEL)),
                   _sds((1, half)), _sds((1, half))],
        scratch_shapes=[pltpu.VMEM((D_MODEL, D_MODEL), F32)],
        compiler_params=_params("arbitrary"),
    )(dy, mixed, o, g_sb, g_ch, wout)


def _ple(p, h3, target, wp, wgate, g, f_post, g_post, *, name):
    t = h3.shape[0]
    tm = 512
    ni = t // tm

    def body(p_ref, h_ref, tgt_ref, wp_ref, wg_ref, g_ref, f_ref, gf_ref,
             loss_ref, dres_ref, dwp_ref, dwg_ref, dg_ref, df_ref, dgf_ref, accp, accg):
        i = pl.program_id(0)

        @pl.when(i == 0)
        def _():
            loss_ref[...] = jnp.zeros_like(loss_ref)
            dg_ref[...] = jnp.zeros_like(dg_ref)
            dgf_ref[...] = jnp.zeros_like(dgf_ref)
            accp[...] = jnp.zeros_like(accp)
            accg[...] = jnp.zeros_like(accg)

        pv, hv, gv = p_ref[...], h_ref[...], g_ref[...]
        pe = _dot(pv, wp_ref[...])
        sig = _sigmoid(_dot(hv, wg_ref[...]))
        e = pe * sig
        err = hv + _rms(e, gv) - tgt_ref[...]
        tok = jnp.mean(err * err, axis=-1, keepdims=True)
        loss_ref[...] += 0.5 * jnp.sum(tok, axis=0, keepdims=True)
        dh4 = err * (1.0 / D_MODEL)
        de, dg = _rms_bwd(dh4, e, gv)
        dg_ref[...] += dg
        dpe = de * sig
        dgt = de * pe * sig * (1.0 - sig)
        accp[...] += _dot_tn(pv, dpe)
        accg[...] += _dot_tn(hv, dgt)
        dres = dh4 + _dot_nt(dgt, wg_ref[...])
        dres_ref[...] = dres
        df, dgf = _rms_bwd(0.5 * dres, f_ref[...], gf_ref[...])
        df_ref[...] = _bf(df)
        dgf_ref[...] += dgf

        @pl.when(i == ni - 1)
        def _():
            dwp_ref[...] = _bf(accp[...])
            dwg_ref[...] = _bf(accg[...])

    row = pl.BlockSpec((tm, D_MODEL), lambda i: (i, 0))
    const = lambda r, c: pl.BlockSpec((r, c), lambda i: (0, 0))
    return pl.pallas_call(
        body, name=name, grid=(ni,),
        in_specs=[pl.BlockSpec((tm, PLE_DIM), lambda i: (i, 0)), row, row,
                  const(PLE_DIM, D_MODEL), const(D_MODEL, D_MODEL), const(1, D_MODEL),
                  row, const(1, D_MODEL)],
        out_specs=[const(1, 128), row, const(PLE_DIM, D_MODEL), const(D_MODEL, D_MODEL),
                   const(1, D_MODEL), row, const(1, D_MODEL)],
        out_shape=[_sds((1, 128)), _sds((t, D_MODEL)), _sds((PLE_DIM, D_MODEL), BF16),
                   _sds((D_MODEL, D_MODEL), BF16), _sds((1, D_MODEL)),
                   _sds((t, D_MODEL), BF16), _sds((1, D_MODEL))],
        scratch_shapes=[pltpu.VMEM((PLE_DIM, D_MODEL), F32), pltpu.VMEM((D_MODEL, D_MODEL), F32)],
        compiler_params=_params("arbitrary"),
    )(p, h3, target, wp, wgate, g, f_post, g_post)


def _rel_bias_to_fvec(rel_bias):
    rev = rel_bias[:, ::-1]
    n_heads = rel_bias.shape[0]
    first = CH_LOOK - REL_CLIP
    n_var = REL_CLIP + CHUNK
    clipped = rev[:, :1]
    fvec = jnp.concatenate([jnp.broadcast_to(clipped, (n_heads, first)), rev[:, :n_var],
                            jnp.broadcast_to(clipped, (n_heads, CH_WIN - first - n_var))], axis=1)
    return fvec.reshape(n_heads, 1, CH_WIN)


def _fvec_grad_to_rel_bias(dfvec):
    first = CH_LOOK - REL_CLIP
    n_var = REL_CLIP + CHUNK
    rev = jnp.pad(dfvec[:, 0, first:first + n_var], ((0, 0), (0, N_REL - n_var)))
    return rev[:, ::-1]


def _local_step(x, p, target, g, weights_for, grads_done, fvec, weights_early=None):
    bias = _bias_expand(fvec, name="bias_expand")
    w, tie = weights_for(0, bias)
    w = dict(w)
    h1, n1, a1, b1, f1 = _ffn_fwd(x, g["ffn1_pre"] + tie, g["ffn1_post"],
                                  w["ffn1_gate"], w["ffn1_up"], w["ffn1_down"], name="ffn1_fwd")
    more, tie = weights_for(1, h1)
    w.update(more)
    qkv, u = _qkv_fwd(h1, g["mix_pre"] + tie, w["in"], name="qkv_fwd")
    o, ltot = _sb_fwd(qkv, name="sb_fwd")
    tie = weights_early(2, ltot) if weights_early else 0.0
    o = _ch_fwd(qkv, bias, o, name="ch_fwd")
    w.update(weights_for(2, o)[0])
    h2, mixed, y = _out_fwd(o, h1, g["out_sb"] + tie, g["out_ch"], g["mix_post"], w["out"],
                            name="out_fwd")
    h3, n2, a2, b2, f2 = _ffn_fwd(h2, g["ffn2_pre"], g["ffn2_post"],
                                  w["ffn2_gate"], w["ffn2_up"], w["ffn2_down"], name="ffn2_fwd")
    loss, dh3, dwp, dwgate, dg_ple, df2, dg_ffn2_post = _ple(
        p, h3, target, w["ple_proj"], w["ple_gate"], g["ple_post"], f2, g["ffn2_post"], name="ple")
    tie = grads_done(0, {"ple_proj": dwp, "ple_gate": dwgate})
    dwg2, dwu2, dwd2, dn2 = _ffn_bwd(n2, df2, a2, b2, w["ffn2_gate"], w["ffn2_up"],
                                     w["ffn2_down"], name="ffn2_bwd")
    tie = tie + grads_done(1, {"ffn2_gate": dwg2, "ffn2_up": dwu2, "ffn2_down": dwd2})
    dh2, dg_ffn2_pre, dy, dg_mix_post = _junction(
        dh3, pre=(dn2, h2, g["ffn2_pre"] + tie), post=(y, g["mix_post"], 1.0), name="junction2")
    dwout, do, dg_sb, dg_ch = _out_bwd(dy, mixed, o, g["out_sb"], g["out_ch"], w["out"],
                                       name="out_bwd")
    dq, dk, dv = _sb_bwd(qkv, ltot, do, name="sb_bwd")
    dq, dk, dv, dbias = _ch_bwd(qkv, bias, o, do, dq, dk, dv, name="ch_bwd")
    dwin, du = _qkv_bwd(dq, dk, dv, u, w["in"], name="qkv_bwd")
    tie = grads_done(2, {"out": dwout, "in": dwin})
    dh1, dg_mix_pre, df1, dg_ffn1_post = _junction(
        dh2, pre=(du, h1, g["mix_pre"] + tie), post=(f1, g["ffn1_post"], 0.5), name="junction1")
    dwg1, dwu1, dwd1, dn1 = _ffn_bwd(n1, df1, a1, b1, w["ffn1_gate"], w["ffn1_up"],
                                     w["ffn1_down"], name="ffn1_bwd")
    tie = grads_done(3, {"ffn1_gate": dwg1, "ffn1_up": dwu1, "ffn1_down": dwd1})
    dx, dg_ffn1_pre = _junction(dh1, pre=(dn1, x, g["ffn1_pre"] + tie), name="junction0")

    dg = {"ffn1_pre": dg_ffn1_pre, "ffn1_post": dg_ffn1_post, "mix_pre": dg_mix_pre,
          "mix_post": dg_mix_post, "out_sb": dg_sb, "out_ch": dg_ch,
          "ffn2_pre": dg_ffn2_pre, "ffn2_post": dg_ffn2_post, "ple_post": dg_ple}
    return loss, dx, dg, dbias


_WEIGHTS = (
    ("ffn1_gate", "row", FF_SHARD, FF_SHARD_PAD, D_MODEL),
    ("ffn1_up", "row", FF_SHARD, FF_SHARD_PAD, D_MODEL),
    ("ffn1_down", "row", FF_SHARD, FF_SHARD_PAD, D_MODEL),
    ("in", "col", QKV_SHARD, QKV_SHARD, D_MODEL),
    ("out", "row", ROW_SHARD, ROW_SHARD, D_MODEL),
    ("ffn2_gate", "row", FF_SHARD, FF_SHARD_PAD, D_MODEL),
    ("ffn2_up", "row", FF_SHARD, FF_SHARD_PAD, D_MODEL),
    ("ffn2_down", "row", FF_SHARD, FF_SHARD_PAD, D_MODEL),
    ("ple_proj", "col", ROW_SHARD, ROW_SHARD, PLE_DIM),
    ("ple_gate", "row", ROW_SHARD, ROW_SHARD, D_MODEL),
)
_TRANSPOSED = ("ffn1_gate", "ffn1_up", "ffn2_gate", "ffn2_up")
_SPEC = {n: (kind, valid, pad, other) for n, kind, valid, pad, other in _WEIGHTS}
_GATHER_STAGES = (("ffn1_gate", "ffn1_up", "ffn1_down"), ("in",),
                  ("out", "ffn2_gate", "ffn2_up", "ffn2_down", "ple_proj", "ple_gate"))
_SCATTER_STAGES = (("ple_proj", "ple_gate"), ("ffn2_gate", "ffn2_up", "ffn2_down"),
                   ("out", "in"), ("ffn1_gate", "ffn1_up", "ffn1_down"))
HBM = pl.BlockSpec(memory_space=pltpu.HBM)
SEM = pl.BlockSpec(memory_space=pltpu.SEMAPHORE)
EFFECT = pltpu.SideEffectType.DATAFLOW_SIDE_EFFECTING


def _shard_shape(kind, size, other):
    return (other, size) if kind == "col" else (size, other)


def _window(ref, kind, start, size):
    return ref.at[:, pl.ds(start, size)] if kind == "col" else ref.at[pl.ds(start, size), :]


def _device_tuple(k):
    return (k // 4, (k // 2) % 2, k % 2)


def _my_index():
    return 4 * lax.axis_index("x") + 2 * lax.axis_index("y") + lax.axis_index("c")


def _pack_weights(names, shards, after, *, name):
    nw = len(names)
    specs = [_SPEC[n] for n in names]

    def body(*refs):
        ins, packed, full = refs[:nw], refs[nw + 1:2 * nw + 1], refs[2 * nw + 1:3 * nw + 1]
        sem = refs[3 * nw + 1]
        me = _my_index()
        for (kind, valid, pad, _), src, dst in zip(specs, ins, packed):
            if pad != valid:
                dst[...] = jnp.zeros_like(dst)
            if kind == "col":
                dst[:, pl.ds(0, valid)] = _bf(src[...])
            else:
                dst[pl.ds(0, valid), :] = _bf(src[...])
        for k in range(N_DEV):
            @pl.when(me == k)
            def _():
                for w, (kind, _, pad, _) in enumerate(specs):
                    pltpu.make_async_copy(packed[w], _window(full[w], kind, k * pad, pad),
                                          sem.at[w]).start()
        for w, (kind, _, pad, _) in enumerate(specs):
            pltpu.make_async_copy(packed[w], _window(full[w], kind, 0, pad), sem.at[w]).wait()

    whole = lambda shape: pl.BlockSpec(shape, lambda i: (0, 0))
    packed_shapes = [_shard_shape(kind, pad, other) for kind, _, pad, other in specs]
    outs = pl.pallas_call(
        body, name=name, grid=(1,),
        in_specs=[whole(a.shape) for a in shards] + [ANY],
        out_specs=[whole(s) for s in packed_shapes] + [ANY] * nw,
        out_shape=[_sds(s, BF16) for s in packed_shapes]
        + [_sds(_shard_shape(kind, N_DEV * pad, other), BF16) for kind, _, pad, other in specs],
        scratch_shapes=[pltpu.SemaphoreType.DMA((nw,))],
        compiler_params=_params("arbitrary"),
    )(*shards, after)
    return dict(zip(names, outs[:nw])), dict(zip(names, outs[nw:]))


def _hbm(a):
    return pltpu.with_memory_space_constraint(a, pltpu.HBM)


def _split_start(name, n, body_copies, sources, lands, after):
    arrays = list(sources) + list(lands)
    ns, na = len(sources), len(arrays)

    def body(*refs):
        src, land = refs[:ns], refs[ns:na]
        send, recv = refs[na + 1], refs[na + 2]
        token = refs[-1]
        body_copies(src, land, send, recv)
        token[...] = jnp.zeros_like(token)

    out = pl.pallas_call(
        body, name=name,
        out_shape=(pltpu.SemaphoreType.DMA((n,)), pltpu.SemaphoreType.DMA((n,)),
                   *[pltpu.HBM(a.shape, a.dtype) for a in arrays], _sds((8, 128))),
        in_specs=[HBM] * na + [ANY], out_specs=(SEM, SEM, *[HBM] * na, VMEM),
        input_output_aliases={i: 2 + i for i in range(na)},
        compiler_params=pltpu.CompilerParams(has_side_effects=EFFECT),
    )(*[_hbm(a) for a in arrays], after)
    return out[0], out[1], out[2:2 + ns], out[2 + ns:2 + na], out[-1]


def _split_wait(name, n, seven_of, send, recv, sources, lands, after, keep_sources=False):
    arrays = list(sources) + list(lands)
    ns, na = len(sources), len(arrays)

    def body(*refs):
        land = refs[ns:na]
        send_ref, recv_ref = refs[na], refs[na + 1]
        myself = (lax.axis_index("x"), lax.axis_index("y"), lax.axis_index("c"))
        for w in range(n):
            seven = seven_of(w, land[w])
            copy = pltpu.make_async_remote_copy(
                src_ref=seven, dst_ref=seven, send_sem=send_ref.at[w], recv_sem=recv_ref.at[w],
                device_id=myself, device_id_type=MESH)
            copy.wait_send()
            copy.wait_recv()

    afters = tuple(after) if isinstance(after, (tuple, list)) else (after,)
    out = pl.pallas_call(
        body, name=name,
        out_shape=[pltpu.HBM(a.shape, a.dtype) for a in arrays],
        in_specs=[HBM] * na + [SEM, SEM] + [ANY] * len(afters), out_specs=[HBM] * na,
        input_output_aliases={i: i for i in range(na)},
        compiler_params=pltpu.CompilerParams(has_side_effects=EFFECT),
    )(*arrays, send, recv, *afters)
    return out if keep_sources else out[ns:]


_ALL_PEERS = (1, 2, 3, 4, 5, 6, 7)
_NEAR_PEERS = (1, 2, 4, 6)
_FAR_CHIPS = (2, 4, 6)


def _gather_start(stage, names, packed, full, after, peers=_ALL_PEERS):
    def copies(src, land, send, recv):
        me = _my_index()
        for k in range(N_DEV):
            @pl.when(me == k)
            def _():
                for w, name in enumerate(names):
                    kind, _, pad, _ = _SPEC[name]
                    dst = _window(land[w], kind, k * pad, pad)
                    for mask in peers:
                        pltpu.make_async_remote_copy(
                            src_ref=src[w], dst_ref=dst, send_sem=send.at[w],
                            recv_sem=recv.at[w], device_id=_device_tuple(k ^ mask),
                            device_id_type=MESH).start()

    return _split_start(f"gather_start{stage}", len(names), copies,
                        [packed[n] for n in names], [full[n] for n in names], after)


def _gather_wait(stage, names, started, after, count=N_DEV - 1):
    send, recv, src, land, _ = started

    def bytes_of(w, ref):
        kind, _, pad, _ = _SPEC[names[w]]
        return _window(ref, kind, 0, count * pad)

    return dict(zip(names, _split_wait(f"gather_wait{stage}", len(names), bytes_of,
                                       send, recv, src, land, after)))


def _relay_start(stage, names, full, after):
    def copies(_, land, send, recv):
        me = _my_index()
        for k in range(N_DEV):
            @pl.when(me == k)
            def _():
                for w, name in enumerate(names):
                    kind, _, pad, _ = _SPEC[name]
                    for mask in _FAR_CHIPS:
                        win = _window(land[w], kind, (k ^ mask) * pad, pad)
                        pltpu.make_async_remote_copy(
                            src_ref=win, dst_ref=win, send_sem=send.at[w], recv_sem=recv.at[w],
                            device_id=_device_tuple(k ^ 1), device_id_type=MESH).start()

    return _split_start(f"relay_start{stage}", len(names), copies, [],
                        [full[n] for n in names], after)


def _scatter_start(stage, names, grads, after):
    def copies(src, land, send, recv):
        me = _my_index()
        for k in range(N_DEV):
            @pl.when(me != k)
            def _():
                slot = lax.rem(me + (N_DEV - 1 - k), N_DEV)
                for w, name in enumerate(names):
                    kind, _, pad, _ = _SPEC[name]
                    pltpu.make_async_remote_copy(
                        src_ref=_window(src[w], kind, k * pad, pad), dst_ref=land[w].at[slot],
                        send_sem=send.at[w], recv_sem=recv.at[w],
                        device_id=_device_tuple(k), device_id_type=MESH).start()

    lands = [lax.empty((N_DEV - 1,) + _shard_shape(_SPEC[m][0], _SPEC[m][2], _SPEC[m][3]), BF16)
             for m in names]
    return _split_start(f"scatter_start{stage}", len(names), copies, grads, lands, after)


def _scatter_wait(stage, names, started, after):
    send, recv, src, land, _ = started
    n = len(names)
    out = _split_wait(f"scatter_wait{stage}", n, lambda w, ref: ref, send, recv, src, land, after,
                      keep_sources=True)
    return dict(zip(names, out[:n])), dict(zip(names, out[n:]))


N_CHIPS = N_DEV // 2


def _pair_start(stage, names, grads, after):
    def copies(src, land, send, recv):
        me = _my_index()
        for k in range(N_DEV):
            @pl.when(me == k)
            def _():
                for w, name in enumerate(names):
                    kind, _, pad, _ = _SPEC[name]
                    for chip in range(N_CHIPS):
                        j = 2 * chip + ((k ^ 1) & 1)
                        pltpu.make_async_remote_copy(
                            src_ref=_window(src[w], kind, j * pad, pad), dst_ref=land[w].at[chip],
                            send_sem=send.at[w], recv_sem=recv.at[w],
                            device_id=_device_tuple(k ^ 1), device_id_type=MESH).start()

    lands = [lax.empty((N_CHIPS,) + _shard_shape(_SPEC[m][0], _SPEC[m][2], _SPEC[m][3]), BF16)
             for m in names]
    return _split_start(f"pair_start{stage}", len(names), copies, grads, lands, after)


def _pair_sum(dw_full, pair, *, pad, name):
    other = dw_full.shape[1]

    def body(own_ref, pair_ref, out_ref):
        out_ref[0] = _bf(own_ref[...].astype(F32) + pair_ref[0].astype(F32))

    slot = pl.BlockSpec((1, pad, other), lambda q: (q, 0, 0))
    return pl.pallas_call(
        body, name=name, grid=(N_CHIPS,),
        in_specs=[pl.BlockSpec((pad, other), lambda q: (2 * q + lax.axis_index("c"), 0)), slot],
        out_specs=slot, out_shape=_sds((N_CHIPS, pad, other), BF16),
        compiler_params=_params("arbitrary"),
    )(dw_full, pair)


def _chip_start(stage, names, sums, after):
    def copies(src, land, send, recv):
        me = _my_index()
        my_chip = lax.shift_right_logical(me, 1)
        for k in range(N_DEV):
            @pl.when((me != k) & (((me ^ k) & 1) == 0))
            def _():
                slot = lax.rem(my_chip + (N_CHIPS - 1 - k // 2), N_CHIPS)
                for w in range(len(names)):
                    pltpu.make_async_remote_copy(
                        src_ref=src[w].at[k // 2], dst_ref=land[w].at[slot],
                        send_sem=send.at[w], recv_sem=recv.at[w],
                        device_id=_device_tuple(k), device_id_type=MESH).start()

    lands = [lax.empty((N_CHIPS - 1,) + a.shape[1:], BF16) for a in sums]
    return _split_start(f"chip_start{stage}", len(names), copies, sums, lands, after)


def _adamw_chip(w, m, v, land, sums, *, name):
    shape = w.shape

    def body(w_ref, m_ref, v_ref, land_ref, own_ref, *outs):
        rows = pl.ds(0, shape[0])
        grad = own_ref[0, rows, :].astype(F32)
        for s in range(N_CHIPS - 1):
            grad = grad + land_ref[s, rows, :].astype(F32)
        _adam_update(w_ref, m_ref, v_ref, grad, *outs)

    whole = lambda a: pl.BlockSpec(a.shape, lambda i: (0,) * a.ndim)
    own = pl.BlockSpec((1,) + sums.shape[1:],
                       lambda i: (2 * lax.axis_index("x") + lax.axis_index("y"), 0, 0))
    return pl.pallas_call(
        body, name=name, grid=(1,),
        in_specs=[whole(w), whole(m), whole(v), whole(land), own],
        out_specs=[whole(w)] * 4, out_shape=[_sds(shape)] * 4,
        compiler_params=_params("arbitrary"),
    )(w, m, v, land, sums)


def _allreduce_small(small, after):
    shape = small.shape

    def body(in_ref, _after, out_ref, gath, send, recv):
        me = _my_index()
        for k in range(N_DEV):
            @pl.when(me != k)
            def _():
                pltpu.make_async_remote_copy(
                    src_ref=in_ref, dst_ref=gath.at[me], send_sem=send, recv_sem=recv,
                    device_id=_device_tuple(k), device_id_type=MESH).start()

            @pl.when(me == k)
            def _():
                gath[k] = in_ref[...]
        seven = gath.at[pl.ds(0, N_DEV - 1)]
        pltpu.make_async_remote_copy(
            src_ref=seven, dst_ref=seven, send_sem=send, recv_sem=recv,
            device_id=_device_tuple(0), device_id_type=MESH).wait()
        total = gath[0]
        for s in range(1, N_DEV):
            total = total + gath[s]
        out_ref[...] = total

    return pl.pallas_call(
        body, name="allreduce_small",
        in_specs=[VMEM, ANY], out_specs=VMEM, out_shape=_sds(shape),
        scratch_shapes=[pltpu.VMEM((N_DEV,) + shape, F32),
                        pltpu.SemaphoreType.DMA, pltpu.SemaphoreType.DMA],
    )(small, after)


def _adam_update(w_ref, m_ref, v_ref, grad, grad_ref, delta_ref, nm_ref, nv_ref):
    new_m = ADAM_B1 * m_ref[...] + (1.0 - ADAM_B1) * grad
    new_v = ADAM_B2 * v_ref[...] + (1.0 - ADAM_B2) * (grad * grad)
    m_hat = new_m / (1.0 - ADAM_B1 ** ADAM_STEP)
    v_hat = new_v / (1.0 - ADAM_B2 ** ADAM_STEP)
    grad_ref[...] = grad
    delta_ref[...] = -ADAM_LR * (m_hat / (jnp.sqrt(v_hat) + ADAM_EPS) + ADAM_WD * w_ref[...])
    nm_ref[...] = new_m
    nv_ref[...] = new_v


def _adamw(w, m, v, g, *, name):
    def body(w_ref, m_ref, v_ref, g_ref, *outs):
        _adam_update(w_ref, m_ref, v_ref, g_ref[...], *outs)

    whole = pl.BlockSpec(w.shape, lambda i: (0,) * w.ndim)
    return pl.pallas_call(
        body, name=name, grid=(1,), in_specs=[whole] * 4, out_specs=[whole] * 4,
        out_shape=[_sds(w.shape)] * 4, compiler_params=_params("arbitrary"),
    )(w, m, v, g)


def _adamw_gains(small, params):
    n = len(params)

    def body(small_ref, *refs):
        ins, outs = refs[:3 * n], refs[3 * n:]
        for r in range(n):
            width = ins[3 * r].shape[1]
            if width == D_MODEL:
                grad = small_ref[pl.ds(r, 1), :]
            else:
                grad = small_ref[pl.ds(len(_GAINS), 1), pl.ds((r - len(_GAINS)) * width, width)]
            _adam_update(*ins[3 * r:3 * r + 3], grad, *outs[4 * r:4 * r + 4])

    whole = lambda a: pl.BlockSpec(a.shape, lambda i: (0, 0))
    flat = [a for group in params for a in group]
    return pl.pallas_call(
        body, name="adamw_gains", grid=(1,),
        in_specs=[whole(small)] + [whole(a) for a in flat],
        out_specs=[whole(w) for w, _, _ in params for _ in range(4)],
        out_shape=[_sds(w.shape) for w, _, _ in params for _ in range(4)],
        compiler_params=_params("arbitrary"),
    )(small, *flat)


def _adamw_shard(w, m, v, land, dw_full, *, kind, pad, name):
    shape = w.shape
    other = shape[0] if kind == "col" else shape[1]

    def body(w_ref, m_ref, v_ref, land_ref, own_ref, *outs):
        valid = ((slice(None), pl.ds(0, shape[1])) if kind == "col"
                 else (pl.ds(0, shape[0]), slice(None)))
        grad = own_ref[valid].astype(F32)
        for s in range(N_DEV - 1):
            grad = grad + land_ref[(s,) + valid].astype(F32)
        _adam_update(w_ref, m_ref, v_ref, grad, *outs)

    whole = lambda a: pl.BlockSpec(a.shape, lambda i: (0,) * a.ndim)
    own = pl.BlockSpec(_shard_shape(kind, pad, other),
                       (lambda i: (0, _my_index())) if kind == "col" else (lambda i: (_my_index(), 0)))
    return pl.pallas_call(
        body, name=name, grid=(1,),
        in_specs=[whole(w), whole(m), whole(v), whole(land), own],
        out_specs=[whole(w)] * 4, out_shape=[_sds(shape)] * 4,
        compiler_params=_params("arbitrary"),
    )(w, m, v, land, dw_full)


_GAINS = ("ffn1_pre", "ffn1_post", "mix_pre", "mix_post", "ffn2_pre", "ffn2_post", "ple_post")
_SMALL_ROWS = 16


def _stack_gains(get):
    return jnp.concatenate([get(n) for n in _GAINS]
                           + [jnp.concatenate([get("out_sb"), get("out_ch")], axis=1)], axis=0)


def kernel(x, p, g_ffn1_pre, g_ffn1_post, w_ffn1_gate, w_ffn1_up, w_ffn1_down, g_mix_pre, g_mix_post, w_in, g_out_sb, g_out_ch, rel_bias, w_out, g_ffn2_pre, g_ffn2_post, w_ffn2_gate, w_ffn2_up, w_ffn2_down, w_ple_proj, w_ple_gate, g_ple_post, loss_target, m_g_ffn1_pre, m_g_ffn1_post, m_w_ffn1_gate, m_w_ffn1_up, m_w_ffn1_down, m_g_mix_pre, m_g_mix_post, m_w_in, m_g_out_sb, m_g_out_ch, m_rel_bias, m_w_out, m_g_ffn2_pre, m_g_ffn2_post, m_w_ffn2_gate, m_w_ffn2_up, m_w_ffn2_down, m_w_ple_proj, m_w_ple_gate, m_g_ple_post, v_g_ffn1_pre, v_g_ffn1_post, v_w_ffn1_gate, v_w_ffn1_up, v_w_ffn1_down, v_g_mix_pre, v_g_mix_post, v_w_in, v_g_out_sb, v_g_out_ch, v_rel_bias, v_w_out, v_g_ffn2_pre, v_g_ffn2_post, v_w_ffn2_gate, v_w_ffn2_up, v_w_ffn2_down, v_w_ple_proj, v_w_ple_gate, v_g_ple_post):
    given = dict(locals())
    wnames = [n for n, *_ in _WEIGHTS]

    def shard(prefix, n):
        a = given[prefix + "w_" + n][0]
        return a.T if n in _TRANSPOSED else a

    anchor = x[0]
    first = _GATHER_STAGES[0]
    packed, full = _pack_weights(first, [shard("", n) for n in first], anchor, name="pack_first")
    two_level = (0, 2)
    gathers = {}

    def start_stage(stage, after):
        peers = _NEAR_PEERS if stage in two_level else _ALL_PEERS
        gathers[stage] = _gather_start(stage, _GATHER_STAGES[stage], packed, full, after,
                                       peers=peers)

    start_stage(0, anchor)
    rest = [n for n in wnames if n not in first]
    packed_rest, full_rest = _pack_weights(rest, [shard("", n) for n in rest], gathers[0][-1],
                                           name="pack_rest")
    packed.update(packed_rest)
    full.update(full_rest)

    relays = {}

    def first_level(stage, after):
        names = _GATHER_STAGES[stage]
        last_stage = stage + 1 == len(_GATHER_STAGES)
        count = len(_NEAR_PEERS) if stage in two_level else N_DEV - 1
        if stage == 0:
            after = (after, packed_rest[rest[0]])
        ws = _gather_wait(stage, names, gathers[stage], after, count=count)
        if not last_stage:
            start_stage(stage + 1, ws[names[0]])
        if stage in two_level:
            relays[stage] = _relay_start(stage, names, ws,
                                         anchor if last_stage else gathers[stage + 1][-1])
            return ws, relays[stage][-1]
        return ws, None if last_stage else gathers[stage + 1][-1]

    def weights_early(stage, after):
        return first_level(stage, after)[1][:1, :1]

    def weights_for(stage, after):
        names = _GATHER_STAGES[stage]
        ws, token = (None, None) if stage in relays else first_level(stage, after)
        if stage in relays:
            relay = relays[stage]
            ws = _gather_wait(f"{stage}r", names, relay, after, count=len(_FAR_CHIPS))
            token = None if stage + 1 == len(_GATHER_STAGES) else gathers[stage + 1][-1]
        return ws, jnp.zeros((1, 1), F32) if token is None else token[:1, :1]

    scatters = {}

    last = len(_SCATTER_STAGES) - 1

    def grads_done(stage, grads):
        names = _SCATTER_STAGES[stage]
        start = _pair_start if stage == last else _scatter_start
        scatters[stage] = start(stage, names, [grads[n] for n in names], anchor)
        return scatters[stage][-1][:1, :1]

    gains = {n: given["g_" + n] for n in _GAINS + ("out_sb", "out_ch")}
    fvec = _rel_bias_to_fvec(rel_bias[0])
    loss, dx, dg, dbias = _local_step(x[0], p[0, 0], loss_target[0], gains,
                                      weights_for, grads_done, fvec, weights_early)

    results = {}

    def finish(stage, after):
        names = _SCATTER_STAGES[stage]
        dws, lands = _scatter_wait(stage, names, scatters[stage], after)
        for n in names:
            kind, _, pad, _ = _SPEC[n]
            out = _adamw_shard(shard("", n), shard("m_", n), shard("v_", n), lands[n], dws[n],
                               kind=kind, pad=pad, name="adamw_" + n)
            results["w_" + n] = [a.T for a in out] if n in _TRANSPOSED else out
        return results["w_" + names[-1]][0]

    names = _SCATTER_STAGES[last]
    whole = lambda w, ref: ref
    send, recv, src, land, _ = scatters[last]
    out = _split_wait(f"pair_wait{last}", len(names), whole, send, recv, src, land, dx,
                      keep_sources=True)
    sums = [_pair_sum(dwf, pair, pad=_SPEC[n][2], name="pair_sum_" + n)
            for n, dwf, pair in zip(names, out[:len(names)], out[len(names):])]
    send, recv, src, land, after = _chip_start(last, names, sums, anchor)
    for stage in range(last):
        after = finish(stage, after)
    dfvec = _bias_grad(dbias, after, name="bias_grad")
    loss_col = jnp.pad(loss[:, :1], ((0, N_DEV - 1), (0, D_MODEL - CH_WIN - 1)))
    dfv = jnp.concatenate([dfvec[:, 0, :], loss_col], axis=1)
    small = _allreduce_small(jnp.concatenate([_stack_gains(lambda n: dg[n]), dfv], axis=0), after)
    gain_names = _GAINS + ("out_sb", "out_ch")
    gain_out = _adamw_gains(small, [(given["g_" + n], given["m_g_" + n], given["v_g_" + n])
                                    for n in gain_names])
    for r, n in enumerate(gain_names):
        results["g_" + n] = gain_out[4 * r:4 * r + 4]
    d_rel = _fvec_grad_to_rel_bias(small[N_DEV:, :CH_WIN].reshape(N_DEV, 1, CH_WIN))
    results["rel_bias"] = _adamw(rel_bias[0], m_rel_bias[0], v_rel_bias[0], d_rel,
                                 name="adamw_rel_bias")
    out = _split_wait(f"chip_wait{last}", len(names), whole, send, recv, src, land,
                      results["rel_bias"][0], keep_sources=True)
    for n, own, landed in zip(names, out[:len(names)], out[len(names):]):
        res = _adamw_chip(shard("", n), shard("m_", n), shard("v_", n), landed, own,
                          name="adamw_" + n)
        results["w_" + n] = [a.T for a in res] if n in _TRANSPOSED else res

    order = ("g_ffn1_pre", "g_ffn1_post", "w_ffn1_gate", "w_ffn1_up", "w_ffn1_down",
             "g_mix_pre", "g_mix_post", "w_in", "g_out_sb", "g_out_ch", "rel_bias", "w_out",
             "g_ffn2_pre", "g_ffn2_post", "w_ffn2_gate", "w_ffn2_up", "w_ffn2_down",
             "w_ple_proj", "w_ple_gate", "g_ple_post")

    def leaf(name, idx):
        a = results[name][idx]
        return a if name.startswith("g_") else a[None]

    total_loss = small[N_DEV, CH_WIN]
    return (total_loss, dx[None],
            *[leaf(n, 0) for n in order], *[leaf(n, 1) for n in order],
            *[leaf(n, 2) for n in order], *[leaf(n, 3) for n in order])
```
